```python
import math
import jax, jax.numpy as jnp
from jax import lax
import numpy as np

D_MODEL = 1024
BATCH = 16
SEQ = 2048
DEPTH = 1

SB_HEADS = 16
SB_HEAD_DIM = 64
SB_WIDTH = SB_HEADS * SB_HEAD_DIM
Q_BLOCK = 128
SSD_EXPAND = 2
SSD_WIDTH = SSD_EXPAND * D_MODEL
SSD_HEAD_DIM = 64
SSD_HEADS = SSD_WIDTH // SSD_HEAD_DIM
SSD_GROUPS = 4
SSD_HEADS_PER_GROUP = SSD_HEADS // SSD_GROUPS
SSD_STATE = 128
SSD_CONV = 4
SSD_CHUNK = 128
SSD_BC_WIDTH = SSD_GROUPS * SSD_STATE
SSD_CONV_DIM = SSD_WIDTH + 2 * SSD_BC_WIDTH
N_BRANCHES = 2
PROJ_SPLITS = (SB_WIDTH, SB_WIDTH, SB_WIDTH, SB_WIDTH, SSD_WIDTH, SSD_CONV_DIM, SSD_HEADS, N_BRANCHES * D_MODEL)
D_PROJ = 4 * SB_WIDTH + SSD_WIDTH + SSD_CONV_DIM + SSD_HEADS + N_BRANCHES * D_MODEL
EPS = 1e-6
DT_MIN = 0.001
DT_MAX = 0.1
A_INIT_MIN = 1.0
A_INIT_MAX = 16.0

kernel_name = "hybrid_stickbreaking_ssd_gated_block"


def rms_norm(x, w):
    xf = x.astype(jnp.float32)
    y = xf * lax.rsqrt(jnp.mean(xf * xf, axis=-1, keepdims=True) + EPS)
    return (y * w.astype(jnp.float32)).astype(x.dtype)


def stick_breaking_attention(q, k, v):
    s_len = q.shape[2]
    scale = q.shape[-1] ** -0.5
    outs = []
    for blk in range(s_len // Q_BLOCK):
        start = blk * Q_BLOCK
        end = start + Q_BLOCK
        qb = q[:, :, start:end]
        kb = k[:, :, :end]
        vb = v[:, :, :end]
        z = jnp.einsum('bhqd,bhkd->bhqk', qb, kb) * scale
        t_idx = start + jnp.arange(Q_BLOCK)[:, None]
        s_idx = jnp.arange(end)[None, :]
        mask = s_idx < t_idx
        log_beta = jax.nn.log_sigmoid(z)
        log_one_minus = jnp.where(mask, jax.nn.log_sigmoid(-z), 0.0)
        later = lax.cumsum(log_one_minus, axis=3, reverse=True) - log_one_minus
        a = jnp.where(mask, jnp.exp(log_beta + later), 0.0)
        outs.append(jnp.einsum('bhqk,bhkd->bhqd', a, vb))
    return jnp.concatenate(outs, axis=2)


def causal_depthwise_conv(x, w, b):
    c = x.shape[-1]
    y = lax.conv_general_dilated(
        x, w[:, None, :].astype(x.dtype), window_strides=(1,),
        padding=[(SSD_CONV - 1, 0)],
        dimension_numbers=('NWC', 'WIO', 'NWC'),
        feature_group_count=c)
    return y + b.astype(x.dtype)


def ssd_chunked(x, dt, a, bm, cm):
    b, s, g, hg, p = x.shape
    n = bm.shape[-1]
    nc = s // SSD_CHUNK
    x = x.reshape(b, nc, SSD_CHUNK, g, hg, p)
    dt = dt.reshape(b, nc, SSD_CHUNK, g, hg)
    bm = bm.reshape(b, nc, SSD_CHUNK, g, n)
    cm = cm.reshape(b, nc, SSD_CHUNK, g, n)
    a_cs = jnp.cumsum(dt * a, axis=2)
    xdt = x * dt[..., None]
    l_idx = jnp.arange(SSD_CHUNK)
    causal = (l_idx[:, None] >= l_idx[None, :])[:, :, None, None]
    seg = a_cs[:, :, :, None] - a_cs[:, :, None, :]
    decay = jnp.exp(jnp.where(causal, seg, -jnp.inf))
    cb = jnp.einsum('bclgn,bcsgn->bclsg', cm, bm)
    y_diag = jnp.einsum('bclsgh,bcsghp->bclghp', cb[..., None] * decay, xdt)
    decay_to_end = jnp.exp(a_cs[:, :, -1:] - a_cs)
    states = jnp.einsum('bclgn,bclghp->bcghpn', bm, xdt * decay_to_end[..., None])
    chunk_decay = jnp.exp(a_cs[:, :, -1])

    def step(h_prev, inp):
        st, dec = inp
        return h_prev * dec[..., None, None] + st, h_prev

    h0 = jnp.zeros((b, g, hg, p, n), jnp.float32)
    _, h_in = lax.scan(step, h0, (jnp.moveaxis(states, 1, 0), jnp.moveaxis(chunk_decay, 1, 0)))
    h_in = jnp.moveaxis(h_in, 0, 1)
    y_off = jnp.einsum('bclgn,bcghpn->bclghp', cm, h_in) * jnp.exp(a_cs)[..., None]
    return (y_diag + y_off).reshape(b, s, g, hg, p)


def hybrid_layer(x, norm_w, w_in, conv_w, conv_b, dt_bias, a_log, d_skip,
                 ssm_norm_w, w_attn_out, w_ssm_out, w_o):
    b, s, _ = x.shape
    f32 = jnp.float32
    h = rms_norm(x, norm_w)
    proj = jnp.einsum('bsd,de->bse', h, w_in)
    split_points = [int(v) for v in np.cumsum(PROJ_SPLITS)[:-1]]
    q, k, v, z_a, z_s, xbc, dt_raw, gate_raw = jnp.split(proj, split_points, axis=-1)

    def to_heads(t):
        return t.reshape(b, s, SB_HEADS, SB_HEAD_DIM).transpose(0, 2, 1, 3).astype(f32)

    o = stick_breaking_attention(to_heads(q), to_heads(k), to_heads(v))
    o = o.transpose(0, 2, 1, 3).reshape(b, s, SB_WIDTH)
    y_a = (o * jax.nn.silu(z_a.astype(f32))).astype(x.dtype)
    y_a = jnp.einsum('bse,ed->bsd', y_a, w_attn_out)

    xbc = jax.nn.silu(causal_depthwise_conv(xbc, conv_w, conv_b))
    xs, bm, cm = jnp.split(xbc, [SSD_WIDTH, SSD_WIDTH + SSD_BC_WIDTH], axis=-1)
    xs = xs.reshape(b, s, SSD_GROUPS, SSD_HEADS_PER_GROUP, SSD_HEAD_DIM).astype(f32)
    bm = bm.reshape(b, s, SSD_GROUPS, SSD_STATE).astype(f32)
    cm = cm.reshape(b, s, SSD_GROUPS, SSD_STATE).astype(f32)
    dt = jax.nn.softplus(dt_raw.astype(f32) + dt_bias.astype(f32))
    dt = dt.reshape(b, s, SSD_GROUPS, SSD_HEADS_PER_GROUP)
    a = -jnp.exp(a_log.astype(f32)).reshape(SSD_GROUPS, SSD_HEADS_PER_GROUP)
    y = ssd_chunked(xs, dt, a, bm, cm)
    y = y + xs * d_skip.astype(f32).reshape(SSD_GROUPS, SSD_HEADS_PER_GROUP)[..., None]
    y = y.reshape(b, s, SSD_WIDTH) * jax.nn.silu(z_s.astype(f32))
    yg = y.reshape(b, s, SSD_GROUPS, SSD_WIDTH // SSD_GROUPS)
    yg = yg * lax.rsqrt(jnp.mean(yg * yg, axis=-1, keepdims=True) + EPS)
    y = yg.reshape(b, s, SSD_WIDTH) * ssm_norm_w.astype(f32)
    y_s = jnp.einsum('bse,ed->bsd', y.astype(x.dtype), w_ssm_out)

    g_a, g_s = jnp.split(jax.nn.sigmoid(gate_raw.astype(f32)), N_BRANCHES, axis=-1)
    merged = (g_a * y_a.astype(f32) + g_s * y_s.astype(f32)).astype(x.dtype)
    return x + jnp.einsum('bsd,de->bse', merged, w_o)


def _fwd_setup_inputs(seed: int = 0) -> dict:
    key = jax.random.key(seed)
    ks = jax.random.split(key, 14)
    f32 = jnp.float32
    x = jax.random.normal(ks[0], (BATCH, SEQ, D_MODEL), f32)
    norm_w = 1.0 + 0.02 * jax.random.normal(ks[1], (DEPTH, D_MODEL), f32)
    w_in = jax.random.normal(ks[2], (DEPTH, D_MODEL, D_PROJ), f32) * D_MODEL ** -0.5
    conv_w = jax.random.normal(ks[3], (DEPTH, SSD_CONV, SSD_CONV_DIM), f32) * SSD_CONV ** -0.5
    conv_b = 0.02 * jax.random.normal(ks[4], (DEPTH, SSD_CONV_DIM), f32)
    u = jax.random.uniform(ks[5], (DEPTH, SSD_HEADS), f32)
    dt0 = jnp.exp(u * (math.log(DT_MAX) - math.log(DT_MIN)) + math.log(DT_MIN))
    dt_bias = dt0 + jnp.log(-jnp.expm1(-dt0))
    a_log = jnp.log(jax.random.uniform(ks[6], (DEPTH, SSD_HEADS), f32, A_INIT_MIN, A_INIT_MAX))
    d_skip = 1.0 + 0.02 * jax.random.normal(ks[7], (DEPTH, SSD_HEADS), f32)
    ssm_norm_w = 1.0 + 0.02 * jax.random.normal(ks[8], (DEPTH, SSD_WIDTH), f32)
    w_attn_out = jax.random.normal(ks[9], (DEPTH, SB_WIDTH, D_MODEL), f32) * SB_WIDTH ** -0.5
    w_ssm_out = jax.random.normal(ks[10], (DEPTH, SSD_WIDTH, D_MODEL), f32) * SSD_WIDTH ** -0.5
    w_o = jax.random.normal(ks[11], (DEPTH, D_MODEL, D_MODEL), f32) * D_MODEL ** -0.5
    final_norm_w = 1.0 + 0.02 * jax.random.normal(ks[12], (D_MODEL,), f32)
    return {"x": x, "norm_w": norm_w, "w_in": w_in, "conv_w": conv_w, "conv_b": conv_b,
            "dt_bias": dt_bias, "a_log": a_log, "d_skip": d_skip, "ssm_norm_w": ssm_norm_w,
            "w_attn_out": w_attn_out, "w_ssm_out": w_ssm_out, "w_o": w_o,
            "final_norm_w": final_norm_w}


def _fwd_reference(x, norm_w, w_in, conv_w, conv_b, dt_bias, a_log, d_skip, ssm_norm_w,
              w_attn_out, w_ssm_out, w_o, final_norm_w):
    h = x
    for layer in range(DEPTH):
        h = hybrid_layer(h, norm_w[layer], w_in[layer], conv_w[layer], conv_b[layer],
                         dt_bias[layer], a_log[layer], d_skip[layer], ssm_norm_w[layer],
                         w_attn_out[layer], w_ssm_out[layer], w_o[layer])
    return rms_norm(h, final_norm_w)


import jax as _jax
import jax.numpy as _jnp

TWIN_FORMAT = 'train_step'
FWD_PARAMS = ['x', 'norm_w', 'w_in', 'conv_w', 'conv_b', 'dt_bias', 'a_log', 'd_skip', 'ssm_norm_w', 'w_attn_out', 'w_ssm_out', 'w_o', 'final_norm_w']
TWIN_WEIGHTS = ['norm_w', 'w_in', 'conv_w', 'conv_b', 'dt_bias', 'a_log', 'd_skip', 'ssm_norm_w', 'w_attn_out', 'w_ssm_out', 'w_o', 'final_norm_w']
TWIN_DIFF_INPUT = 'x'
TWIN_INPUTS = ['x', 'norm_w', 'w_in', 'conv_w', 'conv_b', 'dt_bias', 'a_log', 'd_skip', 'ssm_norm_w', 'w_attn_out', 'w_ssm_out', 'w_o', 'final_norm_w', 'loss_target', 'm_norm_w', 'm_w_in', 'm_conv_w', 'm_conv_b', 'm_dt_bias', 'm_a_log', 'm_d_skip', 'm_ssm_norm_w', 'm_w_attn_out', 'm_w_ssm_out', 'm_w_o', 'm_final_norm_w', 'v_norm_w', 'v_w_in', 'v_conv_w', 'v_conv_b', 'v_dt_bias', 'v_a_log', 'v_d_skip', 'v_ssm_norm_w', 'v_w_attn_out', 'v_w_ssm_out', 'v_w_o', 'v_final_norm_w']
TWIN_OUTPUTS = ['loss', 'grad_x', 'grad_norm_w', 'grad_w_in', 'grad_conv_w', 'grad_conv_b', 'grad_dt_bias', 'grad_a_log', 'grad_d_skip', 'grad_ssm_norm_w', 'grad_w_attn_out', 'grad_w_ssm_out', 'grad_w_o', 'grad_final_norm_w', 'delta_norm_w', 'delta_w_in', 'delta_conv_w', 'delta_conv_b', 'delta_dt_bias', 'delta_a_log', 'delta_d_skip', 'delta_ssm_norm_w', 'delta_w_attn_out', 'delta_w_ssm_out', 'delta_w_o', 'delta_final_norm_w', 'new_m_norm_w', 'new_m_w_in', 'new_m_conv_w', 'new_m_conv_b', 'new_m_dt_bias', 'new_m_a_log', 'new_m_d_skip', 'new_m_ssm_norm_w', 'new_m_w_attn_out', 'new_m_w_ssm_out', 'new_m_w_o', 'new_m_final_norm_w', 'new_v_norm_w', 'new_v_w_in', 'new_v_conv_w', 'new_v_conv_b', 'new_v_dt_bias', 'new_v_a_log', 'new_v_d_skip', 'new_v_ssm_norm_w', 'new_v_w_attn_out', 'new_v_w_ssm_out', 'new_v_w_o', 'new_v_final_norm_w']
TWIN_LEAF_KINDS = {'loss': 'loss', 'grad_x': 'grad_x', 'grad_norm_w': 'grad_w', 'grad_w_in': 'grad_w', 'grad_conv_w': 'grad_w', 'grad_conv_b': 'grad_w', 'grad_dt_bias': 'grad_w', 'grad_a_log': 'grad_w', 'grad_d_skip': 'grad_w', 'grad_ssm_norm_w': 'grad_w', 'grad_w_attn_out': 'grad_w', 'grad_w_ssm_out': 'grad_w', 'grad_w_o': 'grad_w', 'grad_final_norm_w': 'grad_w', 'delta_norm_w': 'delta_w', 'delta_w_in': 'delta_w', 'delta_conv_w': 'delta_w', 'delta_conv_b': 'delta_w', 'delta_dt_bias': 'delta_w', 'delta_a_log': 'delta_w', 'delta_d_skip': 'delta_w', 'delta_ssm_norm_w': 'delta_w', 'delta_w_attn_out': 'delta_w', 'delta_w_ssm_out': 'delta_w', 'delta_w_o': 'delta_w', 'delta_final_norm_w': 'delta_w', 'new_m_norm_w': 'new_m', 'new_m_w_in': 'new_m', 'new_m_conv_w': 'new_m', 'new_m_conv_b': 'new_m', 'new_m_dt_bias': 'new_m', 'new_m_a_log': 'new_m', 'new_m_d_skip': 'new_m', 'new_m_ssm_norm_w': 'new_m', 'new_m_w_attn_out': 'new_m', 'new_m_w_ssm_out': 'new_m', 'new_m_w_o': 'new_m', 'new_m_final_norm_w': 'new_m', 'new_v_norm_w': 'new_v', 'new_v_w_in': 'new_v', 'new_v_conv_w': 'new_v', 'new_v_conv_b': 'new_v', 'new_v_dt_bias': 'new_v', 'new_v_a_log': 'new_v', 'new_v_d_skip': 'new_v', 'new_v_ssm_norm_w': 'new_v', 'new_v_w_attn_out': 'new_v', 'new_v_w_ssm_out': 'new_v', 'new_v_w_o': 'new_v', 'new_v_final_norm_w': 'new_v'}


def _forward(args):
    return _fwd_reference(*[args[k] for k in FWD_PARAMS])


def _output_shape():
    out = _jax.eval_shape(lambda: _forward(_fwd_setup_inputs(0)))
    return out.shape, out.dtype

N_MICROBATCH = 1
ADAM_LR = 0.001
ADAM_B1 = 0.9
ADAM_B2 = 0.999
ADAM_EPS = 1e-08
ADAM_WD = 0.01
ADAM_STEP = 10
PER_EXAMPLE_BATCH_AXIS = {'x': 0, 'loss_target': 0}
SHARED_INPUTS = []
_WEIGHT_DTYPES = {'norm_w': _jnp.float32, 'w_in': _jnp.float32, 'conv_w': _jnp.float32, 'conv_b': _jnp.float32, 'dt_bias': _jnp.float32, 'a_log': _jnp.float32, 'd_skip': _jnp.float32, 'ssm_norm_w': _jnp.float32, 'w_attn_out': _jnp.float32, 'w_ssm_out': _jnp.float32, 'w_o': _jnp.float32, 'final_norm_w': _jnp.float32}
MOMENT_SCALE = {'norm_w': 1.597317e-01, 'w_in': 4.202042e-02, 'conv_w': 5.112296e-02, 'conv_b': 7.873374e-02, 'dt_bias': 2.041793e-01, 'a_log': 1.533078e-01, 'd_skip': 3.450966e-01, 'ssm_norm_w': 6.367041e-02, 'w_attn_out': 3.179888e-02, 'w_ssm_out': 8.326739e-02, 'w_o': 8.904598e-02, 'final_norm_w': 3.202136e+01}


def _to_microbatches(a, axis):
    t = _jnp.moveaxis(a, axis, 0)
    t = t.reshape((N_MICROBATCH, t.shape[0] // N_MICROBATCH) + t.shape[1:])
    return _jnp.moveaxis(t, 1, axis + 1)


def setup_inputs(seed: int = 0) -> dict:
    inp = _fwd_setup_inputs(seed)
    key = _jax.random.fold_in(_jax.random.key(seed), 7919)
    shape, _ = _output_shape()
    out = dict(inp)
    out["loss_target"] = _jax.random.normal(_jax.random.fold_in(key, 0), shape, _jnp.float32)
    for i, name in enumerate(TWIN_WEIGHTS):
        w = inp[name].astype(_jnp.float32)
        if MOMENT_SCALE is None:
            s = _jnp.sqrt(_jnp.mean(_jnp.square(w)) + 1e-30)
        else:
            s = MOMENT_SCALE[name]
        km, kv = _jax.random.split(_jax.random.fold_in(key, i + 1))
        out[name] = w
        out["m_" + name] = s * _jax.random.normal(km, w.shape, _jnp.float32)
        out["v_" + name] = (s * s) * _jax.random.uniform(kv, w.shape, _jnp.float32, 0.5, 1.5)
    if N_MICROBATCH > 1:
        for name, axis in PER_EXAMPLE_BATCH_AXIS.items():
            out[name] = _to_microbatches(out[name], axis)
    return {'x': out['x'], 'norm_w': out['norm_w'], 'w_in': out['w_in'], 'conv_w': out['conv_w'], 'conv_b': out['conv_b'], 'dt_bias': out['dt_bias'], 'a_log': out['a_log'], 'd_skip': out['d_skip'], 'ssm_norm_w': out['ssm_norm_w'], 'w_attn_out': out['w_attn_out'], 'w_ssm_out': out['w_ssm_out'], 'w_o': out['w_o'], 'final_norm_w': out['final_norm_w'], 'loss_target': out['loss_target'], 'm_norm_w': out['m_norm_w'], 'm_w_in': out['m_w_in'], 'm_conv_w': out['m_conv_w'], 'm_conv_b': out['m_conv_b'], 'm_dt_bias': out['m_dt_bias'], 'm_a_log': out['m_a_log'], 'm_d_skip': out['m_d_skip'], 'm_ssm_norm_w': out['m_ssm_norm_w'], 'm_w_attn_out': out['m_w_attn_out'], 'm_w_ssm_out': out['m_w_ssm_out'], 'm_w_o': out['m_w_o'], 'm_final_norm_w': out['m_final_norm_w'], 'v_norm_w': out['v_norm_w'], 'v_w_in': out['v_w_in'], 'v_conv_w': out['v_conv_w'], 'v_conv_b': out['v_conv_b'], 'v_dt_bias': out['v_dt_bias'], 'v_a_log': out['v_a_log'], 'v_d_skip': out['v_d_skip'], 'v_ssm_norm_w': out['v_ssm_norm_w'], 'v_w_attn_out': out['v_w_attn_out'], 'v_w_ssm_out': out['v_w_ssm_out'], 'v_w_o': out['v_w_o'], 'v_final_norm_w': out['v_final_norm_w']}


def _loss(weights, diff, rest, loss_target):
    with _jax.named_scope("forward"):
        args = {**rest, TWIN_DIFF_INPUT: diff, **{k: w.astype(_WEIGHT_DTYPES[k]) for k, w in weights.items()}}
        y = _forward(args)
    with _jax.named_scope("loss_head"):
        err = _jnp.square(y.astype(_jnp.float32) - loss_target)
        return 0.5 * _jnp.sum(_jnp.mean(err, axis=-1)) if err.ndim else 0.5 * err


def _adamw(w, g, m, v):
    m = ADAM_B1 * m + (1.0 - ADAM_B1) * g
    v = ADAM_B2 * v + (1.0 - ADAM_B2) * _jnp.square(g)
    m_hat = m / (1.0 - ADAM_B1 ** ADAM_STEP)
    v_hat = v / (1.0 - ADAM_B2 ** ADAM_STEP)
    delta = -ADAM_LR * (m_hat / (_jnp.sqrt(v_hat) + ADAM_EPS) + ADAM_WD * w)
    return delta, m, v


def reference(x, norm_w, w_in, conv_w, conv_b, dt_bias, a_log, d_skip, ssm_norm_w, w_attn_out, w_ssm_out, w_o, final_norm_w, loss_target, m_norm_w, m_w_in, m_conv_w, m_conv_b, m_dt_bias, m_a_log, m_d_skip, m_ssm_norm_w, m_w_attn_out, m_w_ssm_out, m_w_o, m_final_norm_w, v_norm_w, v_w_in, v_conv_w, v_conv_b, v_dt_bias, v_a_log, v_d_skip, v_ssm_norm_w, v_w_attn_out, v_w_ssm_out, v_w_o, v_final_norm_w):
    given = dict(x=x, norm_w=norm_w, w_in=w_in, conv_w=conv_w, conv_b=conv_b, dt_bias=dt_bias, a_log=a_log, d_skip=d_skip, ssm_norm_w=ssm_norm_w, w_attn_out=w_attn_out, w_ssm_out=w_ssm_out, w_o=w_o, final_norm_w=final_norm_w, loss_target=loss_target, m_norm_w=m_norm_w, m_w_in=m_w_in, m_conv_w=m_conv_w, m_conv_b=m_conv_b, m_dt_bias=m_dt_bias, m_a_log=m_a_log, m_d_skip=m_d_skip, m_ssm_norm_w=m_ssm_norm_w, m_w_attn_out=m_w_attn_out, m_w_ssm_out=m_w_ssm_out, m_w_o=m_w_o, m_final_norm_w=m_final_norm_w, v_norm_w=v_norm_w, v_w_in=v_w_in, v_conv_w=v_conv_w, v_conv_b=v_conv_b, v_dt_bias=v_dt_bias, v_a_log=v_a_log, v_d_skip=v_d_skip, v_ssm_norm_w=v_ssm_norm_w, v_w_attn_out=v_w_attn_out, v_w_ssm_out=v_w_ssm_out, v_w_o=v_w_o, v_final_norm_w=v_final_norm_w)
    weights = {n: given[n] for n in TWIN_WEIGHTS}
    shared = {n: given[n] for n in SHARED_INPUTS}
    per_example = {n: given[n] for n in ['x']}
    grad_fn = _jax.value_and_grad(_loss, argnums=(0, 1))

    def one_microbatch(ex, loss_target):
        ex = dict(ex)
        diff = ex.pop(TWIN_DIFF_INPUT)
        return grad_fn(weights, diff, {**shared, **ex}, loss_target)

    if N_MICROBATCH == 1:
        loss, (grad_w, grad_x) = one_microbatch(per_example, given["loss_target"])
    else:
        def body(carry, xs):
            loss_sum, grad_sum = carry
            l_k, (gw_k, gx_k) = one_microbatch(xs[0], xs[1])
            with _jax.named_scope("update"):
                return (loss_sum + l_k, _jax.tree.map(_jnp.add, grad_sum, gw_k)), gx_k

        init = (_jnp.zeros((), _jnp.float32), _jax.tree.map(_jnp.zeros_like, weights))
        (loss, grad_w), grad_x = _jax.lax.scan(body, init, (per_example, given["loss_target"]))
    with _jax.named_scope("update"):
        delta_w, new_m, new_v = {}, {}, {}
        for n in TWIN_WEIGHTS:
            delta_w[n], new_m[n], new_v[n] = _adamw(weights[n], grad_w[n], given["m_" + n], given["v_" + n])
    return (loss, grad_x, *[grad_w[n] for n in TWIN_WEIGHTS], *[delta_w[n] for n in TWIN_WEIGHTS],
            *[new_m[n] for n in TWIN_WEIGHTS], *[new_v[n] for n in TWIN_WEIGHTS])
```

```python
import jax
import jax.numpy as jnp
from jax import lax
from jax.experimental import pallas as pl
from jax.experimental.pallas import tpu as pltpu

F32 = jnp.float32
BF16 = jnp.bfloat16
HIGHEST = lax.Precision.HIGHEST
MESH = pl.DeviceIdType.MESH

D_MODEL = 1024
SB_HEADS = 16
HEAD_DIM = 64
SSD_WIDTH = 2048
SSD_GROUPS = 4
GROUP_WIDTH = SSD_WIDTH // SSD_GROUPS
HEADS_PER_GROUP = 8
SSD_STATE = 128
CHUNK = 128
CONV_K = 4
CONV_DIM = 3072
D_PROJ = 11296
EPS = 1e-6
ADAM_LR, ADAM_B1, ADAM_B2, ADAM_EPS, ADAM_WD, ADAM_STEP = 0.001, 0.9, 0.999, 1e-08, 0.01, 10

LANES = 128
Q0, K0, V0, ZA0, ZS0, XBC0, GATE0, DT0 = 0, 1024, 2048, 3072, 4096, 6144, 9216, 11264
DT_PAD = 256
NP = DT0 + DT_PAD
N_CHIPS = 4
VMEM_LIMIT = 56 * 1024 * 1024


def _cparams(*sem):
    return pltpu.CompilerParams(dimension_semantics=sem or None, vmem_limit_bytes=VMEM_LIMIT)


def _sigmoid(z):
    return 1.0 / (1.0 + jnp.exp(-z))


def _dot(a, b, dims, precision=None):
    return lax.dot_general(a, b, (dims, ((), ())), preferred_element_type=F32, precision=precision)


NN = ((1,), (0,))
NT = ((1,), (1,))
TN = ((0,), (0,))


def _matmul(a, b, *, ta=False, tb=False, out_dtype=F32, tm, tn, tk, name):
    m, k = (a.shape[1], a.shape[0]) if ta else a.shape
    n = b.shape[0] if tb else b.shape[1]
    assert m % tm == 0 and n % tn == 0 and k % tk == 0, (name, m, n, k)
    nk = k // tk
    use_scratch = out_dtype != F32
    dims = ((0,) if ta else (1,), (1,) if tb else (0,))

    def kern(a_ref, b_ref, o_ref, *scratch):
        acc = scratch[0] if use_scratch else o_ref
        kk = pl.program_id(2)

        @pl.when(kk == 0)
        def _():
            acc[...] = jnp.zeros_like(acc)

        acc[...] += _dot(a_ref[...], b_ref[...], dims)
        if use_scratch:
            @pl.when(kk == nk - 1)
            def _():
                o_ref[...] = acc[...].astype(out_dtype)

    a_spec = pl.BlockSpec((tk, tm), lambda i, j, q: (q, i)) if ta else pl.BlockSpec((tm, tk), lambda i, j, q: (i, q))
    b_spec = pl.BlockSpec((tn, tk), lambda i, j, q: (j, q)) if tb else pl.BlockSpec((tk, tn), lambda i, j, q: (q, j))
    return pl.pallas_call(
        kern, name=name,
        out_shape=jax.ShapeDtypeStruct((m, n), out_dtype),
        grid=(m // tm, n // tn, nk),
        in_specs=[a_spec, b_spec],
        out_specs=pl.BlockSpec((tm, tn), lambda i, j, q: (i, j)),
        scratch_shapes=[pltpu.VMEM((tm, tn), F32)] if use_scratch else [],
        compiler_params=_cparams("parallel", "parallel", "arbitrary"),
    )(a, b)


ROWS = 256


def _rms_fwd(x2, w):
    t, d = x2.shape

    def kern(x_ref, w_ref, h_ref):
        x = x_ref[...]
        r = lax.rsqrt(jnp.mean(x * x, axis=-1, keepdims=True) + EPS)
        h_ref[...] = (x * r * w_ref[...]).astype(BF16)

    return pl.pallas_call(
        kern, name="rms_fwd",
        out_shape=jax.ShapeDtypeStruct((t, d), BF16),
        grid=(t // ROWS,),
        in_specs=[pl.BlockSpec((ROWS, d), lambda i: (i, 0)), pl.BlockSpec((1, d), lambda i: (0, 0))],
        out_specs=pl.BlockSpec((ROWS, d), lambda i: (i, 0)),
        compiler_params=_cparams("parallel"),
    )(x2, w)


def _rms_bwd(dh, x2, w, dout):
    t, d = x2.shape

    def kern(dh_ref, x_ref, w_ref, dout_ref, gx_ref, dw_ref):
        @pl.when(pl.program_id(0) == 0)
        def _():
            dw_ref[...] = jnp.zeros_like(dw_ref)

        x = x_ref[...]
        r = lax.rsqrt(jnp.mean(x * x, axis=-1, keepdims=True) + EPS)
        xh = x * r
        g = dh_ref[...]
        dw_ref[...] += jnp.sum(g * xh, axis=0, keepdims=True)
        gw = g * w_ref[...]
        gx_ref[...] = dout_ref[...] + r * (gw - xh * jnp.mean(gw * xh, axis=-1, keepdims=True))

    row = pl.BlockSpec((ROWS, d), lambda i: (i, 0))
    vec = pl.BlockSpec((1, d), lambda i: (0, 0))
    return pl.pallas_call(
        kern, name="rms_bwd",
        out_shape=(jax.ShapeDtypeStruct((t, d), F32), jax.ShapeDtypeStruct((1, d), F32)),
        grid=(t // ROWS,),
        in_specs=[row, row, vec, row],
        out_specs=(row, vec),
        compiler_params=_cparams("arbitrary"),
    )(dh, x2, w, dout)


def _final_fwd_bwd(x2, mo, target, w):
    t, d = x2.shape

    def kern(x_ref, mo_ref, t_ref, w_ref, dout_ref, doutb_ref, loss_ref, dw_ref):
        @pl.when(pl.program_id(0) == 0)
        def _():
            loss_ref[...] = jnp.zeros_like(loss_ref)
            dw_ref[...] = jnp.zeros_like(dw_ref)

        u = x_ref[...] + mo_ref[...]
        r = lax.rsqrt(jnp.mean(u * u, axis=-1, keepdims=True) + EPS)
        uh = u * r
        wv = w_ref[...]
        err = uh * wv - t_ref[...]
        loss_ref[...] += (0.5 / d) * jnp.sum(err * err)
        dy = err * (1.0 / d)
        dw_ref[...] += jnp.sum(dy * uh, axis=0, keepdims=True)
        gw = dy * wv
        du = r * (gw - uh * jnp.mean(gw * uh, axis=-1, keepdims=True))
        dout_ref[...] = du
        doutb_ref[...] = du.astype(BF16)

    row = pl.BlockSpec((ROWS, d), lambda i: (i, 0))
    vec = pl.BlockSpec((1, d), lambda i: (0, 0))
    return pl.pallas_call(
        kern, name="final_fwd_bwd",
        out_shape=(jax.ShapeDtypeStruct((t, d), F32), jax.ShapeDtypeStruct((t, d), BF16),
                   jax.ShapeDtypeStruct((1, LANES), F32), jax.ShapeDtypeStruct((1, d), F32)),
        grid=(t // ROWS,),
        in_specs=[row, row, row, vec],
        out_specs=(row, row, pl.BlockSpec((1, LANES), lambda i: (0, 0)), vec),
        compiler_params=_cparams("arbitrary"),
    )(x2, mo, target, w)


def _merge_fwd(proj2, ya, ys):
    t = ya.shape[0]
    gblk = GATE0 // D_MODEL

    def kern(ga_ref, gs_ref, ya_ref, ys_ref, o_ref):
        o_ref[...] = (_sigmoid(ga_ref[...]) * ya_ref[...] + _sigmoid(gs_ref[...]) * ys_ref[...]).astype(BF16)

    row = pl.BlockSpec((ROWS, D_MODEL), lambda i: (i, 0))
    return pl.pallas_call(
        kern, name="merge_fwd",
        out_shape=jax.ShapeDtypeStruct((t, D_MODEL), BF16),
        grid=(t // ROWS,),
        in_specs=[pl.BlockSpec((ROWS, D_MODEL), lambda i: (i, gblk)),
                  pl.BlockSpec((ROWS, D_MODEL), lambda i: (i, gblk + 1)), row, row],
        out_specs=row,
        compiler_params=_cparams("parallel"),
    )(proj2, proj2, ya, ys)


def _merge_bwd(dm, proj2, ya, ys):
    t = ya.shape[0]
    gblk = GATE0 // D_MODEL

    def kern(dm_ref, ga_ref, gs_ref, ya_ref, ys_ref, dya_ref, dys_ref, dg_ref):
        g = dm_ref[...]
        sa = _sigmoid(ga_ref[...])
        ss = _sigmoid(gs_ref[...])
        dya_ref[...] = (g * sa).astype(BF16)
        dys_ref[...] = (g * ss).astype(BF16)
        dg_ref[:, :D_MODEL] = (g * ya_ref[...] * sa * (1.0 - sa)).astype(BF16)
        dg_ref[:, D_MODEL:] = (g * ys_ref[...] * ss * (1.0 - ss)).astype(BF16)

    row = pl.BlockSpec((ROWS, D_MODEL), lambda i: (i, 0))
    return pl.pallas_call(
        kern, name="merge_bwd",
        out_shape=(jax.ShapeDtypeStruct((t, D_MODEL), BF16), jax.ShapeDtypeStruct((t, D_MODEL), BF16),
                   jax.ShapeDtypeStruct((t, 2 * D_MODEL), BF16)),
        grid=(t // ROWS,),
        in_specs=[row, pl.BlockSpec((ROWS, D_MODEL), lambda i: (i, gblk)),
                  pl.BlockSpec((ROWS, D_MODEL), lambda i: (i, gblk + 1)), row, row],
        out_specs=(row, row, pl.BlockSpec((ROWS, 2 * D_MODEL), lambda i: (i, 0))),
        compiler_params=_cparams("parallel"),
    )(dm, proj2, proj2, ya, ys)


TQ = 128


def _split_bf16(v):
    hi = v.astype(BF16)
    lo = (v - hi.astype(F32)).astype(BF16)
    return hi, lo


def _tri_dot(v, tri):
    hi, lo = _split_bf16(v)
    return _dot(hi, tri, NN) + _dot(lo, tri, NN)


def _sb_scores(q, k, tri_gt, carry_r, mask):
    z = _dot(q, k, NT)
    l1p = jnp.log(1.0 + jnp.exp(-jnp.abs(z)))
    lb = jnp.minimum(z, 0.0) - l1p
    lom = -jnp.maximum(z, 0.0) - l1p
    if mask is not None:
        lom = jnp.where(mask, lom, 0.0)
    later = _tri_dot(lom, tri_gt) + carry_r
    a = jnp.exp(lb + later)
    if mask is not None:
        a = jnp.where(mask, a, 0.0)
    return lb, lom, a


def _attn_fwd(proj3):
    b, s, _ = proj3.shape
    nq = s // TQ
    scale = HEAD_DIM ** -0.5

    def kern(q_ref, k_ref, v_ref, za_ref, o_ref, yp_ref, qs, ks, vs):
        qs[...] = (q_ref[0] * scale).astype(BF16)
        ks[...] = k_ref[0].astype(BF16)
        vs[...] = v_ref[0].astype(BF16)
        row = lax.broadcasted_iota(jnp.int32, (TQ, TQ), 0)
        col = lax.broadcasted_iota(jnp.int32, (TQ, TQ), 1)
        tri_gt = (row > col).astype(BF16)
        causal = col < row

        for hh in range(2):
            lanes = slice(hh * HEAD_DIM, (hh + 1) * HEAD_DIM)

            def q_block(i, _):
                r0 = pl.multiple_of(i * TQ, TQ)
                q = qs[pl.ds(r0, TQ), lanes]

                def k_block(c0, carry, mask):
                    carry_r, acc = carry
                    k = ks[pl.ds(c0, TQ), lanes]
                    v = vs[pl.ds(c0, TQ), lanes]
                    _, lom, a = _sb_scores(q, k, tri_gt, carry_r, mask)
                    acc = acc + _dot(a.astype(BF16), v, NN)
                    return carry_r + jnp.sum(lom, axis=1, keepdims=True), acc

                carry = k_block(r0, (jnp.zeros((TQ, 1), F32), jnp.zeros((TQ, HEAD_DIM), F32)), causal)

                def off_diag(jj, carry):
                    c0 = pl.multiple_of((i - 1 - jj) * TQ, TQ)
                    return k_block(c0, carry, None)

                _, acc = lax.fori_loop(0, i, off_diag, carry)
                o_ref[0, pl.ds(r0, TQ), lanes] = acc
                za = za_ref[0, pl.ds(r0, TQ), lanes]
                yp_ref[0, pl.ds(r0, TQ), lanes] = (acc * (za * _sigmoid(za))).astype(BF16)
                return 0

            lax.fori_loop(0, nq, q_block, 0)

    def spec(c0):
        return pl.BlockSpec((1, s, LANES), lambda bi, hp: (bi, 0, c0 // LANES + hp))

    out_spec = pl.BlockSpec((1, s, LANES), lambda bi, hp: (bi, 0, hp))
    return pl.pallas_call(
        kern, name="attn_fwd",
        out_shape=(jax.ShapeDtypeStruct((b, s, D_MODEL), F32), jax.ShapeDtypeStruct((b, s, D_MODEL), BF16)),
        grid=(b, SB_HEADS // 2),
        in_specs=[spec(Q0), spec(K0), spec(V0), spec(ZA0)],
        out_specs=(out_spec, out_spec),
        scratch_shapes=[pltpu.VMEM((s, LANES), BF16)] * 3,
        compiler_params=_cparams("parallel", "parallel"),
    )(proj3, proj3, proj3, proj3)


def _attn_bwd(proj3, dyp3, o3):
    b, s, _ = proj3.shape
    nq = s // TQ
    scale = HEAD_DIM ** -0.5

    def kern(q_ref, k_ref, v_ref, za_ref, dyp_ref, o_ref, dq_ref, dk_ref, dv_ref, dza_ref,
             qs, ks, vs, dos, dk_acc, dv_acc):
        qs[...] = (q_ref[0] * scale).astype(BF16)
        ks[...] = k_ref[0].astype(BF16)
        vs[...] = v_ref[0].astype(BF16)
        za = za_ref[0]
        sg = _sigmoid(za)
        dyp = dyp_ref[0]
        dos[...] = (dyp * (za * sg)).astype(BF16)
        dza_ref[0] = (dyp * o_ref[0] * (sg * (1.0 + za * (1.0 - sg)))).astype(BF16)
        dk_acc[...] = jnp.zeros_like(dk_acc)
        dv_acc[...] = jnp.zeros_like(dv_acc)
        row = lax.broadcasted_iota(jnp.int32, (TQ, TQ), 0)
        col = lax.broadcasted_iota(jnp.int32, (TQ, TQ), 1)
        tri_gt = (row > col).astype(BF16)
        tri_ge = (row >= col).astype(BF16)
        causal = col < row

        for hh in range(2):
            lanes = slice(hh * HEAD_DIM, (hh + 1) * HEAD_DIM)

            def q_block(i, _):
                r0 = pl.multiple_of(i * TQ, TQ)
                q = qs[pl.ds(r0, TQ), lanes]
                do = dos[pl.ds(r0, TQ), lanes]
                total = jnp.sum(do.astype(F32) * o_ref[0, pl.ds(r0, TQ), lanes], axis=1, keepdims=True)

                def k_block(c0, carry, mask):
                    carry_r, carry_g, dq = carry
                    k = ks[pl.ds(c0, TQ), lanes]
                    v = vs[pl.ds(c0, TQ), lanes]
                    lb, lom, a = _sb_scores(q, k, tri_gt, carry_r, mask)
                    ab = a.astype(BF16)
                    g = _dot(do, v, NT) * ab.astype(F32)
                    suffix = _tri_dot(g, tri_ge) + carry_g
                    sig = jnp.exp(lb)
                    dz = g - (g + total - suffix) * sig
                    if mask is not None:
                        dz = jnp.where(mask, dz, 0.0)
                    dzb = dz.astype(BF16)
                    dq = dq + _dot(dzb, k, NN)
                    dk_acc[pl.ds(c0, TQ), lanes] += _dot(dzb, q, TN)
                    dv_acc[pl.ds(c0, TQ), lanes] += _dot(ab, do, TN)
                    return (carry_r + jnp.sum(lom, axis=1, keepdims=True),
                            carry_g + jnp.sum(g, axis=1, keepdims=True), dq)

                zero = jnp.zeros((TQ, 1), F32)
                carry = k_block(r0, (zero, zero, jnp.zeros((TQ, HEAD_DIM), F32)), causal)

                def off_diag(jj, carry):
                    c0 = pl.multiple_of((i - 1 - jj) * TQ, TQ)
                    return k_block(c0, carry, None)

                _, _, dq = lax.fori_loop(0, i, off_diag, carry)
                dq_ref[0, pl.ds(r0, TQ), lanes] = (dq * scale).astype(BF16)
                return 0

            lax.fori_loop(0, nq, q_block, 0)

        dk_ref[0] = dk_acc[...].astype(BF16)
        dv_ref[0] = dv_acc[...].astype(BF16)

    def spec(c0):
        return pl.BlockSpec((1, s, LANES), lambda bi, hp: (bi, 0, c0 // LANES + hp))

    plain = pl.BlockSpec((1, s, LANES), lambda bi, hp: (bi, 0, hp))
    out = jax.ShapeDtypeStruct((b, s, D_MODEL), BF16)
    return pl.pallas_call(
        kern, name="attn_bwd",
        out_shape=(out, out, out, out),
        grid=(b, SB_HEADS // 2),
        in_specs=[spec(Q0), spec(K0), spec(V0), spec(ZA0), plain, plain],
        out_specs=(plain, plain, plain, plain),
        scratch_shapes=[pltpu.VMEM((s, LANES), BF16)] * 4 + [pltpu.VMEM((s, LANES), F32)] * 2,
        compiler_params=_cparams("parallel", "parallel"),
    )(proj3, proj3, proj3, proj3, dyp3, o3)


CONV_COLS = 256
HALO = 8


def _conv_pre(xp, w_ref, b_ref, r0):
    pre = b_ref[...] + w_ref[CONV_K - 1:CONV_K, :] * xp[pl.ds(HALO + r0, CHUNK), :]
    for kk in range(1, CONV_K):
        pre = pre + w_ref[CONV_K - 1 - kk:CONV_K - kk, :] * xp[pl.ds(HALO + r0 - kk, CHUNK), :]
    return pre


def _conv_fwd(proj3, conv_w, conv_b):
    b, s, _ = proj3.shape
    nc = s // CHUNK

    def kern(x_ref, w_ref, b_ref, o_ref, xp):
        xp[0:HALO, :] = jnp.zeros((HALO, CONV_COLS), F32)
        xp[HALO:, :] = x_ref[0]
        for ci in range(nc):
            pre = _conv_pre(xp, w_ref, b_ref, ci * CHUNK)
            o_ref[0, ci * CHUNK:(ci + 1) * CHUNK, :] = pre * _sigmoid(pre)

    return pl.pallas_call(
        kern, name="conv_fwd",
        out_shape=jax.ShapeDtypeStruct((b, s, CONV_DIM), F32),
        grid=(CONV_DIM // CONV_COLS, b),
        in_specs=[pl.BlockSpec((1, s, CONV_COLS), lambda j, bi: (bi, 0, XBC0 // CONV_COLS + j)),
                  pl.BlockSpec((CONV_K, CONV_COLS), lambda j, bi: (0, j)),
                  pl.BlockSpec((1, CONV_COLS), lambda j, bi: (0, j))],
        out_specs=pl.BlockSpec((1, s, CONV_COLS), lambda j, bi: (bi, 0, j)),
        scratch_shapes=[pltpu.VMEM((s + HALO, CONV_COLS), F32)],
        compiler_params=_cparams("parallel", "parallel"),
    )(proj3, conv_w, conv_b)


def _conv_bwd(dact, proj3, conv_w, conv_b, col0, name):
    b, s, width = dact.shape
    nc = s // CHUNK
    j0 = col0 // CONV_COLS

    def kern(da_ref, x_ref, w_ref, b_ref, dx_ref, dw_ref, db_ref, xp, dp):
        @pl.when(pl.program_id(1) == 0)
        def _():
            dw_ref[...] = jnp.zeros_like(dw_ref)
            db_ref[...] = jnp.zeros_like(db_ref)

        xp[0:HALO, :] = jnp.zeros((HALO, CONV_COLS), F32)
        xp[HALO:, :] = x_ref[0]
        dp[s:, :] = jnp.zeros((HALO, CONV_COLS), F32)
        for ci in range(nc):
            r0 = ci * CHUNK
            pre = _conv_pre(xp, w_ref, b_ref, r0)
            sg = _sigmoid(pre)
            dpre = da_ref[0, r0:r0 + CHUNK, :] * (sg * (1.0 + pre * (1.0 - sg)))
            dp[r0:r0 + CHUNK, :] = dpre
            db_ref[...] += jnp.sum(dpre, axis=0, keepdims=True)
            for kk in range(CONV_K):
                tap = CONV_K - 1 - kk
                dw_ref[tap:tap + 1, :] += jnp.sum(dpre * xp[pl.ds(HALO + r0 - kk, CHUNK), :], axis=0, keepdims=True)
        for ci in range(nc):
            r0 = ci * CHUNK
            dx = w_ref[CONV_K - 1:CONV_K, :] * dp[pl.ds(r0, CHUNK), :]
            for kk in range(1, CONV_K):
                dx = dx + w_ref[CONV_K - 1 - kk:CONV_K - kk, :] * dp[pl.ds(r0 + kk, CHUNK), :]
            dx_ref[0, r0:r0 + CHUNK, :] = dx.astype(BF16)

    return pl.pallas_call(
        kern, name=name,
        out_shape=(jax.ShapeDtypeStruct((b, s, width), BF16), jax.ShapeDtypeStruct((CONV_K, width), F32),
                   jax.ShapeDtypeStruct((1, width), F32)),
        grid=(width // CONV_COLS, b),
        in_specs=[pl.BlockSpec((1, s, CONV_COLS), lambda j, bi: (bi, 0, j)),
                  pl.BlockSpec((1, s, CONV_COLS), lambda j, bi: (bi, 0, XBC0 // CONV_COLS + j0 + j)),
                  pl.BlockSpec((CONV_K, CONV_COLS), lambda j, bi: (0, j0 + j)),
                  pl.BlockSpec((1, CONV_COLS), lambda j, bi: (0, j0 + j))],
        out_specs=(pl.BlockSpec((1, s, CONV_COLS), lambda j, bi: (bi, 0, j)),
                   pl.BlockSpec((CONV_K, CONV_COLS), lambda j, bi: (0, j)),
                   pl.BlockSpec((1, CONV_COLS), lambda j, bi: (0, j))),
        scratch_shapes=[pltpu.VMEM((s + HALO, CONV_COLS), F32)] * 2,
        compiler_params=_cparams("parallel", "arbitrary"),
    )(dact, proj3, conv_w, conv_b)


def _ssd_common(dtr_ref, dtb_ref, alog_ref):
    lane = lax.broadcasted_iota(jnp.int32, (CHUNK, LANES), 1)
    row = lax.broadcasted_iota(jnp.int32, (CHUNK, LANES), 0)
    head_lane = lane < HEADS_PER_GROUP
    pre = dtr_ref[0, 0] + dtb_ref[0]
    dt = jnp.where(head_lane, jnp.maximum(pre, 0.0) + jnp.log(1.0 + jnp.exp(-jnp.abs(pre))), 0.0)
    a = jnp.where(head_lane[0:1], -jnp.exp(alog_ref[0]), 0.0)
    tril = (row >= lane).astype(F32)
    acs = _dot(tril, dt * a, NN, HIGHEST)
    acs_t = acs.T
    er = lax.broadcasted_iota(jnp.int32, (LANES, GROUP_WIDTH), 0)
    ec = lax.broadcasted_iota(jnp.int32, (LANES, GROUP_WIDTH), 1)
    expand = ((ec // HEAD_DIM) == er).astype(F32)
    tr = lax.broadcasted_iota(jnp.int32, (GROUP_WIDTH, LANES), 0)
    tc = lax.broadcasted_iota(jnp.int32, (GROUP_WIDTH, LANES), 1)
    reduce = ((tr // HEAD_DIM) == tc).astype(F32)
    dt_x = _dot(dt, expand, NN, HIGHEST)
    acs_x = _dot(acs, expand, NN, HIGHEST)
    end_x = acs_x[CHUNK - 1:CHUNK, :]
    end_col = jnp.broadcast_to(acs_t[:, CHUNK - 1:CHUNK], (LANES, LANES))
    chunk_decay = jnp.exp(_dot(reduce, end_col, NN, HIGHEST))
    causal = row >= lane
    return dict(dt=dt, a=a, pre=pre, head_lane=head_lane, acs=acs, acs_t=acs_t, expand=expand, reduce=reduce,
                dt_x=dt_x, acs_x=acs_x, end_x=end_x, chunk_decay=chunk_decay, causal=causal, row=row, lane=lane)


def _ssd_decay(cm, h):
    seg = cm["acs"][:, h:h + 1] - cm["acs_t"][h:h + 1, :]
    return jnp.where(cm["causal"], jnp.exp(jnp.minimum(seg, 0.0)), 0.0)


def _ssd_fwd(xact, proj3, dtr_g, dtb_g, alog_g, dskip_x, snw):
    b, s, _ = xact.shape
    nc = s // CHUNK
    g4 = SSD_GROUPS

    def kern(xs_ref, bm_ref, cm_ref, zs_ref, dtr_ref, dtb_ref, alog_ref, dsk_ref, snw_ref,
             y_ref, yn_ref, hst_ref, h_sc):
        @pl.when(pl.program_id(2) == 0)
        def _():
            h_sc[...] = jnp.zeros_like(h_sc)

        cm = _ssd_common(dtr_ref, dtb_ref, alog_ref)
        x = xs_ref[0]
        bmb = bm_ref[0].astype(BF16)
        cmb = cm_ref[0].astype(BF16)
        h_in = h_sc[...]
        hst_ref[0, 0, 0] = h_in
        xdt = x * cm["dt_x"]
        xdtb = xdt.astype(BF16)
        cb = _dot(cmb, bmb, NT)
        y_off = _dot(cmb, h_in.astype(BF16), NT) * jnp.exp(cm["acs_x"])
        for h in range(HEADS_PER_GROUP):
            lanes = slice(h * HEAD_DIM, (h + 1) * HEAD_DIM)
            m = (cb * _ssd_decay(cm, h)).astype(BF16)
            y_ref[0, :, lanes] = _dot(m, xdtb[:, lanes], NN)
        y = y_ref[0] + y_off + x * dsk_ref[...]
        y_ref[0] = y
        w = (xdt * jnp.exp(cm["end_x"] - cm["acs_x"])).astype(BF16)
        h_sc[...] = h_in * cm["chunk_decay"] + _dot(w, bmb, TN)
        zs = zs_ref[0]
        y2 = y * (zs * _sigmoid(zs))
        yn_ref[0] = (y2 * lax.rsqrt(jnp.mean(y2 * y2, axis=-1, keepdims=True) + EPS) * snw_ref[...]).astype(BF16)

    gw = GROUP_WIDTH
    small = pl.BlockSpec((1, 1, LANES), lambda gi, bi, ci: (gi, 0, 0))
    xblk = pl.BlockSpec((1, CHUNK, gw), lambda gi, bi, ci: (bi, ci, gi))
    return pl.pallas_call(
        kern, name="ssd_fwd",
        out_shape=(jax.ShapeDtypeStruct((b, s, SSD_WIDTH), F32), jax.ShapeDtypeStruct((b, s, SSD_WIDTH), BF16),
                   jax.ShapeDtypeStruct((b, nc, g4, gw, SSD_STATE), F32)),
        grid=(g4, b, nc),
        in_specs=[xblk,
                  pl.BlockSpec((1, CHUNK, LANES), lambda gi, bi, ci: (bi, ci, SSD_WIDTH // LANES + gi)),
                  pl.BlockSpec((1, CHUNK, LANES), lambda gi, bi, ci: (bi, ci, SSD_WIDTH // LANES + g4 + gi)),
                  pl.BlockSpec((1, CHUNK, gw), lambda gi, bi, ci: (bi, ci, ZS0 // gw + gi)),
                  pl.BlockSpec((1, 1, CHUNK, LANES), lambda gi, bi, ci: (bi, gi, ci, 0)),
                  small, small,
                  pl.BlockSpec((1, gw), lambda gi, bi, ci: (0, gi)),
                  pl.BlockSpec((1, gw), lambda gi, bi, ci: (0, gi))],
        out_specs=(xblk, xblk, pl.BlockSpec((1, 1, 1, gw, SSD_STATE), lambda gi, bi, ci: (bi, ci, gi, 0, 0))),
        scratch_shapes=[pltpu.VMEM((gw, SSD_STATE), F32)],
        compiler_params=_cparams("parallel", "parallel", "arbitrary"),
    )(xact, xact, xact, proj3, dtr_g, dtb_g, alog_g, dskip_x, snw)


def _ssd_bwd(dyn3, y3, xact, proj3, hst, dtr_g, dtb_g, alog_g, dskip_x, snw):
    b, s, _ = xact.shape
    nc = s // CHUNK
    g4 = SSD_GROUPS
    gw = GROUP_WIDTH

    def kern(dyn_ref, y_ref, xs_ref, bm_ref, cm_ref, zs_ref, hst_ref, dtr_ref, dtb_ref, alog_ref, dsk_ref, snw_ref,
             dxs_ref, dbm_ref, dcm_ref, dzs_ref, ddtr_ref, dsnw_ref, dalog_ref, ddtb_ref, ddsk_ref, dh_sc):
        first = jnp.logical_and(pl.program_id(1) == 0, pl.program_id(2) == 0)

        @pl.when(first)
        def _():
            dsnw_ref[...] = jnp.zeros_like(dsnw_ref)
            dalog_ref[...] = jnp.zeros_like(dalog_ref)
            ddtb_ref[...] = jnp.zeros_like(ddtb_ref)
            ddsk_ref[...] = jnp.zeros_like(ddsk_ref)

        @pl.when(pl.program_id(2) == 0)
        def _():
            dh_sc[...] = jnp.zeros_like(dh_sc)

        cm = _ssd_common(dtr_ref, dtb_ref, alog_ref)
        row, lane = cm["row"], cm["lane"]
        y = y_ref[0]
        zs = zs_ref[0]
        sg = _sigmoid(zs)
        silu = zs * sg
        y2 = y * silu
        rstd = lax.rsqrt(jnp.mean(y2 * y2, axis=-1, keepdims=True) + EPS)
        y2h = y2 * rstd
        dyn = dyn_ref[0]
        dsnw_ref[0] += jnp.sum(dyn * y2h, axis=0, keepdims=True)
        gwv = dyn * snw_ref[...]
        dy2 = rstd * (gwv - y2h * jnp.mean(gwv * y2h, axis=-1, keepdims=True))
        dzs_ref[0] = (dy2 * y * (sg * (1.0 + zs * (1.0 - sg)))).astype(BF16)
        dy = dy2 * silu
        dyb = dy.astype(BF16)

        x = xs_ref[0]
        bmb = bm_ref[0].astype(BF16)
        cmb = cm_ref[0].astype(BF16)
        h_in = hst_ref[0, 0, 0]
        h_inb = h_in.astype(BF16)
        d_hn = dh_sc[...]
        d_hnb = d_hn.astype(BF16)
        xdt = x * cm["dt_x"]
        xdtb = xdt.astype(BF16)
        eacs = jnp.exp(cm["acs_x"])
        dte = jnp.exp(cm["end_x"] - cm["acs_x"])
        wb = (xdt * dte).astype(BF16)

        dsk_lanes = jnp.broadcast_to(jnp.sum(dy * x, axis=0, keepdims=True), (8, gw))
        ddsk_ref[0] += _dot(dsk_lanes, cm["reduce"], NN, HIGHEST)[0:1, :]
        dyo = dy * eacs
        dyob = dyo.astype(BF16)
        dacs_x = dyo * _dot(cmb, h_inb, NT)
        dcm = _dot(dyob, h_inb, NN)
        dh_in = _dot(dyob, cmb, TN)
        dw = _dot(bmb, d_hnb, NT)
        dbm = _dot(wb, d_hnb, NN)
        dxdt = dw * dte
        e_l = dw * xdt * dte
        dacs_x = dacs_x - e_l
        dend_x = jnp.sum(e_l, axis=0, keepdims=True)
        dh_sc[...] = d_hn * cm["chunk_decay"] + dh_in
        q = d_hn * h_in * cm["chunk_decay"]
        dend_x = dend_x + _dot(jnp.ones((8, SSD_STATE), F32), q, NT, HIGHEST)[0:1, :]
        last_row = lax.broadcasted_iota(jnp.int32, (CHUNK, gw), 0) == CHUNK - 1
        dacs_x = dacs_x + jnp.where(last_row, dend_x, 0.0)

        cb = _dot(cmb, bmb, NT)
        dcb = jnp.zeros((CHUNK, CHUNK), F32)
        dacs = jnp.zeros((CHUNK, LANES), F32)
        dacs_t = jnp.zeros((LANES, CHUNK), F32)
        for h in range(HEADS_PER_GROUP):
            lanes = slice(h * HEAD_DIM, (h + 1) * HEAD_DIM)
            decay = _ssd_decay(cm, h)
            m = cb * decay
            dm = _dot(dyb[:, lanes], xdtb[:, lanes], NT)
            dxs_ref[0, :, lanes] = _dot(m.astype(BF16), dyb[:, lanes], TN)
            dcb_h = dm * decay
            dcb = dcb + dcb_h
            n = dcb_h * cb
            dacs = dacs + jnp.where(lane == h, jnp.sum(n, axis=1, keepdims=True), 0.0)
            dacs_t = dacs_t + jnp.where(row == h, jnp.sum(n, axis=0, keepdims=True), 0.0)
        dcbb = dcb.astype(BF16)
        dcm_ref[0] = dcm + _dot(dcbb, bmb, NN)
        dbm_ref[0] = dbm + _dot(dcbb, cmb, TN)
        dxdt = dxdt + dxs_ref[0]
        dxs_ref[0] = dy * dsk_ref[...] + dxdt * cm["dt_x"]

        dacs = dacs - dacs_t.T + _dot(dacs_x, cm["reduce"], NN, HIGHEST)
        ddt = _dot(dxdt * x, cm["reduce"], NN, HIGHEST)
        triu = (row <= lane).astype(F32)
        rc = _dot(triu, dacs, NN, HIGHEST)
        ddt = ddt + cm["a"] * rc
        dalog_ref[0] += jnp.sum(cm["dt"] * rc, axis=0, keepdims=True) * cm["a"]
        ddtr = jnp.where(cm["head_lane"], ddt * _sigmoid(cm["pre"]), 0.0)
        ddtr_ref[0, 0] = ddtr
        ddtb_ref[0] += jnp.sum(ddtr, axis=0, keepdims=True)

    def rev(ci):
        return nc - 1 - ci

    small = pl.BlockSpec((1, 1, LANES), lambda gi, bi, ci: (gi, 0, 0))
    xblk = pl.BlockSpec((1, CHUNK, gw), lambda gi, bi, ci: (bi, rev(ci), gi))
    nblk = pl.BlockSpec((1, CHUNK, LANES), lambda gi, bi, ci: (bi, rev(ci), gi))
    gvec = pl.BlockSpec((1, gw), lambda gi, bi, ci: (0, gi))
    gacc = pl.BlockSpec((1, 1, gw), lambda gi, bi, ci: (gi, 0, 0))
    return pl.pallas_call(
        kern, name="ssd_bwd",
        out_shape=(jax.ShapeDtypeStruct((b, s, SSD_WIDTH), F32),
                   jax.ShapeDtypeStruct((b, s, g4 * SSD_STATE), F32),
                   jax.ShapeDtypeStruct((b, s, g4 * SSD_STATE), F32),
                   jax.ShapeDtypeStruct((b, s, SSD_WIDTH), BF16),
                   jax.ShapeDtypeStruct((b, g4, s, LANES), F32),
                   jax.ShapeDtypeStruct((g4, 1, gw), F32),
                   jax.ShapeDtypeStruct((g4, 1, LANES), F32),
                   jax.ShapeDtypeStruct((g4, 1, LANES), F32),
                   jax.ShapeDtypeStruct((g4, 1, LANES), F32)),
        grid=(g4, b, nc),
        in_specs=[xblk, xblk, xblk,
                  pl.BlockSpec((1, CHUNK, LANES), lambda gi, bi, ci: (bi, rev(ci), SSD_WIDTH // LANES + gi)),
                  pl.BlockSpec((1, CHUNK, LANES), lambda gi, bi, ci: (bi, rev(ci), SSD_WIDTH // LANES + g4 + gi)),
                  pl.BlockSpec((1, CHUNK, gw), lambda gi, bi, ci: (bi, rev(ci), ZS0 // gw + gi)),
                  pl.BlockSpec((1, 1, 1, gw, SSD_STATE), lambda gi, bi, ci: (bi, rev(ci), gi, 0, 0)),
                  pl.BlockSpec((1, 1, CHUNK, LANES), lambda gi, bi, ci: (bi, gi, rev(ci), 0)),
                  small, small, gvec, gvec],
        out_specs=(xblk, nblk, nblk, xblk,
                   pl.BlockSpec((1, 1, CHUNK, LANES), lambda gi, bi, ci: (bi, gi, rev(ci), 0)),
                   gacc, small, small, small),
        scratch_shapes=[pltpu.VMEM((gw, SSD_STATE), F32)],
        compiler_params=_cparams("parallel", "arbitrary", "arbitrary"),
    )(dyn3, y3, xact, xact, xact, proj3, hst, dtr_g, dtb_g, alog_g, dskip_x, snw)


def _adamw(w, g, m, v, name):
    r, c = w.shape
    tr = 128 if r % 128 == 0 else r

    def kern(w_ref, g_ref, m_ref, v_ref, d_ref, nm_ref, nv_ref):
        gv = g_ref[...]
        nm = ADAM_B1 * m_ref[...] + (1.0 - ADAM_B1) * gv
        nv = ADAM_B2 * v_ref[...] + (1.0 - ADAM_B2) * (gv * gv)
        m_hat = nm / (1.0 - ADAM_B1 ** ADAM_STEP)
        v_hat = nv / (1.0 - ADAM_B2 ** ADAM_STEP)
        d_ref[...] = -ADAM_LR * (m_hat / (jnp.sqrt(v_hat) + ADAM_EPS) + ADAM_WD * w_ref[...])
        nm_ref[...] = nm
        nv_ref[...] = nv

    blk = pl.BlockSpec((tr, c), lambda i: (i, 0))
    out = jax.ShapeDtypeStruct((r, c), F32)
    return pl.pallas_call(
        kern, name=name, out_shape=(out, out, out), grid=(r // tr,),
        in_specs=[blk] * 4, out_specs=(blk, blk, blk),
        compiler_params=_cparams("parallel"),
    )(w, g, m, v)


ANY = pl.BlockSpec(memory_space=pl.ANY)


def _position():
    return lax.axis_index("x"), lax.axis_index("y"), lax.axis_index("c")


def _other_chips(x, y):
    return [(1 - x, y), (x, 1 - y), (1 - x, 1 - y)]


def _gather_weights(pack):
    rows = pack.shape[0]
    half = rows // 2

    def body(p_ref, out_ref, send_sems, recv_sems, local_sem):
        x, y, c = _position()
        me = 2 * x + y
        chips = _other_chips(x, y)

        def slab(chip, hf):
            return out_ref.at[chip, pl.ds(hf * half, half), :]

        mine = pltpu.make_async_copy(p_ref, out_ref.at[me], local_sem)
        mine.start()
        first = [pltpu.make_async_remote_copy(
            src_ref=p_ref.at[pl.ds(c * half, half), :], dst_ref=slab(me, c),
            send_sem=send_sems.at[j], recv_sem=recv_sems.at[j],
            device_id=(px, py, c), device_id_type=MESH) for j, (px, py) in enumerate(chips)]
        for cp in first:
            cp.start()
        passed = [pltpu.make_async_remote_copy(
            src_ref=slab(2 * px + py, c), dst_ref=slab(2 * px + py, c),
            send_sem=send_sems.at[3 + j], recv_sem=recv_sems.at[3 + j],
            device_id=(x, y, 1 - c), device_id_type=MESH) for j, (px, py) in enumerate(chips)]
        for j, (px, py) in enumerate(chips):
            pltpu.make_async_remote_copy(
                src_ref=slab(2 * px + py, c), dst_ref=slab(2 * px + py, c),
                send_sem=send_sems.at[j], recv_sem=recv_sems.at[j],
                device_id=(px, py, c), device_id_type=MESH).wait_recv()
            passed[j].start()
        for j, (px, py) in enumerate(chips):
            pltpu.make_async_remote_copy(
                src_ref=slab(2 * px + py, 1 - c), dst_ref=slab(2 * px + py, 1 - c),
                send_sem=send_sems.at[3 + j], recv_sem=recv_sems.at[3 + j],
                device_id=(x, y, 1 - c), device_id_type=MESH).wait_recv()
        for cp in first + passed:
            cp.wait_send()
        mine.wait()

    return pl.pallas_call(
        body, name="gather_weights",
        out_shape=jax.ShapeDtypeStruct((N_CHIPS, rows, LANES), pack.dtype),
        in_specs=[ANY], out_specs=ANY,
        scratch_shapes=[pltpu.SemaphoreType.DMA((6,)), pltpu.SemaphoreType.DMA((6,)), pltpu.SemaphoreType.DMA],
    )(pack)


def _sibling_swap(v, part, name):
    n, rows, _ = v.shape
    half = rows // 2 if part == "other" else rows

    def body(v_ref, out_ref, send_sem, recv_sem):
        x, y, c = _position()
        start = (1 - c) * half if part == "other" else 0
        cp = pltpu.make_async_remote_copy(
            src_ref=v_ref.at[:, pl.ds(start, half), :], dst_ref=out_ref,
            send_sem=send_sem, recv_sem=recv_sem, device_id=(x, y, 1 - c), device_id_type=MESH)
        cp.start()
        cp.wait()

    return pl.pallas_call(
        body, name=name,
        out_shape=jax.ShapeDtypeStruct((n, half, LANES), v.dtype),
        in_specs=[ANY], out_specs=ANY,
        scratch_shapes=[pltpu.SemaphoreType.DMA, pltpu.SemaphoreType.DMA],
    )(v)


def _chip_all_to_all(p):
    def body(p_ref, out_ref, send_sems, recv_sems, local_sem):
        x, y, c = _position()
        me = 2 * x + y
        chips = _other_chips(x, y)
        mine = pltpu.make_async_copy(p_ref.at[me], out_ref.at[me], local_sem)
        mine.start()
        sends = [pltpu.make_async_remote_copy(
            src_ref=p_ref.at[2 * px + py], dst_ref=out_ref.at[me],
            send_sem=send_sems.at[j], recv_sem=recv_sems.at[j],
            device_id=(px, py, c), device_id_type=MESH) for j, (px, py) in enumerate(chips)]
        for cp in sends:
            cp.start()
        for j, (px, py) in enumerate(chips):
            pltpu.make_async_remote_copy(
                src_ref=p_ref.at[me], dst_ref=out_ref.at[2 * px + py],
                send_sem=send_sems.at[j], recv_sem=recv_sems.at[j],
                device_id=(px, py, c), device_id_type=MESH).wait_recv()
        for cp in sends:
            cp.wait_send()
        mine.wait()

    return pl.pallas_call(
        body, name="grad_all_to_all",
        out_shape=jax.ShapeDtypeStruct(p.shape, p.dtype),
        in_specs=[ANY], out_specs=ANY,
        scratch_shapes=[pltpu.SemaphoreType.DMA((3,)), pltpu.SemaphoreType.DMA((3,)), pltpu.SemaphoreType.DMA],
    )(p)


ADD_ROWS = 1712


def _add_halves(g, sw, core):
    n, rows, _ = g.shape
    half = rows // 2
    nb = half // ADD_ROWS

    def kern(c_ref, g_ref, s_ref, o_ref):
        o_ref[...] = g_ref[...] + s_ref[...]

    blk = pl.BlockSpec((1, ADD_ROWS, LANES), lambda j, i, c_ref: (j, i, 0))
    return pl.pallas_call(
        kern, name="grad_add_halves",
        out_shape=jax.ShapeDtypeStruct((n, half, LANES), F32),
        grid_spec=pltpu.PrefetchScalarGridSpec(
            num_scalar_prefetch=1, grid=(n, nb),
            in_specs=[pl.BlockSpec((1, ADD_ROWS, LANES), lambda j, i, c_ref: (j, c_ref[0] * nb + i, 0)), blk],
            out_specs=blk),
        compiler_params=_cparams("parallel", "parallel"),
    )(core, g, sw)


def _sum_chips(rx):
    n, rows, _ = rx.shape

    def kern(r_ref, o_ref):
        o_ref[0] = ((r_ref[0] + r_ref[1]) + r_ref[2]) + r_ref[3]

    return pl.pallas_call(
        kern, name="grad_sum_chips",
        out_shape=jax.ShapeDtypeStruct((1, rows, LANES), F32),
        grid=(rows // ADD_ROWS,),
        in_specs=[pl.BlockSpec((n, ADD_ROWS, LANES), lambda i: (0, i, 0))],
        out_specs=pl.BlockSpec((1, ADD_ROWS, LANES), lambda i: (0, i, 0)),
        compiler_params=_cparams("parallel"),
    )(rx)


def _gather_small(v, reduce, name):
    rows = v.shape[0]

    def body(v_ref, out_ref, buf, send_sems, recv_sems):
        x, y, c = _position()
        me = 4 * x + 2 * y + c
        buf[me] = v_ref[...]
        peers = [(x ^ (k >> 2), y ^ ((k >> 1) & 1), c ^ (k & 1)) for k in range(1, 8)]
        copies = [pltpu.make_async_remote_copy(
            src_ref=v_ref, dst_ref=buf.at[me],
            send_sem=send_sems.at[k], recv_sem=recv_sems.at[k],
            device_id=peer, device_id_type=MESH) for k, peer in enumerate(peers)]
        for cp in copies:
            cp.start()
        for k, (px, py, pc) in enumerate(peers):
            pltpu.make_async_remote_copy(
                src_ref=v_ref, dst_ref=buf.at[4 * px + 2 * py + pc],
                send_sem=send_sems.at[k], recv_sem=recv_sems.at[k],
                device_id=(px, py, pc), device_id_type=MESH).wait_recv()
        for cp in copies:
            cp.wait_send()
        if reduce:
            total = buf[0]
            for d in range(1, 8):
                total = total + buf[d]
            out_ref[...] = total
        else:
            out_ref[...] = buf[...]

    vm = pl.BlockSpec(memory_space=pltpu.VMEM)
    return pl.pallas_call(
        body, name=name,
        out_shape=jax.ShapeDtypeStruct((rows, LANES) if reduce else (8, rows, LANES), F32),
        in_specs=[vm], out_specs=vm,
        scratch_shapes=[pltpu.VMEM((8, rows, LANES), F32), pltpu.SemaphoreType.DMA((7,)), pltpu.SemaphoreType.DMA((7,))],
    )(v)


def _pad_rows(a, rows):
    return jnp.pad(a, ((0, rows - a.shape[0]), (0, 0)))


def _lane_pad(v):
    n = v.shape[1]
    return jnp.pad(v, ((0, 0), (0, -n % LANES)))


def _gather_all(w_in, w_attn_out, w_ssm_out, w_o, conv_w):
    d = D_MODEL
    shard_cols = w_in.shape[2]
    big = [w_in[0], w_attn_out[0], w_ssm_out[0], w_o[0]]
    big_rows = [a.size // LANES for a in big]
    pack = jnp.concatenate([a.reshape(-1, LANES) for a in big], axis=0).astype(BF16)
    gathered = _gather_weights(pack)
    offs = [0]
    for r in big_rows:
        offs.append(offs[-1] + r)
    w_in_all = gathered[:, offs[0]:offs[1]].reshape(N_CHIPS, d, shard_cols).transpose(1, 0, 2).reshape(d, D_PROJ)
    w_proj = jnp.concatenate([w_in_all[:, :GATE0], w_in_all[:, GATE0 + 32:], w_in_all[:, GATE0:GATE0 + 32],
                              jnp.zeros((d, DT_PAD - 32), BF16)], axis=1)
    w_ao = gathered[:, offs[1]:offs[2]].reshape(D_MODEL, d)
    w_so = gathered[:, offs[2]:offs[3]].reshape(SSD_WIDTH, d)
    w_oo = gathered[:, offs[3]:offs[4]].reshape(D_MODEL, d)
    conv_rows = conv_w[0].size // LANES
    conv_all = _gather_small(conv_w[0].reshape(conv_rows, LANES), False, "gather_conv_w")
    conv_w_all = conv_all[0::2].reshape(N_CHIPS, CONV_K, CONV_DIM // N_CHIPS).transpose(1, 0, 2).reshape(CONV_K, CONV_DIM)

    return w_proj, w_ao, w_so, w_oo, conv_w_all, offs, conv_rows


def _local_step(x, loss_target, norm_w, w_proj, conv_w_all, conv_b, dt_bias, a_log, d_skip, ssm_norm_w,
                w_ao, w_so, w_oo, final_norm_w):
    b, s, d = x.shape
    t = b * s
    g4, hg = SSD_GROUPS, HEADS_PER_GROUP
    dtb_g = _lane_pad(dt_bias.reshape(g4, hg)).reshape(g4, 1, LANES)
    alog_g = _lane_pad(a_log.reshape(g4, hg)).reshape(g4, 1, LANES)
    dskip_x = jnp.repeat(d_skip, HEAD_DIM, axis=1)
    fnw = final_norm_w.reshape(1, d)

    x2 = x.reshape(t, d)
    h = _rms_fwd(x2, norm_w)
    proj = _matmul(h, w_proj, tm=512, tn=1280, tk=1024, name="proj")
    proj3 = proj.reshape(b, s, NP)
    o3, yp3 = _attn_fwd(proj3)
    xact = _conv_fwd(proj3, conv_w_all, conv_b)
    dtr = proj3[:, :, DT0:DT0 + g4 * hg].reshape(b, s, g4, hg).transpose(0, 2, 1, 3)
    dtr_g = jnp.pad(dtr, ((0, 0), (0, 0), (0, 0), (0, LANES - hg)))
    y3, yn3, hst = _ssd_fwd(xact, proj3, dtr_g, dtb_g, alog_g, dskip_x, ssm_norm_w)
    yp = yp3.reshape(t, D_MODEL)
    yn = yn3.reshape(t, SSD_WIDTH)
    ya = _matmul(yp, w_ao, tm=512, tn=1024, tk=1024, name="attn_out")
    ys = _matmul(yn, w_so, tm=512, tn=1024, tk=2048, name="ssm_out")
    merged = _merge_fwd(proj, ya, ys)
    mo = _matmul(merged, w_oo, tm=512, tn=1024, tk=1024, name="out_proj")
    dout, doutb, loss_part, d_fnw = _final_fwd_bwd(x2, mo, loss_target.reshape(t, d), fnw)

    dmerged = _matmul(doutb, w_oo, tb=True, tm=512, tn=1024, tk=1024, name="d_merged")
    g_wo = _matmul(merged, doutb, ta=True, tm=512, tn=1024, tk=1024, name="g_w_o")
    dya, dys, dgate = _merge_bwd(dmerged, proj, ya, ys)
    dyp = _matmul(dya, w_ao, tb=True, tm=512, tn=1024, tk=1024, name="d_attn_pre")
    g_wao = _matmul(yp, dya, ta=True, tm=512, tn=1024, tk=1024, name="g_w_attn_out")
    dyn = _matmul(dys, w_so, tb=True, tm=512, tn=2048, tk=1024, name="d_ssm_norm")
    g_wso = _matmul(yn, dys, ta=True, tm=512, tn=1024, tk=1024, name="g_w_ssm_out")
    dq, dk, dv, dza = _attn_bwd(proj3, dyp.reshape(b, s, D_MODEL), o3)
    (dxs, dbm, dcm, dzs, ddtr_g, d_snw_g, d_alog_g, d_dtb_g, d_dsk_g) = _ssd_bwd(
        dyn.reshape(b, s, SSD_WIDTH), y3, xact, proj3, hst, dtr_g, dtb_g, alog_g, dskip_x, ssm_norm_w)
    dx_xs, g_cw_xs, g_cb_xs = _conv_bwd(dxs, proj3, conv_w_all, conv_b, 0, "conv_bwd_x")
    dx_bm, g_cw_bm, g_cb_bm = _conv_bwd(dbm, proj3, conv_w_all, conv_b, SSD_WIDTH, "conv_bwd_b")
    dx_cm, g_cw_cm, g_cb_cm = _conv_bwd(dcm, proj3, conv_w_all, conv_b, SSD_WIDTH + g4 * SSD_STATE, "conv_bwd_c")
    ddt = ddtr_g[:, :, :, :hg].transpose(0, 2, 1, 3).reshape(b, s, g4 * hg).astype(BF16)
    dproj = jnp.concatenate([dq, dk, dv, dza, dzs, dx_xs, dx_bm, dx_cm, dgate.reshape(b, s, 2 * D_MODEL),
                             jnp.pad(ddt, ((0, 0), (0, 0), (0, DT_PAD - g4 * hg)))], axis=2).reshape(t, NP)
    g_wproj = _matmul(h, dproj, ta=True, tm=512, tn=1280, tk=1024, name="g_w_in")
    dh = _matmul(dproj, w_proj, tb=True, tm=512, tn=1024, tk=1280, name="d_h")
    grad_x, d_nw = _rms_bwd(dh, x2, norm_w, dout)
    g_cw = jnp.concatenate([g_cw_xs, g_cw_bm, g_cw_cm], axis=1)
    g_cb = jnp.concatenate([g_cb_xs, g_cb_bm, g_cb_cm], axis=1)
    return (loss_part, grad_x, d_nw, g_wproj, g_cw, g_cb, d_dtb_g, d_alog_g, d_dsk_g, d_snw_g, g_wao, g_wso, g_wo, d_fnw)


def kernel(x, norm_w, w_in, conv_w, conv_b, dt_bias, a_log, d_skip, ssm_norm_w, w_attn_out, w_ssm_out, w_o, final_norm_w, loss_target, m_norm_w, m_w_in, m_conv_w, m_conv_b, m_dt_bias, m_a_log, m_d_skip, m_ssm_norm_w, m_w_attn_out, m_w_ssm_out, m_w_o, m_final_norm_w, v_norm_w, v_w_in, v_conv_w, v_conv_b, v_dt_bias, v_a_log, v_d_skip, v_ssm_norm_w, v_w_attn_out, v_w_ssm_out, v_w_o, v_final_norm_w):
    b, s, d = x.shape
    core = lax.axis_index("c")
    g4, hg = SSD_GROUPS, HEADS_PER_GROUP
    shard_cols = w_in.shape[2]
    w_proj, w_ao, w_so, w_oo, conv_w_all, offs, conv_rows = _gather_all(w_in, w_attn_out, w_ssm_out, w_o, conv_w)
    (loss_part, grad_x, d_nw, g_wproj, g_cw, g_cb, d_dtb_g, d_alog_g, d_dsk_g, d_snw_g, g_wao, g_wso, g_wo, d_fnw) = _local_step(
        x, loss_target, norm_w, w_proj, conv_w_all, conv_b, dt_bias, a_log, d_skip, ssm_norm_w, w_ao, w_so, w_oo, final_norm_w)

    g_win = jnp.concatenate([g_wproj[:, :GATE0], g_wproj[:, DT0:DT0 + 32], g_wproj[:, GATE0:DT0]], axis=1)
    per_chip = jnp.concatenate([
        g_win.reshape(d, N_CHIPS, shard_cols).transpose(1, 0, 2).reshape(N_CHIPS, -1, LANES),
        g_wao.reshape(N_CHIPS, -1, LANES), g_wso.reshape(N_CHIPS, -1, LANES), g_wo.reshape(N_CHIPS, -1, LANES),
        g_cw.reshape(CONV_K, N_CHIPS, CONV_DIM // N_CHIPS).transpose(1, 0, 2).reshape(N_CHIPS, -1, LANES)], axis=1)
    n_rows = per_chip.shape[1]
    pad_rows = -n_rows % (2 * ADD_ROWS)
    per_chip = jnp.pad(per_chip, ((0, 0), (0, pad_rows), (0, 0)))
    from_sibling = _sibling_swap(per_chip, "other", "grad_swap_halves")
    chip_sum = _add_halves(per_chip, from_sibling, core.reshape(1).astype(jnp.int32))
    from_chips = _chip_all_to_all(chip_sum)
    mine = _sum_chips(from_chips)
    theirs = _sibling_swap(mine, "all", "grad_swap_result")
    is_south = core == 0
    shard = jnp.concatenate([jnp.where(is_south, mine[0], theirs[0]), jnp.where(is_south, theirs[0], mine[0])], axis=0)

    small = jnp.concatenate([
        loss_part, d_nw, g_cb, _lane_pad(d_dtb_g[:, 0, :hg].reshape(1, -1)), _lane_pad(d_alog_g[:, 0, :hg].reshape(1, -1)),
        _lane_pad(d_dsk_g[:, 0, :hg].reshape(1, -1)),
        d_snw_g.reshape(1, -1), d_fnw], axis=1)
    small_rows = small.shape[1] // LANES
    reduced = _gather_small(_pad_rows(small.reshape(small_rows, LANES), -(-small_rows // 8) * 8), True, "reduce_small")
    flat = reduced.reshape(-1)

    def take(start, n):
        return flat[start:start + n].reshape(1, n)

    loss = flat[0]
    pos = LANES
    g_norm_w = take(pos, d); pos += d
    g_conv_b = take(pos, CONV_DIM); pos += CONV_DIM
    g_dt_bias = take(pos, g4 * hg); pos += LANES
    g_a_log = take(pos, g4 * hg); pos += LANES
    g_d_skip = take(pos, g4 * hg); pos += LANES
    g_ssm_norm_w = take(pos, SSD_WIDTH); pos += SSD_WIDTH
    g_final_norm_w = take(pos, d)

    g_w_in = shard[offs[0]:offs[1]].reshape(d, shard_cols)
    g_w_attn_out = shard[offs[1]:offs[2]].reshape(D_MODEL // N_CHIPS, d)
    g_w_ssm_out = shard[offs[2]:offs[3]].reshape(SSD_WIDTH // N_CHIPS, d)
    g_w_o = shard[offs[3]:offs[4]].reshape(D_MODEL // N_CHIPS, d)
    g_conv_w = shard[offs[4]:offs[4] + conv_rows].reshape(CONV_K, CONV_DIM // N_CHIPS)

    names = ["norm_w", "w_in", "conv_w", "conv_b", "dt_bias", "a_log", "d_skip", "ssm_norm_w",
             "w_attn_out", "w_ssm_out", "w_o", "final_norm_w"]
    weights = [norm_w, w_in, conv_w, conv_b, dt_bias, a_log, d_skip, ssm_norm_w, w_attn_out, w_ssm_out, w_o, final_norm_w]
    grads = [g_norm_w, g_w_in, g_conv_w, g_conv_b, g_dt_bias, g_a_log, g_d_skip, g_ssm_norm_w,
             g_w_attn_out, g_w_ssm_out, g_w_o, g_final_norm_w]
    ms = [m_norm_w, m_w_in, m_conv_w, m_conv_b, m_dt_bias, m_a_log, m_d_skip, m_ssm_norm_w,
          m_w_attn_out, m_w_ssm_out, m_w_o, m_final_norm_w]
    vs = [v_norm_w, v_w_in, v_conv_w, v_conv_b, v_dt_bias, v_a_log, v_d_skip, v_ssm_norm_w,
          v_w_attn_out, v_w_ssm_out, v_w_o, v_final_norm_w]
    out_g, out_d, out_m, out_v = [], [], [], []
    for name, w, g, m, v in zip(names, weights, grads, ms, vs):
        shape2 = g.shape
        dlt, nm, nv = _adamw(w.reshape(shape2), g, m.reshape(shape2), v.reshape(shape2), "adamw_" + name)
        out_g.append(g.reshape(w.shape))
        out_d.append(dlt.reshape(w.shape))
        out_m.append(nm.reshape(w.shape))
        out_v.append(nv.reshape(w.shape))

    return (loss, grad_x.reshape(b, s, d), *out_g, *out_d, *out_m, *out_v)
```

```python
import jax
import jax.numpy as jnp
from jax import lax
from jax.experimental import pallas as pl
from jax.experimental.pallas import tpu as pltpu

F32 = jnp.float32
BF16 = jnp.bfloat16
HIGHEST = lax.Precision.HIGHEST
MESH = pl.DeviceIdType.MESH

D_MODEL = 1024
SB_HEADS = 16
HEAD_DIM = 64
SSD_WIDTH = 2048
SSD_GROUPS = 4
GROUP_WIDTH = SSD_WIDTH // SSD_GROUPS
HEADS_PER_GROUP = 8
SSD_STATE = 128
CHUNK = 128
CONV_K = 4
CONV_DIM = 3072
D_PROJ = 11296
EPS = 1e-6
ADAM_LR, ADAM_B1, ADAM_B2, ADAM_EPS, ADAM_WD, ADAM_STEP = 0.001, 0.9, 0.999, 1e-08, 0.01, 10

LANES = 128
Q0, K0, V0, ZA0, ZS0, XBC0, GATE0, DT0 = 0, 1024, 2048, 3072, 4096, 6144, 9216, 11264
DT_PAD = 256
NP = DT0 + DT_PAD
N_CHIPS = 4
VMEM_LIMIT = 56 * 1024 * 1024


def _cparams(*sem):
    return pltpu.CompilerParams(dimension_semantics=sem or None, vmem_limit_bytes=VMEM_LIMIT)


def _sigmoid(z):
    return 1.0 / (1.0 + jnp.exp(-z))


def _dot(a, b, dims, precision=None):
    return lax.dot_general(a, b, (dims, ((), ())), preferred_element_type=F32, precision=precision)


NN = ((1,), (0,))
NT = ((1,), (1,))
TN = ((0,), (0,))


def _matmul(a, b, *, ta=False, tb=False, out_dtype=F32, tm, tn, tk, name):
    m, k = (a.shape[1], a.shape[0]) if ta else a.shape
    n = b.shape[0] if tb else b.shape[1]
    assert m % tm == 0 and n % tn == 0 and k % tk == 0, (name, m, n, k)
    nk = k // tk
    use_scratch = out_dtype != F32
    dims = ((0,) if ta else (1,), (1,) if tb else (0,))

    def kern(a_ref, b_ref, o_ref, *scratch):
        acc = scratch[0] if use_scratch else o_ref
        kk = pl.program_id(2)

        @pl.when(kk == 0)
        def _():
            acc[...] = jnp.zeros_like(acc)

        acc[...] += _dot(a_ref[...], b_ref[...], dims)
        if use_scratch:
            @pl.when(kk == nk - 1)
            def _():
                o_ref[...] = acc[...].astype(out_dtype)

    a_spec = pl.BlockSpec((tk, tm), lambda i, j, q: (q, i)) if ta else pl.BlockSpec((tm, tk), lambda i, j, q: (i, q))
    b_spec = pl.BlockSpec((tn, tk), lambda i, j, q: (j, q)) if tb else pl.BlockSpec((tk, tn), lambda i, j, q: (q, j))
    return pl.pallas_call(
        kern, name=name,
        out_shape=jax.ShapeDtypeStruct((m, n), out_dtype),
        grid=(m // tm, n // tn, nk),
        in_specs=[a_spec, b_spec],
        out_specs=pl.BlockSpec((tm, tn), lambda i, j, q: (i, j)),
        scratch_shapes=[pltpu.VMEM((tm, tn), F32)] if use_scratch else [],
        compiler_params=_cparams("parallel", "parallel", "arbitrary"),
    )(a, b)


ROWS = 256


def _rms_fwd(x2, w):
    t, d = x2.shape

    def kern(x_ref, w_ref, h_ref):
        x = x_ref[...]
        r = lax.rsqrt(jnp.mean(x * x, axis=-1, keepdims=True) + EPS)
        h_ref[...] = (x * r * w_ref[...]).astype(BF16)

    return pl.pallas_call(
        kern, name="rms_fwd",
        out_shape=jax.ShapeDtypeStruct((t, d), BF16),
        grid=(t // ROWS,),
        in_specs=[pl.BlockSpec((ROWS, d), lambda i: (i, 0)), pl.BlockSpec((1, d), lambda i: (0, 0))],
        out_specs=pl.BlockSpec((ROWS, d), lambda i: (i, 0)),
        compiler_params=_cparams("parallel"),
    )(x2, w)


def _rms_bwd(dh, x2, w, dout):
    t, d = x2.shape

    def kern(dh_ref, x_ref, w_ref, dout_ref, gx_ref, dw_ref):
        @pl.when(pl.program_id(0) == 0)
        def _():
            dw_ref[...] = jnp.zeros_like(dw_ref)

        x = x_ref[...]
        r = lax.rsqrt(jnp.mean(x * x, axis=-1, keepdims=True) + EPS)
        xh = x * r
        g = dh_ref[...]
        dw_ref[...] += jnp.sum(g * xh, axis=0, keepdims=True)
        gw = g * w_ref[...]
        gx_ref[...] = dout_ref[...] + r * (gw - xh * jnp.mean(gw * xh, axis=-1, keepdims=True))

    row = pl.BlockSpec((ROWS, d), lambda i: (i, 0))
    vec = pl.BlockSpec((1, d), lambda i: (0, 0))
    return pl.pallas_call(
        kern, name="rms_bwd",
        out_shape=(jax.ShapeDtypeStruct((t, d), F32), jax.ShapeDtypeStruct((1, d), F32)),
        grid=(t // ROWS,),
        in_specs=[row, row, vec, row],
        out_specs=(row, vec),
        compiler_params=_cparams("arbitrary"),
    )(dh, x2, w, dout)


def _final_fwd_bwd(x2, mo, target, w):
    t, d = x2.shape

    def kern(x_ref, mo_ref, t_ref, w_ref, dout_ref, doutb_ref, loss_ref, dw_ref):
        @pl.when(pl.program_id(0) == 0)
        def _():
            loss_ref[...] = jnp.zeros_like(loss_ref)
            dw_ref[...] = jnp.zeros_like(dw_ref)

        u = x_ref[...] + mo_ref[...]
        r = lax.rsqrt(jnp.mean(u * u, axis=-1, keepdims=True) + EPS)
        uh = u * r
        wv = w_ref[...]
        err = uh * wv - t_ref[...]
        loss_ref[...] += (0.5 / d) * jnp.sum(err * err)
        dy = err * (1.0 / d)
        dw_ref[...] += jnp.sum(dy * uh, axis=0, keepdims=True)
        gw = dy * wv
        du = r * (gw - uh * jnp.mean(gw * uh, axis=-1, keepdims=True))
        dout_ref[...] = du
        doutb_ref[...] = du.astype(BF16)

    row = pl.BlockSpec((ROWS, d), lambda i: (i, 0))
    vec = pl.BlockSpec((1, d), lambda i: (0, 0))
    return pl.pallas_call(
        kern, name="final_fwd_bwd",
        out_shape=(jax.ShapeDtypeStruct((t, d), F32), jax.ShapeDtypeStruct((t, d), BF16),
                   jax.ShapeDtypeStruct((1, LANES), F32), jax.ShapeDtypeStruct((1, d), F32)),
        grid=(t // ROWS,),
        in_specs=[row, row, row, vec],
        out_specs=(row, row, pl.BlockSpec((1, LANES), lambda i: (0, 0)), vec),
        compiler_params=_cparams("arbitrary"),
    )(x2, mo, target, w)


def _merge_fwd(proj2, ya, ys):
    t = ya.shape[0]
    gblk = GATE0 // D_MODEL

    def kern(ga_ref, gs_ref, ya_ref, ys_ref, o_ref):
        o_ref[...] = (_sigmoid(ga_ref[...]) * ya_ref[...] + _sigmoid(gs_ref[...]) * ys_ref[...]).astype(BF16)

    row = pl.BlockSpec((ROWS, D_MODEL), lambda i: (i, 0))
    return pl.pallas_call(
        kern, name="merge_fwd",
        out_shape=jax.ShapeDtypeStruct((t, D_MODEL), BF16),
        grid=(t // ROWS,),
        in_specs=[pl.BlockSpec((ROWS, D_MODEL), lambda i: (i, gblk)),
                  pl.BlockSpec((ROWS, D_MODEL), lambda i: (i, gblk + 1)), row, row],
        out_specs=row,
        compiler_params=_cparams("parallel"),
    )(proj2, proj2, ya, ys)


def _merge_bwd(dm, proj2, ya, ys):
    t = ya.shape[0]
    gblk = GATE0 // D_MODEL

    def kern(dm_ref, ga_ref, gs_ref, ya_ref, ys_ref, dya_ref, dys_ref, dg_ref):
        g = dm_ref[...]
        sa = _sigmoid(ga_ref[...])
        ss = _sigmoid(gs_ref[...])
        dya_ref[...] = (g * sa).astype(BF16)
        dys_ref[...] = (g * ss).astype(BF16)
        dg_ref[:, :D_MODEL] = (g * ya_ref[...] * sa * (1.0 - sa)).astype(BF16)
        dg_ref[:, D_MODEL:] = (g * ys_ref[...] * ss * (1.0 - ss)).astype(BF16)

    row = pl.BlockSpec((ROWS, D_MODEL), lambda i: (i, 0))
    return pl.pallas_call(
        kern, name="merge_bwd",
        out_shape=(jax.ShapeDtypeStruct((t, D_MODEL), BF16), jax.ShapeDtypeStruct((t, D_MODEL), BF16),
                   jax.ShapeDtypeStruct((t, 2 * D_MODEL), BF16)),
        grid=(t // ROWS,),
        in_specs=[row, pl.BlockSpec((ROWS, D_MODEL), lambda i: (i, gblk)),
                  pl.BlockSpec((ROWS, D_MODEL), lambda i: (i, gblk + 1)), row, row],
        out_specs=(row, row, pl.BlockSpec((ROWS, 2 * D_MODEL), lambda i: (i, 0))),
        compiler_params=_cparams("parallel"),
    )(dm, proj2, proj2, ya, ys)


TQ = 256
TK = 256
HEAD_LANES = (slice(0, HEAD_DIM), slice(HEAD_DIM, 2 * HEAD_DIM))


def _tri(pred):
    r = lax.broadcasted_iota(jnp.int32, (TK, TK), 0)
    c = lax.broadcasted_iota(jnp.int32, (TK, TK), 1)
    return pred(r, c).astype(BF16)


def _split_bf16(v):
    hi = v.astype(BF16)
    lo = (v - hi.astype(F32)).astype(BF16)
    return hi, lo


def _tri_dot(v, tri):
    hi, lo = _split_bf16(v)
    return _dot(hi, tri, NN) + _dot(lo, tri, NN)


def _sb_scores(q, k, tri_gt, carry_r, mask):
    z = _dot(q, k, NT)
    l1p = jnp.log(1.0 + jnp.exp(-jnp.abs(z)))
    lb = jnp.minimum(z, 0.0) - l1p
    lom = -jnp.maximum(z, 0.0) - l1p
    if mask is not None:
        lom = jnp.where(mask, lom, 0.0)
    later = _tri_dot(lom, tri_gt) + carry_r
    a = jnp.exp(lb + later)
    if mask is not None:
        a = jnp.where(mask, a, 0.0)
    return lb, lom, a


def _attn_fwd(proj3):
    b, s, _ = proj3.shape
    nq = s // TQ
    scale = HEAD_DIM ** -0.5

    def kern(q_ref, k_ref, v_ref, za_ref, o_ref, yp_ref, qs, ks, vs):
        qs[...] = (q_ref[0] * scale).astype(BF16)
        ks[...] = k_ref[0].astype(BF16)
        vs[...] = v_ref[0].astype(BF16)
        row = lax.broadcasted_iota(jnp.int32, (TQ, TK), 0)
        col = lax.broadcasted_iota(jnp.int32, (TQ, TK), 1)
        tri_gt = _tri(lambda j, sk: j > sk)

        def q_block(i, _):
            r0 = pl.multiple_of(i * TQ, TQ)
            n_kb = (r0 + TQ + TK - 1) // TK
            qh = [qs[pl.ds(r0, TQ), lanes] for lanes in HEAD_LANES]

            def k_block(c0, carry, mask):
                out = []
                for (carry_r, acc), q, lanes in zip(carry, qh, HEAD_LANES):
                    k = ks[pl.ds(c0, TK), lanes]
                    v = vs[pl.ds(c0, TK), lanes]
                    _, lom, a = _sb_scores(q, k, tri_gt, carry_r, mask)
                    out.append((carry_r + jnp.sum(lom, axis=1, keepdims=True), acc + _dot(a.astype(BF16), v, NN)))
                return tuple(out)

            c_last = pl.multiple_of((n_kb - 1) * TK, TK)
            start = (jnp.zeros((TQ, 1), F32), jnp.zeros((TQ, HEAD_DIM), F32))
            carry = k_block(c_last, (start, start), col + c_last < row + r0)

            def unmasked(jj, carry):
                return k_block(pl.multiple_of((n_kb - 2 - jj) * TK, TK), carry, None)

            carry = lax.fori_loop(0, n_kb - 1, unmasked, carry)
            for (_, acc), lanes in zip(carry, HEAD_LANES):
                o_ref[0, pl.ds(r0, TQ), lanes] = acc
                za = za_ref[0, pl.ds(r0, TQ), lanes]
                yp_ref[0, pl.ds(r0, TQ), lanes] = (acc * (za * _sigmoid(za))).astype(BF16)
            return 0

        lax.fori_loop(0, nq, q_block, 0)

    def spec(c0):
        return pl.BlockSpec((1, s, LANES), lambda bi, hp: (bi, 0, c0 // LANES + hp))

    out_spec = pl.BlockSpec((1, s, LANES), lambda bi, hp: (bi, 0, hp))
    return pl.pallas_call(
        kern, name="attn_fwd",
        out_shape=(jax.ShapeDtypeStruct((b, s, D_MODEL), F32), jax.ShapeDtypeStruct((b, s, D_MODEL), BF16)),
        grid=(b, SB_HEADS // 2),
        in_specs=[spec(Q0), spec(K0), spec(V0), spec(ZA0)],
        out_specs=(out_spec, out_spec),
        scratch_shapes=[pltpu.VMEM((s, LANES), BF16)] * 3,
        compiler_params=_cparams("parallel", "parallel"),
    )(proj3, proj3, proj3, proj3)


def _attn_bwd(proj3, dyp3, o3):
    b, s, _ = proj3.shape
    nq = s // TQ
    scale = HEAD_DIM ** -0.5

    def kern(q_ref, k_ref, v_ref, za_ref, dyp_ref, o_ref, dq_ref, dk_ref, dv_ref, dza_ref,
             qs, ks, vs, dos, dk_acc, dv_acc):
        qs[...] = (q_ref[0] * scale).astype(BF16)
        ks[...] = k_ref[0].astype(BF16)
        vs[...] = v_ref[0].astype(BF16)
        za = za_ref[0]
        sg = _sigmoid(za)
        dyp = dyp_ref[0]
        dos[...] = (dyp * (za * sg)).astype(BF16)
        dza_ref[0] = (dyp * o_ref[0] * (sg * (1.0 + za * (1.0 - sg)))).astype(BF16)
        dk_acc[...] = jnp.zeros_like(dk_acc)
        dv_acc[...] = jnp.zeros_like(dv_acc)
        row = lax.broadcasted_iota(jnp.int32, (TQ, TK), 0)
        col = lax.broadcasted_iota(jnp.int32, (TQ, TK), 1)
        tri_gt = _tri(lambda j, sk: j > sk)
        tri_ge = _tri(lambda j, sk: j >= sk)

        def q_block(i, _):
            r0 = pl.multiple_of(i * TQ, TQ)
            n_kb = (r0 + TQ + TK - 1) // TK
            qh = [qs[pl.ds(r0, TQ), lanes] for lanes in HEAD_LANES]
            doh = [dos[pl.ds(r0, TQ), lanes] for lanes in HEAD_LANES]
            totals = [jnp.sum(do.astype(F32) * o_ref[0, pl.ds(r0, TQ), lanes], axis=1, keepdims=True)
                      for do, lanes in zip(doh, HEAD_LANES)]

            def k_block(c0, carry, mask):
                out = []
                for (carry_r, carry_g, dq), q, do, total, lanes in zip(carry, qh, doh, totals, HEAD_LANES):
                    k = ks[pl.ds(c0, TK), lanes]
                    v = vs[pl.ds(c0, TK), lanes]
                    lb, lom, a = _sb_scores(q, k, tri_gt, carry_r, mask)
                    ab = a.astype(BF16)
                    g = _dot(do, v, NT) * ab.astype(F32)
                    suffix = _tri_dot(g, tri_ge) + carry_g
                    sig = jnp.exp(lb)
                    dz = g - (g + total - suffix) * sig
                    if mask is not None:
                        dz = jnp.where(mask, dz, 0.0)
                    dzb = dz.astype(BF16)
                    dk_acc[pl.ds(c0, TK), lanes] += _dot(dzb, q, TN)
                    dv_acc[pl.ds(c0, TK), lanes] += _dot(ab, do, TN)
                    out.append((carry_r + jnp.sum(lom, axis=1, keepdims=True),
                                carry_g + jnp.sum(g, axis=1, keepdims=True), dq + _dot(dzb, k, NN)))
                return tuple(out)

            c_last = pl.multiple_of((n_kb - 1) * TK, TK)
            zero = jnp.zeros((TQ, 1), F32)
            start = (zero, zero, jnp.zeros((TQ, HEAD_DIM), F32))
            carry = k_block(c_last, (start, start), col + c_last < row + r0)

            def unmasked(jj, carry):
                return k_block(pl.multiple_of((n_kb - 2 - jj) * TK, TK), carry, None)

            carry = lax.fori_loop(0, n_kb - 1, unmasked, carry)
            for (_, _, dq), lanes in zip(carry, HEAD_LANES):
                dq_ref[0, pl.ds(r0, TQ), lanes] = (dq * scale).astype(BF16)
            return 0

        lax.fori_loop(0, nq, q_block, 0)

        dk_ref[0] = dk_acc[...].astype(BF16)
        dv_ref[0] = dv_acc[...].astype(BF16)

    def spec(c0):
        return pl.BlockSpec((1, s, LANES), lambda bi, hp: (bi, 0, c0 // LANES + hp))

    plain = pl.BlockSpec((1, s, LANES), lambda bi, hp: (bi, 0, hp))
    out = jax.ShapeDtypeStruct((b, s, D_MODEL), BF16)
    return pl.pallas_call(
        kern, name="attn_bwd",
        out_shape=(out, out, out, out),
        grid=(b, SB_HEADS // 2),
        in_specs=[spec(Q0), spec(K0), spec(V0), spec(ZA0), plain, plain],
        out_specs=(plain, plain, plain, plain),
        scratch_shapes=[pltpu.VMEM((s, LANES), BF16)] * 4 + [pltpu.VMEM((s, LANES), F32)] * 2,
        compiler_params=_cparams("parallel", "parallel"),
    )(proj3, proj3, proj3, proj3, dyp3, o3)


CONV_COLS = 256
HALO = 8


def _conv_pre(xp, w_ref, b_ref, r0):
    pre = b_ref[...] + w_ref[CONV_K - 1:CONV_K, :] * xp[pl.ds(HALO + r0, CHUNK), :]
    for kk in range(1, CONV_K):
        pre = pre + w_ref[CONV_K - 1 - kk:CONV_K - kk, :] * xp[pl.ds(HALO + r0 - kk, CHUNK), :]
    return pre


def _conv_fwd(proj3, conv_w, conv_b):
    b, s, _ = proj3.shape
    nc = s // CHUNK

    def kern(x_ref, w_ref, b_ref, o_ref, xp):
        xp[0:HALO, :] = jnp.zeros((HALO, CONV_COLS), F32)
        xp[HALO:, :] = x_ref[0]
        for ci in range(nc):
            pre = _conv_pre(xp, w_ref, b_ref, ci * CHUNK)
            o_ref[0, ci * CHUNK:(ci + 1) * CHUNK, :] = pre * _sigmoid(pre)

    return pl.pallas_call(
        kern, name="conv_fwd",
        out_shape=jax.ShapeDtypeStruct((b, s, CONV_DIM), F32),
        grid=(CONV_DIM // CONV_COLS, b),
        in_specs=[pl.BlockSpec((1, s, CONV_COLS), lambda j, bi: (bi, 0, XBC0 // CONV_COLS + j)),
                  pl.BlockSpec((CONV_K, CONV_COLS), lambda j, bi: (0, j)),
                  pl.BlockSpec((1, CONV_COLS), lambda j, bi: (0, j))],
        out_specs=pl.BlockSpec((1, s, CONV_COLS), lambda j, bi: (bi, 0, j)),
        scratch_shapes=[pltpu.VMEM((s + HALO, CONV_COLS), F32)],
        compiler_params=_cparams("parallel", "parallel"),
    )(proj3, conv_w, conv_b)


def _conv_bwd(dact, proj3, conv_w, conv_b, col0, name):
    b, s, width = dact.shape
    nc = s // CHUNK
    j0 = col0 // CONV_COLS

    def kern(da_ref, x_ref, w_ref, b_ref, dx_ref, dw_ref, db_ref, xp, dp):
        @pl.when(pl.program_id(1) == 0)
        def _():
            dw_ref[...] = jnp.zeros_like(dw_ref)
            db_ref[...] = jnp.zeros_like(db_ref)

        xp[0:HALO, :] = jnp.zeros((HALO, CONV_COLS), F32)
        xp[HALO:, :] = x_ref[0]
        dp[s:, :] = jnp.zeros((HALO, CONV_COLS), F32)
        for ci in range(nc):
            r0 = ci * CHUNK
            pre = _conv_pre(xp, w_ref, b_ref, r0)
            sg = _sigmoid(pre)
            dpre = da_ref[0, r0:r0 + CHUNK, :] * (sg * (1.0 + pre * (1.0 - sg)))
            dp[r0:r0 + CHUNK, :] = dpre
            db_ref[...] += jnp.sum(dpre, axis=0, keepdims=True)
            for kk in range(CONV_K):
                tap = CONV_K - 1 - kk
                dw_ref[tap:tap + 1, :] += jnp.sum(dpre * xp[pl.ds(HALO + r0 - kk, CHUNK), :], axis=0, keepdims=True)
        for ci in range(nc):
            r0 = ci * CHUNK
            dx = w_ref[CONV_K - 1:CONV_K, :] * dp[pl.ds(r0, CHUNK), :]
            for kk in range(1, CONV_K):
                dx = dx + w_ref[CONV_K - 1 - kk:CONV_K - kk, :] * dp[pl.ds(r0 + kk, CHUNK), :]
            dx_ref[0, r0:r0 + CHUNK, :] = dx.astype(BF16)

    return pl.pallas_call(
        kern, name=name,
        out_shape=(jax.ShapeDtypeStruct((b, s, width), BF16), jax.ShapeDtypeStruct((CONV_K, width), F32),
                   jax.ShapeDtypeStruct((1, width), F32)),
        grid=(width // CONV_COLS, b),
        in_specs=[pl.BlockSpec((1, s, CONV_COLS), lambda j, bi: (bi, 0, j)),
                  pl.BlockSpec((1, s, CONV_COLS), lambda j, bi: (bi, 0, XBC0 // CONV_COLS + j0 + j)),
                  pl.BlockSpec((CONV_K, CONV_COLS), lambda j, bi: (0, j0 + j)),
                  pl.BlockSpec((1, CONV_COLS), lambda j, bi: (0, j0 + j))],
        out_specs=(pl.BlockSpec((1, s, CONV_COLS), lambda j, bi: (bi, 0, j)),
                   pl.BlockSpec((CONV_K, CONV_COLS), lambda j, bi: (0, j)),
                   pl.BlockSpec((1, CONV_COLS), lambda j, bi: (0, j))),
        scratch_shapes=[pltpu.VMEM((s + HALO, CONV_COLS), F32)] * 2,
        compiler_params=_cparams("parallel", "arbitrary"),
    )(dact, proj3, conv_w, conv_b)


def _ssd_common(dtr_ref, dtb_ref, alog_ref):
    lane = lax.broadcasted_iota(jnp.int32, (CHUNK, LANES), 1)
    row = lax.broadcasted_iota(jnp.int32, (CHUNK, LANES), 0)
    head_lane = lane < HEADS_PER_GROUP
    pre = dtr_ref[0, 0] + dtb_ref[0]
    dt = jnp.where(head_lane, jnp.maximum(pre, 0.0) + jnp.log(1.0 + jnp.exp(-jnp.abs(pre))), 0.0)
    a = jnp.where(head_lane[0:1], -jnp.exp(alog_ref[0]), 0.0)
    tril = (row >= lane).astype(F32)
    acs = _dot(tril, dt * a, NN, HIGHEST)
    acs_t = acs.T
    er = lax.broadcasted_iota(jnp.int32, (LANES, GROUP_WIDTH), 0)
    ec = lax.broadcasted_iota(jnp.int32, (LANES, GROUP_WIDTH), 1)
    expand = ((ec // HEAD_DIM) == er).astype(F32)
    tr = lax.broadcasted_iota(jnp.int32, (GROUP_WIDTH, LANES), 0)
    tc = lax.broadcasted_iota(jnp.int32, (GROUP_WIDTH, LANES), 1)
    reduce = ((tr // HEAD_DIM) == tc).astype(F32)
    dt_x = _dot(dt, expand, NN, HIGHEST)
    acs_x = _dot(acs, expand, NN, HIGHEST)
    end_x = acs_x[CHUNK - 1:CHUNK, :]
    end_col = jnp.broadcast_to(acs_t[:, CHUNK - 1:CHUNK], (LANES, LANES))
    chunk_decay = jnp.exp(_dot(reduce, end_col, NN, HIGHEST))
    causal = row >= lane
    return dict(dt=dt, a=a, pre=pre, head_lane=head_lane, acs=acs, acs_t=acs_t, expand=expand, reduce=reduce,
                dt_x=dt_x, acs_x=acs_x, end_x=end_x, chunk_decay=chunk_decay, causal=causal, row=row, lane=lane)


def _ssd_decay(cm, h):
    seg = cm["acs"][:, h:h + 1] - cm["acs_t"][h:h + 1, :]
    return jnp.where(cm["causal"], jnp.exp(jnp.minimum(seg, 0.0)), 0.0)


def _ssd_fwd(xact, proj3, dtr_g, dtb_g, alog_g, dskip_x, snw):
    b, s, _ = xact.shape
    nc = s // CHUNK
    g4 = SSD_GROUPS

    def kern(xs_ref, bm_ref, cm_ref, zs_ref, dtr_ref, dtb_ref, alog_ref, dsk_ref, snw_ref,
             y_ref, yn_ref, hst_ref, h_sc):
        @pl.when(pl.program_id(2) == 0)
        def _():
            h_sc[...] = jnp.zeros_like(h_sc)

        cm = _ssd_common(dtr_ref, dtb_ref, alog_ref)
        x = xs_ref[0]
        bmb = bm_ref[0].astype(BF16)
        cmb = cm_ref[0].astype(BF16)
        h_in = h_sc[...]
        hst_ref[0, 0, 0] = h_in
        xdt = x * cm["dt_x"]
        xdtb = xdt.astype(BF16)
        cb = _dot(cmb, bmb, NT)
        y_off = _dot(cmb, h_in.astype(BF16), NT) * jnp.exp(cm["acs_x"])
        for h in range(HEADS_PER_GROUP):
            lanes = slice(h * HEAD_DIM, (h + 1) * HEAD_DIM)
            m = (cb * _ssd_decay(cm, h)).astype(BF16)
            y_ref[0, :, lanes] = _dot(m, xdtb[:, lanes], NN)
        y = y_ref[0] + y_off + x * dsk_ref[...]
        y_ref[0] = y
        w = (xdt * jnp.exp(cm["end_x"] - cm["acs_x"])).astype(BF16)
        h_sc[...] = h_in * cm["chunk_decay"] + _dot(w, bmb, TN)
        zs = zs_ref[0]
        y2 = y * (zs * _sigmoid(zs))
        yn_ref[0] = (y2 * lax.rsqrt(jnp.mean(y2 * y2, axis=-1, keepdims=True) + EPS) * snw_ref[...]).astype(BF16)

    gw = GROUP_WIDTH
    small = pl.BlockSpec((1, 1, LANES), lambda gi, bi, ci: (gi, 0, 0))
    xblk = pl.BlockSpec((1, CHUNK, gw), lambda gi, bi, ci: (bi, ci, gi))
    return pl.pallas_call(
        kern, name="ssd_fwd",
        out_shape=(jax.ShapeDtypeStruct((b, s, SSD_WIDTH), F32), jax.ShapeDtypeStruct((b, s, SSD_WIDTH), BF16),
                   jax.ShapeDtypeStruct((b, nc, g4, gw, SSD_STATE), F32)),
        grid=(g4, b, nc),
        in_specs=[xblk,
                  pl.BlockSpec((1, CHUNK, LANES), lambda gi, bi, ci: (bi, ci, SSD_WIDTH // LANES + gi)),
                  pl.BlockSpec((1, CHUNK, LANES), lambda gi, bi, ci: (bi, ci, SSD_WIDTH // LANES + g4 + gi)),
                  pl.BlockSpec((1, CHUNK, gw), lambda gi, bi, ci: (bi, ci, ZS0 // gw + gi)),
                  pl.BlockSpec((1, 1, CHUNK, LANES), lambda gi, bi, ci: (bi, gi, ci, 0)),
                  small, small,
                  pl.BlockSpec((1, gw), lambda gi, bi, ci: (0, gi)),
                  pl.BlockSpec((1, gw), lambda gi, bi, ci: (0, gi))],
        out_specs=(xblk, xblk, pl.BlockSpec((1, 1, 1, gw, SSD_STATE), lambda gi, bi, ci: (bi, ci, gi, 0, 0))),
        scratch_shapes=[pltpu.VMEM((gw, SSD_STATE), F32)],
        compiler_params=_cparams("parallel", "parallel", "arbitrary"),
    )(xact, xact, xact, proj3, dtr_g, dtb_g, alog_g, dskip_x, snw)


def _ssd_bwd(dyn3, y3, xact, proj3, hst, dtr_g, dtb_g, alog_g, dskip_x, snw):
    b, s, _ = xact.shape
    nc = s // CHUNK
    g4 = SSD_GROUPS
    gw = GROUP_WIDTH

    def kern(dyn_ref, y_ref, xs_ref, bm_ref, cm_ref, zs_ref, hst_ref, dtr_ref, dtb_ref, alog_ref, dsk_ref, snw_ref,
             dxs_ref, dbm_ref, dcm_ref, dzs_ref, ddtr_ref, dsnw_ref, dalog_ref, ddtb_ref, ddsk_ref, dh_sc):
        first = jnp.logical_and(pl.program_id(1) == 0, pl.program_id(2) == 0)

        @pl.when(first)
        def _():
            dsnw_ref[...] = jnp.zeros_like(dsnw_ref)
            dalog_ref[...] = jnp.zeros_like(dalog_ref)
            ddtb_ref[...] = jnp.zeros_like(ddtb_ref)
            ddsk_ref[...] = jnp.zeros_like(ddsk_ref)

        @pl.when(pl.program_id(2) == 0)
        def _():
            dh_sc[...] = jnp.zeros_like(dh_sc)

        cm = _ssd_common(dtr_ref, dtb_ref, alog_ref)
        row, lane = cm["row"], cm["lane"]
        y = y_ref[0]
        zs = zs_ref[0]
        sg = _sigmoid(zs)
        silu = zs * sg
        y2 = y * silu
        rstd = lax.rsqrt(jnp.mean(y2 * y2, axis=-1, keepdims=True) + EPS)
        y2h = y2 * rstd
        dyn = dyn_ref[0]
        dsnw_ref[0] += jnp.sum(dyn * y2h, axis=0, keepdims=True)
        gwv = dyn * snw_ref[...]
        dy2 = rstd * (gwv - y2h * jnp.mean(gwv * y2h, axis=-1, keepdims=True))
        dzs_ref[0] = (dy2 * y * (sg * (1.0 + zs * (1.0 - sg)))).astype(BF16)
        dy = dy2 * silu
        dyb = dy.astype(BF16)

        x = xs_ref[0]
        bmb = bm_ref[0].astype(BF16)
        cmb = cm_ref[0].astype(BF16)
        h_in = hst_ref[0, 0, 0]
        h_inb = h_in.astype(BF16)
        d_hn = dh_sc[...]
        d_hnb = d_hn.astype(BF16)
        xdt = x * cm["dt_x"]
        xdtb = xdt.astype(BF16)
        eacs = jnp.exp(cm["acs_x"])
        dte = jnp.exp(cm["end_x"] - cm["acs_x"])
        wb = (xdt * dte).astype(BF16)

        dsk_lanes = jnp.broadcast_to(jnp.sum(dy * x, axis=0, keepdims=True), (8, gw))
        ddsk_ref[0] += _dot(dsk_lanes, cm["reduce"], NN, HIGHEST)[0:1, :]
        dyo = dy * eacs
        dyob = dyo.astype(BF16)
        dacs_x = dyo * _dot(cmb, h_inb, NT)
        dcm = _dot(dyob, h_inb, NN)
        dh_in = _dot(dyob, cmb, TN)
        dw = _dot(bmb, d_hnb, NT)
        dbm = _dot(wb, d_hnb, NN)
        dxdt = dw * dte
        e_l = dw * xdt * dte
        dacs_x = dacs_x - e_l
        dend_x = jnp.sum(e_l, axis=0, keepdims=True)
        dh_sc[...] = d_hn * cm["chunk_decay"] + dh_in
        q = d_hn * h_in * cm["chunk_decay"]
        dend_x = dend_x + _dot(jnp.ones((8, SSD_STATE), F32), q, NT, HIGHEST)[0:1, :]
        last_row = lax.broadcasted_iota(jnp.int32, (CHUNK, gw), 0) == CHUNK - 1
        dacs_x = dacs_x + jnp.where(last_row, dend_x, 0.0)

        cb = _dot(cmb, bmb, NT)
        dcb = jnp.zeros((CHUNK, CHUNK), F32)
        dacs = jnp.zeros((CHUNK, LANES), F32)
        dacs_t = jnp.zeros((LANES, CHUNK), F32)
        for h in range(HEADS_PER_GROUP):
            lanes = slice(h * HEAD_DIM, (h + 1) * HEAD_DIM)
            decay = _ssd_decay(cm, h)
            m = cb * decay
            dm = _dot(dyb[:, lanes], xdtb[:, lanes], NT)
            dxs_ref[0, :, lanes] = _dot(m.astype(BF16), dyb[:, lanes], TN)
            dcb_h = dm * decay
            dcb = dcb + dcb_h
            n = dcb_h * cb
            dacs = dacs + jnp.where(lane == h, jnp.sum(n, axis=1, keepdims=True), 0.0)
            dacs_t = dacs_t + jnp.where(row == h, jnp.sum(n, axis=0, keepdims=True), 0.0)
        dcbb = dcb.astype(BF16)
        dcm_ref[0] = dcm + _dot(dcbb, bmb, NN)
        dbm_ref[0] = dbm + _dot(dcbb, cmb, TN)
        dxdt = dxdt + dxs_ref[0]
        dxs_ref[0] = dy * dsk_ref[...] + dxdt * cm["dt_x"]

        dacs = dacs - dacs_t.T + _dot(dacs_x, cm["reduce"], NN, HIGHEST)
        ddt = _dot(dxdt * x, cm["reduce"], NN, HIGHEST)
        triu = (row <= lane).astype(F32)
        rc = _dot(triu, dacs, NN, HIGHEST)
        ddt = ddt + cm["a"] * rc
        dalog_ref[0] += jnp.sum(cm["dt"] * rc, axis=0, keepdims=True) * cm["a"]
        ddtr = jnp.where(cm["head_lane"], ddt * _sigmoid(cm["pre"]), 0.0)
        ddtr_ref[0, 0] = ddtr
        ddtb_ref[0] += jnp.sum(ddtr, axis=0, keepdims=True)

    def rev(ci):
        return nc - 1 - ci

    small = pl.BlockSpec((1, 1, LANES), lambda gi, bi, ci: (gi, 0, 0))
    xblk = pl.BlockSpec((1, CHUNK, gw), lambda gi, bi, ci: (bi, rev(ci), gi))
    nblk = pl.BlockSpec((1, CHUNK, LANES), lambda gi, bi, ci: (bi, rev(ci), gi))
    gvec = pl.BlockSpec((1, gw), lambda gi, bi, ci: (0, gi))
    gacc = pl.BlockSpec((1, 1, gw), lambda gi, bi, ci: (gi, 0, 0))
    return pl.pallas_call(
        kern, name="ssd_bwd",
        out_shape=(jax.ShapeDtypeStruct((b, s, SSD_WIDTH), F32),
                   jax.ShapeDtypeStruct((b, s, g4 * SSD_STATE), F32),
                   jax.ShapeDtypeStruct((b, s, g4 * SSD_STATE), F32),
                   jax.ShapeDtypeStruct((b, s, SSD_WIDTH), BF16),
                   jax.ShapeDtypeStruct((b, g4, s, LANES), F32),
                   jax.ShapeDtypeStruct((g4, 1, gw), F32),
                   jax.ShapeDtypeStruct((g4, 1, LANES), F32),
                   jax.ShapeDtypeStruct((g4, 1, LANES), F32),
                   jax.ShapeDtypeStruct((g4, 1, LANES), F32)),
        grid=(g4, b, nc),
        in_specs=[xblk, xblk, xblk,
                  pl.BlockSpec((1, CHUNK, LANES), lambda gi, bi, ci: (bi, rev(ci), SSD_WIDTH // LANES + gi)),
                  pl.BlockSpec((1, CHUNK, LANES), lambda gi, bi, ci: (bi, rev(ci), SSD_WIDTH // LANES + g4 + gi)),
                  pl.BlockSpec((1, CHUNK, gw), lambda gi, bi, ci: (bi, rev(ci), ZS0 // gw + gi)),
                  pl.BlockSpec((1, 1, 1, gw, SSD_STATE), lambda gi, bi, ci: (bi, rev(ci), gi, 0, 0)),
                  pl.BlockSpec((1, 1, CHUNK, LANES), lambda gi, bi, ci: (bi, gi, rev(ci), 0)),
                  small, small, gvec, gvec],
        out_specs=(xblk, nblk, nblk, xblk,
                   pl.BlockSpec((1, 1, CHUNK, LANES), lambda gi, bi, ci: (bi, gi, rev(ci), 0)),
                   gacc, small, small, small),
        scratch_shapes=[pltpu.VMEM((gw, SSD_STATE), F32)],
        compiler_params=_cparams("parallel", "arbitrary", "arbitrary"),
    )(dyn3, y3, xact, xact, xact, proj3, hst, dtr_g, dtb_g, alog_g, dskip_x, snw)


def _adamw(w, g, m, v, name):
    r, c = w.shape
    tr = 128 if r % 128 == 0 else r

    def kern(w_ref, g_ref, m_ref, v_ref, d_ref, nm_ref, nv_ref):
        gv = g_ref[...]
        nm = ADAM_B1 * m_ref[...] + (1.0 - ADAM_B1) * gv
        nv = ADAM_B2 * v_ref[...] + (1.0 - ADAM_B2) * (gv * gv)
        m_hat = nm / (1.0 - ADAM_B1 ** ADAM_STEP)
        v_hat = nv / (1.0 - ADAM_B2 ** ADAM_STEP)
        d_ref[...] = -ADAM_LR * (m_hat / (jnp.sqrt(v_hat) + ADAM_EPS) + ADAM_WD * w_ref[...])
        nm_ref[...] = nm
        nv_ref[...] = nv

    blk = pl.BlockSpec((tr, c), lambda i: (i, 0))
    out = jax.ShapeDtypeStruct((r, c), F32)
    return pl.pallas_call(
        kern, name=name, out_shape=(out, out, out), grid=(r // tr,),
        in_specs=[blk] * 4, out_specs=(blk, blk, blk),
        compiler_params=_cparams("parallel"),
    )(w, g, m, v)


ANY = pl.BlockSpec(memory_space=pl.ANY)


def _position():
    return lax.axis_index("x"), lax.axis_index("y"), lax.axis_index("c")


def _other_chips(x, y):
    return [(1 - x, y), (x, 1 - y), (1 - x, 1 - y)]


def _gather_weights(pack):
    rows = pack.shape[0]
    half = rows // 2

    def body(p_ref, out_ref, send_sems, recv_sems, local_sem):
        x, y, c = _position()
        me = 2 * x + y
        chips = _other_chips(x, y)

        def slab(chip, hf):
            return out_ref.at[chip, pl.ds(hf * half, half), :]

        mine = pltpu.make_async_copy(p_ref, out_ref.at[me], local_sem)
        mine.start()
        first = [pltpu.make_async_remote_copy(
            src_ref=p_ref.at[pl.ds(c * half, half), :], dst_ref=slab(me, c),
            send_sem=send_sems.at[j], recv_sem=recv_sems.at[j],
            device_id=(px, py, c), device_id_type=MESH) for j, (px, py) in enumerate(chips)]
        for cp in first:
            cp.start()
        passed = [pltpu.make_async_remote_copy(
            src_ref=slab(2 * px + py, c), dst_ref=slab(2 * px + py, c),
            send_sem=send_sems.at[3 + j], recv_sem=recv_sems.at[3 + j],
            device_id=(x, y, 1 - c), device_id_type=MESH) for j, (px, py) in enumerate(chips)]
        for j, (px, py) in enumerate(chips):
            pltpu.make_async_remote_copy(
                src_ref=slab(2 * px + py, c), dst_ref=slab(2 * px + py, c),
                send_sem=send_sems.at[j], recv_sem=recv_sems.at[j],
                device_id=(px, py, c), device_id_type=MESH).wait_recv()
            passed[j].start()
        for j, (px, py) in enumerate(chips):
            pltpu.make_async_remote_copy(
                src_ref=slab(2 * px + py, 1 - c), dst_ref=slab(2 * px + py, 1 - c),
                send_sem=send_sems.at[3 + j], recv_sem=recv_sems.at[3 + j],
                device_id=(x, y, 1 - c), device_id_type=MESH).wait_recv()
        for cp in first + passed:
            cp.wait_send()
        mine.wait()

    return pl.pallas_call(
        body, name="gather_weights",
        out_shape=jax.ShapeDtypeStruct((N_CHIPS, rows, LANES), pack.dtype),
        in_specs=[ANY], out_specs=ANY,
        scratch_shapes=[pltpu.SemaphoreType.DMA((6,)), pltpu.SemaphoreType.DMA((6,)), pltpu.SemaphoreType.DMA],
    )(pack)


def _sibling_swap(v, part, name):
    n, rows, _ = v.shape
    half = rows // 2 if part == "other" else rows

    def body(v_ref, out_ref, send_sem, recv_sem):
        x, y, c = _position()
        start = (1 - c) * half if part == "other" else 0
        cp = pltpu.make_async_remote_copy(
            src_ref=v_ref.at[:, pl.ds(start, half), :], dst_ref=out_ref,
            send_sem=send_sem, recv_sem=recv_sem, device_id=(x, y, 1 - c), device_id_type=MESH)
        cp.start()
        cp.wait()

    return pl.pallas_call(
        body, name=name,
        out_shape=jax.ShapeDtypeStruct((n, half, LANES), v.dtype),
        in_specs=[ANY], out_specs=ANY,
        scratch_shapes=[pltpu.SemaphoreType.DMA, pltpu.SemaphoreType.DMA],
    )(v)


def _chip_all_to_all(p):
    def body(p_ref, out_ref, send_sems, recv_sems, local_sem):
        x, y, c = _position()
        me = 2 * x + y
        chips = _other_chips(x, y)
        mine = pltpu.make_async_copy(p_ref.at[me], out_ref.at[me], local_sem)
        mine.start()
        sends = [pltpu.make_async_remote_copy(
            src_ref=p_ref.at[2 * px + py], dst_ref=out_ref.at[me],
            send_sem=send_sems.at[j], recv_sem=recv_sems.at[j],
            device_id=(px, py, c), device_id_type=MESH) for j, (px, py) in enumerate(chips)]
        for cp in sends:
            cp.start()
        for j, (px, py) in enumerate(chips):
            pltpu.make_async_remote_copy(
                src_ref=p_ref.at[me], dst_ref=out_ref.at[2 * px + py],
                send_sem=send_sems.at[j], recv_sem=recv_sems.at[j],
                device_id=(px, py, c), device_id_type=MESH).wait_recv()
        for cp in sends:
            cp.wait_send()
        mine.wait()

    return pl.pallas_call(
        body, name="grad_all_to_all",
        out_shape=jax.ShapeDtypeStruct(p.shape, p.dtype),
        in_specs=[ANY], out_specs=ANY,
        scratch_shapes=[pltpu.SemaphoreType.DMA((3,)), pltpu.SemaphoreType.DMA((3,)), pltpu.SemaphoreType.DMA],
    )(p)


ADD_ROWS = 1712


def _add_halves(g, sw, core):
    n, rows, _ = g.shape
    half = rows // 2
    nb = half // ADD_ROWS

    def kern(c_ref, g_ref, s_ref, o_ref):
        o_ref[...] = g_ref[...] + s_ref[...]

    blk = pl.BlockSpec((1, ADD_ROWS, LANES), lambda j, i, c_ref: (j, i, 0))
    return pl.pallas_call(
        kern, name="grad_add_halves",
        out_shape=jax.ShapeDtypeStruct((n, half, LANES), F32),
        grid_spec=pltpu.PrefetchScalarGridSpec(
            num_scalar_prefetch=1, grid=(n, nb),
            in_specs=[pl.BlockSpec((1, ADD_ROWS, LANES), lambda j, i, c_ref: (j, c_ref[0] * nb + i, 0)), blk],
            out_specs=blk),
        compiler_params=_cparams("parallel", "parallel"),
    )(core, g, sw)


def _sum_chips(rx):
    n, rows, _ = rx.shape

    def kern(r_ref, o_ref):
        o_ref[0] = ((r_ref[0] + r_ref[1]) + r_ref[2]) + r_ref[3]

    return pl.pallas_call(
        kern, name="grad_sum_chips",
        out_shape=jax.ShapeDtypeStruct((1, rows, LANES), F32),
        grid=(rows // ADD_ROWS,),
        in_specs=[pl.BlockSpec((n, ADD_ROWS, LANES), lambda i: (0, i, 0))],
        out_specs=pl.BlockSpec((1, ADD_ROWS, LANES), lambda i: (0, i, 0)),
        compiler_params=_cparams("parallel"),
    )(rx)


def _gather_small(v, reduce, name):
    rows = v.shape[0]

    def body(v_ref, out_ref, buf, send_sems, recv_sems):
        x, y, c = _position()
        me = 4 * x + 2 * y + c
        buf[me] = v_ref[...]
        peers = [(x ^ (k >> 2), y ^ ((k >> 1) & 1), c ^ (k & 1)) for k in range(1, 8)]
        copies = [pltpu.make_async_remote_copy(
            src_ref=v_ref, dst_ref=buf.at[me],
            send_sem=send_sems.at[k], recv_sem=recv_sems.at[k],
            device_id=peer, device_id_type=MESH) for k, peer in enumerate(peers)]
        for cp in copies:
            cp.start()
        for k, (px, py, pc) in enumerate(peers):
            pltpu.make_async_remote_copy(
                src_ref=v_ref, dst_ref=buf.at[4 * px + 2 * py + pc],
                send_sem=send_sems.at[k], recv_sem=recv_sems.at[k],
                device_id=(px, py, pc), device_id_type=MESH).wait_recv()
        for cp in copies:
            cp.wait_send()
        if reduce:
            total = buf[0]
            for d in range(1, 8):
                total = total + buf[d]
            out_ref[...] = total
        else:
            out_ref[...] = buf[...]

    vm = pl.BlockSpec(memory_space=pltpu.VMEM)
    return pl.pallas_call(
        body, name=name,
        out_shape=jax.ShapeDtypeStruct((rows, LANES) if reduce else (8, rows, LANES), F32),
        in_specs=[vm], out_specs=vm,
        scratch_shapes=[pltpu.VMEM((8, rows, LANES), F32), pltpu.SemaphoreType.DMA((7,)), pltpu.SemaphoreType.DMA((7,))],
    )(v)


def _pad_rows(a, rows):
    return jnp.pad(a, ((0, rows - a.shape[0]), (0, 0)))


def _lane_pad(v):
    n = v.shape[1]
    return jnp.pad(v, ((0, 0), (0, -n % LANES)))


def _gather_all(w_in, w_attn_out, w_ssm_out, w_o, conv_w):
    d = D_MODEL
    shard_cols = w_in.shape[2]
    big = [w_in[0], w_attn_out[0], w_ssm_out[0], w_o[0]]
    big_rows = [a.size // LANES for a in big]
    pack = jnp.concatenate([a.reshape(-1, LANES) for a in big], axis=0).astype(BF16)
    gathered = _gather_weights(pack)
    offs = [0]
    for r in big_rows:
        offs.append(offs[-1] + r)
    w_in_all = gathered[:, offs[0]:offs[1]].reshape(N_CHIPS, d, shard_cols).transpose(1, 0, 2).reshape(d, D_PROJ)
    w_proj = jnp.concatenate([w_in_all[:, :GATE0], w_in_all[:, GATE0 + 32:], w_in_all[:, GATE0:GATE0 + 32],
                              jnp.zeros((d, DT_PAD - 32), BF16)], axis=1)
    w_ao = gathered[:, offs[1]:offs[2]].reshape(D_MODEL, d)
    w_so = gathered[:, offs[2]:offs[3]].reshape(SSD_WIDTH, d)
    w_oo = gathered[:, offs[3]:offs[4]].reshape(D_MODEL, d)
    conv_rows = conv_w[0].size // LANES
    conv_all = _gather_small(conv_w[0].reshape(conv_rows, LANES), False, "gather_conv_w")
    conv_w_all = conv_all[0::2].reshape(N_CHIPS, CONV_K, CONV_DIM // N_CHIPS).transpose(1, 0, 2).reshape(CONV_K, CONV_DIM)

    return w_proj, w_ao, w_so, w_oo, conv_w_all, offs, conv_rows


def _local_step(x, loss_target, norm_w, w_proj, conv_w_all, conv_b, dt_bias, a_log, d_skip, ssm_norm_w,
                w_ao, w_so, w_oo, final_norm_w):
    b, s, d = x.shape
    t = b * s
    g4, hg = SSD_GROUPS, HEADS_PER_GROUP
    dtb_g = _lane_pad(dt_bias.reshape(g4, hg)).reshape(g4, 1, LANES)
    alog_g = _lane_pad(a_log.reshape(g4, hg)).reshape(g4, 1, LANES)
    dskip_x = jnp.repeat(d_skip, HEAD_DIM, axis=1)
    fnw = final_norm_w.reshape(1, d)

    x2 = x.reshape(t, d)
    h = _rms_fwd(x2, norm_w)
    proj = _matmul(h, w_proj, tm=512, tn=1280, tk=1024, name="proj")
    proj3 = proj.reshape(b, s, NP)
    o3, yp3 = _attn_fwd(proj3)
    xact = _conv_fwd(proj3, conv_w_all, conv_b)
    dtr = proj3[:, :, DT0:DT0 + g4 * hg].reshape(b, s, g4, hg).transpose(0, 2, 1, 3)
    dtr_g = jnp.pad(dtr, ((0, 0), (0, 0), (0, 0), (0, LANES - hg)))
    y3, yn3, hst = _ssd_fwd(xact, proj3, dtr_g, dtb_g, alog_g, dskip_x, ssm_norm_w)
    yp = yp3.reshape(t, D_MODEL)
    yn = yn3.reshape(t, SSD_WIDTH)
    ya = _matmul(yp, w_ao, tm=512, tn=1024, tk=1024, name="attn_out")
    ys = _matmul(yn, w_so, tm=512, tn=1024, tk=2048, name="ssm_out")
    merged = _merge_fwd(proj, ya, ys)
    mo = _matmul(merged, w_oo, tm=512, tn=1024, tk=1024, name="out_proj")
    dout, doutb, loss_part, d_fnw = _final_fwd_bwd(x2, mo, loss_target.reshape(t, d), fnw)

    dmerged = _matmul(doutb, w_oo, tb=True, tm=512, tn=1024, tk=1024, name="d_merged")
    g_wo = _matmul(merged, doutb, ta=True, tm=512, tn=1024, tk=1024, name="g_w_o")
    dya, dys, dgate = _merge_bwd(dmerged, proj, ya, ys)
    dyp = _matmul(dya, w_ao, tb=True, tm=512, tn=1024, tk=1024, name="d_attn_pre")
    g_wao = _matmul(yp, dya, ta=True, tm=512, tn=1024, tk=1024, name="g_w_attn_out")
    dyn = _matmul(dys, w_so, tb=True, tm=512, tn=2048, tk=1024, name="d_ssm_norm")
    g_wso = _matmul(yn, dys, ta=True, tm=512, tn=1024, tk=1024, name="g_w_ssm_out")
    dq, dk, dv, dza = _attn_bwd(proj3, dyp.reshape(b, s, D_MODEL), o3)
    (dxs, dbm, dcm, dzs, ddtr_g, d_snw_g, d_alog_g, d_dtb_g, d_dsk_g) = _ssd_bwd(
        dyn.reshape(b, s, SSD_WIDTH), y3, xact, proj3, hst, dtr_g, dtb_g, alog_g, dskip_x, ssm_norm_w)
    dx_xs, g_cw_xs, g_cb_xs = _conv_bwd(dxs, proj3, conv_w_all, conv_b, 0, "conv_bwd_x")
    dx_bm, g_cw_bm, g_cb_bm = _conv_bwd(dbm, proj3, conv_w_all, conv_b, SSD_WIDTH, "conv_bwd_b")
    dx_cm, g_cw_cm, g_cb_cm = _conv_bwd(dcm, proj3, conv_w_all, conv_b, SSD_WIDTH + g4 * SSD_STATE, "conv_bwd_c")
    ddt = ddtr_g[:, :, :, :hg].transpose(0, 2, 1, 3).reshape(b, s, g4 * hg).astype(BF16)
    dproj = jnp.concatenate([dq, dk, dv, dza, dzs, dx_xs, dx_bm, dx_cm, dgate.reshape(b, s, 2 * D_MODEL),
                             jnp.pad(ddt, ((0, 0), (0, 0), (0, DT_PAD - g4 * hg)))], axis=2).reshape(t, NP)
    g_wproj = _matmul(h, dproj, ta=True, tm=512, tn=1280, tk=1024, name="g_w_in")
    dh = _matmul(dproj, w_proj, tb=True, tm=512, tn=1024, tk=1280, name="d_h")
    grad_x, d_nw = _rms_bwd(dh, x2, norm_w, dout)
    g_cw = jnp.concatenate([g_cw_xs, g_cw_bm, g_cw_cm], axis=1)
    g_cb = jnp.concatenate([g_cb_xs, g_cb_bm, g_cb_cm], axis=1)
    return (loss_part, grad_x, d_nw, g_wproj, g_cw, g_cb, d_dtb_g, d_alog_g, d_dsk_g, d_snw_g, g_wao, g_wso, g_wo, d_fnw)


def kernel(x, norm_w, w_in, conv_w, conv_b, dt_bias, a_log, d_skip, ssm_norm_w, w_attn_out, w_ssm_out, w_o, final_norm_w, loss_target, m_norm_w, m_w_in, m_conv_w, m_conv_b, m_dt_bias, m_a_log, m_d_skip, m_ssm_norm_w, m_w_attn_out, m_w_ssm_out, m_w_o, m_final_norm_w, v_norm_w, v_w_in, v_conv_w, v_conv_b, v_dt_bias, v_a_log, v_d_skip, v_ssm_norm_w, v_w_attn_out, v_w_ssm_out, v_w_o, v_final_norm_w):
    b, s, d = x.shape
    core = lax.axis_index("c")
    g4, hg = SSD_GROUPS, HEADS_PER_GROUP
    shard_cols = w_in.shape[2]
    w_proj, w_ao, w_so, w_oo, conv_w_all, offs, conv_rows = _gather_all(w_in, w_attn_out, w_ssm_out, w_o, conv_w)
    (loss_part, grad_x, d_nw, g_wproj, g_cw, g_cb, d_dtb_g, d_alog_g, d_dsk_g, d_snw_g, g_wao, g_wso, g_wo, d_fnw) = _local_step(
        x, loss_target, norm_w, w_proj, conv_w_all, conv_b, dt_bias, a_log, d_skip, ssm_norm_w, w_ao, w_so, w_oo, final_norm_w)

    g_win = jnp.concatenate([g_wproj[:, :GATE0], g_wproj[:, DT0:DT0 + 32], g_wproj[:, GATE0:DT0]], axis=1)
    per_chip = jnp.concatenate([
        g_win.reshape(d, N_CHIPS, shard_cols).transpose(1, 0, 2).reshape(N_CHIPS, -1, LANES),
        g_wao.reshape(N_CHIPS, -1, LANES), g_wso.reshape(N_CHIPS, -1, LANES), g_wo.reshape(N_CHIPS, -1, LANES),
        g_cw.reshape(CONV_K, N_CHIPS, CONV_DIM // N_CHIPS).transpose(1, 0, 2).reshape(N_CHIPS, -1, LANES)], axis=1)
    n_rows = per_chip.shape[1]
    pad_rows = -n_rows % (2 * ADD_ROWS)
    per_chip = jnp.pad(per_chip, ((0, 0), (0, pad_rows), (0, 0)))
    from_sibling = _sibling_swap(per_chip, "other", "grad_swap_halves")
    chip_sum = _add_halves(per_chip, from_sibling, core.reshape(1).astype(jnp.int32))
    from_chips = _chip_all_to_all(chip_sum)
    mine = _sum_chips(from_chips)
    theirs = _sibling_swap(mine, "all", "grad_swap_result")
    is_south = core == 0
    shard = jnp.concatenate([jnp.where(is_south, mine[0], theirs[0]), jnp.where(is_south, theirs[0], mine[0])], axis=0)

    small = jnp.concatenate([
        loss_part, d_nw, g_cb, _lane_pad(d_dtb_g[:, 0, :hg].reshape(1, -1)), _lane_pad(d_alog_g[:, 0, :hg].reshape(1, -1)),
        _lane_pad(d_dsk_g[:, 0, :hg].reshape(1, -1)),
        d_snw_g.reshape(1, -1), d_fnw], axis=1)
    small_rows = small.shape[1] // LANES
    reduced = _gather_small(_pad_rows(small.reshape(small_rows, LANES), -(-small_rows // 8) * 8), True, "reduce_small")
    flat = reduced.reshape(-1)

    def take(start, n):
        return flat[start:start + n].reshape(1, n)

    loss = flat[0]
    pos = LANES
    g_norm_w = take(pos, d); pos += d
    g_conv_b = take(pos, CONV_DIM); pos += CONV_DIM
    g_dt_bias = take(pos, g4 * hg); pos += LANES
    g_a_log = take(pos, g4 * hg); pos += LANES
    g_d_skip = take(pos, g4 * hg); pos += LANES
    g_ssm_norm_w = take(pos, SSD_WIDTH); pos += SSD_WIDTH
    g_final_norm_w = take(pos, d)

    g_w_in = shard[offs[0]:offs[1]].reshape(d, shard_cols)
    g_w_attn_out = shard[offs[1]:offs[2]].reshape(D_MODEL // N_CHIPS, d)
    g_w_ssm_out = shard[offs[2]:offs[3]].reshape(SSD_WIDTH // N_CHIPS, d)
    g_w_o = shard[offs[3]:offs[4]].reshape(D_MODEL // N_CHIPS, d)
    g_conv_w = shard[offs[4]:offs[4] + conv_rows].reshape(CONV_K, CONV_DIM // N_CHIPS)

    names = ["norm_w", "w_in", "conv_w", "conv_b", "dt_bias", "a_log", "d_skip", "ssm_norm_w",
             "w_attn_out", "w_ssm_out", "w_o", "final_norm_w"]
    weights = [norm_w, w_in, conv_w, conv_b, dt_bias, a_log, d_skip, ssm_norm_w, w_attn_out, w_ssm_out, w_o, final_norm_w]
    grads = [g_norm_w, g_w_in, g_conv_w, g_conv_b, g_dt_bias, g_a_log, g_d_skip, g_ssm_norm_w,
             g_w_attn_out, g_w_ssm_out, g_w_o, g_final_norm_w]
    ms = [m_norm_w, m_w_in, m_conv_w, m_conv_b, m_dt_bias, m_a_log, m_d_skip, m_ssm_norm_w,
          m_w_attn_out, m_w_ssm_out, m_w_o, m_final_norm_w]
    vs = [v_norm_w, v_w_in, v_conv_w, v_conv_b, v_dt_bias, v_a_log, v_d_skip, v_ssm_norm_w,
          v_w_attn_out, v_w_ssm_out, v_w_o, v_final_norm_w]
    out_g, out_d, out_m, out_v = [], [], [], []
    for name, w, g, m, v in zip(names, weights, grads, ms, vs):
        shape2 = g.shape
        dlt, nm, nv = _adamw(w.reshape(shape2), g, m.reshape(shape2), v.reshape(shape2), "adamw_" + name)
        out_g.append(g.reshape(w.shape))
        out_d.append(dlt.reshape(w.shape))
        out_m.append(nm.reshape(w.shape))
        out_v.append(nv.reshape(w.shape))

    return (loss, grad_x.reshape(b, s, d), *out_g, *out_d, *out_m, *out_v)
```

```python
import jax
import jax.numpy as jnp
from jax import lax
from jax.experimental import pallas as pl
from jax.experimental.pallas import tpu as pltpu

F32 = jnp.float32
BF16 = jnp.bfloat16
HIGHEST = lax.Precision.HIGHEST
MESH = pl.DeviceIdType.MESH

D_MODEL = 1024
SB_HEADS = 16
HEAD_DIM = 64
SSD_WIDTH = 2048
SSD_GROUPS = 4
GROUP_WIDTH = SSD_WIDTH // SSD_GROUPS
HEADS_PER_GROUP = 8
SSD_STATE = 128
CHUNK = 128
CONV_K = 4
CONV_DIM = 3072
D_PROJ = 11296
EPS = 1e-6
ADAM_LR, ADAM_B1, ADAM_B2, ADAM_EPS, ADAM_WD, ADAM_STEP = 0.001, 0.9, 0.999, 1e-08, 0.01, 10

LANES = 128
Q0, K0, V0, ZA0, ZS0, XBC0, GATE0, DT0 = 0, 1024, 2048, 3072, 4096, 6144, 9216, 11264
DT_PAD = 256
NP = DT0 + DT_PAD
N_CHIPS = 4
LAST_DT0 = GATE0 - (N_CHIPS - 1) * (D_PROJ // N_CHIPS)
VMEM_LIMIT = 56 * 1024 * 1024


def _cparams(*sem):
    return pltpu.CompilerParams(dimension_semantics=sem or None, vmem_limit_bytes=VMEM_LIMIT)


def _sigmoid(z):
    return 1.0 / (1.0 + jnp.exp(-z))


def _dot(a, b, dims, precision=None):
    return lax.dot_general(a, b, (dims, ((), ())), preferred_element_type=F32, precision=precision)


NN = ((1,), (0,))
NT = ((1,), (1,))
TN = ((0,), (0,))


def _matmul(a, b, *, ta=False, tb=False, out_dtype=F32, tm, tn, tk, name):
    m, k = (a.shape[1], a.shape[0]) if ta else a.shape
    n = b.shape[0] if tb else b.shape[1]
    assert m % tm == 0 and n % tn == 0 and k % tk == 0, (name, m, n, k)
    nk = k // tk
    use_scratch = out_dtype != F32
    dims = ((0,) if ta else (1,), (1,) if tb else (0,))

    def kern(a_ref, b_ref, o_ref, *scratch):
        acc = scratch[0] if use_scratch else o_ref
        kk = pl.program_id(2)

        @pl.when(kk == 0)
        def _():
            acc[...] = jnp.zeros_like(acc)

        acc[...] += _dot(a_ref[...], b_ref[...], dims)
        if use_scratch:
            @pl.when(kk == nk - 1)
            def _():
                o_ref[...] = acc[...].astype(out_dtype)

    a_spec = pl.BlockSpec((tk, tm), lambda i, j, q: (q, i)) if ta else pl.BlockSpec((tm, tk), lambda i, j, q: (i, q))
    b_spec = pl.BlockSpec((tn, tk), lambda i, j, q: (j, q)) if tb else pl.BlockSpec((tk, tn), lambda i, j, q: (q, j))
    return pl.pallas_call(
        kern, name=name,
        out_shape=jax.ShapeDtypeStruct((m, n), out_dtype),
        grid=(m // tm, n // tn, nk),
        in_specs=[a_spec, b_spec],
        out_specs=pl.BlockSpec((tm, tn), lambda i, j, q: (i, j)),
        scratch_shapes=[pltpu.VMEM((tm, tn), F32)] if use_scratch else [],
        compiler_params=_cparams("parallel", "parallel", "arbitrary"),
    )(a, b)


ROWS = 256


def _rms_fwd(x2, w):
    t, d = x2.shape

    def kern(x_ref, w_ref, h_ref):
        x = x_ref[...]
        r = lax.rsqrt(jnp.mean(x * x, axis=-1, keepdims=True) + EPS)
        h_ref[...] = (x * r * w_ref[...]).astype(BF16)

    return pl.pallas_call(
        kern, name="rms_fwd",
        out_shape=jax.ShapeDtypeStruct((t, d), BF16),
        grid=(t // ROWS,),
        in_specs=[pl.BlockSpec((ROWS, d), lambda i: (i, 0)), pl.BlockSpec((1, d), lambda i: (0, 0))],
        out_specs=pl.BlockSpec((ROWS, d), lambda i: (i, 0)),
        compiler_params=_cparams("parallel"),
    )(x2, w)


def _rms_bwd(dh, x2, w, dout):
    t, d = x2.shape

    def kern(dh_ref, x_ref, w_ref, dout_ref, gx_ref, dw_ref):
        @pl.when(pl.program_id(0) == 0)
        def _():
            dw_ref[...] = jnp.zeros_like(dw_ref)

        x = x_ref[...]
        r = lax.rsqrt(jnp.mean(x * x, axis=-1, keepdims=True) + EPS)
        xh = x * r
        g = dh_ref[...]
        dw_ref[...] += jnp.sum(g * xh, axis=0, keepdims=True)
        gw = g * w_ref[...]
        gx_ref[...] = dout_ref[...] + r * (gw - xh * jnp.mean(gw * xh, axis=-1, keepdims=True))

    row = pl.BlockSpec((ROWS, d), lambda i: (i, 0))
    vec = pl.BlockSpec((1, d), lambda i: (0, 0))
    return pl.pallas_call(
        kern, name="rms_bwd",
        out_shape=(jax.ShapeDtypeStruct((t, d), F32), jax.ShapeDtypeStruct((1, d), F32)),
        grid=(t // ROWS,),
        in_specs=[row, row, vec, row],
        out_specs=(row, vec),
        compiler_params=_cparams("arbitrary"),
    )(dh, x2, w, dout)


def _final_fwd_bwd(x2, mo, target, w):
    t, d = x2.shape

    def kern(x_ref, mo_ref, t_ref, w_ref, dout_ref, doutb_ref, loss_ref, dw_ref):
        @pl.when(pl.program_id(0) == 0)
        def _():
            loss_ref[...] = jnp.zeros_like(loss_ref)
            dw_ref[...] = jnp.zeros_like(dw_ref)

        u = x_ref[...] + mo_ref[...]
        r = lax.rsqrt(jnp.mean(u * u, axis=-1, keepdims=True) + EPS)
        uh = u * r
        wv = w_ref[...]
        err = uh * wv - t_ref[...]
        loss_ref[...] += (0.5 / d) * jnp.sum(err * err)
        dy = err * (1.0 / d)
        dw_ref[...] += jnp.sum(dy * uh, axis=0, keepdims=True)
        gw = dy * wv
        du = r * (gw - uh * jnp.mean(gw * uh, axis=-1, keepdims=True))
        dout_ref[...] = du
        doutb_ref[...] = du.astype(BF16)

    row = pl.BlockSpec((ROWS, d), lambda i: (i, 0))
    vec = pl.BlockSpec((1, d), lambda i: (0, 0))
    return pl.pallas_call(
        kern, name="final_fwd_bwd",
        out_shape=(jax.ShapeDtypeStruct((t, d), F32), jax.ShapeDtypeStruct((t, d), BF16),
                   jax.ShapeDtypeStruct((1, LANES), F32), jax.ShapeDtypeStruct((1, d), F32)),
        grid=(t // ROWS,),
        in_specs=[row, row, row, vec],
        out_specs=(row, row, pl.BlockSpec((1, LANES), lambda i: (0, 0)), vec),
        compiler_params=_cparams("arbitrary"),
    )(x2, mo, target, w)


def _merge_fwd(proj2, ya, ys):
    t = ya.shape[0]
    gblk = GATE0 // D_MODEL

    def kern(ga_ref, gs_ref, ya_ref, ys_ref, o_ref):
        o_ref[...] = (_sigmoid(ga_ref[...]) * ya_ref[...] + _sigmoid(gs_ref[...]) * ys_ref[...]).astype(BF16)

    row = pl.BlockSpec((ROWS, D_MODEL), lambda i: (i, 0))
    return pl.pallas_call(
        kern, name="merge_fwd",
        out_shape=jax.ShapeDtypeStruct((t, D_MODEL), BF16),
        grid=(t // ROWS,),
        in_specs=[pl.BlockSpec((ROWS, D_MODEL), lambda i: (i, gblk)),
                  pl.BlockSpec((ROWS, D_MODEL), lambda i: (i, gblk + 1)), row, row],
        out_specs=row,
        compiler_params=_cparams("parallel"),
    )(proj2, proj2, ya, ys)


def _merge_bwd(dm, proj2, ya, ys):
    t = ya.shape[0]
    gblk = GATE0 // D_MODEL

    def kern(dm_ref, ga_ref, gs_ref, ya_ref, ys_ref, dya_ref, dys_ref, dg_ref):
        g = dm_ref[...]
        sa = _sigmoid(ga_ref[...])
        ss = _sigmoid(gs_ref[...])
        dya_ref[...] = (g * sa).astype(BF16)
        dys_ref[...] = (g * ss).astype(BF16)
        dg_ref[:, :D_MODEL] = (g * ya_ref[...] * sa * (1.0 - sa)).astype(BF16)
        dg_ref[:, D_MODEL:] = (g * ys_ref[...] * ss * (1.0 - ss)).astype(BF16)

    row = pl.BlockSpec((ROWS, D_MODEL), lambda i: (i, 0))
    return pl.pallas_call(
        kern, name="merge_bwd",
        out_shape=(jax.ShapeDtypeStruct((t, D_MODEL), BF16), jax.ShapeDtypeStruct((t, D_MODEL), BF16),
                   jax.ShapeDtypeStruct((t, 2 * D_MODEL), BF16)),
        grid=(t // ROWS,),
        in_specs=[row, pl.BlockSpec((ROWS, D_MODEL), lambda i: (i, gblk)),
                  pl.BlockSpec((ROWS, D_MODEL), lambda i: (i, gblk + 1)), row, row],
        out_specs=(row, row, pl.BlockSpec((ROWS, 2 * D_MODEL), lambda i: (i, 0))),
        compiler_params=_cparams("parallel"),
    )(dm, proj2, proj2, ya, ys)


TQ = 256
TK = 256
HEAD_LANES = (slice(0, HEAD_DIM), slice(HEAD_DIM, 2 * HEAD_DIM))


def _tri(pred):
    r = lax.broadcasted_iota(jnp.int32, (TK, TK), 0)
    c = lax.broadcasted_iota(jnp.int32, (TK, TK), 1)
    return pred(r, c).astype(BF16)


def _split_bf16(v):
    hi = v.astype(BF16)
    lo = (v - hi.astype(F32)).astype(BF16)
    return hi, lo


def _tri_dot(v, tri):
    hi, lo = _split_bf16(v)
    return _dot(hi, tri, NN) + _dot(lo, tri, NN)


def _sb_logs(z, mask):
    l1p = jnp.log(1.0 + jnp.exp(-jnp.abs(z)))
    lb = jnp.minimum(z, 0.0) - l1p
    lom = -jnp.maximum(z, 0.0) - l1p
    if mask is not None:
        lom = jnp.where(mask, lom, 0.0)
    return lb, lom


def _sb_weights(lb, later, carry_r, mask):
    a = jnp.exp(lb + (later + carry_r))
    if mask is not None:
        a = jnp.where(mask, a, 0.0)
    return a


def _split_heads(dst, src, scale=None):
    for h, lanes in enumerate(HEAD_LANES):
        v = src[:, lanes]
        dst[h] = (v if scale is None else v * scale).astype(BF16)


def _attn_fwd(proj3):
    b, s, _ = proj3.shape
    nq = s // TQ
    scale = HEAD_DIM ** -0.5

    def kern(q_ref, k_ref, v_ref, za_ref, o_ref, yp_ref, qs, ks, vs):
        _split_heads(qs, q_ref[0], scale)
        _split_heads(ks, k_ref[0])
        _split_heads(vs, v_ref[0])
        row = lax.broadcasted_iota(jnp.int32, (TQ, TK), 0)
        col = lax.broadcasted_iota(jnp.int32, (TQ, TK), 1)
        tri_gt = _tri(lambda j, sk: j > sk)

        def q_block(i, _):
            r0 = pl.multiple_of(i * TQ, TQ)
            n_kb = (r0 + TQ + TK - 1) // TK
            qh = [qs[h, pl.ds(r0, TQ), :] for h in range(2)]

            def k_block(c0, carry, mask):
                kh = [ks[h, pl.ds(c0, TK), :] for h in range(2)]
                vh = [vs[h, pl.ds(c0, TK), :] for h in range(2)]
                z = [_dot(qh[h], kh[h], NT) for h in range(2)]
                logs, later = [], []
                for h in range(2):
                    logs.append(_sb_logs(z[h], mask))
                    later.append(_tri_dot(logs[h][1], tri_gt))
                out = []
                for h in range(2):
                    carry_r, acc = carry[h]
                    lb, lom = logs[h]
                    a = _sb_weights(lb, later[h], carry_r, mask)
                    row_sum = later[h][:, 0:1] + lom[:, 0:1]
                    out.append((carry_r + row_sum, acc + _dot(a.astype(BF16), vh[h], NN)))
                return tuple(out)

            c_last = pl.multiple_of((n_kb - 1) * TK, TK)
            start = (jnp.zeros((TQ, 1), F32), jnp.zeros((TQ, HEAD_DIM), F32))
            carry = k_block(c_last, (start, start), col + c_last < row + r0)

            def unmasked(jj, carry):
                return k_block(pl.multiple_of((n_kb - 2 - jj) * TK, TK), carry, None)

            carry = lax.fori_loop(0, n_kb - 1, unmasked, carry)
            for (_, acc), lanes in zip(carry, HEAD_LANES):
                o_ref[0, pl.ds(r0, TQ), lanes] = acc
                za = za_ref[0, pl.ds(r0, TQ), lanes]
                yp_ref[0, pl.ds(r0, TQ), lanes] = (acc * (za * _sigmoid(za))).astype(BF16)
            return 0

        lax.fori_loop(0, nq, q_block, 0)

    def spec(c0):
        return pl.BlockSpec((1, s, LANES), lambda bi, hp: (bi, 0, c0 // LANES + hp))

    out_spec = pl.BlockSpec((1, s, LANES), lambda bi, hp: (bi, 0, hp))
    return pl.pallas_call(
        kern, name="attn_fwd",
        out_shape=(jax.ShapeDtypeStruct((b, s, D_MODEL), F32), jax.ShapeDtypeStruct((b, s, D_MODEL), BF16)),
        grid=(b, SB_HEADS // 2),
        in_specs=[spec(Q0), spec(K0), spec(V0), spec(ZA0)],
        out_specs=(out_spec, out_spec),
        scratch_shapes=[pltpu.VMEM((2, s, HEAD_DIM), BF16)] * 3,
        compiler_params=_cparams("parallel", "parallel"),
    )(proj3, proj3, proj3, proj3)


def _attn_bwd(proj3, dyp3, o3):
    b, s, _ = proj3.shape
    nq = s // TQ
    scale = HEAD_DIM ** -0.5

    def kern(q_ref, k_ref, v_ref, za_ref, dyp_ref, o_ref, dq_ref, dk_ref, dv_ref, dza_ref,
             qs, ks, vs, dos, dk_acc, dv_acc):
        _split_heads(qs, q_ref[0], scale)
        _split_heads(ks, k_ref[0])
        _split_heads(vs, v_ref[0])
        za = za_ref[0]
        sg = _sigmoid(za)
        dyp = dyp_ref[0]
        _split_heads(dos, dyp * (za * sg))
        dza_ref[0] = (dyp * o_ref[0] * (sg * (1.0 + za * (1.0 - sg)))).astype(BF16)
        dk_acc[...] = jnp.zeros_like(dk_acc)
        dv_acc[...] = jnp.zeros_like(dv_acc)
        row = lax.broadcasted_iota(jnp.int32, (TQ, TK), 0)
        col = lax.broadcasted_iota(jnp.int32, (TQ, TK), 1)
        tri_gt = _tri(lambda j, sk: j > sk)
        tri_ge = _tri(lambda j, sk: j >= sk)

        def q_block(i, _):
            r0 = pl.multiple_of(i * TQ, TQ)
            n_kb = (r0 + TQ + TK - 1) // TK
            qh = [qs[h, pl.ds(r0, TQ), :] for h in range(2)]
            doh = [dos[h, pl.ds(r0, TQ), :] for h in range(2)]
            totals = [jnp.sum(doh[h].astype(F32) * o_ref[0, pl.ds(r0, TQ), lanes], axis=1, keepdims=True)
                      for h, lanes in enumerate(HEAD_LANES)]

            def k_block(c0, carry, mask):
                kh = [ks[h, pl.ds(c0, TK), :] for h in range(2)]
                vh = [vs[h, pl.ds(c0, TK), :] for h in range(2)]
                z = [_dot(qh[h], kh[h], NT) for h in range(2)]
                da = [_dot(doh[h], vh[h], NT) for h in range(2)]
                logs, later = [], []
                for h in range(2):
                    logs.append(_sb_logs(z[h], mask))
                    later.append(_tri_dot(logs[h][1], tri_gt))
                ab, g, suffix = [], [], []
                for h in range(2):
                    a = _sb_weights(logs[h][0], later[h], carry[h][0], mask)
                    ab.append(a.astype(BF16))
                    g.append(da[h] * ab[h].astype(F32))
                    suffix.append(_tri_dot(g[h], tri_ge))
                out = []
                for h in range(2):
                    carry_r, carry_g, dq = carry[h]
                    lb, lom = logs[h]
                    dz = g[h] - (g[h] + (totals[h] - carry_g) - suffix[h]) * jnp.exp(lb)
                    if mask is not None:
                        dz = jnp.where(mask, dz, 0.0)
                    dzb = dz.astype(BF16)
                    dk_acc[h, pl.ds(c0, TK), :] += _dot(dzb, qh[h], TN)
                    dv_acc[h, pl.ds(c0, TK), :] += _dot(ab[h], doh[h], TN)
                    out.append((carry_r + (later[h][:, 0:1] + lom[:, 0:1]), carry_g + suffix[h][:, 0:1],
                                dq + _dot(dzb, kh[h], NN)))
                return tuple(out)

            c_last = pl.multiple_of((n_kb - 1) * TK, TK)
            zero = jnp.zeros((TQ, 1), F32)
            start = (zero, zero, jnp.zeros((TQ, HEAD_DIM), F32))
            carry = k_block(c_last, (start, start), col + c_last < row + r0)

            def unmasked(jj, carry):
                return k_block(pl.multiple_of((n_kb - 2 - jj) * TK, TK), carry, None)

            carry = lax.fori_loop(0, n_kb - 1, unmasked, carry)
            for (_, _, dq), lanes in zip(carry, HEAD_LANES):
                dq_ref[0, pl.ds(r0, TQ), lanes] = (dq * scale).astype(BF16)
            return 0

        lax.fori_loop(0, nq, q_block, 0)

        for h, lanes in enumerate(HEAD_LANES):
            dk_ref[0, :, lanes] = dk_acc[h].astype(BF16)
            dv_ref[0, :, lanes] = dv_acc[h].astype(BF16)

    def spec(c0):
        return pl.BlockSpec((1, s, LANES), lambda bi, hp: (bi, 0, c0 // LANES + hp))

    plain = pl.BlockSpec((1, s, LANES), lambda bi, hp: (bi, 0, hp))
    out = jax.ShapeDtypeStruct((b, s, D_MODEL), BF16)
    return pl.pallas_call(
        kern, name="attn_bwd",
        out_shape=(out, out, out, out),
        grid=(b, SB_HEADS // 2),
        in_specs=[spec(Q0), spec(K0), spec(V0), spec(ZA0), plain, plain],
        out_specs=(plain, plain, plain, plain),
        scratch_shapes=[pltpu.VMEM((2, s, HEAD_DIM), BF16)] * 4 + [pltpu.VMEM((2, s, HEAD_DIM), F32)] * 2,
        compiler_params=_cparams("parallel", "parallel"),
    )(proj3, proj3, proj3, proj3, dyp3, o3)


CONV_COLS = 256
HALO = 8


def _conv_pre(xp, w_ref, b_ref, r0):
    pre = b_ref[...] + w_ref[CONV_K - 1:CONV_K, :] * xp[pl.ds(HALO + r0, CHUNK), :]
    for kk in range(1, CONV_K):
        pre = pre + w_ref[CONV_K - 1 - kk:CONV_K - kk, :] * xp[pl.ds(HALO + r0 - kk, CHUNK), :]
    return pre


def _conv_fwd(proj3, conv_w, conv_b):
    b, s, _ = proj3.shape
    nc = s // CHUNK

    def kern(x_ref, w_ref, b_ref, o_ref, xp):
        xp[0:HALO, :] = jnp.zeros((HALO, CONV_COLS), F32)
        xp[HALO:, :] = x_ref[0]
        for ci in range(nc):
            pre = _conv_pre(xp, w_ref, b_ref, ci * CHUNK)
            o_ref[0, ci * CHUNK:(ci + 1) * CHUNK, :] = pre * _sigmoid(pre)

    return pl.pallas_call(
        kern, name="conv_fwd",
        out_shape=jax.ShapeDtypeStruct((b, s, CONV_DIM), F32),
        grid=(CONV_DIM // CONV_COLS, b),
        in_specs=[pl.BlockSpec((1, s, CONV_COLS), lambda j, bi: (bi, 0, XBC0 // CONV_COLS + j)),
                  pl.BlockSpec((CONV_K, CONV_COLS), lambda j, bi: (0, j)),
                  pl.BlockSpec((1, CONV_COLS), lambda j, bi: (0, j))],
        out_specs=pl.BlockSpec((1, s, CONV_COLS), lambda j, bi: (bi, 0, j)),
        scratch_shapes=[pltpu.VMEM((s + HALO, CONV_COLS), F32)],
        compiler_params=_cparams("parallel", "parallel"),
    )(proj3, conv_w, conv_b)


def _conv_bwd(dact, proj3, conv_w, conv_b, col0, name):
    b, s, width = dact.shape
    nc = s // CHUNK
    j0 = col0 // CONV_COLS

    def kern(da_ref, x_ref, w_ref, b_ref, dx_ref, dw_ref, db_ref, xp, dp):
        @pl.when(pl.program_id(1) == 0)
        def _():
            dw_ref[...] = jnp.zeros_like(dw_ref)
            db_ref[...] = jnp.zeros_like(db_ref)

        xp[0:HALO, :] = jnp.zeros((HALO, CONV_COLS), F32)
        xp[HALO:, :] = x_ref[0]
        dp[s:, :] = jnp.zeros((HALO, CONV_COLS), F32)
        for ci in range(nc):
            r0 = ci * CHUNK
            pre = _conv_pre(xp, w_ref, b_ref, r0)
            sg = _sigmoid(pre)
            dpre = da_ref[0, r0:r0 + CHUNK, :] * (sg * (1.0 + pre * (1.0 - sg)))
            dp[r0:r0 + CHUNK, :] = dpre
            db_ref[...] += jnp.sum(dpre, axis=0, keepdims=True)
            for kk in range(CONV_K):
                tap = CONV_K - 1 - kk
                dw_ref[tap:tap + 1, :] += jnp.sum(dpre * xp[pl.ds(HALO + r0 - kk, CHUNK), :], axis=0, keepdims=True)
        for ci in range(nc):
            r0 = ci * CHUNK
            dx = w_ref[CONV_K - 1:CONV_K, :] * dp[pl.ds(r0, CHUNK), :]
            for kk in range(1, CONV_K):
                dx = dx + w_ref[CONV_K - 1 - kk:CONV_K - kk, :] * dp[pl.ds(r0 + kk, CHUNK), :]
            dx_ref[0, r0:r0 + CHUNK, :] = dx.astype(BF16)

    return pl.pallas_call(
        kern, name=name,
        out_shape=(jax.ShapeDtypeStruct((b, s, width), BF16), jax.ShapeDtypeStruct((CONV_K, width), F32),
                   jax.ShapeDtypeStruct((1, width), F32)),
        grid=(width // CONV_COLS, b),
        in_specs=[pl.BlockSpec((1, s, CONV_COLS), lambda j, bi: (bi, 0, j)),
                  pl.BlockSpec((1, s, CONV_COLS), lambda j, bi: (bi, 0, XBC0 // CONV_COLS + j0 + j)),
                  pl.BlockSpec((CONV_K, CONV_COLS), lambda j, bi: (0, j0 + j)),
                  pl.BlockSpec((1, CONV_COLS), lambda j, bi: (0, j0 + j))],
        out_specs=(pl.BlockSpec((1, s, CONV_COLS), lambda j, bi: (bi, 0, j)),
                   pl.BlockSpec((CONV_K, CONV_COLS), lambda j, bi: (0, j)),
                   pl.BlockSpec((1, CONV_COLS), lambda j, bi: (0, j))),
        scratch_shapes=[pltpu.VMEM((s + HALO, CONV_COLS), F32)] * 2,
        compiler_params=_cparams("parallel", "arbitrary"),
    )(dact, proj3, conv_w, conv_b)


def _ssd_common(dtr_ref, dtb_ref, alog_ref):
    lane = lax.broadcasted_iota(jnp.int32, (CHUNK, LANES), 1)
    row = lax.broadcasted_iota(jnp.int32, (CHUNK, LANES), 0)
    head_lane = lane < HEADS_PER_GROUP
    pre = dtr_ref[0, 0] + dtb_ref[0]
    dt = jnp.where(head_lane, jnp.maximum(pre, 0.0) + jnp.log(1.0 + jnp.exp(-jnp.abs(pre))), 0.0)
    a = jnp.where(head_lane[0:1], -jnp.exp(alog_ref[0]), 0.0)
    tril = (row >= lane).astype(F32)
    acs = _dot(tril, dt * a, NN, HIGHEST)
    acs_t = acs.T
    er = lax.broadcasted_iota(jnp.int32, (LANES, GROUP_WIDTH), 0)
    ec = lax.broadcasted_iota(jnp.int32, (LANES, GROUP_WIDTH), 1)
    expand = ((ec // HEAD_DIM) == er).astype(F32)
    tr = lax.broadcasted_iota(jnp.int32, (GROUP_WIDTH, LANES), 0)
    tc = lax.broadcasted_iota(jnp.int32, (GROUP_WIDTH, LANES), 1)
    reduce = ((tr // HEAD_DIM) == tc).astype(F32)
    dt_x = _dot(dt, expand, NN, HIGHEST)
    acs_x = _dot(acs, expand, NN, HIGHEST)
    end_x = acs_x[CHUNK - 1:CHUNK, :]
    end_col = jnp.broadcast_to(acs_t[:, CHUNK - 1:CHUNK], (LANES, LANES))
    chunk_decay = jnp.exp(_dot(reduce, end_col, NN, HIGHEST))
    causal = row >= lane
    return dict(dt=dt, a=a, pre=pre, head_lane=head_lane, acs=acs, acs_t=acs_t, expand=expand, reduce=reduce,
                dt_x=dt_x, acs_x=acs_x, end_x=end_x, chunk_decay=chunk_decay, causal=causal, row=row, lane=lane)


def _ssd_decay(cm, h):
    seg = cm["acs"][:, h:h + 1] - cm["acs_t"][h:h + 1, :]
    return jnp.where(cm["causal"], jnp.exp(jnp.minimum(seg, 0.0)), 0.0)


def _ssd_fwd(xact, proj3, dtr_g, dtb_g, alog_g, dskip_x, snw):
    b, s, _ = xact.shape
    nc = s // CHUNK
    g4 = SSD_GROUPS

    def kern(xs_ref, bm_ref, cm_ref, zs_ref, dtr_ref, dtb_ref, alog_ref, dsk_ref, snw_ref,
             y_ref, yn_ref, hst_ref, h_sc):
        @pl.when(pl.program_id(2) == 0)
        def _():
            h_sc[...] = jnp.zeros_like(h_sc)

        cm = _ssd_common(dtr_ref, dtb_ref, alog_ref)
        x = xs_ref[0]
        bmb = bm_ref[0].astype(BF16)
        cmb = cm_ref[0].astype(BF16)
        h_in = h_sc[...]
        hst_ref[0, 0, 0] = h_in
        xdt = x * cm["dt_x"]
        xdtb = xdt.astype(BF16)
        cb = _dot(cmb, bmb, NT)
        y_off = _dot(cmb, h_in.astype(BF16), NT) * jnp.exp(cm["acs_x"])
        for h in range(HEADS_PER_GROUP):
            lanes = slice(h * HEAD_DIM, (h + 1) * HEAD_DIM)
            m = (cb * _ssd_decay(cm, h)).astype(BF16)
            y_ref[0, :, lanes] = _dot(m, xdtb[:, lanes], NN)
        y = y_ref[0] + y_off + x * dsk_ref[...]
        y_ref[0] = y
        w = (xdt * jnp.exp(cm["end_x"] - cm["acs_x"])).astype(BF16)
        h_sc[...] = h_in * cm["chunk_decay"] + _dot(w, bmb, TN)
        zs = zs_ref[0]
        y2 = y * (zs * _sigmoid(zs))
        yn_ref[0] = (y2 * lax.rsqrt(jnp.mean(y2 * y2, axis=-1, keepdims=True) + EPS) * snw_ref[...]).astype(BF16)

    gw = GROUP_WIDTH
    small = pl.BlockSpec((1, 1, LANES), lambda gi, bi, ci: (gi, 0, 0))
    xblk = pl.BlockSpec((1, CHUNK, gw), lambda gi, bi, ci: (bi, ci, gi))
    return pl.pallas_call(
        kern, name="ssd_fwd",
        out_shape=(jax.ShapeDtypeStruct((b, s, SSD_WIDTH), F32), jax.ShapeDtypeStruct((b, s, SSD_WIDTH), BF16),
                   jax.ShapeDtypeStruct((b, nc, g4, gw, SSD_STATE), F32)),
        grid=(g4, b, nc),
        in_specs=[xblk,
                  pl.BlockSpec((1, CHUNK, LANES), lambda gi, bi, ci: (bi, ci, SSD_WIDTH // LANES + gi)),
                  pl.BlockSpec((1, CHUNK, LANES), lambda gi, bi, ci: (bi, ci, SSD_WIDTH // LANES + g4 + gi)),
                  pl.BlockSpec((1, CHUNK, gw), lambda gi, bi, ci: (bi, ci, ZS0 // gw + gi)),
                  pl.BlockSpec((1, 1, CHUNK, LANES), lambda gi, bi, ci: (bi, gi, ci, 0)),
                  small, small,
                  pl.BlockSpec((1, gw), lambda gi, bi, ci: (0, gi)),
                  pl.BlockSpec((1, gw), lambda gi, bi, ci: (0, gi))],
        out_specs=(xblk, xblk, pl.BlockSpec((1, 1, 1, gw, SSD_STATE), lambda gi, bi, ci: (bi, ci, gi, 0, 0))),
        scratch_shapes=[pltpu.VMEM((gw, SSD_STATE), F32)],
        compiler_params=_cparams("parallel", "parallel", "arbitrary"),
    )(xact, xact, xact, proj3, dtr_g, dtb_g, alog_g, dskip_x, snw)


def _ssd_bwd(dyn3, y3, xact, proj3, hst, dtr_g, dtb_g, alog_g, dskip_x, snw):
    b, s, _ = xact.shape
    nc = s // CHUNK
    g4 = SSD_GROUPS
    gw = GROUP_WIDTH

    def kern(dyn_ref, y_ref, xs_ref, bm_ref, cm_ref, zs_ref, hst_ref, dtr_ref, dtb_ref, alog_ref, dsk_ref, snw_ref,
             dxs_ref, dbm_ref, dcm_ref, dzs_ref, ddtr_ref, dsnw_ref, dalog_ref, ddtb_ref, ddsk_ref, dh_sc):
        first = jnp.logical_and(pl.program_id(1) == 0, pl.program_id(2) == 0)

        @pl.when(first)
        def _():
            dsnw_ref[...] = jnp.zeros_like(dsnw_ref)
            dalog_ref[...] = jnp.zeros_like(dalog_ref)
            ddtb_ref[...] = jnp.zeros_like(ddtb_ref)
            ddsk_ref[...] = jnp.zeros_like(ddsk_ref)

        @pl.when(pl.program_id(2) == 0)
        def _():
            dh_sc[...] = jnp.zeros_like(dh_sc)

        cm = _ssd_common(dtr_ref, dtb_ref, alog_ref)
        row, lane = cm["row"], cm["lane"]
        y = y_ref[0]
        zs = zs_ref[0]
        sg = _sigmoid(zs)
        silu = zs * sg
        y2 = y * silu
        rstd = lax.rsqrt(jnp.mean(y2 * y2, axis=-1, keepdims=True) + EPS)
        y2h = y2 * rstd
        dyn = dyn_ref[0]
        dsnw_ref[0] += jnp.sum(dyn * y2h, axis=0, keepdims=True)
        gwv = dyn * snw_ref[...]
        dy2 = rstd * (gwv - y2h * jnp.mean(gwv * y2h, axis=-1, keepdims=True))
        dzs_ref[0] = (dy2 * y * (sg * (1.0 + zs * (1.0 - sg)))).astype(BF16)
        dy = dy2 * silu
        dyb = dy.astype(BF16)

        x = xs_ref[0]
        bmb = bm_ref[0].astype(BF16)
        cmb = cm_ref[0].astype(BF16)
        h_in = hst_ref[0, 0, 0]
        h_inb = h_in.astype(BF16)
        d_hn = dh_sc[...]
        d_hnb = d_hn.astype(BF16)
        xdt = x * cm["dt_x"]
        xdtb = xdt.astype(BF16)
        eacs = jnp.exp(cm["acs_x"])
        dte = jnp.exp(cm["end_x"] - cm["acs_x"])
        wb = (xdt * dte).astype(BF16)

        dsk_lanes = jnp.broadcast_to(jnp.sum(dy * x, axis=0, keepdims=True), (8, gw))
        ddsk_ref[0] += _dot(dsk_lanes, cm["reduce"], NN, HIGHEST)[0:1, :]
        dyo = dy * eacs
        dyob = dyo.astype(BF16)
        dacs_x = dyo * _dot(cmb, h_inb, NT)
        dcm = _dot(dyob, h_inb, NN)
        dh_in = _dot(dyob, cmb, TN)
        dw = _dot(bmb, d_hnb, NT)
        dbm = _dot(wb, d_hnb, NN)
        dxdt = dw * dte
        e_l = dw * xdt * dte
        dacs_x = dacs_x - e_l
        dend_x = jnp.sum(e_l, axis=0, keepdims=True)
        dh_sc[...] = d_hn * cm["chunk_decay"] + dh_in
        q = d_hn * h_in * cm["chunk_decay"]
        dend_x = dend_x + _dot(jnp.ones((8, SSD_STATE), F32), q, NT, HIGHEST)[0:1, :]
        last_row = lax.broadcasted_iota(jnp.int32, (CHUNK, gw), 0) == CHUNK - 1
        dacs_x = dacs_x + jnp.where(last_row, dend_x, 0.0)

        cb = _dot(cmb, bmb, NT)
        dcb = jnp.zeros((CHUNK, CHUNK), F32)
        dacs = jnp.zeros((CHUNK, LANES), F32)
        dacs_t = jnp.zeros((LANES, CHUNK), F32)
        for h in range(HEADS_PER_GROUP):
            lanes = slice(h * HEAD_DIM, (h + 1) * HEAD_DIM)
            decay = _ssd_decay(cm, h)
            m = cb * decay
            dm = _dot(dyb[:, lanes], xdtb[:, lanes], NT)
            dxs_ref[0, :, lanes] = _dot(m.astype(BF16), dyb[:, lanes], TN)
            dcb_h = dm * decay
            dcb = dcb + dcb_h
            n = dcb_h * cb
            dacs = dacs + jnp.where(lane == h, jnp.sum(n, axis=1, keepdims=True), 0.0)
            dacs_t = dacs_t + jnp.where(row == h, jnp.sum(n, axis=0, keepdims=True), 0.0)
        dcbb = dcb.astype(BF16)
        dcm_ref[0] = dcm + _dot(dcbb, bmb, NN)
        dbm_ref[0] = dbm + _dot(dcbb, cmb, TN)
        dxdt = dxdt + dxs_ref[0]
        dxs_ref[0] = dy * dsk_ref[...] + dxdt * cm["dt_x"]

        dacs = dacs - dacs_t.T + _dot(dacs_x, cm["reduce"], NN, HIGHEST)
        ddt = _dot(dxdt * x, cm["reduce"], NN, HIGHEST)
        triu = (row <= lane).astype(F32)
        rc = _dot(triu, dacs, NN, HIGHEST)
        ddt = ddt + cm["a"] * rc
        dalog_ref[0] += jnp.sum(cm["dt"] * rc, axis=0, keepdims=True) * cm["a"]
        ddtr = jnp.where(cm["head_lane"], ddt * _sigmoid(cm["pre"]), 0.0)
        ddtr_ref[0, 0] = ddtr
        ddtb_ref[0] += jnp.sum(ddtr, axis=0, keepdims=True)

    def rev(ci):
        return nc - 1 - ci

    small = pl.BlockSpec((1, 1, LANES), lambda gi, bi, ci: (gi, 0, 0))
    xblk = pl.BlockSpec((1, CHUNK, gw), lambda gi, bi, ci: (bi, rev(ci), gi))
    nblk = pl.BlockSpec((1, CHUNK, LANES), lambda gi, bi, ci: (bi, rev(ci), gi))
    gvec = pl.BlockSpec((1, gw), lambda gi, bi, ci: (0, gi))
    gacc = pl.BlockSpec((1, 1, gw), lambda gi, bi, ci: (gi, 0, 0))
    return pl.pallas_call(
        kern, name="ssd_bwd",
        out_shape=(jax.ShapeDtypeStruct((b, s, SSD_WIDTH), F32),
                   jax.ShapeDtypeStruct((b, s, g4 * SSD_STATE), F32),
                   jax.ShapeDtypeStruct((b, s, g4 * SSD_STATE), F32),
                   jax.ShapeDtypeStruct((b, s, SSD_WIDTH), BF16),
                   jax.ShapeDtypeStruct((b, g4, s, LANES), F32),
                   jax.ShapeDtypeStruct((g4, 1, gw), F32),
                   jax.ShapeDtypeStruct((g4, 1, LANES), F32),
                   jax.ShapeDtypeStruct((g4, 1, LANES), F32),
                   jax.ShapeDtypeStruct((g4, 1, LANES), F32)),
        grid=(g4, b, nc),
        in_specs=[xblk, xblk, xblk,
                  pl.BlockSpec((1, CHUNK, LANES), lambda gi, bi, ci: (bi, rev(ci), SSD_WIDTH // LANES + gi)),
                  pl.BlockSpec((1, CHUNK, LANES), lambda gi, bi, ci: (bi, rev(ci), SSD_WIDTH // LANES + g4 + gi)),
                  pl.BlockSpec((1, CHUNK, gw), lambda gi, bi, ci: (bi, rev(ci), ZS0 // gw + gi)),
                  pl.BlockSpec((1, 1, 1, gw, SSD_STATE), lambda gi, bi, ci: (bi, rev(ci), gi, 0, 0)),
                  pl.BlockSpec((1, 1, CHUNK, LANES), lambda gi, bi, ci: (bi, gi, rev(ci), 0)),
                  small, small, gvec, gvec],
        out_specs=(xblk, nblk, nblk, xblk,
                   pl.BlockSpec((1, 1, CHUNK, LANES), lambda gi, bi, ci: (bi, gi, rev(ci), 0)),
                   gacc, small, small, small),
        scratch_shapes=[pltpu.VMEM((gw, SSD_STATE), F32)],
        compiler_params=_cparams("parallel", "arbitrary", "arbitrary"),
    )(dyn3, y3, xact, xact, xact, proj3, hst, dtr_g, dtb_g, alog_g, dskip_x, snw)


def _adamw(w, g, m, v, name):
    r, c = w.shape
    tr = 128 if r % 128 == 0 else r

    def kern(w_ref, g_ref, m_ref, v_ref, d_ref, nm_ref, nv_ref):
        gv = g_ref[...]
        nm = ADAM_B1 * m_ref[...] + (1.0 - ADAM_B1) * gv
        nv = ADAM_B2 * v_ref[...] + (1.0 - ADAM_B2) * (gv * gv)
        m_hat = nm / (1.0 - ADAM_B1 ** ADAM_STEP)
        v_hat = nv / (1.0 - ADAM_B2 ** ADAM_STEP)
        d_ref[...] = -ADAM_LR * (m_hat / (jnp.sqrt(v_hat) + ADAM_EPS) + ADAM_WD * w_ref[...])
        nm_ref[...] = nm
        nv_ref[...] = nv

    blk = pl.BlockSpec((tr, c), lambda i: (i, 0))
    out = jax.ShapeDtypeStruct((r, c), F32)
    return pl.pallas_call(
        kern, name=name, out_shape=(out, out, out), grid=(r // tr,),
        in_specs=[blk] * 4, out_specs=(blk, blk, blk),
        compiler_params=_cparams("parallel"),
    )(w, g, m, v)


ANY = pl.BlockSpec(memory_space=pl.ANY)


def _position():
    return lax.axis_index("x"), lax.axis_index("y"), lax.axis_index("c")


def _other_chips(x, y):
    return [(1 - x, y), (x, 1 - y), (1 - x, 1 - y)]


def _dma_sems(n):
    return [pltpu.SemaphoreType.DMA((n,)), pltpu.SemaphoreType.DMA((n,))]


def _gather_weights(shards):
    n = len(shards)

    def body(*refs):
        p_refs, out_refs = refs[:n], refs[n:2 * n]
        send_sems, recv_sems, local_sems = refs[2 * n:]
        x, y, c = _position()
        me = 2 * x + y
        chips = _other_chips(x, y)

        def slab(a, chip, hf):
            half = shards[a].shape[0] // 2
            return out_refs[a].at[chip, pl.ds(hf * half, half), :]

        def my_half(a):
            half = shards[a].shape[0] // 2
            return p_refs[a].at[pl.ds(c * half, half), :]

        def over_ici(a, j, chip_from):
            px, py = chips[j]
            return pltpu.make_async_remote_copy(
                src_ref=my_half(a), dst_ref=slab(a, chip_from, c),
                send_sem=send_sems.at[3 * a + j], recv_sem=recv_sems.at[3 * a + j],
                device_id=(px, py, c), device_id_type=MESH)

        def to_sibling(a, j, hf):
            px, py = chips[j]
            return pltpu.make_async_remote_copy(
                src_ref=slab(a, 2 * px + py, hf), dst_ref=slab(a, 2 * px + py, hf),
                send_sem=send_sems.at[3 * (n + a) + j], recv_sem=recv_sems.at[3 * (n + a) + j],
                device_id=(x, y, 1 - c), device_id_type=MESH)

        mine = [pltpu.make_async_copy(p_refs[a], out_refs[a].at[me], local_sems.at[a]) for a in range(n)]
        for cp in mine:
            cp.start()
        first = [over_ici(a, j, me) for a in range(n) for j in range(3)]
        for cp in first:
            cp.start()
        passed = []
        for a in range(n):
            for j, (px, py) in enumerate(chips):
                over_ici(a, j, 2 * px + py).wait_recv()
                passed.append(to_sibling(a, j, c))
                passed[-1].start()
        for a in range(n):
            for j in range(3):
                to_sibling(a, j, 1 - c).wait_recv()
        for cp in first + passed:
            cp.wait_send()
        for cp in mine:
            cp.wait()

    return pl.pallas_call(
        body, name="gather_weights",
        out_shape=[jax.ShapeDtypeStruct((N_CHIPS, *v.shape), v.dtype) for v in shards],
        in_specs=[ANY] * n, out_specs=[ANY] * n,
        scratch_shapes=_dma_sems(6 * n) + [pltpu.SemaphoreType.DMA((n,))],
    )(*shards)


def _swap_halves(parts):
    n = len(parts)

    def body(*refs):
        v_refs, out_refs = refs[:n], refs[n:2 * n]
        send_sems, recv_sems = refs[2 * n:]
        x, y, c = _position()
        copies = []
        for a in range(n):
            half = parts[a].shape[1] // 2
            copies.append(pltpu.make_async_remote_copy(
                src_ref=v_refs[a].at[:, pl.ds((1 - c) * half, half), :], dst_ref=out_refs[a],
                send_sem=send_sems.at[a], recv_sem=recv_sems.at[a], device_id=(x, y, 1 - c), device_id_type=MESH))
        for cp in copies:
            cp.start()
        for cp in copies:
            cp.wait()

    return pl.pallas_call(
        body, name="grad_swap_halves",
        out_shape=[jax.ShapeDtypeStruct((v.shape[0], v.shape[1] // 2, v.shape[2]), v.dtype) for v in parts],
        in_specs=[ANY] * n, out_specs=[ANY] * n,
        scratch_shapes=_dma_sems(n),
    )(*parts)


def _chip_all_to_all(parts):
    n = len(parts)

    def body(*refs):
        p_refs, out_refs = refs[:n], refs[n:2 * n]
        send_sems, recv_sems, local_sems = refs[2 * n:]
        x, y, c = _position()
        me = 2 * x + y
        chips = _other_chips(x, y)

        def copy(a, j, slab_from, slab_to):
            px, py = chips[j]
            return pltpu.make_async_remote_copy(
                src_ref=p_refs[a].at[slab_from], dst_ref=out_refs[a].at[slab_to],
                send_sem=send_sems.at[3 * a + j], recv_sem=recv_sems.at[3 * a + j],
                device_id=(px, py, c), device_id_type=MESH)

        mine = [pltpu.make_async_copy(p_refs[a].at[me], out_refs[a].at[me], local_sems.at[a]) for a in range(n)]
        for cp in mine:
            cp.start()
        sends = [copy(a, j, 2 * px + py, me) for a in range(n) for j, (px, py) in enumerate(chips)]
        for cp in sends:
            cp.start()
        for a in range(n):
            for j, (px, py) in enumerate(chips):
                copy(a, j, me, 2 * px + py).wait_recv()
        for cp in sends:
            cp.wait_send()
        for cp in mine:
            cp.wait()

    return pl.pallas_call(
        body, name="grad_all_to_all",
        out_shape=[jax.ShapeDtypeStruct(v.shape, v.dtype) for v in parts],
        in_specs=[ANY] * n, out_specs=[ANY] * n,
        scratch_shapes=_dma_sems(3 * n) + [pltpu.SemaphoreType.DMA((n,))],
    )(*parts)


def _join_halves(halves):
    n = len(halves)

    def body(*refs):
        h_refs, out_refs = refs[:n], refs[n:2 * n]
        send_sems, recv_sems, local_sems = refs[2 * n:]
        x, y, c = _position()
        copies, mine = [], []
        for a in range(n):
            half = halves[a].shape[0]
            rows = out_refs[a].at[pl.ds(c * half, half), :]
            mine.append(pltpu.make_async_copy(h_refs[a], rows, local_sems.at[a]))
            copies.append(pltpu.make_async_remote_copy(
                src_ref=h_refs[a], dst_ref=rows, send_sem=send_sems.at[a], recv_sem=recv_sems.at[a],
                device_id=(x, y, 1 - c), device_id_type=MESH))
        for cp in mine + copies:
            cp.start()
        for cp in copies + mine:
            cp.wait()

    return pl.pallas_call(
        body, name="grad_join_halves",
        out_shape=[jax.ShapeDtypeStruct((2 * v.shape[0], v.shape[1]), v.dtype) for v in halves],
        in_specs=[ANY] * n, out_specs=[ANY] * n,
        scratch_shapes=_dma_sems(n) + [pltpu.SemaphoreType.DMA((n,))],
    )(*halves)


ADD_ROWS = 128


def _add_halves(g, sw, core, name):
    n, rows, cols = g.shape
    half = rows // 2
    nb = half // ADD_ROWS

    def kern(c_ref, g_ref, s_ref, o_ref):
        o_ref[...] = g_ref[...] + s_ref[...]

    blk = pl.BlockSpec((1, ADD_ROWS, cols), lambda j, i, c_ref: (j, i, 0))
    return pl.pallas_call(
        kern, name=name,
        out_shape=jax.ShapeDtypeStruct((n, half, cols), F32),
        grid_spec=pltpu.PrefetchScalarGridSpec(
            num_scalar_prefetch=1, grid=(n, nb),
            in_specs=[pl.BlockSpec((1, ADD_ROWS, cols), lambda j, i, c_ref: (j, c_ref[0] * nb + i, 0)), blk],
            out_specs=blk),
        compiler_params=_cparams("parallel", "parallel"),
    )(core, g, sw)


def _sum_chips(rx, name):
    n, rows, cols = rx.shape

    def kern(r_ref, o_ref):
        o_ref[...] = ((r_ref[0] + r_ref[1]) + r_ref[2]) + r_ref[3]

    return pl.pallas_call(
        kern, name=name,
        out_shape=jax.ShapeDtypeStruct((rows, cols), F32),
        grid=(rows // ADD_ROWS,),
        in_specs=[pl.BlockSpec((n, ADD_ROWS, cols), lambda i: (0, i, 0))],
        out_specs=pl.BlockSpec((ADD_ROWS, cols), lambda i: (i, 0)),
        compiler_params=_cparams("parallel"),
    )(rx)


def _gather_small(v, reduce, name):
    rows = v.shape[0]

    def body(v_ref, out_ref, buf, send_sems, recv_sems):
        x, y, c = _position()
        me = 4 * x + 2 * y + c
        buf[me] = v_ref[...]
        peers = [(x ^ (k >> 2), y ^ ((k >> 1) & 1), c ^ (k & 1)) for k in range(1, 8)]
        copies = [pltpu.make_async_remote_copy(
            src_ref=v_ref, dst_ref=buf.at[me],
            send_sem=send_sems.at[k], recv_sem=recv_sems.at[k],
            device_id=peer, device_id_type=MESH) for k, peer in enumerate(peers)]
        for cp in copies:
            cp.start()
        for k, (px, py, pc) in enumerate(peers):
            pltpu.make_async_remote_copy(
                src_ref=v_ref, dst_ref=buf.at[4 * px + 2 * py + pc],
                send_sem=send_sems.at[k], recv_sem=recv_sems.at[k],
                device_id=(px, py, pc), device_id_type=MESH).wait_recv()
        for cp in copies:
            cp.wait_send()
        if reduce:
            total = buf[0]
            for d in range(1, 8):
                total = total + buf[d]
            out_ref[...] = total
        else:
            out_ref[...] = buf[...]

    vm = pl.BlockSpec(memory_space=pltpu.VMEM)
    return pl.pallas_call(
        body, name=name,
        out_shape=jax.ShapeDtypeStruct((rows, LANES) if reduce else (8, rows, LANES), F32),
        in_specs=[vm], out_specs=vm,
        scratch_shapes=[pltpu.VMEM((8, rows, LANES), F32), pltpu.SemaphoreType.DMA((7,)), pltpu.SemaphoreType.DMA((7,))],
    )(v)


def _pad_rows(a, rows):
    return jnp.pad(a, ((0, rows - a.shape[0]), (0, 0)))


def _lane_pad(v):
    n = v.shape[1]
    return jnp.pad(v, ((0, 0), (0, -n % LANES)))


def _gather_all(w_in, w_attn_out, w_ssm_out, w_o, conv_w):
    d = D_MODEL
    w_in_all, w_ao, w_so, w_oo = _gather_weights([a[0].astype(BF16) for a in (w_in, w_attn_out, w_ssm_out, w_o)])
    last = w_in_all[N_CHIPS - 1]
    w_proj = jnp.concatenate([w_in_all[0], w_in_all[1], w_in_all[2], last[:, :LAST_DT0], last[:, LAST_DT0 + 32:],
                              last[:, LAST_DT0:LAST_DT0 + 32], jnp.zeros((d, DT_PAD - 32), BF16)], axis=1)
    w_ao = w_ao.reshape(D_MODEL, d)
    w_so = w_so.reshape(SSD_WIDTH, d)
    w_oo = w_oo.reshape(D_MODEL, d)
    conv_rows = conv_w[0].size // LANES
    conv_all = _gather_small(conv_w[0].reshape(conv_rows, LANES), False, "gather_conv_w")
    conv_w_all = conv_all[0::2].reshape(N_CHIPS, CONV_K, CONV_DIM // N_CHIPS).transpose(1, 0, 2).reshape(CONV_K, CONV_DIM)

    return w_proj, w_ao, w_so, w_oo, conv_w_all


def _local_step(x, loss_target, norm_w, w_proj, conv_w_all, conv_b, dt_bias, a_log, d_skip, ssm_norm_w,
                w_ao, w_so, w_oo, final_norm_w):
    b, s, d = x.shape
    t = b * s
    g4, hg = SSD_GROUPS, HEADS_PER_GROUP
    dtb_g = _lane_pad(dt_bias.reshape(g4, hg)).reshape(g4, 1, LANES)
    alog_g = _lane_pad(a_log.reshape(g4, hg)).reshape(g4, 1, LANES)
    dskip_x = jnp.repeat(d_skip, HEAD_DIM, axis=1)
    fnw = final_norm_w.reshape(1, d)

    x2 = x.reshape(t, d)
    h = _rms_fwd(x2, norm_w)
    proj = _matmul(h, w_proj, tm=512, tn=1280, tk=1024, name="proj")
    proj3 = proj.reshape(b, s, NP)
    o3, yp3 = _attn_fwd(proj3)
    xact = _conv_fwd(proj3, conv_w_all, conv_b)
    dtr = proj3[:, :, DT0:DT0 + g4 * hg].reshape(b, s, g4, hg).transpose(0, 2, 1, 3)
    dtr_g = jnp.pad(dtr, ((0, 0), (0, 0), (0, 0), (0, LANES - hg)))
    y3, yn3, hst = _ssd_fwd(xact, proj3, dtr_g, dtb_g, alog_g, dskip_x, ssm_norm_w)
    yp = yp3.reshape(t, D_MODEL)
    yn = yn3.reshape(t, SSD_WIDTH)
    ya = _matmul(yp, w_ao, tm=512, tn=1024, tk=1024, name="attn_out")
    ys = _matmul(yn, w_so, tm=512, tn=1024, tk=2048, name="ssm_out")
    merged = _merge_fwd(proj, ya, ys)
    mo = _matmul(merged, w_oo, tm=512, tn=1024, tk=1024, name="out_proj")
    dout, doutb, loss_part, d_fnw = _final_fwd_bwd(x2, mo, loss_target.reshape(t, d), fnw)

    dmerged = _matmul(doutb, w_oo, tb=True, tm=512, tn=1024, tk=1024, name="d_merged")
    g_wo = _matmul(merged, doutb, ta=True, tm=512, tn=1024, tk=1024, name="g_w_o")
    dya, dys, dgate = _merge_bwd(dmerged, proj, ya, ys)
    dyp = _matmul(dya, w_ao, tb=True, tm=512, tn=1024, tk=1024, name="d_attn_pre")
    g_wao = _matmul(yp, dya, ta=True, tm=512, tn=1024, tk=1024, name="g_w_attn_out")
    dyn = _matmul(dys, w_so, tb=True, tm=512, tn=2048, tk=1024, name="d_ssm_norm")
    g_wso = _matmul(yn, dys, ta=True, tm=512, tn=1024, tk=1024, name="g_w_ssm_out")
    dq, dk, dv, dza = _attn_bwd(proj3, dyp.reshape(b, s, D_MODEL), o3)
    (dxs, dbm, dcm, dzs, ddtr_g, d_snw_g, d_alog_g, d_dtb_g, d_dsk_g) = _ssd_bwd(
        dyn.reshape(b, s, SSD_WIDTH), y3, xact, proj3, hst, dtr_g, dtb_g, alog_g, dskip_x, ssm_norm_w)
    dx_xs, g_cw_xs, g_cb_xs = _conv_bwd(dxs, proj3, conv_w_all, conv_b, 0, "conv_bwd_x")
    dx_bm, g_cw_bm, g_cb_bm = _conv_bwd(dbm, proj3, conv_w_all, conv_b, SSD_WIDTH, "conv_bwd_b")
    dx_cm, g_cw_cm, g_cb_cm = _conv_bwd(dcm, proj3, conv_w_all, conv_b, SSD_WIDTH + g4 * SSD_STATE, "conv_bwd_c")
    ddt = ddtr_g[:, :, :, :hg].transpose(0, 2, 1, 3).reshape(b, s, g4 * hg).astype(BF16)
    dproj = jnp.concatenate([dq, dk, dv, dza, dzs, dx_xs, dx_bm, dx_cm, dgate.reshape(b, s, 2 * D_MODEL),
                             jnp.pad(ddt, ((0, 0), (0, 0), (0, DT_PAD - g4 * hg)))], axis=2).reshape(t, NP)
    g_wproj = _matmul(h, dproj, ta=True, tm=512, tn=1280, tk=1024, name="g_w_in")
    dh = _matmul(dproj, w_proj, tb=True, tm=512, tn=1024, tk=1280, name="d_h")
    grad_x, d_nw = _rms_bwd(dh, x2, norm_w, dout)
    g_cw = jnp.concatenate([g_cw_xs, g_cw_bm, g_cw_cm], axis=1)
    g_cb = jnp.concatenate([g_cb_xs, g_cb_bm, g_cb_cm], axis=1)
    return (loss_part, grad_x, d_nw, g_wproj, g_cw, g_cb, d_dtb_g, d_alog_g, d_dsk_g, d_snw_g, g_wao, g_wso, g_wo, d_fnw)


def kernel(x, norm_w, w_in, conv_w, conv_b, dt_bias, a_log, d_skip, ssm_norm_w, w_attn_out, w_ssm_out, w_o, final_norm_w, loss_target, m_norm_w, m_w_in, m_conv_w, m_conv_b, m_dt_bias, m_a_log, m_d_skip, m_ssm_norm_w, m_w_attn_out, m_w_ssm_out, m_w_o, m_final_norm_w, v_norm_w, v_w_in, v_conv_w, v_conv_b, v_dt_bias, v_a_log, v_d_skip, v_ssm_norm_w, v_w_attn_out, v_w_ssm_out, v_w_o, v_final_norm_w):
    b, s, d = x.shape
    core = lax.axis_index("c")
    g4, hg = SSD_GROUPS, HEADS_PER_GROUP
    shard_cols = w_in.shape[2]
    w_proj, w_ao, w_so, w_oo, conv_w_all = _gather_all(w_in, w_attn_out, w_ssm_out, w_o, conv_w)
    (loss_part, grad_x, d_nw, g_wproj, g_cw, g_cb, d_dtb_g, d_alog_g, d_dsk_g, d_snw_g, g_wao, g_wso, g_wo, d_fnw) = _local_step(
        x, loss_target, norm_w, w_proj, conv_w_all, conv_b, dt_bias, a_log, d_skip, ssm_norm_w, w_ao, w_so, w_oo, final_norm_w)

    last0 = (N_CHIPS - 1) * shard_cols
    g_last = jnp.concatenate([g_wproj[:, last0:GATE0], g_wproj[:, DT0:DT0 + 32], g_wproj[:, GATE0:DT0]], axis=1)
    g_win_chips = jnp.stack([g_wproj[:, j * shard_cols:(j + 1) * shard_cols] for j in range(N_CHIPS - 1)] + [g_last])
    g_out_chips = jnp.concatenate([g.reshape(N_CHIPS, -1, d) for g in (g_wao, g_wso, g_wo)], axis=1)
    parts = [g_win_chips, g_out_chips]
    core_id = core.reshape(1).astype(jnp.int32)
    from_sibling = _swap_halves(parts)
    chip_sums = [_add_halves(p, f, core_id, "grad_add_halves_%d" % i) for i, (p, f) in enumerate(zip(parts, from_sibling))]
    from_chips = _chip_all_to_all(chip_sums)
    halves = [_sum_chips(r, "grad_sum_chips_%d" % i) for i, r in enumerate(from_chips)]
    g_w_in, g_out = _join_halves(halves)

    small = jnp.concatenate([
        loss_part, d_nw, g_cb, _lane_pad(d_dtb_g[:, 0, :hg].reshape(1, -1)), _lane_pad(d_alog_g[:, 0, :hg].reshape(1, -1)),
        _lane_pad(d_dsk_g[:, 0, :hg].reshape(1, -1)),
        d_snw_g.reshape(1, -1), d_fnw, g_cw.reshape(1, -1)], axis=1)
    small_rows = small.shape[1] // LANES
    reduced = _gather_small(_pad_rows(small.reshape(small_rows, LANES), -(-small_rows // 8) * 8), True, "reduce_small")
    flat = reduced.reshape(-1)

    def take(start, n):
        return flat[start:start + n].reshape(1, n)

    loss = flat[0]
    pos = LANES
    g_norm_w = take(pos, d); pos += d
    g_conv_b = take(pos, CONV_DIM); pos += CONV_DIM
    g_dt_bias = take(pos, g4 * hg); pos += LANES
    g_a_log = take(pos, g4 * hg); pos += LANES
    g_d_skip = take(pos, g4 * hg); pos += LANES
    g_ssm_norm_w = take(pos, SSD_WIDTH); pos += SSD_WIDTH
    g_final_norm_w = take(pos, d); pos += d
    conv_cols = CONV_DIM // N_CHIPS
    chip = 2 * lax.axis_index("x") + lax.axis_index("y")
    g_conv_w = lax.dynamic_slice_in_dim(flat[pos:pos + CONV_K * CONV_DIM].reshape(CONV_K, CONV_DIM), chip * conv_cols, conv_cols, axis=1)

    rows_ao, rows_so = D_MODEL // N_CHIPS, SSD_WIDTH // N_CHIPS
    g_w_attn_out = g_out[:rows_ao]
    g_w_ssm_out = g_out[rows_ao:rows_ao + rows_so]
    g_w_o = g_out[rows_ao + rows_so:]

    names = ["norm_w", "w_in", "conv_w", "conv_b", "dt_bias", "a_log", "d_skip", "ssm_norm_w",
             "w_attn_out", "w_ssm_out", "w_o", "final_norm_w"]
    weights = [norm_w, w_in, conv_w, conv_b, dt_bias, a_log, d_skip, ssm_norm_w, w_attn_out, w_ssm_out, w_o, final_norm_w]
    grads = [g_norm_w, g_w_in, g_conv_w, g_conv_b, g_dt_bias, g_a_log, g_d_skip, g_ssm_norm_w,
             g_w_attn_out, g_w_ssm_out, g_w_o, g_final_norm_w]
    ms = [m_norm_w, m_w_in, m_conv_w, m_conv_b, m_dt_bias, m_a_log, m_d_skip, m_ssm_norm_w,
          m_w_attn_out, m_w_ssm_out, m_w_o, m_final_norm_w]
    vs = [v_norm_w, v_w_in, v_conv_w, v_conv_b, v_dt_bias, v_a_log, v_d_skip, v_ssm_norm_w,
          v_w_attn_out, v_w_ssm_out, v_w_o, v_final_norm_w]
    out_g, out_d, out_m, out_v = [], [], [], []
    for name, w, g, m, v in zip(names, weights, grads, ms, vs):
        shape2 = g.shape
        dlt, nm, nv = _adamw(w.reshape(shape2), g, m.reshape(shape2), v.reshape(shape2), "adamw_" + name)
        out_g.append(g.reshape(w.shape))
        out_d.append(dlt.reshape(w.shape))
        out_m.append(nm.reshape(w.shape))
        out_v.append(nv.reshape(w.shape))

    return (loss, grad_x.reshape(b, s, d), *out_g, *out_d, *out_m, *out_v)
```

```python
import jax
import jax.numpy as jnp
from jax import lax
from jax.experimental import pallas as pl
from jax.experimental.pallas import tpu as pltpu

F32 = jnp.float32
BF16 = jnp.bfloat16
HIGHEST = lax.Precision.HIGHEST
MESH = pl.DeviceIdType.MESH

D_MODEL = 1024
SB_HEADS = 16
HEAD_DIM = 64
SSD_WIDTH = 2048
SSD_GROUPS = 4
GROUP_WIDTH = SSD_WIDTH // SSD_GROUPS
HEADS_PER_GROUP = 8
SSD_STATE = 128
CHUNK = 128
CONV_K = 4
CONV_DIM = 3072
D_PROJ = 11296
EPS = 1e-6
ADAM_LR, ADAM_B1, ADAM_B2, ADAM_EPS, ADAM_WD, ADAM_STEP = 0.001, 0.9, 0.999, 1e-08, 0.01, 10

LANES = 128
Q0, K0, V0, ZA0, ZS0, XBC0, GATE0, DT0 = 0, 1024, 2048, 3072, 4096, 6144, 9216, 11264
DT_PAD = 256
NP = DT0 + DT_PAD
N_CHIPS = 4
LAST_DT0 = GATE0 - (N_CHIPS - 1) * (D_PROJ // N_CHIPS)
VMEM_LIMIT = 56 * 1024 * 1024


def _cparams(*sem):
    return pltpu.CompilerParams(dimension_semantics=sem or None, vmem_limit_bytes=VMEM_LIMIT)


def _sigmoid(z):
    return 1.0 / (1.0 + jnp.exp(-z))


def _dot(a, b, dims, precision=None):
    return lax.dot_general(a, b, (dims, ((), ())), preferred_element_type=F32, precision=precision)


NN = ((1,), (0,))
NT = ((1,), (1,))
TN = ((0,), (0,))


def _matmul(a, b, *, ta=False, tb=False, out_dtype=F32, tm, tn, tk, name):
    m, k = (a.shape[1], a.shape[0]) if ta else a.shape
    n = b.shape[0] if tb else b.shape[1]
    assert m % tm == 0 and n % tn == 0 and k % tk == 0, (name, m, n, k)
    nk = k // tk
    use_scratch = out_dtype != F32
    dims = ((0,) if ta else (1,), (1,) if tb else (0,))

    def kern(a_ref, b_ref, o_ref, *scratch):
        acc = scratch[0] if use_scratch else o_ref
        kk = pl.program_id(2)

        @pl.when(kk == 0)
        def _():
            acc[...] = jnp.zeros_like(acc)

        acc[...] += _dot(a_ref[...], b_ref[...], dims)
        if use_scratch:
            @pl.when(kk == nk - 1)
            def _():
                o_ref[...] = acc[...].astype(out_dtype)

    a_spec = pl.BlockSpec((tk, tm), lambda i, j, q: (q, i)) if ta else pl.BlockSpec((tm, tk), lambda i, j, q: (i, q))
    b_spec = pl.BlockSpec((tn, tk), lambda i, j, q: (j, q)) if tb else pl.BlockSpec((tk, tn), lambda i, j, q: (q, j))
    return pl.pallas_call(
        kern, name=name,
        out_shape=jax.ShapeDtypeStruct((m, n), out_dtype),
        grid=(m // tm, n // tn, nk),
        in_specs=[a_spec, b_spec],
        out_specs=pl.BlockSpec((tm, tn), lambda i, j, q: (i, j)),
        scratch_shapes=[pltpu.VMEM((tm, tn), F32)] if use_scratch else [],
        compiler_params=_cparams("parallel", "parallel", "arbitrary"),
    )(a, b)


ROWS = 256


def _rms_fwd(x2, w):
    t, d = x2.shape

    def kern(x_ref, w_ref, h_ref):
        x = x_ref[...]
        r = lax.rsqrt(jnp.mean(x * x, axis=-1, keepdims=True) + EPS)
        h_ref[...] = (x * r * w_ref[...]).astype(BF16)

    return pl.pallas_call(
        kern, name="rms_fwd",
        out_shape=jax.ShapeDtypeStruct((t, d), BF16),
        grid=(t // ROWS,),
        in_specs=[pl.BlockSpec((ROWS, d), lambda i: (i, 0)), pl.BlockSpec((1, d), lambda i: (0, 0))],
        out_specs=pl.BlockSpec((ROWS, d), lambda i: (i, 0)),
        compiler_params=_cparams("parallel"),
    )(x2, w)


def _rms_bwd(dh, x2, w, dout):
    t, d = x2.shape

    def kern(dh_ref, x_ref, w_ref, dout_ref, gx_ref, dw_ref):
        @pl.when(pl.program_id(0) == 0)
        def _():
            dw_ref[...] = jnp.zeros_like(dw_ref)

        x = x_ref[...]
        r = lax.rsqrt(jnp.mean(x * x, axis=-1, keepdims=True) + EPS)
        xh = x * r
        g = dh_ref[...]
        dw_ref[...] += jnp.sum(g * xh, axis=0, keepdims=True)
        gw = g * w_ref[...]
        gx_ref[...] = dout_ref[...] + r * (gw - xh * jnp.mean(gw * xh, axis=-1, keepdims=True))

    row = pl.BlockSpec((ROWS, d), lambda i: (i, 0))
    vec = pl.BlockSpec((1, d), lambda i: (0, 0))
    return pl.pallas_call(
        kern, name="rms_bwd",
        out_shape=(jax.ShapeDtypeStruct((t, d), F32), jax.ShapeDtypeStruct((1, d), F32)),
        grid=(t // ROWS,),
        in_specs=[row, row, vec, row],
        out_specs=(row, vec),
        compiler_params=_cparams("arbitrary"),
    )(dh, x2, w, dout)


def _final_fwd_bwd(x2, mo, target, w):
    t, d = x2.shape

    def kern(x_ref, mo_ref, t_ref, w_ref, dout_ref, doutb_ref, loss_ref, dw_ref):
        @pl.when(pl.program_id(0) == 0)
        def _():
            loss_ref[...] = jnp.zeros_like(loss_ref)
            dw_ref[...] = jnp.zeros_like(dw_ref)

        u = x_ref[...] + mo_ref[...]
        r = lax.rsqrt(jnp.mean(u * u, axis=-1, keepdims=True) + EPS)
        uh = u * r
        wv = w_ref[...]
        err = uh * wv - t_ref[...]
        loss_ref[...] += (0.5 / d) * jnp.sum(err * err)
        dy = err * (1.0 / d)
        dw_ref[...] += jnp.sum(dy * uh, axis=0, keepdims=True)
        gw = dy * wv
        du = r * (gw - uh * jnp.mean(gw * uh, axis=-1, keepdims=True))
        dout_ref[...] = du
        doutb_ref[...] = du.astype(BF16)

    row = pl.BlockSpec((ROWS, d), lambda i: (i, 0))
    vec = pl.BlockSpec((1, d), lambda i: (0, 0))
    return pl.pallas_call(
        kern, name="final_fwd_bwd",
        out_shape=(jax.ShapeDtypeStruct((t, d), F32), jax.ShapeDtypeStruct((t, d), BF16),
                   jax.ShapeDtypeStruct((1, LANES), F32), jax.ShapeDtypeStruct((1, d), F32)),
        grid=(t // ROWS,),
        in_specs=[row, row, row, vec],
        out_specs=(row, row, pl.BlockSpec((1, LANES), lambda i: (0, 0)), vec),
        compiler_params=_cparams("arbitrary"),
    )(x2, mo, target, w)


def _merge_fwd(proj2, ya, ys):
    t = ya.shape[0]
    gblk = GATE0 // D_MODEL

    def kern(ga_ref, gs_ref, ya_ref, ys_ref, o_ref):
        o_ref[...] = (_sigmoid(ga_ref[...]) * ya_ref[...] + _sigmoid(gs_ref[...]) * ys_ref[...]).astype(BF16)

    row = pl.BlockSpec((ROWS, D_MODEL), lambda i: (i, 0))
    return pl.pallas_call(
        kern, name="merge_fwd",
        out_shape=jax.ShapeDtypeStruct((t, D_MODEL), BF16),
        grid=(t // ROWS,),
        in_specs=[pl.BlockSpec((ROWS, D_MODEL), lambda i: (i, gblk)),
                  pl.BlockSpec((ROWS, D_MODEL), lambda i: (i, gblk + 1)), row, row],
        out_specs=row,
        compiler_params=_cparams("parallel"),
    )(proj2, proj2, ya, ys)


def _merge_bwd(dm, proj2, ya, ys):
    t = ya.shape[0]
    gblk = GATE0 // D_MODEL

    def kern(dm_ref, ga_ref, gs_ref, ya_ref, ys_ref, dya_ref, dys_ref, dg_ref):
        g = dm_ref[...]
        sa = _sigmoid(ga_ref[...])
        ss = _sigmoid(gs_ref[...])
        dya_ref[...] = (g * sa).astype(BF16)
        dys_ref[...] = (g * ss).astype(BF16)
        dg_ref[:, :D_MODEL] = (g * ya_ref[...] * sa * (1.0 - sa)).astype(BF16)
        dg_ref[:, D_MODEL:] = (g * ys_ref[...] * ss * (1.0 - ss)).astype(BF16)

    row = pl.BlockSpec((ROWS, D_MODEL), lambda i: (i, 0))
    return pl.pallas_call(
        kern, name="merge_bwd",
        out_shape=(jax.ShapeDtypeStruct((t, D_MODEL), BF16), jax.ShapeDtypeStruct((t, D_MODEL), BF16),
                   jax.ShapeDtypeStruct((t, 2 * D_MODEL), BF16)),
        grid=(t // ROWS,),
        in_specs=[row, pl.BlockSpec((ROWS, D_MODEL), lambda i: (i, gblk)),
                  pl.BlockSpec((ROWS, D_MODEL), lambda i: (i, gblk + 1)), row, row],
        out_specs=(row, row, pl.BlockSpec((ROWS, 2 * D_MODEL), lambda i: (i, 0))),
        compiler_params=_cparams("parallel"),
    )(dm, proj2, proj2, ya, ys)


TQ = 256
TK = 256
HEAD_LANES = (slice(0, HEAD_DIM), slice(HEAD_DIM, 2 * HEAD_DIM))


def _tri(pred):
    r = lax.broadcasted_iota(jnp.int32, (TK, TK), 0)
    c = lax.broadcasted_iota(jnp.int32, (TK, TK), 1)
    return pred(r, c).astype(BF16)


def _split_bf16(v):
    hi = v.astype(BF16)
    lo = (v - hi.astype(F32)).astype(BF16)
    return hi, lo


def _tri_dot(v, tri):
    hi, lo = _split_bf16(v)
    return _dot(hi, tri, NN) + _dot(lo, tri, NN)


def _sb_logs(z, mask):
    l1p = jnp.log(1.0 + jnp.exp(-jnp.abs(z)))
    lb = jnp.minimum(z, 0.0) - l1p
    lom = -jnp.maximum(z, 0.0) - l1p
    if mask is not None:
        lom = jnp.where(mask, lom, 0.0)
    return lb, lom


def _sb_weights(lb, later, carry_r, mask):
    a = jnp.exp(lb + (later + carry_r))
    if mask is not None:
        a = jnp.where(mask, a, 0.0)
    return a


def _split_heads(dst, src, scale=None):
    for h, lanes in enumerate(HEAD_LANES):
        v = src[:, lanes]
        dst[h] = (v if scale is None else v * scale).astype(BF16)


def _attn_fwd(proj3):
    b, s, _ = proj3.shape
    nq = s // TQ
    scale = HEAD_DIM ** -0.5

    def kern(q_ref, k_ref, v_ref, za_ref, o_ref, yp_ref, qs, ks, vs):
        _split_heads(qs, q_ref[0], scale)
        _split_heads(ks, k_ref[0])
        _split_heads(vs, v_ref[0])
        row = lax.broadcasted_iota(jnp.int32, (TQ, TK), 0)
        col = lax.broadcasted_iota(jnp.int32, (TQ, TK), 1)
        tri_gt = _tri(lambda j, sk: j > sk)

        def q_block(i, _):
            r0 = pl.multiple_of(i * TQ, TQ)
            n_kb = (r0 + TQ + TK - 1) // TK
            qh = [qs[h, pl.ds(r0, TQ), :] for h in range(2)]

            def k_block(c0, carry, mask):
                kh = [ks[h, pl.ds(c0, TK), :] for h in range(2)]
                vh = [vs[h, pl.ds(c0, TK), :] for h in range(2)]
                z = [_dot(qh[h], kh[h], NT) for h in range(2)]
                logs, later = [], []
                for h in range(2):
                    logs.append(_sb_logs(z[h], mask))
                    later.append(_tri_dot(logs[h][1], tri_gt))
                out = []
                for h in range(2):
                    carry_r, acc = carry[h]
                    lb, lom = logs[h]
                    a = _sb_weights(lb, later[h], carry_r, mask)
                    row_sum = later[h][:, 0:1] + lom[:, 0:1]
                    out.append((carry_r + row_sum, acc + _dot(a.astype(BF16), vh[h], NN)))
                return tuple(out)

            c_last = pl.multiple_of((n_kb - 1) * TK, TK)
            start = (jnp.zeros((TQ, 1), F32), jnp.zeros((TQ, HEAD_DIM), F32))
            carry = k_block(c_last, (start, start), col + c_last < row + r0)

            def unmasked(jj, carry):
                return k_block(pl.multiple_of((n_kb - 2 - jj) * TK, TK), carry, None)

            carry = lax.fori_loop(0, n_kb - 1, unmasked, carry)
            for (_, acc), lanes in zip(carry, HEAD_LANES):
                o_ref[0, pl.ds(r0, TQ), lanes] = acc
                za = za_ref[0, pl.ds(r0, TQ), lanes]
                yp_ref[0, pl.ds(r0, TQ), lanes] = (acc * (za * _sigmoid(za))).astype(BF16)
            return 0

        lax.fori_loop(0, nq, q_block, 0)

    def spec(c0):
        return pl.BlockSpec((1, s, LANES), lambda bi, hp: (bi, 0, c0 // LANES + hp))

    out_spec = pl.BlockSpec((1, s, LANES), lambda bi, hp: (bi, 0, hp))
    return pl.pallas_call(
        kern, name="attn_fwd",
        out_shape=(jax.ShapeDtypeStruct((b, s, D_MODEL), F32), jax.ShapeDtypeStruct((b, s, D_MODEL), BF16)),
        grid=(b, SB_HEADS // 2),
        in_specs=[spec(Q0), spec(K0), spec(V0), spec(ZA0)],
        out_specs=(out_spec, out_spec),
        scratch_shapes=[pltpu.VMEM((2, s, HEAD_DIM), BF16)] * 3,
        compiler_params=_cparams("parallel", "parallel"),
    )(proj3, proj3, proj3, proj3)


def _attn_bwd(proj3, dyp3, o3):
    b, s, _ = proj3.shape
    nq = s // TQ
    scale = HEAD_DIM ** -0.5

    def kern(q_ref, k_ref, v_ref, za_ref, dyp_ref, o_ref, dq_ref, dk_ref, dv_ref, dza_ref,
             qs, ks, vs, dos, dk_acc, dv_acc):
        _split_heads(qs, q_ref[0], scale)
        _split_heads(ks, k_ref[0])
        _split_heads(vs, v_ref[0])
        za = za_ref[0]
        sg = _sigmoid(za)
        dyp = dyp_ref[0]
        _split_heads(dos, dyp * (za * sg))
        dza_ref[0] = (dyp * o_ref[0] * (sg * (1.0 + za * (1.0 - sg)))).astype(BF16)
        dk_acc[...] = jnp.zeros_like(dk_acc)
        dv_acc[...] = jnp.zeros_like(dv_acc)
        row = lax.broadcasted_iota(jnp.int32, (TQ, TK), 0)
        col = lax.broadcasted_iota(jnp.int32, (TQ, TK), 1)
        tri_gt = _tri(lambda j, sk: j > sk)
        tri_ge = _tri(lambda j, sk: j >= sk)

        def q_block(i, _):
            r0 = pl.multiple_of(i * TQ, TQ)
            n_kb = (r0 + TQ + TK - 1) // TK
            qh = [qs[h, pl.ds(r0, TQ), :] for h in range(2)]
            doh = [dos[h, pl.ds(r0, TQ), :] for h in range(2)]
            totals = [jnp.sum(doh[h].astype(F32) * o_ref[0, pl.ds(r0, TQ), lanes], axis=1, keepdims=True)
                      for h, lanes in enumerate(HEAD_LANES)]

            def k_block(c0, carry, mask):
                kh = [ks[h, pl.ds(c0, TK), :] for h in range(2)]
                vh = [vs[h, pl.ds(c0, TK), :] for h in range(2)]
                z = [_dot(qh[h], kh[h], NT) for h in range(2)]
                da = [_dot(doh[h], vh[h], NT) for h in range(2)]
                logs, later = [], []
                for h in range(2):
                    logs.append(_sb_logs(z[h], mask))
                    later.append(_tri_dot(logs[h][1], tri_gt))
                ab, g, suffix = [], [], []
                for h in range(2):
                    a = _sb_weights(logs[h][0], later[h], carry[h][0], mask)
                    ab.append(a.astype(BF16))
                    g.append(da[h] * ab[h].astype(F32))
                    suffix.append(_tri_dot(g[h], tri_ge))
                out = []
                for h in range(2):
                    carry_r, carry_g, dq = carry[h]
                    lb, lom = logs[h]
                    dz = g[h] - (g[h] + (totals[h] - carry_g) - suffix[h]) * jnp.exp(lb)
                    if mask is not None:
                        dz = jnp.where(mask, dz, 0.0)
                    dzb = dz.astype(BF16)
                    dk_acc[h, pl.ds(c0, TK), :] += _dot(dzb, qh[h], TN)
                    dv_acc[h, pl.ds(c0, TK), :] += _dot(ab[h], doh[h], TN)
                    out.append((carry_r + (later[h][:, 0:1] + lom[:, 0:1]), carry_g + suffix[h][:, 0:1],
                                dq + _dot(dzb, kh[h], NN)))
                return tuple(out)

            c_last = pl.multiple_of((n_kb - 1) * TK, TK)
            zero = jnp.zeros((TQ, 1), F32)
            start = (zero, zero, jnp.zeros((TQ, HEAD_DIM), F32))
            carry = k_block(c_last, (start, start), col + c_last < row + r0)

            def unmasked(jj, carry):
                return k_block(pl.multiple_of((n_kb - 2 - jj) * TK, TK), carry, None)

            carry = lax.fori_loop(0, n_kb - 1, unmasked, carry)
            for (_, _, dq), lanes in zip(carry, HEAD_LANES):
                dq_ref[0, pl.ds(r0, TQ), lanes] = (dq * scale).astype(BF16)
            return 0

        lax.fori_loop(0, nq, q_block, 0)

        for h, lanes in enumerate(HEAD_LANES):
            dk_ref[0, :, lanes] = dk_acc[h].astype(BF16)
            dv_ref[0, :, lanes] = dv_acc[h].astype(BF16)

    def spec(c0):
        return pl.BlockSpec((1, s, LANES), lambda bi, hp: (bi, 0, c0 // LANES + hp))

    plain = pl.BlockSpec((1, s, LANES), lambda bi, hp: (bi, 0, hp))
    out = jax.ShapeDtypeStruct((b, s, D_MODEL), BF16)
    return pl.pallas_call(
        kern, name="attn_bwd",
        out_shape=(out, out, out, out),
        grid=(b, SB_HEADS // 2),
        in_specs=[spec(Q0), spec(K0), spec(V0), spec(ZA0), plain, plain],
        out_specs=(plain, plain, plain, plain),
        scratch_shapes=[pltpu.VMEM((2, s, HEAD_DIM), BF16)] * 4 + [pltpu.VMEM((2, s, HEAD_DIM), F32)] * 2,
        compiler_params=_cparams("parallel", "parallel"),
    )(proj3, proj3, proj3, proj3, dyp3, o3)


CONV_COLS = 256
HALO = 8


def _conv_pre(xp, w_ref, b_ref, r0):
    pre = b_ref[...] + w_ref[CONV_K - 1:CONV_K, :] * xp[pl.ds(HALO + r0, CHUNK), :]
    for kk in range(1, CONV_K):
        pre = pre + w_ref[CONV_K - 1 - kk:CONV_K - kk, :] * xp[pl.ds(HALO + r0 - kk, CHUNK), :]
    return pre


def _conv_fwd(proj3, conv_w, conv_b):
    b, s, _ = proj3.shape
    nc = s // CHUNK

    def kern(x_ref, w_ref, b_ref, o_ref, xp):
        xp[0:HALO, :] = jnp.zeros((HALO, CONV_COLS), F32)
        xp[HALO:, :] = x_ref[0]
        for ci in range(nc):
            pre = _conv_pre(xp, w_ref, b_ref, ci * CHUNK)
            o_ref[0, ci * CHUNK:(ci + 1) * CHUNK, :] = pre * _sigmoid(pre)

    return pl.pallas_call(
        kern, name="conv_fwd",
        out_shape=jax.ShapeDtypeStruct((b, s, CONV_DIM), F32),
        grid=(CONV_DIM // CONV_COLS, b),
        in_specs=[pl.BlockSpec((1, s, CONV_COLS), lambda j, bi: (bi, 0, XBC0 // CONV_COLS + j)),
                  pl.BlockSpec((CONV_K, CONV_COLS), lambda j, bi: (0, j)),
                  pl.BlockSpec((1, CONV_COLS), lambda j, bi: (0, j))],
        out_specs=pl.BlockSpec((1, s, CONV_COLS), lambda j, bi: (bi, 0, j)),
        scratch_shapes=[pltpu.VMEM((s + HALO, CONV_COLS), F32)],
        compiler_params=_cparams("parallel", "parallel"),
    )(proj3, conv_w, conv_b)


def _conv_bwd(dact, proj3, conv_w, conv_b, col0, name):
    b, s, width = dact.shape
    nc = s // CHUNK
    j0 = col0 // CONV_COLS

    def kern(da_ref, x_ref, w_ref, b_ref, dx_ref, dw_ref, db_ref, xp, dp):
        @pl.when(pl.program_id(1) == 0)
        def _():
            dw_ref[...] = jnp.zeros_like(dw_ref)
            db_ref[...] = jnp.zeros_like(db_ref)

        xp[0:HALO, :] = jnp.zeros((HALO, CONV_COLS), F32)
        xp[HALO:, :] = x_ref[0]
        dp[s:, :] = jnp.zeros((HALO, CONV_COLS), F32)
        for ci in range(nc):
            r0 = ci * CHUNK
            pre = _conv_pre(xp, w_ref, b_ref, r0)
            sg = _sigmoid(pre)
            dpre = da_ref[0, r0:r0 + CHUNK, :] * (sg * (1.0 + pre * (1.0 - sg)))
            dp[r0:r0 + CHUNK, :] = dpre
            db_ref[...] += jnp.sum(dpre, axis=0, keepdims=True)
            for kk in range(CONV_K):
                tap = CONV_K - 1 - kk
                dw_ref[tap:tap + 1, :] += jnp.sum(dpre * xp[pl.ds(HALO + r0 - kk, CHUNK), :], axis=0, keepdims=True)
        for ci in range(nc):
            r0 = ci * CHUNK
            dx = w_ref[CONV_K - 1:CONV_K, :] * dp[pl.ds(r0, CHUNK), :]
            for kk in range(1, CONV_K):
                dx = dx + w_ref[CONV_K - 1 - kk:CONV_K - kk, :] * dp[pl.ds(r0 + kk, CHUNK), :]
            dx_ref[0, r0:r0 + CHUNK, :] = dx.astype(BF16)

    return pl.pallas_call(
        kern, name=name,
        out_shape=(jax.ShapeDtypeStruct((b, s, width), BF16), jax.ShapeDtypeStruct((CONV_K, width), F32),
                   jax.ShapeDtypeStruct((1, width), F32)),
        grid=(width // CONV_COLS, b),
        in_specs=[pl.BlockSpec((1, s, CONV_COLS), lambda j, bi: (bi, 0, j)),
                  pl.BlockSpec((1, s, CONV_COLS), lambda j, bi: (bi, 0, XBC0 // CONV_COLS + j0 + j)),
                  pl.BlockSpec((CONV_K, CONV_COLS), lambda j, bi: (0, j0 + j)),
                  pl.BlockSpec((1, CONV_COLS), lambda j, bi: (0, j0 + j))],
        out_specs=(pl.BlockSpec((1, s, CONV_COLS), lambda j, bi: (bi, 0, j)),
                   pl.BlockSpec((CONV_K, CONV_COLS), lambda j, bi: (0, j)),
                   pl.BlockSpec((1, CONV_COLS), lambda j, bi: (0, j))),
        scratch_shapes=[pltpu.VMEM((s + HALO, CONV_COLS), F32)] * 2,
        compiler_params=_cparams("parallel", "arbitrary"),
    )(dact, proj3, conv_w, conv_b)


def _ssd_common(dtr_ref, dtb_ref, alog_ref):
    lane = lax.broadcasted_iota(jnp.int32, (CHUNK, LANES), 1)
    row = lax.broadcasted_iota(jnp.int32, (CHUNK, LANES), 0)
    head_lane = lane < HEADS_PER_GROUP
    pre = dtr_ref[0, 0] + dtb_ref[0]
    dt = jnp.where(head_lane, jnp.maximum(pre, 0.0) + jnp.log(1.0 + jnp.exp(-jnp.abs(pre))), 0.0)
    a = jnp.where(head_lane[0:1], -jnp.exp(alog_ref[0]), 0.0)
    tril = (row >= lane).astype(F32)
    acs = _dot(tril, dt * a, NN, HIGHEST)
    acs_t = acs.T
    er = lax.broadcasted_iota(jnp.int32, (LANES, GROUP_WIDTH), 0)
    ec = lax.broadcasted_iota(jnp.int32, (LANES, GROUP_WIDTH), 1)
    expand = ((ec // HEAD_DIM) == er).astype(F32)
    tr = lax.broadcasted_iota(jnp.int32, (GROUP_WIDTH, LANES), 0)
    tc = lax.broadcasted_iota(jnp.int32, (GROUP_WIDTH, LANES), 1)
    reduce = ((tr // HEAD_DIM) == tc).astype(F32)
    dt_x = _dot(dt, expand, NN, HIGHEST)
    acs_x = _dot(acs, expand, NN, HIGHEST)
    end_x = acs_x[CHUNK - 1:CHUNK, :]
    end_col = jnp.broadcast_to(acs_t[:, CHUNK - 1:CHUNK], (LANES, LANES))
    chunk_decay = jnp.exp(_dot(reduce, end_col, NN, HIGHEST))
    causal = row >= lane
    return dict(dt=dt, a=a, pre=pre, head_lane=head_lane, acs=acs, acs_t=acs_t, expand=expand, reduce=reduce,
                dt_x=dt_x, acs_x=acs_x, end_x=end_x, chunk_decay=chunk_decay, causal=causal, row=row, lane=lane)


def _ssd_decay(cm, h):
    seg = cm["acs"][:, h:h + 1] - cm["acs_t"][h:h + 1, :]
    return jnp.where(cm["causal"], jnp.exp(jnp.minimum(seg, 0.0)), 0.0)


def _ssd_fwd(xact, proj3, dtr_g, dtb_g, alog_g, dskip_x, snw):
    b, s, _ = xact.shape
    nc = s // CHUNK
    g4 = SSD_GROUPS

    def kern(xs_ref, bm_ref, cm_ref, zs_ref, dtr_ref, dtb_ref, alog_ref, dsk_ref, snw_ref,
             y_ref, yn_ref, hst_ref, h_sc):
        @pl.when(pl.program_id(2) == 0)
        def _():
            h_sc[...] = jnp.zeros_like(h_sc)

        cm = _ssd_common(dtr_ref, dtb_ref, alog_ref)
        x = xs_ref[0]
        bmb = bm_ref[0].astype(BF16)
        cmb = cm_ref[0].astype(BF16)
        h_in = h_sc[...]
        hst_ref[0, 0, 0] = h_in
        xdt = x * cm["dt_x"]
        xdtb = xdt.astype(BF16)
        cb = _dot(cmb, bmb, NT)
        y_off = _dot(cmb, h_in.astype(BF16), NT) * jnp.exp(cm["acs_x"])
        for h in range(HEADS_PER_GROUP):
            lanes = slice(h * HEAD_DIM, (h + 1) * HEAD_DIM)
            m = (cb * _ssd_decay(cm, h)).astype(BF16)
            y_ref[0, :, lanes] = _dot(m, xdtb[:, lanes], NN)
        y = y_ref[0] + y_off + x * dsk_ref[...]
        y_ref[0] = y
        w = (xdt * jnp.exp(cm["end_x"] - cm["acs_x"])).astype(BF16)
        h_sc[...] = h_in * cm["chunk_decay"] + _dot(w, bmb, TN)
        zs = zs_ref[0]
        y2 = y * (zs * _sigmoid(zs))
        yn_ref[0] = (y2 * lax.rsqrt(jnp.mean(y2 * y2, axis=-1, keepdims=True) + EPS) * snw_ref[...]).astype(BF16)

    gw = GROUP_WIDTH
    small = pl.BlockSpec((1, 1, LANES), lambda gi, bi, ci: (gi, 0, 0))
    xblk = pl.BlockSpec((1, CHUNK, gw), lambda gi, bi, ci: (bi, ci, gi))
    return pl.pallas_call(
        kern, name="ssd_fwd",
        out_shape=(jax.ShapeDtypeStruct((b, s, SSD_WIDTH), F32), jax.ShapeDtypeStruct((b, s, SSD_WIDTH), BF16),
                   jax.ShapeDtypeStruct((b, nc, g4, gw, SSD_STATE), F32)),
        grid=(g4, b, nc),
        in_specs=[xblk,
                  pl.BlockSpec((1, CHUNK, LANES), lambda gi, bi, ci: (bi, ci, SSD_WIDTH // LANES + gi)),
                  pl.BlockSpec((1, CHUNK, LANES), lambda gi, bi, ci: (bi, ci, SSD_WIDTH // LANES + g4 + gi)),
                  pl.BlockSpec((1, CHUNK, gw), lambda gi, bi, ci: (bi, ci, ZS0 // gw + gi)),
                  pl.BlockSpec((1, 1, CHUNK, LANES), lambda gi, bi, ci: (bi, gi, ci, 0)),
                  small, small,
                  pl.BlockSpec((1, gw), lambda gi, bi, ci: (0, gi)),
                  pl.BlockSpec((1, gw), lambda gi, bi, ci: (0, gi))],
        out_specs=(xblk, xblk, pl.BlockSpec((1, 1, 1, gw, SSD_STATE), lambda gi, bi, ci: (bi, ci, gi, 0, 0))),
        scratch_shapes=[pltpu.VMEM((gw, SSD_STATE), F32)],
        compiler_params=_cparams("parallel", "parallel", "arbitrary"),
    )(xact, xact, xact, proj3, dtr_g, dtb_g, alog_g, dskip_x, snw)


def _ssd_bwd(dyn3, y3, xact, proj3, hst, dtr_g, dtb_g, alog_g, dskip_x, snw):
    b, s, _ = xact.shape
    nc = s // CHUNK
    g4 = SSD_GROUPS
    gw = GROUP_WIDTH

    def kern(dyn_ref, y_ref, xs_ref, bm_ref, cm_ref, zs_ref, hst_ref, dtr_ref, dtb_ref, alog_ref, dsk_ref, snw_ref,
             dxs_ref, dbm_ref, dcm_ref, dzs_ref, ddtr_ref, dsnw_ref, dalog_ref, ddtb_ref, ddsk_ref, dh_sc):
        first = jnp.logical_and(pl.program_id(1) == 0, pl.program_id(2) == 0)

        @pl.when(first)
        def _():
            dsnw_ref[...] = jnp.zeros_like(dsnw_ref)
            dalog_ref[...] = jnp.zeros_like(dalog_ref)
            ddtb_ref[...] = jnp.zeros_like(ddtb_ref)
            ddsk_ref[...] = jnp.zeros_like(ddsk_ref)

        @pl.when(pl.program_id(2) == 0)
        def _():
            dh_sc[...] = jnp.zeros_like(dh_sc)

        cm = _ssd_common(dtr_ref, dtb_ref, alog_ref)
        row, lane = cm["row"], cm["lane"]
        y = y_ref[0]
        zs = zs_ref[0]
        sg = _sigmoid(zs)
        silu = zs * sg
        y2 = y * silu
        rstd = lax.rsqrt(jnp.mean(y2 * y2, axis=-1, keepdims=True) + EPS)
        y2h = y2 * rstd
        dyn = dyn_ref[0]
        dsnw_ref[0] += jnp.sum(dyn * y2h, axis=0, keepdims=True)
        gwv = dyn * snw_ref[...]
        dy2 = rstd * (gwv - y2h * jnp.mean(gwv * y2h, axis=-1, keepdims=True))
        dzs_ref[0] = (dy2 * y * (sg * (1.0 + zs * (1.0 - sg)))).astype(BF16)
        dy = dy2 * silu
        dyb = dy.astype(BF16)

        x = xs_ref[0]
        bmb = bm_ref[0].astype(BF16)
        cmb = cm_ref[0].astype(BF16)
        h_in = hst_ref[0, 0, 0]
        h_inb = h_in.astype(BF16)
        d_hn = dh_sc[...]
        d_hnb = d_hn.astype(BF16)
        xdt = x * cm["dt_x"]
        xdtb = xdt.astype(BF16)
        eacs = jnp.exp(cm["acs_x"])
        dte = jnp.exp(cm["end_x"] - cm["acs_x"])
        wb = (xdt * dte).astype(BF16)

        dsk_lanes = jnp.broadcast_to(jnp.sum(dy * x, axis=0, keepdims=True), (8, gw))
        ddsk_ref[0] += _dot(dsk_lanes, cm["reduce"], NN, HIGHEST)[0:1, :]
        dyo = dy * eacs
        dyob = dyo.astype(BF16)
        dacs_x = dyo * _dot(cmb, h_inb, NT)
        dcm = _dot(dyob, h_inb, NN)
        dh_in = _dot(dyob, cmb, TN)
        dw = _dot(bmb, d_hnb, NT)
        dbm = _dot(wb, d_hnb, NN)
        dxdt = dw * dte
        e_l = dw * xdt * dte
        dacs_x = dacs_x - e_l
        dend_x = jnp.sum(e_l, axis=0, keepdims=True)
        dh_sc[...] = d_hn * cm["chunk_decay"] + dh_in
        q = d_hn * h_in * cm["chunk_decay"]
        dend_x = dend_x + _dot(jnp.ones((8, SSD_STATE), F32), q, NT, HIGHEST)[0:1, :]
        last_row = lax.broadcasted_iota(jnp.int32, (CHUNK, gw), 0) == CHUNK - 1
        dacs_x = dacs_x + jnp.where(last_row, dend_x, 0.0)

        cb = _dot(cmb, bmb, NT)
        dcb = jnp.zeros((CHUNK, CHUNK), F32)
        dacs = jnp.zeros((CHUNK, LANES), F32)
        dacs_t = jnp.zeros((LANES, CHUNK), F32)
        for h in range(HEADS_PER_GROUP):
            lanes = slice(h * HEAD_DIM, (h + 1) * HEAD_DIM)
            decay = _ssd_decay(cm, h)
            m = cb * decay
            dm = _dot(dyb[:, lanes], xdtb[:, lanes], NT)
            dxs_ref[0, :, lanes] = _dot(m.astype(BF16), dyb[:, lanes], TN)
            dcb_h = dm * decay
            dcb = dcb + dcb_h
            n = dcb_h * cb
            dacs = dacs + jnp.where(lane == h, jnp.sum(n, axis=1, keepdims=True), 0.0)
            dacs_t = dacs_t + jnp.where(row == h, jnp.sum(n, axis=0, keepdims=True), 0.0)
        dcbb = dcb.astype(BF16)
        dcm_ref[0] = dcm + _dot(dcbb, bmb, NN)
        dbm_ref[0] = dbm + _dot(dcbb, cmb, TN)
        dxdt = dxdt + dxs_ref[0]
        dxs_ref[0] = dy * dsk_ref[...] + dxdt * cm["dt_x"]

        dacs = dacs - dacs_t.T + _dot(dacs_x, cm["reduce"], NN, HIGHEST)
        ddt = _dot(dxdt * x, cm["reduce"], NN, HIGHEST)
        triu = (row <= lane).astype(F32)
        rc = _dot(triu, dacs, NN, HIGHEST)
        ddt = ddt + cm["a"] * rc
        dalog_ref[0] += jnp.sum(cm["dt"] * rc, axis=0, keepdims=True) * cm["a"]
        ddtr = jnp.where(cm["head_lane"], ddt * _sigmoid(cm["pre"]), 0.0)
        ddtr_ref[0, 0] = ddtr
        ddtb_ref[0] += jnp.sum(ddtr, axis=0, keepdims=True)

    def rev(ci):
        return nc - 1 - ci

    small = pl.BlockSpec((1, 1, LANES), lambda gi, bi, ci: (gi, 0, 0))
    xblk = pl.BlockSpec((1, CHUNK, gw), lambda gi, bi, ci: (bi, rev(ci), gi))
    nblk = pl.BlockSpec((1, CHUNK, LANES), lambda gi, bi, ci: (bi, rev(ci), gi))
    gvec = pl.BlockSpec((1, gw), lambda gi, bi, ci: (0, gi))
    gacc = pl.BlockSpec((1, 1, gw), lambda gi, bi, ci: (gi, 0, 0))
    return pl.pallas_call(
        kern, name="ssd_bwd",
        out_shape=(jax.ShapeDtypeStruct((b, s, SSD_WIDTH), F32),
                   jax.ShapeDtypeStruct((b, s, g4 * SSD_STATE), F32),
                   jax.ShapeDtypeStruct((b, s, g4 * SSD_STATE), F32),
                   jax.ShapeDtypeStruct((b, s, SSD_WIDTH), BF16),
                   jax.ShapeDtypeStruct((b, g4, s, LANES), F32),
                   jax.ShapeDtypeStruct((g4, 1, gw), F32),
                   jax.ShapeDtypeStruct((g4, 1, LANES), F32),
                   jax.ShapeDtypeStruct((g4, 1, LANES), F32),
                   jax.ShapeDtypeStruct((g4, 1, LANES), F32)),
        grid=(g4, b, nc),
        in_specs=[xblk, xblk, xblk,
                  pl.BlockSpec((1, CHUNK, LANES), lambda gi, bi, ci: (bi, rev(ci), SSD_WIDTH // LANES + gi)),
                  pl.BlockSpec((1, CHUNK, LANES), lambda gi, bi, ci: (bi, rev(ci), SSD_WIDTH // LANES + g4 + gi)),
                  pl.BlockSpec((1, CHUNK, gw), lambda gi, bi, ci: (bi, rev(ci), ZS0 // gw + gi)),
                  pl.BlockSpec((1, 1, 1, gw, SSD_STATE), lambda gi, bi, ci: (bi, rev(ci), gi, 0, 0)),
                  pl.BlockSpec((1, 1, CHUNK, LANES), lambda gi, bi, ci: (bi, gi, rev(ci), 0)),
                  small, small, gvec, gvec],
        out_specs=(xblk, nblk, nblk, xblk,
                   pl.BlockSpec((1, 1, CHUNK, LANES), lambda gi, bi, ci: (bi, gi, rev(ci), 0)),
                   gacc, small, small, small),
        scratch_shapes=[pltpu.VMEM((gw, SSD_STATE), F32)],
        compiler_params=_cparams("parallel", "arbitrary", "arbitrary"),
    )(dyn3, y3, xact, xact, xact, proj3, hst, dtr_g, dtb_g, alog_g, dskip_x, snw)


def _adamw(w, g, m, v, name):
    r, c = w.shape
    tr = 128 if r % 128 == 0 else r

    def kern(w_ref, g_ref, m_ref, v_ref, d_ref, nm_ref, nv_ref):
        gv = g_ref[...]
        nm = ADAM_B1 * m_ref[...] + (1.0 - ADAM_B1) * gv
        nv = ADAM_B2 * v_ref[...] + (1.0 - ADAM_B2) * (gv * gv)
        m_hat = nm / (1.0 - ADAM_B1 ** ADAM_STEP)
        v_hat = nv / (1.0 - ADAM_B2 ** ADAM_STEP)
        d_ref[...] = -ADAM_LR * (m_hat / (jnp.sqrt(v_hat) + ADAM_EPS) + ADAM_WD * w_ref[...])
        nm_ref[...] = nm
        nv_ref[...] = nv

    blk = pl.BlockSpec((tr, c), lambda i: (i, 0))
    out = jax.ShapeDtypeStruct((r, c), F32)
    return pl.pallas_call(
        kern, name=name, out_shape=(out, out, out), grid=(r // tr,),
        in_specs=[blk] * 4, out_specs=(blk, blk, blk),
        compiler_params=_cparams("parallel"),
    )(w, g, m, v)


ANY = pl.BlockSpec(memory_space=pl.ANY)


def _position():
    return lax.axis_index("x"), lax.axis_index("y"), lax.axis_index("c")


def _other_chips(x, y):
    return [(1 - x, y), (x, 1 - y), (1 - x, 1 - y)]


def _dma_sems(n):
    return [pltpu.SemaphoreType.DMA((n,)), pltpu.SemaphoreType.DMA((n,))]


def _gather_weights(shards):
    n = len(shards)

    def body(*refs):
        p_refs, out_refs = refs[:n], refs[n:2 * n]
        send_sems, recv_sems = refs[2 * n:]
        x, y, c = _position()
        me = 2 * x + y
        chips = _other_chips(x, y)

        def slab(a, chip, hf):
            half = shards[a].shape[0] // 2
            return out_refs[a].at[chip, pl.ds(hf * half, half), :]

        def my_half(a):
            half = shards[a].shape[0] // 2
            return p_refs[a].at[pl.ds(c * half, half), :]

        def over_ici(a, j, chip_from):
            px, py = chips[j]
            return pltpu.make_async_remote_copy(
                src_ref=my_half(a), dst_ref=slab(a, chip_from, c),
                send_sem=send_sems.at[3 * a + j], recv_sem=recv_sems.at[3 * a + j],
                device_id=(px, py, c), device_id_type=MESH)

        def to_sibling(a, j, hf):
            px, py = chips[j]
            return pltpu.make_async_remote_copy(
                src_ref=slab(a, 2 * px + py, hf), dst_ref=slab(a, 2 * px + py, hf),
                send_sem=send_sems.at[3 * (n + a) + j], recv_sem=recv_sems.at[3 * (n + a) + j],
                device_id=(x, y, 1 - c), device_id_type=MESH)

        first = [over_ici(a, j, me) for a in range(n) for j in range(3)]
        for cp in first:
            cp.start()
        passed = []
        for a in range(n):
            for j, (px, py) in enumerate(chips):
                over_ici(a, j, 2 * px + py).wait_recv()
                passed.append(to_sibling(a, j, c))
                passed[-1].start()
        for a in range(n):
            for j in range(3):
                to_sibling(a, j, 1 - c).wait_recv()
        for cp in first + passed:
            cp.wait_send()

    return pl.pallas_call(
        body, name="gather_weights",
        out_shape=[jax.ShapeDtypeStruct((N_CHIPS, *v.shape), v.dtype) for v in shards],
        in_specs=[ANY] * n, out_specs=[ANY] * n,
        scratch_shapes=_dma_sems(6 * n),
    )(*shards)


def _swap_halves(parts):
    n = len(parts)

    def body(*refs):
        v_refs, out_refs = refs[:n], refs[n:2 * n]
        send_sems, recv_sems = refs[2 * n:]
        x, y, c = _position()
        copies = []
        for a in range(n):
            half = parts[a].shape[1] // 2
            copies.append(pltpu.make_async_remote_copy(
                src_ref=v_refs[a].at[:, pl.ds((1 - c) * half, half), :], dst_ref=out_refs[a],
                send_sem=send_sems.at[a], recv_sem=recv_sems.at[a], device_id=(x, y, 1 - c), device_id_type=MESH))
        for cp in copies:
            cp.start()
        for cp in copies:
            cp.wait()

    return pl.pallas_call(
        body, name="grad_swap_halves",
        out_shape=[jax.ShapeDtypeStruct((v.shape[0], v.shape[1] // 2, v.shape[2]), v.dtype) for v in parts],
        in_specs=[ANY] * n, out_specs=[ANY] * n,
        scratch_shapes=_dma_sems(n),
    )(*parts)


def _chip_all_to_all(parts):
    n = len(parts)

    def body(*refs):
        p_refs, out_refs = refs[:n], refs[n:2 * n]
        send_sems, recv_sems = refs[2 * n:]
        x, y, c = _position()
        chips = _other_chips(x, y)
        sends = [pltpu.make_async_remote_copy(
            src_ref=p_refs[a].at[2 * px + py], dst_ref=out_refs[a].at[j],
            send_sem=send_sems.at[3 * a + j], recv_sem=recv_sems.at[3 * a + j],
            device_id=(px, py, c), device_id_type=MESH) for a in range(n) for j, (px, py) in enumerate(chips)]
        for cp in sends:
            cp.start()
        for cp in sends:
            cp.wait()

    return pl.pallas_call(
        body, name="grad_all_to_all",
        out_shape=[jax.ShapeDtypeStruct((N_CHIPS - 1, *v.shape[1:]), v.dtype) for v in parts],
        in_specs=[ANY] * n, out_specs=[ANY] * n,
        scratch_shapes=_dma_sems(3 * n),
    )(*parts)


def _join_halves(wholes):
    n = len(wholes)

    def body(*refs):
        out_refs = refs[n:2 * n]
        send_sems, recv_sems = refs[2 * n:]
        x, y, c = _position()
        copies = []
        for a in range(n):
            half = wholes[a].shape[0] // 2
            rows = out_refs[a].at[pl.ds(c * half, half), :]
            copies.append(pltpu.make_async_remote_copy(
                src_ref=rows, dst_ref=rows, send_sem=send_sems.at[a], recv_sem=recv_sems.at[a],
                device_id=(x, y, 1 - c), device_id_type=MESH))
        for cp in copies:
            cp.start()
        for cp in copies:
            cp.wait()

    return pl.pallas_call(
        body, name="grad_join_halves",
        out_shape=[jax.ShapeDtypeStruct(v.shape, v.dtype) for v in wholes],
        in_specs=[ANY] * n, out_specs=[ANY] * n,
        input_output_aliases={a: a for a in range(n)},
        scratch_shapes=_dma_sems(n),
    )(*wholes)


ADD_ROWS = 128


def _add_halves(g, sw, place, name):
    n, rows, cols = g.shape
    half = rows // 2
    nb = half // ADD_ROWS

    def kern(p_ref, g_ref, s_ref, o_ref):
        o_ref[...] = (g_ref[...] + s_ref[...]).astype(BF16)

    blk = pl.BlockSpec((1, ADD_ROWS, cols), lambda j, i, p_ref: (j, i, 0))
    return pl.pallas_call(
        kern, name=name,
        out_shape=jax.ShapeDtypeStruct((n, half, cols), BF16),
        grid_spec=pltpu.PrefetchScalarGridSpec(
            num_scalar_prefetch=1, grid=(n, nb),
            in_specs=[pl.BlockSpec((1, ADD_ROWS, cols), lambda j, i, p_ref: (j, p_ref[0] * nb + i, 0)), blk],
            out_specs=blk),
        compiler_params=_cparams("parallel", "parallel"),
    )(place, g, sw)


def _sum_chips(own, rx, place, name):
    _, half, cols = rx.shape
    nb = half // ADD_ROWS

    def kern(p_ref, own_ref, r_ref, o_ref):
        total = own_ref[0].astype(F32)
        for j in range(N_CHIPS - 1):
            total = total + r_ref[j].astype(F32)
        o_ref[...] = total

    return pl.pallas_call(
        kern, name=name,
        out_shape=jax.ShapeDtypeStruct((2 * half, cols), F32),
        grid_spec=pltpu.PrefetchScalarGridSpec(
            num_scalar_prefetch=1, grid=(nb,),
            in_specs=[pl.BlockSpec((1, ADD_ROWS, cols), lambda i, p_ref: (p_ref[1], i, 0)),
                      pl.BlockSpec((N_CHIPS - 1, ADD_ROWS, cols), lambda i, p_ref: (0, i, 0))],
            out_specs=pl.BlockSpec((ADD_ROWS, cols), lambda i, p_ref: (p_ref[0] * nb + i, 0))),
        compiler_params=_cparams("parallel"),
    )(place, own, rx)


def _gather_small(v, reduce, name):
    rows = v.shape[0]

    def body(v_ref, out_ref, buf, send_sems, recv_sems):
        x, y, c = _position()
        me = 4 * x + 2 * y + c
        buf[me] = v_ref[...]
        peers = [(x ^ (k >> 2), y ^ ((k >> 1) & 1), c ^ (k & 1)) for k in range(1, 8)]
        copies = [pltpu.make_async_remote_copy(
            src_ref=v_ref, dst_ref=buf.at[me],
            send_sem=send_sems.at[k], recv_sem=recv_sems.at[k],
            device_id=peer, device_id_type=MESH) for k, peer in enumerate(peers)]
        for cp in copies:
            cp.start()
        for k, (px, py, pc) in enumerate(peers):
            pltpu.make_async_remote_copy(
                src_ref=v_ref, dst_ref=buf.at[4 * px + 2 * py + pc],
                send_sem=send_sems.at[k], recv_sem=recv_sems.at[k],
                device_id=(px, py, pc), device_id_type=MESH).wait_recv()
        for cp in copies:
            cp.wait_send()
        if reduce:
            total = buf[0]
            for d in range(1, 8):
                total = total + buf[d]
            out_ref[...] = total
        else:
            out_ref[...] = buf[...]

    vm = pl.BlockSpec(memory_space=pltpu.VMEM)
    return pl.pallas_call(
        body, name=name,
        out_shape=jax.ShapeDtypeStruct((rows, LANES) if reduce else (8, rows, LANES), F32),
        in_specs=[vm], out_specs=vm,
        scratch_shapes=[pltpu.VMEM((8, rows, LANES), F32), pltpu.SemaphoreType.DMA((7,)), pltpu.SemaphoreType.DMA((7,))],
    )(v)


def _pad_rows(a, rows):
    return jnp.pad(a, ((0, rows - a.shape[0]), (0, 0)))


def _lane_pad(v):
    n = v.shape[1]
    return jnp.pad(v, ((0, 0), (0, -n % LANES)))


def _gather_all(w_in, w_attn_out, w_ssm_out, w_o, conv_w):
    d = D_MODEL
    own = [a[0].astype(BF16) for a in (w_in, w_attn_out, w_ssm_out, w_o)]
    gathered = _gather_weights(own)
    chip = 2 * lax.axis_index("x") + lax.axis_index("y")
    w_in_all, w_ao, w_so, w_oo = [[jnp.where(chip == q, o, g[q]) for q in range(N_CHIPS)] for o, g in zip(own, gathered)]
    last = w_in_all[N_CHIPS - 1]
    w_proj = jnp.concatenate([w_in_all[0], w_in_all[1], w_in_all[2], last[:, :LAST_DT0], last[:, LAST_DT0 + 32:],
                              last[:, LAST_DT0:LAST_DT0 + 32], jnp.zeros((d, DT_PAD - 32), BF16)], axis=1)
    w_ao = jnp.concatenate(w_ao, axis=0)
    w_so = jnp.concatenate(w_so, axis=0)
    w_oo = jnp.concatenate(w_oo, axis=0)
    conv_rows = conv_w[0].size // LANES
    conv_all = _gather_small(conv_w[0].reshape(conv_rows, LANES), False, "gather_conv_w")
    conv_w_all = conv_all[0::2].reshape(N_CHIPS, CONV_K, CONV_DIM // N_CHIPS).transpose(1, 0, 2).reshape(CONV_K, CONV_DIM)

    return w_proj, w_ao, w_so, w_oo, conv_w_all


def _local_step(x, loss_target, norm_w, w_proj, conv_w_all, conv_b, dt_bias, a_log, d_skip, ssm_norm_w,
                w_ao, w_so, w_oo, final_norm_w):
    b, s, d = x.shape
    t = b * s
    g4, hg = SSD_GROUPS, HEADS_PER_GROUP
    dtb_g = _lane_pad(dt_bias.reshape(g4, hg)).reshape(g4, 1, LANES)
    alog_g = _lane_pad(a_log.reshape(g4, hg)).reshape(g4, 1, LANES)
    dskip_x = jnp.repeat(d_skip, HEAD_DIM, axis=1)
    fnw = final_norm_w.reshape(1, d)

    x2 = x.reshape(t, d)
    h = _rms_fwd(x2, norm_w)
    proj = _matmul(h, w_proj, tm=512, tn=1280, tk=1024, name="proj")
    proj3 = proj.reshape(b, s, NP)
    o3, yp3 = _attn_fwd(proj3)
    xact = _conv_fwd(proj3, conv_w_all, conv_b)
    dtr = proj3[:, :, DT0:DT0 + g4 * hg].reshape(b, s, g4, hg).transpose(0, 2, 1, 3)
    dtr_g = jnp.pad(dtr, ((0, 0), (0, 0), (0, 0), (0, LANES - hg)))
    y3, yn3, hst = _ssd_fwd(xact, proj3, dtr_g, dtb_g, alog_g, dskip_x, ssm_norm_w)
    yp = yp3.reshape(t, D_MODEL)
    yn = yn3.reshape(t, SSD_WIDTH)
    ya = _matmul(yp, w_ao, tm=512, tn=1024, tk=1024, name="attn_out")
    ys = _matmul(yn, w_so, tm=512, tn=1024, tk=2048, name="ssm_out")
    merged = _merge_fwd(proj, ya, ys)
    mo = _matmul(merged, w_oo, tm=512, tn=1024, tk=1024, name="out_proj")
    dout, doutb, loss_part, d_fnw = _final_fwd_bwd(x2, mo, loss_target.reshape(t, d), fnw)

    dmerged = _matmul(doutb, w_oo, tb=True, tm=512, tn=1024, tk=1024, name="d_merged")
    g_wo = _matmul(merged, doutb, ta=True, tm=512, tn=1024, tk=1024, name="g_w_o")
    dya, dys, dgate = _merge_bwd(dmerged, proj, ya, ys)
    dyp = _matmul(dya, w_ao, tb=True, tm=512, tn=1024, tk=1024, name="d_attn_pre")
    g_wao = _matmul(yp, dya, ta=True, tm=512, tn=1024, tk=1024, name="g_w_attn_out")
    dyn = _matmul(dys, w_so, tb=True, tm=512, tn=2048, tk=1024, name="d_ssm_norm")
    g_wso = _matmul(yn, dys, ta=True, tm=512, tn=1024, tk=1024, name="g_w_ssm_out")
    dq, dk, dv, dza = _attn_bwd(proj3, dyp.reshape(b, s, D_MODEL), o3)
    (dxs, dbm, dcm, dzs, ddtr_g, d_snw_g, d_alog_g, d_dtb_g, d_dsk_g) = _ssd_bwd(
        dyn.reshape(b, s, SSD_WIDTH), y3, xact, proj3, hst, dtr_g, dtb_g, alog_g, dskip_x, ssm_norm_w)
    dx_xs, g_cw_xs, g_cb_xs = _conv_bwd(dxs, proj3, conv_w_all, conv_b, 0, "conv_bwd_x")
    dx_bm, g_cw_bm, g_cb_bm = _conv_bwd(dbm, proj3, conv_w_all, conv_b, SSD_WIDTH, "conv_bwd_b")
    dx_cm, g_cw_cm, g_cb_cm = _conv_bwd(dcm, proj3, conv_w_all, conv_b, SSD_WIDTH + g4 * SSD_STATE, "conv_bwd_c")
    ddt = ddtr_g[:, :, :, :hg].transpose(0, 2, 1, 3).reshape(b, s, g4 * hg).astype(BF16)
    dproj = jnp.concatenate([dq, dk, dv, dza, dzs, dx_xs, dx_bm, dx_cm, dgate.reshape(b, s, 2 * D_MODEL),
                             jnp.pad(ddt, ((0, 0), (0, 0), (0, DT_PAD - g4 * hg)))], axis=2).reshape(t, NP)
    g_wproj = _matmul(h, dproj, ta=True, tm=512, tn=1280, tk=1024, name="g_w_in")
    dh = _matmul(dproj, w_proj, tb=True, tm=512, tn=1024, tk=1280, name="d_h")
    grad_x, d_nw = _rms_bwd(dh, x2, norm_w, dout)
    g_cw = jnp.concatenate([g_cw_xs, g_cw_bm, g_cw_cm], axis=1)
    g_cb = jnp.concatenate([g_cb_xs, g_cb_bm, g_cb_cm], axis=1)
    return (loss_part, grad_x, d_nw, g_wproj, g_cw, g_cb, d_dtb_g, d_alog_g, d_dsk_g, d_snw_g, g_wao, g_wso, g_wo, d_fnw)


def kernel(x, norm_w, w_in, conv_w, conv_b, dt_bias, a_log, d_skip, ssm_norm_w, w_attn_out, w_ssm_out, w_o, final_norm_w, loss_target, m_norm_w, m_w_in, m_conv_w, m_conv_b, m_dt_bias, m_a_log, m_d_skip, m_ssm_norm_w, m_w_attn_out, m_w_ssm_out, m_w_o, m_final_norm_w, v_norm_w, v_w_in, v_conv_w, v_conv_b, v_dt_bias, v_a_log, v_d_skip, v_ssm_norm_w, v_w_attn_out, v_w_ssm_out, v_w_o, v_final_norm_w):
    b, s, d = x.shape
    core = lax.axis_index("c")
    g4, hg = SSD_GROUPS, HEADS_PER_GROUP
    shard_cols = w_in.shape[2]
    w_proj, w_ao, w_so, w_oo, conv_w_all = _gather_all(w_in, w_attn_out, w_ssm_out, w_o, conv_w)
    (loss_part, grad_x, d_nw, g_wproj, g_cw, g_cb, d_dtb_g, d_alog_g, d_dsk_g, d_snw_g, g_wao, g_wso, g_wo, d_fnw) = _local_step(
        x, loss_target, norm_w, w_proj, conv_w_all, conv_b, dt_bias, a_log, d_skip, ssm_norm_w, w_ao, w_so, w_oo, final_norm_w)

    last0 = (N_CHIPS - 1) * shard_cols
    g_last = jnp.concatenate([g_wproj[:, last0:GATE0], g_wproj[:, DT0:DT0 + 32], g_wproj[:, GATE0:DT0]], axis=1)
    g_win_chips = jnp.stack([g_wproj[:, j * shard_cols:(j + 1) * shard_cols] for j in range(N_CHIPS - 1)] + [g_last])
    g_out_chips = jnp.concatenate([g.reshape(N_CHIPS, -1, d) for g in (g_wao, g_wso, g_wo)], axis=1)
    parts = [g_win_chips, g_out_chips]
    chip = 2 * lax.axis_index("x") + lax.axis_index("y")
    place = jnp.stack([core, chip]).astype(jnp.int32)
    from_sibling = _swap_halves(parts)
    chip_sums = [_add_halves(p, f, place, "grad_add_halves_%d" % i) for i, (p, f) in enumerate(zip(parts, from_sibling))]
    from_chips = _chip_all_to_all(chip_sums)
    wholes = [_sum_chips(o, r, place, "grad_sum_chips_%d" % i) for i, (o, r) in enumerate(zip(chip_sums, from_chips))]
    g_w_in, g_out = _join_halves(wholes)

    small = jnp.concatenate([
        loss_part, d_nw, g_cb, _lane_pad(d_dtb_g[:, 0, :hg].reshape(1, -1)), _lane_pad(d_alog_g[:, 0, :hg].reshape(1, -1)),
        _lane_pad(d_dsk_g[:, 0, :hg].reshape(1, -1)),
        d_snw_g.reshape(1, -1), d_fnw, g_cw.reshape(1, -1)], axis=1)
    small_rows = small.shape[1] // LANES
    reduced = _gather_small(_pad_rows(small.reshape(small_rows, LANES), -(-small_rows // 8) * 8), True, "reduce_small")
    flat = reduced.reshape(-1)

    def take(start, n):
        return flat[start:start + n].reshape(1, n)

    loss = flat[0]
    pos = LANES
    g_norm_w = take(pos, d); pos += d
    g_conv_b = take(pos, CONV_DIM); pos += CONV_DIM
    g_dt_bias = take(pos, g4 * hg); pos += LANES
    g_a_log = take(pos, g4 * hg); pos += LANES
    g_d_skip = take(pos, g4 * hg); pos += LANES
    g_ssm_norm_w = take(pos, SSD_WIDTH); pos += SSD_WIDTH
    g_final_norm_w = take(pos, d); pos += d
    conv_cols = CONV_DIM // N_CHIPS
    g_conv_w = lax.dynamic_slice_in_dim(flat[pos:pos + CONV_K * CONV_DIM].reshape(CONV_K, CONV_DIM), chip * conv_cols, conv_cols, axis=1)

    rows_ao, rows_so = D_MODEL // N_CHIPS, SSD_WIDTH // N_CHIPS
    g_w_attn_out = g_out[:rows_ao]
    g_w_ssm_out = g_out[rows_ao:rows_ao + rows_so]
    g_w_o = g_out[rows_ao + rows_so:]

    names = ["norm_w", "w_in", "conv_w", "conv_b", "dt_bias", "a_log", "d_skip", "ssm_norm_w",
             "w_attn_out", "w_ssm_out", "w_o", "final_norm_w"]
    weights = [norm_w, w_in, conv_w, conv_b, dt_bias, a_log, d_skip, ssm_norm_w, w_attn_out, w_ssm_out, w_o, final_norm_w]
    grads = [g_norm_w, g_w_in, g_conv_w, g_conv_b, g_dt_bias, g_a_log, g_d_skip, g_ssm_norm_w,
             g_w_attn_out, g_w_ssm_out, g_w_o, g_final_norm_w]
    ms = [m_norm_w, m_w_in, m_conv_w, m_conv_b, m_dt_bias, m_a_log, m_d_skip, m_ssm_norm_w,
          m_w_attn_out, m_w_ssm_out, m_w_o, m_final_norm_w]
    vs = [v_norm_w, v_w_in, v_conv_w, v_conv_b, v_dt_bias, v_a_log, v_d_skip, v_ssm_norm_w,
          v_w_attn_out, v_w_ssm_out, v_w_o, v_final_norm_w]
    out_g, out_d, out_m, out_v = [], [], [], []
    for name, w, g, m, v in zip(names, weights, grads, ms, vs):
        shape2 = g.shape
        dlt, nm, nv = _adamw(w.reshape(shape2), g, m.reshape(shape2), v.reshape(shape2), "adamw_" + name)
        out_g.append(g.reshape(w.shape))
        out_d.append(dlt.reshape(w.shape))
        out_m.append(nm.reshape(w.shape))
        out_v.append(nv.reshape(w.shape))

    return (loss, grad_x.reshape(b, s, d), *out_g, *out_d, *out_m, *out_v)
```

```python
import jax
import jax.numpy as jnp
from jax import lax
from jax.experimental import pallas as pl
from jax.experimental.pallas import tpu as pltpu

F32 = jnp.float32
BF16 = jnp.bfloat16
MESH = pl.DeviceIdType.MESH

D_MODEL = 1024
SB_HEADS = 16
HEAD_DIM = 64
SSD_WIDTH = 2048
SSD_GROUPS = 4
GROUP_WIDTH = SSD_WIDTH // SSD_GROUPS
HEADS_PER_GROUP = 8
SSD_STATE = 128
CHUNK = 128
CONV_K = 4
CONV_DIM = 3072
D_PROJ = 11296
EPS = 1e-6
ADAM_LR, ADAM_B1, ADAM_B2, ADAM_EPS, ADAM_WD, ADAM_STEP = 0.001, 0.9, 0.999, 1e-08, 0.01, 10

LANES = 128
Q0, K0, V0, ZA0, ZS0, XBC0, GATE0, DT0 = 0, 1024, 2048, 3072, 4096, 6144, 9216, 11264
DT_PAD = 256
NP = DT0 + DT_PAD
N_CHIPS = 4
LAST_DT0 = GATE0 - (N_CHIPS - 1) * (D_PROJ // N_CHIPS)
VMEM_LIMIT = 56 * 1024 * 1024


def _cparams(*sem):
    return pltpu.CompilerParams(dimension_semantics=sem or None, vmem_limit_bytes=VMEM_LIMIT)


def _sigmoid(z):
    return 1.0 / (1.0 + jnp.exp(-z))


def _dot(a, b, dims, precision=None):
    return lax.dot_general(a, b, (dims, ((), ())), preferred_element_type=F32, precision=precision)


NN = ((1,), (0,))
NT = ((1,), (1,))
TN = ((0,), (0,))


def _matmul(a, b, *, ta=False, tb=False, out_dtype=F32, tm, tn, tk, name):
    m, k = (a.shape[1], a.shape[0]) if ta else a.shape
    n = b.shape[0] if tb else b.shape[1]
    assert m % tm == 0 and n % tn == 0 and k % tk == 0, (name, m, n, k)
    nk = k // tk
    use_scratch = out_dtype != F32
    dims = ((0,) if ta else (1,), (1,) if tb else (0,))

    def kern(a_ref, b_ref, o_ref, *scratch):
        acc = scratch[0] if use_scratch else o_ref
        kk = pl.program_id(2)

        @pl.when(kk == 0)
        def _():
            acc[...] = jnp.zeros_like(acc)

        acc[...] += _dot(a_ref[...], b_ref[...], dims)
        if use_scratch:
            @pl.when(kk == nk - 1)
            def _():
                o_ref[...] = acc[...].astype(out_dtype)

    a_spec = pl.BlockSpec((tk, tm), lambda i, j, q: (q, i)) if ta else pl.BlockSpec((tm, tk), lambda i, j, q: (i, q))
    b_spec = pl.BlockSpec((tn, tk), lambda i, j, q: (j, q)) if tb else pl.BlockSpec((tk, tn), lambda i, j, q: (q, j))
    return pl.pallas_call(
        kern, name=name,
        out_shape=jax.ShapeDtypeStruct((m, n), out_dtype),
        grid=(m // tm, n // tn, nk),
        in_specs=[a_spec, b_spec],
        out_specs=pl.BlockSpec((tm, tn), lambda i, j, q: (i, j)),
        scratch_shapes=[pltpu.VMEM((tm, tn), F32)] if use_scratch else [],
        compiler_params=_cparams("parallel", "parallel", "arbitrary"),
    )(a, b)


ROWS = 256


def _rms_fwd(x2, w):
    t, d = x2.shape

    def kern(x_ref, w_ref, h_ref):
        x = x_ref[...]
        r = lax.rsqrt(jnp.mean(x * x, axis=-1, keepdims=True) + EPS)
        h_ref[...] = (x * r * w_ref[...]).astype(BF16)

    return pl.pallas_call(
        kern, name="rms_fwd",
        out_shape=jax.ShapeDtypeStruct((t, d), BF16),
        grid=(t // ROWS,),
        in_specs=[pl.BlockSpec((ROWS, d), lambda i: (i, 0)), pl.BlockSpec((1, d), lambda i: (0, 0))],
        out_specs=pl.BlockSpec((ROWS, d), lambda i: (i, 0)),
        compiler_params=_cparams("parallel"),
    )(x2, w)


def _rms_bwd(dh, x2, w, dout):
    t, d = x2.shape

    def kern(dh_ref, x_ref, w_ref, dout_ref, gx_ref, dw_ref):
        @pl.when(pl.program_id(0) == 0)
        def _():
            dw_ref[...] = jnp.zeros_like(dw_ref)

        x = x_ref[...]
        r = lax.rsqrt(jnp.mean(x * x, axis=-1, keepdims=True) + EPS)
        xh = x * r
        g = dh_ref[...]
        dw_ref[...] += jnp.sum(g * xh, axis=0, keepdims=True)
        gw = g * w_ref[...]
        gx_ref[...] = dout_ref[...] + r * (gw - xh * jnp.mean(gw * xh, axis=-1, keepdims=True))

    row = pl.BlockSpec((ROWS, d), lambda i: (i, 0))
    vec = pl.BlockSpec((1, d), lambda i: (0, 0))
    return pl.pallas_call(
        kern, name="rms_bwd",
        out_shape=(jax.ShapeDtypeStruct((t, d), F32), jax.ShapeDtypeStruct((1, d), F32)),
        grid=(t // ROWS,),
        in_specs=[row, row, vec, row],
        out_specs=(row, vec),
        compiler_params=_cparams("arbitrary"),
    )(dh, x2, w, dout)


def _final_fwd_bwd(x2, mo, target, w):
    t, d = x2.shape

    def kern(x_ref, mo_ref, t_ref, w_ref, dout_ref, doutb_ref, loss_ref, dw_ref):
        @pl.when(pl.program_id(0) == 0)
        def _():
            loss_ref[...] = jnp.zeros_like(loss_ref)
            dw_ref[...] = jnp.zeros_like(dw_ref)

        u = x_ref[...] + mo_ref[...]
        r = lax.rsqrt(jnp.mean(u * u, axis=-1, keepdims=True) + EPS)
        uh = u * r
        wv = w_ref[...]
        err = uh * wv - t_ref[...]
        loss_ref[...] += (0.5 / d) * jnp.sum(err * err)
        dy = err * (1.0 / d)
        dw_ref[...] += jnp.sum(dy * uh, axis=0, keepdims=True)
        gw = dy * wv
        du = r * (gw - uh * jnp.mean(gw * uh, axis=-1, keepdims=True))
        dout_ref[...] = du
        doutb_ref[...] = du.astype(BF16)

    row = pl.BlockSpec((ROWS, d), lambda i: (i, 0))
    vec = pl.BlockSpec((1, d), lambda i: (0, 0))
    return pl.pallas_call(
        kern, name="final_fwd_bwd",
        out_shape=(jax.ShapeDtypeStruct((t, d), F32), jax.ShapeDtypeStruct((t, d), BF16),
                   jax.ShapeDtypeStruct((1, LANES), F32), jax.ShapeDtypeStruct((1, d), F32)),
        grid=(t // ROWS,),
        in_specs=[row, row, row, vec],
        out_specs=(row, row, pl.BlockSpec((1, LANES), lambda i: (0, 0)), vec),
        compiler_params=_cparams("arbitrary"),
    )(x2, mo, target, w)


def _merge_fwd(proj2, ya, ys):
    t = ya.shape[0]
    gblk = GATE0 // D_MODEL

    def kern(ga_ref, gs_ref, ya_ref, ys_ref, o_ref):
        o_ref[...] = (_sigmoid(ga_ref[...]) * ya_ref[...] + _sigmoid(gs_ref[...]) * ys_ref[...]).astype(BF16)

    row = pl.BlockSpec((ROWS, D_MODEL), lambda i: (i, 0))
    return pl.pallas_call(
        kern, name="merge_fwd",
        out_shape=jax.ShapeDtypeStruct((t, D_MODEL), BF16),
        grid=(t // ROWS,),
        in_specs=[pl.BlockSpec((ROWS, D_MODEL), lambda i: (i, gblk)),
                  pl.BlockSpec((ROWS, D_MODEL), lambda i: (i, gblk + 1)), row, row],
        out_specs=row,
        compiler_params=_cparams("parallel"),
    )(proj2, proj2, ya, ys)


def _merge_bwd(dm, proj2, ya, ys):
    t = ya.shape[0]
    gblk = GATE0 // D_MODEL

    def kern(dm_ref, ga_ref, gs_ref, ya_ref, ys_ref, dya_ref, dys_ref, dg_ref):
        g = dm_ref[...]
        sa = _sigmoid(ga_ref[...])
        ss = _sigmoid(gs_ref[...])
        dya_ref[...] = (g * sa).astype(BF16)
        dys_ref[...] = (g * ss).astype(BF16)
        dg_ref[:, :D_MODEL] = (g * ya_ref[...] * sa * (1.0 - sa)).astype(BF16)
        dg_ref[:, D_MODEL:] = (g * ys_ref[...] * ss * (1.0 - ss)).astype(BF16)

    row = pl.BlockSpec((ROWS, D_MODEL), lambda i: (i, 0))
    return pl.pallas_call(
        kern, name="merge_bwd",
        out_shape=(jax.ShapeDtypeStruct((t, D_MODEL), BF16), jax.ShapeDtypeStruct((t, D_MODEL), BF16),
                   jax.ShapeDtypeStruct((t, 2 * D_MODEL), BF16)),
        grid=(t // ROWS,),
        in_specs=[row, pl.BlockSpec((ROWS, D_MODEL), lambda i: (i, gblk)),
                  pl.BlockSpec((ROWS, D_MODEL), lambda i: (i, gblk + 1)), row, row],
        out_specs=(row, row, pl.BlockSpec((ROWS, 2 * D_MODEL), lambda i: (i, 0))),
        compiler_params=_cparams("parallel"),
    )(dm, proj2, proj2, ya, ys)


TQ = 256
TK = 256
HEAD_LANES = (slice(0, HEAD_DIM), slice(HEAD_DIM, 2 * HEAD_DIM))


def _tri(pred):
    r = lax.broadcasted_iota(jnp.int32, (TK, TK), 0)
    c = lax.broadcasted_iota(jnp.int32, (TK, TK), 1)
    return pred(r, c).astype(BF16)


def _split_bf16(v):
    hi = v.astype(BF16)
    lo = (v - hi.astype(F32)).astype(BF16)
    return hi, lo


def _tri_dot(v, tri):
    hi, lo = _split_bf16(v)
    return _dot(hi, tri, NN) + _dot(lo, tri, NN)


def _sb_logs(z, mask):
    l1p = jnp.log(1.0 + jnp.exp(-jnp.abs(z)))
    lb = jnp.minimum(z, 0.0) - l1p
    lom = -jnp.maximum(z, 0.0) - l1p
    if mask is not None:
        lom = jnp.where(mask, lom, 0.0)
    return lb, lom


def _sb_weights(lb, later, carry_r, mask):
    a = jnp.exp(lb + (later + carry_r))
    if mask is not None:
        a = jnp.where(mask, a, 0.0)
    return a


def _split_heads(dst, src, scale=None):
    for h, lanes in enumerate(HEAD_LANES):
        v = src[:, lanes]
        dst[h] = (v if scale is None else v * scale).astype(BF16)


def _attn_fwd(proj3):
    b, s, _ = proj3.shape
    nq = s // TQ
    scale = HEAD_DIM ** -0.5

    def kern(q_ref, k_ref, v_ref, za_ref, o_ref, yp_ref, qs, ks, vs):
        _split_heads(qs, q_ref[0], scale)
        _split_heads(ks, k_ref[0])
        _split_heads(vs, v_ref[0])
        row = lax.broadcasted_iota(jnp.int32, (TQ, TK), 0)
        col = lax.broadcasted_iota(jnp.int32, (TQ, TK), 1)
        tri_gt = _tri(lambda j, sk: j > sk)

        def q_block(i, _):
            r0 = pl.multiple_of(i * TQ, TQ)
            n_kb = (r0 + TQ + TK - 1) // TK
            qh = [qs[h, pl.ds(r0, TQ), :] for h in range(2)]

            def k_block(c0, carry, mask):
                kh = [ks[h, pl.ds(c0, TK), :] for h in range(2)]
                vh = [vs[h, pl.ds(c0, TK), :] for h in range(2)]
                z = [_dot(qh[h], kh[h], NT) for h in range(2)]
                logs, later = [], []
                for h in range(2):
                    logs.append(_sb_logs(z[h], mask))
                    later.append(_tri_dot(logs[h][1], tri_gt))
                out = []
                for h in range(2):
                    carry_r, acc = carry[h]
                    lb, lom = logs[h]
                    a = _sb_weights(lb, later[h], carry_r, mask)
                    row_sum = later[h][:, 0:1] + lom[:, 0:1]
                    out.append((carry_r + row_sum, acc + _dot(a.astype(BF16), vh[h], NN)))
                return tuple(out)

            c_last = pl.multiple_of((n_kb - 1) * TK, TK)
            start = (jnp.zeros((TQ, 1), F32), jnp.zeros((TQ, HEAD_DIM), F32))
            carry = k_block(c_last, (start, start), col + c_last < row + r0)

            def unmasked(jj, carry):
                return k_block(pl.multiple_of((n_kb - 2 - jj) * TK, TK), carry, None)

            carry = lax.fori_loop(0, n_kb - 1, unmasked, carry)
            for (_, acc), lanes in zip(carry, HEAD_LANES):
                o_ref[0, pl.ds(r0, TQ), lanes] = acc
                za = za_ref[0, pl.ds(r0, TQ), lanes]
                yp_ref[0, pl.ds(r0, TQ), lanes] = (acc * (za * _sigmoid(za))).astype(BF16)
            return 0

        lax.fori_loop(0, nq, q_block, 0)

    def spec(c0):
        return pl.BlockSpec((1, s, LANES), lambda bi, hp: (bi, 0, c0 // LANES + hp))

    out_spec = pl.BlockSpec((1, s, LANES), lambda bi, hp: (bi, 0, hp))
    return pl.pallas_call(
        kern, name="attn_fwd",
        out_shape=(jax.ShapeDtypeStruct((b, s, D_MODEL), F32), jax.ShapeDtypeStruct((b, s, D_MODEL), BF16)),
        grid=(b, SB_HEADS // 2),
        in_specs=[spec(Q0), spec(K0), spec(V0), spec(ZA0)],
        out_specs=(out_spec, out_spec),
        scratch_shapes=[pltpu.VMEM((2, s, HEAD_DIM), BF16)] * 3,
        compiler_params=_cparams("parallel", "parallel"),
    )(proj3, proj3, proj3, proj3)


def _attn_bwd(proj3, dyp3, o3):
    b, s, _ = proj3.shape
    nq = s // TQ
    scale = HEAD_DIM ** -0.5

    def kern(q_ref, k_ref, v_ref, za_ref, dyp_ref, o_ref, dq_ref, dk_ref, dv_ref, dza_ref,
             qs, ks, vs, dos, dk_acc, dv_acc):
        _split_heads(qs, q_ref[0], scale)
        _split_heads(ks, k_ref[0])
        _split_heads(vs, v_ref[0])
        za = za_ref[0]
        sg = _sigmoid(za)
        dyp = dyp_ref[0]
        _split_heads(dos, dyp * (za * sg))
        dza_ref[0] = (dyp * o_ref[0] * (sg * (1.0 + za * (1.0 - sg)))).astype(BF16)
        dk_acc[...] = jnp.zeros_like(dk_acc)
        dv_acc[...] = jnp.zeros_like(dv_acc)
        row = lax.broadcasted_iota(jnp.int32, (TQ, TK), 0)
        col = lax.broadcasted_iota(jnp.int32, (TQ, TK), 1)
        tri_gt = _tri(lambda j, sk: j > sk)
        tri_ge = _tri(lambda j, sk: j >= sk)

        def q_block(i, _):
            r0 = pl.multiple_of(i * TQ, TQ)
            n_kb = (r0 + TQ + TK - 1) // TK
            qh = [qs[h, pl.ds(r0, TQ), :] for h in range(2)]
            doh = [dos[h, pl.ds(r0, TQ), :] for h in range(2)]
            totals = [jnp.sum(doh[h].astype(F32) * o_ref[0, pl.ds(r0, TQ), lanes], axis=1, keepdims=True)
                      for h, lanes in enumerate(HEAD_LANES)]

            def k_block(c0, carry, mask):
                kh = [ks[h, pl.ds(c0, TK), :] for h in range(2)]
                vh = [vs[h, pl.ds(c0, TK), :] for h in range(2)]
                z = [_dot(qh[h], kh[h], NT) for h in range(2)]
                da = [_dot(doh[h], vh[h], NT) for h in range(2)]
                logs, later = [], []
                for h in range(2):
                    logs.append(_sb_logs(z[h], mask))
                    later.append(_tri_dot(logs[h][1], tri_gt))
                ab, g, suffix = [], [], []
                for h in range(2):
                    a = _sb_weights(logs[h][0], later[h], carry[h][0], mask)
                    ab.append(a.astype(BF16))
                    g.append(da[h] * ab[h].astype(F32))
                    suffix.append(_tri_dot(g[h], tri_ge))
                out = []
                for h in range(2):
                    carry_r, carry_g, dq = carry[h]
                    lb, lom = logs[h]
                    dz = g[h] - (g[h] + (totals[h] - carry_g) - suffix[h]) * jnp.exp(lb)
                    if mask is not None:
                        dz = jnp.where(mask, dz, 0.0)
                    dzb = dz.astype(BF16)
                    dk_acc[h, pl.ds(c0, TK), :] += _dot(dzb, qh[h], TN)
                    dv_acc[h, pl.ds(c0, TK), :] += _dot(ab[h], doh[h], TN)
                    out.append((carry_r + (later[h][:, 0:1] + lom[:, 0:1]), carry_g + suffix[h][:, 0:1],
                                dq + _dot(dzb, kh[h], NN)))
                return tuple(out)

            c_last = pl.multiple_of((n_kb - 1) * TK, TK)
            zero = jnp.zeros((TQ, 1), F32)
            start = (zero, zero, jnp.zeros((TQ, HEAD_DIM), F32))
            carry = k_block(c_last, (start, start), col + c_last < row + r0)

            def unmasked(jj, carry):
                return k_block(pl.multiple_of((n_kb - 2 - jj) * TK, TK), carry, None)

            carry = lax.fori_loop(0, n_kb - 1, unmasked, carry)
            for (_, _, dq), lanes in zip(carry, HEAD_LANES):
                dq_ref[0, pl.ds(r0, TQ), lanes] = (dq * scale).astype(BF16)
            return 0

        lax.fori_loop(0, nq, q_block, 0)

        for h, lanes in enumerate(HEAD_LANES):
            dk_ref[0, :, lanes] = dk_acc[h].astype(BF16)
            dv_ref[0, :, lanes] = dv_acc[h].astype(BF16)

    def spec(c0):
        return pl.BlockSpec((1, s, LANES), lambda bi, hp: (bi, 0, c0 // LANES + hp))

    plain = pl.BlockSpec((1, s, LANES), lambda bi, hp: (bi, 0, hp))
    out = jax.ShapeDtypeStruct((b, s, D_MODEL), BF16)
    return pl.pallas_call(
        kern, name="attn_bwd",
        out_shape=(out, out, out, out),
        grid=(b, SB_HEADS // 2),
        in_specs=[spec(Q0), spec(K0), spec(V0), spec(ZA0), plain, plain],
        out_specs=(plain, plain, plain, plain),
        scratch_shapes=[pltpu.VMEM((2, s, HEAD_DIM), BF16)] * 4 + [pltpu.VMEM((2, s, HEAD_DIM), F32)] * 2,
        compiler_params=_cparams("parallel", "parallel"),
    )(proj3, proj3, proj3, proj3, dyp3, o3)


CONV_COLS = 256
HALO = 8


def _conv_pre(xp, w_ref, b_ref, r0):
    pre = b_ref[...] + w_ref[CONV_K - 1:CONV_K, :] * xp[pl.ds(HALO + r0, CHUNK), :]
    for kk in range(1, CONV_K):
        pre = pre + w_ref[CONV_K - 1 - kk:CONV_K - kk, :] * xp[pl.ds(HALO + r0 - kk, CHUNK), :]
    return pre


def _conv_fwd(proj3, conv_w, conv_b):
    b, s, _ = proj3.shape
    nc = s // CHUNK

    def kern(x_ref, w_ref, b_ref, o_ref, xp):
        xp[0:HALO, :] = jnp.zeros((HALO, CONV_COLS), F32)
        xp[HALO:, :] = x_ref[0]
        for ci in range(nc):
            pre = _conv_pre(xp, w_ref, b_ref, ci * CHUNK)
            o_ref[0, ci * CHUNK:(ci + 1) * CHUNK, :] = pre * _sigmoid(pre)

    return pl.pallas_call(
        kern, name="conv_fwd",
        out_shape=jax.ShapeDtypeStruct((b, s, CONV_DIM), F32),
        grid=(CONV_DIM // CONV_COLS, b),
        in_specs=[pl.BlockSpec((1, s, CONV_COLS), lambda j, bi: (bi, 0, XBC0 // CONV_COLS + j)),
                  pl.BlockSpec((CONV_K, CONV_COLS), lambda j, bi: (0, j)),
                  pl.BlockSpec((1, CONV_COLS), lambda j, bi: (0, j))],
        out_specs=pl.BlockSpec((1, s, CONV_COLS), lambda j, bi: (bi, 0, j)),
        scratch_shapes=[pltpu.VMEM((s + HALO, CONV_COLS), F32)],
        compiler_params=_cparams("parallel", "parallel"),
    )(proj3, conv_w, conv_b)


def _conv_bwd(dact, proj3, conv_w, conv_b, col0, name):
    b, s, width = dact.shape
    nc = s // CHUNK
    j0 = col0 // CONV_COLS

    def kern(da_ref, x_ref, w_ref, b_ref, dx_ref, dw_ref, db_ref, xp, dp):
        @pl.when(pl.program_id(1) == 0)
        def _():
            dw_ref[...] = jnp.zeros_like(dw_ref)
            db_ref[...] = jnp.zeros_like(db_ref)

        xp[0:HALO, :] = jnp.zeros((HALO, CONV_COLS), F32)
        xp[HALO:, :] = x_ref[0]
        dp[s:, :] = jnp.zeros((HALO, CONV_COLS), F32)
        for ci in range(nc):
            r0 = ci * CHUNK
            pre = _conv_pre(xp, w_ref, b_ref, r0)
            sg = _sigmoid(pre)
            dpre = da_ref[0, r0:r0 + CHUNK, :] * (sg * (1.0 + pre * (1.0 - sg)))
            dp[r0:r0 + CHUNK, :] = dpre
            db_ref[...] += jnp.sum(dpre, axis=0, keepdims=True)
            for kk in range(CONV_K):
                tap = CONV_K - 1 - kk
                dw_ref[tap:tap + 1, :] += jnp.sum(dpre * xp[pl.ds(HALO + r0 - kk, CHUNK), :], axis=0, keepdims=True)
        for ci in range(nc):
            r0 = ci * CHUNK
            dx = w_ref[CONV_K - 1:CONV_K, :] * dp[pl.ds(r0, CHUNK), :]
            for kk in range(1, CONV_K):
                dx = dx + w_ref[CONV_K - 1 - kk:CONV_K - kk, :] * dp[pl.ds(r0 + kk, CHUNK), :]
            dx_ref[0, r0:r0 + CHUNK, :] = dx.astype(BF16)

    return pl.pallas_call(
        kern, name=name,
        out_shape=(jax.ShapeDtypeStruct((b, s, width), BF16), jax.ShapeDtypeStruct((CONV_K, width), F32),
                   jax.ShapeDtypeStruct((1, width), F32)),
        grid=(width // CONV_COLS, b),
        in_specs=[pl.BlockSpec((1, s, CONV_COLS), lambda j, bi: (bi, 0, j)),
                  pl.BlockSpec((1, s, CONV_COLS), lambda j, bi: (bi, 0, XBC0 // CONV_COLS + j0 + j)),
                  pl.BlockSpec((CONV_K, CONV_COLS), lambda j, bi: (0, j0 + j)),
                  pl.BlockSpec((1, CONV_COLS), lambda j, bi: (0, j0 + j))],
        out_specs=(pl.BlockSpec((1, s, CONV_COLS), lambda j, bi: (bi, 0, j)),
                   pl.BlockSpec((CONV_K, CONV_COLS), lambda j, bi: (0, j)),
                   pl.BlockSpec((1, CONV_COLS), lambda j, bi: (0, j))),
        scratch_shapes=[pltpu.VMEM((s + HALO, CONV_COLS), F32)] * 2,
        compiler_params=_cparams("parallel", "arbitrary"),
    )(dact, proj3, conv_w, conv_b)


def _sel_dot(v, sel, left=False):
    hi = v.astype(BF16)
    rest = v - hi.astype(F32)
    mid = rest.astype(BF16)
    lo = (rest - mid.astype(F32)).astype(BF16)
    if left:
        return _dot(sel, hi, NN) + _dot(sel, mid, NN) + _dot(sel, lo, NN)
    return _dot(hi, sel, NN) + _dot(mid, sel, NN) + _dot(lo, sel, NN)


def _ssd_common(dtr_ref, dtb_ref, alog_ref):
    lane = lax.broadcasted_iota(jnp.int32, (CHUNK, LANES), 1)
    row = lax.broadcasted_iota(jnp.int32, (CHUNK, LANES), 0)
    head_lane = lane < HEADS_PER_GROUP
    pre = dtr_ref[0, 0] + dtb_ref[0]
    dt = jnp.where(head_lane, jnp.maximum(pre, 0.0) + jnp.log(1.0 + jnp.exp(-jnp.abs(pre))), 0.0)
    a = jnp.where(head_lane[0:1], -jnp.exp(alog_ref[0]), 0.0)
    tril = (row >= lane).astype(BF16)
    acs = _sel_dot(dt * a, tril, left=True)
    acs_t = acs.T
    er = lax.broadcasted_iota(jnp.int32, (LANES, GROUP_WIDTH), 0)
    ec = lax.broadcasted_iota(jnp.int32, (LANES, GROUP_WIDTH), 1)
    expand = ((ec // HEAD_DIM) == er).astype(BF16)
    tr = lax.broadcasted_iota(jnp.int32, (GROUP_WIDTH, LANES), 0)
    tc = lax.broadcasted_iota(jnp.int32, (GROUP_WIDTH, LANES), 1)
    reduce = ((tr // HEAD_DIM) == tc).astype(BF16)
    dt_x = _sel_dot(dt, expand)
    acs_x = _sel_dot(acs, expand)
    end_x = acs_x[CHUNK - 1:CHUNK, :]
    causal = row >= lane
    return dict(dt=dt, a=a, pre=pre, head_lane=head_lane, acs=acs, acs_t=acs_t, expand=expand, reduce=reduce,
                dt_x=dt_x, acs_x=acs_x, end_x=end_x, causal=causal, row=row, lane=lane)


def _ssd_decay(cm, h):
    seg = cm["acs"][:, h:h + 1] - cm["acs_t"][h:h + 1, :]
    return jnp.where(cm["causal"], jnp.exp(jnp.minimum(seg, 0.0)), 0.0)


def _ssd_fwd(xact, proj3, dtr_g, dtb_g, alog_g, dskip_x, snw):
    b, s, _ = xact.shape
    nc = s // CHUNK
    g4 = SSD_GROUPS

    def kern(xs_ref, bm_ref, cm_ref, zs_ref, dtr_ref, dtb_ref, alog_ref, dsk_ref, snw_ref,
             y_ref, yn_ref, hst_ref, h_sc):
        @pl.when(pl.program_id(2) == 0)
        def _():
            h_sc[...] = jnp.zeros_like(h_sc)

        cm = _ssd_common(dtr_ref, dtb_ref, alog_ref)
        x = xs_ref[0]
        bmb = bm_ref[0].astype(BF16)
        cmb = cm_ref[0].astype(BF16)
        h_in = h_sc[...]
        hst_ref[0, 0, 0] = h_in
        xdt = x * cm["dt_x"]
        xdtb = xdt.astype(BF16)
        cb = _dot(cmb, bmb, NT)
        y_off = _dot(cmb, h_in.astype(BF16), NN) * jnp.exp(cm["acs_x"])
        for h in range(HEADS_PER_GROUP):
            lanes = slice(h * HEAD_DIM, (h + 1) * HEAD_DIM)
            m = (cb * _ssd_decay(cm, h)).astype(BF16)
            y_ref[0, :, lanes] = _dot(m, xdtb[:, lanes], NN)
        y = y_ref[0] + y_off + x * dsk_ref[...]
        y_ref[0] = y
        w = (xdt * jnp.exp(cm["end_x"] - cm["acs_x"])).astype(BF16)
        h_sc[...] = h_in * jnp.exp(cm["end_x"]) + _dot(bmb, w, TN)
        zs = zs_ref[0]
        y2 = y * (zs * _sigmoid(zs))
        yn_ref[0] = (y2 * lax.rsqrt(jnp.mean(y2 * y2, axis=-1, keepdims=True) + EPS) * snw_ref[...]).astype(BF16)

    gw = GROUP_WIDTH
    small = pl.BlockSpec((1, 1, LANES), lambda gi, bi, ci: (gi, 0, 0))
    xblk = pl.BlockSpec((1, CHUNK, gw), lambda gi, bi, ci: (bi, ci, gi))
    return pl.pallas_call(
        kern, name="ssd_fwd",
        out_shape=(jax.ShapeDtypeStruct((b, s, SSD_WIDTH), F32), jax.ShapeDtypeStruct((b, s, SSD_WIDTH), BF16),
                   jax.ShapeDtypeStruct((b, nc, g4, SSD_STATE, gw), F32)),
        grid=(g4, b, nc),
        in_specs=[xblk,
                  pl.BlockSpec((1, CHUNK, LANES), lambda gi, bi, ci: (bi, ci, SSD_WIDTH // LANES + gi)),
                  pl.BlockSpec((1, CHUNK, LANES), lambda gi, bi, ci: (bi, ci, SSD_WIDTH // LANES + g4 + gi)),
                  pl.BlockSpec((1, CHUNK, gw), lambda gi, bi, ci: (bi, ci, ZS0 // gw + gi)),
                  pl.BlockSpec((1, 1, CHUNK, LANES), lambda gi, bi, ci: (bi, gi, ci, 0)),
                  small, small,
                  pl.BlockSpec((1, gw), lambda gi, bi, ci: (0, gi)),
                  pl.BlockSpec((1, gw), lambda gi, bi, ci: (0, gi))],
        out_specs=(xblk, xblk, pl.BlockSpec((1, 1, 1, SSD_STATE, gw), lambda gi, bi, ci: (bi, ci, gi, 0, 0))),
        scratch_shapes=[pltpu.VMEM((SSD_STATE, gw), F32)],
        compiler_params=_cparams("parallel", "parallel", "arbitrary"),
    )(xact, xact, xact, proj3, dtr_g, dtb_g, alog_g, dskip_x, snw)


def _ssd_bwd(dyn3, y3, xact, proj3, hst, dtr_g, dtb_g, alog_g, dskip_x, snw):
    b, s, _ = xact.shape
    nc = s // CHUNK
    g4 = SSD_GROUPS
    gw = GROUP_WIDTH

    def kern(dyn_ref, y_ref, xs_ref, bm_ref, cm_ref, zs_ref, hst_ref, dtr_ref, dtb_ref, alog_ref, dsk_ref, snw_ref,
             dxs_ref, dbm_ref, dcm_ref, dzs_ref, ddtr_ref, dsnw_ref, dalog_ref, ddtb_ref, ddsk_ref, dh_sc):
        first = jnp.logical_and(pl.program_id(1) == 0, pl.program_id(2) == 0)

        @pl.when(first)
        def _():
            dsnw_ref[...] = jnp.zeros_like(dsnw_ref)
            dalog_ref[...] = jnp.zeros_like(dalog_ref)
            ddtb_ref[...] = jnp.zeros_like(ddtb_ref)
            ddsk_ref[...] = jnp.zeros_like(ddsk_ref)

        @pl.when(pl.program_id(2) == 0)
        def _():
            dh_sc[...] = jnp.zeros_like(dh_sc)

        cm = _ssd_common(dtr_ref, dtb_ref, alog_ref)
        row, lane = cm["row"], cm["lane"]
        y = y_ref[0]
        zs = zs_ref[0]
        sg = _sigmoid(zs)
        silu = zs * sg
        y2 = y * silu
        rstd = lax.rsqrt(jnp.mean(y2 * y2, axis=-1, keepdims=True) + EPS)
        y2h = y2 * rstd
        dyn = dyn_ref[0]
        dsnw_ref[0] += jnp.sum(dyn * y2h, axis=0, keepdims=True)
        gwv = dyn * snw_ref[...]
        dy2 = rstd * (gwv - y2h * jnp.mean(gwv * y2h, axis=-1, keepdims=True))
        dzs_ref[0] = (dy2 * y * (sg * (1.0 + zs * (1.0 - sg)))).astype(BF16)
        dy = dy2 * silu
        dyb = dy.astype(BF16)

        x = xs_ref[0]
        bmb = bm_ref[0].astype(BF16)
        cmb = cm_ref[0].astype(BF16)
        h_in = hst_ref[0, 0, 0]
        h_inb = h_in.astype(BF16)
        d_hn = dh_sc[...]
        d_hnb = d_hn.astype(BF16)
        xdt = x * cm["dt_x"]
        xdtb = xdt.astype(BF16)
        eacs = jnp.exp(cm["acs_x"])
        dte = jnp.exp(cm["end_x"] - cm["acs_x"])
        wb = (xdt * dte).astype(BF16)

        dsk_lanes = jnp.broadcast_to(jnp.sum(dy * x, axis=0, keepdims=True), (8, gw))
        ddsk_ref[0] += _sel_dot(dsk_lanes, cm["reduce"])[0:1, :]
        dyo = dy * eacs
        dyob = dyo.astype(BF16)
        dacs_x = dyo * _dot(cmb, h_inb, NN)
        dcm = _dot(dyob, h_inb, NT)
        dh_in = _dot(cmb, dyob, TN)
        dw = _dot(bmb, d_hnb, NN)
        dbm = _dot(wb, d_hnb, NT)
        dxdt = dw * dte
        e_l = dw * xdt * dte
        dacs_x = dacs_x - e_l
        dend_x = jnp.sum(e_l, axis=0, keepdims=True)
        chunk_decay = jnp.exp(cm["end_x"])
        dh_sc[...] = d_hn * chunk_decay + dh_in
        dend_x = dend_x + jnp.sum(d_hn * h_in, axis=0, keepdims=True) * chunk_decay
        last_row = lax.broadcasted_iota(jnp.int32, (CHUNK, gw), 0) == CHUNK - 1
        dacs_x = dacs_x + jnp.where(last_row, dend_x, 0.0)

        cb = _dot(cmb, bmb, NT)
        dcb = jnp.zeros((CHUNK, CHUNK), F32)
        dacs = jnp.zeros((CHUNK, LANES), F32)
        dacs_t = jnp.zeros((LANES, CHUNK), F32)
        for h in range(HEADS_PER_GROUP):
            lanes = slice(h * HEAD_DIM, (h + 1) * HEAD_DIM)
            decay = _ssd_decay(cm, h)
            m = cb * decay
            dm = _dot(dyb[:, lanes], xdtb[:, lanes], NT)
            dxs_ref[0, :, lanes] = _dot(m.astype(BF16), dyb[:, lanes], TN)
            dcb_h = dm * decay
            dcb = dcb + dcb_h
            n = dcb_h * cb
            dacs = dacs + jnp.where(lane == h, jnp.sum(n, axis=1, keepdims=True), 0.0)
            dacs_t = dacs_t + jnp.where(row == h, jnp.sum(n, axis=0, keepdims=True), 0.0)
        dcbb = dcb.astype(BF16)
        dcm_ref[0] = dcm + _dot(dcbb, bmb, NN)
        dbm_ref[0] = dbm + _dot(dcbb, cmb, TN)
        dxdt = dxdt + dxs_ref[0]
        dxs_ref[0] = dy * dsk_ref[...] + dxdt * cm["dt_x"]

        dacs = dacs - dacs_t.T + _sel_dot(dacs_x, cm["reduce"])
        ddt = _sel_dot(dxdt * x, cm["reduce"])
        triu = (row <= lane).astype(BF16)
        rc = _sel_dot(dacs, triu, left=True)
        ddt = ddt + cm["a"] * rc
        dalog_ref[0] += jnp.sum(cm["dt"] * rc, axis=0, keepdims=True) * cm["a"]
        ddtr = jnp.where(cm["head_lane"], ddt * _sigmoid(cm["pre"]), 0.0)
        ddtr_ref[0, 0] = ddtr
        ddtb_ref[0] += jnp.sum(ddtr, axis=0, keepdims=True)

    def rev(ci):
        return nc - 1 - ci

    small = pl.BlockSpec((1, 1, LANES), lambda gi, bi, ci: (gi, 0, 0))
    xblk = pl.BlockSpec((1, CHUNK, gw), lambda gi, bi, ci: (bi, rev(ci), gi))
    nblk = pl.BlockSpec((1, CHUNK, LANES), lambda gi, bi, ci: (bi, rev(ci), gi))
    gvec = pl.BlockSpec((1, gw), lambda gi, bi, ci: (0, gi))
    gacc = pl.BlockSpec((1, 1, gw), lambda gi, bi, ci: (gi, 0, 0))
    return pl.pallas_call(
        kern, name="ssd_bwd",
        out_shape=(jax.ShapeDtypeStruct((b, s, SSD_WIDTH), F32),
                   jax.ShapeDtypeStruct((b, s, g4 * SSD_STATE), F32),
                   jax.ShapeDtypeStruct((b, s, g4 * SSD_STATE), F32),
                   jax.ShapeDtypeStruct((b, s, SSD_WIDTH), BF16),
                   jax.ShapeDtypeStruct((b, g4, s, LANES), F32),
                   jax.ShapeDtypeStruct((g4, 1, gw), F32),
                   jax.ShapeDtypeStruct((g4, 1, LANES), F32),
                   jax.ShapeDtypeStruct((g4, 1, LANES), F32),
                   jax.ShapeDtypeStruct((g4, 1, LANES), F32)),
        grid=(g4, b, nc),
        in_specs=[xblk, xblk, xblk,
                  pl.BlockSpec((1, CHUNK, LANES), lambda gi, bi, ci: (bi, rev(ci), SSD_WIDTH // LANES + gi)),
                  pl.BlockSpec((1, CHUNK, LANES), lambda gi, bi, ci: (bi, rev(ci), SSD_WIDTH // LANES + g4 + gi)),
                  pl.BlockSpec((1, CHUNK, gw), lambda gi, bi, ci: (bi, rev(ci), ZS0 // gw + gi)),
                  pl.BlockSpec((1, 1, 1, SSD_STATE, gw), lambda gi, bi, ci: (bi, rev(ci), gi, 0, 0)),
                  pl.BlockSpec((1, 1, CHUNK, LANES), lambda gi, bi, ci: (bi, gi, rev(ci), 0)),
                  small, small, gvec, gvec],
        out_specs=(xblk, nblk, nblk, xblk,
                   pl.BlockSpec((1, 1, CHUNK, LANES), lambda gi, bi, ci: (bi, gi, rev(ci), 0)),
                   gacc, small, small, small),
        scratch_shapes=[pltpu.VMEM((SSD_STATE, gw), F32)],
        compiler_params=_cparams("parallel", "arbitrary", "arbitrary"),
    )(dyn3, y3, xact, xact, xact, proj3, hst, dtr_g, dtb_g, alog_g, dskip_x, snw)


def _adamw(w, g, m, v, name):
    r, c = w.shape
    tr = 128 if r % 128 == 0 else r

    def kern(w_ref, g_ref, m_ref, v_ref, d_ref, nm_ref, nv_ref):
        gv = g_ref[...]
        nm = ADAM_B1 * m_ref[...] + (1.0 - ADAM_B1) * gv
        nv = ADAM_B2 * v_ref[...] + (1.0 - ADAM_B2) * (gv * gv)
        m_hat = nm / (1.0 - ADAM_B1 ** ADAM_STEP)
        v_hat = nv / (1.0 - ADAM_B2 ** ADAM_STEP)
        d_ref[...] = -ADAM_LR * (m_hat / (jnp.sqrt(v_hat) + ADAM_EPS) + ADAM_WD * w_ref[...])
        nm_ref[...] = nm
        nv_ref[...] = nv

    blk = pl.BlockSpec((tr, c), lambda i: (i, 0))
    out = jax.ShapeDtypeStruct((r, c), F32)
    return pl.pallas_call(
        kern, name=name, out_shape=(out, out, out), grid=(r // tr,),
        in_specs=[blk] * 4, out_specs=(blk, blk, blk),
        compiler_params=_cparams("parallel"),
    )(w, g, m, v)


ANY = pl.BlockSpec(memory_space=pl.ANY)


def _position():
    return lax.axis_index("x"), lax.axis_index("y"), lax.axis_index("c")


def _other_chips(x, y):
    return [(1 - x, y), (x, 1 - y), (1 - x, 1 - y)]


def _dma_sems(n):
    return [pltpu.SemaphoreType.DMA((n,)), pltpu.SemaphoreType.DMA((n,))]


def _gather_weights(shards):
    n = len(shards)

    def body(*refs):
        p_refs, out_refs = refs[:n], refs[n:2 * n]
        send_sems, recv_sems = refs[2 * n:]
        x, y, c = _position()
        me = 2 * x + y
        chips = _other_chips(x, y)

        def slab(a, chip, hf):
            half = shards[a].shape[0] // 2
            return out_refs[a].at[chip, pl.ds(hf * half, half), :]

        def my_half(a):
            half = shards[a].shape[0] // 2
            return p_refs[a].at[pl.ds(c * half, half), :]

        def over_ici(a, j, chip_from):
            px, py = chips[j]
            return pltpu.make_async_remote_copy(
                src_ref=my_half(a), dst_ref=slab(a, chip_from, c),
                send_sem=send_sems.at[3 * a + j], recv_sem=recv_sems.at[3 * a + j],
                device_id=(px, py, c), device_id_type=MESH)

        def to_sibling(a, j, hf):
            px, py = chips[j]
            return pltpu.make_async_remote_copy(
                src_ref=slab(a, 2 * px + py, hf), dst_ref=slab(a, 2 * px + py, hf),
                send_sem=send_sems.at[3 * (n + a) + j], recv_sem=recv_sems.at[3 * (n + a) + j],
                device_id=(x, y, 1 - c), device_id_type=MESH)

        first = [over_ici(a, j, me) for a in range(n) for j in range(3)]
        for cp in first:
            cp.start()
        passed = []
        for a in range(n):
            for j, (px, py) in enumerate(chips):
                over_ici(a, j, 2 * px + py).wait_recv()
                passed.append(to_sibling(a, j, c))
                passed[-1].start()
        for a in range(n):
            for j in range(3):
                to_sibling(a, j, 1 - c).wait_recv()
        for cp in first + passed:
            cp.wait_send()

    return pl.pallas_call(
        body, name="gather_weights",
        out_shape=[jax.ShapeDtypeStruct((N_CHIPS, *v.shape), v.dtype) for v in shards],
        in_specs=[ANY] * n, out_specs=[ANY] * n,
        scratch_shapes=_dma_sems(6 * n),
    )(*shards)


def _swap_halves(parts):
    n = len(parts)

    def body(*refs):
        v_refs, out_refs = refs[:n], refs[n:2 * n]
        send_sems, recv_sems = refs[2 * n:]
        x, y, c = _position()
        copies = []
        for a in range(n):
            half = parts[a].shape[1] // 2
            copies.append(pltpu.make_async_remote_copy(
                src_ref=v_refs[a].at[:, pl.ds((1 - c) * half, half), :], dst_ref=out_refs[a],
                send_sem=send_sems.at[a], recv_sem=recv_sems.at[a], device_id=(x, y, 1 - c), device_id_type=MESH))
        for cp in copies:
            cp.start()
        for cp in copies:
            cp.wait()

    return pl.pallas_call(
        body, name="grad_swap_halves",
        out_shape=[jax.ShapeDtypeStruct((v.shape[0], v.shape[1] // 2, v.shape[2]), v.dtype) for v in parts],
        in_specs=[ANY] * n, out_specs=[ANY] * n,
        scratch_shapes=_dma_sems(n),
    )(*parts)


def _chip_all_to_all(parts):
    n = len(parts)

    def body(*refs):
        p_refs, out_refs = refs[:n], refs[n:2 * n]
        send_sems, recv_sems = refs[2 * n:]
        x, y, c = _position()
        chips = _other_chips(x, y)
        sends = [pltpu.make_async_remote_copy(
            src_ref=p_refs[a].at[2 * px + py], dst_ref=out_refs[a].at[j],
            send_sem=send_sems.at[3 * a + j], recv_sem=recv_sems.at[3 * a + j],
            device_id=(px, py, c), device_id_type=MESH) for a in range(n) for j, (px, py) in enumerate(chips)]
        for cp in sends:
            cp.start()
        for cp in sends:
            cp.wait()

    return pl.pallas_call(
        body, name="grad_all_to_all",
        out_shape=[jax.ShapeDtypeStruct((N_CHIPS - 1, *v.shape[1:]), v.dtype) for v in parts],
        in_specs=[ANY] * n, out_specs=[ANY] * n,
        scratch_shapes=_dma_sems(3 * n),
    )(*parts)


def _join_halves(wholes):
    n = len(wholes)

    def body(*refs):
        out_refs = refs[n:2 * n]
        send_sems, recv_sems = refs[2 * n:]
        x, y, c = _position()
        copies = []
        for a in range(n):
            half = wholes[a].shape[0] // 2
            rows = out_refs[a].at[pl.ds(c * half, half), :]
            copies.append(pltpu.make_async_remote_copy(
                src_ref=rows, dst_ref=rows, send_sem=send_sems.at[a], recv_sem=recv_sems.at[a],
                device_id=(x, y, 1 - c), device_id_type=MESH))
        for cp in copies:
            cp.start()
        for cp in copies:
            cp.wait()

    return pl.pallas_call(
        body, name="grad_join_halves",
        out_shape=[jax.ShapeDtypeStruct(v.shape, v.dtype) for v in wholes],
        in_specs=[ANY] * n, out_specs=[ANY] * n,
        input_output_aliases={a: a for a in range(n)},
        scratch_shapes=_dma_sems(n),
    )(*wholes)


ADD_ROWS = 128


def _add_halves(g, sw, place, name):
    n, rows, cols = g.shape
    half = rows // 2
    nb = half // ADD_ROWS

    def kern(p_ref, g_ref, s_ref, o_ref):
        o_ref[...] = (g_ref[...] + s_ref[...]).astype(BF16)

    blk = pl.BlockSpec((1, ADD_ROWS, cols), lambda j, i, p_ref: (j, i, 0))
    return pl.pallas_call(
        kern, name=name,
        out_shape=jax.ShapeDtypeStruct((n, half, cols), BF16),
        grid_spec=pltpu.PrefetchScalarGridSpec(
            num_scalar_prefetch=1, grid=(n, nb),
            in_specs=[pl.BlockSpec((1, ADD_ROWS, cols), lambda j, i, p_ref: (j, p_ref[0] * nb + i, 0)), blk],
            out_specs=blk),
        compiler_params=_cparams("parallel", "parallel"),
    )(place, g, sw)


def _sum_chips(own, rx, place, name):
    _, half, cols = rx.shape
    nb = half // ADD_ROWS

    def kern(p_ref, own_ref, r_ref, o_ref):
        total = own_ref[0].astype(F32)
        for j in range(N_CHIPS - 1):
            total = total + r_ref[j].astype(F32)
        o_ref[...] = total

    return pl.pallas_call(
        kern, name=name,
        out_shape=jax.ShapeDtypeStruct((2 * half, cols), F32),
        grid_spec=pltpu.PrefetchScalarGridSpec(
            num_scalar_prefetch=1, grid=(nb,),
            in_specs=[pl.BlockSpec((1, ADD_ROWS, cols), lambda i, p_ref: (p_ref[1], i, 0)),
                      pl.BlockSpec((N_CHIPS - 1, ADD_ROWS, cols), lambda i, p_ref: (0, i, 0))],
            out_specs=pl.BlockSpec((ADD_ROWS, cols), lambda i, p_ref: (p_ref[0] * nb + i, 0))),
        compiler_params=_cparams("parallel"),
    )(place, own, rx)


def _gather_small(v, reduce, name):
    rows = v.shape[0]

    def body(v_ref, out_ref, buf, send_sems, recv_sems):
        x, y, c = _position()
        me = 4 * x + 2 * y + c
        buf[me] = v_ref[...]
        peers = [(x ^ (k >> 2), y ^ ((k >> 1) & 1), c ^ (k & 1)) for k in range(1, 8)]
        copies = [pltpu.make_async_remote_copy(
            src_ref=v_ref, dst_ref=buf.at[me],
            send_sem=send_sems.at[k], recv_sem=recv_sems.at[k],
            device_id=peer, device_id_type=MESH) for k, peer in enumerate(peers)]
        for cp in copies:
            cp.start()
        for k, (px, py, pc) in enumerate(peers):
            pltpu.make_async_remote_copy(
                src_ref=v_ref, dst_ref=buf.at[4 * px + 2 * py + pc],
                send_sem=send_sems.at[k], recv_sem=recv_sems.at[k],
                device_id=(px, py, pc), device_id_type=MESH).wait_recv()
        for cp in copies:
            cp.wait_send()
        if reduce:
            total = buf[0]
            for d in range(1, 8):
                total = total + buf[d]
            out_ref[...] = total
        else:
            out_ref[...] = buf[...]

    vm = pl.BlockSpec(memory_space=pltpu.VMEM)
    return pl.pallas_call(
        body, name=name,
        out_shape=jax.ShapeDtypeStruct((rows, LANES) if reduce else (8, rows, LANES), F32),
        in_specs=[vm], out_specs=vm,
        scratch_shapes=[pltpu.VMEM((8, rows, LANES), F32), pltpu.SemaphoreType.DMA((7,)), pltpu.SemaphoreType.DMA((7,))],
    )(v)


def _pad_rows(a, rows):
    return jnp.pad(a, ((0, rows - a.shape[0]), (0, 0)))


def _lane_pad(v):
    n = v.shape[1]
    return jnp.pad(v, ((0, 0), (0, -n % LANES)))


def _gather_all(w_in, w_attn_out, w_ssm_out, w_o, conv_w):
    d = D_MODEL
    own = [a[0].astype(BF16) for a in (w_in, w_attn_out, w_ssm_out, w_o)]
    gathered = _gather_weights(own)
    chip = 2 * lax.axis_index("x") + lax.axis_index("y")
    w_in_all, w_ao, w_so, w_oo = [[jnp.where(chip == q, o, g[q]) for q in range(N_CHIPS)] for o, g in zip(own, gathered)]
    last = w_in_all[N_CHIPS - 1]
    w_proj = jnp.concatenate([w_in_all[0], w_in_all[1], w_in_all[2], last[:, :LAST_DT0], last[:, LAST_DT0 + 32:],
                              last[:, LAST_DT0:LAST_DT0 + 32], jnp.zeros((d, DT_PAD - 32), BF16)], axis=1)
    w_ao = jnp.concatenate(w_ao, axis=0)
    w_so = jnp.concatenate(w_so, axis=0)
    w_oo = jnp.concatenate(w_oo, axis=0)
    conv_rows = conv_w[0].size // LANES
    conv_all = _gather_small(conv_w[0].reshape(conv_rows, LANES), False, "gather_conv_w")
    conv_w_all = conv_all[0::2].reshape(N_CHIPS, CONV_K, CONV_DIM // N_CHIPS).transpose(1, 0, 2).reshape(CONV_K, CONV_DIM)

    return w_proj, w_ao, w_so, w_oo, conv_w_all


def _local_step(x, loss_target, norm_w, w_proj, conv_w_all, conv_b, dt_bias, a_log, d_skip, ssm_norm_w,
                w_ao, w_so, w_oo, final_norm_w):
    b, s, d = x.shape
    t = b * s
    g4, hg = SSD_GROUPS, HEADS_PER_GROUP
    dtb_g = _lane_pad(dt_bias.reshape(g4, hg)).reshape(g4, 1, LANES)
    alog_g = _lane_pad(a_log.reshape(g4, hg)).reshape(g4, 1, LANES)
    dskip_x = jnp.repeat(d_skip, HEAD_DIM, axis=1)
    fnw = final_norm_w.reshape(1, d)

    x2 = x.reshape(t, d)
    h = _rms_fwd(x2, norm_w)
    proj = _matmul(h, w_proj, tm=512, tn=1280, tk=1024, name="proj")
    proj3 = proj.reshape(b, s, NP)
    o3, yp3 = _attn_fwd(proj3)
    xact = _conv_fwd(proj3, conv_w_all, conv_b)
    dtr = proj3[:, :, DT0:DT0 + g4 * hg].reshape(b, s, g4, hg).transpose(0, 2, 1, 3)
    dtr_g = jnp.pad(dtr, ((0, 0), (0, 0), (0, 0), (0, LANES - hg)))
    y3, yn3, hst = _ssd_fwd(xact, proj3, dtr_g, dtb_g, alog_g, dskip_x, ssm_norm_w)
    yp = yp3.reshape(t, D_MODEL)
    yn = yn3.reshape(t, SSD_WIDTH)
    ya = _matmul(yp, w_ao, tm=512, tn=1024, tk=1024, name="attn_out")
    ys = _matmul(yn, w_so, tm=512, tn=1024, tk=2048, name="ssm_out")
    merged = _merge_fwd(proj, ya, ys)
    mo = _matmul(merged, w_oo, tm=512, tn=1024, tk=1024, name="out_proj")
    dout, doutb, loss_part, d_fnw = _final_fwd_bwd(x2, mo, loss_target.reshape(t, d), fnw)

    dmerged = _matmul(doutb, w_oo, tb=True, tm=512, tn=1024, tk=1024, name="d_merged")
    g_wo = _matmul(merged, doutb, ta=True, tm=512, tn=1024, tk=1024, name="g_w_o")
    dya, dys, dgate = _merge_bwd(dmerged, proj, ya, ys)
    dyp = _matmul(dya, w_ao, tb=True, tm=512, tn=1024, tk=1024, name="d_attn_pre")
    g_wao = _matmul(yp, dya, ta=True, tm=512, tn=1024, tk=1024, name="g_w_attn_out")
    dyn = _matmul(dys, w_so, tb=True, tm=512, tn=2048, tk=1024, name="d_ssm_norm")
    g_wso = _matmul(yn, dys, ta=True, tm=512, tn=1024, tk=1024, name="g_w_ssm_out")
    dq, dk, dv, dza = _attn_bwd(proj3, dyp.reshape(b, s, D_MODEL), o3)
    (dxs, dbm, dcm, dzs, ddtr_g, d_snw_g, d_alog_g, d_dtb_g, d_dsk_g) = _ssd_bwd(
        dyn.reshape(b, s, SSD_WIDTH), y3, xact, proj3, hst, dtr_g, dtb_g, alog_g, dskip_x, ssm_norm_w)
    dx_xs, g_cw_xs, g_cb_xs = _conv_bwd(dxs, proj3, conv_w_all, conv_b, 0, "conv_bwd_x")
    dx_bm, g_cw_bm, g_cb_bm = _conv_bwd(dbm, proj3, conv_w_all, conv_b, SSD_WIDTH, "conv_bwd_b")
    dx_cm, g_cw_cm, g_cb_cm = _conv_bwd(dcm, proj3, conv_w_all, conv_b, SSD_WIDTH + g4 * SSD_STATE, "conv_bwd_c")
    ddt = ddtr_g[:, :, :, :hg].transpose(0, 2, 1, 3).reshape(b, s, g4 * hg).astype(BF16)
    dproj = jnp.concatenate([dq, dk, dv, dza, dzs, dx_xs, dx_bm, dx_cm, dgate.reshape(b, s, 2 * D_MODEL),
                             jnp.pad(ddt, ((0, 0), (0, 0), (0, DT_PAD - g4 * hg)))], axis=2).reshape(t, NP)
    g_wproj = _matmul(h, dproj, ta=True, tm=512, tn=1280, tk=1024, name="g_w_in")
    dh = _matmul(dproj, w_proj, tb=True, tm=512, tn=1024, tk=1280, name="d_h")
    grad_x, d_nw = _rms_bwd(dh, x2, norm_w, dout)
    g_cw = jnp.concatenate([g_cw_xs, g_cw_bm, g_cw_cm], axis=1)
    g_cb = jnp.concatenate([g_cb_xs, g_cb_bm, g_cb_cm], axis=1)
    return (loss_part, grad_x, d_nw, g_wproj, g_cw, g_cb, d_dtb_g, d_alog_g, d_dsk_g, d_snw_g, g_wao, g_wso, g_wo, d_fnw)


def kernel(x, norm_w, w_in, conv_w, conv_b, dt_bias, a_log, d_skip, ssm_norm_w, w_attn_out, w_ssm_out, w_o, final_norm_w, loss_target, m_norm_w, m_w_in, m_conv_w, m_conv_b, m_dt_bias, m_a_log, m_d_skip, m_ssm_norm_w, m_w_attn_out, m_w_ssm_out, m_w_o, m_final_norm_w, v_norm_w, v_w_in, v_conv_w, v_conv_b, v_dt_bias, v_a_log, v_d_skip, v_ssm_norm_w, v_w_attn_out, v_w_ssm_out, v_w_o, v_final_norm_w):
    b, s, d = x.shape
    core = lax.axis_index("c")
    g4, hg = SSD_GROUPS, HEADS_PER_GROUP
    shard_cols = w_in.shape[2]
    w_proj, w_ao, w_so, w_oo, conv_w_all = _gather_all(w_in, w_attn_out, w_ssm_out, w_o, conv_w)
    (loss_part, grad_x, d_nw, g_wproj, g_cw, g_cb, d_dtb_g, d_alog_g, d_dsk_g, d_snw_g, g_wao, g_wso, g_wo, d_fnw) = _local_step(
        x, loss_target, norm_w, w_proj, conv_w_all, conv_b, dt_bias, a_log, d_skip, ssm_norm_w, w_ao, w_so, w_oo, final_norm_w)

    last0 = (N_CHIPS - 1) * shard_cols
    g_last = jnp.concatenate([g_wproj[:, last0:GATE0], g_wproj[:, DT0:DT0 + 32], g_wproj[:, GATE0:DT0]], axis=1)
    g_win_chips = jnp.stack([g_wproj[:, j * shard_cols:(j + 1) * shard_cols] for j in range(N_CHIPS - 1)] + [g_last])
    g_out_chips = jnp.concatenate([g.reshape(N_CHIPS, -1, d) for g in (g_wao, g_wso, g_wo)], axis=1)
    parts = [g_win_chips, g_out_chips]
    chip = 2 * lax.axis_index("x") + lax.axis_index("y")
    place = jnp.stack([core, chip]).astype(jnp.int32)
    from_sibling = _swap_halves(parts)
    chip_sums = [_add_halves(p, f, place, "grad_add_halves_%d" % i) for i, (p, f) in enumerate(zip(parts, from_sibling))]
    from_chips = _chip_all_to_all(chip_sums)
    wholes = [_sum_chips(o, r, place, "grad_sum_chips_%d" % i) for i, (o, r) in enumerate(zip(chip_sums, from_chips))]
    g_w_in, g_out = _join_halves(wholes)

    small = jnp.concatenate([
        loss_part, d_nw, g_cb, _lane_pad(d_dtb_g[:, 0, :hg].reshape(1, -1)), _lane_pad(d_alog_g[:, 0, :hg].reshape(1, -1)),
        _lane_pad(d_dsk_g[:, 0, :hg].reshape(1, -1)),
        d_snw_g.reshape(1, -1), d_fnw, g_cw.reshape(1, -1)], axis=1)
    small_rows = small.shape[1] // LANES
    reduced = _gather_small(_pad_rows(small.reshape(small_rows, LANES), -(-small_rows // 8) * 8), True, "reduce_small")
    flat = reduced.reshape(-1)

    def take(start, n):
        return flat[start:start + n].reshape(1, n)

    loss = flat[0]
    pos = LANES
    g_norm_w = take(pos, d); pos += d
    g_conv_b = take(pos, CONV_DIM); pos += CONV_DIM
    g_dt_bias = take(pos, g4 * hg); pos += LANES
    g_a_log = take(pos, g4 * hg); pos += LANES
    g_d_skip = take(pos, g4 * hg); pos += LANES
    g_ssm_norm_w = take(pos, SSD_WIDTH); pos += SSD_WIDTH
    g_final_norm_w = take(pos, d); pos += d
    conv_cols = CONV_DIM // N_CHIPS
    g_conv_w = lax.dynamic_slice_in_dim(flat[pos:pos + CONV_K * CONV_DIM].reshape(CONV_K, CONV_DIM), chip * conv_cols, conv_cols, axis=1)

    rows_ao, rows_so = D_MODEL // N_CHIPS, SSD_WIDTH // N_CHIPS
    g_w_attn_out = g_out[:rows_ao]
    g_w_ssm_out = g_out[rows_ao:rows_ao + rows_so]
    g_w_o = g_out[rows_ao + rows_so:]

    names = ["norm_w", "w_in", "conv_w", "conv_b", "dt_bias", "a_log", "d_skip", "ssm_norm_w",
             "w_attn_out", "w_ssm_out", "w_o", "final_norm_w"]
    weights = [norm_w, w_in, conv_w, conv_b, dt_bias, a_log, d_skip, ssm_norm_w, w_attn_out, w_ssm_out, w_o, final_norm_w]
    grads = [g_norm_w, g_w_in, g_conv_w, g_conv_b, g_dt_bias, g_a_log, g_d_skip, g_ssm_norm_w,
             g_w_attn_out, g_w_ssm_out, g_w_o, g_final_norm_w]
    ms = [m_norm_w, m_w_in, m_conv_w, m_conv_b, m_dt_bias, m_a_log, m_d_skip, m_ssm_norm_w,
          m_w_attn_out, m_w_ssm_out, m_w_o, m_final_norm_w]
    vs = [v_norm_w, v_w_in, v_conv_w, v_conv_b, v_dt_bias, v_a_log, v_d_skip, v_ssm_norm_w,
          v_w_attn_out, v_w_ssm_out, v_w_o, v_final_norm_w]
    out_g, out_d, out_m, out_v = [], [], [], []
    for name, w, g, m, v in zip(names, weights, grads, ms, vs):
        shape2 = g.shape
        dlt, nm, nv = _adamw(w.reshape(shape2), g, m.reshape(shape2), v.reshape(shape2), "adamw_" + name)
        out_g.append(g.reshape(w.shape))
        out_d.append(dlt.reshape(w.shape))
        out_m.append(nm.reshape(w.shape))
        out_v.append(nv.reshape(w.shape))

    return (loss, grad_x.reshape(b, s, d), *out_g, *out_d, *out_m, *out_v)
```

```python
import jax
import jax.numpy as jnp
from jax import lax
from jax.experimental import pallas as pl
from jax.experimental.pallas import tpu as pltpu

F32 = jnp.float32
BF16 = jnp.bfloat16
MESH = pl.DeviceIdType.MESH

D_MODEL = 1024
SB_HEADS = 16
HEAD_DIM = 64
SSD_WIDTH = 2048
SSD_GROUPS = 4
GROUP_WIDTH = SSD_WIDTH // SSD_GROUPS
HEADS_PER_GROUP = 8
SSD_STATE = 128
CHUNK = 128
CONV_K = 4
CONV_DIM = 3072
D_PROJ = 11296
EPS = 1e-6
ADAM_LR, ADAM_B1, ADAM_B2, ADAM_EPS, ADAM_WD, ADAM_STEP = 0.001, 0.9, 0.999, 1e-08, 0.01, 10

LANES = 128
Q0, K0, V0, ZA0, ZS0, XBC0, GATE0, DT0 = 0, 1024, 2048, 3072, 4096, 6144, 9216, 11264
DT_PAD = 256
NP = DT0 + DT_PAD
N_CHIPS = 4
LAST_DT0 = GATE0 - (N_CHIPS - 1) * (D_PROJ // N_CHIPS)
VMEM_LIMIT = 56 * 1024 * 1024


def _cparams(*sem):
    return pltpu.CompilerParams(dimension_semantics=sem or None, vmem_limit_bytes=VMEM_LIMIT)


def _sigmoid(z):
    return 1.0 / (1.0 + jnp.exp(-z))


def _dot(a, b, dims, precision=None):
    return lax.dot_general(a, b, (dims, ((), ())), preferred_element_type=F32, precision=precision)


NN = ((1,), (0,))
NT = ((1,), (1,))
TN = ((0,), (0,))


def _matmul(a, b, *, ta=False, tb=False, out_dtype=F32, tm, tn, tk, name):
    m, k = (a.shape[1], a.shape[0]) if ta else a.shape
    n = b.shape[0] if tb else b.shape[1]
    assert m % tm == 0 and n % tn == 0 and k % tk == 0, (name, m, n, k)
    nk = k // tk
    use_scratch = out_dtype != F32
    dims = ((0,) if ta else (1,), (1,) if tb else (0,))

    def kern(a_ref, b_ref, o_ref, *scratch):
        acc = scratch[0] if use_scratch else o_ref
        kk = pl.program_id(2)

        @pl.when(kk == 0)
        def _():
            acc[...] = jnp.zeros_like(acc)

        acc[...] += _dot(a_ref[...], b_ref[...], dims)
        if use_scratch:
            @pl.when(kk == nk - 1)
            def _():
                o_ref[...] = acc[...].astype(out_dtype)

    a_spec = pl.BlockSpec((tk, tm), lambda i, j, q: (q, i)) if ta else pl.BlockSpec((tm, tk), lambda i, j, q: (i, q))
    b_spec = pl.BlockSpec((tn, tk), lambda i, j, q: (j, q)) if tb else pl.BlockSpec((tk, tn), lambda i, j, q: (q, j))
    return pl.pallas_call(
        kern, name=name,
        out_shape=jax.ShapeDtypeStruct((m, n), out_dtype),
        grid=(m // tm, n // tn, nk),
        in_specs=[a_spec, b_spec],
        out_specs=pl.BlockSpec((tm, tn), lambda i, j, q: (i, j)),
        scratch_shapes=[pltpu.VMEM((tm, tn), F32)] if use_scratch else [],
        compiler_params=_cparams("parallel", "parallel", "arbitrary"),
    )(a, b)


ROWS = 256


def _rms_fwd(x2, w):
    t, d = x2.shape

    def kern(x_ref, w_ref, h_ref):
        x = x_ref[...]
        r = lax.rsqrt(jnp.mean(x * x, axis=-1, keepdims=True) + EPS)
        h_ref[...] = (x * r * w_ref[...]).astype(BF16)

    return pl.pallas_call(
        kern, name="rms_fwd",
        out_shape=jax.ShapeDtypeStruct((t, d), BF16),
        grid=(t // ROWS,),
        in_specs=[pl.BlockSpec((ROWS, d), lambda i: (i, 0)), pl.BlockSpec((1, d), lambda i: (0, 0))],
        out_specs=pl.BlockSpec((ROWS, d), lambda i: (i, 0)),
        compiler_params=_cparams("parallel"),
    )(x2, w)


def _rms_bwd(dh, x2, w, dout):
    t, d = x2.shape

    def kern(dh_ref, x_ref, w_ref, dout_ref, gx_ref, dw_ref):
        @pl.when(pl.program_id(0) == 0)
        def _():
            dw_ref[...] = jnp.zeros_like(dw_ref)

        x = x_ref[...]
        r = lax.rsqrt(jnp.mean(x * x, axis=-1, keepdims=True) + EPS)
        xh = x * r
        g = dh_ref[...]
        dw_ref[...] += jnp.sum(g * xh, axis=0, keepdims=True)
        gw = g * w_ref[...]
        gx_ref[...] = dout_ref[...] + r * (gw - xh * jnp.mean(gw * xh, axis=-1, keepdims=True))

    row = pl.BlockSpec((ROWS, d), lambda i: (i, 0))
    vec = pl.BlockSpec((1, d), lambda i: (0, 0))
    return pl.pallas_call(
        kern, name="rms_bwd",
        out_shape=(jax.ShapeDtypeStruct((t, d), F32), jax.ShapeDtypeStruct((1, d), F32)),
        grid=(t // ROWS,),
        in_specs=[row, row, vec, row],
        out_specs=(row, vec),
        compiler_params=_cparams("arbitrary"),
    )(dh, x2, w, dout)


def _final_fwd_bwd(x2, mo, target, w):
    t, d = x2.shape

    def kern(x_ref, mo_ref, t_ref, w_ref, dout_ref, doutb_ref, loss_ref, dw_ref):
        @pl.when(pl.program_id(0) == 0)
        def _():
            loss_ref[...] = jnp.zeros_like(loss_ref)
            dw_ref[...] = jnp.zeros_like(dw_ref)

        u = x_ref[...] + mo_ref[...]
        r = lax.rsqrt(jnp.mean(u * u, axis=-1, keepdims=True) + EPS)
        uh = u * r
        wv = w_ref[...]
        err = uh * wv - t_ref[...]
        loss_ref[...] += (0.5 / d) * jnp.sum(err * err)
        dy = err * (1.0 / d)
        dw_ref[...] += jnp.sum(dy * uh, axis=0, keepdims=True)
        gw = dy * wv
        du = r * (gw - uh * jnp.mean(gw * uh, axis=-1, keepdims=True))
        dout_ref[...] = du
        doutb_ref[...] = du.astype(BF16)

    row = pl.BlockSpec((ROWS, d), lambda i: (i, 0))
    vec = pl.BlockSpec((1, d), lambda i: (0, 0))
    return pl.pallas_call(
        kern, name="final_fwd_bwd",
        out_shape=(jax.ShapeDtypeStruct((t, d), F32), jax.ShapeDtypeStruct((t, d), BF16),
                   jax.ShapeDtypeStruct((1, LANES), F32), jax.ShapeDtypeStruct((1, d), F32)),
        grid=(t // ROWS,),
        in_specs=[row, row, row, vec],
        out_specs=(row, row, pl.BlockSpec((1, LANES), lambda i: (0, 0)), vec),
        compiler_params=_cparams("arbitrary"),
    )(x2, mo, target, w)


def _merge_fwd(proj2, ya, ys):
    t = ya.shape[0]
    gblk = GATE0 // D_MODEL

    def kern(ga_ref, gs_ref, ya_ref, ys_ref, o_ref):
        o_ref[...] = (_sigmoid(ga_ref[...]) * ya_ref[...] + _sigmoid(gs_ref[...]) * ys_ref[...]).astype(BF16)

    row = pl.BlockSpec((ROWS, D_MODEL), lambda i: (i, 0))
    return pl.pallas_call(
        kern, name="merge_fwd",
        out_shape=jax.ShapeDtypeStruct((t, D_MODEL), BF16),
        grid=(t // ROWS,),
        in_specs=[pl.BlockSpec((ROWS, D_MODEL), lambda i: (i, gblk)),
                  pl.BlockSpec((ROWS, D_MODEL), lambda i: (i, gblk + 1)), row, row],
        out_specs=row,
        compiler_params=_cparams("parallel"),
    )(proj2, proj2, ya, ys)


def _merge_bwd(dm, proj2, ya, ys):
    t = ya.shape[0]
    gblk = GATE0 // D_MODEL

    def kern(dm_ref, ga_ref, gs_ref, ya_ref, ys_ref, dya_ref, dys_ref, dg_ref):
        g = dm_ref[...]
        sa = _sigmoid(ga_ref[...])
        ss = _sigmoid(gs_ref[...])
        dya_ref[...] = (g * sa).astype(BF16)
        dys_ref[...] = (g * ss).astype(BF16)
        dg_ref[:, :D_MODEL] = (g * ya_ref[...] * sa * (1.0 - sa)).astype(BF16)
        dg_ref[:, D_MODEL:] = (g * ys_ref[...] * ss * (1.0 - ss)).astype(BF16)

    row = pl.BlockSpec((ROWS, D_MODEL), lambda i: (i, 0))
    return pl.pallas_call(
        kern, name="merge_bwd",
        out_shape=(jax.ShapeDtypeStruct((t, D_MODEL), BF16), jax.ShapeDtypeStruct((t, D_MODEL), BF16),
                   jax.ShapeDtypeStruct((t, 2 * D_MODEL), BF16)),
        grid=(t // ROWS,),
        in_specs=[row, pl.BlockSpec((ROWS, D_MODEL), lambda i: (i, gblk)),
                  pl.BlockSpec((ROWS, D_MODEL), lambda i: (i, gblk + 1)), row, row],
        out_specs=(row, row, pl.BlockSpec((ROWS, 2 * D_MODEL), lambda i: (i, 0))),
        compiler_params=_cparams("parallel"),
    )(dm, proj2, proj2, ya, ys)


TQ = 256
TK = 256
HEAD_LANES = (slice(0, HEAD_DIM), slice(HEAD_DIM, 2 * HEAD_DIM))


def _tri(pred):
    r = lax.broadcasted_iota(jnp.int32, (TK, TK), 0)
    c = lax.broadcasted_iota(jnp.int32, (TK, TK), 1)
    return pred(r, c).astype(BF16)


def _split_bf16(v):
    hi = v.astype(BF16)
    lo = (v - hi.astype(F32)).astype(BF16)
    return hi, lo


def _tri_dot(v, tri):
    hi, lo = _split_bf16(v)
    return _dot(hi, tri, NN) + _dot(lo, tri, NN)


def _sb_logs(z, mask):
    l1p = jnp.log(1.0 + jnp.exp(-jnp.abs(z)))
    lb = jnp.minimum(z, 0.0) - l1p
    lom = -jnp.maximum(z, 0.0) - l1p
    if mask is not None:
        lom = jnp.where(mask, lom, 0.0)
    return lb, lom


def _sb_weights(lb, later, carry_r, mask):
    a = jnp.exp(lb + (later + carry_r))
    if mask is not None:
        a = jnp.where(mask, a, 0.0)
    return a


DEAD = -104.0


def _while_alive(n, carry, step):
    def alive(cr):
        return jnp.max(jnp.maximum(cr[0][0], cr[1][0])) > DEAD

    def cond(state):
        jj, go, _ = state
        return jnp.logical_and(jj < n, go)

    def body(state):
        jj, _, cr = state
        cr = step(jj, cr)
        return jj + 1, alive(cr), cr

    return lax.while_loop(cond, body, (jnp.int32(0), alive(carry), carry))[2]


def _split_heads(dst, src, scale=None):
    for h, lanes in enumerate(HEAD_LANES):
        v = src[:, lanes]
        dst[h] = (v if scale is None else v * scale).astype(BF16)


def _attn_fwd(proj3):
    b, s, _ = proj3.shape
    nq = s // TQ
    scale = HEAD_DIM ** -0.5

    def kern(q_ref, k_ref, v_ref, za_ref, o_ref, yp_ref, qs, ks, vs):
        _split_heads(qs, q_ref[0], scale)
        _split_heads(ks, k_ref[0])
        _split_heads(vs, v_ref[0])
        row = lax.broadcasted_iota(jnp.int32, (TQ, TK), 0)
        col = lax.broadcasted_iota(jnp.int32, (TQ, TK), 1)
        tri_gt = _tri(lambda j, sk: j > sk)

        def q_block(i, _):
            r0 = pl.multiple_of(i * TQ, TQ)
            n_kb = (r0 + TQ + TK - 1) // TK
            qh = [qs[h, pl.ds(r0, TQ), :] for h in range(2)]

            def k_block(c0, carry, mask):
                kh = [ks[h, pl.ds(c0, TK), :] for h in range(2)]
                vh = [vs[h, pl.ds(c0, TK), :] for h in range(2)]
                z = [_dot(qh[h], kh[h], NT) for h in range(2)]
                logs, later = [], []
                for h in range(2):
                    logs.append(_sb_logs(z[h], mask))
                    later.append(_tri_dot(logs[h][1], tri_gt))
                out = []
                for h in range(2):
                    carry_r, acc = carry[h]
                    lb, lom = logs[h]
                    a = _sb_weights(lb, later[h], carry_r, mask)
                    row_sum = later[h][:, 0:1] + lom[:, 0:1]
                    out.append((carry_r + row_sum, acc + _dot(a.astype(BF16), vh[h], NN)))
                return tuple(out)

            c_last = pl.multiple_of((n_kb - 1) * TK, TK)
            start = (jnp.zeros((TQ, 1), F32), jnp.zeros((TQ, HEAD_DIM), F32))
            carry = k_block(c_last, (start, start), col + c_last < row + r0)

            carry = _while_alive(n_kb - 1, carry, lambda jj, cr: k_block(pl.multiple_of((n_kb - 2 - jj) * TK, TK), cr, None))
            for (_, acc), lanes in zip(carry, HEAD_LANES):
                o_ref[0, pl.ds(r0, TQ), lanes] = acc
                za = za_ref[0, pl.ds(r0, TQ), lanes]
                yp_ref[0, pl.ds(r0, TQ), lanes] = (acc * (za * _sigmoid(za))).astype(BF16)
            return 0

        lax.fori_loop(0, nq, q_block, 0)

    def spec(c0):
        return pl.BlockSpec((1, s, LANES), lambda bi, hp: (bi, 0, c0 // LANES + hp))

    out_spec = pl.BlockSpec((1, s, LANES), lambda bi, hp: (bi, 0, hp))
    return pl.pallas_call(
        kern, name="attn_fwd",
        out_shape=(jax.ShapeDtypeStruct((b, s, D_MODEL), F32), jax.ShapeDtypeStruct((b, s, D_MODEL), BF16)),
        grid=(b, SB_HEADS // 2),
        in_specs=[spec(Q0), spec(K0), spec(V0), spec(ZA0)],
        out_specs=(out_spec, out_spec),
        scratch_shapes=[pltpu.VMEM((2, s, HEAD_DIM), BF16)] * 3,
        compiler_params=_cparams("parallel", "parallel"),
    )(proj3, proj3, proj3, proj3)


def _attn_bwd(proj3, dyp3, o3):
    b, s, _ = proj3.shape
    nq = s // TQ
    scale = HEAD_DIM ** -0.5

    def kern(q_ref, k_ref, v_ref, za_ref, dyp_ref, o_ref, dq_ref, dk_ref, dv_ref, dza_ref,
             qs, ks, vs, dos, dk_acc, dv_acc):
        _split_heads(qs, q_ref[0], scale)
        _split_heads(ks, k_ref[0])
        _split_heads(vs, v_ref[0])
        za = za_ref[0]
        sg = _sigmoid(za)
        dyp = dyp_ref[0]
        _split_heads(dos, dyp * (za * sg))
        dza_ref[0] = (dyp * o_ref[0] * (sg * (1.0 + za * (1.0 - sg)))).astype(BF16)
        dk_acc[...] = jnp.zeros_like(dk_acc)
        dv_acc[...] = jnp.zeros_like(dv_acc)
        row = lax.broadcasted_iota(jnp.int32, (TQ, TK), 0)
        col = lax.broadcasted_iota(jnp.int32, (TQ, TK), 1)
        tri_gt = _tri(lambda j, sk: j > sk)
        tri_ge = _tri(lambda j, sk: j >= sk)

        def q_block(i, _):
            r0 = pl.multiple_of(i * TQ, TQ)
            n_kb = (r0 + TQ + TK - 1) // TK
            qh = [qs[h, pl.ds(r0, TQ), :] for h in range(2)]
            doh = [dos[h, pl.ds(r0, TQ), :] for h in range(2)]
            totals = [jnp.sum(doh[h].astype(F32) * o_ref[0, pl.ds(r0, TQ), lanes], axis=1, keepdims=True)
                      for h, lanes in enumerate(HEAD_LANES)]

            def k_block(c0, carry, mask):
                kh = [ks[h, pl.ds(c0, TK), :] for h in range(2)]
                vh = [vs[h, pl.ds(c0, TK), :] for h in range(2)]
                z = [_dot(qh[h], kh[h], NT) for h in range(2)]
                da = [_dot(doh[h], vh[h], NT) for h in range(2)]
                logs, later = [], []
                for h in range(2):
                    logs.append(_sb_logs(z[h], mask))
                    later.append(_tri_dot(logs[h][1], tri_gt))
                ab, g, suffix = [], [], []
                for h in range(2):
                    a = _sb_weights(logs[h][0], later[h], carry[h][0], mask)
                    ab.append(a.astype(BF16))
                    g.append(da[h] * ab[h].astype(F32))
                    suffix.append(_tri_dot(g[h], tri_ge))
                out = []
                for h in range(2):
                    carry_r, carry_g, dq = carry[h]
                    lb, lom = logs[h]
                    dz = g[h] - (g[h] + (totals[h] - carry_g) - suffix[h]) * jnp.exp(lb)
                    if mask is not None:
                        dz = jnp.where(mask, dz, 0.0)
                    dzb = dz.astype(BF16)
                    dk_acc[h, pl.ds(c0, TK), :] += _dot(dzb, qh[h], TN)
                    dv_acc[h, pl.ds(c0, TK), :] += _dot(ab[h], doh[h], TN)
                    out.append((carry_r + (later[h][:, 0:1] + lom[:, 0:1]), carry_g + suffix[h][:, 0:1],
                                dq + _dot(dzb, kh[h], NN)))
                return tuple(out)

            c_last = pl.multiple_of((n_kb - 1) * TK, TK)
            zero = jnp.zeros((TQ, 1), F32)
            start = (zero, zero, jnp.zeros((TQ, HEAD_DIM), F32))
            carry = k_block(c_last, (start, start), col + c_last < row + r0)

            carry = _while_alive(n_kb - 1, carry, lambda jj, cr: k_block(pl.multiple_of((n_kb - 2 - jj) * TK, TK), cr, None))
            for (_, _, dq), lanes in zip(carry, HEAD_LANES):
                dq_ref[0, pl.ds(r0, TQ), lanes] = (dq * scale).astype(BF16)
            return 0

        lax.fori_loop(0, nq, q_block, 0)

        for h, lanes in enumerate(HEAD_LANES):
            dk_ref[0, :, lanes] = dk_acc[h].astype(BF16)
            dv_ref[0, :, lanes] = dv_acc[h].astype(BF16)

    def spec(c0):
        return pl.BlockSpec((1, s, LANES), lambda bi, hp: (bi, 0, c0 // LANES + hp))

    plain = pl.BlockSpec((1, s, LANES), lambda bi, hp: (bi, 0, hp))
    out = jax.ShapeDtypeStruct((b, s, D_MODEL), BF16)
    return pl.pallas_call(
        kern, name="attn_bwd",
        out_shape=(out, out, out, out),
        grid=(b, SB_HEADS // 2),
        in_specs=[spec(Q0), spec(K0), spec(V0), spec(ZA0), plain, plain],
        out_specs=(plain, plain, plain, plain),
        scratch_shapes=[pltpu.VMEM((2, s, HEAD_DIM), BF16)] * 4 + [pltpu.VMEM((2, s, HEAD_DIM), F32)] * 2,
        compiler_params=_cparams("parallel", "parallel"),
    )(proj3, proj3, proj3, proj3, dyp3, o3)


CONV_COLS = 256
HALO = 8


def _conv_pre(xp, w_ref, b_ref, r0):
    pre = b_ref[...] + w_ref[CONV_K - 1:CONV_K, :] * xp[pl.ds(HALO + r0, CHUNK), :]
    for kk in range(1, CONV_K):
        pre = pre + w_ref[CONV_K - 1 - kk:CONV_K - kk, :] * xp[pl.ds(HALO + r0 - kk, CHUNK), :]
    return pre


def _conv_fwd(proj3, conv_w, conv_b):
    b, s, _ = proj3.shape
    nc = s // CHUNK

    def kern(x_ref, w_ref, b_ref, o_ref, xp):
        xp[0:HALO, :] = jnp.zeros((HALO, CONV_COLS), F32)
        xp[HALO:, :] = x_ref[0]
        for ci in range(nc):
            pre = _conv_pre(xp, w_ref, b_ref, ci * CHUNK)
            o_ref[0, ci * CHUNK:(ci + 1) * CHUNK, :] = pre * _sigmoid(pre)

    return pl.pallas_call(
        kern, name="conv_fwd",
        out_shape=jax.ShapeDtypeStruct((b, s, CONV_DIM), F32),
        grid=(CONV_DIM // CONV_COLS, b),
        in_specs=[pl.BlockSpec((1, s, CONV_COLS), lambda j, bi: (bi, 0, XBC0 // CONV_COLS + j)),
                  pl.BlockSpec((CONV_K, CONV_COLS), lambda j, bi: (0, j)),
                  pl.BlockSpec((1, CONV_COLS), lambda j, bi: (0, j))],
        out_specs=pl.BlockSpec((1, s, CONV_COLS), lambda j, bi: (bi, 0, j)),
        scratch_shapes=[pltpu.VMEM((s + HALO, CONV_COLS), F32)],
        compiler_params=_cparams("parallel", "parallel"),
    )(proj3, conv_w, conv_b)


def _conv_bwd(dact, proj3, conv_w, conv_b, col0, name):
    b, s, width = dact.shape
    nc = s // CHUNK
    j0 = col0 // CONV_COLS

    def kern(da_ref, x_ref, w_ref, b_ref, dx_ref, dw_ref, db_ref, xp, dp):
        @pl.when(pl.program_id(1) == 0)
        def _():
            dw_ref[...] = jnp.zeros_like(dw_ref)
            db_ref[...] = jnp.zeros_like(db_ref)

        xp[0:HALO, :] = jnp.zeros((HALO, CONV_COLS), F32)
        xp[HALO:, :] = x_ref[0]
        dp[s:, :] = jnp.zeros((HALO, CONV_COLS), F32)
        for ci in range(nc):
            r0 = ci * CHUNK
            pre = _conv_pre(xp, w_ref, b_ref, r0)
            sg = _sigmoid(pre)
            dpre = da_ref[0, r0:r0 + CHUNK, :] * (sg * (1.0 + pre * (1.0 - sg)))
            dp[r0:r0 + CHUNK, :] = dpre
            db_ref[...] += jnp.sum(dpre, axis=0, keepdims=True)
            for kk in range(CONV_K):
                tap = CONV_K - 1 - kk
                dw_ref[tap:tap + 1, :] += jnp.sum(dpre * xp[pl.ds(HALO + r0 - kk, CHUNK), :], axis=0, keepdims=True)
        for ci in range(nc):
            r0 = ci * CHUNK
            dx = w_ref[CONV_K - 1:CONV_K, :] * dp[pl.ds(r0, CHUNK), :]
            for kk in range(1, CONV_K):
                dx = dx + w_ref[CONV_K - 1 - kk:CONV_K - kk, :] * dp[pl.ds(r0 + kk, CHUNK), :]
            dx_ref[0, r0:r0 + CHUNK, :] = dx.astype(BF16)

    return pl.pallas_call(
        kern, name=name,
        out_shape=(jax.ShapeDtypeStruct((b, s, width), BF16), jax.ShapeDtypeStruct((CONV_K, width), F32),
                   jax.ShapeDtypeStruct((1, width), F32)),
        grid=(width // CONV_COLS, b),
        in_specs=[pl.BlockSpec((1, s, CONV_COLS), lambda j, bi: (bi, 0, j)),
                  pl.BlockSpec((1, s, CONV_COLS), lambda j, bi: (bi, 0, XBC0 // CONV_COLS + j0 + j)),
                  pl.BlockSpec((CONV_K, CONV_COLS), lambda j, bi: (0, j0 + j)),
                  pl.BlockSpec((1, CONV_COLS), lambda j, bi: (0, j0 + j))],
        out_specs=(pl.BlockSpec((1, s, CONV_COLS), lambda j, bi: (bi, 0, j)),
                   pl.BlockSpec((CONV_K, CONV_COLS), lambda j, bi: (0, j)),
                   pl.BlockSpec((1, CONV_COLS), lambda j, bi: (0, j))),
        scratch_shapes=[pltpu.VMEM((s + HALO, CONV_COLS), F32)] * 2,
        compiler_params=_cparams("parallel", "arbitrary"),
    )(dact, proj3, conv_w, conv_b)


def _sel_dot(v, sel, left=False):
    hi = v.astype(BF16)
    rest = v - hi.astype(F32)
    mid = rest.astype(BF16)
    lo = (rest - mid.astype(F32)).astype(BF16)
    if left:
        return _dot(sel, hi, NN) + _dot(sel, mid, NN) + _dot(sel, lo, NN)
    return _dot(hi, sel, NN) + _dot(mid, sel, NN) + _dot(lo, sel, NN)


def _ssd_common(dtr_ref, dtb_ref, alog_ref):
    lane = lax.broadcasted_iota(jnp.int32, (CHUNK, LANES), 1)
    row = lax.broadcasted_iota(jnp.int32, (CHUNK, LANES), 0)
    head_lane = lane < HEADS_PER_GROUP
    pre = dtr_ref[0, 0] + dtb_ref[0]
    dt = jnp.where(head_lane, jnp.maximum(pre, 0.0) + jnp.log(1.0 + jnp.exp(-jnp.abs(pre))), 0.0)
    a = jnp.where(head_lane[0:1], -jnp.exp(alog_ref[0]), 0.0)
    tril = (row >= lane).astype(BF16)
    acs = _sel_dot(dt * a, tril, left=True)
    acs_t = acs.T
    er = lax.broadcasted_iota(jnp.int32, (LANES, GROUP_WIDTH), 0)
    ec = lax.broadcasted_iota(jnp.int32, (LANES, GROUP_WIDTH), 1)
    expand = ((ec // HEAD_DIM) == er).astype(BF16)
    tr = lax.broadcasted_iota(jnp.int32, (GROUP_WIDTH, LANES), 0)
    tc = lax.broadcasted_iota(jnp.int32, (GROUP_WIDTH, LANES), 1)
    reduce = ((tr // HEAD_DIM) == tc).astype(BF16)
    dt_x = _sel_dot(dt, expand)
    acs_x = _sel_dot(acs, expand)
    end_x = acs_x[CHUNK - 1:CHUNK, :]
    causal = row >= lane
    return dict(dt=dt, a=a, pre=pre, head_lane=head_lane, acs=acs, acs_t=acs_t, expand=expand, reduce=reduce,
                dt_x=dt_x, acs_x=acs_x, end_x=end_x, causal=causal, row=row, lane=lane)


def _ssd_decay(cm, h):
    seg = cm["acs"][:, h:h + 1] - cm["acs_t"][h:h + 1, :]
    return jnp.where(cm["causal"], jnp.exp(jnp.minimum(seg, 0.0)), 0.0)


def _ssd_fwd(xact, proj3, dtr_g, dtb_g, alog_g, dskip_x, snw):
    b, s, _ = xact.shape
    nc = s // CHUNK
    g4 = SSD_GROUPS

    def kern(xs_ref, bm_ref, cm_ref, zs_ref, dtr_ref, dtb_ref, alog_ref, dsk_ref, snw_ref,
             y_ref, yn_ref, hst_ref, h_sc):
        @pl.when(pl.program_id(2) == 0)
        def _():
            h_sc[...] = jnp.zeros_like(h_sc)

        cm = _ssd_common(dtr_ref, dtb_ref, alog_ref)
        x = xs_ref[0]
        bmb = bm_ref[0].astype(BF16)
        cmb = cm_ref[0].astype(BF16)
        h_in = h_sc[...]
        hst_ref[0, 0, 0] = h_in
        xdt = x * cm["dt_x"]
        xdtb = xdt.astype(BF16)
        cb = _dot(cmb, bmb, NT)
        y_off = _dot(cmb, h_in.astype(BF16), NN) * jnp.exp(cm["acs_x"])
        for h in range(HEADS_PER_GROUP):
            lanes = slice(h * HEAD_DIM, (h + 1) * HEAD_DIM)
            m = (cb * _ssd_decay(cm, h)).astype(BF16)
            y_ref[0, :, lanes] = _dot(m, xdtb[:, lanes], NN)
        y = y_ref[0] + y_off + x * dsk_ref[...]
        y_ref[0] = y
        w = (xdt * jnp.exp(cm["end_x"] - cm["acs_x"])).astype(BF16)
        h_sc[...] = h_in * jnp.exp(cm["end_x"]) + _dot(bmb, w, TN)
        zs = zs_ref[0]
        y2 = y * (zs * _sigmoid(zs))
        yn_ref[0] = (y2 * lax.rsqrt(jnp.mean(y2 * y2, axis=-1, keepdims=True) + EPS) * snw_ref[...]).astype(BF16)

    gw = GROUP_WIDTH
    small = pl.BlockSpec((1, 1, LANES), lambda gi, bi, ci: (gi, 0, 0))
    xblk = pl.BlockSpec((1, CHUNK, gw), lambda gi, bi, ci: (bi, ci, gi))
    return pl.pallas_call(
        kern, name="ssd_fwd",
        out_shape=(jax.ShapeDtypeStruct((b, s, SSD_WIDTH), F32), jax.ShapeDtypeStruct((b, s, SSD_WIDTH), BF16),
                   jax.ShapeDtypeStruct((b, nc, g4, SSD_STATE, gw), F32)),
        grid=(g4, b, nc),
        in_specs=[xblk,
                  pl.BlockSpec((1, CHUNK, LANES), lambda gi, bi, ci: (bi, ci, SSD_WIDTH // LANES + gi)),
                  pl.BlockSpec((1, CHUNK, LANES), lambda gi, bi, ci: (bi, ci, SSD_WIDTH // LANES + g4 + gi)),
                  pl.BlockSpec((1, CHUNK, gw), lambda gi, bi, ci: (bi, ci, ZS0 // gw + gi)),
                  pl.BlockSpec((1, 1, CHUNK, LANES), lambda gi, bi, ci: (bi, gi, ci, 0)),
                  small, small,
                  pl.BlockSpec((1, gw), lambda gi, bi, ci: (0, gi)),
                  pl.BlockSpec((1, gw), lambda gi, bi, ci: (0, gi))],
        out_specs=(xblk, xblk, pl.BlockSpec((1, 1, 1, SSD_STATE, gw), lambda gi, bi, ci: (bi, ci, gi, 0, 0))),
        scratch_shapes=[pltpu.VMEM((SSD_STATE, gw), F32)],
        compiler_params=_cparams("parallel", "parallel", "arbitrary"),
    )(xact, xact, xact, proj3, dtr_g, dtb_g, alog_g, dskip_x, snw)


def _ssd_bwd(dyn3, y3, xact, proj3, hst, dtr_g, dtb_g, alog_g, dskip_x, snw):
    b, s, _ = xact.shape
    nc = s // CHUNK
    g4 = SSD_GROUPS
    gw = GROUP_WIDTH

    def kern(dyn_ref, y_ref, xs_ref, bm_ref, cm_ref, zs_ref, hst_ref, dtr_ref, dtb_ref, alog_ref, dsk_ref, snw_ref,
             dxs_ref, dbm_ref, dcm_ref, dzs_ref, ddtr_ref, dsnw_ref, dalog_ref, ddtb_ref, ddsk_ref, dh_sc):
        first = jnp.logical_and(pl.program_id(1) == 0, pl.program_id(2) == 0)

        @pl.when(first)
        def _():
            dsnw_ref[...] = jnp.zeros_like(dsnw_ref)
            dalog_ref[...] = jnp.zeros_like(dalog_ref)
            ddtb_ref[...] = jnp.zeros_like(ddtb_ref)
            ddsk_ref[...] = jnp.zeros_like(ddsk_ref)

        @pl.when(pl.program_id(2) == 0)
        def _():
            dh_sc[...] = jnp.zeros_like(dh_sc)

        cm = _ssd_common(dtr_ref, dtb_ref, alog_ref)
        row, lane = cm["row"], cm["lane"]
        y = y_ref[0]
        zs = zs_ref[0]
        sg = _sigmoid(zs)
        silu = zs * sg
        y2 = y * silu
        rstd = lax.rsqrt(jnp.mean(y2 * y2, axis=-1, keepdims=True) + EPS)
        y2h = y2 * rstd
        dyn = dyn_ref[0]
        dsnw_ref[0] += jnp.sum(dyn * y2h, axis=0, keepdims=True)
        gwv = dyn * snw_ref[...]
        dy2 = rstd * (gwv - y2h * jnp.mean(gwv * y2h, axis=-1, keepdims=True))
        dzs_ref[0] = (dy2 * y * (sg * (1.0 + zs * (1.0 - sg)))).astype(BF16)
        dy = dy2 * silu
        dyb = dy.astype(BF16)

        x = xs_ref[0]
        bmb = bm_ref[0].astype(BF16)
        cmb = cm_ref[0].astype(BF16)
        h_in = hst_ref[0, 0, 0]
        h_inb = h_in.astype(BF16)
        d_hn = dh_sc[...]
        d_hnb = d_hn.astype(BF16)
        xdt = x * cm["dt_x"]
        xdtb = xdt.astype(BF16)
        eacs = jnp.exp(cm["acs_x"])
        dte = jnp.exp(cm["end_x"] - cm["acs_x"])
        wb = (xdt * dte).astype(BF16)

        dsk_lanes = jnp.broadcast_to(jnp.sum(dy * x, axis=0, keepdims=True), (8, gw))
        ddsk_ref[0] += _sel_dot(dsk_lanes, cm["reduce"])[0:1, :]
        dyo = dy * eacs
        dyob = dyo.astype(BF16)
        dacs_x = dyo * _dot(cmb, h_inb, NN)
        dcm = _dot(dyob, h_inb, NT)
        dh_in = _dot(cmb, dyob, TN)
        dw = _dot(bmb, d_hnb, NN)
        dbm = _dot(wb, d_hnb, NT)
        dxdt = dw * dte
        e_l = dw * xdt * dte
        dacs_x = dacs_x - e_l
        dend_x = jnp.sum(e_l, axis=0, keepdims=True)
        chunk_decay = jnp.exp(cm["end_x"])
        dh_sc[...] = d_hn * chunk_decay + dh_in
        dend_x = dend_x + jnp.sum(d_hn * h_in, axis=0, keepdims=True) * chunk_decay
        last_row = lax.broadcasted_iota(jnp.int32, (CHUNK, gw), 0) == CHUNK - 1
        dacs_x = dacs_x + jnp.where(last_row, dend_x, 0.0)

        cb = _dot(cmb, bmb, NT)
        dcb = jnp.zeros((CHUNK, CHUNK), F32)
        dacs = jnp.zeros((CHUNK, LANES), F32)
        dacs_t = jnp.zeros((LANES, CHUNK), F32)
        for h in range(HEADS_PER_GROUP):
            lanes = slice(h * HEAD_DIM, (h + 1) * HEAD_DIM)
            decay = _ssd_decay(cm, h)
            m = cb * decay
            dm = _dot(dyb[:, lanes], xdtb[:, lanes], NT)
            dxs_ref[0, :, lanes] = _dot(m.astype(BF16), dyb[:, lanes], TN)
            dcb_h = dm * decay
            dcb = dcb + dcb_h
            n = dcb_h * cb
            dacs = dacs + jnp.where(lane == h, jnp.sum(n, axis=1, keepdims=True), 0.0)
            dacs_t = dacs_t + jnp.where(row == h, jnp.sum(n, axis=0, keepdims=True), 0.0)
        dcbb = dcb.astype(BF16)
        dcm_ref[0] = dcm + _dot(dcbb, bmb, NN)
        dbm_ref[0] = dbm + _dot(dcbb, cmb, TN)
        dxdt = dxdt + dxs_ref[0]
        dxs_ref[0] = dy * dsk_ref[...] + dxdt * cm["dt_x"]

        dacs = dacs - dacs_t.T + _sel_dot(dacs_x, cm["reduce"])
        ddt = _sel_dot(dxdt * x, cm["reduce"])
        triu = (row <= lane).astype(BF16)
        rc = _sel_dot(dacs, triu, left=True)
        ddt = ddt + cm["a"] * rc
        dalog_ref[0] += jnp.sum(cm["dt"] * rc, axis=0, keepdims=True) * cm["a"]
        ddtr = jnp.where(cm["head_lane"], ddt * _sigmoid(cm["pre"]), 0.0)
        ddtr_ref[0, 0] = ddtr
        ddtb_ref[0] += jnp.sum(ddtr, axis=0, keepdims=True)

    def rev(ci):
        return nc - 1 - ci

    small = pl.BlockSpec((1, 1, LANES), lambda gi, bi, ci: (gi, 0, 0))
    xblk = pl.BlockSpec((1, CHUNK, gw), lambda gi, bi, ci: (bi, rev(ci), gi))
    nblk = pl.BlockSpec((1, CHUNK, LANES), lambda gi, bi, ci: (bi, rev(ci), gi))
    gvec = pl.BlockSpec((1, gw), lambda gi, bi, ci: (0, gi))
    gacc = pl.BlockSpec((1, 1, gw), lambda gi, bi, ci: (gi, 0, 0))
    return pl.pallas_call(
        kern, name="ssd_bwd",
        out_shape=(jax.ShapeDtypeStruct((b, s, SSD_WIDTH), F32),
                   jax.ShapeDtypeStruct((b, s, g4 * SSD_STATE), F32),
                   jax.ShapeDtypeStruct((b, s, g4 * SSD_STATE), F32),
                   jax.ShapeDtypeStruct((b, s, SSD_WIDTH), BF16),
                   jax.ShapeDtypeStruct((b, g4, s, LANES), F32),
                   jax.ShapeDtypeStruct((g4, 1, gw), F32),
                   jax.ShapeDtypeStruct((g4, 1, LANES), F32),
                   jax.ShapeDtypeStruct((g4, 1, LANES), F32),
                   jax.ShapeDtypeStruct((g4, 1, LANES), F32)),
        grid=(g4, b, nc),
        in_specs=[xblk, xblk, xblk,
                  pl.BlockSpec((1, CHUNK, LANES), lambda gi, bi, ci: (bi, rev(ci), SSD_WIDTH // LANES + gi)),
                  pl.BlockSpec((1, CHUNK, LANES), lambda gi, bi, ci: (bi, rev(ci), SSD_WIDTH // LANES + g4 + gi)),
                  pl.BlockSpec((1, CHUNK, gw), lambda gi, bi, ci: (bi, rev(ci), ZS0 // gw + gi)),
                  pl.BlockSpec((1, 1, 1, SSD_STATE, gw), lambda gi, bi, ci: (bi, rev(ci), gi, 0, 0)),
                  pl.BlockSpec((1, 1, CHUNK, LANES), lambda gi, bi, ci: (bi, gi, rev(ci), 0)),
                  small, small, gvec, gvec],
        out_specs=(xblk, nblk, nblk, xblk,
                   pl.BlockSpec((1, 1, CHUNK, LANES), lambda gi, bi, ci: (bi, gi, rev(ci), 0)),
                   gacc, small, small, small),
        scratch_shapes=[pltpu.VMEM((SSD_STATE, gw), F32)],
        compiler_params=_cparams("parallel", "arbitrary", "arbitrary"),
    )(dyn3, y3, xact, xact, xact, proj3, hst, dtr_g, dtb_g, alog_g, dskip_x, snw)


def _adamw(w, g, m, v, name):
    r, c = w.shape
    tr = 128 if r % 128 == 0 else r

    def kern(w_ref, g_ref, m_ref, v_ref, d_ref, nm_ref, nv_ref):
        gv = g_ref[...]
        nm = ADAM_B1 * m_ref[...] + (1.0 - ADAM_B1) * gv
        nv = ADAM_B2 * v_ref[...] + (1.0 - ADAM_B2) * (gv * gv)
        m_hat = nm / (1.0 - ADAM_B1 ** ADAM_STEP)
        v_hat = nv / (1.0 - ADAM_B2 ** ADAM_STEP)
        d_ref[...] = -ADAM_LR * (m_hat / (jnp.sqrt(v_hat) + ADAM_EPS) + ADAM_WD * w_ref[...])
        nm_ref[...] = nm
        nv_ref[...] = nv

    blk = pl.BlockSpec((tr, c), lambda i: (i, 0))
    out = jax.ShapeDtypeStruct((r, c), F32)
    return pl.pallas_call(
        kern, name=name, out_shape=(out, out, out), grid=(r // tr,),
        in_specs=[blk] * 4, out_specs=(blk, blk, blk),
        compiler_params=_cparams("parallel"),
    )(w, g, m, v)


ANY = pl.BlockSpec(memory_space=pl.ANY)


def _position():
    return lax.axis_index("x"), lax.axis_index("y"), lax.axis_index("c")


def _other_chips(x, y):
    return [(1 - x, y), (x, 1 - y), (1 - x, 1 - y)]


def _dma_sems(n):
    return [pltpu.SemaphoreType.DMA((n,)), pltpu.SemaphoreType.DMA((n,))]


def _gather_weights(shards):
    n = len(shards)

    def body(*refs):
        p_refs, out_refs = refs[:n], refs[n:2 * n]
        send_sems, recv_sems = refs[2 * n:]
        x, y, c = _position()
        me = 2 * x + y
        chips = _other_chips(x, y)

        def slab(a, chip, hf):
            half = shards[a].shape[0] // 2
            return out_refs[a].at[chip, pl.ds(hf * half, half), :]

        def my_half(a):
            half = shards[a].shape[0] // 2
            return p_refs[a].at[pl.ds(c * half, half), :]

        def over_ici(a, j, chip_from):
            px, py = chips[j]
            return pltpu.make_async_remote_copy(
                src_ref=my_half(a), dst_ref=slab(a, chip_from, c),
                send_sem=send_sems.at[3 * a + j], recv_sem=recv_sems.at[3 * a + j],
                device_id=(px, py, c), device_id_type=MESH)

        def to_sibling(a, j, hf):
            px, py = chips[j]
            return pltpu.make_async_remote_copy(
                src_ref=slab(a, 2 * px + py, hf), dst_ref=slab(a, 2 * px + py, hf),
                send_sem=send_sems.at[3 * (n + a) + j], recv_sem=recv_sems.at[3 * (n + a) + j],
                device_id=(x, y, 1 - c), device_id_type=MESH)

        first = [over_ici(a, j, me) for a in range(n) for j in range(3)]
        for cp in first:
            cp.start()
        passed = []
        for a in range(n):
            for j, (px, py) in enumerate(chips):
                over_ici(a, j, 2 * px + py).wait_recv()
                passed.append(to_sibling(a, j, c))
                passed[-1].start()
        for a in range(n):
            for j in range(3):
                to_sibling(a, j, 1 - c).wait_recv()
        for cp in first + passed:
            cp.wait_send()

    return pl.pallas_call(
        body, name="gather_weights",
        out_shape=[jax.ShapeDtypeStruct((N_CHIPS, *v.shape), v.dtype) for v in shards],
        in_specs=[ANY] * n, out_specs=[ANY] * n,
        scratch_shapes=_dma_sems(6 * n),
    )(*shards)


def _swap_halves(parts):
    n = len(parts)

    def body(*refs):
        v_refs, out_refs = refs[:n], refs[n:2 * n]
        send_sems, recv_sems = refs[2 * n:]
        x, y, c = _position()
        copies = []
        for a in range(n):
            half = parts[a].shape[1] // 2
            copies.append(pltpu.make_async_remote_copy(
                src_ref=v_refs[a].at[:, pl.ds((1 - c) * half, half), :], dst_ref=out_refs[a],
                send_sem=send_sems.at[a], recv_sem=recv_sems.at[a], device_id=(x, y, 1 - c), device_id_type=MESH))
        for cp in copies:
            cp.start()
        for cp in copies:
            cp.wait()

    return pl.pallas_call(
        body, name="grad_swap_halves",
        out_shape=[jax.ShapeDtypeStruct((v.shape[0], v.shape[1] // 2, v.shape[2]), v.dtype) for v in parts],
        in_specs=[ANY] * n, out_specs=[ANY] * n,
        scratch_shapes=_dma_sems(n),
    )(*parts)


def _chip_all_to_all(parts):
    n = len(parts)

    def body(*refs):
        p_refs, out_refs = refs[:n], refs[n:2 * n]
        send_sems, recv_sems = refs[2 * n:]
        x, y, c = _position()
        chips = _other_chips(x, y)
        sends = [pltpu.make_async_remote_copy(
            src_ref=p_refs[a].at[2 * px + py], dst_ref=out_refs[a].at[j],
            send_sem=send_sems.at[3 * a + j], recv_sem=recv_sems.at[3 * a + j],
            device_id=(px, py, c), device_id_type=MESH) for a in range(n) for j, (px, py) in enumerate(chips)]
        for cp in sends:
            cp.start()
        for cp in sends:
            cp.wait()

    return pl.pallas_call(
        body, name="grad_all_to_all",
        out_shape=[jax.ShapeDtypeStruct((N_CHIPS - 1, *v.shape[1:]), v.dtype) for v in parts],
        in_specs=[ANY] * n, out_specs=[ANY] * n,
        scratch_shapes=_dma_sems(3 * n),
    )(*parts)


def _join_halves(wholes):
    n = len(wholes)

    def body(*refs):
        out_refs = refs[n:2 * n]
        send_sems, recv_sems = refs[2 * n:]
        x, y, c = _position()
        copies = []
        for a in range(n):
            half = wholes[a].shape[0] // 2
            rows = out_refs[a].at[pl.ds(c * half, half), :]
            copies.append(pltpu.make_async_remote_copy(
                src_ref=rows, dst_ref=rows, send_sem=send_sems.at[a], recv_sem=recv_sems.at[a],
                device_id=(x, y, 1 - c), device_id_type=MESH))
        for cp in copies:
            cp.start()
        for cp in copies:
            cp.wait()

    return pl.pallas_call(
        body, name="grad_join_halves",
        out_shape=[jax.ShapeDtypeStruct(v.shape, v.dtype) for v in wholes],
        in_specs=[ANY] * n, out_specs=[ANY] * n,
        input_output_aliases={a: a for a in range(n)},
        scratch_shapes=_dma_sems(n),
    )(*wholes)


ADD_ROWS = 128


def _add_halves(g, sw, place, name):
    n, rows, cols = g.shape
    half = rows // 2
    nb = half // ADD_ROWS

    def kern(p_ref, g_ref, s_ref, o_ref):
        o_ref[...] = (g_ref[...] + s_ref[...]).astype(BF16)

    blk = pl.BlockSpec((1, ADD_ROWS, cols), lambda j, i, p_ref: (j, i, 0))
    return pl.pallas_call(
        kern, name=name,
        out_shape=jax.ShapeDtypeStruct((n, half, cols), BF16),
        grid_spec=pltpu.PrefetchScalarGridSpec(
            num_scalar_prefetch=1, grid=(n, nb),
            in_specs=[pl.BlockSpec((1, ADD_ROWS, cols), lambda j, i, p_ref: (j, p_ref[0] * nb + i, 0)), blk],
            out_specs=blk),
        compiler_params=_cparams("parallel", "parallel"),
    )(place, g, sw)


def _sum_chips(own, rx, place, name):
    _, half, cols = rx.shape
    nb = half // ADD_ROWS

    def kern(p_ref, own_ref, r_ref, o_ref):
        total = own_ref[0].astype(F32)
        for j in range(N_CHIPS - 1):
            total = total + r_ref[j].astype(F32)
        o_ref[...] = total

    return pl.pallas_call(
        kern, name=name,
        out_shape=jax.ShapeDtypeStruct((2 * half, cols), F32),
        grid_spec=pltpu.PrefetchScalarGridSpec(
            num_scalar_prefetch=1, grid=(nb,),
            in_specs=[pl.BlockSpec((1, ADD_ROWS, cols), lambda i, p_ref: (p_ref[1], i, 0)),
                      pl.BlockSpec((N_CHIPS - 1, ADD_ROWS, cols), lambda i, p_ref: (0, i, 0))],
            out_specs=pl.BlockSpec((ADD_ROWS, cols), lambda i, p_ref: (p_ref[0] * nb + i, 0))),
        compiler_params=_cparams("parallel"),
    )(place, own, rx)


def _gather_small(v, reduce, name):
    rows = v.shape[0]

    def body(v_ref, out_ref, buf, send_sems, recv_sems):
        x, y, c = _position()
        me = 4 * x + 2 * y + c
        buf[me] = v_ref[...]
        peers = [(x ^ (k >> 2), y ^ ((k >> 1) & 1), c ^ (k & 1)) for k in range(1, 8)]
        copies = [pltpu.make_async_remote_copy(
            src_ref=v_ref, dst_ref=buf.at[me],
            send_sem=send_sems.at[k], recv_sem=recv_sems.at[k],
            device_id=peer, device_id_type=MESH) for k, peer in enumerate(peers)]
        for cp in copies:
            cp.start()
        for k, (px, py, pc) in enumerate(peers):
            pltpu.make_async_remote_copy(
                src_ref=v_ref, dst_ref=buf.at[4 * px + 2 * py + pc],
                send_sem=send_sems.at[k], recv_sem=recv_sems.at[k],
                device_id=(px, py, pc), device_id_type=MESH).wait_recv()
        for cp in copies:
            cp.wait_send()
        if reduce:
            total = buf[0]
            for d in range(1, 8):
                total = total + buf[d]
            out_ref[...] = total
        else:
            out_ref[...] = buf[...]

    vm = pl.BlockSpec(memory_space=pltpu.VMEM)
    return pl.pallas_call(
        body, name=name,
        out_shape=jax.ShapeDtypeStruct((rows, LANES) if reduce else (8, rows, LANES), F32),
        in_specs=[vm], out_specs=vm,
        scratch_shapes=[pltpu.VMEM((8, rows, LANES), F32), pltpu.SemaphoreType.DMA((7,)), pltpu.SemaphoreType.DMA((7,))],
    )(v)


def _pad_rows(a, rows):
    return jnp.pad(a, ((0, rows - a.shape[0]), (0, 0)))


def _lane_pad(v):
    n = v.shape[1]
    return jnp.pad(v, ((0, 0), (0, -n % LANES)))


def _gather_all(w_in, w_attn_out, w_ssm_out, w_o, conv_w):
    d = D_MODEL
    own = [a[0].astype(BF16) for a in (w_in, w_attn_out, w_ssm_out, w_o)]
    gathered = _gather_weights(own)
    chip = 2 * lax.axis_index("x") + lax.axis_index("y")
    w_in_all, w_ao, w_so, w_oo = [[jnp.where(chip == q, o, g[q]) for q in range(N_CHIPS)] for o, g in zip(own, gathered)]
    last = w_in_all[N_CHIPS - 1]
    w_proj = jnp.concatenate([w_in_all[0], w_in_all[1], w_in_all[2], last[:, :LAST_DT0], last[:, LAST_DT0 + 32:],
                              last[:, LAST_DT0:LAST_DT0 + 32], jnp.zeros((d, DT_PAD - 32), BF16)], axis=1)
    w_ao = jnp.concatenate(w_ao, axis=0)
    w_so = jnp.concatenate(w_so, axis=0)
    w_oo = jnp.concatenate(w_oo, axis=0)
    conv_rows = conv_w[0].size // LANES
    conv_all = _gather_small(conv_w[0].reshape(conv_rows, LANES), False, "gather_conv_w")
    conv_w_all = conv_all[0::2].reshape(N_CHIPS, CONV_K, CONV_DIM // N_CHIPS).transpose(1, 0, 2).reshape(CONV_K, CONV_DIM)

    return w_proj, w_ao, w_so, w_oo, conv_w_all


def _local_step(x, loss_target, norm_w, w_proj, conv_w_all, conv_b, dt_bias, a_log, d_skip, ssm_norm_w,
                w_ao, w_so, w_oo, final_norm_w):
    b, s, d = x.shape
    t = b * s
    g4, hg = SSD_GROUPS, HEADS_PER_GROUP
    dtb_g = _lane_pad(dt_bias.reshape(g4, hg)).reshape(g4, 1, LANES)
    alog_g = _lane_pad(a_log.reshape(g4, hg)).reshape(g4, 1, LANES)
    dskip_x = jnp.repeat(d_skip, HEAD_DIM, axis=1)
    fnw = final_norm_w.reshape(1, d)

    x2 = x.reshape(t, d)
    h = _rms_fwd(x2, norm_w)
    proj = _matmul(h, w_proj, tm=512, tn=1280, tk=1024, name="proj")
    proj3 = proj.reshape(b, s, NP)
    o3, yp3 = _attn_fwd(proj3)
    xact = _conv_fwd(proj3, conv_w_all, conv_b)
    dtr = proj3[:, :, DT0:DT0 + g4 * hg].reshape(b, s, g4, hg).transpose(0, 2, 1, 3)
    dtr_g = jnp.pad(dtr, ((0, 0), (0, 0), (0, 0), (0, LANES - hg)))
    y3, yn3, hst = _ssd_fwd(xact, proj3, dtr_g, dtb_g, alog_g, dskip_x, ssm_norm_w)
    yp = yp3.reshape(t, D_MODEL)
    yn = yn3.reshape(t, SSD_WIDTH)
    ya = _matmul(yp, w_ao, tm=512, tn=1024, tk=1024, name="attn_out")
    ys = _matmul(yn, w_so, tm=512, tn=1024, tk=2048, name="ssm_out")
    merged = _merge_fwd(proj, ya, ys)
    mo = _matmul(merged, w_oo, tm=512, tn=1024, tk=1024, name="out_proj")
    dout, doutb, loss_part, d_fnw = _final_fwd_bwd(x2, mo, loss_target.reshape(t, d), fnw)

    dmerged = _matmul(doutb, w_oo, tb=True, tm=512, tn=1024, tk=1024, name="d_merged")
    g_wo = _matmul(merged, doutb, ta=True, tm=512, tn=1024, tk=1024, name="g_w_o")
    dya, dys, dgate = _merge_bwd(dmerged, proj, ya, ys)
    dyp = _matmul(dya, w_ao, tb=True, tm=512, tn=1024, tk=1024, name="d_attn_pre")
    g_wao = _matmul(yp, dya, ta=True, tm=512, tn=1024, tk=1024, name="g_w_attn_out")
    dyn = _matmul(dys, w_so, tb=True, tm=512, tn=2048, tk=1024, name="d_ssm_norm")
    g_wso = _matmul(yn, dys, ta=True, tm=512, tn=1024, tk=1024, name="g_w_ssm_out")
    dq, dk, dv, dza = _attn_bwd(proj3, dyp.reshape(b, s, D_MODEL), o3)
    (dxs, dbm, dcm, dzs, ddtr_g, d_snw_g, d_alog_g, d_dtb_g, d_dsk_g) = _ssd_bwd(
        dyn.reshape(b, s, SSD_WIDTH), y3, xact, proj3, hst, dtr_g, dtb_g, alog_g, dskip_x, ssm_norm_w)
    dx_xs, g_cw_xs, g_cb_xs = _conv_bwd(dxs, proj3, conv_w_all, conv_b, 0, "conv_bwd_x")
    dx_bm, g_cw_bm, g_cb_bm = _conv_bwd(dbm, proj3, conv_w_all, conv_b, SSD_WIDTH, "conv_bwd_b")
    dx_cm, g_cw_cm, g_cb_cm = _conv_bwd(dcm, proj3, conv_w_all, conv_b, SSD_WIDTH + g4 * SSD_STATE, "conv_bwd_c")
    ddt = ddtr_g[:, :, :, :hg].transpose(0, 2, 1, 3).reshape(b, s, g4 * hg).astype(BF16)
    dproj = jnp.concatenate([dq, dk, dv, dza, dzs, dx_xs, dx_bm, dx_cm, dgate.reshape(b, s, 2 * D_MODEL),
                             jnp.pad(ddt, ((0, 0), (0, 0), (0, DT_PAD - g4 * hg)))], axis=2).reshape(t, NP)
    g_wproj = _matmul(h, dproj, ta=True, tm=512, tn=1280, tk=1024, name="g_w_in")
    dh = _matmul(dproj, w_proj, tb=True, tm=512, tn=1024, tk=1280, name="d_h")
    grad_x, d_nw = _rms_bwd(dh, x2, norm_w, dout)
    g_cw = jnp.concatenate([g_cw_xs, g_cw_bm, g_cw_cm], axis=1)
    g_cb = jnp.concatenate([g_cb_xs, g_cb_bm, g_cb_cm], axis=1)
    return (loss_part, grad_x, d_nw, g_wproj, g_cw, g_cb, d_dtb_g, d_alog_g, d_dsk_g, d_snw_g, g_wao, g_wso, g_wo, d_fnw)


def kernel(x, norm_w, w_in, conv_w, conv_b, dt_bias, a_log, d_skip, ssm_norm_w, w_attn_out, w_ssm_out, w_o, final_norm_w, loss_target, m_norm_w, m_w_in, m_conv_w, m_conv_b, m_dt_bias, m_a_log, m_d_skip, m_ssm_norm_w, m_w_attn_out, m_w_ssm_out, m_w_o, m_final_norm_w, v_norm_w, v_w_in, v_conv_w, v_conv_b, v_dt_bias, v_a_log, v_d_skip, v_ssm_norm_w, v_w_attn_out, v_w_ssm_out, v_w_o, v_final_norm_w):
    b, s, d = x.shape
    core = lax.axis_index("c")
    g4, hg = SSD_GROUPS, HEADS_PER_GROUP
    shard_cols = w_in.shape[2]
    w_proj, w_ao, w_so, w_oo, conv_w_all = _gather_all(w_in, w_attn_out, w_ssm_out, w_o, conv_w)
    (loss_part, grad_x, d_nw, g_wproj, g_cw, g_cb, d_dtb_g, d_alog_g, d_dsk_g, d_snw_g, g_wao, g_wso, g_wo, d_fnw) = _local_step(
        x, loss_target, norm_w, w_proj, conv_w_all, conv_b, dt_bias, a_log, d_skip, ssm_norm_w, w_ao, w_so, w_oo, final_norm_w)

    last0 = (N_CHIPS - 1) * shard_cols
    g_last = jnp.concatenate([g_wproj[:, last0:GATE0], g_wproj[:, DT0:DT0 + 32], g_wproj[:, GATE0:DT0]], axis=1)
    g_win_chips = jnp.stack([g_wproj[:, j * shard_cols:(j + 1) * shard_cols] for j in range(N_CHIPS - 1)] + [g_last])
    g_out_chips = jnp.concatenate([g.reshape(N_CHIPS, -1, d) for g in (g_wao, g_wso, g_wo)], axis=1)
    parts = [g_win_chips, g_out_chips]
    chip = 2 * lax.axis_index("x") + lax.axis_index("y")
    place = jnp.stack([core, chip]).astype(jnp.int32)
    from_sibling = _swap_halves(parts)
    chip_sums = [_add_halves(p, f, place, "grad_add_halves_%d" % i) for i, (p, f) in enumerate(zip(parts, from_sibling))]
    from_chips = _chip_all_to_all(chip_sums)
    wholes = [_sum_chips(o, r, place, "grad_sum_chips_%d" % i) for i, (o, r) in enumerate(zip(chip_sums, from_chips))]
    g_w_in, g_out = _join_halves(wholes)

    small = jnp.concatenate([
        loss_part, d_nw, g_cb, _lane_pad(d_dtb_g[:, 0, :hg].reshape(1, -1)), _lane_pad(d_alog_g[:, 0, :hg].reshape(1, -1)),
        _lane_pad(d_dsk_g[:, 0, :hg].reshape(1, -1)),
        d_snw_g.reshape(1, -1), d_fnw, g_cw.reshape(1, -1)], axis=1)
    small_rows = small.shape[1] // LANES
    reduced = _gather_small(_pad_rows(small.reshape(small_rows, LANES), -(-small_rows // 8) * 8), True, "reduce_small")
    flat = reduced.reshape(-1)

    def take(start, n):
        return flat[start:start + n].reshape(1, n)

    loss = flat[0]
    pos = LANES
    g_norm_w = take(pos, d); pos += d
    g_conv_b = take(pos, CONV_DIM); pos += CONV_DIM
    g_dt_bias = take(pos, g4 * hg); pos += LANES
    g_a_log = take(pos, g4 * hg); pos += LANES
    g_d_skip = take(pos, g4 * hg); pos += LANES
    g_ssm_norm_w = take(pos, SSD_WIDTH); pos += SSD_WIDTH
    g_final_norm_w = take(pos, d); pos += d
    conv_cols = CONV_DIM // N_CHIPS
    g_conv_w = lax.dynamic_slice_in_dim(flat[pos:pos + CONV_K * CONV_DIM].reshape(CONV_K, CONV_DIM), chip * conv_cols, conv_cols, axis=1)

    rows_ao, rows_so = D_MODEL // N_CHIPS, SSD_WIDTH // N_CHIPS
    g_w_attn_out = g_out[:rows_ao]
    g_w_ssm_out = g_out[rows_ao:rows_ao + rows_so]
    g_w_o = g_out[rows_ao + rows_so:]

    names = ["norm_w", "w_in", "conv_w", "conv_b", "dt_bias", "a_log", "d_skip", "ssm_norm_w",
             "w_attn_out", "w_ssm_out", "w_o", "final_norm_w"]
    weights = [norm_w, w_in, conv_w, conv_b, dt_bias, a_log, d_skip, ssm_norm_w, w_attn_out, w_ssm_out, w_o, final_norm_w]
    grads = [g_norm_w, g_w_in, g_conv_w, g_conv_b, g_dt_bias, g_a_log, g_d_skip, g_ssm_norm_w,
             g_w_attn_out, g_w_ssm_out, g_w_o, g_final_norm_w]
    ms = [m_norm_w, m_w_in, m_conv_w, m_conv_b, m_dt_bias, m_a_log, m_d_skip, m_ssm_norm_w,
          m_w_attn_out, m_w_ssm_out, m_w_o, m_final_norm_w]
    vs = [v_norm_w, v_w_in, v_conv_w, v_conv_b, v_dt_bias, v_a_log, v_d_skip, v_ssm_norm_w,
          v_w_attn_out, v_w_ssm_out, v_w_o, v_final_norm_w]
    out_g, out_d, out_m, out_v = [], [], [], []
    for name, w, g, m, v in zip(names, weights, grads, ms, vs):
        shape2 = g.shape
        dlt, nm, nv = _adamw(w.reshape(shape2), g, m.reshape(shape2), v.reshape(shape2), "adamw_" + name)
        out_g.append(g.reshape(w.shape))
        out_d.append(dlt.reshape(w.shape))
        out_m.append(nm.reshape(w.shape))
        out_v.append(nv.reshape(w.shape))

    return (loss, grad_x.reshape(b, s, d), *out_g, *out_d, *out_m, *out_v)
```

```python
import jax
import jax.numpy as jnp
from jax import lax
from jax.experimental import pallas as pl
from jax.experimental.pallas import tpu as pltpu

F32 = jnp.float32
BF16 = jnp.bfloat16
MESH = pl.DeviceIdType.MESH

D_MODEL = 1024
SB_HEADS = 16
HEAD_DIM = 64
SSD_WIDTH = 2048
SSD_GROUPS = 4
GROUP_WIDTH = SSD_WIDTH // SSD_GROUPS
HEADS_PER_GROUP = 8
SSD_STATE = 128
CHUNK = 128
CONV_K = 4
CONV_DIM = 3072
D_PROJ = 11296
EPS = 1e-6
ADAM_LR, ADAM_B1, ADAM_B2, ADAM_EPS, ADAM_WD, ADAM_STEP = 0.001, 0.9, 0.999, 1e-08, 0.01, 10

LANES = 128
Q0, K0, V0, ZA0, ZS0, XBC0, GATE0, DT0 = 0, 1024, 2048, 3072, 4096, 6144, 9216, 11264
DT_PAD = 256
NP = DT0 + DT_PAD
N_CHIPS = 4
LAST_DT0 = GATE0 - (N_CHIPS - 1) * (D_PROJ // N_CHIPS)
VMEM_LIMIT = 56 * 1024 * 1024


def _cparams(*sem):
    return pltpu.CompilerParams(dimension_semantics=sem or None, vmem_limit_bytes=VMEM_LIMIT)


def _sigmoid(z):
    return 1.0 / (1.0 + jnp.exp(-z))


def _dot(a, b, dims, precision=None):
    return lax.dot_general(a, b, (dims, ((), ())), preferred_element_type=F32, precision=precision)


NN = ((1,), (0,))
NT = ((1,), (1,))
TN = ((0,), (0,))


def _matmul(a, b, *, ta=False, tb=False, out_dtype=F32, tm, tn, tk, name):
    m, k = (a.shape[1], a.shape[0]) if ta else a.shape
    n = b.shape[0] if tb else b.shape[1]
    assert m % tm == 0 and n % tn == 0 and k % tk == 0, (name, m, n, k)
    nk = k // tk
    use_scratch = out_dtype != F32
    dims = ((0,) if ta else (1,), (1,) if tb else (0,))

    def kern(a_ref, b_ref, o_ref, *scratch):
        acc = scratch[0] if use_scratch else o_ref
        kk = pl.program_id(2)

        @pl.when(kk == 0)
        def _():
            acc[...] = jnp.zeros_like(acc)

        acc[...] += _dot(a_ref[...], b_ref[...], dims)
        if use_scratch:
            @pl.when(kk == nk - 1)
            def _():
                o_ref[...] = acc[...].astype(out_dtype)

    a_spec = pl.BlockSpec((tk, tm), lambda i, j, q: (q, i)) if ta else pl.BlockSpec((tm, tk), lambda i, j, q: (i, q))
    b_spec = pl.BlockSpec((tn, tk), lambda i, j, q: (j, q)) if tb else pl.BlockSpec((tk, tn), lambda i, j, q: (q, j))
    return pl.pallas_call(
        kern, name=name,
        out_shape=jax.ShapeDtypeStruct((m, n), out_dtype),
        grid=(m // tm, n // tn, nk),
        in_specs=[a_spec, b_spec],
        out_specs=pl.BlockSpec((tm, tn), lambda i, j, q: (i, j)),
        scratch_shapes=[pltpu.VMEM((tm, tn), F32)] if use_scratch else [],
        compiler_params=_cparams("parallel", "parallel", "arbitrary"),
    )(a, b)


ROWS = 256


def _rms_fwd(x2, w):
    t, d = x2.shape

    def kern(x_ref, w_ref, h_ref):
        x = x_ref[...]
        r = lax.rsqrt(jnp.mean(x * x, axis=-1, keepdims=True) + EPS)
        h_ref[...] = (x * r * w_ref[...]).astype(BF16)

    return pl.pallas_call(
        kern, name="rms_fwd",
        out_shape=jax.ShapeDtypeStruct((t, d), BF16),
        grid=(t // ROWS,),
        in_specs=[pl.BlockSpec((ROWS, d), lambda i: (i, 0)), pl.BlockSpec((1, d), lambda i: (0, 0))],
        out_specs=pl.BlockSpec((ROWS, d), lambda i: (i, 0)),
        compiler_params=_cparams("parallel"),
    )(x2, w)


def _rms_bwd(dh, x2, w, dout):
    t, d = x2.shape

    def kern(dh_ref, x_ref, w_ref, dout_ref, gx_ref, dw_ref):
        @pl.when(pl.program_id(0) == 0)
        def _():
            dw_ref[...] = jnp.zeros_like(dw_ref)

        x = x_ref[...]
        r = lax.rsqrt(jnp.mean(x * x, axis=-1, keepdims=True) + EPS)
        xh = x * r
        g = dh_ref[...]
        dw_ref[...] += jnp.sum(g * xh, axis=0, keepdims=True)
        gw = g * w_ref[...]
        gx_ref[...] = dout_ref[...] + r * (gw - xh * jnp.mean(gw * xh, axis=-1, keepdims=True))

    row = pl.BlockSpec((ROWS, d), lambda i: (i, 0))
    vec = pl.BlockSpec((1, d), lambda i: (0, 0))
    return pl.pallas_call(
        kern, name="rms_bwd",
        out_shape=(jax.ShapeDtypeStruct((t, d), F32), jax.ShapeDtypeStruct((1, d), F32)),
        grid=(t // ROWS,),
        in_specs=[row, row, vec, row],
        out_specs=(row, vec),
        compiler_params=_cparams("arbitrary"),
    )(dh, x2, w, dout)


def _final_fwd_bwd(x2, mo, target, w):
    t, d = x2.shape

    def kern(x_ref, mo_ref, t_ref, w_ref, dout_ref, doutb_ref, loss_ref, dw_ref):
        @pl.when(pl.program_id(0) == 0)
        def _():
            loss_ref[...] = jnp.zeros_like(loss_ref)
            dw_ref[...] = jnp.zeros_like(dw_ref)

        u = x_ref[...] + mo_ref[...]
        r = lax.rsqrt(jnp.mean(u * u, axis=-1, keepdims=True) + EPS)
        uh = u * r
        wv = w_ref[...]
        err = uh * wv - t_ref[...]
        loss_ref[...] += (0.5 / d) * jnp.sum(err * err)
        dy = err * (1.0 / d)
        dw_ref[...] += jnp.sum(dy * uh, axis=0, keepdims=True)
        gw = dy * wv
        du = r * (gw - uh * jnp.mean(gw * uh, axis=-1, keepdims=True))
        dout_ref[...] = du
        doutb_ref[...] = du.astype(BF16)

    row = pl.BlockSpec((ROWS, d), lambda i: (i, 0))
    vec = pl.BlockSpec((1, d), lambda i: (0, 0))
    return pl.pallas_call(
        kern, name="final_fwd_bwd",
        out_shape=(jax.ShapeDtypeStruct((t, d), F32), jax.ShapeDtypeStruct((t, d), BF16),
                   jax.ShapeDtypeStruct((1, LANES), F32), jax.ShapeDtypeStruct((1, d), F32)),
        grid=(t // ROWS,),
        in_specs=[row, row, row, vec],
        out_specs=(row, row, pl.BlockSpec((1, LANES), lambda i: (0, 0)), vec),
        compiler_params=_cparams("arbitrary"),
    )(x2, mo, target, w)


def _merge_fwd(proj2, ya, ys):
    t = ya.shape[0]
    gblk = GATE0 // D_MODEL

    def kern(ga_ref, gs_ref, ya_ref, ys_ref, o_ref):
        o_ref[...] = (_sigmoid(ga_ref[...]) * ya_ref[...] + _sigmoid(gs_ref[...]) * ys_ref[...]).astype(BF16)

    row = pl.BlockSpec((ROWS, D_MODEL), lambda i: (i, 0))
    return pl.pallas_call(
        kern, name="merge_fwd",
        out_shape=jax.ShapeDtypeStruct((t, D_MODEL), BF16),
        grid=(t // ROWS,),
        in_specs=[pl.BlockSpec((ROWS, D_MODEL), lambda i: (i, gblk)),
                  pl.BlockSpec((ROWS, D_MODEL), lambda i: (i, gblk + 1)), row, row],
        out_specs=row,
        compiler_params=_cparams("parallel"),
    )(proj2, proj2, ya, ys)


def _merge_bwd(dm, proj2, ya, ys):
    t = ya.shape[0]
    gblk = GATE0 // D_MODEL

    def kern(dm_ref, ga_ref, gs_ref, ya_ref, ys_ref, dya_ref, dys_ref, dg_ref):
        g = dm_ref[...]
        sa = _sigmoid(ga_ref[...])
        ss = _sigmoid(gs_ref[...])
        dya_ref[...] = (g * sa).astype(BF16)
        dys_ref[...] = (g * ss).astype(BF16)
        dg_ref[:, :D_MODEL] = (g * ya_ref[...] * sa * (1.0 - sa)).astype(BF16)
        dg_ref[:, D_MODEL:] = (g * ys_ref[...] * ss * (1.0 - ss)).astype(BF16)

    row = pl.BlockSpec((ROWS, D_MODEL), lambda i: (i, 0))
    return pl.pallas_call(
        kern, name="merge_bwd",
        out_shape=(jax.ShapeDtypeStruct((t, D_MODEL), BF16), jax.ShapeDtypeStruct((t, D_MODEL), BF16),
                   jax.ShapeDtypeStruct((t, 2 * D_MODEL), BF16)),
        grid=(t // ROWS,),
        in_specs=[row, pl.BlockSpec((ROWS, D_MODEL), lambda i: (i, gblk)),
                  pl.BlockSpec((ROWS, D_MODEL), lambda i: (i, gblk + 1)), row, row],
        out_specs=(row, row, pl.BlockSpec((ROWS, 2 * D_MODEL), lambda i: (i, 0))),
        compiler_params=_cparams("parallel"),
    )(dm, proj2, proj2, ya, ys)


TQ = 256
TK = 256
HEAD_LANES = (slice(0, HEAD_DIM), slice(HEAD_DIM, 2 * HEAD_DIM))


def _tri(pred):
    r = lax.broadcasted_iota(jnp.int32, (TK, TK), 0)
    c = lax.broadcasted_iota(jnp.int32, (TK, TK), 1)
    return pred(r, c).astype(BF16)


def _split_bf16(v):
    hi = v.astype(BF16)
    lo = (v - hi.astype(F32)).astype(BF16)
    return hi, lo


def _tri_dot(v, tri):
    hi, lo = _split_bf16(v)
    return _dot(hi, tri, NN) + _dot(lo, tri, NN)


def _sb_logs(z, mask):
    l1p = jnp.log(1.0 + jnp.exp(-jnp.abs(z)))
    lb = jnp.minimum(z, 0.0) - l1p
    lom = -jnp.maximum(z, 0.0) - l1p
    if mask is not None:
        lom = jnp.where(mask, lom, 0.0)
    return lb, lom


def _sb_weights(lb, later, carry_r, mask):
    a = jnp.exp(lb + (later + carry_r))
    if mask is not None:
        a = jnp.where(mask, a, 0.0)
    return a


DEAD = -104.0


def _while_alive(n, carry, step):
    def alive(cr):
        return jnp.max(jnp.maximum(cr[0][0], cr[1][0])) > DEAD

    def cond(state):
        jj, go, _ = state
        return jnp.logical_and(jj < n, go)

    def body(state):
        jj, _, cr = state
        cr = step(jj, cr)
        return jj + 1, alive(cr), cr

    return lax.while_loop(cond, body, (jnp.int32(0), alive(carry), carry))[2]


def _split_heads(dst, src, scale=None):
    for h, lanes in enumerate(HEAD_LANES):
        v = src[:, lanes]
        dst[h] = (v if scale is None else v * scale).astype(BF16)


def _attn_fwd(proj3):
    b, s, _ = proj3.shape
    nq = s // TQ
    scale = HEAD_DIM ** -0.5

    def kern(q_ref, k_ref, v_ref, za_ref, o_ref, yp_ref, qs, ks, vs):
        _split_heads(qs, q_ref[0], scale)
        _split_heads(ks, k_ref[0])
        _split_heads(vs, v_ref[0])
        row = lax.broadcasted_iota(jnp.int32, (TQ, TK), 0)
        col = lax.broadcasted_iota(jnp.int32, (TQ, TK), 1)
        tri_gt = _tri(lambda j, sk: j > sk)

        def q_block(i, _):
            r0 = pl.multiple_of(i * TQ, TQ)
            n_kb = (r0 + TQ + TK - 1) // TK
            qh = [qs[h, pl.ds(r0, TQ), :] for h in range(2)]

            def k_block(c0, carry, mask):
                kh = [ks[h, pl.ds(c0, TK), :] for h in range(2)]
                vh = [vs[h, pl.ds(c0, TK), :] for h in range(2)]
                z = [_dot(qh[h], kh[h], NT) for h in range(2)]
                logs, later = [], []
                for h in range(2):
                    logs.append(_sb_logs(z[h], mask))
                    later.append(_tri_dot(logs[h][1], tri_gt))
                out = []
                for h in range(2):
                    carry_r, acc = carry[h]
                    lb, lom = logs[h]
                    a = _sb_weights(lb, later[h], carry_r, mask)
                    row_sum = later[h][:, 0:1] + lom[:, 0:1]
                    out.append((carry_r + row_sum, acc + _dot(a.astype(BF16), vh[h], NN)))
                return tuple(out)

            c_last = pl.multiple_of((n_kb - 1) * TK, TK)
            start = (jnp.zeros((TQ, 1), F32), jnp.zeros((TQ, HEAD_DIM), F32))
            carry = k_block(c_last, (start, start), col + c_last < row + r0)

            carry = _while_alive(n_kb - 1, carry, lambda jj, cr: k_block(pl.multiple_of((n_kb - 2 - jj) * TK, TK), cr, None))
            for (_, acc), lanes in zip(carry, HEAD_LANES):
                o_ref[0, pl.ds(r0, TQ), lanes] = acc
                za = za_ref[0, pl.ds(r0, TQ), lanes]
                yp_ref[0, pl.ds(r0, TQ), lanes] = (acc * (za * _sigmoid(za))).astype(BF16)
            return 0

        lax.fori_loop(0, nq, q_block, 0)

    def spec(c0):
        return pl.BlockSpec((1, s, LANES), lambda bi, hp: (bi, 0, c0 // LANES + hp))

    out_spec = pl.BlockSpec((1, s, LANES), lambda bi, hp: (bi, 0, hp))
    return pl.pallas_call(
        kern, name="attn_fwd",
        out_shape=(jax.ShapeDtypeStruct((b, s, D_MODEL), F32), jax.ShapeDtypeStruct((b, s, D_MODEL), BF16)),
        grid=(b, SB_HEADS // 2),
        in_specs=[spec(Q0), spec(K0), spec(V0), spec(ZA0)],
        out_specs=(out_spec, out_spec),
        scratch_shapes=[pltpu.VMEM((2, s, HEAD_DIM), BF16)] * 3,
        compiler_params=_cparams("parallel", "parallel"),
    )(proj3, proj3, proj3, proj3)


def _attn_bwd(proj3, dyp3, o3):
    b, s, _ = proj3.shape
    nq = s // TQ
    scale = HEAD_DIM ** -0.5

    def kern(q_ref, k_ref, v_ref, za_ref, dyp_ref, o_ref, dq_ref, dk_ref, dv_ref, dza_ref,
             qs, ks, vs, dos, dk_acc, dv_acc):
        _split_heads(qs, q_ref[0], scale)
        _split_heads(ks, k_ref[0])
        _split_heads(vs, v_ref[0])
        za = za_ref[0]
        sg = _sigmoid(za)
        dyp = dyp_ref[0]
        _split_heads(dos, dyp * (za * sg))
        dza_ref[0] = (dyp * o_ref[0] * (sg * (1.0 + za * (1.0 - sg)))).astype(BF16)
        dk_acc[...] = jnp.zeros_like(dk_acc)
        dv_acc[...] = jnp.zeros_like(dv_acc)
        row = lax.broadcasted_iota(jnp.int32, (TQ, TK), 0)
        col = lax.broadcasted_iota(jnp.int32, (TQ, TK), 1)
        tri_gt = _tri(lambda j, sk: j > sk)
        tri_ge = _tri(lambda j, sk: j >= sk)

        def q_block(i, _):
            r0 = pl.multiple_of(i * TQ, TQ)
            n_kb = (r0 + TQ + TK - 1) // TK
            qh = [qs[h, pl.ds(r0, TQ), :] for h in range(2)]
            doh = [dos[h, pl.ds(r0, TQ), :] for h in range(2)]
            totals = [jnp.sum(doh[h].astype(F32) * o_ref[0, pl.ds(r0, TQ), lanes], axis=1, keepdims=True)
                      for h, lanes in enumerate(HEAD_LANES)]

            def k_block(c0, carry, mask):
                kh = [ks[h, pl.ds(c0, TK), :] for h in range(2)]
                vh = [vs[h, pl.ds(c0, TK), :] for h in range(2)]
                z = [_dot(qh[h], kh[h], NT) for h in range(2)]
                da = [_dot(doh[h], vh[h], NT) for h in range(2)]
                logs, later = [], []
                for h in range(2):
                    logs.append(_sb_logs(z[h], mask))
                    later.append(_tri_dot(logs[h][1], tri_gt))
                ab, g, suffix = [], [], []
                for h in range(2):
                    a = _sb_weights(logs[h][0], later[h], carry[h][0], mask)
                    ab.append(a.astype(BF16))
                    g.append(da[h] * ab[h].astype(F32))
                    suffix.append(_tri_dot(g[h], tri_ge))
                out = []
                for h in range(2):
                    carry_r, carry_g, dq = carry[h]
                    lb, lom = logs[h]
                    dz = g[h] - (g[h] + (totals[h] - carry_g) - suffix[h]) * jnp.exp(lb)
                    if mask is not None:
                        dz = jnp.where(mask, dz, 0.0)
                    dzb = dz.astype(BF16)
                    dk_acc[h, pl.ds(c0, TK), :] += _dot(dzb, qh[h], TN)
                    dv_acc[h, pl.ds(c0, TK), :] += _dot(ab[h], doh[h], TN)
                    out.append((carry_r + (later[h][:, 0:1] + lom[:, 0:1]), carry_g + suffix[h][:, 0:1],
                                dq + _dot(dzb, kh[h], NN)))
                return tuple(out)

            c_last = pl.multiple_of((n_kb - 1) * TK, TK)
            zero = jnp.zeros((TQ, 1), F32)
            start = (zero, zero, jnp.zeros((TQ, HEAD_DIM), F32))
            carry = k_block(c_last, (start, start), col + c_last < row + r0)

            carry = _while_alive(n_kb - 1, carry, lambda jj, cr: k_block(pl.multiple_of((n_kb - 2 - jj) * TK, TK), cr, None))
            for (_, _, dq), lanes in zip(carry, HEAD_LANES):
                dq_ref[0, pl.ds(r0, TQ), lanes] = (dq * scale).astype(BF16)
            return 0

        lax.fori_loop(0, nq, q_block, 0)

        for h, lanes in enumerate(HEAD_LANES):
            dk_ref[0, :, lanes] = dk_acc[h].astype(BF16)
            dv_ref[0, :, lanes] = dv_acc[h].astype(BF16)

    def spec(c0):
        return pl.BlockSpec((1, s, LANES), lambda bi, hp: (bi, 0, c0 // LANES + hp))

    plain = pl.BlockSpec((1, s, LANES), lambda bi, hp: (bi, 0, hp))
    out = jax.ShapeDtypeStruct((b, s, D_MODEL), BF16)
    return pl.pallas_call(
        kern, name="attn_bwd",
        out_shape=(out, out, out, out),
        grid=(b, SB_HEADS // 2),
        in_specs=[spec(Q0), spec(K0), spec(V0), spec(ZA0), plain, plain],
        out_specs=(plain, plain, plain, plain),
        scratch_shapes=[pltpu.VMEM((2, s, HEAD_DIM), BF16)] * 4 + [pltpu.VMEM((2, s, HEAD_DIM), F32)] * 2,
        compiler_params=_cparams("parallel", "parallel"),
    )(proj3, proj3, proj3, proj3, dyp3, o3)


CONV_COLS = 256
HALO = 8


def _conv_pre(xp, w_ref, b_ref, r0):
    pre = b_ref[...] + w_ref[CONV_K - 1:CONV_K, :] * xp[pl.ds(HALO + r0, CHUNK), :]
    for kk in range(1, CONV_K):
        pre = pre + w_ref[CONV_K - 1 - kk:CONV_K - kk, :] * xp[pl.ds(HALO + r0 - kk, CHUNK), :]
    return pre


def _conv_fwd(proj3, conv_w, conv_b):
    b, s, _ = proj3.shape
    nc = s // CHUNK

    def kern(x_ref, w_ref, b_ref, o_ref, xp):
        xp[0:HALO, :] = jnp.zeros((HALO, CONV_COLS), F32)
        xp[HALO:, :] = x_ref[0]
        for ci in range(nc):
            pre = _conv_pre(xp, w_ref, b_ref, ci * CHUNK)
            o_ref[0, ci * CHUNK:(ci + 1) * CHUNK, :] = pre * _sigmoid(pre)

    return pl.pallas_call(
        kern, name="conv_fwd",
        out_shape=jax.ShapeDtypeStruct((b, s, CONV_DIM), F32),
        grid=(CONV_DIM // CONV_COLS, b),
        in_specs=[pl.BlockSpec((1, s, CONV_COLS), lambda j, bi: (bi, 0, XBC0 // CONV_COLS + j)),
                  pl.BlockSpec((CONV_K, CONV_COLS), lambda j, bi: (0, j)),
                  pl.BlockSpec((1, CONV_COLS), lambda j, bi: (0, j))],
        out_specs=pl.BlockSpec((1, s, CONV_COLS), lambda j, bi: (bi, 0, j)),
        scratch_shapes=[pltpu.VMEM((s + HALO, CONV_COLS), F32)],
        compiler_params=_cparams("parallel", "parallel"),
    )(proj3, conv_w, conv_b)


def _conv_bwd(dact, proj3, conv_w, conv_b, col0, name):
    b, s, width = dact.shape
    nc = s // CHUNK
    j0 = col0 // CONV_COLS

    def kern(da_ref, x_ref, w_ref, b_ref, dx_ref, dw_ref, db_ref, xp, dp):
        @pl.when(pl.program_id(1) == 0)
        def _():
            dw_ref[...] = jnp.zeros_like(dw_ref)
            db_ref[...] = jnp.zeros_like(db_ref)

        xp[0:HALO, :] = jnp.zeros((HALO, CONV_COLS), F32)
        xp[HALO:, :] = x_ref[0]
        dp[s:, :] = jnp.zeros((HALO, CONV_COLS), F32)
        for ci in range(nc):
            r0 = ci * CHUNK
            pre = _conv_pre(xp, w_ref, b_ref, r0)
            sg = _sigmoid(pre)
            dpre = da_ref[0, r0:r0 + CHUNK, :] * (sg * (1.0 + pre * (1.0 - sg)))
            dp[r0:r0 + CHUNK, :] = dpre
            db_ref[...] += jnp.sum(dpre, axis=0, keepdims=True)
            for kk in range(CONV_K):
                tap = CONV_K - 1 - kk
                dw_ref[tap:tap + 1, :] += jnp.sum(dpre * xp[pl.ds(HALO + r0 - kk, CHUNK), :], axis=0, keepdims=True)
        for ci in range(nc):
            r0 = ci * CHUNK
            dx = w_ref[CONV_K - 1:CONV_K, :] * dp[pl.ds(r0, CHUNK), :]
            for kk in range(1, CONV_K):
                dx = dx + w_ref[CONV_K - 1 - kk:CONV_K - kk, :] * dp[pl.ds(r0 + kk, CHUNK), :]
            dx_ref[0, r0:r0 + CHUNK, :] = dx.astype(BF16)

    return pl.pallas_call(
        kern, name=name,
        out_shape=(jax.ShapeDtypeStruct((b, s, width), BF16), jax.ShapeDtypeStruct((CONV_K, width), F32),
                   jax.ShapeDtypeStruct((1, width), F32)),
        grid=(width // CONV_COLS, b),
        in_specs=[pl.BlockSpec((1, s, CONV_COLS), lambda j, bi: (bi, 0, j)),
                  pl.BlockSpec((1, s, CONV_COLS), lambda j, bi: (bi, 0, XBC0 // CONV_COLS + j0 + j)),
                  pl.BlockSpec((CONV_K, CONV_COLS), lambda j, bi: (0, j0 + j)),
                  pl.BlockSpec((1, CONV_COLS), lambda j, bi: (0, j0 + j))],
        out_specs=(pl.BlockSpec((1, s, CONV_COLS), lambda j, bi: (bi, 0, j)),
                   pl.BlockSpec((CONV_K, CONV_COLS), lambda j, bi: (0, j)),
                   pl.BlockSpec((1, CONV_COLS), lambda j, bi: (0, j))),
        scratch_shapes=[pltpu.VMEM((s + HALO, CONV_COLS), F32)] * 2,
        compiler_params=_cparams("parallel", "arbitrary"),
    )(dact, proj3, conv_w, conv_b)


def _sel_dot(v, sel, left=False):
    hi = v.astype(BF16)
    rest = v - hi.astype(F32)
    mid = rest.astype(BF16)
    lo = (rest - mid.astype(F32)).astype(BF16)
    if left:
        return _dot(sel, hi, NN) + _dot(sel, mid, NN) + _dot(sel, lo, NN)
    return _dot(hi, sel, NN) + _dot(mid, sel, NN) + _dot(lo, sel, NN)


def _ssd_common(dtr_ref, dtb_ref, alog_ref):
    lane = lax.broadcasted_iota(jnp.int32, (CHUNK, LANES), 1)
    row = lax.broadcasted_iota(jnp.int32, (CHUNK, LANES), 0)
    head_lane = lane < HEADS_PER_GROUP
    pre = dtr_ref[0, 0] + dtb_ref[0]
    dt = jnp.where(head_lane, jnp.maximum(pre, 0.0) + jnp.log(1.0 + jnp.exp(-jnp.abs(pre))), 0.0)
    a = jnp.where(head_lane[0:1], -jnp.exp(alog_ref[0]), 0.0)
    tril = (row >= lane).astype(BF16)
    acs = _sel_dot(dt * a, tril, left=True)
    acs_t = acs.T
    er = lax.broadcasted_iota(jnp.int32, (LANES, GROUP_WIDTH), 0)
    ec = lax.broadcasted_iota(jnp.int32, (LANES, GROUP_WIDTH), 1)
    expand = ((ec // HEAD_DIM) == er).astype(BF16)
    tr = lax.broadcasted_iota(jnp.int32, (GROUP_WIDTH, LANES), 0)
    tc = lax.broadcasted_iota(jnp.int32, (GROUP_WIDTH, LANES), 1)
    reduce = ((tr // HEAD_DIM) == tc).astype(BF16)
    dt_x = _sel_dot(dt, expand)
    acs_x = _sel_dot(acs, expand)
    end_x = acs_x[CHUNK - 1:CHUNK, :]
    causal = row >= lane
    return dict(dt=dt, a=a, pre=pre, head_lane=head_lane, acs=acs, acs_t=acs_t, expand=expand, reduce=reduce,
                dt_x=dt_x, acs_x=acs_x, end_x=end_x, causal=causal, row=row, lane=lane)


def _ssd_decay(cm, h):
    seg = cm["acs"][:, h:h + 1] - cm["acs_t"][h:h + 1, :]
    return jnp.where(cm["causal"], jnp.exp(jnp.minimum(seg, 0.0)), 0.0)


def _ssd_fwd(xact, proj3, dtr_g, dtb_g, alog_g, dskip_x, snw):
    b, s, _ = xact.shape
    nc = s // CHUNK
    g4 = SSD_GROUPS

    def kern(xs_ref, bm_ref, cm_ref, zs_ref, dtr_ref, dtb_ref, alog_ref, dsk_ref, snw_ref,
             y_ref, yn_ref, hst_ref, h_sc):
        @pl.when(pl.program_id(2) == 0)
        def _():
            h_sc[...] = jnp.zeros_like(h_sc)

        cm = _ssd_common(dtr_ref, dtb_ref, alog_ref)
        x = xs_ref[0]
        bmb = bm_ref[0].astype(BF16)
        cmb = cm_ref[0].astype(BF16)
        h_in = h_sc[...]
        hst_ref[0, 0, 0] = h_in
        xdt = x * cm["dt_x"]
        xdtb = xdt.astype(BF16)
        cb = _dot(cmb, bmb, NT)
        y_off = _dot(cmb, h_in.astype(BF16), NN) * jnp.exp(cm["acs_x"])
        for h in range(HEADS_PER_GROUP):
            lanes = slice(h * HEAD_DIM, (h + 1) * HEAD_DIM)
            m = (cb * _ssd_decay(cm, h)).astype(BF16)
            y_ref[0, :, lanes] = _dot(m, xdtb[:, lanes], NN)
        y = y_ref[0] + y_off + x * dsk_ref[...]
        y_ref[0] = y
        w = (xdt * jnp.exp(cm["end_x"] - cm["acs_x"])).astype(BF16)
        h_sc[...] = h_in * jnp.exp(cm["end_x"]) + _dot(bmb, w, TN)
        zs = zs_ref[0]
        y2 = y * (zs * _sigmoid(zs))
        yn_ref[0] = (y2 * lax.rsqrt(jnp.mean(y2 * y2, axis=-1, keepdims=True) + EPS) * snw_ref[...]).astype(BF16)

    gw = GROUP_WIDTH
    small = pl.BlockSpec((1, 1, LANES), lambda gi, bi, ci: (gi, 0, 0))
    xblk = pl.BlockSpec((1, CHUNK, gw), lambda gi, bi, ci: (bi, ci, gi))
    return pl.pallas_call(
        kern, name="ssd_fwd",
        out_shape=(jax.ShapeDtypeStruct((b, s, SSD_WIDTH), F32), jax.ShapeDtypeStruct((b, s, SSD_WIDTH), BF16),
                   jax.ShapeDtypeStruct((b, nc, g4, SSD_STATE, gw), F32)),
        grid=(g4, b, nc),
        in_specs=[xblk,
                  pl.BlockSpec((1, CHUNK, LANES), lambda gi, bi, ci: (bi, ci, SSD_WIDTH // LANES + gi)),
                  pl.BlockSpec((1, CHUNK, LANES), lambda gi, bi, ci: (bi, ci, SSD_WIDTH // LANES + g4 + gi)),
                  pl.BlockSpec((1, CHUNK, gw), lambda gi, bi, ci: (bi, ci, ZS0 // gw + gi)),
                  pl.BlockSpec((1, 1, CHUNK, LANES), lambda gi, bi, ci: (bi, gi, ci, 0)),
                  small, small,
                  pl.BlockSpec((1, gw), lambda gi, bi, ci: (0, gi)),
                  pl.BlockSpec((1, gw), lambda gi, bi, ci: (0, gi))],
        out_specs=(xblk, xblk, pl.BlockSpec((1, 1, 1, SSD_STATE, gw), lambda gi, bi, ci: (bi, ci, gi, 0, 0))),
        scratch_shapes=[pltpu.VMEM((SSD_STATE, gw), F32)],
        compiler_params=_cparams("parallel", "parallel", "arbitrary"),
    )(xact, xact, xact, proj3, dtr_g, dtb_g, alog_g, dskip_x, snw)


def _ssd_bwd(dyn3, y3, xact, proj3, hst, dtr_g, dtb_g, alog_g, dskip_x, snw):
    b, s, _ = xact.shape
    nc = s // CHUNK
    g4 = SSD_GROUPS
    gw = GROUP_WIDTH

    def kern(dyn_ref, y_ref, xs_ref, bm_ref, cm_ref, zs_ref, hst_ref, dtr_ref, dtb_ref, alog_ref, dsk_ref, snw_ref,
             dxs_ref, dbm_ref, dcm_ref, dzs_ref, ddtr_ref, dsnw_ref, dalog_ref, ddtb_ref, ddsk_ref, dh_sc):
        first = jnp.logical_and(pl.program_id(1) == 0, pl.program_id(2) == 0)

        @pl.when(first)
        def _():
            dsnw_ref[...] = jnp.zeros_like(dsnw_ref)
            dalog_ref[...] = jnp.zeros_like(dalog_ref)
            ddtb_ref[...] = jnp.zeros_like(ddtb_ref)
            ddsk_ref[...] = jnp.zeros_like(ddsk_ref)

        @pl.when(pl.program_id(2) == 0)
        def _():
            dh_sc[...] = jnp.zeros_like(dh_sc)

        cm = _ssd_common(dtr_ref, dtb_ref, alog_ref)
        row, lane = cm["row"], cm["lane"]
        y = y_ref[0]
        zs = zs_ref[0]
        sg = _sigmoid(zs)
        silu = zs * sg
        y2 = y * silu
        rstd = lax.rsqrt(jnp.mean(y2 * y2, axis=-1, keepdims=True) + EPS)
        y2h = y2 * rstd
        dyn = dyn_ref[0]
        dsnw_ref[0] += jnp.sum(dyn * y2h, axis=0, keepdims=True)
        gwv = dyn * snw_ref[...]
        dy2 = rstd * (gwv - y2h * jnp.mean(gwv * y2h, axis=-1, keepdims=True))
        dzs_ref[0] = (dy2 * y * (sg * (1.0 + zs * (1.0 - sg)))).astype(BF16)
        dy = dy2 * silu
        dyb = dy.astype(BF16)

        x = xs_ref[0]
        bmb = bm_ref[0].astype(BF16)
        cmb = cm_ref[0].astype(BF16)
        h_in = hst_ref[0, 0, 0]
        h_inb = h_in.astype(BF16)
        d_hn = dh_sc[...]
        d_hnb = d_hn.astype(BF16)
        xdt = x * cm["dt_x"]
        xdtb = xdt.astype(BF16)
        eacs = jnp.exp(cm["acs_x"])
        dte = jnp.exp(cm["end_x"] - cm["acs_x"])
        wb = (xdt * dte).astype(BF16)

        dsk_lanes = jnp.broadcast_to(jnp.sum(dy * x, axis=0, keepdims=True), (8, gw))
        ddsk_ref[0] += _sel_dot(dsk_lanes, cm["reduce"])[0:1, :]
        dyo = dy * eacs
        dyob = dyo.astype(BF16)
        dacs_x = dyo * _dot(cmb, h_inb, NN)
        dcm = _dot(dyob, h_inb, NT)
        dh_in = _dot(cmb, dyob, TN)
        dw = _dot(bmb, d_hnb, NN)
        dbm = _dot(wb, d_hnb, NT)
        dxdt = dw * dte
        e_l = dw * xdt * dte
        dacs_x = dacs_x - e_l
        dend_x = jnp.sum(e_l, axis=0, keepdims=True)
        chunk_decay = jnp.exp(cm["end_x"])
        dh_sc[...] = d_hn * chunk_decay + dh_in
        dend_x = dend_x + jnp.sum(d_hn * h_in, axis=0, keepdims=True) * chunk_decay
        last_row = lax.broadcasted_iota(jnp.int32, (CHUNK, gw), 0) == CHUNK - 1
        dacs_x = dacs_x + jnp.where(last_row, dend_x, 0.0)

        cb = _dot(cmb, bmb, NT)
        dcb = jnp.zeros((CHUNK, CHUNK), F32)
        dacs = jnp.zeros((CHUNK, LANES), F32)
        dacs_t = jnp.zeros((LANES, CHUNK), F32)
        for h in range(HEADS_PER_GROUP):
            lanes = slice(h * HEAD_DIM, (h + 1) * HEAD_DIM)
            decay = _ssd_decay(cm, h)
            m = cb * decay
            dm = _dot(dyb[:, lanes], xdtb[:, lanes], NT)
            dxs_ref[0, :, lanes] = _dot(m.astype(BF16), dyb[:, lanes], TN)
            dcb_h = dm * decay
            dcb = dcb + dcb_h
            n = dcb_h * cb
            dacs = dacs + jnp.where(lane == h, jnp.sum(n, axis=1, keepdims=True), 0.0)
            dacs_t = dacs_t + jnp.where(row == h, jnp.sum(n, axis=0, keepdims=True), 0.0)
        dcbb = dcb.astype(BF16)
        dcm_ref[0] = dcm + _dot(dcbb, bmb, NN)
        dbm_ref[0] = dbm + _dot(dcbb, cmb, TN)
        dxdt = dxdt + dxs_ref[0]
        dxs_ref[0] = dy * dsk_ref[...] + dxdt * cm["dt_x"]

        dacs = dacs - dacs_t.T + _sel_dot(dacs_x, cm["reduce"])
        ddt = _sel_dot(dxdt * x, cm["reduce"])
        triu = (row <= lane).astype(BF16)
        rc = _sel_dot(dacs, triu, left=True)
        ddt = ddt + cm["a"] * rc
        dalog_ref[0] += jnp.sum(cm["dt"] * rc, axis=0, keepdims=True) * cm["a"]
        ddtr = jnp.where(cm["head_lane"], ddt * _sigmoid(cm["pre"]), 0.0)
        ddtr_ref[0, 0] = ddtr
        ddtb_ref[0] += jnp.sum(ddtr, axis=0, keepdims=True)

    def rev(ci):
        return nc - 1 - ci

    small = pl.BlockSpec((1, 1, LANES), lambda gi, bi, ci: (gi, 0, 0))
    xblk = pl.BlockSpec((1, CHUNK, gw), lambda gi, bi, ci: (bi, rev(ci), gi))
    nblk = pl.BlockSpec((1, CHUNK, LANES), lambda gi, bi, ci: (bi, rev(ci), gi))
    gvec = pl.BlockSpec((1, gw), lambda gi, bi, ci: (0, gi))
    gacc = pl.BlockSpec((1, 1, gw), lambda gi, bi, ci: (gi, 0, 0))
    return pl.pallas_call(
        kern, name="ssd_bwd",
        out_shape=(jax.ShapeDtypeStruct((b, s, SSD_WIDTH), F32),
                   jax.ShapeDtypeStruct((b, s, g4 * SSD_STATE), F32),
                   jax.ShapeDtypeStruct((b, s, g4 * SSD_STATE), F32),
                   jax.ShapeDtypeStruct((b, s, SSD_WIDTH), BF16),
                   jax.ShapeDtypeStruct((b, g4, s, LANES), F32),
                   jax.ShapeDtypeStruct((g4, 1, gw), F32),
                   jax.ShapeDtypeStruct((g4, 1, LANES), F32),
                   jax.ShapeDtypeStruct((g4, 1, LANES), F32),
                   jax.ShapeDtypeStruct((g4, 1, LANES), F32)),
        grid=(g4, b, nc),
        in_specs=[xblk, xblk, xblk,
                  pl.BlockSpec((1, CHUNK, LANES), lambda gi, bi, ci: (bi, rev(ci), SSD_WIDTH // LANES + gi)),
                  pl.BlockSpec((1, CHUNK, LANES), lambda gi, bi, ci: (bi, rev(ci), SSD_WIDTH // LANES + g4 + gi)),
                  pl.BlockSpec((1, CHUNK, gw), lambda gi, bi, ci: (bi, rev(ci), ZS0 // gw + gi)),
                  pl.BlockSpec((1, 1, 1, SSD_STATE, gw), lambda gi, bi, ci: (bi, rev(ci), gi, 0, 0)),
                  pl.BlockSpec((1, 1, CHUNK, LANES), lambda gi, bi, ci: (bi, gi, rev(ci), 0)),
                  small, small, gvec, gvec],
        out_specs=(xblk, nblk, nblk, xblk,
                   pl.BlockSpec((1, 1, CHUNK, LANES), lambda gi, bi, ci: (bi, gi, rev(ci), 0)),
                   gacc, small, small, small),
        scratch_shapes=[pltpu.VMEM((SSD_STATE, gw), F32)],
        compiler_params=_cparams("parallel", "arbitrary", "arbitrary"),
    )(dyn3, y3, xact, xact, xact, proj3, hst, dtr_g, dtb_g, alog_g, dskip_x, snw)


def _adamw(w, g, m, v, name):
    r, c = w.shape
    tr = 128 if r % 128 == 0 else r

    def kern(w_ref, g_ref, m_ref, v_ref, d_ref, nm_ref, nv_ref):
        gv = g_ref[...]
        nm = ADAM_B1 * m_ref[...] + (1.0 - ADAM_B1) * gv
        nv = ADAM_B2 * v_ref[...] + (1.0 - ADAM_B2) * (gv * gv)
        m_hat = nm / (1.0 - ADAM_B1 ** ADAM_STEP)
        v_hat = nv / (1.0 - ADAM_B2 ** ADAM_STEP)
        d_ref[...] = -ADAM_LR * (m_hat / (jnp.sqrt(v_hat) + ADAM_EPS) + ADAM_WD * w_ref[...])
        nm_ref[...] = nm
        nv_ref[...] = nv

    blk = pl.BlockSpec((tr, c), lambda i: (i, 0))
    out = jax.ShapeDtypeStruct((r, c), F32)
    return pl.pallas_call(
        kern, name=name, out_shape=(out, out, out), grid=(r // tr,),
        in_specs=[blk] * 4, out_specs=(blk, blk, blk),
        compiler_params=_cparams("parallel"),
    )(w, g, m, v)


ANY = pl.BlockSpec(memory_space=pl.ANY)


def _position():
    return lax.axis_index("x"), lax.axis_index("y"), lax.axis_index("c")


def _other_chips(x, y):
    return [(1 - x, y), (x, 1 - y), (1 - x, 1 - y)]


def _dma_sems(n):
    return [pltpu.SemaphoreType.DMA((n,)), pltpu.SemaphoreType.DMA((n,))]


def _gather_weights(shards):
    n = len(shards)

    def body(*refs):
        p_refs, out_refs = refs[:n], refs[n:2 * n]
        send_sems, recv_sems = refs[2 * n:]
        x, y, c = _position()
        me = 2 * x + y
        chips = _other_chips(x, y)

        def slab(a, chip, hf):
            half = shards[a].shape[0] // 2
            return out_refs[a].at[chip, pl.ds(hf * half, half), :]

        def my_half(a):
            half = shards[a].shape[0] // 2
            return p_refs[a].at[pl.ds(c * half, half), :]

        def over_ici(a, j, chip_from):
            px, py = chips[j]
            return pltpu.make_async_remote_copy(
                src_ref=my_half(a), dst_ref=slab(a, chip_from, c),
                send_sem=send_sems.at[3 * a + j], recv_sem=recv_sems.at[3 * a + j],
                device_id=(px, py, c), device_id_type=MESH)

        def to_sibling(a, j, hf):
            px, py = chips[j]
            return pltpu.make_async_remote_copy(
                src_ref=slab(a, 2 * px + py, hf), dst_ref=slab(a, 2 * px + py, hf),
                send_sem=send_sems.at[3 * (n + a) + j], recv_sem=recv_sems.at[3 * (n + a) + j],
                device_id=(x, y, 1 - c), device_id_type=MESH)

        first = [over_ici(a, j, me) for a in range(n) for j in range(3)]
        for cp in first:
            cp.start()
        passed = []
        for a in range(n):
            for j, (px, py) in enumerate(chips):
                over_ici(a, j, 2 * px + py).wait_recv()
                passed.append(to_sibling(a, j, c))
                passed[-1].start()
        for a in range(n):
            for j in range(3):
                to_sibling(a, j, 1 - c).wait_recv()
        for cp in first + passed:
            cp.wait_send()

    return pl.pallas_call(
        body, name="gather_weights",
        out_shape=[jax.ShapeDtypeStruct((N_CHIPS, *v.shape), v.dtype) for v in shards],
        in_specs=[ANY] * n, out_specs=[ANY] * n,
        scratch_shapes=_dma_sems(6 * n),
    )(*shards)


def _swap_halves(parts):
    n = len(parts)

    def body(*refs):
        v_refs, out_refs = refs[:n], refs[n:2 * n]
        send_sems, recv_sems = refs[2 * n:]
        x, y, c = _position()
        copies = []
        for a in range(n):
            half = parts[a].shape[1] // 2
            copies.append(pltpu.make_async_remote_copy(
                src_ref=v_refs[a].at[:, pl.ds((1 - c) * half, half), :], dst_ref=out_refs[a],
                send_sem=send_sems.at[a], recv_sem=recv_sems.at[a], device_id=(x, y, 1 - c), device_id_type=MESH))
        for cp in copies:
            cp.start()
        for cp in copies:
            cp.wait()

    return pl.pallas_call(
        body, name="grad_swap_halves",
        out_shape=[jax.ShapeDtypeStruct((v.shape[0], v.shape[1] // 2, v.shape[2]), v.dtype) for v in parts],
        in_specs=[ANY] * n, out_specs=[ANY] * n,
        scratch_shapes=_dma_sems(n),
    )(*parts)


def _chip_all_to_all(parts):
    n = len(parts)

    def body(*refs):
        p_refs, out_refs = refs[:n], refs[n:2 * n]
        send_sems, recv_sems = refs[2 * n:]
        x, y, c = _position()
        chips = _other_chips(x, y)
        sends = [pltpu.make_async_remote_copy(
            src_ref=p_refs[a].at[2 * px + py], dst_ref=out_refs[a].at[j],
            send_sem=send_sems.at[3 * a + j], recv_sem=recv_sems.at[3 * a + j],
            device_id=(px, py, c), device_id_type=MESH) for a in range(n) for j, (px, py) in enumerate(chips)]
        for cp in sends:
            cp.start()
        for cp in sends:
            cp.wait()

    return pl.pallas_call(
        body, name="grad_all_to_all",
        out_shape=[jax.ShapeDtypeStruct((N_CHIPS - 1, *v.shape[1:]), v.dtype) for v in parts],
        in_specs=[ANY] * n, out_specs=[ANY] * n,
        scratch_shapes=_dma_sems(3 * n),
    )(*parts)


def _join_halves(wholes):
    n = len(wholes)

    def body(*refs):
        out_refs = refs[n:2 * n]
        send_sems, recv_sems = refs[2 * n:]
        x, y, c = _position()
        copies = []
        for a in range(n):
            half = wholes[a].shape[0] // 2
            rows = out_refs[a].at[pl.ds(c * half, half), :]
            copies.append(pltpu.make_async_remote_copy(
                src_ref=rows, dst_ref=rows, send_sem=send_sems.at[a], recv_sem=recv_sems.at[a],
                device_id=(x, y, 1 - c), device_id_type=MESH))
        for cp in copies:
            cp.start()
        for cp in copies:
            cp.wait()

    return pl.pallas_call(
        body, name="grad_join_halves",
        out_shape=[jax.ShapeDtypeStruct(v.shape, v.dtype) for v in wholes],
        in_specs=[ANY] * n, out_specs=[ANY] * n,
        input_output_aliases={a: a for a in range(n)},
        scratch_shapes=_dma_sems(n),
    )(*wholes)


ADD_ROWS = 128


def _add_halves(g, sw, place, name):
    n, rows, cols = g.shape
    half = rows // 2
    nb = half // ADD_ROWS

    def kern(p_ref, g_ref, s_ref, o_ref):
        o_ref[...] = (g_ref[...] + s_ref[...]).astype(BF16)

    blk = pl.BlockSpec((1, ADD_ROWS, cols), lambda j, i, p_ref: (j, i, 0))
    return pl.pallas_call(
        kern, name=name,
        out_shape=jax.ShapeDtypeStruct((n, half, cols), BF16),
        grid_spec=pltpu.PrefetchScalarGridSpec(
            num_scalar_prefetch=1, grid=(n, nb),
            in_specs=[pl.BlockSpec((1, ADD_ROWS, cols), lambda j, i, p_ref: (j, p_ref[0] * nb + i, 0)), blk],
            out_specs=blk),
        compiler_params=_cparams("parallel", "parallel"),
    )(place, g, sw)


def _sum_chips(own, rx, place, name):
    _, half, cols = rx.shape
    nb = half // ADD_ROWS

    def kern(p_ref, own_ref, r_ref, o_ref):
        total = own_ref[0].astype(F32)
        for j in range(N_CHIPS - 1):
            total = total + r_ref[j].astype(F32)
        o_ref[...] = total

    return pl.pallas_call(
        kern, name=name,
        out_shape=jax.ShapeDtypeStruct((2 * half, cols), F32),
        grid_spec=pltpu.PrefetchScalarGridSpec(
            num_scalar_prefetch=1, grid=(nb,),
            in_specs=[pl.BlockSpec((1, ADD_ROWS, cols), lambda i, p_ref: (p_ref[1], i, 0)),
                      pl.BlockSpec((N_CHIPS - 1, ADD_ROWS, cols), lambda i, p_ref: (0, i, 0))],
            out_specs=pl.BlockSpec((ADD_ROWS, cols), lambda i, p_ref: (p_ref[0] * nb + i, 0))),
        compiler_params=_cparams("parallel"),
    )(place, own, rx)


def _gather_small(v, reduce, name):
    rows = v.shape[0]

    def body(v_ref, out_ref, buf, send_sems, recv_sems):
        x, y, c = _position()
        me = 4 * x + 2 * y + c
        buf[me] = v_ref[...]
        peers = [(x ^ (k >> 2), y ^ ((k >> 1) & 1), c ^ (k & 1)) for k in range(1, 8)]
        copies = [pltpu.make_async_remote_copy(
            src_ref=v_ref, dst_ref=buf.at[me],
            send_sem=send_sems.at[k], recv_sem=recv_sems.at[k],
            device_id=peer, device_id_type=MESH) for k, peer in enumerate(peers)]
        for cp in copies:
            cp.start()
        for k, (px, py, pc) in enumerate(peers):
            pltpu.make_async_remote_copy(
                src_ref=v_ref, dst_ref=buf.at[4 * px + 2 * py + pc],
                send_sem=send_sems.at[k], recv_sem=recv_sems.at[k],
                device_id=(px, py, pc), device_id_type=MESH).wait_recv()
        for cp in copies:
            cp.wait_send()
        if reduce:
            total = buf[0]
            for d in range(1, 8):
                total = total + buf[d]
            out_ref[...] = total
        else:
            out_ref[...] = buf[...]

    vm = pl.BlockSpec(memory_space=pltpu.VMEM)
    return pl.pallas_call(
        body, name=name,
        out_shape=jax.ShapeDtypeStruct((rows, LANES) if reduce else (8, rows, LANES), F32),
        in_specs=[vm], out_specs=vm,
        scratch_shapes=[pltpu.VMEM((8, rows, LANES), F32), pltpu.SemaphoreType.DMA((7,)), pltpu.SemaphoreType.DMA((7,))],
    )(v)


def _pad_rows(a, rows):
    return jnp.pad(a, ((0, rows - a.shape[0]), (0, 0)))


def _lane_pad(v):
    n = v.shape[1]
    return jnp.pad(v, ((0, 0), (0, -n % LANES)))


def _gather_all(w_in, w_attn_out, w_ssm_out, w_o, conv_w):
    d = D_MODEL
    own = [a[0].astype(BF16) for a in (w_in, w_attn_out, w_ssm_out, w_o)]
    gathered = _gather_weights(own)
    chip = 2 * lax.axis_index("x") + lax.axis_index("y")
    w_in_all, w_ao, w_so, w_oo = [[jnp.where(chip == q, o, g[q]) for q in range(N_CHIPS)] for o, g in zip(own, gathered)]
    last = w_in_all[N_CHIPS - 1]
    w_proj = jnp.concatenate([w_in_all[0], w_in_all[1], w_in_all[2], last[:, :LAST_DT0], last[:, LAST_DT0 + 32:],
                              last[:, LAST_DT0:LAST_DT0 + 32], jnp.zeros((d, DT_PAD - 32), BF16)], axis=1)
    w_ao = jnp.concatenate(w_ao, axis=0)
    w_so = jnp.concatenate(w_so, axis=0)
    w_oo = jnp.concatenate(w_oo, axis=0)
    conv_rows = conv_w[0].size // LANES
    conv_all = _gather_small(conv_w[0].reshape(conv_rows, LANES), False, "gather_conv_w")
    conv_w_all = conv_all[0::2].reshape(N_CHIPS, CONV_K, CONV_DIM // N_CHIPS).transpose(1, 0, 2).reshape(CONV_K, CONV_DIM)

    return w_proj, w_ao, w_so, w_oo, conv_w_all


def _local_step(x, loss_target, norm_w, w_proj, conv_w_all, conv_b, dt_bias, a_log, d_skip, ssm_norm_w,
                w_ao, w_so, w_oo, final_norm_w):
    b, s, d = x.shape
    t = b * s
    g4, hg = SSD_GROUPS, HEADS_PER_GROUP
    dtb_g = _lane_pad(dt_bias.reshape(g4, hg)).reshape(g4, 1, LANES)
    alog_g = _lane_pad(a_log.reshape(g4, hg)).reshape(g4, 1, LANES)
    dskip_x = jnp.repeat(d_skip, HEAD_DIM, axis=1)
    fnw = final_norm_w.reshape(1, d)

    x2 = x.reshape(t, d)
    h = _rms_fwd(x2, norm_w)
    big_tm = min(t, 2048)
    proj = _matmul(h, w_proj, tm=big_tm, tn=1280, tk=1024, name="proj")
    proj3 = proj.reshape(b, s, NP)
    o3, yp3 = _attn_fwd(proj3)
    xact = _conv_fwd(proj3, conv_w_all, conv_b)
    dtr = proj3[:, :, DT0:DT0 + g4 * hg].reshape(b, s, g4, hg).transpose(0, 2, 1, 3)
    dtr_g = jnp.pad(dtr, ((0, 0), (0, 0), (0, 0), (0, LANES - hg)))
    y3, yn3, hst = _ssd_fwd(xact, proj3, dtr_g, dtb_g, alog_g, dskip_x, ssm_norm_w)
    yp = yp3.reshape(t, D_MODEL)
    yn = yn3.reshape(t, SSD_WIDTH)
    ya = _matmul(yp, w_ao, tm=512, tn=1024, tk=1024, name="attn_out")
    ys = _matmul(yn, w_so, tm=512, tn=1024, tk=2048, name="ssm_out")
    merged = _merge_fwd(proj, ya, ys)
    mo = _matmul(merged, w_oo, tm=512, tn=1024, tk=1024, name="out_proj")
    dout, doutb, loss_part, d_fnw = _final_fwd_bwd(x2, mo, loss_target.reshape(t, d), fnw)

    dmerged = _matmul(doutb, w_oo, tb=True, tm=512, tn=1024, tk=1024, name="d_merged")
    g_wo = _matmul(merged, doutb, ta=True, tm=512, tn=1024, tk=1024, name="g_w_o")
    dya, dys, dgate = _merge_bwd(dmerged, proj, ya, ys)
    dyp = _matmul(dya, w_ao, tb=True, tm=512, tn=1024, tk=1024, name="d_attn_pre")
    g_wao = _matmul(yp, dya, ta=True, tm=512, tn=1024, tk=1024, name="g_w_attn_out")
    dyn = _matmul(dys, w_so, tb=True, tm=1024, tn=2048, tk=1024, name="d_ssm_norm")
    g_wso = _matmul(yn, dys, ta=True, tm=1024, tn=1024, tk=1024, name="g_w_ssm_out")
    dq, dk, dv, dza = _attn_bwd(proj3, dyp.reshape(b, s, D_MODEL), o3)
    (dxs, dbm, dcm, dzs, ddtr_g, d_snw_g, d_alog_g, d_dtb_g, d_dsk_g) = _ssd_bwd(
        dyn.reshape(b, s, SSD_WIDTH), y3, xact, proj3, hst, dtr_g, dtb_g, alog_g, dskip_x, ssm_norm_w)
    dx_xs, g_cw_xs, g_cb_xs = _conv_bwd(dxs, proj3, conv_w_all, conv_b, 0, "conv_bwd_x")
    dx_bm, g_cw_bm, g_cb_bm = _conv_bwd(dbm, proj3, conv_w_all, conv_b, SSD_WIDTH, "conv_bwd_b")
    dx_cm, g_cw_cm, g_cb_cm = _conv_bwd(dcm, proj3, conv_w_all, conv_b, SSD_WIDTH + g4 * SSD_STATE, "conv_bwd_c")
    ddt = ddtr_g[:, :, :, :hg].transpose(0, 2, 1, 3).reshape(b, s, g4 * hg).astype(BF16)
    dproj = jnp.concatenate([dq, dk, dv, dza, dzs, dx_xs, dx_bm, dx_cm, dgate.reshape(b, s, 2 * D_MODEL),
                             jnp.pad(ddt, ((0, 0), (0, 0), (0, DT_PAD - g4 * hg)))], axis=2).reshape(t, NP)
    g_wproj = _matmul(h, dproj, ta=True, tm=1024, tn=1280, tk=1024, name="g_w_in")
    dh = _matmul(dproj, w_proj, tb=True, tm=big_tm, tn=1024, tk=1280, name="d_h")
    grad_x, d_nw = _rms_bwd(dh, x2, norm_w, dout)
    g_cw = jnp.concatenate([g_cw_xs, g_cw_bm, g_cw_cm], axis=1)
    g_cb = jnp.concatenate([g_cb_xs, g_cb_bm, g_cb_cm], axis=1)
    return (loss_part, grad_x, d_nw, g_wproj, g_cw, g_cb, d_dtb_g, d_alog_g, d_dsk_g, d_snw_g, g_wao, g_wso, g_wo, d_fnw)


def kernel(x, norm_w, w_in, conv_w, conv_b, dt_bias, a_log, d_skip, ssm_norm_w, w_attn_out, w_ssm_out, w_o, final_norm_w, loss_target, m_norm_w, m_w_in, m_conv_w, m_conv_b, m_dt_bias, m_a_log, m_d_skip, m_ssm_norm_w, m_w_attn_out, m_w_ssm_out, m_w_o, m_final_norm_w, v_norm_w, v_w_in, v_conv_w, v_conv_b, v_dt_bias, v_a_log, v_d_skip, v_ssm_norm_w, v_w_attn_out, v_w_ssm_out, v_w_o, v_final_norm_w):
    b, s, d = x.shape
    core = lax.axis_index("c")
    g4, hg = SSD_GROUPS, HEADS_PER_GROUP
    shard_cols = w_in.shape[2]
    w_proj, w_ao, w_so, w_oo, conv_w_all = _gather_all(w_in, w_attn_out, w_ssm_out, w_o, conv_w)
    (loss_part, grad_x, d_nw, g_wproj, g_cw, g_cb, d_dtb_g, d_alog_g, d_dsk_g, d_snw_g, g_wao, g_wso, g_wo, d_fnw) = _local_step(
        x, loss_target, norm_w, w_proj, conv_w_all, conv_b, dt_bias, a_log, d_skip, ssm_norm_w, w_ao, w_so, w_oo, final_norm_w)

    last0 = (N_CHIPS - 1) * shard_cols
    g_last = jnp.concatenate([g_wproj[:, last0:GATE0], g_wproj[:, DT0:DT0 + 32], g_wproj[:, GATE0:DT0]], axis=1)
    g_win_chips = jnp.stack([g_wproj[:, j * shard_cols:(j + 1) * shard_cols] for j in range(N_CHIPS - 1)] + [g_last])
    g_out_chips = jnp.concatenate([g.reshape(N_CHIPS, -1, d) for g in (g_wao, g_wso, g_wo)], axis=1)
    parts = [g_win_chips, g_out_chips]
    chip = 2 * lax.axis_index("x") + lax.axis_index("y")
    place = jnp.stack([core, chip]).astype(jnp.int32)
    from_sibling = _swap_halves(parts)
    chip_sums = [_add_halves(p, f, place, "grad_add_halves_%d" % i) for i, (p, f) in enumerate(zip(parts, from_sibling))]
    from_chips = _chip_all_to_all(chip_sums)
    wholes = [_sum_chips(o, r, place, "grad_sum_chips_%d" % i) for i, (o, r) in enumerate(zip(chip_sums, from_chips))]
    g_w_in, g_out = _join_halves(wholes)

    small = jnp.concatenate([
        loss_part, d_nw, g_cb, _lane_pad(d_dtb_g[:, 0, :hg].reshape(1, -1)), _lane_pad(d_alog_g[:, 0, :hg].reshape(1, -1)),
        _lane_pad(d_dsk_g[:, 0, :hg].reshape(1, -1)),
        d_snw_g.reshape(1, -1), d_fnw, g_cw.reshape(1, -1)], axis=1)
    small_rows = small.shape[1] // LANES
    reduced = _gather_small(_pad_rows(small.reshape(small_rows, LANES), -(-small_rows // 8) * 8), True, "reduce_small")
    flat = reduced.reshape(-1)

    def take(start, n):
        return flat[start:start + n].reshape(1, n)

    loss = flat[0]
    pos = LANES
    g_norm_w = take(pos, d); pos += d
    g_conv_b = take(pos, CONV_DIM); pos += CONV_DIM
    g_dt_bias = take(pos, g4 * hg); pos += LANES
    g_a_log = take(pos, g4 * hg); pos += LANES
    g_d_skip = take(pos, g4 * hg); pos += LANES
    g_ssm_norm_w = take(pos, SSD_WIDTH); pos += SSD_WIDTH
    g_final_norm_w = take(pos, d); pos += d
    conv_cols = CONV_DIM // N_CHIPS
    g_conv_w = lax.dynamic_slice_in_dim(flat[pos:pos + CONV_K * CONV_DIM].reshape(CONV_K, CONV_DIM), chip * conv_cols, conv_cols, axis=1)

    rows_ao, rows_so = D_MODEL // N_CHIPS, SSD_WIDTH // N_CHIPS
    g_w_attn_out = g_out[:rows_ao]
    g_w_ssm_out = g_out[rows_ao:rows_ao + rows_so]
    g_w_o = g_out[rows_ao + rows_so:]

    names = ["norm_w", "w_in", "conv_w", "conv_b", "dt_bias", "a_log", "d_skip", "ssm_norm_w",
             "w_attn_out", "w_ssm_out", "w_o", "final_norm_w"]
    weights = [norm_w, w_in, conv_w, conv_b, dt_bias, a_log, d_skip, ssm_norm_w, w_attn_out, w_ssm_out, w_o, final_norm_w]
    grads = [g_norm_w, g_w_in, g_conv_w, g_conv_b, g_dt_bias, g_a_log, g_d_skip, g_ssm_norm_w,
             g_w_attn_out, g_w_ssm_out, g_w_o, g_final_norm_w]
    ms = [m_norm_w, m_w_in, m_conv_w, m_conv_b, m_dt_bias, m_a_log, m_d_skip, m_ssm_norm_w,
          m_w_attn_out, m_w_ssm_out, m_w_o, m_final_norm_w]
    vs = [v_norm_w, v_w_in, v_conv_w, v_conv_b, v_dt_bias, v_a_log, v_d_skip, v_ssm_norm_w,
          v_w_attn_out, v_w_ssm_out, v_w_o, v_final_norm_w]
    out_g, out_d, out_m, out_v = [], [], [], []
    for name, w, g, m, v in zip(names, weights, grads, ms, vs):
        shape2 = g.shape
        dlt, nm, nv = _adamw(w.reshape(shape2), g, m.reshape(shape2), v.reshape(shape2), "adamw_" + name)
        out_g.append(g.reshape(w.shape))
        out_d.append(dlt.reshape(w.shape))
        out_m.append(nm.reshape(w.shape))
        out_v.append(nv.reshape(w.shape))

    return (loss, grad_x.reshape(b, s, d), *out_g, *out_d, *out_m, *out_v)
```

```python
import jax
import jax.numpy as jnp
from jax import lax
from jax.experimental import pallas as pl
from jax.experimental.pallas import tpu as pltpu

F32 = jnp.float32
BF16 = jnp.bfloat16
MESH = pl.DeviceIdType.MESH

D_MODEL = 1024
SB_HEADS = 16
HEAD_DIM = 64
SSD_WIDTH = 2048
SSD_GROUPS = 4
GROUP_WIDTH = SSD_WIDTH // SSD_GROUPS
HEADS_PER_GROUP = 8
SSD_STATE = 128
CHUNK = 128
CONV_K = 4
CONV_DIM = 3072
D_PROJ = 11296
EPS = 1e-6
ADAM_LR, ADAM_B1, ADAM_B2, ADAM_EPS, ADAM_WD, ADAM_STEP = 0.001, 0.9, 0.999, 1e-08, 0.01, 10

LANES = 128
HP_WIDTH = 4 * LANES
ZS0, GATE0, XBC0, DT0 = 4096, 6144, 8192, 11264
DT_PAD = 256
NP = DT0 + DT_PAD
N_CHIPS = 4
VMEM_LIMIT = 56 * 1024 * 1024


def _proj_segments():
    segs = [(base + hp * LANES, LANES) for hp in range(SB_HEADS // 2) for base in (0, 1024, 2048, 3072)]
    return segs + [(4096, SSD_WIDTH), (9248, 2 * D_MODEL), (6144, CONV_DIM), (9216, SSD_WIDTH // HEAD_DIM)]


def _to_proj_layout(w):
    pieces = [w[:, a:a + n] for a, n in _proj_segments()]
    return jnp.concatenate(pieces + [jnp.zeros((w.shape[0], NP - D_PROJ), w.dtype)], axis=1)


def _from_proj_layout(g):
    at, pos = [], 0
    for a, n in _proj_segments():
        at.append((a, n, pos))
        pos += n
    return jnp.concatenate([g[:, pos:pos + n] for a, n, pos in sorted(at)], axis=1)


def _cparams(*sem):
    return pltpu.CompilerParams(dimension_semantics=sem or None, vmem_limit_bytes=VMEM_LIMIT)


def _sigmoid(z):
    return 1.0 / (1.0 + jnp.exp(-z))


def _dot(a, b, dims, precision=None):
    return lax.dot_general(a, b, (dims, ((), ())), preferred_element_type=F32, precision=precision)


NN = ((1,), (0,))
NT = ((1,), (1,))
TN = ((0,), (0,))


def _matmul(a, b, *, ta=False, tb=False, out_dtype=F32, tm, tn, tk, name):
    m, k = (a.shape[1], a.shape[0]) if ta else a.shape
    n = b.shape[0] if tb else b.shape[1]
    assert m % tm == 0 and n % tn == 0 and k % tk == 0, (name, m, n, k)
    nk = k // tk
    use_scratch = out_dtype != F32
    dims = ((0,) if ta else (1,), (1,) if tb else (0,))

    def kern(a_ref, b_ref, o_ref, *scratch):
        acc = scratch[0] if use_scratch else o_ref
        kk = pl.program_id(2)

        @pl.when(kk == 0)
        def _():
            acc[...] = jnp.zeros_like(acc)

        acc[...] += _dot(a_ref[...], b_ref[...], dims)
        if use_scratch:
            @pl.when(kk == nk - 1)
            def _():
                o_ref[...] = acc[...].astype(out_dtype)

    a_spec = pl.BlockSpec((tk, tm), lambda i, j, q: (q, i)) if ta else pl.BlockSpec((tm, tk), lambda i, j, q: (i, q))
    b_spec = pl.BlockSpec((tn, tk), lambda i, j, q: (j, q)) if tb else pl.BlockSpec((tk, tn), lambda i, j, q: (q, j))
    return pl.pallas_call(
        kern, name=name,
        out_shape=jax.ShapeDtypeStruct((m, n), out_dtype),
        grid=(m // tm, n // tn, nk),
        in_specs=[a_spec, b_spec],
        out_specs=pl.BlockSpec((tm, tn), lambda i, j, q: (i, j)),
        scratch_shapes=[pltpu.VMEM((tm, tn), F32)] if use_scratch else [],
        compiler_params=_cparams("parallel", "parallel", "arbitrary"),
    )(a, b)


ROWS = 256


def _rms_fwd(x2, w):
    t, d = x2.shape

    def kern(x_ref, w_ref, h_ref):
        x = x_ref[...]
        r = lax.rsqrt(jnp.mean(x * x, axis=-1, keepdims=True) + EPS)
        h_ref[...] = (x * r * w_ref[...]).astype(BF16)

    return pl.pallas_call(
        kern, name="rms_fwd",
        out_shape=jax.ShapeDtypeStruct((t, d), BF16),
        grid=(t // ROWS,),
        in_specs=[pl.BlockSpec((ROWS, d), lambda i: (i, 0)), pl.BlockSpec((1, d), lambda i: (0, 0))],
        out_specs=pl.BlockSpec((ROWS, d), lambda i: (i, 0)),
        compiler_params=_cparams("parallel"),
    )(x2, w)


def _rms_bwd(dh, x2, w, dout):
    t, d = x2.shape

    def kern(dh_ref, x_ref, w_ref, dout_ref, gx_ref, dw_ref):
        @pl.when(pl.program_id(0) == 0)
        def _():
            dw_ref[...] = jnp.zeros_like(dw_ref)

        x = x_ref[...]
        r = lax.rsqrt(jnp.mean(x * x, axis=-1, keepdims=True) + EPS)
        xh = x * r
        g = dh_ref[...]
        dw_ref[...] += jnp.sum(g * xh, axis=0, keepdims=True)
        gw = g * w_ref[...]
        gx_ref[...] = dout_ref[...] + r * (gw - xh * jnp.mean(gw * xh, axis=-1, keepdims=True))

    row = pl.BlockSpec((ROWS, d), lambda i: (i, 0))
    vec = pl.BlockSpec((1, d), lambda i: (0, 0))
    return pl.pallas_call(
        kern, name="rms_bwd",
        out_shape=(jax.ShapeDtypeStruct((t, d), F32), jax.ShapeDtypeStruct((1, d), F32)),
        grid=(t // ROWS,),
        in_specs=[row, row, vec, row],
        out_specs=(row, vec),
        compiler_params=_cparams("arbitrary"),
    )(dh, x2, w, dout)


def _final_fwd_bwd(x2, mo, target, w):
    t, d = x2.shape

    def kern(x_ref, mo_ref, t_ref, w_ref, dout_ref, doutb_ref, loss_ref, dw_ref):
        @pl.when(pl.program_id(0) == 0)
        def _():
            loss_ref[...] = jnp.zeros_like(loss_ref)
            dw_ref[...] = jnp.zeros_like(dw_ref)

        u = x_ref[...] + mo_ref[...]
        r = lax.rsqrt(jnp.mean(u * u, axis=-1, keepdims=True) + EPS)
        uh = u * r
        wv = w_ref[...]
        err = uh * wv - t_ref[...]
        loss_ref[...] += (0.5 / d) * jnp.sum(err * err)
        dy = err * (1.0 / d)
        dw_ref[...] += jnp.sum(dy * uh, axis=0, keepdims=True)
        gw = dy * wv
        du = r * (gw - uh * jnp.mean(gw * uh, axis=-1, keepdims=True))
        dout_ref[...] = du
        doutb_ref[...] = du.astype(BF16)

    row = pl.BlockSpec((ROWS, d), lambda i: (i, 0))
    vec = pl.BlockSpec((1, d), lambda i: (0, 0))
    return pl.pallas_call(
        kern, name="final_fwd_bwd",
        out_shape=(jax.ShapeDtypeStruct((t, d), F32), jax.ShapeDtypeStruct((t, d), BF16),
                   jax.ShapeDtypeStruct((1, LANES), F32), jax.ShapeDtypeStruct((1, d), F32)),
        grid=(t // ROWS,),
        in_specs=[row, row, row, vec],
        out_specs=(row, row, pl.BlockSpec((1, LANES), lambda i: (0, 0)), vec),
        compiler_params=_cparams("arbitrary"),
    )(x2, mo, target, w)


def _merge_fwd(proj2, ya, ys):
    t = ya.shape[0]
    gblk = GATE0 // D_MODEL

    def kern(ga_ref, gs_ref, ya_ref, ys_ref, o_ref):
        o_ref[...] = (_sigmoid(ga_ref[...]) * ya_ref[...] + _sigmoid(gs_ref[...]) * ys_ref[...]).astype(BF16)

    row = pl.BlockSpec((ROWS, D_MODEL), lambda i: (i, 0))
    return pl.pallas_call(
        kern, name="merge_fwd",
        out_shape=jax.ShapeDtypeStruct((t, D_MODEL), BF16),
        grid=(t // ROWS,),
        in_specs=[pl.BlockSpec((ROWS, D_MODEL), lambda i: (i, gblk)),
                  pl.BlockSpec((ROWS, D_MODEL), lambda i: (i, gblk + 1)), row, row],
        out_specs=row,
        compiler_params=_cparams("parallel"),
    )(proj2, proj2, ya, ys)


def _merge_bwd(dm, proj2, ya, ys):
    t = ya.shape[0]
    gblk = GATE0 // D_MODEL

    def kern(dm_ref, ga_ref, gs_ref, ya_ref, ys_ref, dya_ref, dys_ref, dg_ref):
        g = dm_ref[...]
        sa = _sigmoid(ga_ref[...])
        ss = _sigmoid(gs_ref[...])
        dya_ref[...] = (g * sa).astype(BF16)
        dys_ref[...] = (g * ss).astype(BF16)
        dg_ref[:, :D_MODEL] = (g * ya_ref[...] * sa * (1.0 - sa)).astype(BF16)
        dg_ref[:, D_MODEL:] = (g * ys_ref[...] * ss * (1.0 - ss)).astype(BF16)

    row = pl.BlockSpec((ROWS, D_MODEL), lambda i: (i, 0))
    return pl.pallas_call(
        kern, name="merge_bwd",
        out_shape=(jax.ShapeDtypeStruct((t, D_MODEL), BF16), jax.ShapeDtypeStruct((t, D_MODEL), BF16),
                   jax.ShapeDtypeStruct((t, NP), BF16)),
        grid=(t // ROWS,),
        in_specs=[row, pl.BlockSpec((ROWS, D_MODEL), lambda i: (i, gblk)),
                  pl.BlockSpec((ROWS, D_MODEL), lambda i: (i, gblk + 1)), row, row],
        out_specs=(row, row, pl.BlockSpec((ROWS, 2 * D_MODEL), lambda i: (i, GATE0 // (2 * D_MODEL)))),
        compiler_params=_cparams("parallel"),
    )(dm, proj2, proj2, ya, ys)


TQ = 256
TK = 256
HEAD_LANES = (slice(0, HEAD_DIM), slice(HEAD_DIM, 2 * HEAD_DIM))


def _tri(pred):
    r = lax.broadcasted_iota(jnp.int32, (TK, TK), 0)
    c = lax.broadcasted_iota(jnp.int32, (TK, TK), 1)
    return pred(r, c).astype(BF16)


def _split_bf16(v):
    hi = v.astype(BF16)
    lo = (v - hi.astype(F32)).astype(BF16)
    return hi, lo


def _tri_dot(v, tri):
    hi, lo = _split_bf16(v)
    return _dot(hi, tri, NN) + _dot(lo, tri, NN)


def _sb_logs(z, mask):
    l1p = jnp.log(1.0 + jnp.exp(-jnp.abs(z)))
    lb = jnp.minimum(z, 0.0) - l1p
    lom = -jnp.maximum(z, 0.0) - l1p
    if mask is not None:
        lom = jnp.where(mask, lom, 0.0)
    return lb, lom


def _sb_weights(lb, later, carry_r, mask):
    a = jnp.exp(lb + (later + carry_r))
    if mask is not None:
        a = jnp.where(mask, a, 0.0)
    return a


DEAD = -104.0


def _while_alive(n, carry, step):
    def alive(cr):
        return jnp.max(jnp.maximum(cr[0][0], cr[1][0])) > DEAD

    def cond(state):
        jj, go, _ = state
        return jnp.logical_and(jj < n, go)

    def body(state):
        jj, _, cr = state
        cr = step(jj, cr)
        return jj + 1, alive(cr), cr

    return lax.while_loop(cond, body, (jnp.int32(0), alive(carry), carry))[2]


Q_LANES, K_LANES, V_LANES, ZA_LANES = (slice(i * LANES, (i + 1) * LANES) for i in range(4))


def _split_heads(dst, src, scale=None):
    for h, lanes in enumerate(HEAD_LANES):
        v = src[:, lanes]
        dst[h] = (v if scale is None else v * scale).astype(BF16)


def _attn_fwd(proj3):
    b, s, _ = proj3.shape
    nq = s // TQ
    scale = HEAD_DIM ** -0.5

    def kern(x_ref, o_ref, yp_ref, qs, ks, vs):
        _split_heads(qs, x_ref[0, :, Q_LANES], scale)
        _split_heads(ks, x_ref[0, :, K_LANES])
        _split_heads(vs, x_ref[0, :, V_LANES])
        za_ref = x_ref.at[:, :, ZA_LANES]
        row = lax.broadcasted_iota(jnp.int32, (TQ, TK), 0)
        col = lax.broadcasted_iota(jnp.int32, (TQ, TK), 1)
        tri_gt = _tri(lambda j, sk: j > sk)

        def q_block(i, _):
            r0 = pl.multiple_of(i * TQ, TQ)
            n_kb = (r0 + TQ + TK - 1) // TK
            qh = [qs[h, pl.ds(r0, TQ), :] for h in range(2)]

            def k_block(c0, carry, mask):
                kh = [ks[h, pl.ds(c0, TK), :] for h in range(2)]
                vh = [vs[h, pl.ds(c0, TK), :] for h in range(2)]
                z = [_dot(qh[h], kh[h], NT) for h in range(2)]
                logs, later = [], []
                for h in range(2):
                    logs.append(_sb_logs(z[h], mask))
                    later.append(_tri_dot(logs[h][1], tri_gt))
                out = []
                for h in range(2):
                    carry_r, acc = carry[h]
                    lb, lom = logs[h]
                    a = _sb_weights(lb, later[h], carry_r, mask)
                    row_sum = later[h][:, 0:1] + lom[:, 0:1]
                    out.append((carry_r + row_sum, acc + _dot(a.astype(BF16), vh[h], NN)))
                return tuple(out)

            c_last = pl.multiple_of((n_kb - 1) * TK, TK)
            start = (jnp.zeros((TQ, 1), F32), jnp.zeros((TQ, HEAD_DIM), F32))
            carry = k_block(c_last, (start, start), col + c_last < row + r0)

            carry = _while_alive(n_kb - 1, carry, lambda jj, cr: k_block(pl.multiple_of((n_kb - 2 - jj) * TK, TK), cr, None))
            for (_, acc), lanes in zip(carry, HEAD_LANES):
                o_ref[0, pl.ds(r0, TQ), lanes] = acc
                za = za_ref[0, pl.ds(r0, TQ), lanes]
                yp_ref[0, pl.ds(r0, TQ), lanes] = (acc * (za * _sigmoid(za))).astype(BF16)
            return 0

        lax.fori_loop(0, nq, q_block, 0)

    out_spec = pl.BlockSpec((1, s, LANES), lambda bi, hp: (bi, 0, hp))
    return pl.pallas_call(
        kern, name="attn_fwd",
        out_shape=(jax.ShapeDtypeStruct((b, s, D_MODEL), F32), jax.ShapeDtypeStruct((b, s, D_MODEL), BF16)),
        grid=(b, SB_HEADS // 2),
        in_specs=[pl.BlockSpec((1, s, HP_WIDTH), lambda bi, hp: (bi, 0, hp))],
        out_specs=(out_spec, out_spec),
        scratch_shapes=[pltpu.VMEM((2, s, HEAD_DIM), BF16)] * 3,
        compiler_params=_cparams("parallel", "parallel"),
    )(proj3)


def _attn_bwd(proj3, dyp3, o3, dproj3):
    b, s, _ = proj3.shape
    nq = s // TQ
    scale = HEAD_DIM ** -0.5

    def kern(x_ref, dyp_ref, o_ref, _, d_ref, qs, ks, vs, dos, dk_acc, dv_acc):
        _split_heads(qs, x_ref[0, :, Q_LANES], scale)
        _split_heads(ks, x_ref[0, :, K_LANES])
        _split_heads(vs, x_ref[0, :, V_LANES])
        dq_ref, dk_ref, dv_ref = (d_ref.at[:, :, lanes] for lanes in (Q_LANES, K_LANES, V_LANES))
        za = x_ref[0, :, ZA_LANES]
        sg = _sigmoid(za)
        dyp = dyp_ref[0]
        _split_heads(dos, dyp * (za * sg))
        d_ref[0, :, ZA_LANES] = (dyp * o_ref[0] * (sg * (1.0 + za * (1.0 - sg)))).astype(BF16)
        dk_acc[...] = jnp.zeros_like(dk_acc)
        dv_acc[...] = jnp.zeros_like(dv_acc)
        row = lax.broadcasted_iota(jnp.int32, (TQ, TK), 0)
        col = lax.broadcasted_iota(jnp.int32, (TQ, TK), 1)
        tri_gt = _tri(lambda j, sk: j > sk)
        tri_ge = _tri(lambda j, sk: j >= sk)

        def q_block(i, _):
            r0 = pl.multiple_of(i * TQ, TQ)
            n_kb = (r0 + TQ + TK - 1) // TK
            qh = [qs[h, pl.ds(r0, TQ), :] for h in range(2)]
            doh = [dos[h, pl.ds(r0, TQ), :] for h in range(2)]
            totals = [jnp.sum(doh[h].astype(F32) * o_ref[0, pl.ds(r0, TQ), lanes], axis=1, keepdims=True)
                      for h, lanes in enumerate(HEAD_LANES)]

            def k_block(c0, carry, mask):
                kh = [ks[h, pl.ds(c0, TK), :] for h in range(2)]
                vh = [vs[h, pl.ds(c0, TK), :] for h in range(2)]
                z = [_dot(qh[h], kh[h], NT) for h in range(2)]
                da = [_dot(doh[h], vh[h], NT) for h in range(2)]
                logs, later = [], []
                for h in range(2):
                    logs.append(_sb_logs(z[h], mask))
                    later.append(_tri_dot(logs[h][1], tri_gt))
                ab, g, suffix = [], [], []
                for h in range(2):
                    a = _sb_weights(logs[h][0], later[h], carry[h][0], mask)
                    ab.append(a.astype(BF16))
                    g.append(da[h] * ab[h].astype(F32))
                    suffix.append(_tri_dot(g[h], tri_ge))
                out = []
                for h in range(2):
                    carry_r, carry_g, dq = carry[h]
                    lb, lom = logs[h]
                    dz = g[h] - (g[h] + (totals[h] - carry_g) - suffix[h]) * jnp.exp(lb)
                    if mask is not None:
                        dz = jnp.where(mask, dz, 0.0)
                    dzb = dz.astype(BF16)
                    dk_acc[h, pl.ds(c0, TK), :] += _dot(dzb, qh[h], TN)
                    dv_acc[h, pl.ds(c0, TK), :] += _dot(ab[h], doh[h], TN)
                    out.append((carry_r + (later[h][:, 0:1] + lom[:, 0:1]), carry_g + suffix[h][:, 0:1],
                                dq + _dot(dzb, kh[h], NN)))
                return tuple(out)

            c_last = pl.multiple_of((n_kb - 1) * TK, TK)
            zero = jnp.zeros((TQ, 1), F32)
            start = (zero, zero, jnp.zeros((TQ, HEAD_DIM), F32))
            carry = k_block(c_last, (start, start), col + c_last < row + r0)

            carry = _while_alive(n_kb - 1, carry, lambda jj, cr: k_block(pl.multiple_of((n_kb - 2 - jj) * TK, TK), cr, None))
            for (_, _, dq), lanes in zip(carry, HEAD_LANES):
                dq_ref[0, pl.ds(r0, TQ), lanes] = (dq * scale).astype(BF16)
            return 0

        lax.fori_loop(0, nq, q_block, 0)

        for h, lanes in enumerate(HEAD_LANES):
            dk_ref[0, :, lanes] = dk_acc[h].astype(BF16)
            dv_ref[0, :, lanes] = dv_acc[h].astype(BF16)

    plain = pl.BlockSpec((1, s, LANES), lambda bi, hp: (bi, 0, hp))
    pair = pl.BlockSpec((1, s, HP_WIDTH), lambda bi, hp: (bi, 0, hp))
    return pl.pallas_call(
        kern, name="attn_bwd",
        out_shape=jax.ShapeDtypeStruct(dproj3.shape, dproj3.dtype),
        grid=(b, SB_HEADS // 2),
        in_specs=[pair, plain, plain, ANY],
        out_specs=pair,
        input_output_aliases={3: 0},
        scratch_shapes=[pltpu.VMEM((2, s, HEAD_DIM), BF16)] * 4 + [pltpu.VMEM((2, s, HEAD_DIM), F32)] * 2,
        compiler_params=_cparams("parallel", "parallel"),
    )(proj3, dyp3, o3, dproj3)


CONV_COLS = 256
HALO = 8


def _conv_pre(xp, w_ref, b_ref, r0):
    pre = b_ref[...] + w_ref[CONV_K - 1:CONV_K, :] * xp[pl.ds(HALO + r0, CHUNK), :]
    for kk in range(1, CONV_K):
        pre = pre + w_ref[CONV_K - 1 - kk:CONV_K - kk, :] * xp[pl.ds(HALO + r0 - kk, CHUNK), :]
    return pre


def _conv_fwd(proj3, conv_w, conv_b):
    b, s, _ = proj3.shape
    nc = s // CHUNK

    def kern(x_ref, w_ref, b_ref, o_ref, xp):
        xp[0:HALO, :] = jnp.zeros((HALO, CONV_COLS), F32)
        xp[HALO:, :] = x_ref[0]
        for ci in range(nc):
            pre = _conv_pre(xp, w_ref, b_ref, ci * CHUNK)
            o_ref[0, ci * CHUNK:(ci + 1) * CHUNK, :] = pre * _sigmoid(pre)

    return pl.pallas_call(
        kern, name="conv_fwd",
        out_shape=jax.ShapeDtypeStruct((b, s, CONV_DIM), F32),
        grid=(CONV_DIM // CONV_COLS, b),
        in_specs=[pl.BlockSpec((1, s, CONV_COLS), lambda j, bi: (bi, 0, XBC0 // CONV_COLS + j)),
                  pl.BlockSpec((CONV_K, CONV_COLS), lambda j, bi: (0, j)),
                  pl.BlockSpec((1, CONV_COLS), lambda j, bi: (0, j))],
        out_specs=pl.BlockSpec((1, s, CONV_COLS), lambda j, bi: (bi, 0, j)),
        scratch_shapes=[pltpu.VMEM((s + HALO, CONV_COLS), F32)],
        compiler_params=_cparams("parallel", "parallel"),
    )(proj3, conv_w, conv_b)


def _conv_bwd(dact, proj3, conv_w, conv_b, col0, name, dproj3):
    b, s, width = dact.shape
    nc = s // CHUNK
    j0 = col0 // CONV_COLS

    def kern(da_ref, x_ref, w_ref, b_ref, _, dx_ref, dw_ref, db_ref, xp, dp):
        @pl.when(pl.program_id(1) == 0)
        def _():
            dw_ref[...] = jnp.zeros_like(dw_ref)
            db_ref[...] = jnp.zeros_like(db_ref)

        xp[0:HALO, :] = jnp.zeros((HALO, CONV_COLS), F32)
        xp[HALO:, :] = x_ref[0]
        dp[s:, :] = jnp.zeros((HALO, CONV_COLS), F32)
        for ci in range(nc):
            r0 = ci * CHUNK
            pre = _conv_pre(xp, w_ref, b_ref, r0)
            sg = _sigmoid(pre)
            dpre = da_ref[0, r0:r0 + CHUNK, :] * (sg * (1.0 + pre * (1.0 - sg)))
            dp[r0:r0 + CHUNK, :] = dpre
            db_ref[...] += jnp.sum(dpre, axis=0, keepdims=True)
            for kk in range(CONV_K):
                tap = CONV_K - 1 - kk
                dw_ref[tap:tap + 1, :] += jnp.sum(dpre * xp[pl.ds(HALO + r0 - kk, CHUNK), :], axis=0, keepdims=True)
        for ci in range(nc):
            r0 = ci * CHUNK
            dx = w_ref[CONV_K - 1:CONV_K, :] * dp[pl.ds(r0, CHUNK), :]
            for kk in range(1, CONV_K):
                dx = dx + w_ref[CONV_K - 1 - kk:CONV_K - kk, :] * dp[pl.ds(r0 + kk, CHUNK), :]
            dx_ref[0, r0:r0 + CHUNK, :] = dx.astype(BF16)

    return pl.pallas_call(
        kern, name=name,
        out_shape=(jax.ShapeDtypeStruct(dproj3.shape, dproj3.dtype), jax.ShapeDtypeStruct((CONV_K, width), F32),
                   jax.ShapeDtypeStruct((1, width), F32)),
        grid=(width // CONV_COLS, b),
        in_specs=[pl.BlockSpec((1, s, CONV_COLS), lambda j, bi: (bi, 0, j)),
                  pl.BlockSpec((1, s, CONV_COLS), lambda j, bi: (bi, 0, XBC0 // CONV_COLS + j0 + j)),
                  pl.BlockSpec((CONV_K, CONV_COLS), lambda j, bi: (0, j0 + j)),
                  pl.BlockSpec((1, CONV_COLS), lambda j, bi: (0, j0 + j)), ANY],
        out_specs=(pl.BlockSpec((1, s, CONV_COLS), lambda j, bi: (bi, 0, XBC0 // CONV_COLS + j0 + j)),
                   pl.BlockSpec((CONV_K, CONV_COLS), lambda j, bi: (0, j)),
                   pl.BlockSpec((1, CONV_COLS), lambda j, bi: (0, j))),
        input_output_aliases={4: 0},
        scratch_shapes=[pltpu.VMEM((s + HALO, CONV_COLS), F32)] * 2,
        compiler_params=_cparams("parallel", "arbitrary"),
    )(dact, proj3, conv_w, conv_b, dproj3)


def _sel_dot(v, sel, left=False):
    hi = v.astype(BF16)
    rest = v - hi.astype(F32)
    mid = rest.astype(BF16)
    lo = (rest - mid.astype(F32)).astype(BF16)
    if left:
        return _dot(sel, hi, NN) + _dot(sel, mid, NN) + _dot(sel, lo, NN)
    return _dot(hi, sel, NN) + _dot(mid, sel, NN) + _dot(lo, sel, NN)


def _ssd_common(dtr_ref, dtb_ref, alog_ref):
    lane = lax.broadcasted_iota(jnp.int32, (CHUNK, LANES), 1)
    row = lax.broadcasted_iota(jnp.int32, (CHUNK, LANES), 0)
    head_lane = lane < HEADS_PER_GROUP
    pre = dtr_ref[0, 0] + dtb_ref[0]
    dt = jnp.where(head_lane, jnp.maximum(pre, 0.0) + jnp.log(1.0 + jnp.exp(-jnp.abs(pre))), 0.0)
    a = jnp.where(head_lane[0:1], -jnp.exp(alog_ref[0]), 0.0)
    tril = (row >= lane).astype(BF16)
    acs = _sel_dot(dt * a, tril, left=True)
    acs_t = acs.T
    er = lax.broadcasted_iota(jnp.int32, (LANES, GROUP_WIDTH), 0)
    ec = lax.broadcasted_iota(jnp.int32, (LANES, GROUP_WIDTH), 1)
    expand = ((ec // HEAD_DIM) == er).astype(BF16)
    tr = lax.broadcasted_iota(jnp.int32, (GROUP_WIDTH, LANES), 0)
    tc = lax.broadcasted_iota(jnp.int32, (GROUP_WIDTH, LANES), 1)
    reduce = ((tr // HEAD_DIM) == tc).astype(BF16)
    dt_x = _sel_dot(dt, expand)
    acs_x = _sel_dot(acs, expand)
    end_x = acs_x[CHUNK - 1:CHUNK, :]
    causal = row >= lane
    return dict(dt=dt, a=a, pre=pre, head_lane=head_lane, acs=acs, acs_t=acs_t, expand=expand, reduce=reduce,
                dt_x=dt_x, acs_x=acs_x, end_x=end_x, causal=causal, row=row, lane=lane)


def _ssd_decay(cm, h):
    seg = cm["acs"][:, h:h + 1] - cm["acs_t"][h:h + 1, :]
    return jnp.where(cm["causal"], jnp.exp(jnp.minimum(seg, 0.0)), 0.0)


def _ssd_fwd(xact, proj3, dtr_g, dtb_g, alog_g, dskip_x, snw):
    b, s, _ = xact.shape
    nc = s // CHUNK
    g4 = SSD_GROUPS

    def kern(xs_ref, bm_ref, cm_ref, zs_ref, dtr_ref, dtb_ref, alog_ref, dsk_ref, snw_ref,
             y_ref, yn_ref, hst_ref, h_sc):
        @pl.when(pl.program_id(2) == 0)
        def _():
            h_sc[...] = jnp.zeros_like(h_sc)

        cm = _ssd_common(dtr_ref, dtb_ref, alog_ref)
        x = xs_ref[0]
        bmb = bm_ref[0].astype(BF16)
        cmb = cm_ref[0].astype(BF16)
        h_in = h_sc[...]
        hst_ref[0, 0, 0] = h_in
        xdt = x * cm["dt_x"]
        xdtb = xdt.astype(BF16)
        cb = _dot(cmb, bmb, NT)
        y_off = _dot(cmb, h_in.astype(BF16), NN) * jnp.exp(cm["acs_x"])
        for h in range(HEADS_PER_GROUP):
            lanes = slice(h * HEAD_DIM, (h + 1) * HEAD_DIM)
            m = (cb * _ssd_decay(cm, h)).astype(BF16)
            y_ref[0, :, lanes] = _dot(m, xdtb[:, lanes], NN)
        y = y_ref[0] + y_off + x * dsk_ref[...]
        y_ref[0] = y
        w = (xdt * jnp.exp(cm["end_x"] - cm["acs_x"])).astype(BF16)
        h_sc[...] = h_in * jnp.exp(cm["end_x"]) + _dot(bmb, w, TN)
        zs = zs_ref[0]
        y2 = y * (zs * _sigmoid(zs))
        yn_ref[0] = (y2 * lax.rsqrt(jnp.mean(y2 * y2, axis=-1, keepdims=True) + EPS) * snw_ref[...]).astype(BF16)

    gw = GROUP_WIDTH
    small = pl.BlockSpec((1, 1, LANES), lambda gi, bi, ci: (gi, 0, 0))
    xblk = pl.BlockSpec((1, CHUNK, gw), lambda gi, bi, ci: (bi, ci, gi))
    return pl.pallas_call(
        kern, name="ssd_fwd",
        out_shape=(jax.ShapeDtypeStruct((b, s, SSD_WIDTH), F32), jax.ShapeDtypeStruct((b, s, SSD_WIDTH), BF16),
                   jax.ShapeDtypeStruct((b, nc, g4, SSD_STATE, gw), F32)),
        grid=(g4, b, nc),
        in_specs=[xblk,
                  pl.BlockSpec((1, CHUNK, LANES), lambda gi, bi, ci: (bi, ci, SSD_WIDTH // LANES + gi)),
                  pl.BlockSpec((1, CHUNK, LANES), lambda gi, bi, ci: (bi, ci, SSD_WIDTH // LANES + g4 + gi)),
                  pl.BlockSpec((1, CHUNK, gw), lambda gi, bi, ci: (bi, ci, ZS0 // gw + gi)),
                  pl.BlockSpec((1, 1, CHUNK, LANES), lambda gi, bi, ci: (bi, gi, ci, 0)),
                  small, small,
                  pl.BlockSpec((1, gw), lambda gi, bi, ci: (0, gi)),
                  pl.BlockSpec((1, gw), lambda gi, bi, ci: (0, gi))],
        out_specs=(xblk, xblk, pl.BlockSpec((1, 1, 1, SSD_STATE, gw), lambda gi, bi, ci: (bi, ci, gi, 0, 0))),
        scratch_shapes=[pltpu.VMEM((SSD_STATE, gw), F32)],
        compiler_params=_cparams("parallel", "parallel", "arbitrary"),
    )(xact, xact, xact, proj3, dtr_g, dtb_g, alog_g, dskip_x, snw)


def _ssd_bwd(dyn3, y3, xact, proj3, hst, dtr_g, dtb_g, alog_g, dskip_x, snw, dproj3):
    b, s, _ = xact.shape
    nc = s // CHUNK
    g4 = SSD_GROUPS
    gw = GROUP_WIDTH

    def kern(dyn_ref, y_ref, xs_ref, bm_ref, cm_ref, zs_ref, hst_ref, dtr_ref, dtb_ref, alog_ref, dsk_ref, snw_ref, _,
             dxs_ref, dbm_ref, dcm_ref, dzs_ref, ddtr_ref, dsnw_ref, dalog_ref, ddtb_ref, ddsk_ref, dh_sc):
        first = jnp.logical_and(pl.program_id(1) == 0, pl.program_id(2) == 0)

        @pl.when(first)
        def _():
            dsnw_ref[...] = jnp.zeros_like(dsnw_ref)
            dalog_ref[...] = jnp.zeros_like(dalog_ref)
            ddtb_ref[...] = jnp.zeros_like(ddtb_ref)
            ddsk_ref[...] = jnp.zeros_like(ddsk_ref)

        @pl.when(pl.program_id(2) == 0)
        def _():
            dh_sc[...] = jnp.zeros_like(dh_sc)

        cm = _ssd_common(dtr_ref, dtb_ref, alog_ref)
        row, lane = cm["row"], cm["lane"]
        y = y_ref[0]
        zs = zs_ref[0]
        sg = _sigmoid(zs)
        silu = zs * sg
        y2 = y * silu
        rstd = lax.rsqrt(jnp.mean(y2 * y2, axis=-1, keepdims=True) + EPS)
        y2h = y2 * rstd
        dyn = dyn_ref[0]
        dsnw_ref[0] += jnp.sum(dyn * y2h, axis=0, keepdims=True)
        gwv = dyn * snw_ref[...]
        dy2 = rstd * (gwv - y2h * jnp.mean(gwv * y2h, axis=-1, keepdims=True))
        dzs_ref[0] = (dy2 * y * (sg * (1.0 + zs * (1.0 - sg)))).astype(BF16)
        dy = dy2 * silu
        dyb = dy.astype(BF16)

        x = xs_ref[0]
        bmb = bm_ref[0].astype(BF16)
        cmb = cm_ref[0].astype(BF16)
        h_in = hst_ref[0, 0, 0]
        h_inb = h_in.astype(BF16)
        d_hn = dh_sc[...]
        d_hnb = d_hn.astype(BF16)
        xdt = x * cm["dt_x"]
        xdtb = xdt.astype(BF16)
        eacs = jnp.exp(cm["acs_x"])
        dte = jnp.exp(cm["end_x"] - cm["acs_x"])
        wb = (xdt * dte).astype(BF16)

        dsk_lanes = jnp.broadcast_to(jnp.sum(dy * x, axis=0, keepdims=True), (8, gw))
        ddsk_ref[0] += _sel_dot(dsk_lanes, cm["reduce"])[0:1, :]
        dyo = dy * eacs
        dyob = dyo.astype(BF16)
        dacs_x = dyo * _dot(cmb, h_inb, NN)
        dcm = _dot(dyob, h_inb, NT)
        dh_in = _dot(cmb, dyob, TN)
        dw = _dot(bmb, d_hnb, NN)
        dbm = _dot(wb, d_hnb, NT)
        dxdt = dw * dte
        e_l = dw * xdt * dte
        dacs_x = dacs_x - e_l
        dend_x = jnp.sum(e_l, axis=0, keepdims=True)
        chunk_decay = jnp.exp(cm["end_x"])
        dh_sc[...] = d_hn * chunk_decay + dh_in
        dend_x = dend_x + jnp.sum(d_hn * h_in, axis=0, keepdims=True) * chunk_decay
        last_row = lax.broadcasted_iota(jnp.int32, (CHUNK, gw), 0) == CHUNK - 1
        dacs_x = dacs_x + jnp.where(last_row, dend_x, 0.0)

        cb = _dot(cmb, bmb, NT)
        dcb = jnp.zeros((CHUNK, CHUNK), F32)
        dacs = jnp.zeros((CHUNK, LANES), F32)
        dacs_t = jnp.zeros((LANES, CHUNK), F32)
        for h in range(HEADS_PER_GROUP):
            lanes = slice(h * HEAD_DIM, (h + 1) * HEAD_DIM)
            decay = _ssd_decay(cm, h)
            m = cb * decay
            dm = _dot(dyb[:, lanes], xdtb[:, lanes], NT)
            dxs_ref[0, :, lanes] = _dot(m.astype(BF16), dyb[:, lanes], TN)
            dcb_h = dm * decay
            dcb = dcb + dcb_h
            n = dcb_h * cb
            dacs = dacs + jnp.where(lane == h, jnp.sum(n, axis=1, keepdims=True), 0.0)
            dacs_t = dacs_t + jnp.where(row == h, jnp.sum(n, axis=0, keepdims=True), 0.0)
        dcbb = dcb.astype(BF16)
        dcm_ref[0] = dcm + _dot(dcbb, bmb, NN)
        dbm_ref[0] = dbm + _dot(dcbb, cmb, TN)
        dxdt = dxdt + dxs_ref[0]
        dxs_ref[0] = dy * dsk_ref[...] + dxdt * cm["dt_x"]

        dacs = dacs - dacs_t.T + _sel_dot(dacs_x, cm["reduce"])
        ddt = _sel_dot(dxdt * x, cm["reduce"])
        triu = (row <= lane).astype(BF16)
        rc = _sel_dot(dacs, triu, left=True)
        ddt = ddt + cm["a"] * rc
        dalog_ref[0] += jnp.sum(cm["dt"] * rc, axis=0, keepdims=True) * cm["a"]
        ddtr = jnp.where(cm["head_lane"], ddt * _sigmoid(cm["pre"]), 0.0)
        ddtr_ref[0, 0] = ddtr
        ddtb_ref[0] += jnp.sum(ddtr, axis=0, keepdims=True)

    def rev(ci):
        return nc - 1 - ci

    small = pl.BlockSpec((1, 1, LANES), lambda gi, bi, ci: (gi, 0, 0))
    xblk = pl.BlockSpec((1, CHUNK, gw), lambda gi, bi, ci: (bi, rev(ci), gi))
    nblk = pl.BlockSpec((1, CHUNK, LANES), lambda gi, bi, ci: (bi, rev(ci), gi))
    gvec = pl.BlockSpec((1, gw), lambda gi, bi, ci: (0, gi))
    gacc = pl.BlockSpec((1, 1, gw), lambda gi, bi, ci: (gi, 0, 0))
    return pl.pallas_call(
        kern, name="ssd_bwd",
        out_shape=(jax.ShapeDtypeStruct((b, s, SSD_WIDTH), F32),
                   jax.ShapeDtypeStruct((b, s, g4 * SSD_STATE), F32),
                   jax.ShapeDtypeStruct((b, s, g4 * SSD_STATE), F32),
                   jax.ShapeDtypeStruct(dproj3.shape, dproj3.dtype),
                   jax.ShapeDtypeStruct((b, g4, s, LANES), F32),
                   jax.ShapeDtypeStruct((g4, 1, gw), F32),
                   jax.ShapeDtypeStruct((g4, 1, LANES), F32),
                   jax.ShapeDtypeStruct((g4, 1, LANES), F32),
                   jax.ShapeDtypeStruct((g4, 1, LANES), F32)),
        grid=(g4, b, nc),
        in_specs=[xblk, xblk, xblk,
                  pl.BlockSpec((1, CHUNK, LANES), lambda gi, bi, ci: (bi, rev(ci), SSD_WIDTH // LANES + gi)),
                  pl.BlockSpec((1, CHUNK, LANES), lambda gi, bi, ci: (bi, rev(ci), SSD_WIDTH // LANES + g4 + gi)),
                  pl.BlockSpec((1, CHUNK, gw), lambda gi, bi, ci: (bi, rev(ci), ZS0 // gw + gi)),
                  pl.BlockSpec((1, 1, 1, SSD_STATE, gw), lambda gi, bi, ci: (bi, rev(ci), gi, 0, 0)),
                  pl.BlockSpec((1, 1, CHUNK, LANES), lambda gi, bi, ci: (bi, gi, rev(ci), 0)),
                  small, small, gvec, gvec, ANY],
        out_specs=(xblk, nblk, nblk,
                   pl.BlockSpec((1, CHUNK, gw), lambda gi, bi, ci: (bi, rev(ci), ZS0 // gw + gi)),
                   pl.BlockSpec((1, 1, CHUNK, LANES), lambda gi, bi, ci: (bi, gi, rev(ci), 0)),
                   gacc, small, small, small),
        input_output_aliases={12: 3},
        scratch_shapes=[pltpu.VMEM((SSD_STATE, gw), F32)],
        compiler_params=_cparams("parallel", "arbitrary", "arbitrary"),
    )(dyn3, y3, xact, xact, xact, proj3, hst, dtr_g, dtb_g, alog_g, dskip_x, snw, dproj3)


def _adamw(w, g, m, v, name):
    r, c = w.shape
    tr = 128 if r % 128 == 0 else r

    def kern(w_ref, g_ref, m_ref, v_ref, d_ref, nm_ref, nv_ref):
        gv = g_ref[...]
        nm = ADAM_B1 * m_ref[...] + (1.0 - ADAM_B1) * gv
        nv = ADAM_B2 * v_ref[...] + (1.0 - ADAM_B2) * (gv * gv)
        m_hat = nm / (1.0 - ADAM_B1 ** ADAM_STEP)
        v_hat = nv / (1.0 - ADAM_B2 ** ADAM_STEP)
        d_ref[...] = -ADAM_LR * (m_hat / (jnp.sqrt(v_hat) + ADAM_EPS) + ADAM_WD * w_ref[...])
        nm_ref[...] = nm
        nv_ref[...] = nv

    blk = pl.BlockSpec((tr, c), lambda i: (i, 0))
    out = jax.ShapeDtypeStruct((r, c), F32)
    return pl.pallas_call(
        kern, name=name, out_shape=(out, out, out), grid=(r // tr,),
        in_specs=[blk] * 4, out_specs=(blk, blk, blk),
        compiler_params=_cparams("parallel"),
    )(w, g, m, v)


ANY = pl.BlockSpec(memory_space=pl.ANY)


def _position():
    return lax.axis_index("x"), lax.axis_index("y"), lax.axis_index("c")


def _other_chips(x, y):
    return [(1 - x, y), (x, 1 - y), (1 - x, 1 - y)]


def _dma_sems(n):
    return [pltpu.SemaphoreType.DMA((n,)), pltpu.SemaphoreType.DMA((n,))]


def _gather_weights(shards):
    n = len(shards)

    def body(*refs):
        p_refs, out_refs = refs[:n], refs[n:2 * n]
        send_sems, recv_sems = refs[2 * n:]
        x, y, c = _position()
        me = 2 * x + y
        chips = _other_chips(x, y)

        def slab(a, chip, hf):
            half = shards[a].shape[0] // 2
            return out_refs[a].at[chip, pl.ds(hf * half, half), :]

        def my_half(a):
            half = shards[a].shape[0] // 2
            return p_refs[a].at[pl.ds(c * half, half), :]

        def over_ici(a, j, chip_from):
            px, py = chips[j]
            return pltpu.make_async_remote_copy(
                src_ref=my_half(a), dst_ref=slab(a, chip_from, c),
                send_sem=send_sems.at[3 * a + j], recv_sem=recv_sems.at[3 * a + j],
                device_id=(px, py, c), device_id_type=MESH)

        def to_sibling(a, j, hf):
            px, py = chips[j]
            return pltpu.make_async_remote_copy(
                src_ref=slab(a, 2 * px + py, hf), dst_ref=slab(a, 2 * px + py, hf),
                send_sem=send_sems.at[3 * (n + a) + j], recv_sem=recv_sems.at[3 * (n + a) + j],
                device_id=(x, y, 1 - c), device_id_type=MESH)

        own = [pltpu.make_async_remote_copy(
            src_ref=p_refs[a], dst_ref=out_refs[a].at[me], send_sem=send_sems.at[6 * n + a], recv_sem=recv_sems.at[6 * n + a],
            device_id=(x, y, 1 - c), device_id_type=MESH) for a in range(n)]
        first = [over_ici(a, j, me) for a in range(n) for j in range(3)]
        for cp in first + own:
            cp.start()
        passed = []
        for a in range(n):
            for j, (px, py) in enumerate(chips):
                over_ici(a, j, 2 * px + py).wait_recv()
                passed.append(to_sibling(a, j, c))
                passed[-1].start()
        for a in range(n):
            for j in range(3):
                to_sibling(a, j, 1 - c).wait_recv()
        for cp in first + passed:
            cp.wait_send()
        for cp in own:
            cp.wait()

    return pl.pallas_call(
        body, name="gather_weights",
        out_shape=[jax.ShapeDtypeStruct((N_CHIPS, *v.shape), v.dtype) for v in shards],
        in_specs=[ANY] * n, out_specs=[ANY] * n,
        scratch_shapes=_dma_sems(7 * n),
    )(*shards)


def _swap_halves(parts):
    n = len(parts)

    def body(*refs):
        v_refs, out_refs = refs[:n], refs[n:2 * n]
        send_sems, recv_sems = refs[2 * n:]
        x, y, c = _position()
        copies = []
        for a in range(n):
            half = parts[a].shape[1] // 2
            copies.append(pltpu.make_async_remote_copy(
                src_ref=v_refs[a].at[:, pl.ds((1 - c) * half, half), :], dst_ref=out_refs[a],
                send_sem=send_sems.at[a], recv_sem=recv_sems.at[a], device_id=(x, y, 1 - c), device_id_type=MESH))
        for cp in copies:
            cp.start()
        for cp in copies:
            cp.wait()

    return pl.pallas_call(
        body, name="grad_swap_halves",
        out_shape=[jax.ShapeDtypeStruct((v.shape[0], v.shape[1] // 2, v.shape[2]), v.dtype) for v in parts],
        in_specs=[ANY] * n, out_specs=[ANY] * n,
        scratch_shapes=_dma_sems(n),
    )(*parts)


def _chip_all_to_all(parts):
    n = len(parts)

    def body(*refs):
        p_refs, out_refs = refs[:n], refs[n:2 * n]
        send_sems, recv_sems = refs[2 * n:]
        x, y, c = _position()
        chips = _other_chips(x, y)
        sends = [pltpu.make_async_remote_copy(
            src_ref=p_refs[a].at[2 * px + py], dst_ref=out_refs[a].at[j],
            send_sem=send_sems.at[3 * a + j], recv_sem=recv_sems.at[3 * a + j],
            device_id=(px, py, c), device_id_type=MESH) for a in range(n) for j, (px, py) in enumerate(chips)]
        for cp in sends:
            cp.start()
        for cp in sends:
            cp.wait()

    return pl.pallas_call(
        body, name="grad_all_to_all",
        out_shape=[jax.ShapeDtypeStruct((N_CHIPS - 1, *v.shape[1:]), v.dtype) for v in parts],
        in_specs=[ANY] * n, out_specs=[ANY] * n,
        scratch_shapes=_dma_sems(3 * n),
    )(*parts)


def _join_halves(wholes):
    n = len(wholes)

    def body(*refs):
        out_refs = refs[n:2 * n]
        send_sems, recv_sems = refs[2 * n:]
        x, y, c = _position()
        copies = []
        for a in range(n):
            half = wholes[a].shape[0] // 2
            rows = out_refs[a].at[pl.ds(c * half, half), :]
            copies.append(pltpu.make_async_remote_copy(
                src_ref=rows, dst_ref=rows, send_sem=send_sems.at[a], recv_sem=recv_sems.at[a],
                device_id=(x, y, 1 - c), device_id_type=MESH))
        for cp in copies:
            cp.start()
        for cp in copies:
            cp.wait()

    return pl.pallas_call(
        body, name="grad_join_halves",
        out_shape=[jax.ShapeDtypeStruct(v.shape, v.dtype) for v in wholes],
        in_specs=[ANY] * n, out_specs=[ANY] * n,
        input_output_aliases={a: a for a in range(n)},
        scratch_shapes=_dma_sems(n),
    )(*wholes)


ADD_ROWS = 128


def _add_halves(g, sw, place, name):
    n, rows, cols = g.shape
    half = rows // 2
    nb = half // ADD_ROWS

    def kern(p_ref, g_ref, s_ref, o_ref):
        o_ref[...] = (g_ref[...] + s_ref[...]).astype(BF16)

    blk = pl.BlockSpec((1, ADD_ROWS, cols), lambda j, i, p_ref: (j, i, 0))
    return pl.pallas_call(
        kern, name=name,
        out_shape=jax.ShapeDtypeStruct((n, half, cols), BF16),
        grid_spec=pltpu.PrefetchScalarGridSpec(
            num_scalar_prefetch=1, grid=(n, nb),
            in_specs=[pl.BlockSpec((1, ADD_ROWS, cols), lambda j, i, p_ref: (j, p_ref[0] * nb + i, 0)), blk],
            out_specs=blk),
        compiler_params=_cparams("parallel", "parallel"),
    )(place, g, sw)


def _sum_chips(own, rx, place, name):
    _, half, cols = rx.shape
    nb = half // ADD_ROWS

    def kern(p_ref, own_ref, r_ref, o_ref):
        total = own_ref[0].astype(F32)
        for j in range(N_CHIPS - 1):
            total = total + r_ref[j].astype(F32)
        o_ref[...] = total

    return pl.pallas_call(
        kern, name=name,
        out_shape=jax.ShapeDtypeStruct((2 * half, cols), F32),
        grid_spec=pltpu.PrefetchScalarGridSpec(
            num_scalar_prefetch=1, grid=(nb,),
            in_specs=[pl.BlockSpec((1, ADD_ROWS, cols), lambda i, p_ref: (p_ref[1], i, 0)),
                      pl.BlockSpec((N_CHIPS - 1, ADD_ROWS, cols), lambda i, p_ref: (0, i, 0))],
            out_specs=pl.BlockSpec((ADD_ROWS, cols), lambda i, p_ref: (p_ref[0] * nb + i, 0))),
        compiler_params=_cparams("parallel"),
    )(place, own, rx)


def _gather_small(v, reduce, name):
    rows = v.shape[0]

    def body(v_ref, out_ref, buf, send_sems, recv_sems):
        x, y, c = _position()
        me = 4 * x + 2 * y + c
        buf[me] = v_ref[...]
        peers = [(x ^ (k >> 2), y ^ ((k >> 1) & 1), c ^ (k & 1)) for k in range(1, 8)]
        copies = [pltpu.make_async_remote_copy(
            src_ref=v_ref, dst_ref=buf.at[me],
            send_sem=send_sems.at[k], recv_sem=recv_sems.at[k],
            device_id=peer, device_id_type=MESH) for k, peer in enumerate(peers)]
        for cp in copies:
            cp.start()
        for k, (px, py, pc) in enumerate(peers):
            pltpu.make_async_remote_copy(
                src_ref=v_ref, dst_ref=buf.at[4 * px + 2 * py + pc],
                send_sem=send_sems.at[k], recv_sem=recv_sems.at[k],
                device_id=(px, py, pc), device_id_type=MESH).wait_recv()
        for cp in copies:
            cp.wait_send()
        if reduce:
            total = buf[0]
            for d in range(1, 8):
                total = total + buf[d]
            out_ref[...] = total
        else:
            out_ref[...] = buf[...]

    vm = pl.BlockSpec(memory_space=pltpu.VMEM)
    return pl.pallas_call(
        body, name=name,
        out_shape=jax.ShapeDtypeStruct((rows, LANES) if reduce else (8, rows, LANES), F32),
        in_specs=[vm], out_specs=vm,
        scratch_shapes=[pltpu.VMEM((8, rows, LANES), F32), pltpu.SemaphoreType.DMA((7,)), pltpu.SemaphoreType.DMA((7,))],
    )(v)


def _pad_rows(a, rows):
    return jnp.pad(a, ((0, rows - a.shape[0]), (0, 0)))


def _lane_pad(v):
    n = v.shape[1]
    return jnp.pad(v, ((0, 0), (0, -n % LANES)))


def _gather_all(w_in, w_attn_out, w_ssm_out, w_o, conv_w):
    d = D_MODEL
    w_in_all, w_ao, w_so, w_oo = _gather_weights([a[0].astype(BF16) for a in (w_in, w_attn_out, w_ssm_out, w_o)])
    w_proj = _to_proj_layout(jnp.concatenate([w_in_all[q] for q in range(N_CHIPS)], axis=1))
    w_ao = w_ao.reshape(D_MODEL, d)
    w_so = w_so.reshape(SSD_WIDTH, d)
    w_oo = w_oo.reshape(D_MODEL, d)
    conv_rows = conv_w[0].size // LANES
    conv_all = _gather_small(conv_w[0].reshape(conv_rows, LANES), False, "gather_conv_w")
    conv_w_all = conv_all[0::2].reshape(N_CHIPS, CONV_K, CONV_DIM // N_CHIPS).transpose(1, 0, 2).reshape(CONV_K, CONV_DIM)

    return w_proj, w_ao, w_so, w_oo, conv_w_all


def _local_step(x, loss_target, norm_w, w_proj, conv_w_all, conv_b, dt_bias, a_log, d_skip, ssm_norm_w,
                w_ao, w_so, w_oo, final_norm_w):
    b, s, d = x.shape
    t = b * s
    g4, hg = SSD_GROUPS, HEADS_PER_GROUP
    dtb_g = _lane_pad(dt_bias.reshape(g4, hg)).reshape(g4, 1, LANES)
    alog_g = _lane_pad(a_log.reshape(g4, hg)).reshape(g4, 1, LANES)
    dskip_x = jnp.repeat(d_skip, HEAD_DIM, axis=1)
    fnw = final_norm_w.reshape(1, d)

    x2 = x.reshape(t, d)
    h = _rms_fwd(x2, norm_w)
    big_tm = min(t, 2048)
    proj = _matmul(h, w_proj, tm=big_tm, tn=1280, tk=1024, name="proj")
    proj3 = proj.reshape(b, s, NP)
    o3, yp3 = _attn_fwd(proj3)
    xact = _conv_fwd(proj3, conv_w_all, conv_b)
    dtr = proj3[:, :, DT0:DT0 + g4 * hg].reshape(b, s, g4, hg).transpose(0, 2, 1, 3)
    dtr_g = jnp.pad(dtr, ((0, 0), (0, 0), (0, 0), (0, LANES - hg)))
    y3, yn3, hst = _ssd_fwd(xact, proj3, dtr_g, dtb_g, alog_g, dskip_x, ssm_norm_w)
    yp = yp3.reshape(t, D_MODEL)
    yn = yn3.reshape(t, SSD_WIDTH)
    ya = _matmul(yp, w_ao, tm=512, tn=1024, tk=1024, name="attn_out")
    ys = _matmul(yn, w_so, tm=512, tn=1024, tk=2048, name="ssm_out")
    merged = _merge_fwd(proj, ya, ys)
    mo = _matmul(merged, w_oo, tm=512, tn=1024, tk=1024, name="out_proj")
    dout, doutb, loss_part, d_fnw = _final_fwd_bwd(x2, mo, loss_target.reshape(t, d), fnw)

    dmerged = _matmul(doutb, w_oo, tb=True, tm=512, tn=1024, tk=1024, name="d_merged")
    g_wo = _matmul(merged, doutb, ta=True, tm=512, tn=1024, tk=1024, name="g_w_o")
    dya, dys, dproj = _merge_bwd(dmerged, proj, ya, ys)
    dyp = _matmul(dya, w_ao, tb=True, tm=512, tn=1024, tk=1024, name="d_attn_pre")
    g_wao = _matmul(yp, dya, ta=True, tm=512, tn=1024, tk=1024, name="g_w_attn_out")
    dyn = _matmul(dys, w_so, tb=True, tm=1024, tn=2048, tk=1024, name="d_ssm_norm")
    g_wso = _matmul(yn, dys, ta=True, tm=1024, tn=1024, tk=1024, name="g_w_ssm_out")
    dproj3 = _attn_bwd(proj3, dyp.reshape(b, s, D_MODEL), o3, dproj.reshape(b, s, NP))
    (dxs, dbm, dcm, dproj3, ddtr_g, d_snw_g, d_alog_g, d_dtb_g, d_dsk_g) = _ssd_bwd(
        dyn.reshape(b, s, SSD_WIDTH), y3, xact, proj3, hst, dtr_g, dtb_g, alog_g, dskip_x, ssm_norm_w, dproj3)
    dproj3, g_cw_xs, g_cb_xs = _conv_bwd(dxs, proj3, conv_w_all, conv_b, 0, "conv_bwd_x", dproj3)
    dproj3, g_cw_bm, g_cb_bm = _conv_bwd(dbm, proj3, conv_w_all, conv_b, SSD_WIDTH, "conv_bwd_b", dproj3)
    dproj3, g_cw_cm, g_cb_cm = _conv_bwd(dcm, proj3, conv_w_all, conv_b, SSD_WIDTH + g4 * SSD_STATE, "conv_bwd_c", dproj3)
    ddt = ddtr_g[:, :, :, :hg].transpose(0, 2, 1, 3).reshape(b, s, g4 * hg).astype(BF16)
    ddt = jnp.pad(ddt, ((0, 0), (0, 0), (0, DT_PAD - g4 * hg)))
    dproj = lax.dynamic_update_slice(dproj3, ddt, (0, 0, DT0)).reshape(t, NP)
    g_wproj = _matmul(h, dproj, ta=True, tm=1024, tn=1280, tk=1024, name="g_w_in")
    dh = _matmul(dproj, w_proj, tb=True, tm=big_tm, tn=1024, tk=1280, name="d_h")
    grad_x, d_nw = _rms_bwd(dh, x2, norm_w, dout)
    g_cw = jnp.concatenate([g_cw_xs, g_cw_bm, g_cw_cm], axis=1)
    g_cb = jnp.concatenate([g_cb_xs, g_cb_bm, g_cb_cm], axis=1)
    return (loss_part, grad_x, d_nw, g_wproj, g_cw, g_cb, d_dtb_g, d_alog_g, d_dsk_g, d_snw_g, g_wao, g_wso, g_wo, d_fnw)


def kernel(x, norm_w, w_in, conv_w, conv_b, dt_bias, a_log, d_skip, ssm_norm_w, w_attn_out, w_ssm_out, w_o, final_norm_w, loss_target, m_norm_w, m_w_in, m_conv_w, m_conv_b, m_dt_bias, m_a_log, m_d_skip, m_ssm_norm_w, m_w_attn_out, m_w_ssm_out, m_w_o, m_final_norm_w, v_norm_w, v_w_in, v_conv_w, v_conv_b, v_dt_bias, v_a_log, v_d_skip, v_ssm_norm_w, v_w_attn_out, v_w_ssm_out, v_w_o, v_final_norm_w):
    b, s, d = x.shape
    core = lax.axis_index("c")
    g4, hg = SSD_GROUPS, HEADS_PER_GROUP
    shard_cols = w_in.shape[2]
    w_proj, w_ao, w_so, w_oo, conv_w_all = _gather_all(w_in, w_attn_out, w_ssm_out, w_o, conv_w)
    (loss_part, grad_x, d_nw, g_wproj, g_cw, g_cb, d_dtb_g, d_alog_g, d_dsk_g, d_snw_g, g_wao, g_wso, g_wo, d_fnw) = _local_step(
        x, loss_target, norm_w, w_proj, conv_w_all, conv_b, dt_bias, a_log, d_skip, ssm_norm_w, w_ao, w_so, w_oo, final_norm_w)

    g_win = _from_proj_layout(g_wproj)
    g_win_chips = jnp.stack([g_win[:, q * shard_cols:(q + 1) * shard_cols] for q in range(N_CHIPS)])
    g_out_chips = jnp.concatenate([g.reshape(N_CHIPS, -1, d) for g in (g_wao, g_wso, g_wo)], axis=1)
    parts = [g_win_chips, g_out_chips]
    chip = 2 * lax.axis_index("x") + lax.axis_index("y")
    place = jnp.stack([core, chip]).astype(jnp.int32)
    from_sibling = _swap_halves(parts)
    chip_sums = [_add_halves(p, f, place, "grad_add_halves_%d" % i) for i, (p, f) in enumerate(zip(parts, from_sibling))]
    from_chips = _chip_all_to_all(chip_sums)
    wholes = [_sum_chips(o, r, place, "grad_sum_chips_%d" % i) for i, (o, r) in enumerate(zip(chip_sums, from_chips))]
    g_w_in, g_out = _join_halves(wholes)

    small = jnp.concatenate([
        loss_part, d_nw, g_cb, _lane_pad(d_dtb_g[:, 0, :hg].reshape(1, -1)), _lane_pad(d_alog_g[:, 0, :hg].reshape(1, -1)),
        _lane_pad(d_dsk_g[:, 0, :hg].reshape(1, -1)),
        d_snw_g.reshape(1, -1), d_fnw, g_cw.reshape(1, -1)], axis=1)
    small_rows = small.shape[1] // LANES
    reduced = _gather_small(_pad_rows(small.reshape(small_rows, LANES), -(-small_rows // 8) * 8), True, "reduce_small")
    flat = reduced.reshape(-1)

    def take(start, n):
        return flat[start:start + n].reshape(1, n)

    loss = flat[0]
    pos = LANES
    g_norm_w = take(pos, d); pos += d
    g_conv_b = take(pos, CONV_DIM); pos += CONV_DIM
    g_dt_bias = take(pos, g4 * hg); pos += LANES
    g_a_log = take(pos, g4 * hg); pos += LANES
    g_d_skip = take(pos, g4 * hg); pos += LANES
    g_ssm_norm_w = take(pos, SSD_WIDTH); pos += SSD_WIDTH
    g_final_norm_w = take(pos, d); pos += d
    conv_cols = CONV_DIM // N_CHIPS
    g_conv_w = lax.dynamic_slice_in_dim(flat[pos:pos + CONV_K * CONV_DIM].reshape(CONV_K, CONV_DIM), chip * conv_cols, conv_cols, axis=1)

    rows_ao, rows_so = D_MODEL // N_CHIPS, SSD_WIDTH // N_CHIPS
    g_w_attn_out = g_out[:rows_ao]
    g_w_ssm_out = g_out[rows_ao:rows_ao + rows_so]
    g_w_o = g_out[rows_ao + rows_so:]

    names = ["norm_w", "w_in", "conv_w", "conv_b", "dt_bias", "a_log", "d_skip", "ssm_norm_w",
             "w_attn_out", "w_ssm_out", "w_o", "final_norm_w"]
    weights = [norm_w, w_in, conv_w, conv_b, dt_bias, a_log, d_skip, ssm_norm_w, w_attn_out, w_ssm_out, w_o, final_norm_w]
    grads = [g_norm_w, g_w_in, g_conv_w, g_conv_b, g_dt_bias, g_a_log, g_d_skip, g_ssm_norm_w,
             g_w_attn_out, g_w_ssm_out, g_w_o, g_final_norm_w]
    ms = [m_norm_w, m_w_in, m_conv_w, m_conv_b, m_dt_bias, m_a_log, m_d_skip, m_ssm_norm_w,
          m_w_attn_out, m_w_ssm_out, m_w_o, m_final_norm_w]
    vs = [v_norm_w, v_w_in, v_conv_w, v_conv_b, v_dt_bias, v_a_log, v_d_skip, v_ssm_norm_w,
          v_w_attn_out, v_w_ssm_out, v_w_o, v_final_norm_w]
    out_g, out_d, out_m, out_v = [], [], [], []
    for name, w, g, m, v in zip(names, weights, grads, ms, vs):
        shape2 = g.shape
        dlt, nm, nv = _adamw(w.reshape(shape2), g, m.reshape(shape2), v.reshape(shape2), "adamw_" + name)
        out_g.append(g.reshape(w.shape))
        out_d.append(dlt.reshape(w.shape))
        out_m.append(nm.reshape(w.shape))
        out_v.append(nv.reshape(w.shape))

    return (loss, grad_x.reshape(b, s, d), *out_g, *out_d, *out_m, *out_v)
```

```python
import jax
import jax.numpy as jnp
from jax import lax
from jax.experimental import pallas as pl
from jax.experimental.pallas import tpu as pltpu

F32 = jnp.float32
BF16 = jnp.bfloat16
MESH = pl.DeviceIdType.MESH

D_MODEL = 1024
SB_HEADS = 16
HEAD_DIM = 64
SSD_WIDTH = 2048
SSD_GROUPS = 4
GROUP_WIDTH = SSD_WIDTH // SSD_GROUPS
HEADS_PER_GROUP = 8
SSD_STATE = 128
CHUNK = 128
CONV_K = 4
CONV_DIM = 3072
D_PROJ = 11296
EPS = 1e-6
ADAM_LR, ADAM_B1, ADAM_B2, ADAM_EPS, ADAM_WD, ADAM_STEP = 0.001, 0.9, 0.999, 1e-08, 0.01, 10

LANES = 128
HP_WIDTH = 4 * LANES
ZS0, GATE0, XBC0, DT0 = 4096, 6144, 8192, 11264
DT_PAD = 256
NP = DT0 + DT_PAD
N_CHIPS = 4
VMEM_LIMIT = 56 * 1024 * 1024


N_HP = SB_HEADS // 2
W_ZS0, W_XBC0, W_DT0, W_GATE0 = 4096, 6144, 9216, 9248


def _to_proj_layout(w):
    d = w.shape[0]
    pairs = w[:, :W_ZS0].reshape(d, 4, N_HP, LANES).transpose(0, 2, 1, 3).reshape(d, W_ZS0)
    return jnp.concatenate([pairs, w[:, W_ZS0:W_XBC0], w[:, W_GATE0:], w[:, W_XBC0:W_DT0], w[:, W_DT0:W_GATE0],
                            jnp.zeros((d, NP - D_PROJ), w.dtype)], axis=1)


def _from_proj_layout(g):
    d = g.shape[0]
    qkvz = g[:, :ZS0].reshape(d, N_HP, 4, LANES).transpose(0, 2, 1, 3).reshape(d, ZS0)
    return jnp.concatenate([qkvz, g[:, ZS0:GATE0], g[:, XBC0:DT0], g[:, DT0:DT0 + W_GATE0 - W_DT0], g[:, GATE0:XBC0]], axis=1)


def _cparams(*sem):
    return pltpu.CompilerParams(dimension_semantics=sem or None, vmem_limit_bytes=VMEM_LIMIT)


def _sigmoid(z):
    return 1.0 / (1.0 + jnp.exp(-z))


def _dot(a, b, dims, precision=None):
    return lax.dot_general(a, b, (dims, ((), ())), preferred_element_type=F32, precision=precision)


NN = ((1,), (0,))
NT = ((1,), (1,))
TN = ((0,), (0,))


def _matmul(a, b, *, ta=False, tb=False, out_dtype=F32, tm, tn, tk, name):
    m, k = (a.shape[1], a.shape[0]) if ta else a.shape
    n = b.shape[0] if tb else b.shape[1]
    assert m % tm == 0 and n % tn == 0 and k % tk == 0, (name, m, n, k)
    nk = k // tk
    use_scratch = out_dtype != F32
    dims = ((0,) if ta else (1,), (1,) if tb else (0,))

    def kern(a_ref, b_ref, o_ref, *scratch):
        acc = scratch[0] if use_scratch else o_ref
        kk = pl.program_id(2)

        @pl.when(kk == 0)
        def _():
            acc[...] = jnp.zeros_like(acc)

        acc[...] += _dot(a_ref[...], b_ref[...], dims)
        if use_scratch:
            @pl.when(kk == nk - 1)
            def _():
                o_ref[...] = acc[...].astype(out_dtype)

    a_spec = pl.BlockSpec((tk, tm), lambda i, j, q: (q, i)) if ta else pl.BlockSpec((tm, tk), lambda i, j, q: (i, q))
    b_spec = pl.BlockSpec((tn, tk), lambda i, j, q: (j, q)) if tb else pl.BlockSpec((tk, tn), lambda i, j, q: (q, j))
    return pl.pallas_call(
        kern, name=name,
        out_shape=jax.ShapeDtypeStruct((m, n), out_dtype),
        grid=(m // tm, n // tn, nk),
        in_specs=[a_spec, b_spec],
        out_specs=pl.BlockSpec((tm, tn), lambda i, j, q: (i, j)),
        scratch_shapes=[pltpu.VMEM((tm, tn), F32)] if use_scratch else [],
        compiler_params=_cparams("parallel", "parallel", "arbitrary"),
    )(a, b)


ROWS = 256


def _rms_fwd(x2, w):
    t, d = x2.shape

    def kern(x_ref, w_ref, h_ref):
        x = x_ref[...]
        r = lax.rsqrt(jnp.mean(x * x, axis=-1, keepdims=True) + EPS)
        h_ref[...] = (x * r * w_ref[...]).astype(BF16)

    return pl.pallas_call(
        kern, name="rms_fwd",
        out_shape=jax.ShapeDtypeStruct((t, d), BF16),
        grid=(t // ROWS,),
        in_specs=[pl.BlockSpec((ROWS, d), lambda i: (i, 0)), pl.BlockSpec((1, d), lambda i: (0, 0))],
        out_specs=pl.BlockSpec((ROWS, d), lambda i: (i, 0)),
        compiler_params=_cparams("parallel"),
    )(x2, w)


def _rms_bwd(dh, x2, w, dout):
    t, d = x2.shape

    def kern(dh_ref, x_ref, w_ref, dout_ref, gx_ref, dw_ref):
        @pl.when(pl.program_id(0) == 0)
        def _():
            dw_ref[...] = jnp.zeros_like(dw_ref)

        x = x_ref[...]
        r = lax.rsqrt(jnp.mean(x * x, axis=-1, keepdims=True) + EPS)
        xh = x * r
        g = dh_ref[...]
        dw_ref[...] += jnp.sum(g * xh, axis=0, keepdims=True)
        gw = g * w_ref[...]
        gx_ref[...] = dout_ref[...] + r * (gw - xh * jnp.mean(gw * xh, axis=-1, keepdims=True))

    row = pl.BlockSpec((ROWS, d), lambda i: (i, 0))
    vec = pl.BlockSpec((1, d), lambda i: (0, 0))
    return pl.pallas_call(
        kern, name="rms_bwd",
        out_shape=(jax.ShapeDtypeStruct((t, d), F32), jax.ShapeDtypeStruct((1, d), F32)),
        grid=(t // ROWS,),
        in_specs=[row, row, vec, row],
        out_specs=(row, vec),
        compiler_params=_cparams("arbitrary"),
    )(dh, x2, w, dout)


def _final_fwd_bwd(x2, mo, target, w):
    t, d = x2.shape

    def kern(x_ref, mo_ref, t_ref, w_ref, dout_ref, doutb_ref, loss_ref, dw_ref):
        @pl.when(pl.program_id(0) == 0)
        def _():
            loss_ref[...] = jnp.zeros_like(loss_ref)
            dw_ref[...] = jnp.zeros_like(dw_ref)

        u = x_ref[...] + mo_ref[...]
        r = lax.rsqrt(jnp.mean(u * u, axis=-1, keepdims=True) + EPS)
        uh = u * r
        wv = w_ref[...]
        err = uh * wv - t_ref[...]
        loss_ref[...] += (0.5 / d) * jnp.sum(err * err)
        dy = err * (1.0 / d)
        dw_ref[...] += jnp.sum(dy * uh, axis=0, keepdims=True)
        gw = dy * wv
        du = r * (gw - uh * jnp.mean(gw * uh, axis=-1, keepdims=True))
        dout_ref[...] = du
        doutb_ref[...] = du.astype(BF16)

    row = pl.BlockSpec((ROWS, d), lambda i: (i, 0))
    vec = pl.BlockSpec((1, d), lambda i: (0, 0))
    return pl.pallas_call(
        kern, name="final_fwd_bwd",
        out_shape=(jax.ShapeDtypeStruct((t, d), F32), jax.ShapeDtypeStruct((t, d), BF16),
                   jax.ShapeDtypeStruct((1, LANES), F32), jax.ShapeDtypeStruct((1, d), F32)),
        grid=(t // ROWS,),
        in_specs=[row, row, row, vec],
        out_specs=(row, row, pl.BlockSpec((1, LANES), lambda i: (0, 0)), vec),
        compiler_params=_cparams("arbitrary"),
    )(x2, mo, target, w)


def _merge_fwd(proj2, ya, ys):
    t = ya.shape[0]
    gblk = GATE0 // D_MODEL

    def kern(ga_ref, gs_ref, ya_ref, ys_ref, o_ref):
        o_ref[...] = (_sigmoid(ga_ref[...]) * ya_ref[...] + _sigmoid(gs_ref[...]) * ys_ref[...]).astype(BF16)

    row = pl.BlockSpec((ROWS, D_MODEL), lambda i: (i, 0))
    return pl.pallas_call(
        kern, name="merge_fwd",
        out_shape=jax.ShapeDtypeStruct((t, D_MODEL), BF16),
        grid=(t // ROWS,),
        in_specs=[pl.BlockSpec((ROWS, D_MODEL), lambda i: (i, gblk)),
                  pl.BlockSpec((ROWS, D_MODEL), lambda i: (i, gblk + 1)), row, row],
        out_specs=row,
        compiler_params=_cparams("parallel"),
    )(proj2, proj2, ya, ys)


def _merge_bwd(dm, proj2, ya, ys):
    t = ya.shape[0]
    gblk = GATE0 // D_MODEL

    def kern(dm_ref, ga_ref, gs_ref, ya_ref, ys_ref, dya_ref, dys_ref, dg_ref):
        g = dm_ref[...]
        sa = _sigmoid(ga_ref[...])
        ss = _sigmoid(gs_ref[...])
        dya_ref[...] = (g * sa).astype(BF16)
        dys_ref[...] = (g * ss).astype(BF16)
        dg_ref[:, :D_MODEL] = (g * ya_ref[...] * sa * (1.0 - sa)).astype(BF16)
        dg_ref[:, D_MODEL:] = (g * ys_ref[...] * ss * (1.0 - ss)).astype(BF16)

    row = pl.BlockSpec((ROWS, D_MODEL), lambda i: (i, 0))
    return pl.pallas_call(
        kern, name="merge_bwd",
        out_shape=(jax.ShapeDtypeStruct((t, D_MODEL), BF16), jax.ShapeDtypeStruct((t, D_MODEL), BF16),
                   jax.ShapeDtypeStruct((t, NP), BF16)),
        grid=(t // ROWS,),
        in_specs=[row, pl.BlockSpec((ROWS, D_MODEL), lambda i: (i, gblk)),
                  pl.BlockSpec((ROWS, D_MODEL), lambda i: (i, gblk + 1)), row, row],
        out_specs=(row, row, pl.BlockSpec((ROWS, 2 * D_MODEL), lambda i: (i, GATE0 // (2 * D_MODEL)))),
        compiler_params=_cparams("parallel"),
    )(dm, proj2, proj2, ya, ys)


TQ = 256
TK = 256
HEAD_LANES = (slice(0, HEAD_DIM), slice(HEAD_DIM, 2 * HEAD_DIM))


def _tri(pred):
    r = lax.broadcasted_iota(jnp.int32, (TK, TK), 0)
    c = lax.broadcasted_iota(jnp.int32, (TK, TK), 1)
    return pred(r, c).astype(BF16)


def _split_bf16(v):
    hi = v.astype(BF16)
    lo = (v - hi.astype(F32)).astype(BF16)
    return hi, lo


def _tri_dot(v, tri):
    hi, lo = _split_bf16(v)
    return _dot(hi, tri, NN) + _dot(lo, tri, NN)


def _sb_logs(z, mask):
    l1p = jnp.log(1.0 + jnp.exp(-jnp.abs(z)))
    lb = jnp.minimum(z, 0.0) - l1p
    lom = -jnp.maximum(z, 0.0) - l1p
    if mask is not None:
        lom = jnp.where(mask, lom, 0.0)
    return lb, lom


def _sb_weights(lb, later, carry_r, mask):
    a = jnp.exp(lb + (later + carry_r))
    if mask is not None:
        a = jnp.where(mask, a, 0.0)
    return a


DEAD = -104.0


def _while_alive(n, carry, step):
    def alive(cr):
        return jnp.max(jnp.maximum(cr[0][0], cr[1][0])) > DEAD

    def cond(state):
        jj, go, _ = state
        return jnp.logical_and(jj < n, go)

    def body(state):
        jj, _, cr = state
        cr = step(jj, cr)
        return jj + 1, alive(cr), cr

    return lax.while_loop(cond, body, (jnp.int32(0), alive(carry), carry))[2]


Q_LANES, K_LANES, V_LANES, ZA_LANES = (slice(i * LANES, (i + 1) * LANES) for i in range(4))


def _split_heads(dst, src, scale=None):
    for h, lanes in enumerate(HEAD_LANES):
        v = src[:, lanes]
        dst[h] = (v if scale is None else v * scale).astype(BF16)


def _attn_fwd(proj3):
    b, s, _ = proj3.shape
    nq = s // TQ
    scale = HEAD_DIM ** -0.5

    def kern(x_ref, o_ref, yp_ref, qs, ks, vs):
        _split_heads(qs, x_ref[0, :, Q_LANES], scale)
        _split_heads(ks, x_ref[0, :, K_LANES])
        _split_heads(vs, x_ref[0, :, V_LANES])
        za_ref = x_ref.at[:, :, ZA_LANES]
        row = lax.broadcasted_iota(jnp.int32, (TQ, TK), 0)
        col = lax.broadcasted_iota(jnp.int32, (TQ, TK), 1)
        tri_gt = _tri(lambda j, sk: j > sk)

        def q_block(i, _):
            r0 = pl.multiple_of(i * TQ, TQ)
            n_kb = (r0 + TQ + TK - 1) // TK
            qh = [qs[h, pl.ds(r0, TQ), :] for h in range(2)]

            def k_block(c0, carry, mask):
                kh = [ks[h, pl.ds(c0, TK), :] for h in range(2)]
                vh = [vs[h, pl.ds(c0, TK), :] for h in range(2)]
                z = [_dot(qh[h], kh[h], NT) for h in range(2)]
                logs, later = [], []
                for h in range(2):
                    logs.append(_sb_logs(z[h], mask))
                    later.append(_tri_dot(logs[h][1], tri_gt))
                out = []
                for h in range(2):
                    carry_r, acc = carry[h]
                    lb, lom = logs[h]
                    a = _sb_weights(lb, later[h], carry_r, mask)
                    row_sum = later[h][:, 0:1] + lom[:, 0:1]
                    out.append((carry_r + row_sum, acc + _dot(a.astype(BF16), vh[h], NN)))
                return tuple(out)

            c_last = pl.multiple_of((n_kb - 1) * TK, TK)
            start = (jnp.zeros((TQ, 1), F32), jnp.zeros((TQ, HEAD_DIM), F32))
            carry = k_block(c_last, (start, start), col + c_last < row + r0)

            carry = _while_alive(n_kb - 1, carry, lambda jj, cr: k_block(pl.multiple_of((n_kb - 2 - jj) * TK, TK), cr, None))
            for (_, acc), lanes in zip(carry, HEAD_LANES):
                o_ref[0, pl.ds(r0, TQ), lanes] = acc
                za = za_ref[0, pl.ds(r0, TQ), lanes]
                yp_ref[0, pl.ds(r0, TQ), lanes] = (acc * (za * _sigmoid(za))).astype(BF16)
            return 0

        lax.fori_loop(0, nq, q_block, 0)

    out_spec = pl.BlockSpec((1, s, LANES), lambda bi, hp: (bi, 0, hp))
    return pl.pallas_call(
        kern, name="attn_fwd",
        out_shape=(jax.ShapeDtypeStruct((b, s, D_MODEL), F32), jax.ShapeDtypeStruct((b, s, D_MODEL), BF16)),
        grid=(b, SB_HEADS // 2),
        in_specs=[pl.BlockSpec((1, s, HP_WIDTH), lambda bi, hp: (bi, 0, hp))],
        out_specs=(out_spec, out_spec),
        scratch_shapes=[pltpu.VMEM((2, s, HEAD_DIM), BF16)] * 3,
        compiler_params=_cparams("parallel", "parallel"),
    )(proj3)


def _attn_bwd(proj3, dyp3, o3, dproj3):
    b, s, _ = proj3.shape
    nq = s // TQ
    scale = HEAD_DIM ** -0.5

    def kern(x_ref, dyp_ref, o_ref, _, d_ref, qs, ks, vs, dos, dk_acc, dv_acc):
        _split_heads(qs, x_ref[0, :, Q_LANES], scale)
        _split_heads(ks, x_ref[0, :, K_LANES])
        _split_heads(vs, x_ref[0, :, V_LANES])
        dq_ref, dk_ref, dv_ref = (d_ref.at[:, :, lanes] for lanes in (Q_LANES, K_LANES, V_LANES))
        za = x_ref[0, :, ZA_LANES]
        sg = _sigmoid(za)
        dyp = dyp_ref[0]
        _split_heads(dos, dyp * (za * sg))
        d_ref[0, :, ZA_LANES] = (dyp * o_ref[0] * (sg * (1.0 + za * (1.0 - sg)))).astype(BF16)
        dk_acc[...] = jnp.zeros_like(dk_acc)
        dv_acc[...] = jnp.zeros_like(dv_acc)
        row = lax.broadcasted_iota(jnp.int32, (TQ, TK), 0)
        col = lax.broadcasted_iota(jnp.int32, (TQ, TK), 1)
        tri_gt = _tri(lambda j, sk: j > sk)
        tri_ge = _tri(lambda j, sk: j >= sk)

        def q_block(i, _):
            r0 = pl.multiple_of(i * TQ, TQ)
            n_kb = (r0 + TQ + TK - 1) // TK
            qh = [qs[h, pl.ds(r0, TQ), :] for h in range(2)]
            doh = [dos[h, pl.ds(r0, TQ), :] for h in range(2)]
            totals = [jnp.sum(doh[h].astype(F32) * o_ref[0, pl.ds(r0, TQ), lanes], axis=1, keepdims=True)
                      for h, lanes in enumerate(HEAD_LANES)]

            def k_block(c0, carry, mask):
                kh = [ks[h, pl.ds(c0, TK), :] for h in range(2)]
                vh = [vs[h, pl.ds(c0, TK), :] for h in range(2)]
                z = [_dot(qh[h], kh[h], NT) for h in range(2)]
                da = [_dot(doh[h], vh[h], NT) for h in range(2)]
                logs, later = [], []
                for h in range(2):
                    logs.append(_sb_logs(z[h], mask))
                    later.append(_tri_dot(logs[h][1], tri_gt))
                ab, g, suffix = [], [], []
                for h in range(2):
                    a = _sb_weights(logs[h][0], later[h], carry[h][0], mask)
                    ab.append(a.astype(BF16))
                    g.append(da[h] * ab[h].astype(F32))
                    suffix.append(_tri_dot(g[h], tri_ge))
                out = []
                for h in range(2):
                    carry_r, carry_g, dq = carry[h]
                    lb, lom = logs[h]
                    dz = g[h] - (g[h] + (totals[h] - carry_g) - suffix[h]) * jnp.exp(lb)
                    if mask is not None:
                        dz = jnp.where(mask, dz, 0.0)
                    dzb = dz.astype(BF16)
                    dk_acc[h, pl.ds(c0, TK), :] += _dot(dzb, qh[h], TN)
                    dv_acc[h, pl.ds(c0, TK), :] += _dot(ab[h], doh[h], TN)
                    out.append((carry_r + (later[h][:, 0:1] + lom[:, 0:1]), carry_g + suffix[h][:, 0:1],
                                dq + _dot(dzb, kh[h], NN)))
                return tuple(out)

            c_last = pl.multiple_of((n_kb - 1) * TK, TK)
            zero = jnp.zeros((TQ, 1), F32)
            start = (zero, zero, jnp.zeros((TQ, HEAD_DIM), F32))
            carry = k_block(c_last, (start, start), col + c_last < row + r0)

            carry = _while_alive(n_kb - 1, carry, lambda jj, cr: k_block(pl.multiple_of((n_kb - 2 - jj) * TK, TK), cr, None))
            for (_, _, dq), lanes in zip(carry, HEAD_LANES):
                dq_ref[0, pl.ds(r0, TQ), lanes] = (dq * scale).astype(BF16)
            return 0

        lax.fori_loop(0, nq, q_block, 0)

        for h, lanes in enumerate(HEAD_LANES):
            dk_ref[0, :, lanes] = dk_acc[h].astype(BF16)
            dv_ref[0, :, lanes] = dv_acc[h].astype(BF16)

    plain = pl.BlockSpec((1, s, LANES), lambda bi, hp: (bi, 0, hp))
    pair = pl.BlockSpec((1, s, HP_WIDTH), lambda bi, hp: (bi, 0, hp))
    return pl.pallas_call(
        kern, name="attn_bwd",
        out_shape=jax.ShapeDtypeStruct(dproj3.shape, dproj3.dtype),
        grid=(b, SB_HEADS // 2),
        in_specs=[pair, plain, plain, ANY],
        out_specs=pair,
        input_output_aliases={3: 0},
        scratch_shapes=[pltpu.VMEM((2, s, HEAD_DIM), BF16)] * 4 + [pltpu.VMEM((2, s, HEAD_DIM), F32)] * 2,
        compiler_params=_cparams("parallel", "parallel"),
    )(proj3, dyp3, o3, dproj3)


CONV_COLS = 256
HALO = 8


def _conv_pre(xp, w_ref, b_ref, r0):
    pre = b_ref[...] + w_ref[CONV_K - 1:CONV_K, :] * xp[pl.ds(HALO + r0, CHUNK), :]
    for kk in range(1, CONV_K):
        pre = pre + w_ref[CONV_K - 1 - kk:CONV_K - kk, :] * xp[pl.ds(HALO + r0 - kk, CHUNK), :]
    return pre


def _conv_fwd(proj3, conv_w, conv_b):
    b, s, _ = proj3.shape
    nc = s // CHUNK

    def kern(x_ref, w_ref, b_ref, o_ref, xp):
        xp[0:HALO, :] = jnp.zeros((HALO, CONV_COLS), F32)
        xp[HALO:, :] = x_ref[0]
        for ci in range(nc):
            pre = _conv_pre(xp, w_ref, b_ref, ci * CHUNK)
            o_ref[0, ci * CHUNK:(ci + 1) * CHUNK, :] = pre * _sigmoid(pre)

    return pl.pallas_call(
        kern, name="conv_fwd",
        out_shape=jax.ShapeDtypeStruct((b, s, CONV_DIM), F32),
        grid=(CONV_DIM // CONV_COLS, b),
        in_specs=[pl.BlockSpec((1, s, CONV_COLS), lambda j, bi: (bi, 0, XBC0 // CONV_COLS + j)),
                  pl.BlockSpec((CONV_K, CONV_COLS), lambda j, bi: (0, j)),
                  pl.BlockSpec((1, CONV_COLS), lambda j, bi: (0, j))],
        out_specs=pl.BlockSpec((1, s, CONV_COLS), lambda j, bi: (bi, 0, j)),
        scratch_shapes=[pltpu.VMEM((s + HALO, CONV_COLS), F32)],
        compiler_params=_cparams("parallel", "parallel"),
    )(proj3, conv_w, conv_b)


def _conv_bwd(dact, proj3, conv_w, conv_b, col0, name, dproj3):
    b, s, width = dact.shape
    nc = s // CHUNK
    j0 = col0 // CONV_COLS

    def kern(da_ref, x_ref, w_ref, b_ref, _, dx_ref, dw_ref, db_ref, xp, dp):
        @pl.when(pl.program_id(1) == 0)
        def _():
            dw_ref[...] = jnp.zeros_like(dw_ref)
            db_ref[...] = jnp.zeros_like(db_ref)

        xp[0:HALO, :] = jnp.zeros((HALO, CONV_COLS), F32)
        xp[HALO:, :] = x_ref[0]
        dp[s:, :] = jnp.zeros((HALO, CONV_COLS), F32)
        for ci in range(nc):
            r0 = ci * CHUNK
            pre = _conv_pre(xp, w_ref, b_ref, r0)
            sg = _sigmoid(pre)
            dpre = da_ref[0, r0:r0 + CHUNK, :] * (sg * (1.0 + pre * (1.0 - sg)))
            dp[r0:r0 + CHUNK, :] = dpre
            db_ref[...] += jnp.sum(dpre, axis=0, keepdims=True)
            for kk in range(CONV_K):
                tap = CONV_K - 1 - kk
                dw_ref[tap:tap + 1, :] += jnp.sum(dpre * xp[pl.ds(HALO + r0 - kk, CHUNK), :], axis=0, keepdims=True)
        for ci in range(nc):
            r0 = ci * CHUNK
            dx = w_ref[CONV_K - 1:CONV_K, :] * dp[pl.ds(r0, CHUNK), :]
            for kk in range(1, CONV_K):
                dx = dx + w_ref[CONV_K - 1 - kk:CONV_K - kk, :] * dp[pl.ds(r0 + kk, CHUNK), :]
            dx_ref[0, r0:r0 + CHUNK, :] = dx.astype(BF16)

    return pl.pallas_call(
        kern, name=name,
        out_shape=(jax.ShapeDtypeStruct(dproj3.shape, dproj3.dtype), jax.ShapeDtypeStruct((CONV_K, width), F32),
                   jax.ShapeDtypeStruct((1, width), F32)),
        grid=(width // CONV_COLS, b),
        in_specs=[pl.BlockSpec((1, s, CONV_COLS), lambda j, bi: (bi, 0, j)),
                  pl.BlockSpec((1, s, CONV_COLS), lambda j, bi: (bi, 0, XBC0 // CONV_COLS + j0 + j)),
                  pl.BlockSpec((CONV_K, CONV_COLS), lambda j, bi: (0, j0 + j)),
                  pl.BlockSpec((1, CONV_COLS), lambda j, bi: (0, j0 + j)), ANY],
        out_specs=(pl.BlockSpec((1, s, CONV_COLS), lambda j, bi: (bi, 0, XBC0 // CONV_COLS + j0 + j)),
                   pl.BlockSpec((CONV_K, CONV_COLS), lambda j, bi: (0, j)),
                   pl.BlockSpec((1, CONV_COLS), lambda j, bi: (0, j))),
        input_output_aliases={4: 0},
        scratch_shapes=[pltpu.VMEM((s + HALO, CONV_COLS), F32)] * 2,
        compiler_params=_cparams("parallel", "arbitrary"),
    )(dact, proj3, conv_w, conv_b, dproj3)


def _sel_dot(v, sel, left=False):
    hi = v.astype(BF16)
    rest = v - hi.astype(F32)
    mid = rest.astype(BF16)
    lo = (rest - mid.astype(F32)).astype(BF16)
    if left:
        return _dot(sel, hi, NN) + _dot(sel, mid, NN) + _dot(sel, lo, NN)
    return _dot(hi, sel, NN) + _dot(mid, sel, NN) + _dot(lo, sel, NN)


def _ssd_common(dtr_ref, dtb_ref, alog_ref):
    lane = lax.broadcasted_iota(jnp.int32, (CHUNK, LANES), 1)
    row = lax.broadcasted_iota(jnp.int32, (CHUNK, LANES), 0)
    head_lane = lane < HEADS_PER_GROUP
    pre = dtr_ref[0, 0] + dtb_ref[0]
    dt = jnp.where(head_lane, jnp.maximum(pre, 0.0) + jnp.log(1.0 + jnp.exp(-jnp.abs(pre))), 0.0)
    a = jnp.where(head_lane[0:1], -jnp.exp(alog_ref[0]), 0.0)
    tril = (row >= lane).astype(BF16)
    acs = _sel_dot(dt * a, tril, left=True)
    acs_t = acs.T
    er = lax.broadcasted_iota(jnp.int32, (LANES, GROUP_WIDTH), 0)
    ec = lax.broadcasted_iota(jnp.int32, (LANES, GROUP_WIDTH), 1)
    expand = ((ec // HEAD_DIM) == er).astype(BF16)
    tr = lax.broadcasted_iota(jnp.int32, (GROUP_WIDTH, LANES), 0)
    tc = lax.broadcasted_iota(jnp.int32, (GROUP_WIDTH, LANES), 1)
    reduce = ((tr // HEAD_DIM) == tc).astype(BF16)
    dt_x = _sel_dot(dt, expand)
    acs_x = _sel_dot(acs, expand)
    end_x = acs_x[CHUNK - 1:CHUNK, :]
    causal = row >= lane
    return dict(dt=dt, a=a, pre=pre, head_lane=head_lane, acs=acs, acs_t=acs_t, expand=expand, reduce=reduce,
                dt_x=dt_x, acs_x=acs_x, end_x=end_x, causal=causal, row=row, lane=lane)


def _ssd_decay(cm, h):
    seg = cm["acs"][:, h:h + 1] - cm["acs_t"][h:h + 1, :]
    return jnp.where(cm["causal"], jnp.exp(jnp.minimum(seg, 0.0)), 0.0)


def _ssd_fwd(xact, proj3, dtr_g, dtb_g, alog_g, dskip_x, snw):
    b, s, _ = xact.shape
    nc = s // CHUNK
    g4 = SSD_GROUPS

    def kern(xs_ref, bm_ref, cm_ref, zs_ref, dtr_ref, dtb_ref, alog_ref, dsk_ref, snw_ref,
             y_ref, yn_ref, hst_ref, h_sc):
        @pl.when(pl.program_id(2) == 0)
        def _():
            h_sc[...] = jnp.zeros_like(h_sc)

        cm = _ssd_common(dtr_ref, dtb_ref, alog_ref)
        x = xs_ref[0]
        bmb = bm_ref[0].astype(BF16)
        cmb = cm_ref[0].astype(BF16)
        h_in = h_sc[...]
        hst_ref[0, 0, 0] = h_in
        xdt = x * cm["dt_x"]
        xdtb = xdt.astype(BF16)
        cb = _dot(cmb, bmb, NT)
        y_off = _dot(cmb, h_in.astype(BF16), NN) * jnp.exp(cm["acs_x"])
        for h in range(HEADS_PER_GROUP):
            lanes = slice(h * HEAD_DIM, (h + 1) * HEAD_DIM)
            m = (cb * _ssd_decay(cm, h)).astype(BF16)
            y_ref[0, :, lanes] = _dot(m, xdtb[:, lanes], NN)
        y = y_ref[0] + y_off + x * dsk_ref[...]
        y_ref[0] = y
        w = (xdt * jnp.exp(cm["end_x"] - cm["acs_x"])).astype(BF16)
        h_sc[...] = h_in * jnp.exp(cm["end_x"]) + _dot(bmb, w, TN)
        zs = zs_ref[0]
        y2 = y * (zs * _sigmoid(zs))
        yn_ref[0] = (y2 * lax.rsqrt(jnp.mean(y2 * y2, axis=-1, keepdims=True) + EPS) * snw_ref[...]).astype(BF16)

    gw = GROUP_WIDTH
    small = pl.BlockSpec((1, 1, LANES), lambda gi, bi, ci: (gi, 0, 0))
    xblk = pl.BlockSpec((1, CHUNK, gw), lambda gi, bi, ci: (bi, ci, gi))
    return pl.pallas_call(
        kern, name="ssd_fwd",
        out_shape=(jax.ShapeDtypeStruct((b, s, SSD_WIDTH), F32), jax.ShapeDtypeStruct((b, s, SSD_WIDTH), BF16),
                   jax.ShapeDtypeStruct((b, nc, g4, SSD_STATE, gw), F32)),
        grid=(g4, b, nc),
        in_specs=[xblk,
                  pl.BlockSpec((1, CHUNK, LANES), lambda gi, bi, ci: (bi, ci, SSD_WIDTH // LANES + gi)),
                  pl.BlockSpec((1, CHUNK, LANES), lambda gi, bi, ci: (bi, ci, SSD_WIDTH // LANES + g4 + gi)),
                  pl.BlockSpec((1, CHUNK, gw), lambda gi, bi, ci: (bi, ci, ZS0 // gw + gi)),
                  pl.BlockSpec((1, 1, CHUNK, LANES), lambda gi, bi, ci: (bi, gi, ci, 0)),
                  small, small,
                  pl.BlockSpec((1, gw), lambda gi, bi, ci: (0, gi)),
                  pl.BlockSpec((1, gw), lambda gi, bi, ci: (0, gi))],
        out_specs=(xblk, xblk, pl.BlockSpec((1, 1, 1, SSD_STATE, gw), lambda gi, bi, ci: (bi, ci, gi, 0, 0))),
        scratch_shapes=[pltpu.VMEM((SSD_STATE, gw), F32)],
        compiler_params=_cparams("parallel", "parallel", "arbitrary"),
    )(xact, xact, xact, proj3, dtr_g, dtb_g, alog_g, dskip_x, snw)


def _ssd_bwd(dyn3, y3, xact, proj3, hst, dtr_g, dtb_g, alog_g, dskip_x, snw, dproj3):
    b, s, _ = xact.shape
    nc = s // CHUNK
    g4 = SSD_GROUPS
    gw = GROUP_WIDTH

    def kern(dyn_ref, y_ref, xs_ref, bm_ref, cm_ref, zs_ref, hst_ref, dtr_ref, dtb_ref, alog_ref, dsk_ref, snw_ref, _,
             dxs_ref, dbm_ref, dcm_ref, dzs_ref, ddtr_ref, dsnw_ref, dalog_ref, ddtb_ref, ddsk_ref, dh_sc):
        first = jnp.logical_and(pl.program_id(1) == 0, pl.program_id(2) == 0)

        @pl.when(first)
        def _():
            dsnw_ref[...] = jnp.zeros_like(dsnw_ref)
            dalog_ref[...] = jnp.zeros_like(dalog_ref)
            ddtb_ref[...] = jnp.zeros_like(ddtb_ref)
            ddsk_ref[...] = jnp.zeros_like(ddsk_ref)

        @pl.when(pl.program_id(2) == 0)
        def _():
            dh_sc[...] = jnp.zeros_like(dh_sc)

        cm = _ssd_common(dtr_ref, dtb_ref, alog_ref)
        row, lane = cm["row"], cm["lane"]
        y = y_ref[0]
        zs = zs_ref[0]
        sg = _sigmoid(zs)
        silu = zs * sg
        y2 = y * silu
        rstd = lax.rsqrt(jnp.mean(y2 * y2, axis=-1, keepdims=True) + EPS)
        y2h = y2 * rstd
        dyn = dyn_ref[0]
        dsnw_ref[0] += jnp.sum(dyn * y2h, axis=0, keepdims=True)
        gwv = dyn * snw_ref[...]
        dy2 = rstd * (gwv - y2h * jnp.mean(gwv * y2h, axis=-1, keepdims=True))
        dzs_ref[0] = (dy2 * y * (sg * (1.0 + zs * (1.0 - sg)))).astype(BF16)
        dy = dy2 * silu
        dyb = dy.astype(BF16)

        x = xs_ref[0]
        bmb = bm_ref[0].astype(BF16)
        cmb = cm_ref[0].astype(BF16)
        h_in = hst_ref[0, 0, 0]
        h_inb = h_in.astype(BF16)
        d_hn = dh_sc[...]
        d_hnb = d_hn.astype(BF16)
        xdt = x * cm["dt_x"]
        xdtb = xdt.astype(BF16)
        eacs = jnp.exp(cm["acs_x"])
        dte = jnp.exp(cm["end_x"] - cm["acs_x"])
        wb = (xdt * dte).astype(BF16)

        dsk_lanes = jnp.broadcast_to(jnp.sum(dy * x, axis=0, keepdims=True), (8, gw))
        ddsk_ref[0] += _sel_dot(dsk_lanes, cm["reduce"])[0:1, :]
        dyo = dy * eacs
        dyob = dyo.astype(BF16)
        dacs_x = dyo * _dot(cmb, h_inb, NN)
        dcm = _dot(dyob, h_inb, NT)
        dh_in = _dot(cmb, dyob, TN)
        dw = _dot(bmb, d_hnb, NN)
        dbm = _dot(wb, d_hnb, NT)
        dxdt = dw * dte
        e_l = dw * xdt * dte
        dacs_x = dacs_x - e_l
        dend_x = jnp.sum(e_l, axis=0, keepdims=True)
        chunk_decay = jnp.exp(cm["end_x"])
        dh_sc[...] = d_hn * chunk_decay + dh_in
        dend_x = dend_x + jnp.sum(d_hn * h_in, axis=0, keepdims=True) * chunk_decay
        last_row = lax.broadcasted_iota(jnp.int32, (CHUNK, gw), 0) == CHUNK - 1
        dacs_x = dacs_x + jnp.where(last_row, dend_x, 0.0)

        cb = _dot(cmb, bmb, NT)
        dcb = jnp.zeros((CHUNK, CHUNK), F32)
        dacs = jnp.zeros((CHUNK, LANES), F32)
        dacs_t = jnp.zeros((LANES, CHUNK), F32)
        for h in range(HEADS_PER_GROUP):
            lanes = slice(h * HEAD_DIM, (h + 1) * HEAD_DIM)
            decay = _ssd_decay(cm, h)
            m = cb * decay
            dm = _dot(dyb[:, lanes], xdtb[:, lanes], NT)
            dxs_ref[0, :, lanes] = _dot(m.astype(BF16), dyb[:, lanes], TN)
            dcb_h = dm * decay
            dcb = dcb + dcb_h
            n = dcb_h * cb
            dacs = dacs + jnp.where(lane == h, jnp.sum(n, axis=1, keepdims=True), 0.0)
            dacs_t = dacs_t + jnp.where(row == h, jnp.sum(n, axis=0, keepdims=True), 0.0)
        dcbb = dcb.astype(BF16)
        dcm_ref[0] = dcm + _dot(dcbb, bmb, NN)
        dbm_ref[0] = dbm + _dot(dcbb, cmb, TN)
        dxdt = dxdt + dxs_ref[0]
        dxs_ref[0] = dy * dsk_ref[...] + dxdt * cm["dt_x"]

        dacs = dacs - dacs_t.T + _sel_dot(dacs_x, cm["reduce"])
        ddt = _sel_dot(dxdt * x, cm["reduce"])
        triu = (row <= lane).astype(BF16)
        rc = _sel_dot(dacs, triu, left=True)
        ddt = ddt + cm["a"] * rc
        dalog_ref[0] += jnp.sum(cm["dt"] * rc, axis=0, keepdims=True) * cm["a"]
        ddtr = jnp.where(cm["head_lane"], ddt * _sigmoid(cm["pre"]), 0.0)
        ddtr_ref[0, 0] = ddtr
        ddtb_ref[0] += jnp.sum(ddtr, axis=0, keepdims=True)

    def rev(ci):
        return nc - 1 - ci

    small = pl.BlockSpec((1, 1, LANES), lambda gi, bi, ci: (gi, 0, 0))
    xblk = pl.BlockSpec((1, CHUNK, gw), lambda gi, bi, ci: (bi, rev(ci), gi))
    nblk = pl.BlockSpec((1, CHUNK, LANES), lambda gi, bi, ci: (bi, rev(ci), gi))
    gvec = pl.BlockSpec((1, gw), lambda gi, bi, ci: (0, gi))
    gacc = pl.BlockSpec((1, 1, gw), lambda gi, bi, ci: (gi, 0, 0))
    return pl.pallas_call(
        kern, name="ssd_bwd",
        out_shape=(jax.ShapeDtypeStruct((b, s, SSD_WIDTH), F32),
                   jax.ShapeDtypeStruct((b, s, g4 * SSD_STATE), F32),
                   jax.ShapeDtypeStruct((b, s, g4 * SSD_STATE), F32),
                   jax.ShapeDtypeStruct(dproj3.shape, dproj3.dtype),
                   jax.ShapeDtypeStruct((b, g4, s, LANES), F32),
                   jax.ShapeDtypeStruct((g4, 1, gw), F32),
                   jax.ShapeDtypeStruct((g4, 1, LANES), F32),
                   jax.ShapeDtypeStruct((g4, 1, LANES), F32),
                   jax.ShapeDtypeStruct((g4, 1, LANES), F32)),
        grid=(g4, b, nc),
        in_specs=[xblk, xblk, xblk,
                  pl.BlockSpec((1, CHUNK, LANES), lambda gi, bi, ci: (bi, rev(ci), SSD_WIDTH // LANES + gi)),
                  pl.BlockSpec((1, CHUNK, LANES), lambda gi, bi, ci: (bi, rev(ci), SSD_WIDTH // LANES + g4 + gi)),
                  pl.BlockSpec((1, CHUNK, gw), lambda gi, bi, ci: (bi, rev(ci), ZS0 // gw + gi)),
                  pl.BlockSpec((1, 1, 1, SSD_STATE, gw), lambda gi, bi, ci: (bi, rev(ci), gi, 0, 0)),
                  pl.BlockSpec((1, 1, CHUNK, LANES), lambda gi, bi, ci: (bi, gi, rev(ci), 0)),
                  small, small, gvec, gvec, ANY],
        out_specs=(xblk, nblk, nblk,
                   pl.BlockSpec((1, CHUNK, gw), lambda gi, bi, ci: (bi, rev(ci), ZS0 // gw + gi)),
                   pl.BlockSpec((1, 1, CHUNK, LANES), lambda gi, bi, ci: (bi, gi, rev(ci), 0)),
                   gacc, small, small, small),
        input_output_aliases={12: 3},
        scratch_shapes=[pltpu.VMEM((SSD_STATE, gw), F32)],
        compiler_params=_cparams("parallel", "arbitrary", "arbitrary"),
    )(dyn3, y3, xact, xact, xact, proj3, hst, dtr_g, dtb_g, alog_g, dskip_x, snw, dproj3)


def _adamw(w, g, m, v, name):
    r, c = w.shape
    tr = 128 if r % 128 == 0 else r

    def kern(w_ref, g_ref, m_ref, v_ref, d_ref, nm_ref, nv_ref):
        gv = g_ref[...]
        nm = ADAM_B1 * m_ref[...] + (1.0 - ADAM_B1) * gv
        nv = ADAM_B2 * v_ref[...] + (1.0 - ADAM_B2) * (gv * gv)
        m_hat = nm / (1.0 - ADAM_B1 ** ADAM_STEP)
        v_hat = nv / (1.0 - ADAM_B2 ** ADAM_STEP)
        d_ref[...] = -ADAM_LR * (m_hat / (jnp.sqrt(v_hat) + ADAM_EPS) + ADAM_WD * w_ref[...])
        nm_ref[...] = nm
        nv_ref[...] = nv

    blk = pl.BlockSpec((tr, c), lambda i: (i, 0))
    out = jax.ShapeDtypeStruct((r, c), F32)
    return pl.pallas_call(
        kern, name=name, out_shape=(out, out, out), grid=(r // tr,),
        in_specs=[blk] * 4, out_specs=(blk, blk, blk),
        compiler_params=_cparams("parallel"),
    )(w, g, m, v)


ANY = pl.BlockSpec(memory_space=pl.ANY)


def _position():
    return lax.axis_index("x"), lax.axis_index("y"), lax.axis_index("c")


def _other_chips(x, y):
    return [(1 - x, y), (x, 1 - y), (1 - x, 1 - y)]


def _dma_sems(n):
    return [pltpu.SemaphoreType.DMA((n,)), pltpu.SemaphoreType.DMA((n,))]


def _gather_weights(shards):
    n = len(shards)

    def body(*refs):
        p_refs, out_refs = refs[:n], refs[n:2 * n]
        send_sems, recv_sems = refs[2 * n:]
        x, y, c = _position()
        me = 2 * x + y
        chips = _other_chips(x, y)

        def slab(a, chip, hf):
            half = shards[a].shape[0] // 2
            return out_refs[a].at[chip, pl.ds(hf * half, half), :]

        def my_half(a):
            half = shards[a].shape[0] // 2
            return p_refs[a].at[pl.ds(c * half, half), :]

        def over_ici(a, j, chip_from):
            px, py = chips[j]
            return pltpu.make_async_remote_copy(
                src_ref=my_half(a), dst_ref=slab(a, chip_from, c),
                send_sem=send_sems.at[3 * a + j], recv_sem=recv_sems.at[3 * a + j],
                device_id=(px, py, c), device_id_type=MESH)

        def to_sibling(a, j, hf):
            px, py = chips[j]
            return pltpu.make_async_remote_copy(
                src_ref=slab(a, 2 * px + py, hf), dst_ref=slab(a, 2 * px + py, hf),
                send_sem=send_sems.at[3 * (n + a) + j], recv_sem=recv_sems.at[3 * (n + a) + j],
                device_id=(x, y, 1 - c), device_id_type=MESH)

        own = [pltpu.make_async_remote_copy(
            src_ref=p_refs[a], dst_ref=out_refs[a].at[me], send_sem=send_sems.at[6 * n + a], recv_sem=recv_sems.at[6 * n + a],
            device_id=(x, y, 1 - c), device_id_type=MESH) for a in range(n)]
        first = [over_ici(a, j, me) for a in range(n) for j in range(3)]
        for cp in first + own:
            cp.start()
        passed = []
        for a in range(n):
            for j, (px, py) in enumerate(chips):
                over_ici(a, j, 2 * px + py).wait_recv()
                passed.append(to_sibling(a, j, c))
                passed[-1].start()
        for a in range(n):
            for j in range(3):
                to_sibling(a, j, 1 - c).wait_recv()
        for cp in first + passed:
            cp.wait_send()
        for cp in own:
            cp.wait()

    return pl.pallas_call(
        body, name="gather_weights",
        out_shape=[jax.ShapeDtypeStruct((N_CHIPS, *v.shape), v.dtype) for v in shards],
        in_specs=[ANY] * n, out_specs=[ANY] * n,
        scratch_shapes=_dma_sems(7 * n),
    )(*shards)


def _swap_halves(parts):
    n = len(parts)

    def body(*refs):
        v_refs, out_refs = refs[:n], refs[n:2 * n]
        send_sems, recv_sems = refs[2 * n:]
        x, y, c = _position()
        copies = []
        for a in range(n):
            half = parts[a].shape[1] // 2
            copies.append(pltpu.make_async_remote_copy(
                src_ref=v_refs[a].at[:, pl.ds((1 - c) * half, half), :], dst_ref=out_refs[a],
                send_sem=send_sems.at[a], recv_sem=recv_sems.at[a], device_id=(x, y, 1 - c), device_id_type=MESH))
        for cp in copies:
            cp.start()
        for cp in copies:
            cp.wait()

    return pl.pallas_call(
        body, name="grad_swap_halves",
        out_shape=[jax.ShapeDtypeStruct((v.shape[0], v.shape[1] // 2, v.shape[2]), v.dtype) for v in parts],
        in_specs=[ANY] * n, out_specs=[ANY] * n,
        scratch_shapes=_dma_sems(n),
    )(*parts)


def _chip_all_to_all(parts):
    n = len(parts)

    def body(*refs):
        p_refs, out_refs = refs[:n], refs[n:2 * n]
        send_sems, recv_sems = refs[2 * n:]
        x, y, c = _position()
        chips = _other_chips(x, y)
        sends = [pltpu.make_async_remote_copy(
            src_ref=p_refs[a].at[2 * px + py], dst_ref=out_refs[a].at[j],
            send_sem=send_sems.at[3 * a + j], recv_sem=recv_sems.at[3 * a + j],
            device_id=(px, py, c), device_id_type=MESH) for a in range(n) for j, (px, py) in enumerate(chips)]
        for cp in sends:
            cp.start()
        for cp in sends:
            cp.wait()

    return pl.pallas_call(
        body, name="grad_all_to_all",
        out_shape=[jax.ShapeDtypeStruct((N_CHIPS - 1, *v.shape[1:]), v.dtype) for v in parts],
        in_specs=[ANY] * n, out_specs=[ANY] * n,
        scratch_shapes=_dma_sems(3 * n),
    )(*parts)


def _join_halves(wholes):
    n = len(wholes)

    def body(*refs):
        out_refs = refs[n:2 * n]
        send_sems, recv_sems = refs[2 * n:]
        x, y, c = _position()
        copies = []
        for a in range(n):
            half = wholes[a].shape[0] // 2
            rows = out_refs[a].at[pl.ds(c * half, half), :]
            copies.append(pltpu.make_async_remote_copy(
                src_ref=rows, dst_ref=rows, send_sem=send_sems.at[a], recv_sem=recv_sems.at[a],
                device_id=(x, y, 1 - c), device_id_type=MESH))
        for cp in copies:
            cp.start()
        for cp in copies:
            cp.wait()

    return pl.pallas_call(
        body, name="grad_join_halves",
        out_shape=[jax.ShapeDtypeStruct(v.shape, v.dtype) for v in wholes],
        in_specs=[ANY] * n, out_specs=[ANY] * n,
        input_output_aliases={a: a for a in range(n)},
        scratch_shapes=_dma_sems(n),
    )(*wholes)


ADD_ROWS = 128


def _add_halves(g, sw, place, name):
    n, rows, cols = g.shape
    half = rows // 2
    nb = half // ADD_ROWS

    def kern(p_ref, g_ref, s_ref, o_ref):
        o_ref[...] = (g_ref[...] + s_ref[...]).astype(BF16)

    blk = pl.BlockSpec((1, ADD_ROWS, cols), lambda j, i, p_ref: (j, i, 0))
    return pl.pallas_call(
        kern, name=name,
        out_shape=jax.ShapeDtypeStruct((n, half, cols), BF16),
        grid_spec=pltpu.PrefetchScalarGridSpec(
            num_scalar_prefetch=1, grid=(n, nb),
            in_specs=[pl.BlockSpec((1, ADD_ROWS, cols), lambda j, i, p_ref: (j, p_ref[0] * nb + i, 0)), blk],
            out_specs=blk),
        compiler_params=_cparams("parallel", "parallel"),
    )(place, g, sw)


def _sum_chips(own, rx, place, name):
    _, half, cols = rx.shape
    nb = half // ADD_ROWS

    def kern(p_ref, own_ref, r_ref, o_ref):
        total = own_ref[0].astype(F32)
        for j in range(N_CHIPS - 1):
            total = total + r_ref[j].astype(F32)
        o_ref[...] = total

    return pl.pallas_call(
        kern, name=name,
        out_shape=jax.ShapeDtypeStruct((2 * half, cols), F32),
        grid_spec=pltpu.PrefetchScalarGridSpec(
            num_scalar_prefetch=1, grid=(nb,),
            in_specs=[pl.BlockSpec((1, ADD_ROWS, cols), lambda i, p_ref: (p_ref[1], i, 0)),
                      pl.BlockSpec((N_CHIPS - 1, ADD_ROWS, cols), lambda i, p_ref: (0, i, 0))],
            out_specs=pl.BlockSpec((ADD_ROWS, cols), lambda i, p_ref: (p_ref[0] * nb + i, 0))),
        compiler_params=_cparams("parallel"),
    )(place, own, rx)


def _gather_small(v, reduce, name):
    rows = v.shape[0]

    def body(v_ref, out_ref, buf, send_sems, recv_sems):
        x, y, c = _position()
        me = 4 * x + 2 * y + c
        buf[me] = v_ref[...]
        peers = [(x ^ (k >> 2), y ^ ((k >> 1) & 1), c ^ (k & 1)) for k in range(1, 8)]
        copies = [pltpu.make_async_remote_copy(
            src_ref=v_ref, dst_ref=buf.at[me],
            send_sem=send_sems.at[k], recv_sem=recv_sems.at[k],
            device_id=peer, device_id_type=MESH) for k, peer in enumerate(peers)]
        for cp in copies:
            cp.start()
        for k, (px, py, pc) in enumerate(peers):
            pltpu.make_async_remote_copy(
                src_ref=v_ref, dst_ref=buf.at[4 * px + 2 * py + pc],
                send_sem=send_sems.at[k], recv_sem=recv_sems.at[k],
                device_id=(px, py, pc), device_id_type=MESH).wait_recv()
        for cp in copies:
            cp.wait_send()
        if reduce:
            total = buf[0]
            for d in range(1, 8):
                total = total + buf[d]
            out_ref[...] = total
        else:
            out_ref[...] = buf[...]

    vm = pl.BlockSpec(memory_space=pltpu.VMEM)
    return pl.pallas_call(
        body, name=name,
        out_shape=jax.ShapeDtypeStruct((rows, LANES) if reduce else (8, rows, LANES), F32),
        in_specs=[vm], out_specs=vm,
        scratch_shapes=[pltpu.VMEM((8, rows, LANES), F32), pltpu.SemaphoreType.DMA((7,)), pltpu.SemaphoreType.DMA((7,))],
    )(v)


def _pad_rows(a, rows):
    return jnp.pad(a, ((0, rows - a.shape[0]), (0, 0)))


def _lane_pad(v):
    n = v.shape[1]
    return jnp.pad(v, ((0, 0), (0, -n % LANES)))


def _gather_all(w_in, w_attn_out, w_ssm_out, w_o, conv_w):
    d = D_MODEL
    w_in_all, w_ao, w_so, w_oo = _gather_weights([a[0].astype(BF16) for a in (w_in, w_attn_out, w_ssm_out, w_o)])
    w_proj = _to_proj_layout(jnp.concatenate([w_in_all[q] for q in range(N_CHIPS)], axis=1))
    w_ao = w_ao.reshape(D_MODEL, d)
    w_so = w_so.reshape(SSD_WIDTH, d)
    w_oo = w_oo.reshape(D_MODEL, d)
    conv_rows = conv_w[0].size // LANES
    conv_all = _gather_small(conv_w[0].reshape(conv_rows, LANES), False, "gather_conv_w")
    conv_w_all = conv_all[0::2].reshape(N_CHIPS, CONV_K, CONV_DIM // N_CHIPS).transpose(1, 0, 2).reshape(CONV_K, CONV_DIM)

    return w_proj, w_ao, w_so, w_oo, conv_w_all


def _local_step(x, loss_target, norm_w, w_proj, conv_w_all, conv_b, dt_bias, a_log, d_skip, ssm_norm_w,
                w_ao, w_so, w_oo, final_norm_w):
    b, s, d = x.shape
    t = b * s
    g4, hg = SSD_GROUPS, HEADS_PER_GROUP
    dtb_g = _lane_pad(dt_bias.reshape(g4, hg)).reshape(g4, 1, LANES)
    alog_g = _lane_pad(a_log.reshape(g4, hg)).reshape(g4, 1, LANES)
    dskip_x = jnp.repeat(d_skip, HEAD_DIM, axis=1)
    fnw = final_norm_w.reshape(1, d)

    x2 = x.reshape(t, d)
    h = _rms_fwd(x2, norm_w)
    big_tm = min(t, 2048)
    proj = _matmul(h, w_proj, tm=big_tm, tn=1280, tk=1024, name="proj")
    proj3 = proj.reshape(b, s, NP)
    o3, yp3 = _attn_fwd(proj3)
    xact = _conv_fwd(proj3, conv_w_all, conv_b)
    dtr = proj3[:, :, DT0:DT0 + g4 * hg].reshape(b, s, g4, hg).transpose(0, 2, 1, 3)
    dtr_g = jnp.pad(dtr, ((0, 0), (0, 0), (0, 0), (0, LANES - hg)))
    y3, yn3, hst = _ssd_fwd(xact, proj3, dtr_g, dtb_g, alog_g, dskip_x, ssm_norm_w)
    yp = yp3.reshape(t, D_MODEL)
    yn = yn3.reshape(t, SSD_WIDTH)
    ya = _matmul(yp, w_ao, tm=512, tn=1024, tk=1024, name="attn_out")
    ys = _matmul(yn, w_so, tm=512, tn=1024, tk=2048, name="ssm_out")
    merged = _merge_fwd(proj, ya, ys)
    mo = _matmul(merged, w_oo, tm=512, tn=1024, tk=1024, name="out_proj")
    dout, doutb, loss_part, d_fnw = _final_fwd_bwd(x2, mo, loss_target.reshape(t, d), fnw)

    dmerged = _matmul(doutb, w_oo, tb=True, tm=512, tn=1024, tk=1024, name="d_merged")
    g_wo = _matmul(merged, doutb, ta=True, tm=512, tn=1024, tk=1024, name="g_w_o")
    dya, dys, dproj = _merge_bwd(dmerged, proj, ya, ys)
    dyp = _matmul(dya, w_ao, tb=True, tm=512, tn=1024, tk=1024, name="d_attn_pre")
    g_wao = _matmul(yp, dya, ta=True, tm=512, tn=1024, tk=1024, name="g_w_attn_out")
    dyn = _matmul(dys, w_so, tb=True, tm=1024, tn=2048, tk=1024, name="d_ssm_norm")
    g_wso = _matmul(yn, dys, ta=True, tm=1024, tn=1024, tk=1024, name="g_w_ssm_out")
    dproj3 = _attn_bwd(proj3, dyp.reshape(b, s, D_MODEL), o3, dproj.reshape(b, s, NP))
    (dxs, dbm, dcm, dproj3, ddtr_g, d_snw_g, d_alog_g, d_dtb_g, d_dsk_g) = _ssd_bwd(
        dyn.reshape(b, s, SSD_WIDTH), y3, xact, proj3, hst, dtr_g, dtb_g, alog_g, dskip_x, ssm_norm_w, dproj3)
    dproj3, g_cw_xs, g_cb_xs = _conv_bwd(dxs, proj3, conv_w_all, conv_b, 0, "conv_bwd_x", dproj3)
    dproj3, g_cw_bm, g_cb_bm = _conv_bwd(dbm, proj3, conv_w_all, conv_b, SSD_WIDTH, "conv_bwd_b", dproj3)
    dproj3, g_cw_cm, g_cb_cm = _conv_bwd(dcm, proj3, conv_w_all, conv_b, SSD_WIDTH + g4 * SSD_STATE, "conv_bwd_c", dproj3)
    ddt = ddtr_g[:, :, :, :hg].transpose(0, 2, 1, 3).reshape(b, s, g4 * hg).astype(BF16)
    ddt = jnp.pad(ddt, ((0, 0), (0, 0), (0, DT_PAD - g4 * hg)))
    dproj = lax.dynamic_update_slice(dproj3, ddt, (0, 0, DT0)).reshape(t, NP)
    g_wproj = _matmul(h, dproj, ta=True, tm=1024, tn=1280, tk=1024, name="g_w_in")
    dh = _matmul(dproj, w_proj, tb=True, tm=big_tm, tn=1024, tk=1280, name="d_h")
    grad_x, d_nw = _rms_bwd(dh, x2, norm_w, dout)
    g_cw = jnp.concatenate([g_cw_xs, g_cw_bm, g_cw_cm], axis=1)
    g_cb = jnp.concatenate([g_cb_xs, g_cb_bm, g_cb_cm], axis=1)
    return (loss_part, grad_x, d_nw, g_wproj, g_cw, g_cb, d_dtb_g, d_alog_g, d_dsk_g, d_snw_g, g_wao, g_wso, g_wo, d_fnw)


def kernel(x, norm_w, w_in, conv_w, conv_b, dt_bias, a_log, d_skip, ssm_norm_w, w_attn_out, w_ssm_out, w_o, final_norm_w, loss_target, m_norm_w, m_w_in, m_conv_w, m_conv_b, m_dt_bias, m_a_log, m_d_skip, m_ssm_norm_w, m_w_attn_out, m_w_ssm_out, m_w_o, m_final_norm_w, v_norm_w, v_w_in, v_conv_w, v_conv_b, v_dt_bias, v_a_log, v_d_skip, v_ssm_norm_w, v_w_attn_out, v_w_ssm_out, v_w_o, v_final_norm_w):
    b, s, d = x.shape
    core = lax.axis_index("c")
    g4, hg = SSD_GROUPS, HEADS_PER_GROUP
    shard_cols = w_in.shape[2]
    w_proj, w_ao, w_so, w_oo, conv_w_all = _gather_all(w_in, w_attn_out, w_ssm_out, w_o, conv_w)
    (loss_part, grad_x, d_nw, g_wproj, g_cw, g_cb, d_dtb_g, d_alog_g, d_dsk_g, d_snw_g, g_wao, g_wso, g_wo, d_fnw) = _local_step(
        x, loss_target, norm_w, w_proj, conv_w_all, conv_b, dt_bias, a_log, d_skip, ssm_norm_w, w_ao, w_so, w_oo, final_norm_w)

    g_win = _from_proj_layout(g_wproj)
    g_win_chips = jnp.stack([g_win[:, q * shard_cols:(q + 1) * shard_cols] for q in range(N_CHIPS)])
    g_out_chips = jnp.concatenate([g.reshape(N_CHIPS, -1, d) for g in (g_wao, g_wso, g_wo)], axis=1)
    parts = [g_win_chips, g_out_chips]
    chip = 2 * lax.axis_index("x") + lax.axis_index("y")
    place = jnp.stack([core, chip]).astype(jnp.int32)
    from_sibling = _swap_halves(parts)
    chip_sums = [_add_halves(p, f, place, "grad_add_halves_%d" % i) for i, (p, f) in enumerate(zip(parts, from_sibling))]
    from_chips = _chip_all_to_all(chip_sums)
    wholes = [_sum_chips(o, r, place, "grad_sum_chips_%d" % i) for i, (o, r) in enumerate(zip(chip_sums, from_chips))]
    g_w_in, g_out = _join_halves(wholes)

    small = jnp.concatenate([
        loss_part, d_nw, g_cb, _lane_pad(d_dtb_g[:, 0, :hg].reshape(1, -1)), _lane_pad(d_alog_g[:, 0, :hg].reshape(1, -1)),
        _lane_pad(d_dsk_g[:, 0, :hg].reshape(1, -1)),
        d_snw_g.reshape(1, -1), d_fnw, g_cw.reshape(1, -1)], axis=1)
    small_rows = small.shape[1] // LANES
    reduced = _gather_small(_pad_rows(small.reshape(small_rows, LANES), -(-small_rows // 8) * 8), True, "reduce_small")
    flat = reduced.reshape(-1)

    def take(start, n):
        return flat[start:start + n].reshape(1, n)

    loss = flat[0]
    pos = LANES
    g_norm_w = take(pos, d); pos += d
    g_conv_b = take(pos, CONV_DIM); pos += CONV_DIM
    g_dt_bias = take(pos, g4 * hg); pos += LANES
    g_a_log = take(pos, g4 * hg); pos += LANES
    g_d_skip = take(pos, g4 * hg); pos += LANES
    g_ssm_norm_w = take(pos, SSD_WIDTH); pos += SSD_WIDTH
    g_final_norm_w = take(pos, d); pos += d
    conv_cols = CONV_DIM // N_CHIPS
    g_conv_w = lax.dynamic_slice_in_dim(flat[pos:pos + CONV_K * CONV_DIM].reshape(CONV_K, CONV_DIM), chip * conv_cols, conv_cols, axis=1)

    rows_ao, rows_so = D_MODEL // N_CHIPS, SSD_WIDTH // N_CHIPS
    g_w_attn_out = g_out[:rows_ao]
    g_w_ssm_out = g_out[rows_ao:rows_ao + rows_so]
    g_w_o = g_out[rows_ao + rows_so:]

    names = ["norm_w", "w_in", "conv_w", "conv_b", "dt_bias", "a_log", "d_skip", "ssm_norm_w",
             "w_attn_out", "w_ssm_out", "w_o", "final_norm_w"]
    weights = [norm_w, w_in, conv_w, conv_b, dt_bias, a_log, d_skip, ssm_norm_w, w_attn_out, w_ssm_out, w_o, final_norm_w]
    grads = [g_norm_w, g_w_in, g_conv_w, g_conv_b, g_dt_bias, g_a_log, g_d_skip, g_ssm_norm_w,
             g_w_attn_out, g_w_ssm_out, g_w_o, g_final_norm_w]
    ms = [m_norm_w, m_w_in, m_conv_w, m_conv_b, m_dt_bias, m_a_log, m_d_skip, m_ssm_norm_w,
          m_w_attn_out, m_w_ssm_out, m_w_o, m_final_norm_w]
    vs = [v_norm_w, v_w_in, v_conv_w, v_conv_b, v_dt_bias, v_a_log, v_d_skip, v_ssm_norm_w,
          v_w_attn_out, v_w_ssm_out, v_w_o, v_final_norm_w]
    out_g, out_d, out_m, out_v = [], [], [], []
    for name, w, g, m, v in zip(names, weights, grads, ms, vs):
        shape2 = g.shape
        dlt, nm, nv = _adamw(w.reshape(shape2), g, m.reshape(shape2), v.reshape(shape2), "adamw_" + name)
        out_g.append(g.reshape(w.shape))
        out_d.append(dlt.reshape(w.shape))
        out_m.append(nm.reshape(w.shape))
        out_v.append(nv.reshape(w.shape))

    return (loss, grad_x.reshape(b, s, d), *out_g, *out_d, *out_m, *out_v)
```

```python
import jax
import jax.numpy as jnp
from jax import lax
from jax.experimental import pallas as pl
from jax.experimental.pallas import tpu as pltpu

F32 = jnp.float32
BF16 = jnp.bfloat16
MESH = pl.DeviceIdType.MESH

D_MODEL = 1024
SB_HEADS = 16
HEAD_DIM = 64
SSD_WIDTH = 2048
SSD_GROUPS = 4
GROUP_WIDTH = SSD_WIDTH // SSD_GROUPS
HEADS_PER_GROUP = 8
SSD_STATE = 128
CHUNK = 128
CONV_K = 4
CONV_DIM = 3072
D_PROJ = 11296
EPS = 1e-6
ADAM_LR, ADAM_B1, ADAM_B2, ADAM_EPS, ADAM_WD, ADAM_STEP = 0.001, 0.9, 0.999, 1e-08, 0.01, 10

LANES = 128
HP_WIDTH = 4 * LANES
ZS0, GATE0, XBC0, DT0 = 4096, 6144, 8192, 11264
DT_PAD = 256
NP = DT0 + DT_PAD
N_CHIPS = 4
VMEM_LIMIT = 56 * 1024 * 1024


N_HP = SB_HEADS // 2
W_ZS0, W_XBC0, W_DT0, W_GATE0 = 4096, 6144, 9216, 9248


def _to_proj_layout(wt):
    d = wt.shape[1]
    pairs = wt[:W_ZS0].reshape(4, N_HP, LANES, d).transpose(1, 0, 2, 3).reshape(W_ZS0, d)
    return jnp.concatenate([pairs, wt[W_ZS0:W_XBC0], wt[W_GATE0:], wt[W_XBC0:W_DT0], wt[W_DT0:W_GATE0],
                            jnp.zeros((NP - D_PROJ, d), wt.dtype)], axis=0)


def _from_proj_layout(gt):
    d = gt.shape[1]
    qkvz = gt[:ZS0].reshape(N_HP, 4, LANES, d).transpose(1, 0, 2, 3).reshape(ZS0, d)
    return jnp.concatenate([qkvz, gt[ZS0:GATE0], gt[XBC0:DT0], gt[DT0:DT0 + W_GATE0 - W_DT0], gt[GATE0:XBC0]], axis=0)


def _cparams(*sem):
    return pltpu.CompilerParams(dimension_semantics=sem or None, vmem_limit_bytes=VMEM_LIMIT)


def _sigmoid(z):
    return 1.0 / (1.0 + jnp.exp(-z))


def _dot(a, b, dims, precision=None):
    return lax.dot_general(a, b, (dims, ((), ())), preferred_element_type=F32, precision=precision)


NN = ((1,), (0,))
NT = ((1,), (1,))
TN = ((0,), (0,))


def _matmul(a, b, *, ta=False, tb=False, out_dtype=F32, tm, tn, tk, name):
    m, k = (a.shape[1], a.shape[0]) if ta else a.shape
    n = b.shape[0] if tb else b.shape[1]
    assert m % tm == 0 and n % tn == 0 and k % tk == 0, (name, m, n, k)
    nk = k // tk
    use_scratch = out_dtype != F32
    dims = ((0,) if ta else (1,), (1,) if tb else (0,))

    def kern(a_ref, b_ref, o_ref, *scratch):
        acc = scratch[0] if use_scratch else o_ref
        kk = pl.program_id(2)

        @pl.when(kk == 0)
        def _():
            acc[...] = jnp.zeros_like(acc)

        acc[...] += _dot(a_ref[...], b_ref[...], dims)
        if use_scratch:
            @pl.when(kk == nk - 1)
            def _():
                o_ref[...] = acc[...].astype(out_dtype)

    a_spec = pl.BlockSpec((tk, tm), lambda i, j, q: (q, i)) if ta else pl.BlockSpec((tm, tk), lambda i, j, q: (i, q))
    b_spec = pl.BlockSpec((tn, tk), lambda i, j, q: (j, q)) if tb else pl.BlockSpec((tk, tn), lambda i, j, q: (q, j))
    return pl.pallas_call(
        kern, name=name,
        out_shape=jax.ShapeDtypeStruct((m, n), out_dtype),
        grid=(m // tm, n // tn, nk),
        in_specs=[a_spec, b_spec],
        out_specs=pl.BlockSpec((tm, tn), lambda i, j, q: (i, j)),
        scratch_shapes=[pltpu.VMEM((tm, tn), F32)] if use_scratch else [],
        compiler_params=_cparams("parallel", "parallel", "arbitrary"),
    )(a, b)


ROWS = 256


def _rms_fwd(x2, w):
    t, d = x2.shape

    def kern(x_ref, w_ref, h_ref):
        x = x_ref[...]
        r = lax.rsqrt(jnp.mean(x * x, axis=-1, keepdims=True) + EPS)
        h_ref[...] = (x * r * w_ref[...]).astype(BF16)

    return pl.pallas_call(
        kern, name="rms_fwd",
        out_shape=jax.ShapeDtypeStruct((t, d), BF16),
        grid=(t // ROWS,),
        in_specs=[pl.BlockSpec((ROWS, d), lambda i: (i, 0)), pl.BlockSpec((1, d), lambda i: (0, 0))],
        out_specs=pl.BlockSpec((ROWS, d), lambda i: (i, 0)),
        compiler_params=_cparams("parallel"),
    )(x2, w)


def _rms_bwd(dh, x2, w, dout):
    t, d = x2.shape

    def kern(dh_ref, x_ref, w_ref, dout_ref, gx_ref, dw_ref):
        @pl.when(pl.program_id(0) == 0)
        def _():
            dw_ref[...] = jnp.zeros_like(dw_ref)

        x = x_ref[...]
        r = lax.rsqrt(jnp.mean(x * x, axis=-1, keepdims=True) + EPS)
        xh = x * r
        g = dh_ref[...]
        dw_ref[...] += jnp.sum(g * xh, axis=0, keepdims=True)
        gw = g * w_ref[...]
        gx_ref[...] = dout_ref[...] + r * (gw - xh * jnp.mean(gw * xh, axis=-1, keepdims=True))

    row = pl.BlockSpec((ROWS, d), lambda i: (i, 0))
    vec = pl.BlockSpec((1, d), lambda i: (0, 0))
    return pl.pallas_call(
        kern, name="rms_bwd",
        out_shape=(jax.ShapeDtypeStruct((t, d), F32), jax.ShapeDtypeStruct((1, d), F32)),
        grid=(t // ROWS,),
        in_specs=[row, row, vec, row],
        out_specs=(row, vec),
        compiler_params=_cparams("arbitrary"),
    )(dh, x2, w, dout)


def _final_fwd_bwd(x2, mo, target, w):
    t, d = x2.shape

    def kern(x_ref, mo_ref, t_ref, w_ref, dout_ref, doutb_ref, loss_ref, dw_ref):
        @pl.when(pl.program_id(0) == 0)
        def _():
            loss_ref[...] = jnp.zeros_like(loss_ref)
            dw_ref[...] = jnp.zeros_like(dw_ref)

        u = x_ref[...] + mo_ref[...]
        r = lax.rsqrt(jnp.mean(u * u, axis=-1, keepdims=True) + EPS)
        uh = u * r
        wv = w_ref[...]
        err = uh * wv - t_ref[...]
        loss_ref[...] += (0.5 / d) * jnp.sum(err * err)
        dy = err * (1.0 / d)
        dw_ref[...] += jnp.sum(dy * uh, axis=0, keepdims=True)
        gw = dy * wv
        du = r * (gw - uh * jnp.mean(gw * uh, axis=-1, keepdims=True))
        dout_ref[...] = du
        doutb_ref[...] = du.astype(BF16)

    row = pl.BlockSpec((ROWS, d), lambda i: (i, 0))
    vec = pl.BlockSpec((1, d), lambda i: (0, 0))
    return pl.pallas_call(
        kern, name="final_fwd_bwd",
        out_shape=(jax.ShapeDtypeStruct((t, d), F32), jax.ShapeDtypeStruct((t, d), BF16),
                   jax.ShapeDtypeStruct((1, LANES), F32), jax.ShapeDtypeStruct((1, d), F32)),
        grid=(t // ROWS,),
        in_specs=[row, row, row, vec],
        out_specs=(row, row, pl.BlockSpec((1, LANES), lambda i: (0, 0)), vec),
        compiler_params=_cparams("arbitrary"),
    )(x2, mo, target, w)


def _merge_fwd(proj2, ya, ys):
    t = ya.shape[0]
    gblk = GATE0 // D_MODEL

    def kern(ga_ref, gs_ref, ya_ref, ys_ref, o_ref):
        o_ref[...] = (_sigmoid(ga_ref[...]) * ya_ref[...] + _sigmoid(gs_ref[...]) * ys_ref[...]).astype(BF16)

    row = pl.BlockSpec((ROWS, D_MODEL), lambda i: (i, 0))
    return pl.pallas_call(
        kern, name="merge_fwd",
        out_shape=jax.ShapeDtypeStruct((t, D_MODEL), BF16),
        grid=(t // ROWS,),
        in_specs=[pl.BlockSpec((ROWS, D_MODEL), lambda i: (i, gblk)),
                  pl.BlockSpec((ROWS, D_MODEL), lambda i: (i, gblk + 1)), row, row],
        out_specs=row,
        compiler_params=_cparams("parallel"),
    )(proj2, proj2, ya, ys)


def _merge_bwd(dm, proj2, ya, ys):
    t = ya.shape[0]
    gblk = GATE0 // D_MODEL

    def kern(dm_ref, ga_ref, gs_ref, ya_ref, ys_ref, dya_ref, dys_ref, dg_ref):
        g = dm_ref[...]
        sa = _sigmoid(ga_ref[...])
        ss = _sigmoid(gs_ref[...])
        dya_ref[...] = (g * sa).astype(BF16)
        dys_ref[...] = (g * ss).astype(BF16)
        dg_ref[:, :D_MODEL] = (g * ya_ref[...] * sa * (1.0 - sa)).astype(BF16)
        dg_ref[:, D_MODEL:] = (g * ys_ref[...] * ss * (1.0 - ss)).astype(BF16)

    row = pl.BlockSpec((ROWS, D_MODEL), lambda i: (i, 0))
    return pl.pallas_call(
        kern, name="merge_bwd",
        out_shape=(jax.ShapeDtypeStruct((t, D_MODEL), BF16), jax.ShapeDtypeStruct((t, D_MODEL), BF16),
                   jax.ShapeDtypeStruct((t, NP), BF16)),
        grid=(t // ROWS,),
        in_specs=[row, pl.BlockSpec((ROWS, D_MODEL), lambda i: (i, gblk)),
                  pl.BlockSpec((ROWS, D_MODEL), lambda i: (i, gblk + 1)), row, row],
        out_specs=(row, row, pl.BlockSpec((ROWS, 2 * D_MODEL), lambda i: (i, GATE0 // (2 * D_MODEL)))),
        compiler_params=_cparams("parallel"),
    )(dm, proj2, proj2, ya, ys)


TQ = 256
TK = 256
HEAD_LANES = (slice(0, HEAD_DIM), slice(HEAD_DIM, 2 * HEAD_DIM))


def _tri(pred):
    r = lax.broadcasted_iota(jnp.int32, (TK, TK), 0)
    c = lax.broadcasted_iota(jnp.int32, (TK, TK), 1)
    return pred(r, c).astype(BF16)


def _split_bf16(v):
    hi = v.astype(BF16)
    lo = (v - hi.astype(F32)).astype(BF16)
    return hi, lo


def _tri_dot(v, tri):
    hi, lo = _split_bf16(v)
    return _dot(hi, tri, NN) + _dot(lo, tri, NN)


def _sb_logs(z, mask):
    l1p = jnp.log(1.0 + jnp.exp(-jnp.abs(z)))
    lb = jnp.minimum(z, 0.0) - l1p
    lom = -jnp.maximum(z, 0.0) - l1p
    if mask is not None:
        lom = jnp.where(mask, lom, 0.0)
    return lb, lom


def _sb_weights(lb, later, carry_r, mask):
    a = jnp.exp(lb + (later + carry_r))
    if mask is not None:
        a = jnp.where(mask, a, 0.0)
    return a


DEAD = -104.0


def _while_alive(n, carry, step):
    def alive(cr):
        return jnp.max(jnp.maximum(cr[0][0], cr[1][0])) > DEAD

    def cond(state):
        jj, go, _ = state
        return jnp.logical_and(jj < n, go)

    def body(state):
        jj, _, cr = state
        cr = step(jj, cr)
        return jj + 1, alive(cr), cr

    return lax.while_loop(cond, body, (jnp.int32(0), alive(carry), carry))[2]


Q_LANES, K_LANES, V_LANES, ZA_LANES = (slice(i * LANES, (i + 1) * LANES) for i in range(4))


def _split_heads(dst, src, scale=None):
    for h, lanes in enumerate(HEAD_LANES):
        v = src[:, lanes]
        dst[h] = (v if scale is None else v * scale).astype(BF16)


def _attn_fwd(proj3):
    b, s, _ = proj3.shape
    nq = s // TQ
    scale = HEAD_DIM ** -0.5

    def kern(x_ref, o_ref, yp_ref, qs, ks, vs):
        _split_heads(qs, x_ref[0, :, Q_LANES], scale)
        _split_heads(ks, x_ref[0, :, K_LANES])
        _split_heads(vs, x_ref[0, :, V_LANES])
        za_ref = x_ref.at[:, :, ZA_LANES]
        row = lax.broadcasted_iota(jnp.int32, (TQ, TK), 0)
        col = lax.broadcasted_iota(jnp.int32, (TQ, TK), 1)
        tri_gt = _tri(lambda j, sk: j > sk)

        def q_block(i, _):
            r0 = pl.multiple_of(i * TQ, TQ)
            n_kb = (r0 + TQ + TK - 1) // TK
            qh = [qs[h, pl.ds(r0, TQ), :] for h in range(2)]

            def k_block(c0, carry, mask):
                kh = [ks[h, pl.ds(c0, TK), :] for h in range(2)]
                vh = [vs[h, pl.ds(c0, TK), :] for h in range(2)]
                z = [_dot(qh[h], kh[h], NT) for h in range(2)]
                logs, later = [], []
                for h in range(2):
                    logs.append(_sb_logs(z[h], mask))
                    later.append(_tri_dot(logs[h][1], tri_gt))
                out = []
                for h in range(2):
                    carry_r, acc = carry[h]
                    lb, lom = logs[h]
                    a = _sb_weights(lb, later[h], carry_r, mask)
                    row_sum = later[h][:, 0:1] + lom[:, 0:1]
                    out.append((carry_r + row_sum, acc + _dot(a.astype(BF16), vh[h], NN)))
                return tuple(out)

            c_last = pl.multiple_of((n_kb - 1) * TK, TK)
            start = (jnp.zeros((TQ, 1), F32), jnp.zeros((TQ, HEAD_DIM), F32))
            carry = k_block(c_last, (start, start), col + c_last < row + r0)

            carry = _while_alive(n_kb - 1, carry, lambda jj, cr: k_block(pl.multiple_of((n_kb - 2 - jj) * TK, TK), cr, None))
            for (_, acc), lanes in zip(carry, HEAD_LANES):
                o_ref[0, pl.ds(r0, TQ), lanes] = acc
                za = za_ref[0, pl.ds(r0, TQ), lanes]
                yp_ref[0, pl.ds(r0, TQ), lanes] = (acc * (za * _sigmoid(za))).astype(BF16)
            return 0

        lax.fori_loop(0, nq, q_block, 0)

    out_spec = pl.BlockSpec((1, s, LANES), lambda bi, hp: (bi, 0, hp))
    return pl.pallas_call(
        kern, name="attn_fwd",
        out_shape=(jax.ShapeDtypeStruct((b, s, D_MODEL), F32), jax.ShapeDtypeStruct((b, s, D_MODEL), BF16)),
        grid=(b, SB_HEADS // 2),
        in_specs=[pl.BlockSpec((1, s, HP_WIDTH), lambda bi, hp: (bi, 0, hp))],
        out_specs=(out_spec, out_spec),
        scratch_shapes=[pltpu.VMEM((2, s, HEAD_DIM), BF16)] * 3,
        compiler_params=_cparams("parallel", "parallel"),
    )(proj3)


def _attn_bwd(proj3, dyp3, o3, dproj3):
    b, s, _ = proj3.shape
    nq = s // TQ
    scale = HEAD_DIM ** -0.5

    def kern(x_ref, dyp_ref, o_ref, _, d_ref, qs, ks, vs, dos, dk_acc, dv_acc):
        _split_heads(qs, x_ref[0, :, Q_LANES], scale)
        _split_heads(ks, x_ref[0, :, K_LANES])
        _split_heads(vs, x_ref[0, :, V_LANES])
        dq_ref, dk_ref, dv_ref = (d_ref.at[:, :, lanes] for lanes in (Q_LANES, K_LANES, V_LANES))
        za = x_ref[0, :, ZA_LANES]
        sg = _sigmoid(za)
        dyp = dyp_ref[0]
        _split_heads(dos, dyp * (za * sg))
        d_ref[0, :, ZA_LANES] = (dyp * o_ref[0] * (sg * (1.0 + za * (1.0 - sg)))).astype(BF16)
        dk_acc[...] = jnp.zeros_like(dk_acc)
        dv_acc[...] = jnp.zeros_like(dv_acc)
        row = lax.broadcasted_iota(jnp.int32, (TQ, TK), 0)
        col = lax.broadcasted_iota(jnp.int32, (TQ, TK), 1)
        tri_gt = _tri(lambda j, sk: j > sk)
        tri_ge = _tri(lambda j, sk: j >= sk)

        def q_block(i, _):
            r0 = pl.multiple_of(i * TQ, TQ)
            n_kb = (r0 + TQ + TK - 1) // TK
            qh = [qs[h, pl.ds(r0, TQ), :] for h in range(2)]
            doh = [dos[h, pl.ds(r0, TQ), :] for h in range(2)]
            totals = [jnp.sum(doh[h].astype(F32) * o_ref[0, pl.ds(r0, TQ), lanes], axis=1, keepdims=True)
                      for h, lanes in enumerate(HEAD_LANES)]

            def k_block(c0, carry, mask):
                kh = [ks[h, pl.ds(c0, TK), :] for h in range(2)]
                vh = [vs[h, pl.ds(c0, TK), :] for h in range(2)]
                z = [_dot(qh[h], kh[h], NT) for h in range(2)]
                da = [_dot(doh[h], vh[h], NT) for h in range(2)]
                logs, later = [], []
                for h in range(2):
                    logs.append(_sb_logs(z[h], mask))
                    later.append(_tri_dot(logs[h][1], tri_gt))
                ab, g, suffix = [], [], []
                for h in range(2):
                    a = _sb_weights(logs[h][0], later[h], carry[h][0], mask)
                    ab.append(a.astype(BF16))
                    g.append(da[h] * ab[h].astype(F32))
                    suffix.append(_tri_dot(g[h], tri_ge))
                out = []
                for h in range(2):
                    carry_r, carry_g, dq = carry[h]
                    lb, lom = logs[h]
                    dz = g[h] - (g[h] + (totals[h] - carry_g) - suffix[h]) * jnp.exp(lb)
                    if mask is not None:
                        dz = jnp.where(mask, dz, 0.0)
                    dzb = dz.astype(BF16)
                    dk_acc[h, pl.ds(c0, TK), :] += _dot(dzb, qh[h], TN)
                    dv_acc[h, pl.ds(c0, TK), :] += _dot(ab[h], doh[h], TN)
                    out.append((carry_r + (later[h][:, 0:1] + lom[:, 0:1]), carry_g + suffix[h][:, 0:1],
                                dq + _dot(dzb, kh[h], NN)))
                return tuple(out)

            c_last = pl.multiple_of((n_kb - 1) * TK, TK)
            zero = jnp.zeros((TQ, 1), F32)
            start = (zero, zero, jnp.zeros((TQ, HEAD_DIM), F32))
            carry = k_block(c_last, (start, start), col + c_last < row + r0)

            carry = _while_alive(n_kb - 1, carry, lambda jj, cr: k_block(pl.multiple_of((n_kb - 2 - jj) * TK, TK), cr, None))
            for (_, _, dq), lanes in zip(carry, HEAD_LANES):
                dq_ref[0, pl.ds(r0, TQ), lanes] = (dq * scale).astype(BF16)
            return 0

        lax.fori_loop(0, nq, q_block, 0)

        for h, lanes in enumerate(HEAD_LANES):
            dk_ref[0, :, lanes] = dk_acc[h].astype(BF16)
            dv_ref[0, :, lanes] = dv_acc[h].astype(BF16)

    plain = pl.BlockSpec((1, s, LANES), lambda bi, hp: (bi, 0, hp))
    pair = pl.BlockSpec((1, s, HP_WIDTH), lambda bi, hp: (bi, 0, hp))
    return pl.pallas_call(
        kern, name="attn_bwd",
        out_shape=jax.ShapeDtypeStruct(dproj3.shape, dproj3.dtype),
        grid=(b, SB_HEADS // 2),
        in_specs=[pair, plain, plain, ANY],
        out_specs=pair,
        input_output_aliases={3: 0},
        scratch_shapes=[pltpu.VMEM((2, s, HEAD_DIM), BF16)] * 4 + [pltpu.VMEM((2, s, HEAD_DIM), F32)] * 2,
        compiler_params=_cparams("parallel", "parallel"),
    )(proj3, dyp3, o3, dproj3)


CONV_COLS = 256
HALO = 8


def _conv_pre(xp, w_ref, b_ref, r0):
    pre = b_ref[...] + w_ref[CONV_K - 1:CONV_K, :] * xp[pl.ds(HALO + r0, CHUNK), :]
    for kk in range(1, CONV_K):
        pre = pre + w_ref[CONV_K - 1 - kk:CONV_K - kk, :] * xp[pl.ds(HALO + r0 - kk, CHUNK), :]
    return pre


def _conv_fwd(proj3, conv_w, conv_b):
    b, s, _ = proj3.shape
    nc = s // CHUNK

    def kern(x_ref, w_ref, b_ref, o_ref, xp):
        xp[0:HALO, :] = jnp.zeros((HALO, CONV_COLS), F32)
        xp[HALO:, :] = x_ref[0]
        for ci in range(nc):
            pre = _conv_pre(xp, w_ref, b_ref, ci * CHUNK)
            o_ref[0, ci * CHUNK:(ci + 1) * CHUNK, :] = pre * _sigmoid(pre)

    return pl.pallas_call(
        kern, name="conv_fwd",
        out_shape=jax.ShapeDtypeStruct((b, s, CONV_DIM), F32),
        grid=(CONV_DIM // CONV_COLS, b),
        in_specs=[pl.BlockSpec((1, s, CONV_COLS), lambda j, bi: (bi, 0, XBC0 // CONV_COLS + j)),
                  pl.BlockSpec((CONV_K, CONV_COLS), lambda j, bi: (0, j)),
                  pl.BlockSpec((1, CONV_COLS), lambda j, bi: (0, j))],
        out_specs=pl.BlockSpec((1, s, CONV_COLS), lambda j, bi: (bi, 0, j)),
        scratch_shapes=[pltpu.VMEM((s + HALO, CONV_COLS), F32)],
        compiler_params=_cparams("parallel", "parallel"),
    )(proj3, conv_w, conv_b)


def _conv_bwd(dact, proj3, conv_w, conv_b, col0, name, dproj3):
    b, s, width = dact.shape
    nc = s // CHUNK
    j0 = col0 // CONV_COLS

    def kern(da_ref, x_ref, w_ref, b_ref, _, dx_ref, dw_ref, db_ref, xp, dp):
        @pl.when(pl.program_id(1) == 0)
        def _():
            dw_ref[...] = jnp.zeros_like(dw_ref)
            db_ref[...] = jnp.zeros_like(db_ref)

        xp[0:HALO, :] = jnp.zeros((HALO, CONV_COLS), F32)
        xp[HALO:, :] = x_ref[0]
        dp[s:, :] = jnp.zeros((HALO, CONV_COLS), F32)
        for ci in range(nc):
            r0 = ci * CHUNK
            pre = _conv_pre(xp, w_ref, b_ref, r0)
            sg = _sigmoid(pre)
            dpre = da_ref[0, r0:r0 + CHUNK, :] * (sg * (1.0 + pre * (1.0 - sg)))
            dp[r0:r0 + CHUNK, :] = dpre
            db_ref[...] += jnp.sum(dpre, axis=0, keepdims=True)
            for kk in range(CONV_K):
                tap = CONV_K - 1 - kk
                dw_ref[tap:tap + 1, :] += jnp.sum(dpre * xp[pl.ds(HALO + r0 - kk, CHUNK), :], axis=0, keepdims=True)
        for ci in range(nc):
            r0 = ci * CHUNK
            dx = w_ref[CONV_K - 1:CONV_K, :] * dp[pl.ds(r0, CHUNK), :]
            for kk in range(1, CONV_K):
                dx = dx + w_ref[CONV_K - 1 - kk:CONV_K - kk, :] * dp[pl.ds(r0 + kk, CHUNK), :]
            dx_ref[0, r0:r0 + CHUNK, :] = dx.astype(BF16)

    return pl.pallas_call(
        kern, name=name,
        out_shape=(jax.ShapeDtypeStruct(dproj3.shape, dproj3.dtype), jax.ShapeDtypeStruct((CONV_K, width), F32),
                   jax.ShapeDtypeStruct((1, width), F32)),
        grid=(width // CONV_COLS, b),
        in_specs=[pl.BlockSpec((1, s, CONV_COLS), lambda j, bi: (bi, 0, j)),
                  pl.BlockSpec((1, s, CONV_COLS), lambda j, bi: (bi, 0, XBC0 // CONV_COLS + j0 + j)),
                  pl.BlockSpec((CONV_K, CONV_COLS), lambda j, bi: (0, j0 + j)),
                  pl.BlockSpec((1, CONV_COLS), lambda j, bi: (0, j0 + j)), ANY],
        out_specs=(pl.BlockSpec((1, s, CONV_COLS), lambda j, bi: (bi, 0, XBC0 // CONV_COLS + j0 + j)),
                   pl.BlockSpec((CONV_K, CONV_COLS), lambda j, bi: (0, j)),
                   pl.BlockSpec((1, CONV_COLS), lambda j, bi: (0, j))),
        input_output_aliases={4: 0},
        scratch_shapes=[pltpu.VMEM((s + HALO, CONV_COLS), F32)] * 2,
        compiler_params=_cparams("parallel", "arbitrary"),
    )(dact, proj3, conv_w, conv_b, dproj3)


def _sel_dot(v, sel, left=False):
    hi = v.astype(BF16)
    rest = v - hi.astype(F32)
    mid = rest.astype(BF16)
    lo = (rest - mid.astype(F32)).astype(BF16)
    if left:
        return _dot(sel, hi, NN) + _dot(sel, mid, NN) + _dot(sel, lo, NN)
    return _dot(hi, sel, NN) + _dot(mid, sel, NN) + _dot(lo, sel, NN)


def _ssd_common(dtr_ref, dtb_ref, alog_ref):
    lane = lax.broadcasted_iota(jnp.int32, (CHUNK, LANES), 1)
    row = lax.broadcasted_iota(jnp.int32, (CHUNK, LANES), 0)
    head_lane = lane < HEADS_PER_GROUP
    pre = dtr_ref[0, 0] + dtb_ref[0]
    dt = jnp.where(head_lane, jnp.maximum(pre, 0.0) + jnp.log(1.0 + jnp.exp(-jnp.abs(pre))), 0.0)
    a = jnp.where(head_lane[0:1], -jnp.exp(alog_ref[0]), 0.0)
    tril = (row >= lane).astype(BF16)
    acs = _sel_dot(dt * a, tril, left=True)
    acs_t = acs.T
    er = lax.broadcasted_iota(jnp.int32, (LANES, GROUP_WIDTH), 0)
    ec = lax.broadcasted_iota(jnp.int32, (LANES, GROUP_WIDTH), 1)
    expand = ((ec // HEAD_DIM) == er).astype(BF16)
    tr = lax.broadcasted_iota(jnp.int32, (GROUP_WIDTH, LANES), 0)
    tc = lax.broadcasted_iota(jnp.int32, (GROUP_WIDTH, LANES), 1)
    reduce = ((tr // HEAD_DIM) == tc).astype(BF16)
    dt_x = _sel_dot(dt, expand)
    acs_x = _sel_dot(acs, expand)
    end_x = acs_x[CHUNK - 1:CHUNK, :]
    causal = row >= lane
    return dict(dt=dt, a=a, pre=pre, head_lane=head_lane, acs=acs, acs_t=acs_t, expand=expand, reduce=reduce,
                dt_x=dt_x, acs_x=acs_x, end_x=end_x, causal=causal, row=row, lane=lane)


def _ssd_decay(cm, h):
    seg = cm["acs"][:, h:h + 1] - cm["acs_t"][h:h + 1, :]
    return jnp.where(cm["causal"], jnp.exp(jnp.minimum(seg, 0.0)), 0.0)


def _ssd_fwd(xact, proj3, dtr_g, dtb_g, alog_g, dskip_x, snw):
    b, s, _ = xact.shape
    nc = s // CHUNK
    g4 = SSD_GROUPS

    def kern(xs_ref, bm_ref, cm_ref, zs_ref, dtr_ref, dtb_ref, alog_ref, dsk_ref, snw_ref,
             y_ref, yn_ref, hst_ref, h_sc):
        @pl.when(pl.program_id(2) == 0)
        def _():
            h_sc[...] = jnp.zeros_like(h_sc)

        cm = _ssd_common(dtr_ref, dtb_ref, alog_ref)
        x = xs_ref[0]
        bmb = bm_ref[0].astype(BF16)
        cmb = cm_ref[0].astype(BF16)
        h_in = h_sc[...]
        hst_ref[0, 0, 0] = h_in
        xdt = x * cm["dt_x"]
        xdtb = xdt.astype(BF16)
        cb = _dot(cmb, bmb, NT)
        y_off = _dot(cmb, h_in.astype(BF16), NN) * jnp.exp(cm["acs_x"])
        for h in range(HEADS_PER_GROUP):
            lanes = slice(h * HEAD_DIM, (h + 1) * HEAD_DIM)
            m = (cb * _ssd_decay(cm, h)).astype(BF16)
            y_ref[0, :, lanes] = _dot(m, xdtb[:, lanes], NN)
        y = y_ref[0] + y_off + x * dsk_ref[...]
        y_ref[0] = y
        w = (xdt * jnp.exp(cm["end_x"] - cm["acs_x"])).astype(BF16)
        h_sc[...] = h_in * jnp.exp(cm["end_x"]) + _dot(bmb, w, TN)
        zs = zs_ref[0]
        y2 = y * (zs * _sigmoid(zs))
        yn_ref[0] = (y2 * lax.rsqrt(jnp.mean(y2 * y2, axis=-1, keepdims=True) + EPS) * snw_ref[...]).astype(BF16)

    gw = GROUP_WIDTH
    small = pl.BlockSpec((1, 1, LANES), lambda gi, bi, ci: (gi, 0, 0))
    xblk = pl.BlockSpec((1, CHUNK, gw), lambda gi, bi, ci: (bi, ci, gi))
    return pl.pallas_call(
        kern, name="ssd_fwd",
        out_shape=(jax.ShapeDtypeStruct((b, s, SSD_WIDTH), F32), jax.ShapeDtypeStruct((b, s, SSD_WIDTH), BF16),
                   jax.ShapeDtypeStruct((b, nc, g4, SSD_STATE, gw), F32)),
        grid=(g4, b, nc),
        in_specs=[xblk,
                  pl.BlockSpec((1, CHUNK, LANES), lambda gi, bi, ci: (bi, ci, SSD_WIDTH // LANES + gi)),
                  pl.BlockSpec((1, CHUNK, LANES), lambda gi, bi, ci: (bi, ci, SSD_WIDTH // LANES + g4 + gi)),
                  pl.BlockSpec((1, CHUNK, gw), lambda gi, bi, ci: (bi, ci, ZS0 // gw + gi)),
                  pl.BlockSpec((1, 1, CHUNK, LANES), lambda gi, bi, ci: (bi, gi, ci, 0)),
                  small, small,
                  pl.BlockSpec((1, gw), lambda gi, bi, ci: (0, gi)),
                  pl.BlockSpec((1, gw), lambda gi, bi, ci: (0, gi))],
        out_specs=(xblk, xblk, pl.BlockSpec((1, 1, 1, SSD_STATE, gw), lambda gi, bi, ci: (bi, ci, gi, 0, 0))),
        scratch_shapes=[pltpu.VMEM((SSD_STATE, gw), F32)],
        compiler_params=_cparams("parallel", "parallel", "arbitrary"),
    )(xact, xact, xact, proj3, dtr_g, dtb_g, alog_g, dskip_x, snw)


def _ssd_bwd(dyn3, y3, xact, proj3, hst, dtr_g, dtb_g, alog_g, dskip_x, snw, dproj3):
    b, s, _ = xact.shape
    nc = s // CHUNK
    g4 = SSD_GROUPS
    gw = GROUP_WIDTH

    def kern(dyn_ref, y_ref, xs_ref, bm_ref, cm_ref, zs_ref, hst_ref, dtr_ref, dtb_ref, alog_ref, dsk_ref, snw_ref, _,
             dxs_ref, dbm_ref, dcm_ref, dzs_ref, ddtr_ref, dsnw_ref, dalog_ref, ddtb_ref, ddsk_ref, dh_sc):
        first = jnp.logical_and(pl.program_id(1) == 0, pl.program_id(2) == 0)

        @pl.when(first)
        def _():
            dsnw_ref[...] = jnp.zeros_like(dsnw_ref)
            dalog_ref[...] = jnp.zeros_like(dalog_ref)
            ddtb_ref[...] = jnp.zeros_like(ddtb_ref)
            ddsk_ref[...] = jnp.zeros_like(ddsk_ref)

        @pl.when(pl.program_id(2) == 0)
        def _():
            dh_sc[...] = jnp.zeros_like(dh_sc)

        cm = _ssd_common(dtr_ref, dtb_ref, alog_ref)
        row, lane = cm["row"], cm["lane"]
        y = y_ref[0]
        zs = zs_ref[0]
        sg = _sigmoid(zs)
        silu = zs * sg
        y2 = y * silu
        rstd = lax.rsqrt(jnp.mean(y2 * y2, axis=-1, keepdims=True) + EPS)
        y2h = y2 * rstd
        dyn = dyn_ref[0]
        dsnw_ref[0] += jnp.sum(dyn * y2h, axis=0, keepdims=True)
        gwv = dyn * snw_ref[...]
        dy2 = rstd * (gwv - y2h * jnp.mean(gwv * y2h, axis=-1, keepdims=True))
        dzs_ref[0] = (dy2 * y * (sg * (1.0 + zs * (1.0 - sg)))).astype(BF16)
        dy = dy2 * silu
        dyb = dy.astype(BF16)

        x = xs_ref[0]
        bmb = bm_ref[0].astype(BF16)
        cmb = cm_ref[0].astype(BF16)
        h_in = hst_ref[0, 0, 0]
        h_inb = h_in.astype(BF16)
        d_hn = dh_sc[...]
        d_hnb = d_hn.astype(BF16)
        xdt = x * cm["dt_x"]
        xdtb = xdt.astype(BF16)
        eacs = jnp.exp(cm["acs_x"])
        dte = jnp.exp(cm["end_x"] - cm["acs_x"])
        wb = (xdt * dte).astype(BF16)

        dsk_lanes = jnp.broadcast_to(jnp.sum(dy * x, axis=0, keepdims=True), (8, gw))
        ddsk_ref[0] += _sel_dot(dsk_lanes, cm["reduce"])[0:1, :]
        dyo = dy * eacs
        dyob = dyo.astype(BF16)
        dacs_x = dyo * _dot(cmb, h_inb, NN)
        dcm = _dot(dyob, h_inb, NT)
        dh_in = _dot(cmb, dyob, TN)
        dw = _dot(bmb, d_hnb, NN)
        dbm = _dot(wb, d_hnb, NT)
        dxdt = dw * dte
        e_l = dw * xdt * dte
        dacs_x = dacs_x - e_l
        dend_x = jnp.sum(e_l, axis=0, keepdims=True)
        chunk_decay = jnp.exp(cm["end_x"])
        dh_sc[...] = d_hn * chunk_decay + dh_in
        dend_x = dend_x + jnp.sum(d_hn * h_in, axis=0, keepdims=True) * chunk_decay
        last_row = lax.broadcasted_iota(jnp.int32, (CHUNK, gw), 0) == CHUNK - 1
        dacs_x = dacs_x + jnp.where(last_row, dend_x, 0.0)

        cb = _dot(cmb, bmb, NT)
        dcb = jnp.zeros((CHUNK, CHUNK), F32)
        dacs = jnp.zeros((CHUNK, LANES), F32)
        dacs_t = jnp.zeros((LANES, CHUNK), F32)
        for h in range(HEADS_PER_GROUP):
            lanes = slice(h * HEAD_DIM, (h + 1) * HEAD_DIM)
            decay = _ssd_decay(cm, h)
            m = cb * decay
            dm = _dot(dyb[:, lanes], xdtb[:, lanes], NT)
            dxs_ref[0, :, lanes] = _dot(m.astype(BF16), dyb[:, lanes], TN)
            dcb_h = dm * decay
            dcb = dcb + dcb_h
            n = dcb_h * cb
            dacs = dacs + jnp.where(lane == h, jnp.sum(n, axis=1, keepdims=True), 0.0)
            dacs_t = dacs_t + jnp.where(row == h, jnp.sum(n, axis=0, keepdims=True), 0.0)
        dcbb = dcb.astype(BF16)
        dcm_ref[0] = dcm + _dot(dcbb, bmb, NN)
        dbm_ref[0] = dbm + _dot(dcbb, cmb, TN)
        dxdt = dxdt + dxs_ref[0]
        dxs_ref[0] = dy * dsk_ref[...] + dxdt * cm["dt_x"]

        dacs = dacs - dacs_t.T + _sel_dot(dacs_x, cm["reduce"])
        ddt = _sel_dot(dxdt * x, cm["reduce"])
        triu = (row <= lane).astype(BF16)
        rc = _sel_dot(dacs, triu, left=True)
        ddt = ddt + cm["a"] * rc
        dalog_ref[0] += jnp.sum(cm["dt"] * rc, axis=0, keepdims=True) * cm["a"]
        ddtr = jnp.where(cm["head_lane"], ddt * _sigmoid(cm["pre"]), 0.0)
        ddtr_ref[0, 0] = ddtr
        ddtb_ref[0] += jnp.sum(ddtr, axis=0, keepdims=True)

    def rev(ci):
        return nc - 1 - ci

    small = pl.BlockSpec((1, 1, LANES), lambda gi, bi, ci: (gi, 0, 0))
    xblk = pl.BlockSpec((1, CHUNK, gw), lambda gi, bi, ci: (bi, rev(ci), gi))
    nblk = pl.BlockSpec((1, CHUNK, LANES), lambda gi, bi, ci: (bi, rev(ci), gi))
    gvec = pl.BlockSpec((1, gw), lambda gi, bi, ci: (0, gi))
    gacc = pl.BlockSpec((1, 1, gw), lambda gi, bi, ci: (gi, 0, 0))
    return pl.pallas_call(
        kern, name="ssd_bwd",
        out_shape=(jax.ShapeDtypeStruct((b, s, SSD_WIDTH), F32),
                   jax.ShapeDtypeStruct((b, s, g4 * SSD_STATE), F32),
                   jax.ShapeDtypeStruct((b, s, g4 * SSD_STATE), F32),
                   jax.ShapeDtypeStruct(dproj3.shape, dproj3.dtype),
                   jax.ShapeDtypeStruct((b, g4, s, LANES), F32),
                   jax.ShapeDtypeStruct((g4, 1, gw), F32),
                   jax.ShapeDtypeStruct((g4, 1, LANES), F32),
                   jax.ShapeDtypeStruct((g4, 1, LANES), F32),
                   jax.ShapeDtypeStruct((g4, 1, LANES), F32)),
        grid=(g4, b, nc),
        in_specs=[xblk, xblk, xblk,
                  pl.BlockSpec((1, CHUNK, LANES), lambda gi, bi, ci: (bi, rev(ci), SSD_WIDTH // LANES + gi)),
                  pl.BlockSpec((1, CHUNK, LANES), lambda gi, bi, ci: (bi, rev(ci), SSD_WIDTH // LANES + g4 + gi)),
                  pl.BlockSpec((1, CHUNK, gw), lambda gi, bi, ci: (bi, rev(ci), ZS0 // gw + gi)),
                  pl.BlockSpec((1, 1, 1, SSD_STATE, gw), lambda gi, bi, ci: (bi, rev(ci), gi, 0, 0)),
                  pl.BlockSpec((1, 1, CHUNK, LANES), lambda gi, bi, ci: (bi, gi, rev(ci), 0)),
                  small, small, gvec, gvec, ANY],
        out_specs=(xblk, nblk, nblk,
                   pl.BlockSpec((1, CHUNK, gw), lambda gi, bi, ci: (bi, rev(ci), ZS0 // gw + gi)),
                   pl.BlockSpec((1, 1, CHUNK, LANES), lambda gi, bi, ci: (bi, gi, rev(ci), 0)),
                   gacc, small, small, small),
        input_output_aliases={12: 3},
        scratch_shapes=[pltpu.VMEM((SSD_STATE, gw), F32)],
        compiler_params=_cparams("parallel", "arbitrary", "arbitrary"),
    )(dyn3, y3, xact, xact, xact, proj3, hst, dtr_g, dtb_g, alog_g, dskip_x, snw, dproj3)


def _adamw(w, g, m, v, name):
    r, c = w.shape
    tr = 128 if r % 128 == 0 else r
    tc = LANES if (tr == r and r > 128 and c % LANES == 0) else c

    def kern(w_ref, g_ref, m_ref, v_ref, d_ref, nm_ref, nv_ref):
        gv = g_ref[...]
        nm = ADAM_B1 * m_ref[...] + (1.0 - ADAM_B1) * gv
        nv = ADAM_B2 * v_ref[...] + (1.0 - ADAM_B2) * (gv * gv)
        m_hat = nm / (1.0 - ADAM_B1 ** ADAM_STEP)
        v_hat = nv / (1.0 - ADAM_B2 ** ADAM_STEP)
        d_ref[...] = -ADAM_LR * (m_hat / (jnp.sqrt(v_hat) + ADAM_EPS) + ADAM_WD * w_ref[...])
        nm_ref[...] = nm
        nv_ref[...] = nv

    blk = pl.BlockSpec((tr, tc), lambda i, j: (i, j))
    out = jax.ShapeDtypeStruct((r, c), F32)
    return pl.pallas_call(
        kern, name=name, out_shape=(out, out, out), grid=(r // tr, c // tc),
        in_specs=[blk] * 4, out_specs=(blk, blk, blk),
        compiler_params=_cparams("parallel", "parallel"),
    )(w, g, m, v)


ANY = pl.BlockSpec(memory_space=pl.ANY)


def _position():
    return lax.axis_index("x"), lax.axis_index("y"), lax.axis_index("c")


def _other_chips(x, y):
    return [(1 - x, y), (x, 1 - y), (1 - x, 1 - y)]


def _dma_sems(n):
    return [pltpu.SemaphoreType.DMA((n,)), pltpu.SemaphoreType.DMA((n,))]


def _gather_weights(shards):
    n = len(shards)

    def body(*refs):
        p_refs, out_refs = refs[:n], refs[n:2 * n]
        send_sems, recv_sems = refs[2 * n:]
        x, y, c = _position()
        me = 2 * x + y
        chips = _other_chips(x, y)

        def slab(a, chip, hf):
            half = shards[a].shape[1] // 2
            return out_refs[a].at[chip, :, pl.ds(hf * half, half)]

        def my_half(a):
            half = shards[a].shape[1] // 2
            return p_refs[a].at[:, pl.ds(c * half, half)]

        def over_ici(a, j, chip_from):
            px, py = chips[j]
            return pltpu.make_async_remote_copy(
                src_ref=my_half(a), dst_ref=slab(a, chip_from, c),
                send_sem=send_sems.at[3 * a + j], recv_sem=recv_sems.at[3 * a + j],
                device_id=(px, py, c), device_id_type=MESH)

        def to_sibling(a, j, hf):
            px, py = chips[j]
            return pltpu.make_async_remote_copy(
                src_ref=slab(a, 2 * px + py, hf), dst_ref=slab(a, 2 * px + py, hf),
                send_sem=send_sems.at[3 * (n + a) + j], recv_sem=recv_sems.at[3 * (n + a) + j],
                device_id=(x, y, 1 - c), device_id_type=MESH)

        own = [pltpu.make_async_remote_copy(
            src_ref=p_refs[a], dst_ref=out_refs[a].at[me], send_sem=send_sems.at[6 * n + a], recv_sem=recv_sems.at[6 * n + a],
            device_id=(x, y, 1 - c), device_id_type=MESH) for a in range(n)]
        first = [over_ici(a, j, me) for a in range(n) for j in range(3)]
        for cp in first + own:
            cp.start()
        passed = []
        for a in range(n):
            for j, (px, py) in enumerate(chips):
                over_ici(a, j, 2 * px + py).wait_recv()
                passed.append(to_sibling(a, j, c))
                passed[-1].start()
        for a in range(n):
            for j in range(3):
                to_sibling(a, j, 1 - c).wait_recv()
        for cp in first + passed:
            cp.wait_send()
        for cp in own:
            cp.wait()

    return pl.pallas_call(
        body, name="gather_weights",
        out_shape=[jax.ShapeDtypeStruct((N_CHIPS, *v.shape), v.dtype) for v in shards],
        in_specs=[ANY] * n, out_specs=[ANY] * n,
        scratch_shapes=_dma_sems(7 * n),
    )(*shards)


def _swap_halves(parts):
    n = len(parts)

    def body(*refs):
        v_refs, out_refs = refs[:n], refs[n:2 * n]
        send_sems, recv_sems = refs[2 * n:]
        x, y, c = _position()
        copies = []
        for a in range(n):
            half = parts[a].shape[2] // 2
            copies.append(pltpu.make_async_remote_copy(
                src_ref=v_refs[a].at[:, :, pl.ds((1 - c) * half, half)], dst_ref=out_refs[a],
                send_sem=send_sems.at[a], recv_sem=recv_sems.at[a], device_id=(x, y, 1 - c), device_id_type=MESH))
        for cp in copies:
            cp.start()
        for cp in copies:
            cp.wait()

    return pl.pallas_call(
        body, name="grad_swap_halves",
        out_shape=[jax.ShapeDtypeStruct((v.shape[0], v.shape[1], v.shape[2] // 2), v.dtype) for v in parts],
        in_specs=[ANY] * n, out_specs=[ANY] * n,
        scratch_shapes=_dma_sems(n),
    )(*parts)


def _chip_all_to_all(parts):
    n = len(parts)

    def body(*refs):
        p_refs, out_refs = refs[:n], refs[n:2 * n]
        send_sems, recv_sems = refs[2 * n:]
        x, y, c = _position()
        chips = _other_chips(x, y)
        sends = [pltpu.make_async_remote_copy(
            src_ref=p_refs[a].at[2 * px + py], dst_ref=out_refs[a].at[j],
            send_sem=send_sems.at[3 * a + j], recv_sem=recv_sems.at[3 * a + j],
            device_id=(px, py, c), device_id_type=MESH) for a in range(n) for j, (px, py) in enumerate(chips)]
        for cp in sends:
            cp.start()
        for cp in sends:
            cp.wait()

    return pl.pallas_call(
        body, name="grad_all_to_all",
        out_shape=[jax.ShapeDtypeStruct((N_CHIPS - 1, *v.shape[1:]), v.dtype) for v in parts],
        in_specs=[ANY] * n, out_specs=[ANY] * n,
        scratch_shapes=_dma_sems(3 * n),
    )(*parts)


def _join_halves(wholes):
    n = len(wholes)

    def body(*refs):
        out_refs = refs[n:2 * n]
        send_sems, recv_sems = refs[2 * n:]
        x, y, c = _position()
        copies = []
        for a in range(n):
            half = wholes[a].shape[1] // 2
            mine = out_refs[a].at[:, pl.ds(c * half, half)]
            copies.append(pltpu.make_async_remote_copy(
                src_ref=mine, dst_ref=mine, send_sem=send_sems.at[a], recv_sem=recv_sems.at[a],
                device_id=(x, y, 1 - c), device_id_type=MESH))
        for cp in copies:
            cp.start()
        for cp in copies:
            cp.wait()

    return pl.pallas_call(
        body, name="grad_join_halves",
        out_shape=[jax.ShapeDtypeStruct(v.shape, v.dtype) for v in wholes],
        in_specs=[ANY] * n, out_specs=[ANY] * n,
        input_output_aliases={a: a for a in range(n)},
        scratch_shapes=_dma_sems(n),
    )(*wholes)


STRIP = 256


def _add_halves(g, sw, place, name):
    n, rows, cols = g.shape
    nb = cols // 2 // STRIP

    def kern(p_ref, g_ref, s_ref, o_ref):
        o_ref[...] = (g_ref[...] + s_ref[...]).astype(BF16)

    blk = pl.BlockSpec((1, rows, STRIP), lambda j, i, p_ref: (j, 0, i))
    return pl.pallas_call(
        kern, name=name,
        out_shape=jax.ShapeDtypeStruct((n, rows, cols // 2), BF16),
        grid_spec=pltpu.PrefetchScalarGridSpec(
            num_scalar_prefetch=1, grid=(n, nb),
            in_specs=[pl.BlockSpec((1, rows, STRIP), lambda j, i, p_ref: (j, 0, p_ref[0] * nb + i)), blk],
            out_specs=blk),
        compiler_params=_cparams("parallel", "parallel"),
    )(place, g, sw)


def _sum_chips(own, rx, place, name):
    _, rows, half = rx.shape
    nb = half // STRIP

    def kern(p_ref, own_ref, r_ref, o_ref):
        total = own_ref[0].astype(F32)
        for j in range(N_CHIPS - 1):
            total = total + r_ref[j].astype(F32)
        o_ref[...] = total

    return pl.pallas_call(
        kern, name=name,
        out_shape=jax.ShapeDtypeStruct((rows, 2 * half), F32),
        grid_spec=pltpu.PrefetchScalarGridSpec(
            num_scalar_prefetch=1, grid=(nb,),
            in_specs=[pl.BlockSpec((1, rows, STRIP), lambda i, p_ref: (p_ref[1], 0, i)),
                      pl.BlockSpec((N_CHIPS - 1, rows, STRIP), lambda i, p_ref: (0, 0, i))],
            out_specs=pl.BlockSpec((rows, STRIP), lambda i, p_ref: (0, p_ref[0] * nb + i))),
        compiler_params=_cparams("parallel"),
    )(place, own, rx)


def _gather_small(v, reduce, name):
    rows = v.shape[0]

    def body(v_ref, out_ref, buf, send_sems, recv_sems):
        x, y, c = _position()
        me = 4 * x + 2 * y + c
        buf[me] = v_ref[...]
        peers = [(x ^ (k >> 2), y ^ ((k >> 1) & 1), c ^ (k & 1)) for k in range(1, 8)]
        copies = [pltpu.make_async_remote_copy(
            src_ref=v_ref, dst_ref=buf.at[me],
            send_sem=send_sems.at[k], recv_sem=recv_sems.at[k],
            device_id=peer, device_id_type=MESH) for k, peer in enumerate(peers)]
        for cp in copies:
            cp.start()
        for k, (px, py, pc) in enumerate(peers):
            pltpu.make_async_remote_copy(
                src_ref=v_ref, dst_ref=buf.at[4 * px + 2 * py + pc],
                send_sem=send_sems.at[k], recv_sem=recv_sems.at[k],
                device_id=(px, py, pc), device_id_type=MESH).wait_recv()
        for cp in copies:
            cp.wait_send()
        if reduce:
            total = buf[0]
            for d in range(1, 8):
                total = total + buf[d]
            out_ref[...] = total
        else:
            out_ref[...] = buf[...]

    vm = pl.BlockSpec(memory_space=pltpu.VMEM)
    return pl.pallas_call(
        body, name=name,
        out_shape=jax.ShapeDtypeStruct((rows, LANES) if reduce else (8, rows, LANES), F32),
        in_specs=[vm], out_specs=vm,
        scratch_shapes=[pltpu.VMEM((8, rows, LANES), F32), pltpu.SemaphoreType.DMA((7,)), pltpu.SemaphoreType.DMA((7,))],
    )(v)


def _pad_rows(a, rows):
    return jnp.pad(a, ((0, rows - a.shape[0]), (0, 0)))


def _lane_pad(v):
    n = v.shape[1]
    return jnp.pad(v, ((0, 0), (0, -n % LANES)))


def _gather_all(w_in, w_attn_out, w_ssm_out, w_o, conv_w):
    d = D_MODEL
    own = [w_in[0].T.astype(BF16)] + [a[0].astype(BF16) for a in (w_attn_out, w_ssm_out, w_o)]
    w_in_t, w_ao, w_so, w_oo = _gather_weights(own)
    w_proj_t = _to_proj_layout(w_in_t.reshape(D_PROJ, d))
    w_ao = w_ao.reshape(D_MODEL, d)
    w_so = w_so.reshape(SSD_WIDTH, d)
    w_oo = w_oo.reshape(D_MODEL, d)
    conv_rows = conv_w[0].size // LANES
    conv_all = _gather_small(conv_w[0].reshape(conv_rows, LANES), False, "gather_conv_w")
    conv_w_all = conv_all[0::2].reshape(N_CHIPS, CONV_K, CONV_DIM // N_CHIPS).transpose(1, 0, 2).reshape(CONV_K, CONV_DIM)

    return w_proj_t, w_ao, w_so, w_oo, conv_w_all


def _local_step(x, loss_target, norm_w, w_proj_t, conv_w_all, conv_b, dt_bias, a_log, d_skip, ssm_norm_w,
                w_ao, w_so, w_oo, final_norm_w):
    b, s, d = x.shape
    t = b * s
    g4, hg = SSD_GROUPS, HEADS_PER_GROUP
    dtb_g = _lane_pad(dt_bias.reshape(g4, hg)).reshape(g4, 1, LANES)
    alog_g = _lane_pad(a_log.reshape(g4, hg)).reshape(g4, 1, LANES)
    dskip_x = jnp.repeat(d_skip, HEAD_DIM, axis=1)
    fnw = final_norm_w.reshape(1, d)

    x2 = x.reshape(t, d)
    h = _rms_fwd(x2, norm_w)
    big_tm = min(t, 2048)
    proj = _matmul(h, w_proj_t, tb=True, tm=big_tm, tn=1280, tk=1024, name="proj")
    proj3 = proj.reshape(b, s, NP)
    o3, yp3 = _attn_fwd(proj3)
    xact = _conv_fwd(proj3, conv_w_all, conv_b)
    dtr = proj3[:, :, DT0:DT0 + g4 * hg].reshape(b, s, g4, hg).transpose(0, 2, 1, 3)
    dtr_g = jnp.pad(dtr, ((0, 0), (0, 0), (0, 0), (0, LANES - hg)))
    y3, yn3, hst = _ssd_fwd(xact, proj3, dtr_g, dtb_g, alog_g, dskip_x, ssm_norm_w)
    yp = yp3.reshape(t, D_MODEL)
    yn = yn3.reshape(t, SSD_WIDTH)
    ya = _matmul(yp, w_ao, tm=512, tn=1024, tk=1024, name="attn_out")
    ys = _matmul(yn, w_so, tm=512, tn=1024, tk=2048, name="ssm_out")
    merged = _merge_fwd(proj, ya, ys)
    mo = _matmul(merged, w_oo, tm=512, tn=1024, tk=1024, name="out_proj")
    dout, doutb, loss_part, d_fnw = _final_fwd_bwd(x2, mo, loss_target.reshape(t, d), fnw)

    dmerged = _matmul(doutb, w_oo, tb=True, tm=512, tn=1024, tk=1024, name="d_merged")
    g_wo = _matmul(merged, doutb, ta=True, tm=512, tn=1024, tk=1024, name="g_w_o")
    dya, dys, dproj = _merge_bwd(dmerged, proj, ya, ys)
    dyp = _matmul(dya, w_ao, tb=True, tm=512, tn=1024, tk=1024, name="d_attn_pre")
    g_wao = _matmul(yp, dya, ta=True, tm=512, tn=1024, tk=1024, name="g_w_attn_out")
    dyn = _matmul(dys, w_so, tb=True, tm=1024, tn=2048, tk=1024, name="d_ssm_norm")
    g_wso = _matmul(yn, dys, ta=True, tm=1024, tn=1024, tk=1024, name="g_w_ssm_out")
    dproj3 = _attn_bwd(proj3, dyp.reshape(b, s, D_MODEL), o3, dproj.reshape(b, s, NP))
    (dxs, dbm, dcm, dproj3, ddtr_g, d_snw_g, d_alog_g, d_dtb_g, d_dsk_g) = _ssd_bwd(
        dyn.reshape(b, s, SSD_WIDTH), y3, xact, proj3, hst, dtr_g, dtb_g, alog_g, dskip_x, ssm_norm_w, dproj3)
    dproj3, g_cw_xs, g_cb_xs = _conv_bwd(dxs, proj3, conv_w_all, conv_b, 0, "conv_bwd_x", dproj3)
    dproj3, g_cw_bm, g_cb_bm = _conv_bwd(dbm, proj3, conv_w_all, conv_b, SSD_WIDTH, "conv_bwd_b", dproj3)
    dproj3, g_cw_cm, g_cb_cm = _conv_bwd(dcm, proj3, conv_w_all, conv_b, SSD_WIDTH + g4 * SSD_STATE, "conv_bwd_c", dproj3)
    ddt = ddtr_g[:, :, :, :hg].transpose(0, 2, 1, 3).reshape(b, s, g4 * hg).astype(BF16)
    ddt = jnp.pad(ddt, ((0, 0), (0, 0), (0, DT_PAD - g4 * hg)))
    dproj = lax.dynamic_update_slice(dproj3, ddt, (0, 0, DT0)).reshape(t, NP)
    g_wproj = _matmul(dproj, h, ta=True, tm=1280, tn=1024, tk=1024, name="g_w_in")
    dh = _matmul(dproj, w_proj_t, tm=big_tm, tn=1024, tk=1280, name="d_h")
    grad_x, d_nw = _rms_bwd(dh, x2, norm_w, dout)
    g_cw = jnp.concatenate([g_cw_xs, g_cw_bm, g_cw_cm], axis=1)
    g_cb = jnp.concatenate([g_cb_xs, g_cb_bm, g_cb_cm], axis=1)
    return (loss_part, grad_x, d_nw, g_wproj, g_cw, g_cb, d_dtb_g, d_alog_g, d_dsk_g, d_snw_g, g_wao, g_wso, g_wo, d_fnw)


def kernel(x, norm_w, w_in, conv_w, conv_b, dt_bias, a_log, d_skip, ssm_norm_w, w_attn_out, w_ssm_out, w_o, final_norm_w, loss_target, m_norm_w, m_w_in, m_conv_w, m_conv_b, m_dt_bias, m_a_log, m_d_skip, m_ssm_norm_w, m_w_attn_out, m_w_ssm_out, m_w_o, m_final_norm_w, v_norm_w, v_w_in, v_conv_w, v_conv_b, v_dt_bias, v_a_log, v_d_skip, v_ssm_norm_w, v_w_attn_out, v_w_ssm_out, v_w_o, v_final_norm_w):
    b, s, d = x.shape
    core = lax.axis_index("c")
    g4, hg = SSD_GROUPS, HEADS_PER_GROUP
    shard_cols = w_in.shape[2]
    w_proj_t, w_ao, w_so, w_oo, conv_w_all = _gather_all(w_in, w_attn_out, w_ssm_out, w_o, conv_w)
    (loss_part, grad_x, d_nw, g_wproj, g_cw, g_cb, d_dtb_g, d_alog_g, d_dsk_g, d_snw_g, g_wao, g_wso, g_wo, d_fnw) = _local_step(
        x, loss_target, norm_w, w_proj_t, conv_w_all, conv_b, dt_bias, a_log, d_skip, ssm_norm_w, w_ao, w_so, w_oo, final_norm_w)

    g_win_chips = _from_proj_layout(g_wproj).reshape(N_CHIPS, shard_cols, d)
    g_out_chips = jnp.concatenate([g.reshape(N_CHIPS, -1, d) for g in (g_wao, g_wso, g_wo)], axis=1)
    parts = [g_win_chips, g_out_chips]
    chip = 2 * lax.axis_index("x") + lax.axis_index("y")
    place = jnp.stack([core, chip]).astype(jnp.int32)
    from_sibling = _swap_halves(parts)
    chip_sums = [_add_halves(p, f, place, "grad_add_halves_%d" % i) for i, (p, f) in enumerate(zip(parts, from_sibling))]
    from_chips = _chip_all_to_all(chip_sums)
    wholes = [_sum_chips(o, r, place, "grad_sum_chips_%d" % i) for i, (o, r) in enumerate(zip(chip_sums, from_chips))]
    g_w_in, g_out = _join_halves(wholes)

    small = jnp.concatenate([
        loss_part, d_nw, g_cb, _lane_pad(d_dtb_g[:, 0, :hg].reshape(1, -1)), _lane_pad(d_alog_g[:, 0, :hg].reshape(1, -1)),
        _lane_pad(d_dsk_g[:, 0, :hg].reshape(1, -1)),
        d_snw_g.reshape(1, -1), d_fnw, g_cw.reshape(1, -1)], axis=1)
    small_rows = small.shape[1] // LANES
    reduced = _gather_small(_pad_rows(small.reshape(small_rows, LANES), -(-small_rows // 8) * 8), True, "reduce_small")
    flat = reduced.reshape(-1)

    def take(start, n):
        return flat[start:start + n].reshape(1, n)

    loss = flat[0]
    pos = LANES
    g_norm_w = take(pos, d); pos += d
    g_conv_b = take(pos, CONV_DIM); pos += CONV_DIM
    g_dt_bias = take(pos, g4 * hg); pos += LANES
    g_a_log = take(pos, g4 * hg); pos += LANES
    g_d_skip = take(pos, g4 * hg); pos += LANES
    g_ssm_norm_w = take(pos, SSD_WIDTH); pos += SSD_WIDTH
    g_final_norm_w = take(pos, d); pos += d
    conv_cols = CONV_DIM // N_CHIPS
    g_conv_w = lax.dynamic_slice_in_dim(flat[pos:pos + CONV_K * CONV_DIM].reshape(CONV_K, CONV_DIM), chip * conv_cols, conv_cols, axis=1)

    rows_ao, rows_so = D_MODEL // N_CHIPS, SSD_WIDTH // N_CHIPS
    g_w_attn_out = g_out[:rows_ao]
    g_w_ssm_out = g_out[rows_ao:rows_ao + rows_so]
    g_w_o = g_out[rows_ao + rows_so:]

    names = ["norm_w", "w_in", "conv_w", "conv_b", "dt_bias", "a_log", "d_skip", "ssm_norm_w",
             "w_attn_out", "w_ssm_out", "w_o", "final_norm_w"]
    weights = [norm_w, w_in, conv_w, conv_b, dt_bias, a_log, d_skip, ssm_norm_w, w_attn_out, w_ssm_out, w_o, final_norm_w]
    grads = [g_norm_w, g_w_in, g_conv_w, g_conv_b, g_dt_bias, g_a_log, g_d_skip, g_ssm_norm_w,
             g_w_attn_out, g_w_ssm_out, g_w_o, g_final_norm_w]
    ms = [m_norm_w, m_w_in, m_conv_w, m_conv_b, m_dt_bias, m_a_log, m_d_skip, m_ssm_norm_w,
          m_w_attn_out, m_w_ssm_out, m_w_o, m_final_norm_w]
    vs = [v_norm_w, v_w_in, v_conv_w, v_conv_b, v_dt_bias, v_a_log, v_d_skip, v_ssm_norm_w,
          v_w_attn_out, v_w_ssm_out, v_w_o, v_final_norm_w]
    out_g, out_d, out_m, out_v = [], [], [], []
    for name, w, g, m, v in zip(names, weights, grads, ms, vs):
        if name == "w_in":
            to2, back = (lambda a: a[0].T), (lambda a: a.T.reshape(w.shape))
        else:
            to2, back = (lambda a: a.reshape(g.shape)), (lambda a: a.reshape(w.shape))
        dlt, nm, nv = _adamw(to2(w), g, to2(m), to2(v), "adamw_" + name)
        out_g.append(back(g))
        out_d.append(back(dlt))
        out_m.append(back(nm))
        out_v.append(back(nv))

    return (loss, grad_x.reshape(b, s, d), *out_g, *out_d, *out_m, *out_v)
```

```python
import jax
import jax.numpy as jnp
from jax import lax
from jax.experimental import pallas as pl
from jax.experimental.pallas import tpu as pltpu

F32 = jnp.float32
BF16 = jnp.bfloat16
MESH = pl.DeviceIdType.MESH

D_MODEL = 1024
SB_HEADS = 16
HEAD_DIM = 64
SSD_WIDTH = 2048
SSD_GROUPS = 4
GROUP_WIDTH = SSD_WIDTH // SSD_GROUPS
HEADS_PER_GROUP = 8
SSD_STATE = 128
CHUNK = 128
CONV_K = 4
CONV_DIM = 3072
D_PROJ = 11296
EPS = 1e-6
ADAM_LR, ADAM_B1, ADAM_B2, ADAM_EPS, ADAM_WD, ADAM_STEP = 0.001, 0.9, 0.999, 1e-08, 0.01, 10

LANES = 128
HP_WIDTH = 4 * LANES
ZS0, GATE0, XBC0, DT0 = 4096, 6144, 8192, 11264
DT_PAD = 256
NP = DT0 + DT_PAD
N_CHIPS = 4
VMEM_LIMIT = 56 * 1024 * 1024


N_HP = SB_HEADS // 2
W_ZS0, W_XBC0, W_DT0, W_GATE0 = 4096, 6144, 9216, 9248


def _to_proj_layout(wt):
    d = wt.shape[1]
    pairs = wt[:W_ZS0].reshape(4, N_HP, LANES, d).transpose(1, 0, 2, 3).reshape(W_ZS0, d)
    return jnp.concatenate([pairs, wt[W_ZS0:W_XBC0], wt[W_GATE0:], wt[W_XBC0:W_DT0], wt[W_DT0:W_GATE0],
                            jnp.zeros((NP - D_PROJ, d), wt.dtype)], axis=0)


def _from_proj_layout(gt):
    d = gt.shape[1]
    qkvz = gt[:ZS0].reshape(N_HP, 4, LANES, d).transpose(1, 0, 2, 3).reshape(ZS0, d)
    return jnp.concatenate([qkvz, gt[ZS0:GATE0], gt[XBC0:DT0], gt[DT0:DT0 + W_GATE0 - W_DT0], gt[GATE0:XBC0]], axis=0)


def _cparams(*sem):
    return pltpu.CompilerParams(dimension_semantics=sem or None, vmem_limit_bytes=VMEM_LIMIT)


def _sigmoid(z):
    return 1.0 / (1.0 + jnp.exp(-z))


def _dot(a, b, dims, precision=None):
    return lax.dot_general(a, b, (dims, ((), ())), preferred_element_type=F32, precision=precision)


NN = ((1,), (0,))
NT = ((1,), (1,))
TN = ((0,), (0,))


def _matmul(a, b, *, ta=False, tb=False, out_dtype=F32, tm, tn, tk, name, exchange=None):
    m, k = (a.shape[1], a.shape[0]) if ta else a.shape
    n = b.shape[0] if tb else b.shape[1]
    assert m % tm == 0 and n % tn == 0 and k % tk == 0, (name, m, n, k)
    grid = (m // tm, n // tn, k // tk)
    nk = grid[2]
    use_scratch = out_dtype != F32
    dims = ((0,) if ta else (1,), (1,) if tb else (0,))
    n_in = len(exchange.inputs) if exchange else 0
    n_out = len(exchange.out_shapes) if exchange else 0

    def kern(a_ref, b_ref, *rest):
        x_in, o_ref, x_out, scratch = rest[:n_in], rest[n_in], rest[n_in + 1:n_in + 1 + n_out], rest[n_in + 1 + n_out:]
        acc = scratch[0] if use_scratch else o_ref
        step = [pl.program_id(ax) for ax in range(3)]
        if exchange:
            sems = scratch[1:] if use_scratch else scratch

            @pl.when(jnp.logical_and(jnp.logical_and(step[0] == 0, step[1] == 0), step[2] == 0))
            def _():
                exchange.start(x_in, x_out, sems)

        @pl.when(step[2] == 0)
        def _():
            acc[...] = jnp.zeros_like(acc)

        acc[...] += _dot(a_ref[...], b_ref[...], dims)
        if use_scratch:
            @pl.when(step[2] == nk - 1)
            def _():
                o_ref[...] = acc[...].astype(out_dtype)
        if exchange:
            @pl.when(jnp.logical_and(jnp.logical_and(step[0] == grid[0] - 1, step[1] == grid[1] - 1), step[2] == nk - 1))
            def _():
                exchange.finish(x_in, x_out, sems)

    a_spec = pl.BlockSpec((tk, tm), lambda i, j, q: (q, i)) if ta else pl.BlockSpec((tm, tk), lambda i, j, q: (i, q))
    b_spec = pl.BlockSpec((tn, tk), lambda i, j, q: (j, q)) if tb else pl.BlockSpec((tk, tn), lambda i, j, q: (q, j))
    out = pl.pallas_call(
        kern, name=name,
        out_shape=[jax.ShapeDtypeStruct((m, n), out_dtype)] + (list(exchange.out_shapes) if exchange else []),
        grid=grid,
        in_specs=[a_spec, b_spec] + [ANY] * n_in,
        out_specs=[pl.BlockSpec((tm, tn), lambda i, j, q: (i, j))] + [ANY] * n_out,
        scratch_shapes=([pltpu.VMEM((tm, tn), F32)] if use_scratch else []) + (list(exchange.sems) if exchange else []),
        compiler_params=_cparams("arbitrary", "arbitrary", "arbitrary") if exchange else _cparams("parallel", "parallel", "arbitrary"),
    )(a, b, *(exchange.inputs if exchange else []))
    return out if exchange else out[0]


ROWS = 256


def _rms_fwd(x2, w):
    t, d = x2.shape

    def kern(x_ref, w_ref, h_ref):
        x = x_ref[...]
        r = lax.rsqrt(jnp.mean(x * x, axis=-1, keepdims=True) + EPS)
        h_ref[...] = (x * r * w_ref[...]).astype(BF16)

    return pl.pallas_call(
        kern, name="rms_fwd",
        out_shape=jax.ShapeDtypeStruct((t, d), BF16),
        grid=(t // ROWS,),
        in_specs=[pl.BlockSpec((ROWS, d), lambda i: (i, 0)), pl.BlockSpec((1, d), lambda i: (0, 0))],
        out_specs=pl.BlockSpec((ROWS, d), lambda i: (i, 0)),
        compiler_params=_cparams("parallel"),
    )(x2, w)


def _rms_bwd(dh, x2, w, dout):
    t, d = x2.shape

    def kern(dh_ref, x_ref, w_ref, dout_ref, gx_ref, dw_ref):
        @pl.when(pl.program_id(0) == 0)
        def _():
            dw_ref[...] = jnp.zeros_like(dw_ref)

        x = x_ref[...]
        r = lax.rsqrt(jnp.mean(x * x, axis=-1, keepdims=True) + EPS)
        xh = x * r
        g = dh_ref[...]
        dw_ref[...] += jnp.sum(g * xh, axis=0, keepdims=True)
        gw = g * w_ref[...]
        gx_ref[...] = dout_ref[...] + r * (gw - xh * jnp.mean(gw * xh, axis=-1, keepdims=True))

    row = pl.BlockSpec((ROWS, d), lambda i: (i, 0))
    vec = pl.BlockSpec((1, d), lambda i: (0, 0))
    return pl.pallas_call(
        kern, name="rms_bwd",
        out_shape=(jax.ShapeDtypeStruct((t, d), F32), jax.ShapeDtypeStruct((1, d), F32)),
        grid=(t // ROWS,),
        in_specs=[row, row, vec, row],
        out_specs=(row, vec),
        compiler_params=_cparams("arbitrary"),
    )(dh, x2, w, dout)


def _final_fwd_bwd(x2, mo, target, w):
    t, d = x2.shape

    def kern(x_ref, mo_ref, t_ref, w_ref, dout_ref, doutb_ref, loss_ref, dw_ref):
        @pl.when(pl.program_id(0) == 0)
        def _():
            loss_ref[...] = jnp.zeros_like(loss_ref)
            dw_ref[...] = jnp.zeros_like(dw_ref)

        u = x_ref[...] + mo_ref[...]
        r = lax.rsqrt(jnp.mean(u * u, axis=-1, keepdims=True) + EPS)
        uh = u * r
        wv = w_ref[...]
        err = uh * wv - t_ref[...]
        loss_ref[...] += (0.5 / d) * jnp.sum(err * err)
        dy = err * (1.0 / d)
        dw_ref[...] += jnp.sum(dy * uh, axis=0, keepdims=True)
        gw = dy * wv
        du = r * (gw - uh * jnp.mean(gw * uh, axis=-1, keepdims=True))
        dout_ref[...] = du
        doutb_ref[...] = du.astype(BF16)

    row = pl.BlockSpec((ROWS, d), lambda i: (i, 0))
    vec = pl.BlockSpec((1, d), lambda i: (0, 0))
    return pl.pallas_call(
        kern, name="final_fwd_bwd",
        out_shape=(jax.ShapeDtypeStruct((t, d), F32), jax.ShapeDtypeStruct((t, d), BF16),
                   jax.ShapeDtypeStruct((1, LANES), F32), jax.ShapeDtypeStruct((1, d), F32)),
        grid=(t // ROWS,),
        in_specs=[row, row, row, vec],
        out_specs=(row, row, pl.BlockSpec((1, LANES), lambda i: (0, 0)), vec),
        compiler_params=_cparams("arbitrary"),
    )(x2, mo, target, w)


def _merge_fwd(proj2, ya, ys):
    t = ya.shape[0]
    gblk = GATE0 // D_MODEL

    def kern(ga_ref, gs_ref, ya_ref, ys_ref, o_ref):
        o_ref[...] = (_sigmoid(ga_ref[...]) * ya_ref[...] + _sigmoid(gs_ref[...]) * ys_ref[...]).astype(BF16)

    row = pl.BlockSpec((ROWS, D_MODEL), lambda i: (i, 0))
    return pl.pallas_call(
        kern, name="merge_fwd",
        out_shape=jax.ShapeDtypeStruct((t, D_MODEL), BF16),
        grid=(t // ROWS,),
        in_specs=[pl.BlockSpec((ROWS, D_MODEL), lambda i: (i, gblk)),
                  pl.BlockSpec((ROWS, D_MODEL), lambda i: (i, gblk + 1)), row, row],
        out_specs=row,
        compiler_params=_cparams("parallel"),
    )(proj2, proj2, ya, ys)


def _merge_bwd(dm, proj2, ya, ys):
    t = ya.shape[0]
    gblk = GATE0 // D_MODEL

    def kern(dm_ref, ga_ref, gs_ref, ya_ref, ys_ref, dya_ref, dys_ref, dg_ref):
        g = dm_ref[...]
        sa = _sigmoid(ga_ref[...])
        ss = _sigmoid(gs_ref[...])
        dya_ref[...] = (g * sa).astype(BF16)
        dys_ref[...] = (g * ss).astype(BF16)
        dg_ref[:, :D_MODEL] = (g * ya_ref[...] * sa * (1.0 - sa)).astype(BF16)
        dg_ref[:, D_MODEL:] = (g * ys_ref[...] * ss * (1.0 - ss)).astype(BF16)

    row = pl.BlockSpec((ROWS, D_MODEL), lambda i: (i, 0))
    return pl.pallas_call(
        kern, name="merge_bwd",
        out_shape=(jax.ShapeDtypeStruct((t, D_MODEL), BF16), jax.ShapeDtypeStruct((t, D_MODEL), BF16),
                   jax.ShapeDtypeStruct((t, NP), BF16)),
        grid=(t // ROWS,),
        in_specs=[row, pl.BlockSpec((ROWS, D_MODEL), lambda i: (i, gblk)),
                  pl.BlockSpec((ROWS, D_MODEL), lambda i: (i, gblk + 1)), row, row],
        out_specs=(row, row, pl.BlockSpec((ROWS, 2 * D_MODEL), lambda i: (i, GATE0 // (2 * D_MODEL)))),
        compiler_params=_cparams("parallel"),
    )(dm, proj2, proj2, ya, ys)


TQ = 256
TK = 256
HEAD_LANES = (slice(0, HEAD_DIM), slice(HEAD_DIM, 2 * HEAD_DIM))


def _tri(pred):
    r = lax.broadcasted_iota(jnp.int32, (TK, TK), 0)
    c = lax.broadcasted_iota(jnp.int32, (TK, TK), 1)
    return pred(r, c).astype(BF16)


def _split_bf16(v):
    hi = v.astype(BF16)
    lo = (v - hi.astype(F32)).astype(BF16)
    return hi, lo


def _tri_dot(v, tri):
    hi, lo = _split_bf16(v)
    return _dot(hi, tri, NN) + _dot(lo, tri, NN)


def _sb_logs(z, mask):
    l1p = jnp.log(1.0 + jnp.exp(-jnp.abs(z)))
    lb = jnp.minimum(z, 0.0) - l1p
    lom = -jnp.maximum(z, 0.0) - l1p
    if mask is not None:
        lom = jnp.where(mask, lom, 0.0)
    return lb, lom


def _sb_weights(lb, later, carry_r, mask):
    a = jnp.exp(lb + (later + carry_r))
    if mask is not None:
        a = jnp.where(mask, a, 0.0)
    return a


DEAD = -104.0


def _while_alive(n, carry, step):
    def alive(cr):
        return jnp.max(jnp.maximum(cr[0][0], cr[1][0])) > DEAD

    def cond(state):
        jj, go, _ = state
        return jnp.logical_and(jj < n, go)

    def body(state):
        jj, _, cr = state
        cr = step(jj, cr)
        return jj + 1, alive(cr), cr

    return lax.while_loop(cond, body, (jnp.int32(0), alive(carry), carry))[2]


Q_LANES, K_LANES, V_LANES, ZA_LANES = (slice(i * LANES, (i + 1) * LANES) for i in range(4))


def _split_heads(dst, src, scale=None):
    for h, lanes in enumerate(HEAD_LANES):
        v = src[:, lanes]
        dst[h] = (v if scale is None else v * scale).astype(BF16)


def _attn_fwd(proj3):
    b, s, _ = proj3.shape
    nq = s // TQ
    scale = HEAD_DIM ** -0.5

    def kern(x_ref, o_ref, yp_ref, qs, ks, vs):
        _split_heads(qs, x_ref[0, :, Q_LANES], scale)
        _split_heads(ks, x_ref[0, :, K_LANES])
        _split_heads(vs, x_ref[0, :, V_LANES])
        za_ref = x_ref.at[:, :, ZA_LANES]
        row = lax.broadcasted_iota(jnp.int32, (TQ, TK), 0)
        col = lax.broadcasted_iota(jnp.int32, (TQ, TK), 1)
        tri_gt = _tri(lambda j, sk: j > sk)

        def q_block(i, _):
            r0 = pl.multiple_of(i * TQ, TQ)
            n_kb = (r0 + TQ + TK - 1) // TK
            qh = [qs[h, pl.ds(r0, TQ), :] for h in range(2)]

            def k_block(c0, carry, mask):
                kh = [ks[h, pl.ds(c0, TK), :] for h in range(2)]
                vh = [vs[h, pl.ds(c0, TK), :] for h in range(2)]
                z = [_dot(qh[h], kh[h], NT) for h in range(2)]
                logs, later = [], []
                for h in range(2):
                    logs.append(_sb_logs(z[h], mask))
                    later.append(_tri_dot(logs[h][1], tri_gt))
                out = []
                for h in range(2):
                    carry_r, acc = carry[h]
                    lb, lom = logs[h]
                    a = _sb_weights(lb, later[h], carry_r, mask)
                    row_sum = later[h][:, 0:1] + lom[:, 0:1]
                    out.append((carry_r + row_sum, acc + _dot(a.astype(BF16), vh[h], NN)))
                return tuple(out)

            c_last = pl.multiple_of((n_kb - 1) * TK, TK)
            start = (jnp.zeros((TQ, 1), F32), jnp.zeros((TQ, HEAD_DIM), F32))
            carry = k_block(c_last, (start, start), col + c_last < row + r0)

            carry = _while_alive(n_kb - 1, carry, lambda jj, cr: k_block(pl.multiple_of((n_kb - 2 - jj) * TK, TK), cr, None))
            for (_, acc), lanes in zip(carry, HEAD_LANES):
                o_ref[0, pl.ds(r0, TQ), lanes] = acc
                za = za_ref[0, pl.ds(r0, TQ), lanes]
                yp_ref[0, pl.ds(r0, TQ), lanes] = (acc * (za * _sigmoid(za))).astype(BF16)
            return 0

        lax.fori_loop(0, nq, q_block, 0)

    out_spec = pl.BlockSpec((1, s, LANES), lambda bi, hp: (bi, 0, hp))
    return pl.pallas_call(
        kern, name="attn_fwd",
        out_shape=(jax.ShapeDtypeStruct((b, s, D_MODEL), F32), jax.ShapeDtypeStruct((b, s, D_MODEL), BF16)),
        grid=(b, SB_HEADS // 2),
        in_specs=[pl.BlockSpec((1, s, HP_WIDTH), lambda bi, hp: (bi, 0, hp))],
        out_specs=(out_spec, out_spec),
        scratch_shapes=[pltpu.VMEM((2, s, HEAD_DIM), BF16)] * 3,
        compiler_params=_cparams("parallel", "parallel"),
    )(proj3)


def _attn_bwd(proj3, dyp3, o3, dproj3):
    b, s, _ = proj3.shape
    nq = s // TQ
    scale = HEAD_DIM ** -0.5

    def kern(x_ref, dyp_ref, o_ref, _, d_ref, qs, ks, vs, dos, dk_acc, dv_acc):
        _split_heads(qs, x_ref[0, :, Q_LANES], scale)
        _split_heads(ks, x_ref[0, :, K_LANES])
        _split_heads(vs, x_ref[0, :, V_LANES])
        dq_ref, dk_ref, dv_ref = (d_ref.at[:, :, lanes] for lanes in (Q_LANES, K_LANES, V_LANES))
        za = x_ref[0, :, ZA_LANES]
        sg = _sigmoid(za)
        dyp = dyp_ref[0]
        _split_heads(dos, dyp * (za * sg))
        d_ref[0, :, ZA_LANES] = (dyp * o_ref[0] * (sg * (1.0 + za * (1.0 - sg)))).astype(BF16)
        dk_acc[...] = jnp.zeros_like(dk_acc)
        dv_acc[...] = jnp.zeros_like(dv_acc)
        row = lax.broadcasted_iota(jnp.int32, (TQ, TK), 0)
        col = lax.broadcasted_iota(jnp.int32, (TQ, TK), 1)
        tri_gt = _tri(lambda j, sk: j > sk)
        tri_ge = _tri(lambda j, sk: j >= sk)

        def q_block(i, _):
            r0 = pl.multiple_of(i * TQ, TQ)
            n_kb = (r0 + TQ + TK - 1) // TK
            qh = [qs[h, pl.ds(r0, TQ), :] for h in range(2)]
            doh = [dos[h, pl.ds(r0, TQ), :] for h in range(2)]
            totals = [jnp.sum(doh[h].astype(F32) * o_ref[0, pl.ds(r0, TQ), lanes], axis=1, keepdims=True)
                      for h, lanes in enumerate(HEAD_LANES)]

            def k_block(c0, carry, mask):
                kh = [ks[h, pl.ds(c0, TK), :] for h in range(2)]
                vh = [vs[h, pl.ds(c0, TK), :] for h in range(2)]
                z = [_dot(qh[h], kh[h], NT) for h in range(2)]
                da = [_dot(doh[h], vh[h], NT) for h in range(2)]
                logs, later = [], []
                for h in range(2):
                    logs.append(_sb_logs(z[h], mask))
                    later.append(_tri_dot(logs[h][1], tri_gt))
                ab, g, suffix = [], [], []
                for h in range(2):
                    a = _sb_weights(logs[h][0], later[h], carry[h][0], mask)
                    ab.append(a.astype(BF16))
                    g.append(da[h] * ab[h].astype(F32))
                    suffix.append(_tri_dot(g[h], tri_ge))
                out = []
                for h in range(2):
                    carry_r, carry_g, dq = carry[h]
                    lb, lom = logs[h]
                    dz = g[h] - (g[h] + (totals[h] - carry_g) - suffix[h]) * jnp.exp(lb)
                    if mask is not None:
                        dz = jnp.where(mask, dz, 0.0)
                    dzb = dz.astype(BF16)
                    dk_acc[h, pl.ds(c0, TK), :] += _dot(dzb, qh[h], TN)
                    dv_acc[h, pl.ds(c0, TK), :] += _dot(ab[h], doh[h], TN)
                    out.append((carry_r + (later[h][:, 0:1] + lom[:, 0:1]), carry_g + suffix[h][:, 0:1],
                                dq + _dot(dzb, kh[h], NN)))
                return tuple(out)

            c_last = pl.multiple_of((n_kb - 1) * TK, TK)
            zero = jnp.zeros((TQ, 1), F32)
            start = (zero, zero, jnp.zeros((TQ, HEAD_DIM), F32))
            carry = k_block(c_last, (start, start), col + c_last < row + r0)

            carry = _while_alive(n_kb - 1, carry, lambda jj, cr: k_block(pl.multiple_of((n_kb - 2 - jj) * TK, TK), cr, None))
            for (_, _, dq), lanes in zip(carry, HEAD_LANES):
                dq_ref[0, pl.ds(r0, TQ), lanes] = (dq * scale).astype(BF16)
            return 0

        lax.fori_loop(0, nq, q_block, 0)

        for h, lanes in enumerate(HEAD_LANES):
            dk_ref[0, :, lanes] = dk_acc[h].astype(BF16)
            dv_ref[0, :, lanes] = dv_acc[h].astype(BF16)

    plain = pl.BlockSpec((1, s, LANES), lambda bi, hp: (bi, 0, hp))
    pair = pl.BlockSpec((1, s, HP_WIDTH), lambda bi, hp: (bi, 0, hp))
    return pl.pallas_call(
        kern, name="attn_bwd",
        out_shape=jax.ShapeDtypeStruct(dproj3.shape, dproj3.dtype),
        grid=(b, SB_HEADS // 2),
        in_specs=[pair, plain, plain, ANY],
        out_specs=pair,
        input_output_aliases={3: 0},
        scratch_shapes=[pltpu.VMEM((2, s, HEAD_DIM), BF16)] * 4 + [pltpu.VMEM((2, s, HEAD_DIM), F32)] * 2,
        compiler_params=_cparams("parallel", "parallel"),
    )(proj3, dyp3, o3, dproj3)


CONV_COLS = 256
HALO = 8


def _conv_pre(xp, w_ref, b_ref, r0):
    pre = b_ref[...] + w_ref[CONV_K - 1:CONV_K, :] * xp[pl.ds(HALO + r0, CHUNK), :]
    for kk in range(1, CONV_K):
        pre = pre + w_ref[CONV_K - 1 - kk:CONV_K - kk, :] * xp[pl.ds(HALO + r0 - kk, CHUNK), :]
    return pre


def _conv_fwd(proj3, conv_w, conv_b):
    b, s, _ = proj3.shape
    nc = s // CHUNK

    def kern(x_ref, w_ref, b_ref, o_ref, xp):
        xp[0:HALO, :] = jnp.zeros((HALO, CONV_COLS), F32)
        xp[HALO:, :] = x_ref[0]
        for ci in range(nc):
            pre = _conv_pre(xp, w_ref, b_ref, ci * CHUNK)
            o_ref[0, ci * CHUNK:(ci + 1) * CHUNK, :] = pre * _sigmoid(pre)

    return pl.pallas_call(
        kern, name="conv_fwd",
        out_shape=jax.ShapeDtypeStruct((b, s, CONV_DIM), F32),
        grid=(CONV_DIM // CONV_COLS, b),
        in_specs=[pl.BlockSpec((1, s, CONV_COLS), lambda j, bi: (bi, 0, XBC0 // CONV_COLS + j)),
                  pl.BlockSpec((CONV_K, CONV_COLS), lambda j, bi: (0, j)),
                  pl.BlockSpec((1, CONV_COLS), lambda j, bi: (0, j))],
        out_specs=pl.BlockSpec((1, s, CONV_COLS), lambda j, bi: (bi, 0, j)),
        scratch_shapes=[pltpu.VMEM((s + HALO, CONV_COLS), F32)],
        compiler_params=_cparams("parallel", "parallel"),
    )(proj3, conv_w, conv_b)


def _conv_bwd(dact, proj3, conv_w, conv_b, col0, name, dproj3):
    b, s, width = dact.shape
    nc = s // CHUNK
    j0 = col0 // CONV_COLS

    def kern(da_ref, x_ref, w_ref, b_ref, _, dx_ref, dw_ref, db_ref, xp, dp):
        @pl.when(pl.program_id(1) == 0)
        def _():
            dw_ref[...] = jnp.zeros_like(dw_ref)
            db_ref[...] = jnp.zeros_like(db_ref)

        xp[0:HALO, :] = jnp.zeros((HALO, CONV_COLS), F32)
        xp[HALO:, :] = x_ref[0]
        dp[s:, :] = jnp.zeros((HALO, CONV_COLS), F32)
        for ci in range(nc):
            r0 = ci * CHUNK
            pre = _conv_pre(xp, w_ref, b_ref, r0)
            sg = _sigmoid(pre)
            dpre = da_ref[0, r0:r0 + CHUNK, :] * (sg * (1.0 + pre * (1.0 - sg)))
            dp[r0:r0 + CHUNK, :] = dpre
            db_ref[...] += jnp.sum(dpre, axis=0, keepdims=True)
            for kk in range(CONV_K):
                tap = CONV_K - 1 - kk
                dw_ref[tap:tap + 1, :] += jnp.sum(dpre * xp[pl.ds(HALO + r0 - kk, CHUNK), :], axis=0, keepdims=True)
        for ci in range(nc):
            r0 = ci * CHUNK
            dx = w_ref[CONV_K - 1:CONV_K, :] * dp[pl.ds(r0, CHUNK), :]
            for kk in range(1, CONV_K):
                dx = dx + w_ref[CONV_K - 1 - kk:CONV_K - kk, :] * dp[pl.ds(r0 + kk, CHUNK), :]
            dx_ref[0, r0:r0 + CHUNK, :] = dx.astype(BF16)

    return pl.pallas_call(
        kern, name=name,
        out_shape=(jax.ShapeDtypeStruct(dproj3.shape, dproj3.dtype), jax.ShapeDtypeStruct((CONV_K, width), F32),
                   jax.ShapeDtypeStruct((1, width), F32)),
        grid=(width // CONV_COLS, b),
        in_specs=[pl.BlockSpec((1, s, CONV_COLS), lambda j, bi: (bi, 0, j)),
                  pl.BlockSpec((1, s, CONV_COLS), lambda j, bi: (bi, 0, XBC0 // CONV_COLS + j0 + j)),
                  pl.BlockSpec((CONV_K, CONV_COLS), lambda j, bi: (0, j0 + j)),
                  pl.BlockSpec((1, CONV_COLS), lambda j, bi: (0, j0 + j)), ANY],
        out_specs=(pl.BlockSpec((1, s, CONV_COLS), lambda j, bi: (bi, 0, XBC0 // CONV_COLS + j0 + j)),
                   pl.BlockSpec((CONV_K, CONV_COLS), lambda j, bi: (0, j)),
                   pl.BlockSpec((1, CONV_COLS), lambda j, bi: (0, j))),
        input_output_aliases={4: 0},
        scratch_shapes=[pltpu.VMEM((s + HALO, CONV_COLS), F32)] * 2,
        compiler_params=_cparams("parallel", "arbitrary"),
    )(dact, proj3, conv_w, conv_b, dproj3)


def _sel_dot(v, sel, left=False):
    hi = v.astype(BF16)
    rest = v - hi.astype(F32)
    mid = rest.astype(BF16)
    lo = (rest - mid.astype(F32)).astype(BF16)
    if left:
        return _dot(sel, hi, NN) + _dot(sel, mid, NN) + _dot(sel, lo, NN)
    return _dot(hi, sel, NN) + _dot(mid, sel, NN) + _dot(lo, sel, NN)


def _ssd_common(dtr_ref, dtb_ref, alog_ref):
    lane = lax.broadcasted_iota(jnp.int32, (CHUNK, LANES), 1)
    row = lax.broadcasted_iota(jnp.int32, (CHUNK, LANES), 0)
    head_lane = lane < HEADS_PER_GROUP
    pre = dtr_ref[0, 0] + dtb_ref[0]
    dt = jnp.where(head_lane, jnp.maximum(pre, 0.0) + jnp.log(1.0 + jnp.exp(-jnp.abs(pre))), 0.0)
    a = jnp.where(head_lane[0:1], -jnp.exp(alog_ref[0]), 0.0)
    tril = (row >= lane).astype(BF16)
    acs = _sel_dot(dt * a, tril, left=True)
    acs_t = acs.T
    er = lax.broadcasted_iota(jnp.int32, (LANES, GROUP_WIDTH), 0)
    ec = lax.broadcasted_iota(jnp.int32, (LANES, GROUP_WIDTH), 1)
    expand = ((ec // HEAD_DIM) == er).astype(BF16)
    tr = lax.broadcasted_iota(jnp.int32, (GROUP_WIDTH, LANES), 0)
    tc = lax.broadcasted_iota(jnp.int32, (GROUP_WIDTH, LANES), 1)
    reduce = ((tr // HEAD_DIM) == tc).astype(BF16)
    dt_x = _sel_dot(dt, expand)
    acs_x = _sel_dot(acs, expand)
    end_x = acs_x[CHUNK - 1:CHUNK, :]
    causal = row >= lane
    return dict(dt=dt, a=a, pre=pre, head_lane=head_lane, acs=acs, acs_t=acs_t, expand=expand, reduce=reduce,
                dt_x=dt_x, acs_x=acs_x, end_x=end_x, causal=causal, row=row, lane=lane)


def _ssd_decay(cm, h):
    seg = cm["acs"][:, h:h + 1] - cm["acs_t"][h:h + 1, :]
    return jnp.where(cm["causal"], jnp.exp(jnp.minimum(seg, 0.0)), 0.0)


def _ssd_fwd(xact, proj3, dtr_g, dtb_g, alog_g, dskip_x, snw):
    b, s, _ = xact.shape
    nc = s // CHUNK
    g4 = SSD_GROUPS

    def kern(xs_ref, bm_ref, cm_ref, zs_ref, dtr_ref, dtb_ref, alog_ref, dsk_ref, snw_ref,
             y_ref, yn_ref, hst_ref, h_sc):
        @pl.when(pl.program_id(2) == 0)
        def _():
            h_sc[...] = jnp.zeros_like(h_sc)

        cm = _ssd_common(dtr_ref, dtb_ref, alog_ref)
        x = xs_ref[0]
        bmb = bm_ref[0].astype(BF16)
        cmb = cm_ref[0].astype(BF16)
        h_in = h_sc[...]
        hst_ref[0, 0, 0] = h_in
        xdt = x * cm["dt_x"]
        xdtb = xdt.astype(BF16)
        cb = _dot(cmb, bmb, NT)
        y_off = _dot(cmb, h_in.astype(BF16), NN) * jnp.exp(cm["acs_x"])
        for h in range(HEADS_PER_GROUP):
            lanes = slice(h * HEAD_DIM, (h + 1) * HEAD_DIM)
            m = (cb * _ssd_decay(cm, h)).astype(BF16)
            y_ref[0, :, lanes] = _dot(m, xdtb[:, lanes], NN)
        y = y_ref[0] + y_off + x * dsk_ref[...]
        y_ref[0] = y
        w = (xdt * jnp.exp(cm["end_x"] - cm["acs_x"])).astype(BF16)
        h_sc[...] = h_in * jnp.exp(cm["end_x"]) + _dot(bmb, w, TN)
        zs = zs_ref[0]
        y2 = y * (zs * _sigmoid(zs))
        yn_ref[0] = (y2 * lax.rsqrt(jnp.mean(y2 * y2, axis=-1, keepdims=True) + EPS) * snw_ref[...]).astype(BF16)

    gw = GROUP_WIDTH
    small = pl.BlockSpec((1, 1, LANES), lambda gi, bi, ci: (gi, 0, 0))
    xblk = pl.BlockSpec((1, CHUNK, gw), lambda gi, bi, ci: (bi, ci, gi))
    return pl.pallas_call(
        kern, name="ssd_fwd",
        out_shape=(jax.ShapeDtypeStruct((b, s, SSD_WIDTH), F32), jax.ShapeDtypeStruct((b, s, SSD_WIDTH), BF16),
                   jax.ShapeDtypeStruct((b, nc, g4, SSD_STATE, gw), F32)),
        grid=(g4, b, nc),
        in_specs=[xblk,
                  pl.BlockSpec((1, CHUNK, LANES), lambda gi, bi, ci: (bi, ci, SSD_WIDTH // LANES + gi)),
                  pl.BlockSpec((1, CHUNK, LANES), lambda gi, bi, ci: (bi, ci, SSD_WIDTH // LANES + g4 + gi)),
                  pl.BlockSpec((1, CHUNK, gw), lambda gi, bi, ci: (bi, ci, ZS0 // gw + gi)),
                  pl.BlockSpec((1, 1, CHUNK, LANES), lambda gi, bi, ci: (bi, gi, ci, 0)),
                  small, small,
                  pl.BlockSpec((1, gw), lambda gi, bi, ci: (0, gi)),
                  pl.BlockSpec((1, gw), lambda gi, bi, ci: (0, gi))],
        out_specs=(xblk, xblk, pl.BlockSpec((1, 1, 1, SSD_STATE, gw), lambda gi, bi, ci: (bi, ci, gi, 0, 0))),
        scratch_shapes=[pltpu.VMEM((SSD_STATE, gw), F32)],
        compiler_params=_cparams("parallel", "parallel", "arbitrary"),
    )(xact, xact, xact, proj3, dtr_g, dtb_g, alog_g, dskip_x, snw)


def _ssd_bwd(dyn3, y3, xact, proj3, hst, dtr_g, dtb_g, alog_g, dskip_x, snw, dproj3):
    b, s, _ = xact.shape
    nc = s // CHUNK
    g4 = SSD_GROUPS
    gw = GROUP_WIDTH

    def kern(dyn_ref, y_ref, xs_ref, bm_ref, cm_ref, zs_ref, hst_ref, dtr_ref, dtb_ref, alog_ref, dsk_ref, snw_ref, _,
             dxs_ref, dbm_ref, dcm_ref, dzs_ref, ddtr_ref, dsnw_ref, dalog_ref, ddtb_ref, ddsk_ref, dh_sc):
        first = jnp.logical_and(pl.program_id(1) == 0, pl.program_id(2) == 0)

        @pl.when(first)
        def _():
            dsnw_ref[...] = jnp.zeros_like(dsnw_ref)
            dalog_ref[...] = jnp.zeros_like(dalog_ref)
            ddtb_ref[...] = jnp.zeros_like(ddtb_ref)
            ddsk_ref[...] = jnp.zeros_like(ddsk_ref)

        @pl.when(pl.program_id(2) == 0)
        def _():
            dh_sc[...] = jnp.zeros_like(dh_sc)

        cm = _ssd_common(dtr_ref, dtb_ref, alog_ref)
        row, lane = cm["row"], cm["lane"]
        y = y_ref[0]
        zs = zs_ref[0]
        sg = _sigmoid(zs)
        silu = zs * sg
        y2 = y * silu
        rstd = lax.rsqrt(jnp.mean(y2 * y2, axis=-1, keepdims=True) + EPS)
        y2h = y2 * rstd
        dyn = dyn_ref[0]
        dsnw_ref[0] += jnp.sum(dyn * y2h, axis=0, keepdims=True)
        gwv = dyn * snw_ref[...]
        dy2 = rstd * (gwv - y2h * jnp.mean(gwv * y2h, axis=-1, keepdims=True))
        dzs_ref[0] = (dy2 * y * (sg * (1.0 + zs * (1.0 - sg)))).astype(BF16)
        dy = dy2 * silu
        dyb = dy.astype(BF16)

        x = xs_ref[0]
        bmb = bm_ref[0].astype(BF16)
        cmb = cm_ref[0].astype(BF16)
        h_in = hst_ref[0, 0, 0]
        h_inb = h_in.astype(BF16)
        d_hn = dh_sc[...]
        d_hnb = d_hn.astype(BF16)
        xdt = x * cm["dt_x"]
        xdtb = xdt.astype(BF16)
        eacs = jnp.exp(cm["acs_x"])
        dte = jnp.exp(cm["end_x"] - cm["acs_x"])
        wb = (xdt * dte).astype(BF16)

        dsk_lanes = jnp.broadcast_to(jnp.sum(dy * x, axis=0, keepdims=True), (8, gw))
        ddsk_ref[0] += _sel_dot(dsk_lanes, cm["reduce"])[0:1, :]
        dyo = dy * eacs
        dyob = dyo.astype(BF16)
        dacs_x = dyo * _dot(cmb, h_inb, NN)
        dcm = _dot(dyob, h_inb, NT)
        dh_in = _dot(cmb, dyob, TN)
        dw = _dot(bmb, d_hnb, NN)
        dbm = _dot(wb, d_hnb, NT)
        dxdt = dw * dte
        e_l = dw * xdt * dte
        dacs_x = dacs_x - e_l
        dend_x = jnp.sum(e_l, axis=0, keepdims=True)
        chunk_decay = jnp.exp(cm["end_x"])
        dh_sc[...] = d_hn * chunk_decay + dh_in
        dend_x = dend_x + jnp.sum(d_hn * h_in, axis=0, keepdims=True) * chunk_decay
        last_row = lax.broadcasted_iota(jnp.int32, (CHUNK, gw), 0) == CHUNK - 1
        dacs_x = dacs_x + jnp.where(last_row, dend_x, 0.0)

        cb = _dot(cmb, bmb, NT)
        dcb = jnp.zeros((CHUNK, CHUNK), F32)
        dacs = jnp.zeros((CHUNK, LANES), F32)
        dacs_t = jnp.zeros((LANES, CHUNK), F32)
        for h in range(HEADS_PER_GROUP):
            lanes = slice(h * HEAD_DIM, (h + 1) * HEAD_DIM)
            decay = _ssd_decay(cm, h)
            m = cb * decay
            dm = _dot(dyb[:, lanes], xdtb[:, lanes], NT)
            dxs_ref[0, :, lanes] = _dot(m.astype(BF16), dyb[:, lanes], TN)
            dcb_h = dm * decay
            dcb = dcb + dcb_h
            n = dcb_h * cb
            dacs = dacs + jnp.where(lane == h, jnp.sum(n, axis=1, keepdims=True), 0.0)
            dacs_t = dacs_t + jnp.where(row == h, jnp.sum(n, axis=0, keepdims=True), 0.0)
        dcbb = dcb.astype(BF16)
        dcm_ref[0] = dcm + _dot(dcbb, bmb, NN)
        dbm_ref[0] = dbm + _dot(dcbb, cmb, TN)
        dxdt = dxdt + dxs_ref[0]
        dxs_ref[0] = dy * dsk_ref[...] + dxdt * cm["dt_x"]

        dacs = dacs - dacs_t.T + _sel_dot(dacs_x, cm["reduce"])
        ddt = _sel_dot(dxdt * x, cm["reduce"])
        triu = (row <= lane).astype(BF16)
        rc = _sel_dot(dacs, triu, left=True)
        ddt = ddt + cm["a"] * rc
        dalog_ref[0] += jnp.sum(cm["dt"] * rc, axis=0, keepdims=True) * cm["a"]
        ddtr = jnp.where(cm["head_lane"], ddt * _sigmoid(cm["pre"]), 0.0)
        ddtr_ref[0, 0] = ddtr
        ddtb_ref[0] += jnp.sum(ddtr, axis=0, keepdims=True)

    def rev(ci):
        return nc - 1 - ci

    small = pl.BlockSpec((1, 1, LANES), lambda gi, bi, ci: (gi, 0, 0))
    xblk = pl.BlockSpec((1, CHUNK, gw), lambda gi, bi, ci: (bi, rev(ci), gi))
    nblk = pl.BlockSpec((1, CHUNK, LANES), lambda gi, bi, ci: (bi, rev(ci), gi))
    gvec = pl.BlockSpec((1, gw), lambda gi, bi, ci: (0, gi))
    gacc = pl.BlockSpec((1, 1, gw), lambda gi, bi, ci: (gi, 0, 0))
    return pl.pallas_call(
        kern, name="ssd_bwd",
        out_shape=(jax.ShapeDtypeStruct((b, s, SSD_WIDTH), F32),
                   jax.ShapeDtypeStruct((b, s, g4 * SSD_STATE), F32),
                   jax.ShapeDtypeStruct((b, s, g4 * SSD_STATE), F32),
                   jax.ShapeDtypeStruct(dproj3.shape, dproj3.dtype),
                   jax.ShapeDtypeStruct((b, g4, s, LANES), F32),
                   jax.ShapeDtypeStruct((g4, 1, gw), F32),
                   jax.ShapeDtypeStruct((g4, 1, LANES), F32),
                   jax.ShapeDtypeStruct((g4, 1, LANES), F32),
                   jax.ShapeDtypeStruct((g4, 1, LANES), F32)),
        grid=(g4, b, nc),
        in_specs=[xblk, xblk, xblk,
                  pl.BlockSpec((1, CHUNK, LANES), lambda gi, bi, ci: (bi, rev(ci), SSD_WIDTH // LANES + gi)),
                  pl.BlockSpec((1, CHUNK, LANES), lambda gi, bi, ci: (bi, rev(ci), SSD_WIDTH // LANES + g4 + gi)),
                  pl.BlockSpec((1, CHUNK, gw), lambda gi, bi, ci: (bi, rev(ci), ZS0 // gw + gi)),
                  pl.BlockSpec((1, 1, 1, SSD_STATE, gw), lambda gi, bi, ci: (bi, rev(ci), gi, 0, 0)),
                  pl.BlockSpec((1, 1, CHUNK, LANES), lambda gi, bi, ci: (bi, gi, rev(ci), 0)),
                  small, small, gvec, gvec, ANY],
        out_specs=(xblk, nblk, nblk,
                   pl.BlockSpec((1, CHUNK, gw), lambda gi, bi, ci: (bi, rev(ci), ZS0 // gw + gi)),
                   pl.BlockSpec((1, 1, CHUNK, LANES), lambda gi, bi, ci: (bi, gi, rev(ci), 0)),
                   gacc, small, small, small),
        input_output_aliases={12: 3},
        scratch_shapes=[pltpu.VMEM((SSD_STATE, gw), F32)],
        compiler_params=_cparams("parallel", "arbitrary", "arbitrary"),
    )(dyn3, y3, xact, xact, xact, proj3, hst, dtr_g, dtb_g, alog_g, dskip_x, snw, dproj3)


def _adamw(w, g, m, v, name):
    r, c = w.shape
    tr = 128 if r % 128 == 0 else r
    tc = LANES if (tr == r and r > 128 and c % LANES == 0) else c

    def kern(w_ref, g_ref, m_ref, v_ref, d_ref, nm_ref, nv_ref):
        gv = g_ref[...]
        nm = ADAM_B1 * m_ref[...] + (1.0 - ADAM_B1) * gv
        nv = ADAM_B2 * v_ref[...] + (1.0 - ADAM_B2) * (gv * gv)
        m_hat = nm / (1.0 - ADAM_B1 ** ADAM_STEP)
        v_hat = nv / (1.0 - ADAM_B2 ** ADAM_STEP)
        d_ref[...] = -ADAM_LR * (m_hat / (jnp.sqrt(v_hat) + ADAM_EPS) + ADAM_WD * w_ref[...])
        nm_ref[...] = nm
        nv_ref[...] = nv

    blk = pl.BlockSpec((tr, tc), lambda i, j: (i, j))
    out = jax.ShapeDtypeStruct((r, c), F32)
    return pl.pallas_call(
        kern, name=name, out_shape=(out, out, out), grid=(r // tr, c // tc),
        in_specs=[blk] * 4, out_specs=(blk, blk, blk),
        compiler_params=_cparams("parallel", "parallel"),
    )(w, g, m, v)


ANY = pl.BlockSpec(memory_space=pl.ANY)


def _position():
    return lax.axis_index("x"), lax.axis_index("y"), lax.axis_index("c")


def _other_chips(x, y):
    return [(1 - x, y), (x, 1 - y), (1 - x, 1 - y)]


def _dma_sems(n):
    return [pltpu.SemaphoreType.DMA((n,)), pltpu.SemaphoreType.DMA((n,))]


class _Exchange:
    def __init__(self, inputs, out_shapes, sems, start, finish):
        self.inputs, self.out_shapes, self.sems, self.start, self.finish = inputs, out_shapes, sems, start, finish


def _run_exchange(ex, name):
    n_in, n_out = len(ex.inputs), len(ex.out_shapes)

    def body(*refs):
        x_in, x_out, sems = refs[:n_in], refs[n_in:n_in + n_out], refs[n_in + n_out:]
        ex.start(x_in, x_out, sems)
        ex.finish(x_in, x_out, sems)

    return pl.pallas_call(
        body, name=name, out_shape=list(ex.out_shapes),
        in_specs=[ANY] * n_in, out_specs=[ANY] * n_out, scratch_shapes=list(ex.sems),
    )(*ex.inputs)


def _gather_exchange(shards):
    n = len(shards)

    def copies(p_refs, out_refs, sems):
        send_sems, recv_sems = sems
        x, y, c = _position()
        me = 2 * x + y
        chips = _other_chips(x, y)

        def slab(a, chip, hf):
            half = shards[a].shape[1] // 2
            return out_refs[a].at[chip, :, pl.ds(hf * half, half)]

        def my_half(a):
            half = shards[a].shape[1] // 2
            return p_refs[a].at[:, pl.ds(c * half, half)]

        def over_ici(a, j, chip_from):
            px, py = chips[j]
            return pltpu.make_async_remote_copy(
                src_ref=my_half(a), dst_ref=slab(a, chip_from, c),
                send_sem=send_sems.at[3 * a + j], recv_sem=recv_sems.at[3 * a + j],
                device_id=(px, py, c), device_id_type=MESH)

        def to_sibling(a, j, hf):
            px, py = chips[j]
            return pltpu.make_async_remote_copy(
                src_ref=slab(a, 2 * px + py, hf), dst_ref=slab(a, 2 * px + py, hf),
                send_sem=send_sems.at[3 * (n + a) + j], recv_sem=recv_sems.at[3 * (n + a) + j],
                device_id=(x, y, 1 - c), device_id_type=MESH)

        own = [pltpu.make_async_remote_copy(
            src_ref=p_refs[a], dst_ref=out_refs[a].at[me], send_sem=send_sems.at[6 * n + a], recv_sem=recv_sems.at[6 * n + a],
            device_id=(x, y, 1 - c), device_id_type=MESH) for a in range(n)]
        first = [over_ici(a, j, me) for a in range(n) for j in range(3)]
        return chips, c, over_ici, to_sibling, first, own

    def start(p_refs, out_refs, sems):
        _, _, _, _, first, own = copies(p_refs, out_refs, sems)
        for cp in first + own:
            cp.start()

    def finish(p_refs, out_refs, sems):
        chips, c, over_ici, to_sibling, first, own = copies(p_refs, out_refs, sems)
        passed = []
        for a in range(n):
            for j, (px, py) in enumerate(chips):
                over_ici(a, j, 2 * px + py).wait_recv()
                passed.append(to_sibling(a, j, c))
                passed[-1].start()
        for a in range(n):
            for j in range(3):
                to_sibling(a, j, 1 - c).wait_recv()
        for cp in first + passed:
            cp.wait_send()
        for cp in own:
            cp.wait()

    return _Exchange(list(shards), [jax.ShapeDtypeStruct((N_CHIPS, *v.shape), v.dtype) for v in shards],
                     _dma_sems(7 * n), start, finish)


def _swap_halves(parts):
    n = len(parts)

    def body(*refs):
        v_refs, out_refs = refs[:n], refs[n:2 * n]
        send_sems, recv_sems = refs[2 * n:]
        x, y, c = _position()
        copies = []
        for a in range(n):
            half = parts[a].shape[2] // 2
            copies.append(pltpu.make_async_remote_copy(
                src_ref=v_refs[a].at[:, :, pl.ds((1 - c) * half, half)], dst_ref=out_refs[a],
                send_sem=send_sems.at[a], recv_sem=recv_sems.at[a], device_id=(x, y, 1 - c), device_id_type=MESH))
        for cp in copies:
            cp.start()
        for cp in copies:
            cp.wait()

    return pl.pallas_call(
        body, name="grad_swap_halves",
        out_shape=[jax.ShapeDtypeStruct((v.shape[0], v.shape[1], v.shape[2] // 2), v.dtype) for v in parts],
        in_specs=[ANY] * n, out_specs=[ANY] * n,
        scratch_shapes=_dma_sems(n),
    )(*parts)


def _all_to_all_exchange(parts):
    n = len(parts)

    def sends(p_refs, out_refs, sems):
        send_sems, recv_sems = sems
        x, y, c = _position()
        return [pltpu.make_async_remote_copy(
            src_ref=p_refs[a].at[2 * px + py], dst_ref=out_refs[a].at[j],
            send_sem=send_sems.at[3 * a + j], recv_sem=recv_sems.at[3 * a + j],
            device_id=(px, py, c), device_id_type=MESH) for a in range(n) for j, (px, py) in enumerate(_other_chips(x, y))]

    def start(p_refs, out_refs, sems):
        for cp in sends(p_refs, out_refs, sems):
            cp.start()

    def finish(p_refs, out_refs, sems):
        for cp in sends(p_refs, out_refs, sems):
            cp.wait()

    return _Exchange(list(parts), [jax.ShapeDtypeStruct((N_CHIPS - 1, *v.shape[1:]), v.dtype) for v in parts],
                     _dma_sems(3 * n), start, finish)


def _join_halves(wholes):
    n = len(wholes)

    def body(*refs):
        out_refs = refs[n:2 * n]
        send_sems, recv_sems = refs[2 * n:]
        x, y, c = _position()
        copies = []
        for a in range(n):
            half = wholes[a].shape[1] // 2
            mine = out_refs[a].at[:, pl.ds(c * half, half)]
            copies.append(pltpu.make_async_remote_copy(
                src_ref=mine, dst_ref=mine, send_sem=send_sems.at[a], recv_sem=recv_sems.at[a],
                device_id=(x, y, 1 - c), device_id_type=MESH))
        for cp in copies:
            cp.start()
        for cp in copies:
            cp.wait()

    return pl.pallas_call(
        body, name="grad_join_halves",
        out_shape=[jax.ShapeDtypeStruct(v.shape, v.dtype) for v in wholes],
        in_specs=[ANY] * n, out_specs=[ANY] * n,
        input_output_aliases={a: a for a in range(n)},
        scratch_shapes=_dma_sems(n),
    )(*wholes)


STRIP = 256


def _add_halves(g, sw, place, name):
    n, rows, cols = g.shape
    nb = cols // 2 // STRIP

    def kern(p_ref, g_ref, s_ref, o_ref):
        o_ref[...] = (g_ref[...] + s_ref[...]).astype(BF16)

    blk = pl.BlockSpec((1, rows, STRIP), lambda j, i, p_ref: (j, 0, i))
    return pl.pallas_call(
        kern, name=name,
        out_shape=jax.ShapeDtypeStruct((n, rows, cols // 2), BF16),
        grid_spec=pltpu.PrefetchScalarGridSpec(
            num_scalar_prefetch=1, grid=(n, nb),
            in_specs=[pl.BlockSpec((1, rows, STRIP), lambda j, i, p_ref: (j, 0, p_ref[0] * nb + i)), blk],
            out_specs=blk),
        compiler_params=_cparams("parallel", "parallel"),
    )(place, g, sw)


def _sum_chips(own, rx, place, name):
    _, rows, half = rx.shape
    nb = half // STRIP

    def kern(p_ref, own_ref, r_ref, o_ref):
        total = own_ref[0].astype(F32)
        for j in range(N_CHIPS - 1):
            total = total + r_ref[j].astype(F32)
        o_ref[...] = total

    return pl.pallas_call(
        kern, name=name,
        out_shape=jax.ShapeDtypeStruct((rows, 2 * half), F32),
        grid_spec=pltpu.PrefetchScalarGridSpec(
            num_scalar_prefetch=1, grid=(nb,),
            in_specs=[pl.BlockSpec((1, rows, STRIP), lambda i, p_ref: (p_ref[1], 0, i)),
                      pl.BlockSpec((N_CHIPS - 1, rows, STRIP), lambda i, p_ref: (0, 0, i))],
            out_specs=pl.BlockSpec((rows, STRIP), lambda i, p_ref: (0, p_ref[0] * nb + i))),
        compiler_params=_cparams("parallel"),
    )(place, own, rx)


def _gather_small(v, reduce, name):
    rows = v.shape[0]

    def body(v_ref, out_ref, buf, send_sems, recv_sems):
        x, y, c = _position()
        me = 4 * x + 2 * y + c
        buf[me] = v_ref[...]
        peers = [(x ^ (k >> 2), y ^ ((k >> 1) & 1), c ^ (k & 1)) for k in range(1, 8)]
        copies = [pltpu.make_async_remote_copy(
            src_ref=v_ref, dst_ref=buf.at[me],
            send_sem=send_sems.at[k], recv_sem=recv_sems.at[k],
            device_id=peer, device_id_type=MESH) for k, peer in enumerate(peers)]
        for cp in copies:
            cp.start()
        for k, (px, py, pc) in enumerate(peers):
            pltpu.make_async_remote_copy(
                src_ref=v_ref, dst_ref=buf.at[4 * px + 2 * py + pc],
                send_sem=send_sems.at[k], recv_sem=recv_sems.at[k],
                device_id=(px, py, pc), device_id_type=MESH).wait_recv()
        for cp in copies:
            cp.wait_send()
        if reduce:
            total = buf[0]
            for d in range(1, 8):
                total = total + buf[d]
            out_ref[...] = total
        else:
            out_ref[...] = buf[...]

    vm = pl.BlockSpec(memory_space=pltpu.VMEM)
    return pl.pallas_call(
        body, name=name,
        out_shape=jax.ShapeDtypeStruct((rows, LANES) if reduce else (8, rows, LANES), F32),
        in_specs=[vm], out_specs=vm,
        scratch_shapes=[pltpu.VMEM((8, rows, LANES), F32), pltpu.SemaphoreType.DMA((7,)), pltpu.SemaphoreType.DMA((7,))],
    )(v)


def _pad_rows(a, rows):
    return jnp.pad(a, ((0, rows - a.shape[0]), (0, 0)))


def _lane_pad(v):
    n = v.shape[1]
    return jnp.pad(v, ((0, 0), (0, -n % LANES)))


def _gather_all(w_in, w_attn_out, w_ssm_out, w_o, conv_w):
    d = D_MODEL
    w_in_t, = _run_exchange(_gather_exchange([w_in[0].T.astype(BF16)]), "gather_w_in")
    w_proj_t = _to_proj_layout(w_in_t.reshape(D_PROJ, d))
    out_w = _gather_exchange([a[0].astype(BF16) for a in (w_attn_out, w_ssm_out, w_o)])
    conv_rows = conv_w[0].size // LANES
    conv_all = _gather_small(conv_w[0].reshape(conv_rows, LANES), False, "gather_conv_w")
    conv_w_all = conv_all[0::2].reshape(N_CHIPS, CONV_K, CONV_DIM // N_CHIPS).transpose(1, 0, 2).reshape(CONV_K, CONV_DIM)

    return w_proj_t, out_w, conv_w_all


def _local_step(x, loss_target, norm_w, w_proj_t, conv_w_all, conv_b, dt_bias, a_log, d_skip, ssm_norm_w,
                out_w, final_norm_w, grad_exchange=None):
    b, s, d = x.shape
    t = b * s
    g4, hg = SSD_GROUPS, HEADS_PER_GROUP
    dtb_g = _lane_pad(dt_bias.reshape(g4, hg)).reshape(g4, 1, LANES)
    alog_g = _lane_pad(a_log.reshape(g4, hg)).reshape(g4, 1, LANES)
    dskip_x = jnp.repeat(d_skip, HEAD_DIM, axis=1)
    fnw = final_norm_w.reshape(1, d)

    x2 = x.reshape(t, d)
    h = _rms_fwd(x2, norm_w)
    big_tm = min(t, 2048)
    if isinstance(out_w, _Exchange):
        proj, *out_w = _matmul(h, w_proj_t, tb=True, tm=big_tm, tn=1280, tk=1024, name="proj", exchange=out_w)
    else:
        proj = _matmul(h, w_proj_t, tb=True, tm=big_tm, tn=1280, tk=1024, name="proj")
    w_ao, w_so, w_oo = (w.reshape(-1, d) for w in out_w)
    proj3 = proj.reshape(b, s, NP)
    o3, yp3 = _attn_fwd(proj3)
    xact = _conv_fwd(proj3, conv_w_all, conv_b)
    dtr = proj3[:, :, DT0:DT0 + g4 * hg].reshape(b, s, g4, hg).transpose(0, 2, 1, 3)
    dtr_g = jnp.pad(dtr, ((0, 0), (0, 0), (0, 0), (0, LANES - hg)))
    y3, yn3, hst = _ssd_fwd(xact, proj3, dtr_g, dtb_g, alog_g, dskip_x, ssm_norm_w)
    yp = yp3.reshape(t, D_MODEL)
    yn = yn3.reshape(t, SSD_WIDTH)
    ya = _matmul(yp, w_ao, tm=512, tn=1024, tk=1024, name="attn_out")
    ys = _matmul(yn, w_so, tm=512, tn=1024, tk=2048, name="ssm_out")
    merged = _merge_fwd(proj, ya, ys)
    mo = _matmul(merged, w_oo, tm=512, tn=1024, tk=1024, name="out_proj")
    dout, doutb, loss_part, d_fnw = _final_fwd_bwd(x2, mo, loss_target.reshape(t, d), fnw)

    dmerged = _matmul(doutb, w_oo, tb=True, tm=512, tn=1024, tk=1024, name="d_merged")
    g_wo = _matmul(merged, doutb, ta=True, tm=512, tn=1024, tk=1024, name="g_w_o")
    dya, dys, dproj = _merge_bwd(dmerged, proj, ya, ys)
    dyp = _matmul(dya, w_ao, tb=True, tm=512, tn=1024, tk=1024, name="d_attn_pre")
    g_wao = _matmul(yp, dya, ta=True, tm=512, tn=1024, tk=1024, name="g_w_attn_out")
    dyn = _matmul(dys, w_so, tb=True, tm=1024, tn=2048, tk=1024, name="d_ssm_norm")
    g_wso = _matmul(yn, dys, ta=True, tm=1024, tn=1024, tk=1024, name="g_w_ssm_out")
    dproj3 = _attn_bwd(proj3, dyp.reshape(b, s, D_MODEL), o3, dproj.reshape(b, s, NP))
    (dxs, dbm, dcm, dproj3, ddtr_g, d_snw_g, d_alog_g, d_dtb_g, d_dsk_g) = _ssd_bwd(
        dyn.reshape(b, s, SSD_WIDTH), y3, xact, proj3, hst, dtr_g, dtb_g, alog_g, dskip_x, ssm_norm_w, dproj3)
    dproj3, g_cw_xs, g_cb_xs = _conv_bwd(dxs, proj3, conv_w_all, conv_b, 0, "conv_bwd_x", dproj3)
    dproj3, g_cw_bm, g_cb_bm = _conv_bwd(dbm, proj3, conv_w_all, conv_b, SSD_WIDTH, "conv_bwd_b", dproj3)
    dproj3, g_cw_cm, g_cb_cm = _conv_bwd(dcm, proj3, conv_w_all, conv_b, SSD_WIDTH + g4 * SSD_STATE, "conv_bwd_c", dproj3)
    ddt = ddtr_g[:, :, :, :hg].transpose(0, 2, 1, 3).reshape(b, s, g4 * hg).astype(BF16)
    ddt = jnp.pad(ddt, ((0, 0), (0, 0), (0, DT_PAD - g4 * hg)))
    dproj = lax.dynamic_update_slice(dproj3, ddt, (0, 0, DT0)).reshape(t, NP)
    g_wproj = _matmul(dproj, h, ta=True, tm=1280, tn=1024, tk=1024, name="g_w_in")
    exchange = grad_exchange(g_wproj, g_wao, g_wso, g_wo) if grad_exchange else None
    exchanged = []
    if exchange:
        dh, *exchanged = _matmul(dproj, w_proj_t, tm=big_tm, tn=1024, tk=1280, name="d_h", exchange=exchange)
    else:
        dh = _matmul(dproj, w_proj_t, tm=big_tm, tn=1024, tk=1280, name="d_h")
    grad_x, d_nw = _rms_bwd(dh, x2, norm_w, dout)
    g_cw = jnp.concatenate([g_cw_xs, g_cw_bm, g_cw_cm], axis=1)
    g_cb = jnp.concatenate([g_cb_xs, g_cb_bm, g_cb_cm], axis=1)
    return (loss_part, grad_x, d_nw, g_wproj, g_cw, g_cb, d_dtb_g, d_alog_g, d_dsk_g, d_snw_g, g_wao, g_wso, g_wo, d_fnw,
            exchanged)


def kernel(x, norm_w, w_in, conv_w, conv_b, dt_bias, a_log, d_skip, ssm_norm_w, w_attn_out, w_ssm_out, w_o, final_norm_w, loss_target, m_norm_w, m_w_in, m_conv_w, m_conv_b, m_dt_bias, m_a_log, m_d_skip, m_ssm_norm_w, m_w_attn_out, m_w_ssm_out, m_w_o, m_final_norm_w, v_norm_w, v_w_in, v_conv_w, v_conv_b, v_dt_bias, v_a_log, v_d_skip, v_ssm_norm_w, v_w_attn_out, v_w_ssm_out, v_w_o, v_final_norm_w):
    b, s, d = x.shape
    core = lax.axis_index("c")
    g4, hg = SSD_GROUPS, HEADS_PER_GROUP
    shard_cols = w_in.shape[2]
    w_proj_t, out_w, conv_w_all = _gather_all(w_in, w_attn_out, w_ssm_out, w_o, conv_w)
    chip = 2 * lax.axis_index("x") + lax.axis_index("y")
    place = jnp.stack([core, chip]).astype(jnp.int32)
    chip_sums = []

    def grad_exchange(g_wproj, g_wao, g_wso, g_wo):
        g_win_chips = _from_proj_layout(g_wproj).reshape(N_CHIPS, shard_cols, d)
        g_out_chips = jnp.concatenate([g.reshape(N_CHIPS, -1, d) for g in (g_wao, g_wso, g_wo)], axis=1)
        parts = [g_win_chips, g_out_chips]
        from_sibling = _swap_halves(parts)
        chip_sums.extend(_add_halves(p, f, place, "grad_add_halves_%d" % i) for i, (p, f) in enumerate(zip(parts, from_sibling)))
        return _all_to_all_exchange(chip_sums)

    (loss_part, grad_x, d_nw, _, g_cw, g_cb, d_dtb_g, d_alog_g, d_dsk_g, d_snw_g, _, _, _, d_fnw, from_chips) = _local_step(
        x, loss_target, norm_w, w_proj_t, conv_w_all, conv_b, dt_bias, a_log, d_skip, ssm_norm_w, out_w, final_norm_w,
        grad_exchange)
    wholes = [_sum_chips(o, r, place, "grad_sum_chips_%d" % i) for i, (o, r) in enumerate(zip(chip_sums, from_chips))]
    g_w_in, g_out = _join_halves(wholes)

    small = jnp.concatenate([
        loss_part, d_nw, g_cb, _lane_pad(d_dtb_g[:, 0, :hg].reshape(1, -1)), _lane_pad(d_alog_g[:, 0, :hg].reshape(1, -1)),
        _lane_pad(d_dsk_g[:, 0, :hg].reshape(1, -1)),
        d_snw_g.reshape(1, -1), d_fnw, g_cw.reshape(1, -1)], axis=1)
    small_rows = small.shape[1] // LANES
    reduced = _gather_small(_pad_rows(small.reshape(small_rows, LANES), -(-small_rows // 8) * 8), True, "reduce_small")
    flat = reduced.reshape(-1)

    def take(start, n):
        return flat[start:start + n].reshape(1, n)

    loss = flat[0]
    pos = LANES
    g_norm_w = take(pos, d); pos += d
    g_conv_b = take(pos, CONV_DIM); pos += CONV_DIM
    g_dt_bias = take(pos, g4 * hg); pos += LANES
    g_a_log = take(pos, g4 * hg); pos += LANES
    g_d_skip = take(pos, g4 * hg); pos += LANES
    g_ssm_norm_w = take(pos, SSD_WIDTH); pos += SSD_WIDTH
    g_final_norm_w = take(pos, d); pos += d
    conv_cols = CONV_DIM // N_CHIPS
    g_conv_w = lax.dynamic_slice_in_dim(flat[pos:pos + CONV_K * CONV_DIM].reshape(CONV_K, CONV_DIM), chip * conv_cols, conv_cols, axis=1)

    rows_ao, rows_so = D_MODEL // N_CHIPS, SSD_WIDTH // N_CHIPS
    g_w_attn_out = g_out[:rows_ao]
    g_w_ssm_out = g_out[rows_ao:rows_ao + rows_so]
    g_w_o = g_out[rows_ao + rows_so:]

    names = ["norm_w", "w_in", "conv_w", "conv_b", "dt_bias", "a_log", "d_skip", "ssm_norm_w",
             "w_attn_out", "w_ssm_out", "w_o", "final_norm_w"]
    weights = [norm_w, w_in, conv_w, conv_b, dt_bias, a_log, d_skip, ssm_norm_w, w_attn_out, w_ssm_out, w_o, final_norm_w]
    grads = [g_norm_w, g_w_in, g_conv_w, g_conv_b, g_dt_bias, g_a_log, g_d_skip, g_ssm_norm_w,
             g_w_attn_out, g_w_ssm_out, g_w_o, g_final_norm_w]
    ms = [m_norm_w, m_w_in, m_conv_w, m_conv_b, m_dt_bias, m_a_log, m_d_skip, m_ssm_norm_w,
          m_w_attn_out, m_w_ssm_out, m_w_o, m_final_norm_w]
    vs = [v_norm_w, v_w_in, v_conv_w, v_conv_b, v_dt_bias, v_a_log, v_d_skip, v_ssm_norm_w,
          v_w_attn_out, v_w_ssm_out, v_w_o, v_final_norm_w]
    out_g, out_d, out_m, out_v = [], [], [], []
    for name, w, g, m, v in zip(names, weights, grads, ms, vs):
        if name == "w_in":
            to2, back = (lambda a: a[0].T), (lambda a: a.T.reshape(w.shape))
        else:
            to2, back = (lambda a: a.reshape(g.shape)), (lambda a: a.reshape(w.shape))
        dlt, nm, nv = _adamw(to2(w), g, to2(m), to2(v), "adamw_" + name)
        out_g.append(back(g))
        out_d.append(back(dlt))
        out_m.append(back(nm))
        out_v.append(back(nv))

    return (loss, grad_x.reshape(b, s, d), *out_g, *out_d, *out_m, *out_v)
```

```python
import jax
import jax.numpy as jnp
from jax import lax
from jax.experimental import pallas as pl
from jax.experimental.pallas import tpu as pltpu

F32 = jnp.float32
BF16 = jnp.bfloat16
MESH = pl.DeviceIdType.MESH

D_MODEL = 1024
SB_HEADS = 16
HEAD_DIM = 64
SSD_WIDTH = 2048
SSD_GROUPS = 4
GROUP_WIDTH = SSD_WIDTH // SSD_GROUPS
HEADS_PER_GROUP = 8
SSD_STATE = 128
CHUNK = 128
CONV_K = 4
CONV_DIM = 3072
D_PROJ = 11296
EPS = 1e-6
ADAM_LR, ADAM_B1, ADAM_B2, ADAM_EPS, ADAM_WD, ADAM_STEP = 0.001, 0.9, 0.999, 1e-08, 0.01, 10

LANES = 128
HP_WIDTH = 4 * LANES
ZS0, GATE0, XBC0, DT0 = 4096, 6144, 8192, 11264
DT_PAD = 256
NP = DT0 + DT_PAD
N_CHIPS = 4
VMEM_LIMIT = 56 * 1024 * 1024


N_HP = SB_HEADS // 2
W_ZS0, W_XBC0, W_DT0, W_GATE0 = 4096, 6144, 9216, 9248


def _to_proj_layout(wt):
    d = wt.shape[1]
    pairs = wt[:W_ZS0].reshape(4, N_HP, LANES, d).transpose(1, 0, 2, 3).reshape(W_ZS0, d)
    return jnp.concatenate([pairs, wt[W_ZS0:W_XBC0], wt[W_GATE0:], wt[W_XBC0:W_DT0], wt[W_DT0:W_GATE0],
                            jnp.zeros((NP - D_PROJ, d), wt.dtype)], axis=0)


def _from_proj_layout(gt):
    d = gt.shape[1]
    qkvz = gt[:ZS0].reshape(N_HP, 4, LANES, d).transpose(1, 0, 2, 3).reshape(ZS0, d)
    return jnp.concatenate([qkvz, gt[ZS0:GATE0], gt[XBC0:DT0], gt[DT0:DT0 + W_GATE0 - W_DT0], gt[GATE0:XBC0]], axis=0)


def _cparams(*sem):
    return pltpu.CompilerParams(dimension_semantics=sem or None, vmem_limit_bytes=VMEM_LIMIT)


def _sigmoid(z):
    return 1.0 / (1.0 + jnp.exp(-z))


def _dot(a, b, dims, precision=None):
    return lax.dot_general(a, b, (dims, ((), ())), preferred_element_type=F32, precision=precision)


NN = ((1,), (0,))
NT = ((1,), (1,))
TN = ((0,), (0,))


def _matmul(a, b, *, ta=False, tb=False, out_dtype=F32, tm, tn, tk, name, exchange=None):
    m, k = (a.shape[1], a.shape[0]) if ta else a.shape
    n = b.shape[0] if tb else b.shape[1]
    assert m % tm == 0 and n % tn == 0 and k % tk == 0, (name, m, n, k)
    grid = (m // tm, n // tn, k // tk)
    nk = grid[2]
    use_scratch = out_dtype != F32
    dims = ((0,) if ta else (1,), (1,) if tb else (0,))
    n_in = len(exchange.inputs) if exchange else 0
    n_out = len(exchange.out_shapes) if exchange else 0

    def kern(a_ref, b_ref, *rest):
        x_in, o_ref, x_out, scratch = rest[:n_in], rest[n_in], rest[n_in + 1:n_in + 1 + n_out], rest[n_in + 1 + n_out:]
        acc = scratch[0] if use_scratch else o_ref
        step = [pl.program_id(ax) for ax in range(3)]
        if exchange:
            sems = scratch[1:] if use_scratch else scratch

            @pl.when(jnp.logical_and(jnp.logical_and(step[0] == 0, step[1] == 0), step[2] == 0))
            def _():
                exchange.start(x_in, x_out, sems)

        @pl.when(step[2] == 0)
        def _():
            acc[...] = jnp.zeros_like(acc)

        acc[...] += _dot(a_ref[...], b_ref[...], dims)
        if use_scratch:
            @pl.when(step[2] == nk - 1)
            def _():
                o_ref[...] = acc[...].astype(out_dtype)
        if exchange:
            @pl.when(jnp.logical_and(jnp.logical_and(step[0] == grid[0] - 1, step[1] == grid[1] - 1), step[2] == nk - 1))
            def _():
                exchange.finish(x_in, x_out, sems)

    a_spec = pl.BlockSpec((tk, tm), lambda i, j, q: (q, i)) if ta else pl.BlockSpec((tm, tk), lambda i, j, q: (i, q))
    b_spec = pl.BlockSpec((tn, tk), lambda i, j, q: (j, q)) if tb else pl.BlockSpec((tk, tn), lambda i, j, q: (q, j))
    out = pl.pallas_call(
        kern, name=name,
        out_shape=[jax.ShapeDtypeStruct((m, n), out_dtype)] + (list(exchange.out_shapes) if exchange else []),
        grid=grid,
        in_specs=[a_spec, b_spec] + [ANY] * n_in,
        out_specs=[pl.BlockSpec((tm, tn), lambda i, j, q: (i, j))] + [ANY] * n_out,
        scratch_shapes=([pltpu.VMEM((tm, tn), F32)] if use_scratch else []) + (list(exchange.sems) if exchange else []),
        compiler_params=_cparams("arbitrary", "arbitrary", "arbitrary") if exchange else _cparams("parallel", "parallel", "arbitrary"),
    )(a, b, *(exchange.inputs if exchange else []))
    return out if exchange else out[0]


ROWS = 256


def _rms_fwd(x2, w):
    t, d = x2.shape

    def kern(x_ref, w_ref, h_ref):
        x = x_ref[...]
        r = lax.rsqrt(jnp.mean(x * x, axis=-1, keepdims=True) + EPS)
        h_ref[...] = (x * r * w_ref[...]).astype(BF16)

    return pl.pallas_call(
        kern, name="rms_fwd",
        out_shape=jax.ShapeDtypeStruct((t, d), BF16),
        grid=(t // ROWS,),
        in_specs=[pl.BlockSpec((ROWS, d), lambda i: (i, 0)), pl.BlockSpec((1, d), lambda i: (0, 0))],
        out_specs=pl.BlockSpec((ROWS, d), lambda i: (i, 0)),
        compiler_params=_cparams("parallel"),
    )(x2, w)


def _rms_bwd(dh, x2, w, dout):
    t, d = x2.shape

    def kern(dh_ref, x_ref, w_ref, dout_ref, gx_ref, dw_ref):
        @pl.when(pl.program_id(0) == 0)
        def _():
            dw_ref[...] = jnp.zeros_like(dw_ref)

        x = x_ref[...]
        r = lax.rsqrt(jnp.mean(x * x, axis=-1, keepdims=True) + EPS)
        xh = x * r
        g = dh_ref[...]
        dw_ref[...] += jnp.sum(g * xh, axis=0, keepdims=True)
        gw = g * w_ref[...]
        gx_ref[...] = dout_ref[...] + r * (gw - xh * jnp.mean(gw * xh, axis=-1, keepdims=True))

    row = pl.BlockSpec((ROWS, d), lambda i: (i, 0))
    vec = pl.BlockSpec((1, d), lambda i: (0, 0))
    return pl.pallas_call(
        kern, name="rms_bwd",
        out_shape=(jax.ShapeDtypeStruct((t, d), F32), jax.ShapeDtypeStruct((1, d), F32)),
        grid=(t // ROWS,),
        in_specs=[row, row, vec, row],
        out_specs=(row, vec),
        compiler_params=_cparams("arbitrary"),
    )(dh, x2, w, dout)


def _final_fwd_bwd(x2, mo, target, w):
    t, d = x2.shape

    def kern(x_ref, mo_ref, t_ref, w_ref, dout_ref, doutb_ref, loss_ref, dw_ref):
        @pl.when(pl.program_id(0) == 0)
        def _():
            loss_ref[...] = jnp.zeros_like(loss_ref)
            dw_ref[...] = jnp.zeros_like(dw_ref)

        u = x_ref[...] + mo_ref[...]
        r = lax.rsqrt(jnp.mean(u * u, axis=-1, keepdims=True) + EPS)
        uh = u * r
        wv = w_ref[...]
        err = uh * wv - t_ref[...]
        loss_ref[...] += (0.5 / d) * jnp.sum(err * err)
        dy = err * (1.0 / d)
        dw_ref[...] += jnp.sum(dy * uh, axis=0, keepdims=True)
        gw = dy * wv
        du = r * (gw - uh * jnp.mean(gw * uh, axis=-1, keepdims=True))
        dout_ref[...] = du
        doutb_ref[...] = du.astype(BF16)

    row = pl.BlockSpec((ROWS, d), lambda i: (i, 0))
    vec = pl.BlockSpec((1, d), lambda i: (0, 0))
    return pl.pallas_call(
        kern, name="final_fwd_bwd",
        out_shape=(jax.ShapeDtypeStruct((t, d), F32), jax.ShapeDtypeStruct((t, d), BF16),
                   jax.ShapeDtypeStruct((1, LANES), F32), jax.ShapeDtypeStruct((1, d), F32)),
        grid=(t // ROWS,),
        in_specs=[row, row, row, vec],
        out_specs=(row, row, pl.BlockSpec((1, LANES), lambda i: (0, 0)), vec),
        compiler_params=_cparams("arbitrary"),
    )(x2, mo, target, w)


def _merge_fwd(proj2, ya, ys):
    t = ya.shape[0]
    gblk = GATE0 // D_MODEL

    def kern(ga_ref, gs_ref, ya_ref, ys_ref, o_ref):
        o_ref[...] = (_sigmoid(ga_ref[...]) * ya_ref[...] + _sigmoid(gs_ref[...]) * ys_ref[...]).astype(BF16)

    row = pl.BlockSpec((ROWS, D_MODEL), lambda i: (i, 0))
    return pl.pallas_call(
        kern, name="merge_fwd",
        out_shape=jax.ShapeDtypeStruct((t, D_MODEL), BF16),
        grid=(t // ROWS,),
        in_specs=[pl.BlockSpec((ROWS, D_MODEL), lambda i: (i, gblk)),
                  pl.BlockSpec((ROWS, D_MODEL), lambda i: (i, gblk + 1)), row, row],
        out_specs=row,
        compiler_params=_cparams("parallel"),
    )(proj2, proj2, ya, ys)


def _merge_bwd(dm, proj2, ya, ys):
    t = ya.shape[0]
    gblk = GATE0 // D_MODEL

    def kern(dm_ref, ga_ref, gs_ref, ya_ref, ys_ref, dya_ref, dys_ref, dg_ref):
        g = dm_ref[...]
        sa = _sigmoid(ga_ref[...])
        ss = _sigmoid(gs_ref[...])
        dya_ref[...] = (g * sa).astype(BF16)
        dys_ref[...] = (g * ss).astype(BF16)
        dg_ref[:, :D_MODEL] = (g * ya_ref[...] * sa * (1.0 - sa)).astype(BF16)
        dg_ref[:, D_MODEL:] = (g * ys_ref[...] * ss * (1.0 - ss)).astype(BF16)

    row = pl.BlockSpec((ROWS, D_MODEL), lambda i: (i, 0))
    return pl.pallas_call(
        kern, name="merge_bwd",
        out_shape=(jax.ShapeDtypeStruct((t, D_MODEL), BF16), jax.ShapeDtypeStruct((t, D_MODEL), BF16),
                   jax.ShapeDtypeStruct((t, NP), BF16)),
        grid=(t // ROWS,),
        in_specs=[row, pl.BlockSpec((ROWS, D_MODEL), lambda i: (i, gblk)),
                  pl.BlockSpec((ROWS, D_MODEL), lambda i: (i, gblk + 1)), row, row],
        out_specs=(row, row, pl.BlockSpec((ROWS, 2 * D_MODEL), lambda i: (i, GATE0 // (2 * D_MODEL)))),
        compiler_params=_cparams("parallel"),
    )(dm, proj2, proj2, ya, ys)


TQ = 256
TK = 256
HEAD_LANES = (slice(0, HEAD_DIM), slice(HEAD_DIM, 2 * HEAD_DIM))


def _tri(pred):
    r = lax.broadcasted_iota(jnp.int32, (TK, TK), 0)
    c = lax.broadcasted_iota(jnp.int32, (TK, TK), 1)
    return pred(r, c).astype(BF16)


def _split_bf16(v):
    hi = v.astype(BF16)
    lo = (v - hi.astype(F32)).astype(BF16)
    return hi, lo


def _tri_dot(v, tri):
    hi, lo = _split_bf16(v)
    return _dot(hi, tri, NN) + _dot(lo, tri, NN)


def _sb_logs(z, mask):
    l1p = jnp.log(1.0 + jnp.exp(-jnp.abs(z)))
    lb = jnp.minimum(z, 0.0) - l1p
    lom = -jnp.maximum(z, 0.0) - l1p
    if mask is not None:
        lom = jnp.where(mask, lom, 0.0)
    return lb, lom


def _sb_weights(lb, later, carry_r, mask):
    a = jnp.exp(lb + (later + carry_r))
    if mask is not None:
        a = jnp.where(mask, a, 0.0)
    return a


DEAD = -104.0


def _while_alive(n, carry, step):
    def alive(cr):
        return jnp.max(jnp.maximum(cr[0][0], cr[1][0])) > DEAD

    def cond(state):
        jj, go, _ = state
        return jnp.logical_and(jj < n, go)

    def body(state):
        jj, _, cr = state
        cr = step(jj, cr)
        return jj + 1, alive(cr), cr

    return lax.while_loop(cond, body, (jnp.int32(0), alive(carry), carry))[2]


Q_LANES, K_LANES, V_LANES, ZA_LANES = (slice(i * LANES, (i + 1) * LANES) for i in range(4))


def _split_heads(dst, src, scale=None):
    for h, lanes in enumerate(HEAD_LANES):
        v = src[:, lanes]
        dst[h] = (v if scale is None else v * scale).astype(BF16)


def _attn_fwd(proj3):
    b, s, _ = proj3.shape
    nq = s // TQ
    scale = HEAD_DIM ** -0.5

    def kern(x_ref, o_ref, yp_ref, qs, ks, vs):
        _split_heads(qs, x_ref[0, :, Q_LANES], scale)
        _split_heads(ks, x_ref[0, :, K_LANES])
        _split_heads(vs, x_ref[0, :, V_LANES])
        za_ref = x_ref.at[:, :, ZA_LANES]
        row = lax.broadcasted_iota(jnp.int32, (TQ, TK), 0)
        col = lax.broadcasted_iota(jnp.int32, (TQ, TK), 1)
        tri_gt = _tri(lambda j, sk: j > sk)

        def q_block(i, _):
            r0 = pl.multiple_of(i * TQ, TQ)
            n_kb = (r0 + TQ + TK - 1) // TK
            qh = [qs[h, pl.ds(r0, TQ), :] for h in range(2)]

            def k_block(c0, carry, mask):
                kh = [ks[h, pl.ds(c0, TK), :] for h in range(2)]
                vh = [vs[h, pl.ds(c0, TK), :] for h in range(2)]
                z = [_dot(qh[h], kh[h], NT) for h in range(2)]
                logs, later = [], []
                for h in range(2):
                    logs.append(_sb_logs(z[h], mask))
                    later.append(_tri_dot(logs[h][1], tri_gt))
                out = []
                for h in range(2):
                    carry_r, acc = carry[h]
                    lb, lom = logs[h]
                    a = _sb_weights(lb, later[h], carry_r, mask)
                    row_sum = later[h][:, 0:1] + lom[:, 0:1]
                    out.append((carry_r + row_sum, acc + _dot(a.astype(BF16), vh[h], NN)))
                return tuple(out)

            c_last = pl.multiple_of((n_kb - 1) * TK, TK)
            start = (jnp.zeros((TQ, 1), F32), jnp.zeros((TQ, HEAD_DIM), F32))
            carry = k_block(c_last, (start, start), col + c_last < row + r0)

            carry = _while_alive(n_kb - 1, carry, lambda jj, cr: k_block(pl.multiple_of((n_kb - 2 - jj) * TK, TK), cr, None))
            for (_, acc), lanes in zip(carry, HEAD_LANES):
                o_ref[0, pl.ds(r0, TQ), lanes] = acc
                za = za_ref[0, pl.ds(r0, TQ), lanes]
                yp_ref[0, pl.ds(r0, TQ), lanes] = (acc * (za * _sigmoid(za))).astype(BF16)
            return 0

        lax.fori_loop(0, nq, q_block, 0)

    out_spec = pl.BlockSpec((1, s, LANES), lambda bi, hp: (bi, 0, hp))
    return pl.pallas_call(
        kern, name="attn_fwd",
        out_shape=(jax.ShapeDtypeStruct((b, s, D_MODEL), F32), jax.ShapeDtypeStruct((b, s, D_MODEL), BF16)),
        grid=(b, SB_HEADS // 2),
        in_specs=[pl.BlockSpec((1, s, HP_WIDTH), lambda bi, hp: (bi, 0, hp))],
        out_specs=(out_spec, out_spec),
        scratch_shapes=[pltpu.VMEM((2, s, HEAD_DIM), BF16)] * 3,
        compiler_params=_cparams("parallel", "parallel"),
    )(proj3)


def _attn_bwd(proj3, dyp3, o3, dproj3):
    b, s, _ = proj3.shape
    nq = s // TQ
    scale = HEAD_DIM ** -0.5

    def kern(x_ref, dyp_ref, o_ref, _, d_ref, qs, ks, vs, dos, dk_acc, dv_acc):
        _split_heads(qs, x_ref[0, :, Q_LANES], scale)
        _split_heads(ks, x_ref[0, :, K_LANES])
        _split_heads(vs, x_ref[0, :, V_LANES])
        dq_ref, dk_ref, dv_ref = (d_ref.at[:, :, lanes] for lanes in (Q_LANES, K_LANES, V_LANES))
        za = x_ref[0, :, ZA_LANES]
        sg = _sigmoid(za)
        dyp = dyp_ref[0]
        _split_heads(dos, dyp * (za * sg))
        d_ref[0, :, ZA_LANES] = (dyp * o_ref[0] * (sg * (1.0 + za * (1.0 - sg)))).astype(BF16)
        dk_acc[...] = jnp.zeros_like(dk_acc)
        dv_acc[...] = jnp.zeros_like(dv_acc)
        row = lax.broadcasted_iota(jnp.int32, (TQ, TK), 0)
        col = lax.broadcasted_iota(jnp.int32, (TQ, TK), 1)
        tri_gt = _tri(lambda j, sk: j > sk)
        tri_ge = _tri(lambda j, sk: j >= sk)

        def q_block(i, _):
            r0 = pl.multiple_of(i * TQ, TQ)
            n_kb = (r0 + TQ + TK - 1) // TK
            qh = [qs[h, pl.ds(r0, TQ), :] for h in range(2)]
            doh = [dos[h, pl.ds(r0, TQ), :] for h in range(2)]
            totals = [jnp.sum(doh[h].astype(F32) * o_ref[0, pl.ds(r0, TQ), lanes], axis=1, keepdims=True)
                      for h, lanes in enumerate(HEAD_LANES)]

            def k_block(c0, carry, mask):
                kh = [ks[h, pl.ds(c0, TK), :] for h in range(2)]
                vh = [vs[h, pl.ds(c0, TK), :] for h in range(2)]
                z = [_dot(qh[h], kh[h], NT) for h in range(2)]
                da = [_dot(doh[h], vh[h], NT) for h in range(2)]
                logs, later = [], []
                for h in range(2):
                    logs.append(_sb_logs(z[h], mask))
                    later.append(_tri_dot(logs[h][1], tri_gt))
                ab, g, suffix = [], [], []
                for h in range(2):
                    a = _sb_weights(logs[h][0], later[h], carry[h][0], mask)
                    ab.append(a.astype(BF16))
                    g.append(da[h] * ab[h].astype(F32))
                    suffix.append(_tri_dot(g[h], tri_ge))
                out = []
                for h in range(2):
                    carry_r, carry_g, dq = carry[h]
                    lb, lom = logs[h]
                    dz = g[h] - (g[h] + (totals[h] - carry_g) - suffix[h]) * jnp.exp(lb)
                    if mask is not None:
                        dz = jnp.where(mask, dz, 0.0)
                    dzb = dz.astype(BF16)
                    dk_acc[h, pl.ds(c0, TK), :] += _dot(dzb, qh[h], TN)
                    dv_acc[h, pl.ds(c0, TK), :] += _dot(ab[h], doh[h], TN)
                    out.append((carry_r + (later[h][:, 0:1] + lom[:, 0:1]), carry_g + suffix[h][:, 0:1],
                                dq + _dot(dzb, kh[h], NN)))
                return tuple(out)

            c_last = pl.multiple_of((n_kb - 1) * TK, TK)
            zero = jnp.zeros((TQ, 1), F32)
            start = (zero, zero, jnp.zeros((TQ, HEAD_DIM), F32))
            carry = k_block(c_last, (start, start), col + c_last < row + r0)

            carry = _while_alive(n_kb - 1, carry, lambda jj, cr: k_block(pl.multiple_of((n_kb - 2 - jj) * TK, TK), cr, None))
            for (_, _, dq), lanes in zip(carry, HEAD_LANES):
                dq_ref[0, pl.ds(r0, TQ), lanes] = (dq * scale).astype(BF16)
            return 0

        lax.fori_loop(0, nq, q_block, 0)

        for h, lanes in enumerate(HEAD_LANES):
            dk_ref[0, :, lanes] = dk_acc[h].astype(BF16)
            dv_ref[0, :, lanes] = dv_acc[h].astype(BF16)

    plain = pl.BlockSpec((1, s, LANES), lambda bi, hp: (bi, 0, hp))
    pair = pl.BlockSpec((1, s, HP_WIDTH), lambda bi, hp: (bi, 0, hp))
    return pl.pallas_call(
        kern, name="attn_bwd",
        out_shape=jax.ShapeDtypeStruct(dproj3.shape, dproj3.dtype),
        grid=(b, SB_HEADS // 2),
        in_specs=[pair, plain, plain, ANY],
        out_specs=pair,
        input_output_aliases={3: 0},
        scratch_shapes=[pltpu.VMEM((2, s, HEAD_DIM), BF16)] * 4 + [pltpu.VMEM((2, s, HEAD_DIM), F32)] * 2,
        compiler_params=_cparams("parallel", "parallel"),
    )(proj3, dyp3, o3, dproj3)


CONV_COLS = 256
HALO = 8


def _conv_pre(xp, w_ref, b_ref, r0):
    pre = b_ref[...] + w_ref[CONV_K - 1:CONV_K, :] * xp[pl.ds(HALO + r0, CHUNK), :]
    for kk in range(1, CONV_K):
        pre = pre + w_ref[CONV_K - 1 - kk:CONV_K - kk, :] * xp[pl.ds(HALO + r0 - kk, CHUNK), :]
    return pre


def _conv_fwd(proj3, conv_w, conv_b):
    b, s, _ = proj3.shape
    nc = s // CHUNK

    def kern(x_ref, w_ref, b_ref, o_ref, xp):
        xp[0:HALO, :] = jnp.zeros((HALO, CONV_COLS), F32)
        xp[HALO:, :] = x_ref[0]
        for ci in range(nc):
            pre = _conv_pre(xp, w_ref, b_ref, ci * CHUNK)
            o_ref[0, ci * CHUNK:(ci + 1) * CHUNK, :] = pre * _sigmoid(pre)

    return pl.pallas_call(
        kern, name="conv_fwd",
        out_shape=jax.ShapeDtypeStruct((b, s, CONV_DIM), F32),
        grid=(CONV_DIM // CONV_COLS, b),
        in_specs=[pl.BlockSpec((1, s, CONV_COLS), lambda j, bi: (bi, 0, XBC0 // CONV_COLS + j)),
                  pl.BlockSpec((CONV_K, CONV_COLS), lambda j, bi: (0, j)),
                  pl.BlockSpec((1, CONV_COLS), lambda j, bi: (0, j))],
        out_specs=pl.BlockSpec((1, s, CONV_COLS), lambda j, bi: (bi, 0, j)),
        scratch_shapes=[pltpu.VMEM((s + HALO, CONV_COLS), F32)],
        compiler_params=_cparams("parallel", "parallel"),
    )(proj3, conv_w, conv_b)


def _conv_bwd(dact, proj3, conv_w, conv_b, col0, name, dproj3):
    b, s, width = dact.shape
    nc = s // CHUNK
    j0 = col0 // CONV_COLS

    def kern(da_ref, x_ref, w_ref, b_ref, _, dx_ref, dw_ref, db_ref, xp, dp):
        @pl.when(pl.program_id(1) == 0)
        def _():
            dw_ref[...] = jnp.zeros_like(dw_ref)
            db_ref[...] = jnp.zeros_like(db_ref)

        xp[0:HALO, :] = jnp.zeros((HALO, CONV_COLS), F32)
        xp[HALO:, :] = x_ref[0]
        dp[s:, :] = jnp.zeros((HALO, CONV_COLS), F32)
        for ci in range(nc):
            r0 = ci * CHUNK
            pre = _conv_pre(xp, w_ref, b_ref, r0)
            sg = _sigmoid(pre)
            dpre = da_ref[0, r0:r0 + CHUNK, :] * (sg * (1.0 + pre * (1.0 - sg)))
            dp[r0:r0 + CHUNK, :] = dpre
            db_ref[...] += jnp.sum(dpre, axis=0, keepdims=True)
            for kk in range(CONV_K):
                tap = CONV_K - 1 - kk
                dw_ref[tap:tap + 1, :] += jnp.sum(dpre * xp[pl.ds(HALO + r0 - kk, CHUNK), :], axis=0, keepdims=True)
        for ci in range(nc):
            r0 = ci * CHUNK
            dx = w_ref[CONV_K - 1:CONV_K, :] * dp[pl.ds(r0, CHUNK), :]
            for kk in range(1, CONV_K):
                dx = dx + w_ref[CONV_K - 1 - kk:CONV_K - kk, :] * dp[pl.ds(r0 + kk, CHUNK), :]
            dx_ref[0, r0:r0 + CHUNK, :] = dx.astype(BF16)

    return pl.pallas_call(
        kern, name=name,
        out_shape=(jax.ShapeDtypeStruct(dproj3.shape, dproj3.dtype), jax.ShapeDtypeStruct((CONV_K, width), F32),
                   jax.ShapeDtypeStruct((1, width), F32)),
        grid=(width // CONV_COLS, b),
        in_specs=[pl.BlockSpec((1, s, CONV_COLS), lambda j, bi: (bi, 0, j)),
                  pl.BlockSpec((1, s, CONV_COLS), lambda j, bi: (bi, 0, XBC0 // CONV_COLS + j0 + j)),
                  pl.BlockSpec((CONV_K, CONV_COLS), lambda j, bi: (0, j0 + j)),
                  pl.BlockSpec((1, CONV_COLS), lambda j, bi: (0, j0 + j)), ANY],
        out_specs=(pl.BlockSpec((1, s, CONV_COLS), lambda j, bi: (bi, 0, XBC0 // CONV_COLS + j0 + j)),
                   pl.BlockSpec((CONV_K, CONV_COLS), lambda j, bi: (0, j)),
                   pl.BlockSpec((1, CONV_COLS), lambda j, bi: (0, j))),
        input_output_aliases={4: 0},
        scratch_shapes=[pltpu.VMEM((s + HALO, CONV_COLS), F32)] * 2,
        compiler_params=_cparams("parallel", "arbitrary"),
    )(dact, proj3, conv_w, conv_b, dproj3)


def _sel_dot(v, sel, left=False):
    hi = v.astype(BF16)
    rest = v - hi.astype(F32)
    mid = rest.astype(BF16)
    lo = (rest - mid.astype(F32)).astype(BF16)
    if left:
        return _dot(sel, hi, NN) + _dot(sel, mid, NN) + _dot(sel, lo, NN)
    return _dot(hi, sel, NN) + _dot(mid, sel, NN) + _dot(lo, sel, NN)


def _ssd_common(dtr_ref, dtb_ref, alog_ref):
    lane = lax.broadcasted_iota(jnp.int32, (CHUNK, LANES), 1)
    row = lax.broadcasted_iota(jnp.int32, (CHUNK, LANES), 0)
    head_lane = lane < HEADS_PER_GROUP
    pre = dtr_ref[0, 0] + dtb_ref[0]
    dt = jnp.where(head_lane, jnp.maximum(pre, 0.0) + jnp.log(1.0 + jnp.exp(-jnp.abs(pre))), 0.0)
    a = jnp.where(head_lane[0:1], -jnp.exp(alog_ref[0]), 0.0)
    tril = (row >= lane).astype(BF16)
    acs = _sel_dot(dt * a, tril, left=True)
    acs_t = acs.T
    er = lax.broadcasted_iota(jnp.int32, (LANES, GROUP_WIDTH), 0)
    ec = lax.broadcasted_iota(jnp.int32, (LANES, GROUP_WIDTH), 1)
    expand = ((ec // HEAD_DIM) == er).astype(BF16)
    tr = lax.broadcasted_iota(jnp.int32, (GROUP_WIDTH, LANES), 0)
    tc = lax.broadcasted_iota(jnp.int32, (GROUP_WIDTH, LANES), 1)
    reduce = ((tr // HEAD_DIM) == tc).astype(BF16)
    dt_x = _sel_dot(dt, expand)
    acs_x = _sel_dot(acs, expand)
    end_x = acs_x[CHUNK - 1:CHUNK, :]
    causal = row >= lane
    return dict(dt=dt, a=a, pre=pre, head_lane=head_lane, acs=acs, acs_t=acs_t, expand=expand, reduce=reduce,
                dt_x=dt_x, acs_x=acs_x, end_x=end_x, causal=causal, row=row, lane=lane)


def _ssd_decay(cm, h):
    seg = cm["acs"][:, h:h + 1] - cm["acs_t"][h:h + 1, :]
    return jnp.where(cm["causal"], jnp.exp(jnp.minimum(seg, 0.0)), 0.0)


def _ssd_fwd(xact, proj3, dtr_g, dtb_g, alog_g, dskip_x, snw):
    b, s, _ = xact.shape
    nc = s // CHUNK
    g4 = SSD_GROUPS

    def kern(xs_ref, bm_ref, cm_ref, zs_ref, dtr_ref, dtb_ref, alog_ref, dsk_ref, snw_ref,
             y_ref, yn_ref, hst_ref, h_sc):
        @pl.when(pl.program_id(2) == 0)
        def _():
            h_sc[...] = jnp.zeros_like(h_sc)

        cm = _ssd_common(dtr_ref, dtb_ref, alog_ref)
        x = xs_ref[0]
        bmb = bm_ref[0].astype(BF16)
        cmb = cm_ref[0].astype(BF16)
        h_in = h_sc[...]
        hst_ref[0, 0, 0] = h_in
        xdt = x * cm["dt_x"]
        xdtb = xdt.astype(BF16)
        cb = _dot(cmb, bmb, NT)
        y_off = _dot(cmb, h_in.astype(BF16), NN) * jnp.exp(cm["acs_x"])
        for h in range(HEADS_PER_GROUP):
            lanes = slice(h * HEAD_DIM, (h + 1) * HEAD_DIM)
            m = (cb * _ssd_decay(cm, h)).astype(BF16)
            y_ref[0, :, lanes] = _dot(m, xdtb[:, lanes], NN)
        y = y_ref[0] + y_off + x * dsk_ref[...]
        y_ref[0] = y
        w = (xdt * jnp.exp(cm["end_x"] - cm["acs_x"])).astype(BF16)
        h_sc[...] = h_in * jnp.exp(cm["end_x"]) + _dot(bmb, w, TN)
        zs = zs_ref[0]
        y2 = y * (zs * _sigmoid(zs))
        yn_ref[0] = (y2 * lax.rsqrt(jnp.mean(y2 * y2, axis=-1, keepdims=True) + EPS) * snw_ref[...]).astype(BF16)

    gw = GROUP_WIDTH
    small = pl.BlockSpec((1, 1, LANES), lambda gi, bi, ci: (gi, 0, 0))
    xblk = pl.BlockSpec((1, CHUNK, gw), lambda gi, bi, ci: (bi, ci, gi))
    return pl.pallas_call(
        kern, name="ssd_fwd",
        out_shape=(jax.ShapeDtypeStruct((b, s, SSD_WIDTH), F32), jax.ShapeDtypeStruct((b, s, SSD_WIDTH), BF16),
                   jax.ShapeDtypeStruct((b, nc, g4, SSD_STATE, gw), F32)),
        grid=(g4, b, nc),
        in_specs=[xblk,
                  pl.BlockSpec((1, CHUNK, LANES), lambda gi, bi, ci: (bi, ci, SSD_WIDTH // LANES + gi)),
                  pl.BlockSpec((1, CHUNK, LANES), lambda gi, bi, ci: (bi, ci, SSD_WIDTH // LANES + g4 + gi)),
                  pl.BlockSpec((1, CHUNK, gw), lambda gi, bi, ci: (bi, ci, ZS0 // gw + gi)),
                  pl.BlockSpec((1, 1, CHUNK, LANES), lambda gi, bi, ci: (bi, gi, ci, 0)),
                  small, small,
                  pl.BlockSpec((1, gw), lambda gi, bi, ci: (0, gi)),
                  pl.BlockSpec((1, gw), lambda gi, bi, ci: (0, gi))],
        out_specs=(xblk, xblk, pl.BlockSpec((1, 1, 1, SSD_STATE, gw), lambda gi, bi, ci: (bi, ci, gi, 0, 0))),
        scratch_shapes=[pltpu.VMEM((SSD_STATE, gw), F32)],
        compiler_params=_cparams("parallel", "parallel", "arbitrary"),
    )(xact, xact, xact, proj3, dtr_g, dtb_g, alog_g, dskip_x, snw)


def _ssd_bwd(dyn3, y3, xact, proj3, hst, dtr_g, dtb_g, alog_g, dskip_x, snw, dproj3):
    b, s, _ = xact.shape
    nc = s // CHUNK
    g4 = SSD_GROUPS
    gw = GROUP_WIDTH

    def kern(dyn_ref, y_ref, xs_ref, bm_ref, cm_ref, zs_ref, hst_ref, dtr_ref, dtb_ref, alog_ref, dsk_ref, snw_ref, _,
             dxs_ref, dbm_ref, dcm_ref, dzs_ref, ddtr_ref, dsnw_ref, dalog_ref, ddtb_ref, ddsk_ref, dh_sc):
        first = jnp.logical_and(pl.program_id(1) == 0, pl.program_id(2) == 0)

        @pl.when(first)
        def _():
            dsnw_ref[...] = jnp.zeros_like(dsnw_ref)
            dalog_ref[...] = jnp.zeros_like(dalog_ref)
            ddtb_ref[...] = jnp.zeros_like(ddtb_ref)
            ddsk_ref[...] = jnp.zeros_like(ddsk_ref)

        @pl.when(pl.program_id(2) == 0)
        def _():
            dh_sc[...] = jnp.zeros_like(dh_sc)

        cm = _ssd_common(dtr_ref, dtb_ref, alog_ref)
        row, lane = cm["row"], cm["lane"]
        y = y_ref[0]
        zs = zs_ref[0]
        sg = _sigmoid(zs)
        silu = zs * sg
        y2 = y * silu
        rstd = lax.rsqrt(jnp.mean(y2 * y2, axis=-1, keepdims=True) + EPS)
        y2h = y2 * rstd
        dyn = dyn_ref[0]
        dsnw_ref[0] += jnp.sum(dyn * y2h, axis=0, keepdims=True)
        gwv = dyn * snw_ref[...]
        dy2 = rstd * (gwv - y2h * jnp.mean(gwv * y2h, axis=-1, keepdims=True))
        dzs_ref[0] = (dy2 * y * (sg * (1.0 + zs * (1.0 - sg)))).astype(BF16)
        dy = dy2 * silu
        dyb = dy.astype(BF16)

        x = xs_ref[0]
        bmb = bm_ref[0].astype(BF16)
        cmb = cm_ref[0].astype(BF16)
        h_in = hst_ref[0, 0, 0]
        h_inb = h_in.astype(BF16)
        d_hn = dh_sc[...]
        d_hnb = d_hn.astype(BF16)
        xdt = x * cm["dt_x"]
        xdtb = xdt.astype(BF16)
        eacs = jnp.exp(cm["acs_x"])
        dte = jnp.exp(cm["end_x"] - cm["acs_x"])
        wb = (xdt * dte).astype(BF16)

        dsk_lanes = jnp.broadcast_to(jnp.sum(dy * x, axis=0, keepdims=True), (8, gw))
        ddsk_ref[0] += _sel_dot(dsk_lanes, cm["reduce"])[0:1, :]
        dyo = dy * eacs
        dyob = dyo.astype(BF16)
        dacs_x = dyo * _dot(cmb, h_inb, NN)
        dcm = _dot(dyob, h_inb, NT)
        dh_in = _dot(cmb, dyob, TN)
        dw = _dot(bmb, d_hnb, NN)
        dbm = _dot(wb, d_hnb, NT)
        dxdt = dw * dte
        e_l = dw * xdt * dte
        dacs_x = dacs_x - e_l
        dend_x = jnp.sum(e_l, axis=0, keepdims=True)
        chunk_decay = jnp.exp(cm["end_x"])
        dh_sc[...] = d_hn * chunk_decay + dh_in
        dend_x = dend_x + jnp.sum(d_hn * h_in, axis=0, keepdims=True) * chunk_decay
        last_row = lax.broadcasted_iota(jnp.int32, (CHUNK, gw), 0) == CHUNK - 1
        dacs_x = dacs_x + jnp.where(last_row, dend_x, 0.0)

        cb = _dot(cmb, bmb, NT)
        dcb = jnp.zeros((CHUNK, CHUNK), F32)
        dacs = jnp.zeros((CHUNK, LANES), F32)
        dacs_t = jnp.zeros((LANES, CHUNK), F32)
        for h in range(HEADS_PER_GROUP):
            lanes = slice(h * HEAD_DIM, (h + 1) * HEAD_DIM)
            decay = _ssd_decay(cm, h)
            m = cb * decay
            dm = _dot(dyb[:, lanes], xdtb[:, lanes], NT)
            dxs_ref[0, :, lanes] = _dot(m.astype(BF16), dyb[:, lanes], TN)
            dcb_h = dm * decay
            dcb = dcb + dcb_h
            n = dcb_h * cb
            dacs = dacs + jnp.where(lane == h, jnp.sum(n, axis=1, keepdims=True), 0.0)
            dacs_t = dacs_t + jnp.where(row == h, jnp.sum(n, axis=0, keepdims=True), 0.0)
        dcbb = dcb.astype(BF16)
        dcm_ref[0] = dcm + _dot(dcbb, bmb, NN)
        dbm_ref[0] = dbm + _dot(dcbb, cmb, TN)
        dxdt = dxdt + dxs_ref[0]
        dxs_ref[0] = dy * dsk_ref[...] + dxdt * cm["dt_x"]

        dacs = dacs - dacs_t.T + _sel_dot(dacs_x, cm["reduce"])
        ddt = _sel_dot(dxdt * x, cm["reduce"])
        triu = (row <= lane).astype(BF16)
        rc = _sel_dot(dacs, triu, left=True)
        ddt = ddt + cm["a"] * rc
        dalog_ref[0] += jnp.sum(cm["dt"] * rc, axis=0, keepdims=True) * cm["a"]
        ddtr = jnp.where(cm["head_lane"], ddt * _sigmoid(cm["pre"]), 0.0)
        ddtr_ref[0, 0] = ddtr
        ddtb_ref[0] += jnp.sum(ddtr, axis=0, keepdims=True)

    def rev(ci):
        return nc - 1 - ci

    small = pl.BlockSpec((1, 1, LANES), lambda gi, bi, ci: (gi, 0, 0))
    xblk = pl.BlockSpec((1, CHUNK, gw), lambda gi, bi, ci: (bi, rev(ci), gi))
    nblk = pl.BlockSpec((1, CHUNK, LANES), lambda gi, bi, ci: (bi, rev(ci), gi))
    gvec = pl.BlockSpec((1, gw), lambda gi, bi, ci: (0, gi))
    gacc = pl.BlockSpec((1, 1, gw), lambda gi, bi, ci: (gi, 0, 0))
    return pl.pallas_call(
        kern, name="ssd_bwd",
        out_shape=(jax.ShapeDtypeStruct((b, s, SSD_WIDTH), F32),
                   jax.ShapeDtypeStruct((b, s, g4 * SSD_STATE), F32),
                   jax.ShapeDtypeStruct((b, s, g4 * SSD_STATE), F32),
                   jax.ShapeDtypeStruct(dproj3.shape, dproj3.dtype),
                   jax.ShapeDtypeStruct((b, g4, s, LANES), F32),
                   jax.ShapeDtypeStruct((g4, 1, gw), F32),
                   jax.ShapeDtypeStruct((g4, 1, LANES), F32),
                   jax.ShapeDtypeStruct((g4, 1, LANES), F32),
                   jax.ShapeDtypeStruct((g4, 1, LANES), F32)),
        grid=(g4, b, nc),
        in_specs=[xblk, xblk, xblk,
                  pl.BlockSpec((1, CHUNK, LANES), lambda gi, bi, ci: (bi, rev(ci), SSD_WIDTH // LANES + gi)),
                  pl.BlockSpec((1, CHUNK, LANES), lambda gi, bi, ci: (bi, rev(ci), SSD_WIDTH // LANES + g4 + gi)),
                  pl.BlockSpec((1, CHUNK, gw), lambda gi, bi, ci: (bi, rev(ci), ZS0 // gw + gi)),
                  pl.BlockSpec((1, 1, 1, SSD_STATE, gw), lambda gi, bi, ci: (bi, rev(ci), gi, 0, 0)),
                  pl.BlockSpec((1, 1, CHUNK, LANES), lambda gi, bi, ci: (bi, gi, rev(ci), 0)),
                  small, small, gvec, gvec, ANY],
        out_specs=(xblk, nblk, nblk,
                   pl.BlockSpec((1, CHUNK, gw), lambda gi, bi, ci: (bi, rev(ci), ZS0 // gw + gi)),
                   pl.BlockSpec((1, 1, CHUNK, LANES), lambda gi, bi, ci: (bi, gi, rev(ci), 0)),
                   gacc, small, small, small),
        input_output_aliases={12: 3},
        scratch_shapes=[pltpu.VMEM((SSD_STATE, gw), F32)],
        compiler_params=_cparams("parallel", "arbitrary", "arbitrary"),
    )(dyn3, y3, xact, xact, xact, proj3, hst, dtr_g, dtb_g, alog_g, dskip_x, snw, dproj3)


def _adamw(w, g, m, v, name):
    r, c = w.shape
    tr = 128 if r % 128 == 0 else r
    tc = LANES if (tr == r and r > 128 and c % LANES == 0) else c

    def kern(w_ref, g_ref, m_ref, v_ref, d_ref, nm_ref, nv_ref):
        gv = g_ref[...]
        nm = ADAM_B1 * m_ref[...] + (1.0 - ADAM_B1) * gv
        nv = ADAM_B2 * v_ref[...] + (1.0 - ADAM_B2) * (gv * gv)
        m_hat = nm / (1.0 - ADAM_B1 ** ADAM_STEP)
        v_hat = nv / (1.0 - ADAM_B2 ** ADAM_STEP)
        d_ref[...] = -ADAM_LR * (m_hat / (jnp.sqrt(v_hat) + ADAM_EPS) + ADAM_WD * w_ref[...])
        nm_ref[...] = nm
        nv_ref[...] = nv

    blk = pl.BlockSpec((tr, tc), lambda i, j: (i, j))
    out = jax.ShapeDtypeStruct((r, c), F32)
    return pl.pallas_call(
        kern, name=name, out_shape=(out, out, out), grid=(r // tr, c // tc),
        in_specs=[blk] * 4, out_specs=(blk, blk, blk),
        compiler_params=_cparams("parallel", "parallel"),
    )(w, g, m, v)


ANY = pl.BlockSpec(memory_space=pl.ANY)


def _position():
    return lax.axis_index("x"), lax.axis_index("y"), lax.axis_index("c")


def _other_chips(x, y):
    return [(1 - x, y), (x, 1 - y), (1 - x, 1 - y)]


def _dma_sems(n):
    return [pltpu.SemaphoreType.DMA((n,)), pltpu.SemaphoreType.DMA((n,))]


class _Exchange:
    def __init__(self, inputs, out_shapes, sems, start, finish):
        self.inputs, self.out_shapes, self.sems, self.start, self.finish = inputs, out_shapes, sems, start, finish


def _run_exchange(ex, name):
    n_in, n_out = len(ex.inputs), len(ex.out_shapes)

    def body(*refs):
        x_in, x_out, sems = refs[:n_in], refs[n_in:n_in + n_out], refs[n_in + n_out:]
        ex.start(x_in, x_out, sems)
        ex.finish(x_in, x_out, sems)

    return pl.pallas_call(
        body, name=name, out_shape=list(ex.out_shapes),
        in_specs=[ANY] * n_in, out_specs=[ANY] * n_out, scratch_shapes=list(ex.sems),
    )(*ex.inputs)


def _gather_exchange(shards):
    n = len(shards)

    def copies(p_refs, out_refs, sems):
        send_sems, recv_sems = sems
        x, y, c = _position()
        me = 2 * x + y
        chips = _other_chips(x, y)

        def slab(a, chip, hf):
            half = shards[a].shape[1] // 2
            return out_refs[a].at[chip, :, pl.ds(hf * half, half)]

        def my_half(a):
            half = shards[a].shape[1] // 2
            return p_refs[a].at[:, pl.ds(c * half, half)]

        def over_ici(a, j, chip_from):
            px, py = chips[j]
            return pltpu.make_async_remote_copy(
                src_ref=my_half(a), dst_ref=slab(a, chip_from, c),
                send_sem=send_sems.at[3 * a + j], recv_sem=recv_sems.at[3 * a + j],
                device_id=(px, py, c), device_id_type=MESH)

        def to_sibling(a, j, hf):
            px, py = chips[j]
            return pltpu.make_async_remote_copy(
                src_ref=slab(a, 2 * px + py, hf), dst_ref=slab(a, 2 * px + py, hf),
                send_sem=send_sems.at[3 * (n + a) + j], recv_sem=recv_sems.at[3 * (n + a) + j],
                device_id=(x, y, 1 - c), device_id_type=MESH)

        own = [pltpu.make_async_remote_copy(
            src_ref=p_refs[a], dst_ref=out_refs[a].at[me], send_sem=send_sems.at[6 * n + a], recv_sem=recv_sems.at[6 * n + a],
            device_id=(x, y, 1 - c), device_id_type=MESH) for a in range(n)]
        first = [over_ici(a, j, me) for a in range(n) for j in range(3)]
        return chips, c, over_ici, to_sibling, first, own

    def start(p_refs, out_refs, sems):
        _, _, _, _, first, own = copies(p_refs, out_refs, sems)
        for cp in first + own:
            cp.start()

    def finish(p_refs, out_refs, sems):
        chips, c, over_ici, to_sibling, first, own = copies(p_refs, out_refs, sems)
        passed = []
        for a in range(n):
            for j, (px, py) in enumerate(chips):
                over_ici(a, j, 2 * px + py).wait_recv()
                passed.append(to_sibling(a, j, c))
                passed[-1].start()
        for a in range(n):
            for j in range(3):
                to_sibling(a, j, 1 - c).wait_recv()
        for cp in first + passed:
            cp.wait_send()
        for cp in own:
            cp.wait()

    return _Exchange(list(shards), [jax.ShapeDtypeStruct((N_CHIPS, *v.shape), v.dtype) for v in shards],
                     _dma_sems(7 * n), start, finish)


def _swap_halves(parts, name):
    n = len(parts)

    def body(*refs):
        v_refs, out_refs = refs[:n], refs[n:2 * n]
        send_sems, recv_sems = refs[2 * n:]
        x, y, c = _position()
        copies = []
        for a in range(n):
            half = parts[a].shape[2] // 2
            copies.append(pltpu.make_async_remote_copy(
                src_ref=v_refs[a].at[:, :, pl.ds((1 - c) * half, half)], dst_ref=out_refs[a],
                send_sem=send_sems.at[a], recv_sem=recv_sems.at[a], device_id=(x, y, 1 - c), device_id_type=MESH))
        for cp in copies:
            cp.start()
        for cp in copies:
            cp.wait()

    return pl.pallas_call(
        body, name=name,
        out_shape=[jax.ShapeDtypeStruct((v.shape[0], v.shape[1], v.shape[2] // 2), v.dtype) for v in parts],
        in_specs=[ANY] * n, out_specs=[ANY] * n,
        scratch_shapes=_dma_sems(n),
    )(*parts)


def _all_to_all_exchange(parts):
    n = len(parts)

    def sends(p_refs, out_refs, sems):
        send_sems, recv_sems = sems
        x, y, c = _position()
        return [pltpu.make_async_remote_copy(
            src_ref=p_refs[a].at[2 * px + py], dst_ref=out_refs[a].at[j],
            send_sem=send_sems.at[3 * a + j], recv_sem=recv_sems.at[3 * a + j],
            device_id=(px, py, c), device_id_type=MESH) for a in range(n) for j, (px, py) in enumerate(_other_chips(x, y))]

    def start(p_refs, out_refs, sems):
        for cp in sends(p_refs, out_refs, sems):
            cp.start()

    def finish(p_refs, out_refs, sems):
        for cp in sends(p_refs, out_refs, sems):
            cp.wait()

    return _Exchange(list(parts), [jax.ShapeDtypeStruct((N_CHIPS - 1, *v.shape[1:]), v.dtype) for v in parts],
                     _dma_sems(3 * n), start, finish)


def _join_halves(wholes):
    n = len(wholes)

    def body(*refs):
        out_refs = refs[n:2 * n]
        send_sems, recv_sems = refs[2 * n:]
        x, y, c = _position()
        copies = []
        for a in range(n):
            half = wholes[a].shape[1] // 2
            mine = out_refs[a].at[:, pl.ds(c * half, half)]
            copies.append(pltpu.make_async_remote_copy(
                src_ref=mine, dst_ref=mine, send_sem=send_sems.at[a], recv_sem=recv_sems.at[a],
                device_id=(x, y, 1 - c), device_id_type=MESH))
        for cp in copies:
            cp.start()
        for cp in copies:
            cp.wait()

    return pl.pallas_call(
        body, name="grad_join_halves",
        out_shape=[jax.ShapeDtypeStruct(v.shape, v.dtype) for v in wholes],
        in_specs=[ANY] * n, out_specs=[ANY] * n,
        input_output_aliases={a: a for a in range(n)},
        scratch_shapes=_dma_sems(n),
    )(*wholes)


STRIP = 256


def _add_halves(g, sw, place, name):
    n, rows, cols = g.shape
    nb = cols // 2 // STRIP

    def kern(p_ref, g_ref, s_ref, o_ref):
        o_ref[...] = (g_ref[...] + s_ref[...]).astype(BF16)

    blk = pl.BlockSpec((1, rows, STRIP), lambda j, i, p_ref: (j, 0, i))
    return pl.pallas_call(
        kern, name=name,
        out_shape=jax.ShapeDtypeStruct((n, rows, cols // 2), BF16),
        grid_spec=pltpu.PrefetchScalarGridSpec(
            num_scalar_prefetch=1, grid=(n, nb),
            in_specs=[pl.BlockSpec((1, rows, STRIP), lambda j, i, p_ref: (j, 0, p_ref[0] * nb + i)), blk],
            out_specs=blk),
        compiler_params=_cparams("parallel", "parallel"),
    )(place, g, sw)


def _sum_chips(own, rx, place, name):
    _, rows, half = rx.shape
    nb = half // STRIP

    def kern(p_ref, own_ref, r_ref, o_ref):
        total = own_ref[0].astype(F32)
        for j in range(N_CHIPS - 1):
            total = total + r_ref[j].astype(F32)
        o_ref[...] = total

    return pl.pallas_call(
        kern, name=name,
        out_shape=jax.ShapeDtypeStruct((rows, 2 * half), F32),
        grid_spec=pltpu.PrefetchScalarGridSpec(
            num_scalar_prefetch=1, grid=(nb,),
            in_specs=[pl.BlockSpec((1, rows, STRIP), lambda i, p_ref: (p_ref[1], 0, i)),
                      pl.BlockSpec((N_CHIPS - 1, rows, STRIP), lambda i, p_ref: (0, 0, i))],
            out_specs=pl.BlockSpec((rows, STRIP), lambda i, p_ref: (0, p_ref[0] * nb + i))),
        compiler_params=_cparams("parallel"),
    )(place, own, rx)


def _gather_small(v, reduce, name):
    rows = v.shape[0]

    def body(v_ref, out_ref, buf, send_sems, recv_sems):
        x, y, c = _position()
        me = 4 * x + 2 * y + c
        buf[me] = v_ref[...]
        peers = [(x ^ (k >> 2), y ^ ((k >> 1) & 1), c ^ (k & 1)) for k in range(1, 8)]
        copies = [pltpu.make_async_remote_copy(
            src_ref=v_ref, dst_ref=buf.at[me],
            send_sem=send_sems.at[k], recv_sem=recv_sems.at[k],
            device_id=peer, device_id_type=MESH) for k, peer in enumerate(peers)]
        for cp in copies:
            cp.start()
        for k, (px, py, pc) in enumerate(peers):
            pltpu.make_async_remote_copy(
                src_ref=v_ref, dst_ref=buf.at[4 * px + 2 * py + pc],
                send_sem=send_sems.at[k], recv_sem=recv_sems.at[k],
                device_id=(px, py, pc), device_id_type=MESH).wait_recv()
        for cp in copies:
            cp.wait_send()
        if reduce:
            total = buf[0]
            for d in range(1, 8):
                total = total + buf[d]
            out_ref[...] = total
        else:
            out_ref[...] = buf[...]

    vm = pl.BlockSpec(memory_space=pltpu.VMEM)
    return pl.pallas_call(
        body, name=name,
        out_shape=jax.ShapeDtypeStruct((rows, LANES) if reduce else (8, rows, LANES), F32),
        in_specs=[vm], out_specs=vm,
        scratch_shapes=[pltpu.VMEM((8, rows, LANES), F32), pltpu.SemaphoreType.DMA((7,)), pltpu.SemaphoreType.DMA((7,))],
    )(v)


def _pad_rows(a, rows):
    return jnp.pad(a, ((0, rows - a.shape[0]), (0, 0)))


def _lane_pad(v):
    n = v.shape[1]
    return jnp.pad(v, ((0, 0), (0, -n % LANES)))


def _gather_all(w_in, w_attn_out, w_ssm_out, w_o, conv_w):
    d = D_MODEL
    w_in_t, = _run_exchange(_gather_exchange([w_in[0].T.astype(BF16)]), "gather_w_in")
    w_proj_t = _to_proj_layout(w_in_t.reshape(D_PROJ, d))
    out_w = _gather_exchange([a[0].astype(BF16) for a in (w_attn_out, w_ssm_out, w_o)])
    conv_rows = conv_w[0].size // LANES
    conv_all = _gather_small(conv_w[0].reshape(conv_rows, LANES), False, "gather_conv_w")
    conv_w_all = conv_all[0::2].reshape(N_CHIPS, CONV_K, CONV_DIM // N_CHIPS).transpose(1, 0, 2).reshape(CONV_K, CONV_DIM)

    return w_proj_t, out_w, conv_w_all


def _local_step(x, loss_target, norm_w, w_proj_t, conv_w_all, conv_b, dt_bias, a_log, d_skip, ssm_norm_w,
                out_w, final_norm_w, grad_exchange=None):
    b, s, d = x.shape
    t = b * s
    g4, hg = SSD_GROUPS, HEADS_PER_GROUP
    dtb_g = _lane_pad(dt_bias.reshape(g4, hg)).reshape(g4, 1, LANES)
    alog_g = _lane_pad(a_log.reshape(g4, hg)).reshape(g4, 1, LANES)
    dskip_x = jnp.repeat(d_skip, HEAD_DIM, axis=1)
    fnw = final_norm_w.reshape(1, d)

    x2 = x.reshape(t, d)
    h = _rms_fwd(x2, norm_w)
    big_tm = min(t, 2048)
    if isinstance(out_w, _Exchange):
        proj, *out_w = _matmul(h, w_proj_t, tb=True, tm=big_tm, tn=1280, tk=1024, name="proj", exchange=out_w)
    else:
        proj = _matmul(h, w_proj_t, tb=True, tm=big_tm, tn=1280, tk=1024, name="proj")
    w_ao, w_so, w_oo = (w.reshape(-1, d) for w in out_w)
    proj3 = proj.reshape(b, s, NP)
    o3, yp3 = _attn_fwd(proj3)
    xact = _conv_fwd(proj3, conv_w_all, conv_b)
    dtr = proj3[:, :, DT0:DT0 + g4 * hg].reshape(b, s, g4, hg).transpose(0, 2, 1, 3)
    dtr_g = jnp.pad(dtr, ((0, 0), (0, 0), (0, 0), (0, LANES - hg)))
    y3, yn3, hst = _ssd_fwd(xact, proj3, dtr_g, dtb_g, alog_g, dskip_x, ssm_norm_w)
    yp = yp3.reshape(t, D_MODEL)
    yn = yn3.reshape(t, SSD_WIDTH)
    ya = _matmul(yp, w_ao, tm=512, tn=1024, tk=1024, name="attn_out")
    ys = _matmul(yn, w_so, tm=512, tn=1024, tk=2048, name="ssm_out")
    merged = _merge_fwd(proj, ya, ys)
    mo = _matmul(merged, w_oo, tm=512, tn=1024, tk=1024, name="out_proj")
    dout, doutb, loss_part, d_fnw = _final_fwd_bwd(x2, mo, loss_target.reshape(t, d), fnw)

    dmerged = _matmul(doutb, w_oo, tb=True, tm=512, tn=1024, tk=1024, name="d_merged")
    g_wo = _matmul(merged, doutb, ta=True, tm=512, tn=1024, tk=1024, name="g_w_o")
    dya, dys, dproj = _merge_bwd(dmerged, proj, ya, ys)
    dyp = _matmul(dya, w_ao, tb=True, tm=512, tn=1024, tk=1024, name="d_attn_pre")
    g_wao = _matmul(yp, dya, ta=True, tm=512, tn=1024, tk=1024, name="g_w_attn_out")
    dyn = _matmul(dys, w_so, tb=True, tm=1024, tn=2048, tk=1024, name="d_ssm_norm")
    g_wso = _matmul(yn, dys, ta=True, tm=1024, tn=1024, tk=1024, name="g_w_ssm_out")
    dproj3 = _attn_bwd(proj3, dyp.reshape(b, s, D_MODEL), o3, dproj.reshape(b, s, NP))
    (dxs, dbm, dcm, dproj3, ddtr_g, d_snw_g, d_alog_g, d_dtb_g, d_dsk_g) = _ssd_bwd(
        dyn.reshape(b, s, SSD_WIDTH), y3, xact, proj3, hst, dtr_g, dtb_g, alog_g, dskip_x, ssm_norm_w, dproj3)
    dproj3, g_cw_xs, g_cb_xs = _conv_bwd(dxs, proj3, conv_w_all, conv_b, 0, "conv_bwd_x", dproj3)
    dproj3, g_cw_bm, g_cb_bm = _conv_bwd(dbm, proj3, conv_w_all, conv_b, SSD_WIDTH, "conv_bwd_b", dproj3)
    dproj3, g_cw_cm, g_cb_cm = _conv_bwd(dcm, proj3, conv_w_all, conv_b, SSD_WIDTH + g4 * SSD_STATE, "conv_bwd_c", dproj3)
    ddt = ddtr_g[:, :, :, :hg].transpose(0, 2, 1, 3).reshape(b, s, g4 * hg).astype(BF16)
    ddt = jnp.pad(ddt, ((0, 0), (0, 0), (0, DT_PAD - g4 * hg)))
    dproj = lax.dynamic_update_slice(dproj3, ddt, (0, 0, DT0)).reshape(t, NP)
    exchanged = []
    if grad_exchange:
        g_wproj, *got = _matmul(dproj, h, ta=True, tm=1280, tn=1024, tk=1024, name="g_w_in",
                                exchange=grad_exchange([g_wao, g_wso, g_wo], "out"))
        exchanged += got
        dh, *got = _matmul(dproj, w_proj_t, tm=big_tm, tn=1024, tk=1280, name="d_h", exchange=grad_exchange([g_wproj], "in"))
        exchanged += got
    else:
        g_wproj = _matmul(dproj, h, ta=True, tm=1280, tn=1024, tk=1024, name="g_w_in")
        dh = _matmul(dproj, w_proj_t, tm=big_tm, tn=1024, tk=1280, name="d_h")
    grad_x, d_nw = _rms_bwd(dh, x2, norm_w, dout)
    g_cw = jnp.concatenate([g_cw_xs, g_cw_bm, g_cw_cm], axis=1)
    g_cb = jnp.concatenate([g_cb_xs, g_cb_bm, g_cb_cm], axis=1)
    return (loss_part, grad_x, d_nw, g_wproj, g_cw, g_cb, d_dtb_g, d_alog_g, d_dsk_g, d_snw_g, g_wao, g_wso, g_wo, d_fnw,
            exchanged)


def kernel(x, norm_w, w_in, conv_w, conv_b, dt_bias, a_log, d_skip, ssm_norm_w, w_attn_out, w_ssm_out, w_o, final_norm_w, loss_target, m_norm_w, m_w_in, m_conv_w, m_conv_b, m_dt_bias, m_a_log, m_d_skip, m_ssm_norm_w, m_w_attn_out, m_w_ssm_out, m_w_o, m_final_norm_w, v_norm_w, v_w_in, v_conv_w, v_conv_b, v_dt_bias, v_a_log, v_d_skip, v_ssm_norm_w, v_w_attn_out, v_w_ssm_out, v_w_o, v_final_norm_w):
    b, s, d = x.shape
    core = lax.axis_index("c")
    g4, hg = SSD_GROUPS, HEADS_PER_GROUP
    shard_cols = w_in.shape[2]
    w_proj_t, out_w, conv_w_all = _gather_all(w_in, w_attn_out, w_ssm_out, w_o, conv_w)
    chip = 2 * lax.axis_index("x") + lax.axis_index("y")
    place = jnp.stack([core, chip]).astype(jnp.int32)
    chip_sums = []

    def grad_exchange(grads, which):
        if which == "in":
            slabs = _from_proj_layout(grads[0]).reshape(N_CHIPS, shard_cols, d)
        else:
            slabs = jnp.concatenate([g.reshape(N_CHIPS, -1, d) for g in grads], axis=1)
        from_sibling, = _swap_halves([slabs], "grad_swap_halves_" + which)
        chip_sums.append(_add_halves(slabs, from_sibling, place, "grad_add_halves_" + which))
        return _all_to_all_exchange(chip_sums[-1:])

    (loss_part, grad_x, d_nw, _, g_cw, g_cb, d_dtb_g, d_alog_g, d_dsk_g, d_snw_g, _, _, _, d_fnw, from_chips) = _local_step(
        x, loss_target, norm_w, w_proj_t, conv_w_all, conv_b, dt_bias, a_log, d_skip, ssm_norm_w, out_w, final_norm_w,
        grad_exchange)
    wholes = [_sum_chips(o, r, place, "grad_sum_chips_%d" % i) for i, (o, r) in enumerate(zip(chip_sums, from_chips))]
    g_out, g_w_in = _join_halves(wholes)

    small = jnp.concatenate([
        loss_part, d_nw, g_cb, _lane_pad(d_dtb_g[:, 0, :hg].reshape(1, -1)), _lane_pad(d_alog_g[:, 0, :hg].reshape(1, -1)),
        _lane_pad(d_dsk_g[:, 0, :hg].reshape(1, -1)),
        d_snw_g.reshape(1, -1), d_fnw, g_cw.reshape(1, -1)], axis=1)
    small_rows = small.shape[1] // LANES
    reduced = _gather_small(_pad_rows(small.reshape(small_rows, LANES), -(-small_rows // 8) * 8), True, "reduce_small")
    flat = reduced.reshape(-1)

    def take(start, n):
        return flat[start:start + n].reshape(1, n)

    loss = flat[0]
    pos = LANES
    g_norm_w = take(pos, d); pos += d
    g_conv_b = take(pos, CONV_DIM); pos += CONV_DIM
    g_dt_bias = take(pos, g4 * hg); pos += LANES
    g_a_log = take(pos, g4 * hg); pos += LANES
    g_d_skip = take(pos, g4 * hg); pos += LANES
    g_ssm_norm_w = take(pos, SSD_WIDTH); pos += SSD_WIDTH
    g_final_norm_w = take(pos, d); pos += d
    conv_cols = CONV_DIM // N_CHIPS
    g_conv_w = lax.dynamic_slice_in_dim(flat[pos:pos + CONV_K * CONV_DIM].reshape(CONV_K, CONV_DIM), chip * conv_cols, conv_cols, axis=1)

    rows_ao, rows_so = D_MODEL // N_CHIPS, SSD_WIDTH // N_CHIPS
    g_w_attn_out = g_out[:rows_ao]
    g_w_ssm_out = g_out[rows_ao:rows_ao + rows_so]
    g_w_o = g_out[rows_ao + rows_so:]

    names = ["norm_w", "w_in", "conv_w", "conv_b", "dt_bias", "a_log", "d_skip", "ssm_norm_w",
             "w_attn_out", "w_ssm_out", "w_o", "final_norm_w"]
    weights = [norm_w, w_in, conv_w, conv_b, dt_bias, a_log, d_skip, ssm_norm_w, w_attn_out, w_ssm_out, w_o, final_norm_w]
    grads = [g_norm_w, g_w_in, g_conv_w, g_conv_b, g_dt_bias, g_a_log, g_d_skip, g_ssm_norm_w,
             g_w_attn_out, g_w_ssm_out, g_w_o, g_final_norm_w]
    ms = [m_norm_w, m_w_in, m_conv_w, m_conv_b, m_dt_bias, m_a_log, m_d_skip, m_ssm_norm_w,
          m_w_attn_out, m_w_ssm_out, m_w_o, m_final_norm_w]
    vs = [v_norm_w, v_w_in, v_conv_w, v_conv_b, v_dt_bias, v_a_log, v_d_skip, v_ssm_norm_w,
          v_w_attn_out, v_w_ssm_out, v_w_o, v_final_norm_w]
    out_g, out_d, out_m, out_v = [], [], [], []
    for name, w, g, m, v in zip(names, weights, grads, ms, vs):
        if name == "w_in":
            to2, back = (lambda a: a[0].T), (lambda a: a.T.reshape(w.shape))
        else:
            to2, back = (lambda a: a.reshape(g.shape)), (lambda a: a.reshape(w.shape))
        dlt, nm, nv = _adamw(to2(w), g, to2(m), to2(v), "adamw_" + name)
        out_g.append(back(g))
        out_d.append(back(dlt))
        out_m.append(back(nm))
        out_v.append(back(nv))

    return (loss, grad_x.reshape(b, s, d), *out_g, *out_d, *out_m, *out_v)
```

```python
import jax
import jax.numpy as jnp
from jax import lax
from jax.experimental import pallas as pl
from jax.experimental.pallas import tpu as pltpu

F32 = jnp.float32
BF16 = jnp.bfloat16
MESH = pl.DeviceIdType.MESH

D_MODEL = 1024
SB_HEADS = 16
HEAD_DIM = 64
SSD_WIDTH = 2048
SSD_GROUPS = 4
GROUP_WIDTH = SSD_WIDTH // SSD_GROUPS
HEADS_PER_GROUP = 8
SSD_STATE = 128
CHUNK = 128
CONV_K = 4
CONV_DIM = 3072
D_PROJ = 11296
EPS = 1e-6
ADAM_LR, ADAM_B1, ADAM_B2, ADAM_EPS, ADAM_WD, ADAM_STEP = 0.001, 0.9, 0.999, 1e-08, 0.01, 10

LANES = 128
HP_WIDTH = 4 * LANES
ZS0, GATE0, XBC0, DT0 = 4096, 6144, 8192, 11264
DT_PAD = 256
NP = DT0 + DT_PAD
N_CHIPS = 4
VMEM_LIMIT = 56 * 1024 * 1024


N_HP = SB_HEADS // 2
W_ZS0, W_XBC0, W_DT0, W_GATE0 = 4096, 6144, 9216, 9248


def _to_proj_layout(wt):
    d = wt.shape[1]
    pairs = wt[:W_ZS0].reshape(4, N_HP, LANES, d).transpose(1, 0, 2, 3).reshape(W_ZS0, d)
    return jnp.concatenate([pairs, wt[W_ZS0:W_XBC0], wt[W_GATE0:], wt[W_XBC0:W_DT0], wt[W_DT0:W_GATE0],
                            jnp.zeros((NP - D_PROJ, d), wt.dtype)], axis=0)


def _from_proj_layout(gt):
    d = gt.shape[1]
    qkvz = gt[:ZS0].reshape(N_HP, 4, LANES, d).transpose(1, 0, 2, 3).reshape(ZS0, d)
    return jnp.concatenate([qkvz, gt[ZS0:GATE0], gt[XBC0:DT0], gt[DT0:DT0 + W_GATE0 - W_DT0], gt[GATE0:XBC0]], axis=0)


def _cparams(*sem):
    return pltpu.CompilerParams(dimension_semantics=sem or None, vmem_limit_bytes=VMEM_LIMIT)


def _sigmoid(z):
    return 1.0 / (1.0 + jnp.exp(-z))


def _dot(a, b, dims, precision=None):
    return lax.dot_general(a, b, (dims, ((), ())), preferred_element_type=F32, precision=precision)


NN = ((1,), (0,))
NT = ((1,), (1,))
TN = ((0,), (0,))


def _matmul(a, b, *, ta=False, tb=False, out_dtype=F32, tm, tn, tk, name, exchange=None):
    m, k = (a.shape[1], a.shape[0]) if ta else a.shape
    n = b.shape[0] if tb else b.shape[1]
    assert m % tm == 0 and n % tn == 0 and k % tk == 0, (name, m, n, k)
    grid = (m // tm, n // tn, k // tk)
    nk = grid[2]
    use_scratch = out_dtype != F32
    dims = ((0,) if ta else (1,), (1,) if tb else (0,))
    n_in = len(exchange.inputs) if exchange else 0
    n_out = len(exchange.out_shapes) if exchange else 0

    def kern(a_ref, b_ref, *rest):
        x_in, o_ref, x_out, scratch = rest[:n_in], rest[n_in], rest[n_in + 1:n_in + 1 + n_out], rest[n_in + 1 + n_out:]
        acc = scratch[0] if use_scratch else o_ref
        step = [pl.program_id(ax) for ax in range(3)]
        if exchange:
            sems = scratch[1:] if use_scratch else scratch

            @pl.when(jnp.logical_and(jnp.logical_and(step[0] == 0, step[1] == 0), step[2] == 0))
            def _():
                exchange.start(x_in, x_out, sems)

        @pl.when(step[2] == 0)
        def _():
            acc[...] = jnp.zeros_like(acc)

        acc[...] += _dot(a_ref[...], b_ref[...], dims)
        if use_scratch:
            @pl.when(step[2] == nk - 1)
            def _():
                o_ref[...] = acc[...].astype(out_dtype)
        if exchange:
            @pl.when(jnp.logical_and(jnp.logical_and(step[0] == grid[0] - 1, step[1] == grid[1] - 1), step[2] == nk - 1))
            def _():
                exchange.finish(x_in, x_out, sems)

    a_spec = pl.BlockSpec((tk, tm), lambda i, j, q: (q, i)) if ta else pl.BlockSpec((tm, tk), lambda i, j, q: (i, q))
    b_spec = pl.BlockSpec((tn, tk), lambda i, j, q: (j, q)) if tb else pl.BlockSpec((tk, tn), lambda i, j, q: (q, j))
    out = pl.pallas_call(
        kern, name=name,
        out_shape=[jax.ShapeDtypeStruct((m, n), out_dtype)] + (list(exchange.out_shapes) if exchange else []),
        grid=grid,
        in_specs=[a_spec, b_spec] + [ANY] * n_in,
        out_specs=[pl.BlockSpec((tm, tn), lambda i, j, q: (i, j))] + [ANY] * n_out,
        scratch_shapes=([pltpu.VMEM((tm, tn), F32)] if use_scratch else []) + (list(exchange.sems) if exchange else []),
        compiler_params=_cparams("arbitrary", "arbitrary", "arbitrary") if exchange else _cparams("parallel", "parallel", "arbitrary"),
    )(a, b, *(exchange.inputs if exchange else []))
    return out if exchange else out[0]


ROWS = 256


def _rms_fwd(x2, w):
    t, d = x2.shape

    def kern(x_ref, w_ref, h_ref):
        x = x_ref[...]
        r = lax.rsqrt(jnp.mean(x * x, axis=-1, keepdims=True) + EPS)
        h_ref[...] = (x * r * w_ref[...]).astype(BF16)

    return pl.pallas_call(
        kern, name="rms_fwd",
        out_shape=jax.ShapeDtypeStruct((t, d), BF16),
        grid=(t // ROWS,),
        in_specs=[pl.BlockSpec((ROWS, d), lambda i: (i, 0)), pl.BlockSpec((1, d), lambda i: (0, 0))],
        out_specs=pl.BlockSpec((ROWS, d), lambda i: (i, 0)),
        compiler_params=_cparams("parallel"),
    )(x2, w)


def _rms_bwd(dh, x2, w, dout):
    t, d = x2.shape

    def kern(dh_ref, x_ref, w_ref, dout_ref, gx_ref, dw_ref):
        @pl.when(pl.program_id(0) == 0)
        def _():
            dw_ref[...] = jnp.zeros_like(dw_ref)

        x = x_ref[...]
        r = lax.rsqrt(jnp.mean(x * x, axis=-1, keepdims=True) + EPS)
        xh = x * r
        g = dh_ref[...]
        dw_ref[...] += jnp.sum(g * xh, axis=0, keepdims=True)
        gw = g * w_ref[...]
        gx_ref[...] = dout_ref[...] + r * (gw - xh * jnp.mean(gw * xh, axis=-1, keepdims=True))

    row = pl.BlockSpec((ROWS, d), lambda i: (i, 0))
    vec = pl.BlockSpec((1, d), lambda i: (0, 0))
    return pl.pallas_call(
        kern, name="rms_bwd",
        out_shape=(jax.ShapeDtypeStruct((t, d), F32), jax.ShapeDtypeStruct((1, d), F32)),
        grid=(t // ROWS,),
        in_specs=[row, row, vec, row],
        out_specs=(row, vec),
        compiler_params=_cparams("arbitrary"),
    )(dh, x2, w, dout)


def _final_fwd_bwd(x2, mo, target, w):
    t, d = x2.shape

    def kern(x_ref, mo_ref, t_ref, w_ref, dout_ref, doutb_ref, loss_ref, dw_ref):
        @pl.when(pl.program_id(0) == 0)
        def _():
            loss_ref[...] = jnp.zeros_like(loss_ref)
            dw_ref[...] = jnp.zeros_like(dw_ref)

        u = x_ref[...] + mo_ref[...]
        r = lax.rsqrt(jnp.mean(u * u, axis=-1, keepdims=True) + EPS)
        uh = u * r
        wv = w_ref[...]
        err = uh * wv - t_ref[...]
        loss_ref[...] += (0.5 / d) * jnp.sum(err * err)
        dy = err * (1.0 / d)
        dw_ref[...] += jnp.sum(dy * uh, axis=0, keepdims=True)
        gw = dy * wv
        du = r * (gw - uh * jnp.mean(gw * uh, axis=-1, keepdims=True))
        dout_ref[...] = du
        doutb_ref[...] = du.astype(BF16)

    row = pl.BlockSpec((ROWS, d), lambda i: (i, 0))
    vec = pl.BlockSpec((1, d), lambda i: (0, 0))
    return pl.pallas_call(
        kern, name="final_fwd_bwd",
        out_shape=(jax.ShapeDtypeStruct((t, d), F32), jax.ShapeDtypeStruct((t, d), BF16),
                   jax.ShapeDtypeStruct((1, LANES), F32), jax.ShapeDtypeStruct((1, d), F32)),
        grid=(t // ROWS,),
        in_specs=[row, row, row, vec],
        out_specs=(row, row, pl.BlockSpec((1, LANES), lambda i: (0, 0)), vec),
        compiler_params=_cparams("arbitrary"),
    )(x2, mo, target, w)


def _merge_fwd(proj2, ya, ys):
    t = ya.shape[0]
    gblk = GATE0 // D_MODEL

    def kern(ga_ref, gs_ref, ya_ref, ys_ref, o_ref):
        o_ref[...] = (_sigmoid(ga_ref[...]) * ya_ref[...] + _sigmoid(gs_ref[...]) * ys_ref[...]).astype(BF16)

    row = pl.BlockSpec((ROWS, D_MODEL), lambda i: (i, 0))
    return pl.pallas_call(
        kern, name="merge_fwd",
        out_shape=jax.ShapeDtypeStruct((t, D_MODEL), BF16),
        grid=(t // ROWS,),
        in_specs=[pl.BlockSpec((ROWS, D_MODEL), lambda i: (i, gblk)),
                  pl.BlockSpec((ROWS, D_MODEL), lambda i: (i, gblk + 1)), row, row],
        out_specs=row,
        compiler_params=_cparams("parallel"),
    )(proj2, proj2, ya, ys)


def _merge_bwd(dm, proj2, ya, ys):
    t = ya.shape[0]
    gblk = GATE0 // D_MODEL

    def kern(dm_ref, ga_ref, gs_ref, ya_ref, ys_ref, dya_ref, dys_ref, dg_ref):
        g = dm_ref[...]
        sa = _sigmoid(ga_ref[...])
        ss = _sigmoid(gs_ref[...])
        dya_ref[...] = (g * sa).astype(BF16)
        dys_ref[...] = (g * ss).astype(BF16)
        dg_ref[:, :D_MODEL] = (g * ya_ref[...] * sa * (1.0 - sa)).astype(BF16)
        dg_ref[:, D_MODEL:] = (g * ys_ref[...] * ss * (1.0 - ss)).astype(BF16)

    row = pl.BlockSpec((ROWS, D_MODEL), lambda i: (i, 0))
    return pl.pallas_call(
        kern, name="merge_bwd",
        out_shape=(jax.ShapeDtypeStruct((t, D_MODEL), BF16), jax.ShapeDtypeStruct((t, D_MODEL), BF16),
                   jax.ShapeDtypeStruct((t, NP), BF16)),
        grid=(t // ROWS,),
        in_specs=[row, pl.BlockSpec((ROWS, D_MODEL), lambda i: (i, gblk)),
                  pl.BlockSpec((ROWS, D_MODEL), lambda i: (i, gblk + 1)), row, row],
        out_specs=(row, row, pl.BlockSpec((ROWS, 2 * D_MODEL), lambda i: (i, GATE0 // (2 * D_MODEL)))),
        compiler_params=_cparams("parallel"),
    )(dm, proj2, proj2, ya, ys)


TQ = 256
TK = 256
assert TQ == TK
HEAD_LANES = (slice(0, HEAD_DIM), slice(HEAD_DIM, 2 * HEAD_DIM))


def _tri(pred):
    r = lax.broadcasted_iota(jnp.int32, (TK, TK), 0)
    c = lax.broadcasted_iota(jnp.int32, (TK, TK), 1)
    return pred(r, c).astype(BF16)


def _split_bf16(v):
    hi = v.astype(BF16)
    lo = (v - hi.astype(F32)).astype(BF16)
    return hi, lo


def _tri_dot(v, tri):
    hi, lo = _split_bf16(v)
    return _dot(hi, tri, NN) + _dot(lo, tri, NN)


def _sb_logs(z, mask):
    l1p = jnp.log(1.0 + jnp.exp(-jnp.abs(z)))
    lb = jnp.minimum(z, 0.0) - l1p
    lom = -jnp.maximum(z, 0.0) - l1p
    if mask is not None:
        lom = jnp.where(mask, lom, 0.0)
    return lb, lom


def _sb_weights(lb, later, carry_r, mask):
    a = jnp.exp(lb + (later + carry_r))
    if mask is not None:
        a = jnp.where(mask, a, 0.0)
    return a


DEAD = -104.0


def _while_alive(n, carry, step):
    def alive(cr):
        return jnp.max(jnp.maximum(cr[0][0], cr[1][0])) > DEAD

    def cond(state):
        jj, go, _ = state
        return jnp.logical_and(jj < n, go)

    def body(state):
        jj, _, cr = state
        cr = step(jj, cr)
        return jj + 1, alive(cr), cr

    return lax.while_loop(cond, body, (jnp.int32(0), alive(carry), carry))[2]


Q_LANES, K_LANES, V_LANES, ZA_LANES = (slice(i * LANES, (i + 1) * LANES) for i in range(4))


def _split_heads(dst, src, scale=None):
    for h, lanes in enumerate(HEAD_LANES):
        v = src[:, lanes]
        dst[h] = (v if scale is None else v * scale).astype(BF16)


def _attn_fwd(proj3):
    b, s, _ = proj3.shape
    nq = s // TQ
    scale = HEAD_DIM ** -0.5

    def kern(x_ref, o_ref, yp_ref, qs, ks, vs):
        _split_heads(qs, x_ref[0, :, Q_LANES], scale)
        _split_heads(ks, x_ref[0, :, K_LANES])
        _split_heads(vs, x_ref[0, :, V_LANES])
        za_ref = x_ref.at[:, :, ZA_LANES]
        row = lax.broadcasted_iota(jnp.int32, (TQ, TK), 0)
        col = lax.broadcasted_iota(jnp.int32, (TQ, TK), 1)
        tri_gt = _tri(lambda j, sk: j > sk)

        def q_block(i, _):
            top = isinstance(i, int)
            r0 = i * TQ if top else pl.multiple_of(i * TQ, TQ)
            qh = [qs[h, pl.ds(r0, TQ), :] for h in range(2)]

            def k_blocks(blocks, carry):
                nb = range(len(blocks))
                kh = [[ks[h, pl.ds(c0, TK), :] for h in range(2)] for c0, _ in blocks]
                vh = [[vs[h, pl.ds(c0, TK), :] for h in range(2)] for c0, _ in blocks]
                z = [[_dot(qh[h], kh[bl][h], NT) for h in range(2)] for bl in nb]
                logs = [[_sb_logs(z[bl][h], blocks[bl][1]) for h in range(2)] for bl in nb]
                later = [[_tri_dot(logs[bl][h][1], tri_gt) for h in range(2)] for bl in nb]
                out = []
                for h in range(2):
                    carry_r, acc = carry[h]
                    for bl in nb:
                        lb, lom = logs[bl][h]
                        a = _sb_weights(lb, later[bl][h], carry_r, blocks[bl][1])
                        acc = acc + _dot(a.astype(BF16), vh[bl][h], NN)
                        carry_r = carry_r + (later[bl][h][:, 0:1] + lom[:, 0:1])
                    out.append((carry_r, acc))
                return tuple(out)

            start = (jnp.zeros((TQ, 1), F32), jnp.zeros((TQ, HEAD_DIM), F32))
            diag = (r0, col < row)
            if top:
                carry = k_blocks([diag], (start, start))
            else:
                carry = k_blocks([diag, (pl.multiple_of(r0 - TK, TK), None)], (start, start))
                carry = _while_alive(i - 1, carry, lambda jj, cr: k_blocks([(pl.multiple_of((i - 2 - jj) * TK, TK), None)], cr))
            for (_, acc), lanes in zip(carry, HEAD_LANES):
                o_ref[0, pl.ds(r0, TQ), lanes] = acc
                za = za_ref[0, pl.ds(r0, TQ), lanes]
                yp_ref[0, pl.ds(r0, TQ), lanes] = (acc * (za * _sigmoid(za))).astype(BF16)
            return 0

        q_block(0, 0)
        lax.fori_loop(1, nq, q_block, 0)

    out_spec = pl.BlockSpec((1, s, LANES), lambda bi, hp: (bi, 0, hp))
    return pl.pallas_call(
        kern, name="attn_fwd",
        out_shape=(jax.ShapeDtypeStruct((b, s, D_MODEL), F32), jax.ShapeDtypeStruct((b, s, D_MODEL), BF16)),
        grid=(b, SB_HEADS // 2),
        in_specs=[pl.BlockSpec((1, s, HP_WIDTH), lambda bi, hp: (bi, 0, hp))],
        out_specs=(out_spec, out_spec),
        scratch_shapes=[pltpu.VMEM((2, s, HEAD_DIM), BF16)] * 3,
        compiler_params=_cparams("parallel", "parallel"),
    )(proj3)


def _attn_bwd(proj3, dyp3, o3, dproj3):
    b, s, _ = proj3.shape
    nq = s // TQ
    scale = HEAD_DIM ** -0.5

    def kern(x_ref, dyp_ref, o_ref, _, d_ref, qs, ks, vs, dos, dk_acc, dv_acc):
        _split_heads(qs, x_ref[0, :, Q_LANES], scale)
        _split_heads(ks, x_ref[0, :, K_LANES])
        _split_heads(vs, x_ref[0, :, V_LANES])
        dq_ref, dk_ref, dv_ref = (d_ref.at[:, :, lanes] for lanes in (Q_LANES, K_LANES, V_LANES))
        za = x_ref[0, :, ZA_LANES]
        sg = _sigmoid(za)
        dyp = dyp_ref[0]
        _split_heads(dos, dyp * (za * sg))
        d_ref[0, :, ZA_LANES] = (dyp * o_ref[0] * (sg * (1.0 + za * (1.0 - sg)))).astype(BF16)
        dk_acc[...] = jnp.zeros_like(dk_acc)
        dv_acc[...] = jnp.zeros_like(dv_acc)
        row = lax.broadcasted_iota(jnp.int32, (TQ, TK), 0)
        col = lax.broadcasted_iota(jnp.int32, (TQ, TK), 1)
        tri_gt = _tri(lambda j, sk: j > sk)
        tri_ge = _tri(lambda j, sk: j >= sk)

        def q_block(i, _):
            top = isinstance(i, int)
            r0 = i * TQ if top else pl.multiple_of(i * TQ, TQ)
            qh = [qs[h, pl.ds(r0, TQ), :] for h in range(2)]
            doh = [dos[h, pl.ds(r0, TQ), :] for h in range(2)]
            totals = [jnp.sum(doh[h].astype(F32) * o_ref[0, pl.ds(r0, TQ), lanes], axis=1, keepdims=True)
                      for h, lanes in enumerate(HEAD_LANES)]

            def k_blocks(blocks, carry):
                nb = range(len(blocks))
                kh = [[ks[h, pl.ds(c0, TK), :] for h in range(2)] for c0, _ in blocks]
                vh = [[vs[h, pl.ds(c0, TK), :] for h in range(2)] for c0, _ in blocks]
                z = [[_dot(qh[h], kh[bl][h], NT) for h in range(2)] for bl in nb]
                da = [[_dot(doh[h], vh[bl][h], NT) for h in range(2)] for bl in nb]
                logs = [[_sb_logs(z[bl][h], blocks[bl][1]) for h in range(2)] for bl in nb]
                later = [[_tri_dot(logs[bl][h][1], tri_gt) for h in range(2)] for bl in nb]
                ab, g, suffix = ([[None, None] for _ in nb] for _ in range(3))
                for h in range(2):
                    cr = carry[h][0]
                    for bl in nb:
                        a = _sb_weights(logs[bl][h][0], later[bl][h], cr, blocks[bl][1])
                        ab[bl][h] = a.astype(BF16)
                        g[bl][h] = da[bl][h] * ab[bl][h].astype(F32)
                        suffix[bl][h] = _tri_dot(g[bl][h], tri_ge)
                        cr = cr + (later[bl][h][:, 0:1] + logs[bl][h][1][:, 0:1])
                out = []
                for h in range(2):
                    _, carry_g, dq = carry[h]
                    cr = carry[h][0]
                    for bl in nb:
                        c0, mask = blocks[bl]
                        lb, lom = logs[bl][h]
                        dz = g[bl][h] - (g[bl][h] + (totals[h] - carry_g) - suffix[bl][h]) * jnp.exp(lb)
                        if mask is not None:
                            dz = jnp.where(mask, dz, 0.0)
                        dzb = dz.astype(BF16)
                        dk_acc[h, pl.ds(c0, TK), :] += _dot(dzb, qh[h], TN)
                        dv_acc[h, pl.ds(c0, TK), :] += _dot(ab[bl][h], doh[h], TN)
                        dq = dq + _dot(dzb, kh[bl][h], NN)
                        carry_g = carry_g + suffix[bl][h][:, 0:1]
                        cr = cr + (later[bl][h][:, 0:1] + lom[:, 0:1])
                    out.append((cr, carry_g, dq))
                return tuple(out)

            def k_block(c0, carry, mask):
                kh = [ks[h, pl.ds(c0, TK), :] for h in range(2)]
                vh = [vs[h, pl.ds(c0, TK), :] for h in range(2)]
                z = [_dot(qh[h], kh[h], NT) for h in range(2)]
                da = [_dot(doh[h], vh[h], NT) for h in range(2)]
                logs, later = [], []
                for h in range(2):
                    logs.append(_sb_logs(z[h], mask))
                    later.append(_tri_dot(logs[h][1], tri_gt))
                ab, g, suffix = [], [], []
                for h in range(2):
                    a = _sb_weights(logs[h][0], later[h], carry[h][0], mask)
                    ab.append(a.astype(BF16))
                    g.append(da[h] * ab[h].astype(F32))
                    suffix.append(_tri_dot(g[h], tri_ge))
                out = []
                for h in range(2):
                    carry_r, carry_g, dq = carry[h]
                    lb, lom = logs[h]
                    dz = g[h] - (g[h] + (totals[h] - carry_g) - suffix[h]) * jnp.exp(lb)
                    if mask is not None:
                        dz = jnp.where(mask, dz, 0.0)
                    dzb = dz.astype(BF16)
                    dk_acc[h, pl.ds(c0, TK), :] += _dot(dzb, qh[h], TN)
                    dv_acc[h, pl.ds(c0, TK), :] += _dot(ab[h], doh[h], TN)
                    out.append((carry_r + (later[h][:, 0:1] + lom[:, 0:1]), carry_g + suffix[h][:, 0:1],
                                dq + _dot(dzb, kh[h], NN)))
                return tuple(out)

            zero = jnp.zeros((TQ, 1), F32)
            start = (zero, zero, jnp.zeros((TQ, HEAD_DIM), F32))
            diag = (r0, col < row)
            if top:
                carry = k_block(r0, (start, start), col < row)
            else:
                carry = k_blocks([diag, (pl.multiple_of(r0 - TK, TK), None)], (start, start))
                carry = _while_alive(i - 1, carry, lambda jj, cr: k_block(pl.multiple_of((i - 2 - jj) * TK, TK), cr, None))
            for (_, _, dq), lanes in zip(carry, HEAD_LANES):
                dq_ref[0, pl.ds(r0, TQ), lanes] = (dq * scale).astype(BF16)
            return 0

        q_block(0, 0)
        lax.fori_loop(1, nq, q_block, 0)

        for h, lanes in enumerate(HEAD_LANES):
            dk_ref[0, :, lanes] = dk_acc[h].astype(BF16)
            dv_ref[0, :, lanes] = dv_acc[h].astype(BF16)

    plain = pl.BlockSpec((1, s, LANES), lambda bi, hp: (bi, 0, hp))
    pair = pl.BlockSpec((1, s, HP_WIDTH), lambda bi, hp: (bi, 0, hp))
    return pl.pallas_call(
        kern, name="attn_bwd",
        out_shape=jax.ShapeDtypeStruct(dproj3.shape, dproj3.dtype),
        grid=(b, SB_HEADS // 2),
        in_specs=[pair, plain, plain, ANY],
        out_specs=pair,
        input_output_aliases={3: 0},
        scratch_shapes=[pltpu.VMEM((2, s, HEAD_DIM), BF16)] * 4 + [pltpu.VMEM((2, s, HEAD_DIM), F32)] * 2,
        compiler_params=_cparams("parallel", "parallel"),
    )(proj3, dyp3, o3, dproj3)


CONV_COLS = 256
HALO = 8


def _conv_pre(xp, w_ref, b_ref, r0):
    pre = b_ref[...] + w_ref[CONV_K - 1:CONV_K, :] * xp[pl.ds(HALO + r0, CHUNK), :]
    for kk in range(1, CONV_K):
        pre = pre + w_ref[CONV_K - 1 - kk:CONV_K - kk, :] * xp[pl.ds(HALO + r0 - kk, CHUNK), :]
    return pre


def _conv_fwd(proj3, conv_w, conv_b):
    b, s, _ = proj3.shape
    nc = s // CHUNK

    def kern(x_ref, w_ref, b_ref, o_ref, xp):
        xp[0:HALO, :] = jnp.zeros((HALO, CONV_COLS), F32)
        xp[HALO:, :] = x_ref[0]
        for ci in range(nc):
            pre = _conv_pre(xp, w_ref, b_ref, ci * CHUNK)
            o_ref[0, ci * CHUNK:(ci + 1) * CHUNK, :] = pre * _sigmoid(pre)

    return pl.pallas_call(
        kern, name="conv_fwd",
        out_shape=jax.ShapeDtypeStruct((b, s, CONV_DIM), F32),
        grid=(CONV_DIM // CONV_COLS, b),
        in_specs=[pl.BlockSpec((1, s, CONV_COLS), lambda j, bi: (bi, 0, XBC0 // CONV_COLS + j)),
                  pl.BlockSpec((CONV_K, CONV_COLS), lambda j, bi: (0, j)),
                  pl.BlockSpec((1, CONV_COLS), lambda j, bi: (0, j))],
        out_specs=pl.BlockSpec((1, s, CONV_COLS), lambda j, bi: (bi, 0, j)),
        scratch_shapes=[pltpu.VMEM((s + HALO, CONV_COLS), F32)],
        compiler_params=_cparams("parallel", "parallel"),
    )(proj3, conv_w, conv_b)


def _conv_bwd(dact, proj3, conv_w, conv_b, col0, name, dproj3):
    b, s, width = dact.shape
    nc = s // CHUNK
    j0 = col0 // CONV_COLS

    def kern(da_ref, x_ref, w_ref, b_ref, _, dx_ref, dw_ref, db_ref, xp, dp):
        @pl.when(pl.program_id(1) == 0)
        def _():
            dw_ref[...] = jnp.zeros_like(dw_ref)
            db_ref[...] = jnp.zeros_like(db_ref)

        xp[0:HALO, :] = jnp.zeros((HALO, CONV_COLS), F32)
        xp[HALO:, :] = x_ref[0]
        dp[s:, :] = jnp.zeros((HALO, CONV_COLS), F32)
        for ci in range(nc):
            r0 = ci * CHUNK
            pre = _conv_pre(xp, w_ref, b_ref, r0)
            sg = _sigmoid(pre)
            dpre = da_ref[0, r0:r0 + CHUNK, :] * (sg * (1.0 + pre * (1.0 - sg)))
            dp[r0:r0 + CHUNK, :] = dpre
            db_ref[...] += jnp.sum(dpre, axis=0, keepdims=True)
            for kk in range(CONV_K):
                tap = CONV_K - 1 - kk
                dw_ref[tap:tap + 1, :] += jnp.sum(dpre * xp[pl.ds(HALO + r0 - kk, CHUNK), :], axis=0, keepdims=True)
        for ci in range(nc):
            r0 = ci * CHUNK
            dx = w_ref[CONV_K - 1:CONV_K, :] * dp[pl.ds(r0, CHUNK), :]
            for kk in range(1, CONV_K):
                dx = dx + w_ref[CONV_K - 1 - kk:CONV_K - kk, :] * dp[pl.ds(r0 + kk, CHUNK), :]
            dx_ref[0, r0:r0 + CHUNK, :] = dx.astype(BF16)

    return pl.pallas_call(
        kern, name=name,
        out_shape=(jax.ShapeDtypeStruct(dproj3.shape, dproj3.dtype), jax.ShapeDtypeStruct((CONV_K, width), F32),
                   jax.ShapeDtypeStruct((1, width), F32)),
        grid=(width // CONV_COLS, b),
        in_specs=[pl.BlockSpec((1, s, CONV_COLS), lambda j, bi: (bi, 0, j)),
                  pl.BlockSpec((1, s, CONV_COLS), lambda j, bi: (bi, 0, XBC0 // CONV_COLS + j0 + j)),
                  pl.BlockSpec((CONV_K, CONV_COLS), lambda j, bi: (0, j0 + j)),
                  pl.BlockSpec((1, CONV_COLS), lambda j, bi: (0, j0 + j)), ANY],
        out_specs=(pl.BlockSpec((1, s, CONV_COLS), lambda j, bi: (bi, 0, XBC0 // CONV_COLS + j0 + j)),
                   pl.BlockSpec((CONV_K, CONV_COLS), lambda j, bi: (0, j)),
                   pl.BlockSpec((1, CONV_COLS), lambda j, bi: (0, j))),
        input_output_aliases={4: 0},
        scratch_shapes=[pltpu.VMEM((s + HALO, CONV_COLS), F32)] * 2,
        compiler_params=_cparams("parallel", "arbitrary"),
    )(dact, proj3, conv_w, conv_b, dproj3)


def _sel_dot(v, sel, left=False):
    hi = v.astype(BF16)
    rest = v - hi.astype(F32)
    mid = rest.astype(BF16)
    lo = (rest - mid.astype(F32)).astype(BF16)
    if left:
        return _dot(sel, hi, NN) + _dot(sel, mid, NN) + _dot(sel, lo, NN)
    return _dot(hi, sel, NN) + _dot(mid, sel, NN) + _dot(lo, sel, NN)


def _ssd_common(dtr_ref, dtb_ref, alog_ref):
    lane = lax.broadcasted_iota(jnp.int32, (CHUNK, LANES), 1)
    row = lax.broadcasted_iota(jnp.int32, (CHUNK, LANES), 0)
    head_lane = lane < HEADS_PER_GROUP
    pre = dtr_ref[0, 0] + dtb_ref[0]
    dt = jnp.where(head_lane, jnp.maximum(pre, 0.0) + jnp.log(1.0 + jnp.exp(-jnp.abs(pre))), 0.0)
    a = jnp.where(head_lane[0:1], -jnp.exp(alog_ref[0]), 0.0)
    tril = (row >= lane).astype(BF16)
    acs = _sel_dot(dt * a, tril, left=True)
    acs_t = acs.T
    er = lax.broadcasted_iota(jnp.int32, (LANES, GROUP_WIDTH), 0)
    ec = lax.broadcasted_iota(jnp.int32, (LANES, GROUP_WIDTH), 1)
    expand = ((ec // HEAD_DIM) == er).astype(BF16)
    tr = lax.broadcasted_iota(jnp.int32, (GROUP_WIDTH, LANES), 0)
    tc = lax.broadcasted_iota(jnp.int32, (GROUP_WIDTH, LANES), 1)
    reduce = ((tr // HEAD_DIM) == tc).astype(BF16)
    dt_x = _sel_dot(dt, expand)
    acs_x = _sel_dot(acs, expand)
    end_x = acs_x[CHUNK - 1:CHUNK, :]
    causal = row >= lane
    return dict(dt=dt, a=a, pre=pre, head_lane=head_lane, acs=acs, acs_t=acs_t, expand=expand, reduce=reduce,
                dt_x=dt_x, acs_x=acs_x, end_x=end_x, causal=causal, row=row, lane=lane)


def _ssd_decay(cm, h):
    seg = cm["acs"][:, h:h + 1] - cm["acs_t"][h:h + 1, :]
    return jnp.where(cm["causal"], jnp.exp(jnp.minimum(seg, 0.0)), 0.0)


def _ssd_fwd(xact, proj3, dtr_g, dtb_g, alog_g, dskip_x, snw):
    b, s, _ = xact.shape
    nc = s // CHUNK
    g4 = SSD_GROUPS

    def kern(xs_ref, bm_ref, cm_ref, zs_ref, dtr_ref, dtb_ref, alog_ref, dsk_ref, snw_ref,
             y_ref, yn_ref, hst_ref, h_sc):
        @pl.when(pl.program_id(2) == 0)
        def _():
            h_sc[...] = jnp.zeros_like(h_sc)

        cm = _ssd_common(dtr_ref, dtb_ref, alog_ref)
        x = xs_ref[0]
        bmb = bm_ref[0].astype(BF16)
        cmb = cm_ref[0].astype(BF16)
        h_in = h_sc[...]
        hst_ref[0, 0, 0] = h_in
        xdt = x * cm["dt_x"]
        xdtb = xdt.astype(BF16)
        cb = _dot(cmb, bmb, NT)
        y_off = _dot(cmb, h_in.astype(BF16), NN) * jnp.exp(cm["acs_x"])
        for h in range(HEADS_PER_GROUP):
            lanes = slice(h * HEAD_DIM, (h + 1) * HEAD_DIM)
            m = (cb * _ssd_decay(cm, h)).astype(BF16)
            y_ref[0, :, lanes] = _dot(m, xdtb[:, lanes], NN)
        y = y_ref[0] + y_off + x * dsk_ref[...]
        y_ref[0] = y
        w = (xdt * jnp.exp(cm["end_x"] - cm["acs_x"])).astype(BF16)
        h_sc[...] = h_in * jnp.exp(cm["end_x"]) + _dot(bmb, w, TN)
        zs = zs_ref[0]
        y2 = y * (zs * _sigmoid(zs))
        yn_ref[0] = (y2 * lax.rsqrt(jnp.mean(y2 * y2, axis=-1, keepdims=True) + EPS) * snw_ref[...]).astype(BF16)

    gw = GROUP_WIDTH
    small = pl.BlockSpec((1, 1, LANES), lambda gi, bi, ci: (gi, 0, 0))
    xblk = pl.BlockSpec((1, CHUNK, gw), lambda gi, bi, ci: (bi, ci, gi))
    return pl.pallas_call(
        kern, name="ssd_fwd",
        out_shape=(jax.ShapeDtypeStruct((b, s, SSD_WIDTH), F32), jax.ShapeDtypeStruct((b, s, SSD_WIDTH), BF16),
                   jax.ShapeDtypeStruct((b, nc, g4, SSD_STATE, gw), F32)),
        grid=(g4, b, nc),
        in_specs=[xblk,
                  pl.BlockSpec((1, CHUNK, LANES), lambda gi, bi, ci: (bi, ci, SSD_WIDTH // LANES + gi)),
                  pl.BlockSpec((1, CHUNK, LANES), lambda gi, bi, ci: (bi, ci, SSD_WIDTH // LANES + g4 + gi)),
                  pl.BlockSpec((1, CHUNK, gw), lambda gi, bi, ci: (bi, ci, ZS0 // gw + gi)),
                  pl.BlockSpec((1, 1, CHUNK, LANES), lambda gi, bi, ci: (bi, gi, ci, 0)),
                  small, small,
                  pl.BlockSpec((1, gw), lambda gi, bi, ci: (0, gi)),
                  pl.BlockSpec((1, gw), lambda gi, bi, ci: (0, gi))],
        out_specs=(xblk, xblk, pl.BlockSpec((1, 1, 1, SSD_STATE, gw), lambda gi, bi, ci: (bi, ci, gi, 0, 0))),
        scratch_shapes=[pltpu.VMEM((SSD_STATE, gw), F32)],
        compiler_params=_cparams("parallel", "parallel", "arbitrary"),
    )(xact, xact, xact, proj3, dtr_g, dtb_g, alog_g, dskip_x, snw)


def _ssd_bwd(dyn3, y3, xact, proj3, hst, dtr_g, dtb_g, alog_g, dskip_x, snw, dproj3):
    b, s, _ = xact.shape
    nc = s // CHUNK
    g4 = SSD_GROUPS
    gw = GROUP_WIDTH

    def kern(dyn_ref, y_ref, xs_ref, bm_ref, cm_ref, zs_ref, hst_ref, dtr_ref, dtb_ref, alog_ref, dsk_ref, snw_ref, _,
             dxs_ref, dbm_ref, dcm_ref, dzs_ref, ddtr_ref, dsnw_ref, dalog_ref, ddtb_ref, ddsk_ref, dh_sc):
        first = jnp.logical_and(pl.program_id(1) == 0, pl.program_id(2) == 0)

        @pl.when(first)
        def _():
            dsnw_ref[...] = jnp.zeros_like(dsnw_ref)
            dalog_ref[...] = jnp.zeros_like(dalog_ref)
            ddtb_ref[...] = jnp.zeros_like(ddtb_ref)
            ddsk_ref[...] = jnp.zeros_like(ddsk_ref)

        @pl.when(pl.program_id(2) == 0)
        def _():
            dh_sc[...] = jnp.zeros_like(dh_sc)

        cm = _ssd_common(dtr_ref, dtb_ref, alog_ref)
        row, lane = cm["row"], cm["lane"]
        y = y_ref[0]
        zs = zs_ref[0]
        sg = _sigmoid(zs)
        silu = zs * sg
        y2 = y * silu
        rstd = lax.rsqrt(jnp.mean(y2 * y2, axis=-1, keepdims=True) + EPS)
        y2h = y2 * rstd
        dyn = dyn_ref[0]
        dsnw_ref[0] += jnp.sum(dyn * y2h, axis=0, keepdims=True)
        gwv = dyn * snw_ref[...]
        dy2 = rstd * (gwv - y2h * jnp.mean(gwv * y2h, axis=-1, keepdims=True))
        dzs_ref[0] = (dy2 * y * (sg * (1.0 + zs * (1.0 - sg)))).astype(BF16)
        dy = dy2 * silu
        dyb = dy.astype(BF16)

        x = xs_ref[0]
        bmb = bm_ref[0].astype(BF16)
        cmb = cm_ref[0].astype(BF16)
        h_in = hst_ref[0, 0, 0]
        h_inb = h_in.astype(BF16)
        d_hn = dh_sc[...]
        d_hnb = d_hn.astype(BF16)
        xdt = x * cm["dt_x"]
        xdtb = xdt.astype(BF16)
        eacs = jnp.exp(cm["acs_x"])
        dte = jnp.exp(cm["end_x"] - cm["acs_x"])
        wb = (xdt * dte).astype(BF16)

        dsk_lanes = jnp.broadcast_to(jnp.sum(dy * x, axis=0, keepdims=True), (8, gw))
        ddsk_ref[0] += _sel_dot(dsk_lanes, cm["reduce"])[0:1, :]
        dyo = dy * eacs
        dyob = dyo.astype(BF16)
        dacs_x = dyo * _dot(cmb, h_inb, NN)
        dcm = _dot(dyob, h_inb, NT)
        dh_in = _dot(cmb, dyob, TN)
        dw = _dot(bmb, d_hnb, NN)
        dbm = _dot(wb, d_hnb, NT)
        dxdt = dw * dte
        e_l = dw * xdt * dte
        dacs_x = dacs_x - e_l
        dend_x = jnp.sum(e_l, axis=0, keepdims=True)
        chunk_decay = jnp.exp(cm["end_x"])
        dh_sc[...] = d_hn * chunk_decay + dh_in
        dend_x = dend_x + jnp.sum(d_hn * h_in, axis=0, keepdims=True) * chunk_decay
        last_row = lax.broadcasted_iota(jnp.int32, (CHUNK, gw), 0) == CHUNK - 1
        dacs_x = dacs_x + jnp.where(last_row, dend_x, 0.0)

        cb = _dot(cmb, bmb, NT)
        dcb = jnp.zeros((CHUNK, CHUNK), F32)
        dacs = jnp.zeros((CHUNK, LANES), F32)
        dacs_t = jnp.zeros((LANES, CHUNK), F32)
        for h in range(HEADS_PER_GROUP):
            lanes = slice(h * HEAD_DIM, (h + 1) * HEAD_DIM)
            decay = _ssd_decay(cm, h)
            m = cb * decay
            dm = _dot(dyb[:, lanes], xdtb[:, lanes], NT)
            dxs_ref[0, :, lanes] = _dot(m.astype(BF16), dyb[:, lanes], TN)
            dcb_h = dm * decay
            dcb = dcb + dcb_h
            n = dcb_h * cb
            dacs = dacs + jnp.where(lane == h, jnp.sum(n, axis=1, keepdims=True), 0.0)
            dacs_t = dacs_t + jnp.where(row == h, jnp.sum(n, axis=0, keepdims=True), 0.0)
        dcbb = dcb.astype(BF16)
        dcm_ref[0] = dcm + _dot(dcbb, bmb, NN)
        dbm_ref[0] = dbm + _dot(dcbb, cmb, TN)
        dxdt = dxdt + dxs_ref[0]
        dxs_ref[0] = dy * dsk_ref[...] + dxdt * cm["dt_x"]

        dacs = dacs - dacs_t.T + _sel_dot(dacs_x, cm["reduce"])
        ddt = _sel_dot(dxdt * x, cm["reduce"])
        triu = (row <= lane).astype(BF16)
        rc = _sel_dot(dacs, triu, left=True)
        ddt = ddt + cm["a"] * rc
        dalog_ref[0] += jnp.sum(cm["dt"] * rc, axis=0, keepdims=True) * cm["a"]
        ddtr = jnp.where(cm["head_lane"], ddt * _sigmoid(cm["pre"]), 0.0)
        ddtr_ref[0, 0] = ddtr
        ddtb_ref[0] += jnp.sum(ddtr, axis=0, keepdims=True)

    def rev(ci):
        return nc - 1 - ci

    small = pl.BlockSpec((1, 1, LANES), lambda gi, bi, ci: (gi, 0, 0))
    xblk = pl.BlockSpec((1, CHUNK, gw), lambda gi, bi, ci: (bi, rev(ci), gi))
    nblk = pl.BlockSpec((1, CHUNK, LANES), lambda gi, bi, ci: (bi, rev(ci), gi))
    gvec = pl.BlockSpec((1, gw), lambda gi, bi, ci: (0, gi))
    gacc = pl.BlockSpec((1, 1, gw), lambda gi, bi, ci: (gi, 0, 0))
    return pl.pallas_call(
        kern, name="ssd_bwd",
        out_shape=(jax.ShapeDtypeStruct((b, s, SSD_WIDTH), F32),
                   jax.ShapeDtypeStruct((b, s, g4 * SSD_STATE), F32),
                   jax.ShapeDtypeStruct((b, s, g4 * SSD_STATE), F32),
                   jax.ShapeDtypeStruct(dproj3.shape, dproj3.dtype),
                   jax.ShapeDtypeStruct((b, g4, s, LANES), F32),
                   jax.ShapeDtypeStruct((g4, 1, gw), F32),
                   jax.ShapeDtypeStruct((g4, 1, LANES), F32),
                   jax.ShapeDtypeStruct((g4, 1, LANES), F32),
                   jax.ShapeDtypeStruct((g4, 1, LANES), F32)),
        grid=(g4, b, nc),
        in_specs=[xblk, xblk, xblk,
                  pl.BlockSpec((1, CHUNK, LANES), lambda gi, bi, ci: (bi, rev(ci), SSD_WIDTH // LANES + gi)),
                  pl.BlockSpec((1, CHUNK, LANES), lambda gi, bi, ci: (bi, rev(ci), SSD_WIDTH // LANES + g4 + gi)),
                  pl.BlockSpec((1, CHUNK, gw), lambda gi, bi, ci: (bi, rev(ci), ZS0 // gw + gi)),
                  pl.BlockSpec((1, 1, 1, SSD_STATE, gw), lambda gi, bi, ci: (bi, rev(ci), gi, 0, 0)),
                  pl.BlockSpec((1, 1, CHUNK, LANES), lambda gi, bi, ci: (bi, gi, rev(ci), 0)),
                  small, small, gvec, gvec, ANY],
        out_specs=(xblk, nblk, nblk,
                   pl.BlockSpec((1, CHUNK, gw), lambda gi, bi, ci: (bi, rev(ci), ZS0 // gw + gi)),
                   pl.BlockSpec((1, 1, CHUNK, LANES), lambda gi, bi, ci: (bi, gi, rev(ci), 0)),
                   gacc, small, small, small),
        input_output_aliases={12: 3},
        scratch_shapes=[pltpu.VMEM((SSD_STATE, gw), F32)],
        compiler_params=_cparams("parallel", "arbitrary", "arbitrary"),
    )(dyn3, y3, xact, xact, xact, proj3, hst, dtr_g, dtb_g, alog_g, dskip_x, snw, dproj3)


def _adamw(w, g, m, v, name):
    r, c = w.shape
    tr = 128 if r % 128 == 0 else r
    tc = LANES if (tr == r and r > 128 and c % LANES == 0) else c

    def kern(w_ref, g_ref, m_ref, v_ref, d_ref, nm_ref, nv_ref):
        gv = g_ref[...]
        nm = ADAM_B1 * m_ref[...] + (1.0 - ADAM_B1) * gv
        nv = ADAM_B2 * v_ref[...] + (1.0 - ADAM_B2) * (gv * gv)
        m_hat = nm / (1.0 - ADAM_B1 ** ADAM_STEP)
        v_hat = nv / (1.0 - ADAM_B2 ** ADAM_STEP)
        d_ref[...] = -ADAM_LR * (m_hat / (jnp.sqrt(v_hat) + ADAM_EPS) + ADAM_WD * w_ref[...])
        nm_ref[...] = nm
        nv_ref[...] = nv

    blk = pl.BlockSpec((tr, tc), lambda i, j: (i, j))
    out = jax.ShapeDtypeStruct((r, c), F32)
    return pl.pallas_call(
        kern, name=name, out_shape=(out, out, out), grid=(r // tr, c // tc),
        in_specs=[blk] * 4, out_specs=(blk, blk, blk),
        compiler_params=_cparams("parallel", "parallel"),
    )(w, g, m, v)


ANY = pl.BlockSpec(memory_space=pl.ANY)


def _position():
    return lax.axis_index("x"), lax.axis_index("y"), lax.axis_index("c")


def _other_chips(x, y):
    return [(1 - x, y), (x, 1 - y), (1 - x, 1 - y)]


def _dma_sems(n):
    return [pltpu.SemaphoreType.DMA((n,)), pltpu.SemaphoreType.DMA((n,))]


class _Exchange:
    def __init__(self, inputs, out_shapes, sems, start, finish):
        self.inputs, self.out_shapes, self.sems, self.start, self.finish = inputs, out_shapes, sems, start, finish


def _run_exchange(ex, name):
    n_in, n_out = len(ex.inputs), len(ex.out_shapes)

    def body(*refs):
        x_in, x_out, sems = refs[:n_in], refs[n_in:n_in + n_out], refs[n_in + n_out:]
        ex.start(x_in, x_out, sems)
        ex.finish(x_in, x_out, sems)

    return pl.pallas_call(
        body, name=name, out_shape=list(ex.out_shapes),
        in_specs=[ANY] * n_in, out_specs=[ANY] * n_out, scratch_shapes=list(ex.sems),
    )(*ex.inputs)


def _gather_exchange(shards):
    n = len(shards)

    def copies(p_refs, out_refs, sems):
        send_sems, recv_sems = sems
        x, y, c = _position()
        me = 2 * x + y
        chips = _other_chips(x, y)

        def slab(a, chip, hf):
            half = shards[a].shape[1] // 2
            return out_refs[a].at[chip, :, pl.ds(hf * half, half)]

        def my_half(a):
            half = shards[a].shape[1] // 2
            return p_refs[a].at[:, pl.ds(c * half, half)]

        def over_ici(a, j, chip_from):
            px, py = chips[j]
            return pltpu.make_async_remote_copy(
                src_ref=my_half(a), dst_ref=slab(a, chip_from, c),
                send_sem=send_sems.at[3 * a + j], recv_sem=recv_sems.at[3 * a + j],
                device_id=(px, py, c), device_id_type=MESH)

        def to_sibling(a, j, hf):
            px, py = chips[j]
            return pltpu.make_async_remote_copy(
                src_ref=slab(a, 2 * px + py, hf), dst_ref=slab(a, 2 * px + py, hf),
                send_sem=send_sems.at[3 * (n + a) + j], recv_sem=recv_sems.at[3 * (n + a) + j],
                device_id=(x, y, 1 - c), device_id_type=MESH)

        own = [pltpu.make_async_remote_copy(
            src_ref=p_refs[a], dst_ref=out_refs[a].at[me], send_sem=send_sems.at[6 * n + a], recv_sem=recv_sems.at[6 * n + a],
            device_id=(x, y, 1 - c), device_id_type=MESH) for a in range(n)]
        first = [over_ici(a, j, me) for a in range(n) for j in range(3)]
        return chips, c, over_ici, to_sibling, first, own

    def start(p_refs, out_refs, sems):
        _, _, _, _, first, own = copies(p_refs, out_refs, sems)
        for cp in first + own:
            cp.start()

    def finish(p_refs, out_refs, sems):
        chips, c, over_ici, to_sibling, first, own = copies(p_refs, out_refs, sems)
        passed = []
        for a in range(n):
            for j, (px, py) in enumerate(chips):
                over_ici(a, j, 2 * px + py).wait_recv()
                passed.append(to_sibling(a, j, c))
                passed[-1].start()
        for a in range(n):
            for j in range(3):
                to_sibling(a, j, 1 - c).wait_recv()
        for cp in first + passed:
            cp.wait_send()
        for cp in own:
            cp.wait()

    return _Exchange(list(shards), [jax.ShapeDtypeStruct((N_CHIPS, *v.shape), v.dtype) for v in shards],
                     _dma_sems(7 * n), start, finish)


def _swap_halves(parts, name):
    n = len(parts)

    def body(*refs):
        v_refs, out_refs = refs[:n], refs[n:2 * n]
        send_sems, recv_sems = refs[2 * n:]
        x, y, c = _position()
        copies = []
        for a in range(n):
            half = parts[a].shape[2] // 2
            copies.append(pltpu.make_async_remote_copy(
                src_ref=v_refs[a].at[:, :, pl.ds((1 - c) * half, half)], dst_ref=out_refs[a],
                send_sem=send_sems.at[a], recv_sem=recv_sems.at[a], device_id=(x, y, 1 - c), device_id_type=MESH))
        for cp in copies:
            cp.start()
        for cp in copies:
            cp.wait()

    return pl.pallas_call(
        body, name=name,
        out_shape=[jax.ShapeDtypeStruct((v.shape[0], v.shape[1], v.shape[2] // 2), v.dtype) for v in parts],
        in_specs=[ANY] * n, out_specs=[ANY] * n,
        scratch_shapes=_dma_sems(n),
    )(*parts)


def _all_to_all_exchange(parts):
    n = len(parts)

    def sends(p_refs, out_refs, sems):
        send_sems, recv_sems = sems
        x, y, c = _position()
        return [pltpu.make_async_remote_copy(
            src_ref=p_refs[a].at[2 * px + py], dst_ref=out_refs[a].at[j],
            send_sem=send_sems.at[3 * a + j], recv_sem=recv_sems.at[3 * a + j],
            device_id=(px, py, c), device_id_type=MESH) for a in range(n) for j, (px, py) in enumerate(_other_chips(x, y))]

    def start(p_refs, out_refs, sems):
        for cp in sends(p_refs, out_refs, sems):
            cp.start()

    def finish(p_refs, out_refs, sems):
        for cp in sends(p_refs, out_refs, sems):
            cp.wait()

    return _Exchange(list(parts), [jax.ShapeDtypeStruct((N_CHIPS - 1, *v.shape[1:]), v.dtype) for v in parts],
                     _dma_sems(3 * n), start, finish)


def _join_halves(wholes):
    n = len(wholes)

    def body(*refs):
        out_refs = refs[n:2 * n]
        send_sems, recv_sems = refs[2 * n:]
        x, y, c = _position()
        copies = []
        for a in range(n):
            half = wholes[a].shape[1] // 2
            mine = out_refs[a].at[:, pl.ds(c * half, half)]
            copies.append(pltpu.make_async_remote_copy(
                src_ref=mine, dst_ref=mine, send_sem=send_sems.at[a], recv_sem=recv_sems.at[a],
                device_id=(x, y, 1 - c), device_id_type=MESH))
        for cp in copies:
            cp.start()
        for cp in copies:
            cp.wait()

    return pl.pallas_call(
        body, name="grad_join_halves",
        out_shape=[jax.ShapeDtypeStruct(v.shape, v.dtype) for v in wholes],
        in_specs=[ANY] * n, out_specs=[ANY] * n,
        input_output_aliases={a: a for a in range(n)},
        scratch_shapes=_dma_sems(n),
    )(*wholes)


STRIP = 256


def _add_halves(g, sw, place, name):
    n, rows, cols = g.shape
    nb = cols // 2 // STRIP

    def kern(p_ref, g_ref, s_ref, o_ref):
        o_ref[...] = (g_ref[...] + s_ref[...]).astype(BF16)

    blk = pl.BlockSpec((1, rows, STRIP), lambda j, i, p_ref: (j, 0, i))
    return pl.pallas_call(
        kern, name=name,
        out_shape=jax.ShapeDtypeStruct((n, rows, cols // 2), BF16),
        grid_spec=pltpu.PrefetchScalarGridSpec(
            num_scalar_prefetch=1, grid=(n, nb),
            in_specs=[pl.BlockSpec((1, rows, STRIP), lambda j, i, p_ref: (j, 0, p_ref[0] * nb + i)), blk],
            out_specs=blk),
        compiler_params=_cparams("parallel", "parallel"),
    )(place, g, sw)


def _sum_chips(own, rx, place, name):
    _, rows, half = rx.shape
    nb = half // STRIP

    def kern(p_ref, own_ref, r_ref, o_ref):
        total = own_ref[0].astype(F32)
        for j in range(N_CHIPS - 1):
            total = total + r_ref[j].astype(F32)
        o_ref[...] = total

    return pl.pallas_call(
        kern, name=name,
        out_shape=jax.ShapeDtypeStruct((rows, 2 * half), F32),
        grid_spec=pltpu.PrefetchScalarGridSpec(
            num_scalar_prefetch=1, grid=(nb,),
            in_specs=[pl.BlockSpec((1, rows, STRIP), lambda i, p_ref: (p_ref[1], 0, i)),
                      pl.BlockSpec((N_CHIPS - 1, rows, STRIP), lambda i, p_ref: (0, 0, i))],
            out_specs=pl.BlockSpec((rows, STRIP), lambda i, p_ref: (0, p_ref[0] * nb + i))),
        compiler_params=_cparams("parallel"),
    )(place, own, rx)


def _gather_small(v, reduce, name):
    rows = v.shape[0]

    def body(v_ref, out_ref, buf, send_sems, recv_sems):
        x, y, c = _position()
        me = 4 * x + 2 * y + c
        buf[me] = v_ref[...]
        peers = [(x ^ (k >> 2), y ^ ((k >> 1) & 1), c ^ (k & 1)) for k in range(1, 8)]
        copies = [pltpu.make_async_remote_copy(
            src_ref=v_ref, dst_ref=buf.at[me],
            send_sem=send_sems.at[k], recv_sem=recv_sems.at[k],
            device_id=peer, device_id_type=MESH) for k, peer in enumerate(peers)]
        for cp in copies:
            cp.start()
        for k, (px, py, pc) in enumerate(peers):
            pltpu.make_async_remote_copy(
                src_ref=v_ref, dst_ref=buf.at[4 * px + 2 * py + pc],
                send_sem=send_sems.at[k], recv_sem=recv_sems.at[k],
                device_id=(px, py, pc), device_id_type=MESH).wait_recv()
        for cp in copies:
            cp.wait_send()
        if reduce:
            total = buf[0]
            for d in range(1, 8):
                total = total + buf[d]
            out_ref[...] = total
        else:
            out_ref[...] = buf[...]

    vm = pl.BlockSpec(memory_space=pltpu.VMEM)
    return pl.pallas_call(
        body, name=name,
        out_shape=jax.ShapeDtypeStruct((rows, LANES) if reduce else (8, rows, LANES), F32),
        in_specs=[vm], out_specs=vm,
        scratch_shapes=[pltpu.VMEM((8, rows, LANES), F32), pltpu.SemaphoreType.DMA((7,)), pltpu.SemaphoreType.DMA((7,))],
    )(v)


def _pad_rows(a, rows):
    return jnp.pad(a, ((0, rows - a.shape[0]), (0, 0)))


def _lane_pad(v):
    n = v.shape[1]
    return jnp.pad(v, ((0, 0), (0, -n % LANES)))


def _gather_all(w_in, w_attn_out, w_ssm_out, w_o, conv_w):
    d = D_MODEL
    w_in_t, = _run_exchange(_gather_exchange([w_in[0].T.astype(BF16)]), "gather_w_in")
    w_proj_t = _to_proj_layout(w_in_t.reshape(D_PROJ, d))
    out_w = _gather_exchange([a[0].astype(BF16) for a in (w_attn_out, w_ssm_out, w_o)])
    conv_rows = conv_w[0].size // LANES
    conv_all = _gather_small(conv_w[0].reshape(conv_rows, LANES), False, "gather_conv_w")
    conv_w_all = conv_all[0::2].reshape(N_CHIPS, CONV_K, CONV_DIM // N_CHIPS).transpose(1, 0, 2).reshape(CONV_K, CONV_DIM)

    return w_proj_t, out_w, conv_w_all


def _local_step(x, loss_target, norm_w, w_proj_t, conv_w_all, conv_b, dt_bias, a_log, d_skip, ssm_norm_w,
                out_w, final_norm_w, grad_exchange=None):
    b, s, d = x.shape
    t = b * s
    g4, hg = SSD_GROUPS, HEADS_PER_GROUP
    dtb_g = _lane_pad(dt_bias.reshape(g4, hg)).reshape(g4, 1, LANES)
    alog_g = _lane_pad(a_log.reshape(g4, hg)).reshape(g4, 1, LANES)
    dskip_x = jnp.repeat(d_skip, HEAD_DIM, axis=1)
    fnw = final_norm_w.reshape(1, d)

    x2 = x.reshape(t, d)
    h = _rms_fwd(x2, norm_w)
    big_tm = min(t, 2048)
    if isinstance(out_w, _Exchange):
        proj, *out_w = _matmul(h, w_proj_t, tb=True, tm=big_tm, tn=1280, tk=1024, name="proj", exchange=out_w)
    else:
        proj = _matmul(h, w_proj_t, tb=True, tm=big_tm, tn=1280, tk=1024, name="proj")
    w_ao, w_so, w_oo = (w.reshape(-1, d) for w in out_w)
    proj3 = proj.reshape(b, s, NP)
    o3, yp3 = _attn_fwd(proj3)
    xact = _conv_fwd(proj3, conv_w_all, conv_b)
    dtr = proj3[:, :, DT0:DT0 + g4 * hg].reshape(b, s, g4, hg).transpose(0, 2, 1, 3)
    dtr_g = jnp.pad(dtr, ((0, 0), (0, 0), (0, 0), (0, LANES - hg)))
    y3, yn3, hst = _ssd_fwd(xact, proj3, dtr_g, dtb_g, alog_g, dskip_x, ssm_norm_w)
    yp = yp3.reshape(t, D_MODEL)
    yn = yn3.reshape(t, SSD_WIDTH)
    ya = _matmul(yp, w_ao, tm=512, tn=1024, tk=1024, name="attn_out")
    ys = _matmul(yn, w_so, tm=512, tn=1024, tk=2048, name="ssm_out")
    merged = _merge_fwd(proj, ya, ys)
    mo = _matmul(merged, w_oo, tm=512, tn=1024, tk=1024, name="out_proj")
    dout, doutb, loss_part, d_fnw = _final_fwd_bwd(x2, mo, loss_target.reshape(t, d), fnw)

    dmerged = _matmul(doutb, w_oo, tb=True, tm=512, tn=1024, tk=1024, name="d_merged")
    g_wo = _matmul(merged, doutb, ta=True, tm=512, tn=1024, tk=1024, name="g_w_o")
    dya, dys, dproj = _merge_bwd(dmerged, proj, ya, ys)
    dyp = _matmul(dya, w_ao, tb=True, tm=512, tn=1024, tk=1024, name="d_attn_pre")
    g_wao = _matmul(yp, dya, ta=True, tm=512, tn=1024, tk=1024, name="g_w_attn_out")
    dyn = _matmul(dys, w_so, tb=True, tm=1024, tn=2048, tk=1024, name="d_ssm_norm")
    g_wso = _matmul(yn, dys, ta=True, tm=1024, tn=1024, tk=1024, name="g_w_ssm_out")
    dproj3 = _attn_bwd(proj3, dyp.reshape(b, s, D_MODEL), o3, dproj.reshape(b, s, NP))
    (dxs, dbm, dcm, dproj3, ddtr_g, d_snw_g, d_alog_g, d_dtb_g, d_dsk_g) = _ssd_bwd(
        dyn.reshape(b, s, SSD_WIDTH), y3, xact, proj3, hst, dtr_g, dtb_g, alog_g, dskip_x, ssm_norm_w, dproj3)
    dproj3, g_cw_xs, g_cb_xs = _conv_bwd(dxs, proj3, conv_w_all, conv_b, 0, "conv_bwd_x", dproj3)
    dproj3, g_cw_bm, g_cb_bm = _conv_bwd(dbm, proj3, conv_w_all, conv_b, SSD_WIDTH, "conv_bwd_b", dproj3)
    dproj3, g_cw_cm, g_cb_cm = _conv_bwd(dcm, proj3, conv_w_all, conv_b, SSD_WIDTH + g4 * SSD_STATE, "conv_bwd_c", dproj3)
    ddt = ddtr_g[:, :, :, :hg].transpose(0, 2, 1, 3).reshape(b, s, g4 * hg).astype(BF16)
    ddt = jnp.pad(ddt, ((0, 0), (0, 0), (0, DT_PAD - g4 * hg)))
    dproj = lax.dynamic_update_slice(dproj3, ddt, (0, 0, DT0)).reshape(t, NP)
    exchanged = []
    if grad_exchange:
        g_wproj, *got = _matmul(dproj, h, ta=True, tm=1280, tn=1024, tk=1024, name="g_w_in",
                                exchange=grad_exchange([g_wao, g_wso, g_wo], "out"))
        exchanged += got
        dh, *got = _matmul(dproj, w_proj_t, tm=big_tm, tn=1024, tk=1280, name="d_h", exchange=grad_exchange([g_wproj], "in"))
        exchanged += got
    else:
        g_wproj = _matmul(dproj, h, ta=True, tm=1280, tn=1024, tk=1024, name="g_w_in")
        dh = _matmul(dproj, w_proj_t, tm=big_tm, tn=1024, tk=1280, name="d_h")
    grad_x, d_nw = _rms_bwd(dh, x2, norm_w, dout)
    g_cw = jnp.concatenate([g_cw_xs, g_cw_bm, g_cw_cm], axis=1)
    g_cb = jnp.concatenate([g_cb_xs, g_cb_bm, g_cb_cm], axis=1)
    return (loss_part, grad_x, d_nw, g_wproj, g_cw, g_cb, d_dtb_g, d_alog_g, d_dsk_g, d_snw_g, g_wao, g_wso, g_wo, d_fnw,
            exchanged)


def kernel(x, norm_w, w_in, conv_w, conv_b, dt_bias, a_log, d_skip, ssm_norm_w, w_attn_out, w_ssm_out, w_o, final_norm_w, loss_target, m_norm_w, m_w_in, m_conv_w, m_conv_b, m_dt_bias, m_a_log, m_d_skip, m_ssm_norm_w, m_w_attn_out, m_w_ssm_out, m_w_o, m_final_norm_w, v_norm_w, v_w_in, v_conv_w, v_conv_b, v_dt_bias, v_a_log, v_d_skip, v_ssm_norm_w, v_w_attn_out, v_w_ssm_out, v_w_o, v_final_norm_w):
    b, s, d = x.shape
    core = lax.axis_index("c")
    g4, hg = SSD_GROUPS, HEADS_PER_GROUP
    shard_cols = w_in.shape[2]
    w_proj_t, out_w, conv_w_all = _gather_all(w_in, w_attn_out, w_ssm_out, w_o, conv_w)
    chip = 2 * lax.axis_index("x") + lax.axis_index("y")
    place = jnp.stack([core, chip]).astype(jnp.int32)
    chip_sums = []

    def grad_exchange(grads, which):
        if which == "in":
            slabs = _from_proj_layout(grads[0]).reshape(N_CHIPS, shard_cols, d)
        else:
            slabs = jnp.concatenate([g.reshape(N_CHIPS, -1, d) for g in grads], axis=1)
        from_sibling, = _swap_halves([slabs], "grad_swap_halves_" + which)
        chip_sums.append(_add_halves(slabs, from_sibling, place, "grad_add_halves_" + which))
        return _all_to_all_exchange(chip_sums[-1:])

    (loss_part, grad_x, d_nw, _, g_cw, g_cb, d_dtb_g, d_alog_g, d_dsk_g, d_snw_g, _, _, _, d_fnw, from_chips) = _local_step(
        x, loss_target, norm_w, w_proj_t, conv_w_all, conv_b, dt_bias, a_log, d_skip, ssm_norm_w, out_w, final_norm_w,
        grad_exchange)
    wholes = [_sum_chips(o, r, place, "grad_sum_chips_%d" % i) for i, (o, r) in enumerate(zip(chip_sums, from_chips))]
    g_out, g_w_in = _join_halves(wholes)

    small = jnp.concatenate([
        loss_part, d_nw, g_cb, _lane_pad(d_dtb_g[:, 0, :hg].reshape(1, -1)), _lane_pad(d_alog_g[:, 0, :hg].reshape(1, -1)),
        _lane_pad(d_dsk_g[:, 0, :hg].reshape(1, -1)),
        d_snw_g.reshape(1, -1), d_fnw, g_cw.reshape(1, -1)], axis=1)
    small_rows = small.shape[1] // LANES
    reduced = _gather_small(_pad_rows(small.reshape(small_rows, LANES), -(-small_rows // 8) * 8), True, "reduce_small")
    flat = reduced.reshape(-1)

    def take(start, n):
        return flat[start:start + n].reshape(1, n)

    loss = flat[0]
    pos = LANES
    g_norm_w = take(pos, d); pos += d
    g_conv_b = take(pos, CONV_DIM); pos += CONV_DIM
    g_dt_bias = take(pos, g4 * hg); pos += LANES
    g_a_log = take(pos, g4 * hg); pos += LANES
    g_d_skip = take(pos, g4 * hg); pos += LANES
    g_ssm_norm_w = take(pos, SSD_WIDTH); pos += SSD_WIDTH
    g_final_norm_w = take(pos, d); pos += d
    conv_cols = CONV_DIM // N_CHIPS
    g_conv_w = lax.dynamic_slice_in_dim(flat[pos:pos + CONV_K * CONV_DIM].reshape(CONV_K, CONV_DIM), chip * conv_cols, conv_cols, axis=1)

    rows_ao, rows_so = D_MODEL // N_CHIPS, SSD_WIDTH // N_CHIPS
    g_w_attn_out = g_out[:rows_ao]
    g_w_ssm_out = g_out[rows_ao:rows_ao + rows_so]
    g_w_o = g_out[rows_ao + rows_so:]

    names = ["norm_w", "w_in", "conv_w", "conv_b", "dt_bias", "a_log", "d_skip", "ssm_norm_w",
             "w_attn_out", "w_ssm_out", "w_o", "final_norm_w"]
    weights = [norm_w, w_in, conv_w, conv_b, dt_bias, a_log, d_skip, ssm_norm_w, w_attn_out, w_ssm_out, w_o, final_norm_w]
    grads = [g_norm_w, g_w_in, g_conv_w, g_conv_b, g_dt_bias, g_a_log, g_d_skip, g_ssm_norm_w,
             g_w_attn_out, g_w_ssm_out, g_w_o, g_final_norm_w]
    ms = [m_norm_w, m_w_in, m_conv_w, m_conv_b, m_dt_bias, m_a_log, m_d_skip, m_ssm_norm_w,
          m_w_attn_out, m_w_ssm_out, m_w_o, m_final_norm_w]
    vs = [v_norm_w, v_w_in, v_conv_w, v_conv_b, v_dt_bias, v_a_log, v_d_skip, v_ssm_norm_w,
          v_w_attn_out, v_w_ssm_out, v_w_o, v_final_norm_w]
    out_g, out_d, out_m, out_v = [], [], [], []
    for name, w, g, m, v in zip(names, weights, grads, ms, vs):
        if name == "w_in":
            to2, back = (lambda a: a[0].T), (lambda a: a.T.reshape(w.shape))
        else:
            to2, back = (lambda a: a.reshape(g.shape)), (lambda a: a.reshape(w.shape))
        dlt, nm, nv = _adamw(to2(w), g, to2(m), to2(v), "adamw_" + name)
        out_g.append(back(g))
        out_d.append(back(dlt))
        out_m.append(back(nm))
        out_v.append(back(nv))

    return (loss, grad_x.reshape(b, s, d), *out_g, *out_d, *out_m, *out_v)
```

```python
import jax
import jax.numpy as jnp
from jax import lax
from jax.experimental import pallas as pl
from jax.experimental.pallas import tpu as pltpu

F32 = jnp.float32
BF16 = jnp.bfloat16
MESH = pl.DeviceIdType.MESH

D_MODEL = 1024
SB_HEADS = 16
HEAD_DIM = 64
SSD_WIDTH = 2048
SSD_GROUPS = 4
GROUP_WIDTH = SSD_WIDTH // SSD_GROUPS
HEADS_PER_GROUP = 8
SSD_STATE = 128
CHUNK = 128
CONV_K = 4
CONV_DIM = 3072
D_PROJ = 11296
EPS = 1e-6
ADAM_LR, ADAM_B1, ADAM_B2, ADAM_EPS, ADAM_WD, ADAM_STEP = 0.001, 0.9, 0.999, 1e-08, 0.01, 10

LANES = 128
HP_WIDTH = 4 * LANES
ZS0, GATE0, XBC0, DT0 = 4096, 6144, 8192, 11264
DT_PAD = 256
NP = DT0 + DT_PAD
N_CHIPS = 4
VMEM_LIMIT = 56 * 1024 * 1024


N_HP = SB_HEADS // 2
PROJ_FIRST = 3840
W_ZS0, W_XBC0, W_DT0, W_GATE0 = 4096, 6144, 9216, 9248


def _to_proj_layout(wt):
    d = wt.shape[1]
    pairs = wt[:W_ZS0].reshape(4, N_HP, LANES, d).transpose(1, 0, 2, 3).reshape(W_ZS0, d)
    return jnp.concatenate([pairs, wt[W_ZS0:W_XBC0], wt[W_GATE0:], wt[W_XBC0:W_DT0], wt[W_DT0:W_GATE0],
                            jnp.zeros((NP - D_PROJ, d), wt.dtype)], axis=0)


def _from_proj_layout(gt):
    d = gt.shape[1]
    qkvz = gt[:ZS0].reshape(N_HP, 4, LANES, d).transpose(1, 0, 2, 3).reshape(ZS0, d)
    return jnp.concatenate([qkvz, gt[ZS0:GATE0], gt[XBC0:DT0], gt[DT0:DT0 + W_GATE0 - W_DT0], gt[GATE0:XBC0]], axis=0)


def _cparams(*sem):
    return pltpu.CompilerParams(dimension_semantics=sem or None, vmem_limit_bytes=VMEM_LIMIT)


def _sigmoid(z):
    return 1.0 / (1.0 + jnp.exp(-z))


def _dot(a, b, dims, precision=None):
    return lax.dot_general(a, b, (dims, ((), ())), preferred_element_type=F32, precision=precision)


NN = ((1,), (0,))
NT = ((1,), (1,))
TN = ((0,), (0,))


def _matmul(a, b, *, ta=False, tb=False, out_dtype=F32, tm, tn, tk, name, exchange=None, window=None):
    m, k = (a.shape[1], a.shape[0]) if ta else a.shape
    n = b.shape[0] if tb else b.shape[1]
    assert m % tm == 0 and n % tn == 0 and k % tk == 0, (name, m, n, k)
    grid = (m // tm, n // tn, k // tk)
    nk = grid[2]
    use_scratch = out_dtype != F32
    dims = ((0,) if ta else (1,), (1,) if tb else (0,))
    col0, n_total, into = window or (0, n, None)
    assert col0 % tn == 0
    j0 = col0 // tn
    pre = [into] if into is not None else []
    n_in = len(pre) + (len(exchange.inputs) if exchange else 0)
    n_out = len(exchange.out_shapes) if exchange else 0

    def kern(a_ref, b_ref, *rest):
        x_in, o_ref, x_out, scratch = rest[len(pre):n_in], rest[n_in], rest[n_in + 1:n_in + 1 + n_out], rest[n_in + 1 + n_out:]
        acc = scratch[0] if use_scratch else o_ref
        step = [pl.program_id(ax) for ax in range(3)]
        if exchange:
            sems = scratch[1:] if use_scratch else scratch

            @pl.when(jnp.logical_and(jnp.logical_and(step[0] == 0, step[1] == 0), step[2] == 0))
            def _():
                exchange.start(x_in, x_out, sems)

        @pl.when(step[2] == 0)
        def _():
            acc[...] = jnp.zeros_like(acc)

        acc[...] += _dot(a_ref[...], b_ref[...], dims)
        if use_scratch:
            @pl.when(step[2] == nk - 1)
            def _():
                o_ref[...] = acc[...].astype(out_dtype)
        if exchange:
            @pl.when(jnp.logical_and(jnp.logical_and(step[0] == grid[0] - 1, step[1] == grid[1] - 1), step[2] == nk - 1))
            def _():
                exchange.finish(x_in, x_out, sems)

    a_spec = pl.BlockSpec((tk, tm), lambda i, j, q: (q, i)) if ta else pl.BlockSpec((tm, tk), lambda i, j, q: (i, q))
    b_spec = pl.BlockSpec((tn, tk), lambda i, j, q: (j, q)) if tb else pl.BlockSpec((tk, tn), lambda i, j, q: (q, j))
    out = pl.pallas_call(
        kern, name=name,
        out_shape=[jax.ShapeDtypeStruct((m, n_total), out_dtype)] + (list(exchange.out_shapes) if exchange else []),
        grid=grid,
        in_specs=[a_spec, b_spec] + [ANY] * n_in,
        out_specs=[pl.BlockSpec((tm, tn), lambda i, j, q: (i, j0 + j))] + [ANY] * n_out,
        input_output_aliases={2: 0} if pre else {},
        scratch_shapes=([pltpu.VMEM((tm, tn), F32)] if use_scratch else []) + (list(exchange.sems) if exchange else []),
        compiler_params=_cparams("arbitrary", "arbitrary", "arbitrary") if exchange else _cparams("parallel", "parallel", "arbitrary"),
    )(a, b, *pre, *(exchange.inputs if exchange else []))
    return out if exchange else out[0]


ROWS = 256


def _rms_fwd(x2, w):
    t, d = x2.shape

    def kern(x_ref, w_ref, h_ref):
        x = x_ref[...]
        r = lax.rsqrt(jnp.mean(x * x, axis=-1, keepdims=True) + EPS)
        h_ref[...] = (x * r * w_ref[...]).astype(BF16)

    return pl.pallas_call(
        kern, name="rms_fwd",
        out_shape=jax.ShapeDtypeStruct((t, d), BF16),
        grid=(t // ROWS,),
        in_specs=[pl.BlockSpec((ROWS, d), lambda i: (i, 0)), pl.BlockSpec((1, d), lambda i: (0, 0))],
        out_specs=pl.BlockSpec((ROWS, d), lambda i: (i, 0)),
        compiler_params=_cparams("parallel"),
    )(x2, w)


def _rms_bwd(dh, x2, w, dout):
    t, d = x2.shape

    def kern(dh_ref, x_ref, w_ref, dout_ref, gx_ref, dw_ref):
        @pl.when(pl.program_id(0) == 0)
        def _():
            dw_ref[...] = jnp.zeros_like(dw_ref)

        x = x_ref[...]
        r = lax.rsqrt(jnp.mean(x * x, axis=-1, keepdims=True) + EPS)
        xh = x * r
        g = dh_ref[...]
        dw_ref[...] += jnp.sum(g * xh, axis=0, keepdims=True)
        gw = g * w_ref[...]
        gx_ref[...] = dout_ref[...] + r * (gw - xh * jnp.mean(gw * xh, axis=-1, keepdims=True))

    row = pl.BlockSpec((ROWS, d), lambda i: (i, 0))
    vec = pl.BlockSpec((1, d), lambda i: (0, 0))
    return pl.pallas_call(
        kern, name="rms_bwd",
        out_shape=(jax.ShapeDtypeStruct((t, d), F32), jax.ShapeDtypeStruct((1, d), F32)),
        grid=(t // ROWS,),
        in_specs=[row, row, vec, row],
        out_specs=(row, vec),
        compiler_params=_cparams("arbitrary"),
    )(dh, x2, w, dout)


def _final_fwd_bwd(x2, mo, target, w):
    t, d = x2.shape

    def kern(x_ref, mo_ref, t_ref, w_ref, dout_ref, doutb_ref, loss_ref, dw_ref):
        @pl.when(pl.program_id(0) == 0)
        def _():
            loss_ref[...] = jnp.zeros_like(loss_ref)
            dw_ref[...] = jnp.zeros_like(dw_ref)

        u = x_ref[...] + mo_ref[...]
        r = lax.rsqrt(jnp.mean(u * u, axis=-1, keepdims=True) + EPS)
        uh = u * r
        wv = w_ref[...]
        err = uh * wv - t_ref[...]
        loss_ref[...] += (0.5 / d) * jnp.sum(err * err)
        dy = err * (1.0 / d)
        dw_ref[...] += jnp.sum(dy * uh, axis=0, keepdims=True)
        gw = dy * wv
        du = r * (gw - uh * jnp.mean(gw * uh, axis=-1, keepdims=True))
        dout_ref[...] = du
        doutb_ref[...] = du.astype(BF16)

    row = pl.BlockSpec((ROWS, d), lambda i: (i, 0))
    vec = pl.BlockSpec((1, d), lambda i: (0, 0))
    return pl.pallas_call(
        kern, name="final_fwd_bwd",
        out_shape=(jax.ShapeDtypeStruct((t, d), F32), jax.ShapeDtypeStruct((t, d), BF16),
                   jax.ShapeDtypeStruct((1, LANES), F32), jax.ShapeDtypeStruct((1, d), F32)),
        grid=(t // ROWS,),
        in_specs=[row, row, row, vec],
        out_specs=(row, row, pl.BlockSpec((1, LANES), lambda i: (0, 0)), vec),
        compiler_params=_cparams("arbitrary"),
    )(x2, mo, target, w)


def _merge_fwd(proj2, ya, ys):
    t = ya.shape[0]
    gblk = GATE0 // D_MODEL

    def kern(ga_ref, gs_ref, ya_ref, ys_ref, o_ref):
        o_ref[...] = (_sigmoid(ga_ref[...]) * ya_ref[...] + _sigmoid(gs_ref[...]) * ys_ref[...]).astype(BF16)

    row = pl.BlockSpec((ROWS, D_MODEL), lambda i: (i, 0))
    return pl.pallas_call(
        kern, name="merge_fwd",
        out_shape=jax.ShapeDtypeStruct((t, D_MODEL), BF16),
        grid=(t // ROWS,),
        in_specs=[pl.BlockSpec((ROWS, D_MODEL), lambda i: (i, gblk)),
                  pl.BlockSpec((ROWS, D_MODEL), lambda i: (i, gblk + 1)), row, row],
        out_specs=row,
        compiler_params=_cparams("parallel"),
    )(proj2, proj2, ya, ys)


def _merge_bwd(dm, proj2, ya, ys):
    t = ya.shape[0]
    gblk = GATE0 // D_MODEL

    def kern(dm_ref, ga_ref, gs_ref, ya_ref, ys_ref, dya_ref, dys_ref, dg_ref):
        g = dm_ref[...]
        sa = _sigmoid(ga_ref[...])
        ss = _sigmoid(gs_ref[...])
        dya_ref[...] = (g * sa).astype(BF16)
        dys_ref[...] = (g * ss).astype(BF16)
        dg_ref[:, :D_MODEL] = (g * ya_ref[...] * sa * (1.0 - sa)).astype(BF16)
        dg_ref[:, D_MODEL:] = (g * ys_ref[...] * ss * (1.0 - ss)).astype(BF16)

    row = pl.BlockSpec((ROWS, D_MODEL), lambda i: (i, 0))
    return pl.pallas_call(
        kern, name="merge_bwd",
        out_shape=(jax.ShapeDtypeStruct((t, D_MODEL), BF16), jax.ShapeDtypeStruct((t, D_MODEL), BF16),
                   jax.ShapeDtypeStruct((t, NP), BF16)),
        grid=(t // ROWS,),
        in_specs=[row, pl.BlockSpec((ROWS, D_MODEL), lambda i: (i, gblk)),
                  pl.BlockSpec((ROWS, D_MODEL), lambda i: (i, gblk + 1)), row, row],
        out_specs=(row, row, pl.BlockSpec((ROWS, 2 * D_MODEL), lambda i: (i, GATE0 // (2 * D_MODEL)))),
        compiler_params=_cparams("parallel"),
    )(dm, proj2, proj2, ya, ys)


TQ = 256
TK = 256
assert TQ == TK
HEAD_LANES = (slice(0, HEAD_DIM), slice(HEAD_DIM, 2 * HEAD_DIM))


def _tri(pred):
    r = lax.broadcasted_iota(jnp.int32, (TK, TK), 0)
    c = lax.broadcasted_iota(jnp.int32, (TK, TK), 1)
    return pred(r, c).astype(BF16)


def _split_bf16(v):
    hi = v.astype(BF16)
    lo = (v - hi.astype(F32)).astype(BF16)
    return hi, lo


def _tri_dot(v, tri):
    hi, lo = _split_bf16(v)
    return _dot(hi, tri, NN) + _dot(lo, tri, NN)


def _sb_logs(z, mask):
    l1p = jnp.log(1.0 + jnp.exp(-jnp.abs(z)))
    lb = jnp.minimum(z, 0.0) - l1p
    lom = -jnp.maximum(z, 0.0) - l1p
    if mask is not None:
        lom = jnp.where(mask, lom, 0.0)
    return lb, lom


def _sb_weights(lb, later, carry_r, mask):
    a = jnp.exp(lb + (later + carry_r))
    if mask is not None:
        a = jnp.where(mask, a, 0.0)
    return a


DEAD = -104.0


def _while_alive(n, carry, step):
    def alive(cr):
        return jnp.max(jnp.maximum(cr[0][0], cr[1][0])) > DEAD

    def cond(state):
        jj, go, _ = state
        return jnp.logical_and(jj < n, go)

    def body(state):
        jj, _, cr = state
        cr = step(jj, cr)
        return jj + 1, alive(cr), cr

    return lax.while_loop(cond, body, (jnp.int32(0), alive(carry), carry))[2]


Q_LANES, K_LANES, V_LANES, ZA_LANES = (slice(i * LANES, (i + 1) * LANES) for i in range(4))


def _split_heads(dst, src, scale=None):
    for h, lanes in enumerate(HEAD_LANES):
        v = src[:, lanes]
        dst[h] = (v if scale is None else v * scale).astype(BF16)


def _attn_fwd(proj3):
    b, s, _ = proj3.shape
    nq = s // TQ
    scale = HEAD_DIM ** -0.5

    def kern(x_ref, o_ref, yp_ref, qs, ks, vs):
        _split_heads(qs, x_ref[0, :, Q_LANES], scale)
        _split_heads(ks, x_ref[0, :, K_LANES])
        _split_heads(vs, x_ref[0, :, V_LANES])
        za_ref = x_ref.at[:, :, ZA_LANES]
        row = lax.broadcasted_iota(jnp.int32, (TQ, TK), 0)
        col = lax.broadcasted_iota(jnp.int32, (TQ, TK), 1)
        tri_gt = _tri(lambda j, sk: j > sk)

        def q_block(i, _):
            top = isinstance(i, int)
            r0 = i * TQ if top else pl.multiple_of(i * TQ, TQ)
            qh = [qs[h, pl.ds(r0, TQ), :] for h in range(2)]

            def k_blocks(blocks, carry):
                nb = range(len(blocks))
                kh = [[ks[h, pl.ds(c0, TK), :] for h in range(2)] for c0, _ in blocks]
                vh = [[vs[h, pl.ds(c0, TK), :] for h in range(2)] for c0, _ in blocks]
                z = [[_dot(qh[h], kh[bl][h], NT) for h in range(2)] for bl in nb]
                logs = [[_sb_logs(z[bl][h], blocks[bl][1]) for h in range(2)] for bl in nb]
                later = [[_tri_dot(logs[bl][h][1], tri_gt) for h in range(2)] for bl in nb]
                out = []
                for h in range(2):
                    carry_r, acc = carry[h]
                    for bl in nb:
                        lb, lom = logs[bl][h]
                        a = _sb_weights(lb, later[bl][h], carry_r, blocks[bl][1])
                        acc = acc + _dot(a.astype(BF16), vh[bl][h], NN)
                        carry_r = carry_r + (later[bl][h][:, 0:1] + lom[:, 0:1])
                    out.append((carry_r, acc))
                return tuple(out)

            start = (jnp.zeros((TQ, 1), F32), jnp.zeros((TQ, HEAD_DIM), F32))
            diag = (r0, col < row)
            if top:
                carry = k_blocks([diag], (start, start))
            else:
                carry = k_blocks([diag, (pl.multiple_of(r0 - TK, TK), None)], (start, start))
                carry = _while_alive(i - 1, carry, lambda jj, cr: k_blocks([(pl.multiple_of((i - 2 - jj) * TK, TK), None)], cr))
            for (_, acc), lanes in zip(carry, HEAD_LANES):
                o_ref[0, pl.ds(r0, TQ), lanes] = acc
                za = za_ref[0, pl.ds(r0, TQ), lanes]
                yp_ref[0, pl.ds(r0, TQ), lanes] = (acc * (za * _sigmoid(za))).astype(BF16)
            return 0

        q_block(0, 0)
        lax.fori_loop(1, nq, q_block, 0)

    out_spec = pl.BlockSpec((1, s, LANES), lambda bi, hp: (bi, 0, hp))
    return pl.pallas_call(
        kern, name="attn_fwd",
        out_shape=(jax.ShapeDtypeStruct((b, s, D_MODEL), F32), jax.ShapeDtypeStruct((b, s, D_MODEL), BF16)),
        grid=(b, SB_HEADS // 2),
        in_specs=[pl.BlockSpec((1, s, HP_WIDTH), lambda bi, hp: (bi, 0, hp))],
        out_specs=(out_spec, out_spec),
        scratch_shapes=[pltpu.VMEM((2, s, HEAD_DIM), BF16)] * 3,
        compiler_params=_cparams("parallel", "parallel"),
    )(proj3)


def _attn_bwd(proj3, dyp3, o3, dproj3):
    b, s, _ = proj3.shape
    nq = s // TQ
    scale = HEAD_DIM ** -0.5

    def kern(x_ref, dyp_ref, o_ref, _, d_ref, qs, ks, vs, dos, dk_acc, dv_acc):
        _split_heads(qs, x_ref[0, :, Q_LANES], scale)
        _split_heads(ks, x_ref[0, :, K_LANES])
        _split_heads(vs, x_ref[0, :, V_LANES])
        dq_ref, dk_ref, dv_ref = (d_ref.at[:, :, lanes] for lanes in (Q_LANES, K_LANES, V_LANES))
        za = x_ref[0, :, ZA_LANES]
        sg = _sigmoid(za)
        dyp = dyp_ref[0]
        _split_heads(dos, dyp * (za * sg))
        d_ref[0, :, ZA_LANES] = (dyp * o_ref[0] * (sg * (1.0 + za * (1.0 - sg)))).astype(BF16)
        dk_acc[...] = jnp.zeros_like(dk_acc)
        dv_acc[...] = jnp.zeros_like(dv_acc)
        row = lax.broadcasted_iota(jnp.int32, (TQ, TK), 0)
        col = lax.broadcasted_iota(jnp.int32, (TQ, TK), 1)
        tri_gt = _tri(lambda j, sk: j > sk)
        tri_ge = _tri(lambda j, sk: j >= sk)

        def q_block(i, _):
            top = isinstance(i, int)
            r0 = i * TQ if top else pl.multiple_of(i * TQ, TQ)
            qh = [qs[h, pl.ds(r0, TQ), :] for h in range(2)]
            doh = [dos[h, pl.ds(r0, TQ), :] for h in range(2)]
            totals = [jnp.sum(doh[h].astype(F32) * o_ref[0, pl.ds(r0, TQ), lanes], axis=1, keepdims=True)
                      for h, lanes in enumerate(HEAD_LANES)]

            def k_blocks(blocks, carry):
                nb = range(len(blocks))
                kh = [[ks[h, pl.ds(c0, TK), :] for h in range(2)] for c0, _ in blocks]
                vh = [[vs[h, pl.ds(c0, TK), :] for h in range(2)] for c0, _ in blocks]
                z = [[_dot(qh[h], kh[bl][h], NT) for h in range(2)] for bl in nb]
                da = [[_dot(doh[h], vh[bl][h], NT) for h in range(2)] for bl in nb]
                logs = [[_sb_logs(z[bl][h], blocks[bl][1]) for h in range(2)] for bl in nb]
                later = [[_tri_dot(logs[bl][h][1], tri_gt) for h in range(2)] for bl in nb]
                ab, g, suffix = ([[None, None] for _ in nb] for _ in range(3))
                for h in range(2):
                    cr = carry[h][0]
                    for bl in nb:
                        a = _sb_weights(logs[bl][h][0], later[bl][h], cr, blocks[bl][1])
                        ab[bl][h] = a.astype(BF16)
                        g[bl][h] = da[bl][h] * ab[bl][h].astype(F32)
                        suffix[bl][h] = _tri_dot(g[bl][h], tri_ge)
                        cr = cr + (later[bl][h][:, 0:1] + logs[bl][h][1][:, 0:1])
                out = []
                for h in range(2):
                    _, carry_g, dq = carry[h]
                    cr = carry[h][0]
                    for bl in nb:
                        c0, mask = blocks[bl]
                        lb, lom = logs[bl][h]
                        dz = g[bl][h] - (g[bl][h] + (totals[h] - carry_g) - suffix[bl][h]) * jnp.exp(lb)
                        if mask is not None:
                            dz = jnp.where(mask, dz, 0.0)
                        dzb = dz.astype(BF16)
                        dk_acc[h, pl.ds(c0, TK), :] += _dot(dzb, qh[h], TN)
                        dv_acc[h, pl.ds(c0, TK), :] += _dot(ab[bl][h], doh[h], TN)
                        dq = dq + _dot(dzb, kh[bl][h], NN)
                        carry_g = carry_g + suffix[bl][h][:, 0:1]
                        cr = cr + (later[bl][h][:, 0:1] + lom[:, 0:1])
                    out.append((cr, carry_g, dq))
                return tuple(out)

            def k_block(c0, carry, mask):
                kh = [ks[h, pl.ds(c0, TK), :] for h in range(2)]
                vh = [vs[h, pl.ds(c0, TK), :] for h in range(2)]
                z = [_dot(qh[h], kh[h], NT) for h in range(2)]
                da = [_dot(doh[h], vh[h], NT) for h in range(2)]
                logs, later = [], []
                for h in range(2):
                    logs.append(_sb_logs(z[h], mask))
                    later.append(_tri_dot(logs[h][1], tri_gt))
                ab, g, suffix = [], [], []
                for h in range(2):
                    a = _sb_weights(logs[h][0], later[h], carry[h][0], mask)
                    ab.append(a.astype(BF16))
                    g.append(da[h] * ab[h].astype(F32))
                    suffix.append(_tri_dot(g[h], tri_ge))
                out = []
                for h in range(2):
                    carry_r, carry_g, dq = carry[h]
                    lb, lom = logs[h]
                    dz = g[h] - (g[h] + (totals[h] - carry_g) - suffix[h]) * jnp.exp(lb)
                    if mask is not None:
                        dz = jnp.where(mask, dz, 0.0)
                    dzb = dz.astype(BF16)
                    dk_acc[h, pl.ds(c0, TK), :] += _dot(dzb, qh[h], TN)
                    dv_acc[h, pl.ds(c0, TK), :] += _dot(ab[h], doh[h], TN)
                    out.append((carry_r + (later[h][:, 0:1] + lom[:, 0:1]), carry_g + suffix[h][:, 0:1],
                                dq + _dot(dzb, kh[h], NN)))
                return tuple(out)

            zero = jnp.zeros((TQ, 1), F32)
            start = (zero, zero, jnp.zeros((TQ, HEAD_DIM), F32))
            diag = (r0, col < row)
            if top:
                carry = k_block(r0, (start, start), col < row)
            else:
                carry = k_blocks([diag, (pl.multiple_of(r0 - TK, TK), None)], (start, start))
                carry = _while_alive(i - 1, carry, lambda jj, cr: k_block(pl.multiple_of((i - 2 - jj) * TK, TK), cr, None))
            for (_, _, dq), lanes in zip(carry, HEAD_LANES):
                dq_ref[0, pl.ds(r0, TQ), lanes] = (dq * scale).astype(BF16)
            return 0

        q_block(0, 0)
        lax.fori_loop(1, nq, q_block, 0)

        for h, lanes in enumerate(HEAD_LANES):
            dk_ref[0, :, lanes] = dk_acc[h].astype(BF16)
            dv_ref[0, :, lanes] = dv_acc[h].astype(BF16)

    plain = pl.BlockSpec((1, s, LANES), lambda bi, hp: (bi, 0, hp))
    pair = pl.BlockSpec((1, s, HP_WIDTH), lambda bi, hp: (bi, 0, hp))
    return pl.pallas_call(
        kern, name="attn_bwd",
        out_shape=jax.ShapeDtypeStruct(dproj3.shape, dproj3.dtype),
        grid=(b, SB_HEADS // 2),
        in_specs=[pair, plain, plain, ANY],
        out_specs=pair,
        input_output_aliases={3: 0},
        scratch_shapes=[pltpu.VMEM((2, s, HEAD_DIM), BF16)] * 4 + [pltpu.VMEM((2, s, HEAD_DIM), F32)] * 2,
        compiler_params=_cparams("parallel", "parallel"),
    )(proj3, dyp3, o3, dproj3)


CONV_COLS = 256
HALO = 8


def _conv_pre(xp, w_ref, b_ref, r0):
    pre = b_ref[...] + w_ref[CONV_K - 1:CONV_K, :] * xp[pl.ds(HALO + r0, CHUNK), :]
    for kk in range(1, CONV_K):
        pre = pre + w_ref[CONV_K - 1 - kk:CONV_K - kk, :] * xp[pl.ds(HALO + r0 - kk, CHUNK), :]
    return pre


def _conv_fwd(proj3, conv_w, conv_b):
    b, s, _ = proj3.shape
    nc = s // CHUNK

    def kern(x_ref, w_ref, b_ref, o_ref, xp):
        xp[0:HALO, :] = jnp.zeros((HALO, CONV_COLS), F32)
        xp[HALO:, :] = x_ref[0]
        for ci in range(nc):
            pre = _conv_pre(xp, w_ref, b_ref, ci * CHUNK)
            o_ref[0, ci * CHUNK:(ci + 1) * CHUNK, :] = pre * _sigmoid(pre)

    return pl.pallas_call(
        kern, name="conv_fwd",
        out_shape=jax.ShapeDtypeStruct((b, s, CONV_DIM), F32),
        grid=(CONV_DIM // CONV_COLS, b),
        in_specs=[pl.BlockSpec((1, s, CONV_COLS), lambda j, bi: (bi, 0, XBC0 // CONV_COLS + j)),
                  pl.BlockSpec((CONV_K, CONV_COLS), lambda j, bi: (0, j)),
                  pl.BlockSpec((1, CONV_COLS), lambda j, bi: (0, j))],
        out_specs=pl.BlockSpec((1, s, CONV_COLS), lambda j, bi: (bi, 0, j)),
        scratch_shapes=[pltpu.VMEM((s + HALO, CONV_COLS), F32)],
        compiler_params=_cparams("parallel", "parallel"),
    )(proj3, conv_w, conv_b)


def _conv_bwd(dact, proj3, conv_w, conv_b, col0, name, dproj3):
    b, s, width = dact.shape
    nc = s // CHUNK
    j0 = col0 // CONV_COLS

    def kern(da_ref, x_ref, w_ref, b_ref, _, dx_ref, dw_ref, db_ref, xp, dp):
        @pl.when(pl.program_id(1) == 0)
        def _():
            dw_ref[...] = jnp.zeros_like(dw_ref)
            db_ref[...] = jnp.zeros_like(db_ref)

        xp[0:HALO, :] = jnp.zeros((HALO, CONV_COLS), F32)
        xp[HALO:, :] = x_ref[0]
        dp[s:, :] = jnp.zeros((HALO, CONV_COLS), F32)
        for ci in range(nc):
            r0 = ci * CHUNK
            pre = _conv_pre(xp, w_ref, b_ref, r0)
            sg = _sigmoid(pre)
            dpre = da_ref[0, r0:r0 + CHUNK, :] * (sg * (1.0 + pre * (1.0 - sg)))
            dp[r0:r0 + CHUNK, :] = dpre
            db_ref[...] += jnp.sum(dpre, axis=0, keepdims=True)
            for kk in range(CONV_K):
                tap = CONV_K - 1 - kk
                dw_ref[tap:tap + 1, :] += jnp.sum(dpre * xp[pl.ds(HALO + r0 - kk, CHUNK), :], axis=0, keepdims=True)
        for ci in range(nc):
            r0 = ci * CHUNK
            dx = w_ref[CONV_K - 1:CONV_K, :] * dp[pl.ds(r0, CHUNK), :]
            for kk in range(1, CONV_K):
                dx = dx + w_ref[CONV_K - 1 - kk:CONV_K - kk, :] * dp[pl.ds(r0 + kk, CHUNK), :]
            dx_ref[0, r0:r0 + CHUNK, :] = dx.astype(BF16)

    return pl.pallas_call(
        kern, name=name,
        out_shape=(jax.ShapeDtypeStruct(dproj3.shape, dproj3.dtype), jax.ShapeDtypeStruct((CONV_K, width), F32),
                   jax.ShapeDtypeStruct((1, width), F32)),
        grid=(width // CONV_COLS, b),
        in_specs=[pl.BlockSpec((1, s, CONV_COLS), lambda j, bi: (bi, 0, j)),
                  pl.BlockSpec((1, s, CONV_COLS), lambda j, bi: (bi, 0, XBC0 // CONV_COLS + j0 + j)),
                  pl.BlockSpec((CONV_K, CONV_COLS), lambda j, bi: (0, j0 + j)),
                  pl.BlockSpec((1, CONV_COLS), lambda j, bi: (0, j0 + j)), ANY],
        out_specs=(pl.BlockSpec((1, s, CONV_COLS), lambda j, bi: (bi, 0, XBC0 // CONV_COLS + j0 + j)),
                   pl.BlockSpec((CONV_K, CONV_COLS), lambda j, bi: (0, j)),
                   pl.BlockSpec((1, CONV_COLS), lambda j, bi: (0, j))),
        input_output_aliases={4: 0},
        scratch_shapes=[pltpu.VMEM((s + HALO, CONV_COLS), F32)] * 2,
        compiler_params=_cparams("parallel", "arbitrary"),
    )(dact, proj3, conv_w, conv_b, dproj3)


def _sel_dot(v, sel, left=False):
    hi = v.astype(BF16)
    rest = v - hi.astype(F32)
    mid = rest.astype(BF16)
    lo = (rest - mid.astype(F32)).astype(BF16)
    if left:
        return _dot(sel, hi, NN) + _dot(sel, mid, NN) + _dot(sel, lo, NN)
    return _dot(hi, sel, NN) + _dot(mid, sel, NN) + _dot(lo, sel, NN)


def _ssd_common(dtr_ref, dtb_ref, alog_ref):
    lane = lax.broadcasted_iota(jnp.int32, (CHUNK, LANES), 1)
    row = lax.broadcasted_iota(jnp.int32, (CHUNK, LANES), 0)
    head_lane = lane < HEADS_PER_GROUP
    pre = dtr_ref[0, 0] + dtb_ref[0]
    dt = jnp.where(head_lane, jnp.maximum(pre, 0.0) + jnp.log(1.0 + jnp.exp(-jnp.abs(pre))), 0.0)
    a = jnp.where(head_lane[0:1], -jnp.exp(alog_ref[0]), 0.0)
    tril = (row >= lane).astype(BF16)
    acs = _sel_dot(dt * a, tril, left=True)
    acs_t = acs.T
    er = lax.broadcasted_iota(jnp.int32, (LANES, GROUP_WIDTH), 0)
    ec = lax.broadcasted_iota(jnp.int32, (LANES, GROUP_WIDTH), 1)
    expand = ((ec // HEAD_DIM) == er).astype(BF16)
    tr = lax.broadcasted_iota(jnp.int32, (GROUP_WIDTH, LANES), 0)
    tc = lax.broadcasted_iota(jnp.int32, (GROUP_WIDTH, LANES), 1)
    reduce = ((tr // HEAD_DIM) == tc).astype(BF16)
    dt_x = _sel_dot(dt, expand)
    acs_x = _sel_dot(acs, expand)
    end_x = acs_x[CHUNK - 1:CHUNK, :]
    causal = row >= lane
    return dict(dt=dt, a=a, pre=pre, head_lane=head_lane, acs=acs, acs_t=acs_t, expand=expand, reduce=reduce,
                dt_x=dt_x, acs_x=acs_x, end_x=end_x, causal=causal, row=row, lane=lane)


def _ssd_decay(cm, h):
    seg = cm["acs"][:, h:h + 1] - cm["acs_t"][h:h + 1, :]
    return jnp.where(cm["causal"], jnp.exp(jnp.minimum(seg, 0.0)), 0.0)


def _ssd_fwd(xact, proj3, dtr_g, dtb_g, alog_g, dskip_x, snw):
    b, s, _ = xact.shape
    nc = s // CHUNK
    g4 = SSD_GROUPS

    def kern(xs_ref, bm_ref, cm_ref, zs_ref, dtr_ref, dtb_ref, alog_ref, dsk_ref, snw_ref,
             y_ref, yn_ref, hst_ref, h_sc):
        @pl.when(pl.program_id(2) == 0)
        def _():
            h_sc[...] = jnp.zeros_like(h_sc)

        cm = _ssd_common(dtr_ref, dtb_ref, alog_ref)
        x = xs_ref[0]
        bmb = bm_ref[0].astype(BF16)
        cmb = cm_ref[0].astype(BF16)
        h_in = h_sc[...]
        hst_ref[0, 0, 0] = h_in
        xdt = x * cm["dt_x"]
        xdtb = xdt.astype(BF16)
        cb = _dot(cmb, bmb, NT)
        y_off = _dot(cmb, h_in.astype(BF16), NN) * jnp.exp(cm["acs_x"])
        for h in range(HEADS_PER_GROUP):
            lanes = slice(h * HEAD_DIM, (h + 1) * HEAD_DIM)
            m = (cb * _ssd_decay(cm, h)).astype(BF16)
            y_ref[0, :, lanes] = _dot(m, xdtb[:, lanes], NN)
        y = y_ref[0] + y_off + x * dsk_ref[...]
        y_ref[0] = y
        w = (xdt * jnp.exp(cm["end_x"] - cm["acs_x"])).astype(BF16)
        h_sc[...] = h_in * jnp.exp(cm["end_x"]) + _dot(bmb, w, TN)
        zs = zs_ref[0]
        y2 = y * (zs * _sigmoid(zs))
        yn_ref[0] = (y2 * lax.rsqrt(jnp.mean(y2 * y2, axis=-1, keepdims=True) + EPS) * snw_ref[...]).astype(BF16)

    gw = GROUP_WIDTH
    small = pl.BlockSpec((1, 1, LANES), lambda gi, bi, ci: (gi, 0, 0))
    xblk = pl.BlockSpec((1, CHUNK, gw), lambda gi, bi, ci: (bi, ci, gi))
    return pl.pallas_call(
        kern, name="ssd_fwd",
        out_shape=(jax.ShapeDtypeStruct((b, s, SSD_WIDTH), F32), jax.ShapeDtypeStruct((b, s, SSD_WIDTH), BF16),
                   jax.ShapeDtypeStruct((b, nc, g4, SSD_STATE, gw), F32)),
        grid=(g4, b, nc),
        in_specs=[xblk,
                  pl.BlockSpec((1, CHUNK, LANES), lambda gi, bi, ci: (bi, ci, SSD_WIDTH // LANES + gi)),
                  pl.BlockSpec((1, CHUNK, LANES), lambda gi, bi, ci: (bi, ci, SSD_WIDTH // LANES + g4 + gi)),
                  pl.BlockSpec((1, CHUNK, gw), lambda gi, bi, ci: (bi, ci, ZS0 // gw + gi)),
                  pl.BlockSpec((1, 1, CHUNK, LANES), lambda gi, bi, ci: (bi, gi, ci, 0)),
                  small, small,
                  pl.BlockSpec((1, gw), lambda gi, bi, ci: (0, gi)),
                  pl.BlockSpec((1, gw), lambda gi, bi, ci: (0, gi))],
        out_specs=(xblk, xblk, pl.BlockSpec((1, 1, 1, SSD_STATE, gw), lambda gi, bi, ci: (bi, ci, gi, 0, 0))),
        scratch_shapes=[pltpu.VMEM((SSD_STATE, gw), F32)],
        compiler_params=_cparams("parallel", "parallel", "arbitrary"),
    )(xact, xact, xact, proj3, dtr_g, dtb_g, alog_g, dskip_x, snw)


def _ssd_bwd(dyn3, y3, xact, proj3, hst, dtr_g, dtb_g, alog_g, dskip_x, snw, dproj3):
    b, s, _ = xact.shape
    nc = s // CHUNK
    g4 = SSD_GROUPS
    gw = GROUP_WIDTH

    def kern(dyn_ref, y_ref, xs_ref, bm_ref, cm_ref, zs_ref, hst_ref, dtr_ref, dtb_ref, alog_ref, dsk_ref, snw_ref, _,
             dxs_ref, dbm_ref, dcm_ref, dzs_ref, ddtr_ref, dsnw_ref, dalog_ref, ddtb_ref, ddsk_ref, dh_sc):
        first = jnp.logical_and(pl.program_id(1) == 0, pl.program_id(2) == 0)

        @pl.when(first)
        def _():
            dsnw_ref[...] = jnp.zeros_like(dsnw_ref)
            dalog_ref[...] = jnp.zeros_like(dalog_ref)
            ddtb_ref[...] = jnp.zeros_like(ddtb_ref)
            ddsk_ref[...] = jnp.zeros_like(ddsk_ref)

        @pl.when(pl.program_id(2) == 0)
        def _():
            dh_sc[...] = jnp.zeros_like(dh_sc)

        cm = _ssd_common(dtr_ref, dtb_ref, alog_ref)
        row, lane = cm["row"], cm["lane"]
        y = y_ref[0]
        zs = zs_ref[0]
        sg = _sigmoid(zs)
        silu = zs * sg
        y2 = y * silu
        rstd = lax.rsqrt(jnp.mean(y2 * y2, axis=-1, keepdims=True) + EPS)
        y2h = y2 * rstd
        dyn = dyn_ref[0]
        dsnw_ref[0] += jnp.sum(dyn * y2h, axis=0, keepdims=True)
        gwv = dyn * snw_ref[...]
        dy2 = rstd * (gwv - y2h * jnp.mean(gwv * y2h, axis=-1, keepdims=True))
        dzs_ref[0] = (dy2 * y * (sg * (1.0 + zs * (1.0 - sg)))).astype(BF16)
        dy = dy2 * silu
        dyb = dy.astype(BF16)

        x = xs_ref[0]
        bmb = bm_ref[0].astype(BF16)
        cmb = cm_ref[0].astype(BF16)
        h_in = hst_ref[0, 0, 0]
        h_inb = h_in.astype(BF16)
        d_hn = dh_sc[...]
        d_hnb = d_hn.astype(BF16)
        xdt = x * cm["dt_x"]
        xdtb = xdt.astype(BF16)
        eacs = jnp.exp(cm["acs_x"])
        dte = jnp.exp(cm["end_x"] - cm["acs_x"])
        wb = (xdt * dte).astype(BF16)

        dsk_lanes = jnp.broadcast_to(jnp.sum(dy * x, axis=0, keepdims=True), (8, gw))
        ddsk_ref[0] += _sel_dot(dsk_lanes, cm["reduce"])[0:1, :]
        dyo = dy * eacs
        dyob = dyo.astype(BF16)
        dacs_x = dyo * _dot(cmb, h_inb, NN)
        dcm = _dot(dyob, h_inb, NT)
        dh_in = _dot(cmb, dyob, TN)
        dw = _dot(bmb, d_hnb, NN)
        dbm = _dot(wb, d_hnb, NT)
        dxdt = dw * dte
        e_l = dw * xdt * dte
        dacs_x = dacs_x - e_l
        dend_x = jnp.sum(e_l, axis=0, keepdims=True)
        chunk_decay = jnp.exp(cm["end_x"])
        dh_sc[...] = d_hn * chunk_decay + dh_in
        dend_x = dend_x + jnp.sum(d_hn * h_in, axis=0, keepdims=True) * chunk_decay
        last_row = lax.broadcasted_iota(jnp.int32, (CHUNK, gw), 0) == CHUNK - 1
        dacs_x = dacs_x + jnp.where(last_row, dend_x, 0.0)

        cb = _dot(cmb, bmb, NT)
        dcb = jnp.zeros((CHUNK, CHUNK), F32)
        dacs = jnp.zeros((CHUNK, LANES), F32)
        dacs_t = jnp.zeros((LANES, CHUNK), F32)
        for h in range(HEADS_PER_GROUP):
            lanes = slice(h * HEAD_DIM, (h + 1) * HEAD_DIM)
            decay = _ssd_decay(cm, h)
            m = cb * decay
            dm = _dot(dyb[:, lanes], xdtb[:, lanes], NT)
            dxs_ref[0, :, lanes] = _dot(m.astype(BF16), dyb[:, lanes], TN)
            dcb_h = dm * decay
            dcb = dcb + dcb_h
            n = dcb_h * cb
            dacs = dacs + jnp.where(lane == h, jnp.sum(n, axis=1, keepdims=True), 0.0)
            dacs_t = dacs_t + jnp.where(row == h, jnp.sum(n, axis=0, keepdims=True), 0.0)
        dcbb = dcb.astype(BF16)
        dcm_ref[0] = dcm + _dot(dcbb, bmb, NN)
        dbm_ref[0] = dbm + _dot(dcbb, cmb, TN)
        dxdt = dxdt + dxs_ref[0]
        dxs_ref[0] = dy * dsk_ref[...] + dxdt * cm["dt_x"]

        dacs = dacs - dacs_t.T + _sel_dot(dacs_x, cm["reduce"])
        ddt = _sel_dot(dxdt * x, cm["reduce"])
        triu = (row <= lane).astype(BF16)
        rc = _sel_dot(dacs, triu, left=True)
        ddt = ddt + cm["a"] * rc
        dalog_ref[0] += jnp.sum(cm["dt"] * rc, axis=0, keepdims=True) * cm["a"]
        ddtr = jnp.where(cm["head_lane"], ddt * _sigmoid(cm["pre"]), 0.0)
        ddtr_ref[0, 0] = ddtr
        ddtb_ref[0] += jnp.sum(ddtr, axis=0, keepdims=True)

    def rev(ci):
        return nc - 1 - ci

    small = pl.BlockSpec((1, 1, LANES), lambda gi, bi, ci: (gi, 0, 0))
    xblk = pl.BlockSpec((1, CHUNK, gw), lambda gi, bi, ci: (bi, rev(ci), gi))
    nblk = pl.BlockSpec((1, CHUNK, LANES), lambda gi, bi, ci: (bi, rev(ci), gi))
    gvec = pl.BlockSpec((1, gw), lambda gi, bi, ci: (0, gi))
    gacc = pl.BlockSpec((1, 1, gw), lambda gi, bi, ci: (gi, 0, 0))
    return pl.pallas_call(
        kern, name="ssd_bwd",
        out_shape=(jax.ShapeDtypeStruct((b, s, SSD_WIDTH), F32),
                   jax.ShapeDtypeStruct((b, s, g4 * SSD_STATE), F32),
                   jax.ShapeDtypeStruct((b, s, g4 * SSD_STATE), F32),
                   jax.ShapeDtypeStruct(dproj3.shape, dproj3.dtype),
                   jax.ShapeDtypeStruct((b, g4, s, LANES), F32),
                   jax.ShapeDtypeStruct((g4, 1, gw), F32),
                   jax.ShapeDtypeStruct((g4, 1, LANES), F32),
                   jax.ShapeDtypeStruct((g4, 1, LANES), F32),
                   jax.ShapeDtypeStruct((g4, 1, LANES), F32)),
        grid=(g4, b, nc),
        in_specs=[xblk, xblk, xblk,
                  pl.BlockSpec((1, CHUNK, LANES), lambda gi, bi, ci: (bi, rev(ci), SSD_WIDTH // LANES + gi)),
                  pl.BlockSpec((1, CHUNK, LANES), lambda gi, bi, ci: (bi, rev(ci), SSD_WIDTH // LANES + g4 + gi)),
                  pl.BlockSpec((1, CHUNK, gw), lambda gi, bi, ci: (bi, rev(ci), ZS0 // gw + gi)),
                  pl.BlockSpec((1, 1, 1, SSD_STATE, gw), lambda gi, bi, ci: (bi, rev(ci), gi, 0, 0)),
                  pl.BlockSpec((1, 1, CHUNK, LANES), lambda gi, bi, ci: (bi, gi, rev(ci), 0)),
                  small, small, gvec, gvec, ANY],
        out_specs=(xblk, nblk, nblk,
                   pl.BlockSpec((1, CHUNK, gw), lambda gi, bi, ci: (bi, rev(ci), ZS0 // gw + gi)),
                   pl.BlockSpec((1, 1, CHUNK, LANES), lambda gi, bi, ci: (bi, gi, rev(ci), 0)),
                   gacc, small, small, small),
        input_output_aliases={12: 3},
        scratch_shapes=[pltpu.VMEM((SSD_STATE, gw), F32)],
        compiler_params=_cparams("parallel", "arbitrary", "arbitrary"),
    )(dyn3, y3, xact, xact, xact, proj3, hst, dtr_g, dtb_g, alog_g, dskip_x, snw, dproj3)


def _adamw(w, g, m, v, name):
    r, c = w.shape
    tr = 128 if r % 128 == 0 else r
    tc = LANES if (tr == r and r > 128 and c % LANES == 0) else c

    def kern(w_ref, g_ref, m_ref, v_ref, d_ref, nm_ref, nv_ref):
        gv = g_ref[...]
        nm = ADAM_B1 * m_ref[...] + (1.0 - ADAM_B1) * gv
        nv = ADAM_B2 * v_ref[...] + (1.0 - ADAM_B2) * (gv * gv)
        m_hat = nm / (1.0 - ADAM_B1 ** ADAM_STEP)
        v_hat = nv / (1.0 - ADAM_B2 ** ADAM_STEP)
        d_ref[...] = -ADAM_LR * (m_hat / (jnp.sqrt(v_hat) + ADAM_EPS) + ADAM_WD * w_ref[...])
        nm_ref[...] = nm
        nv_ref[...] = nv

    blk = pl.BlockSpec((tr, tc), lambda i, j: (i, j))
    out = jax.ShapeDtypeStruct((r, c), F32)
    return pl.pallas_call(
        kern, name=name, out_shape=(out, out, out), grid=(r // tr, c // tc),
        in_specs=[blk] * 4, out_specs=(blk, blk, blk),
        compiler_params=_cparams("parallel", "parallel"),
    )(w, g, m, v)


ANY = pl.BlockSpec(memory_space=pl.ANY)


def _position():
    return lax.axis_index("x"), lax.axis_index("y"), lax.axis_index("c")


def _other_chips(x, y):
    return [(1 - x, y), (x, 1 - y), (1 - x, 1 - y)]


def _dma_sems(n):
    return [pltpu.SemaphoreType.DMA((n,)), pltpu.SemaphoreType.DMA((n,))]


class _Exchange:
    def __init__(self, inputs, out_shapes, sems, start, finish):
        self.inputs, self.out_shapes, self.sems, self.start, self.finish = inputs, out_shapes, sems, start, finish


def _run_exchange(ex, name):
    n_in, n_out = len(ex.inputs), len(ex.out_shapes)

    def body(*refs):
        x_in, x_out, sems = refs[:n_in], refs[n_in:n_in + n_out], refs[n_in + n_out:]
        ex.start(x_in, x_out, sems)
        ex.finish(x_in, x_out, sems)

    return pl.pallas_call(
        body, name=name, out_shape=list(ex.out_shapes),
        in_specs=[ANY] * n_in, out_specs=[ANY] * n_out, scratch_shapes=list(ex.sems),
    )(*ex.inputs)


def _gather_exchange(shards, sources=(0, 1, 2, 3)):
    n = len(shards)

    def copies(p_refs, out_refs, sems):
        send_sems, recv_sems = sems
        x, y, c = _position()
        me = 2 * x + y
        chips = _other_chips(x, y)

        def slab(a, chip, hf):
            half = shards[a].shape[1] // 2
            return out_refs[a].at[chip, :, pl.ds(hf * half, half)]

        def my_half(a):
            half = shards[a].shape[1] // 2
            return p_refs[a].at[:, pl.ds(c * half, half)]

        def over_ici(a, j, chip_from):
            px, py = chips[j]
            return pltpu.make_async_remote_copy(
                src_ref=my_half(a), dst_ref=slab(a, chip_from, c),
                send_sem=send_sems.at[3 * a + j], recv_sem=recv_sems.at[3 * a + j],
                device_id=(px, py, c), device_id_type=MESH)

        def to_sibling(a, j, hf):
            px, py = chips[j]
            return pltpu.make_async_remote_copy(
                src_ref=slab(a, 2 * px + py, hf), dst_ref=slab(a, 2 * px + py, hf),
                send_sem=send_sems.at[3 * (n + a) + j], recv_sem=recv_sems.at[3 * (n + a) + j],
                device_id=(x, y, 1 - c), device_id_type=MESH)

        own = [pltpu.make_async_remote_copy(
            src_ref=p_refs[a], dst_ref=out_refs[a].at[me], send_sem=send_sems.at[6 * n + a], recv_sem=recv_sems.at[6 * n + a],
            device_id=(x, y, 1 - c), device_id_type=MESH) for a in range(n)]
        first = [over_ici(a, j, me) for a in range(n) for j in range(3)]
        return me, chips, c, over_ici, to_sibling, first, own

    def when_source(chip, fn):
        if len(sources) == N_CHIPS:
            fn()
        else:
            pl.when(jnp.logical_and(chip >= min(sources), chip <= max(sources)))(fn)

    def start(p_refs, out_refs, sems):
        me, _, _, _, _, first, own = copies(p_refs, out_refs, sems)

        def send():
            for cp in first + own:
                cp.start()

        when_source(me, send)

    def finish(p_refs, out_refs, sems):
        me, chips, c, over_ici, to_sibling, first, own = copies(p_refs, out_refs, sems)

        def forward(a, j, chip):
            def fn():
                over_ici(a, j, chip).wait_recv()
                to_sibling(a, j, c).start()
            return fn

        for a in range(n):
            for j, (px, py) in enumerate(chips):
                when_source(2 * px + py, forward(a, j, 2 * px + py))
        for a in range(n):
            for j, (px, py) in enumerate(chips):
                when_source(2 * px + py, to_sibling(a, j, 1 - c).wait_recv)

        def sent():
            for cp in first:
                cp.wait_send()
            for cp in own:
                cp.wait()

        when_source(me, sent)
        for a in range(n):
            for j, (px, py) in enumerate(chips):
                when_source(2 * px + py, to_sibling(a, j, c).wait_send)

    return _Exchange(list(shards), [jax.ShapeDtypeStruct((N_CHIPS, *v.shape), v.dtype) for v in shards],
                     _dma_sems(7 * n), start, finish)


def _swap_halves(parts, name):
    n = len(parts)

    def body(*refs):
        v_refs, out_refs = refs[:n], refs[n:2 * n]
        send_sems, recv_sems = refs[2 * n:]
        x, y, c = _position()
        copies = []
        for a in range(n):
            half = parts[a].shape[2] // 2
            copies.append(pltpu.make_async_remote_copy(
                src_ref=v_refs[a].at[:, :, pl.ds((1 - c) * half, half)], dst_ref=out_refs[a],
                send_sem=send_sems.at[a], recv_sem=recv_sems.at[a], device_id=(x, y, 1 - c), device_id_type=MESH))
        for cp in copies:
            cp.start()
        for cp in copies:
            cp.wait()

    return pl.pallas_call(
        body, name=name,
        out_shape=[jax.ShapeDtypeStruct((v.shape[0], v.shape[1], v.shape[2] // 2), v.dtype) for v in parts],
        in_specs=[ANY] * n, out_specs=[ANY] * n,
        scratch_shapes=_dma_sems(n),
    )(*parts)


def _all_to_all_exchange(parts):
    n = len(parts)

    def sends(p_refs, out_refs, sems):
        send_sems, recv_sems = sems
        x, y, c = _position()
        return [pltpu.make_async_remote_copy(
            src_ref=p_refs[a].at[2 * px + py], dst_ref=out_refs[a].at[j],
            send_sem=send_sems.at[3 * a + j], recv_sem=recv_sems.at[3 * a + j],
            device_id=(px, py, c), device_id_type=MESH) for a in range(n) for j, (px, py) in enumerate(_other_chips(x, y))]

    def start(p_refs, out_refs, sems):
        for cp in sends(p_refs, out_refs, sems):
            cp.start()

    def finish(p_refs, out_refs, sems):
        for cp in sends(p_refs, out_refs, sems):
            cp.wait()

    return _Exchange(list(parts), [jax.ShapeDtypeStruct((N_CHIPS - 1, *v.shape[1:]), v.dtype) for v in parts],
                     _dma_sems(3 * n), start, finish)


def _join_halves(wholes):
    n = len(wholes)

    def body(*refs):
        out_refs = refs[n:2 * n]
        send_sems, recv_sems = refs[2 * n:]
        x, y, c = _position()
        copies = []
        for a in range(n):
            half = wholes[a].shape[1] // 2
            mine = out_refs[a].at[:, pl.ds(c * half, half)]
            copies.append(pltpu.make_async_remote_copy(
                src_ref=mine, dst_ref=mine, send_sem=send_sems.at[a], recv_sem=recv_sems.at[a],
                device_id=(x, y, 1 - c), device_id_type=MESH))
        for cp in copies:
            cp.start()
        for cp in copies:
            cp.wait()

    return pl.pallas_call(
        body, name="grad_join_halves",
        out_shape=[jax.ShapeDtypeStruct(v.shape, v.dtype) for v in wholes],
        in_specs=[ANY] * n, out_specs=[ANY] * n,
        input_output_aliases={a: a for a in range(n)},
        scratch_shapes=_dma_sems(n),
    )(*wholes)


STRIP = 256


def _add_halves(g, sw, place, name):
    n, rows, cols = g.shape
    nb = cols // 2 // STRIP

    def kern(p_ref, g_ref, s_ref, o_ref):
        o_ref[...] = (g_ref[...] + s_ref[...]).astype(BF16)

    blk = pl.BlockSpec((1, rows, STRIP), lambda j, i, p_ref: (j, 0, i))
    return pl.pallas_call(
        kern, name=name,
        out_shape=jax.ShapeDtypeStruct((n, rows, cols // 2), BF16),
        grid_spec=pltpu.PrefetchScalarGridSpec(
            num_scalar_prefetch=1, grid=(n, nb),
            in_specs=[pl.BlockSpec((1, rows, STRIP), lambda j, i, p_ref: (j, 0, p_ref[0] * nb + i)), blk],
            out_specs=blk),
        compiler_params=_cparams("parallel", "parallel"),
    )(place, g, sw)


def _sum_chips(own, rx, place, name):
    _, rows, half = rx.shape
    nb = half // STRIP

    def kern(p_ref, own_ref, r_ref, o_ref):
        total = own_ref[0].astype(F32)
        for j in range(N_CHIPS - 1):
            total = total + r_ref[j].astype(F32)
        o_ref[...] = total

    return pl.pallas_call(
        kern, name=name,
        out_shape=jax.ShapeDtypeStruct((rows, 2 * half), F32),
        grid_spec=pltpu.PrefetchScalarGridSpec(
            num_scalar_prefetch=1, grid=(nb,),
            in_specs=[pl.BlockSpec((1, rows, STRIP), lambda i, p_ref: (p_ref[1], 0, i)),
                      pl.BlockSpec((N_CHIPS - 1, rows, STRIP), lambda i, p_ref: (0, 0, i))],
            out_specs=pl.BlockSpec((rows, STRIP), lambda i, p_ref: (0, p_ref[0] * nb + i))),
        compiler_params=_cparams("parallel"),
    )(place, own, rx)


def _gather_small(v, reduce, name):
    rows = v.shape[0]

    def body(v_ref, out_ref, buf, send_sems, recv_sems):
        x, y, c = _position()
        me = 4 * x + 2 * y + c
        buf[me] = v_ref[...]
        peers = [(x ^ (k >> 2), y ^ ((k >> 1) & 1), c ^ (k & 1)) for k in range(1, 8)]
        copies = [pltpu.make_async_remote_copy(
            src_ref=v_ref, dst_ref=buf.at[me],
            send_sem=send_sems.at[k], recv_sem=recv_sems.at[k],
            device_id=peer, device_id_type=MESH) for k, peer in enumerate(peers)]
        for cp in copies:
            cp.start()
        for k, (px, py, pc) in enumerate(peers):
            pltpu.make_async_remote_copy(
                src_ref=v_ref, dst_ref=buf.at[4 * px + 2 * py + pc],
                send_sem=send_sems.at[k], recv_sem=recv_sems.at[k],
                device_id=(px, py, pc), device_id_type=MESH).wait_recv()
        for cp in copies:
            cp.wait_send()
        if reduce:
            total = buf[0]
            for d in range(1, 8):
                total = total + buf[d]
            out_ref[...] = total
        else:
            out_ref[...] = buf[...]

    vm = pl.BlockSpec(memory_space=pltpu.VMEM)
    return pl.pallas_call(
        body, name=name,
        out_shape=jax.ShapeDtypeStruct((rows, LANES) if reduce else (8, rows, LANES), F32),
        in_specs=[vm], out_specs=vm,
        scratch_shapes=[pltpu.VMEM((8, rows, LANES), F32), pltpu.SemaphoreType.DMA((7,)), pltpu.SemaphoreType.DMA((7,))],
    )(v)


def _pad_rows(a, rows):
    return jnp.pad(a, ((0, rows - a.shape[0]), (0, 0)))


def _lane_pad(v):
    n = v.shape[1]
    return jnp.pad(v, ((0, 0), (0, -n % LANES)))


def _gather_all(w_in, w_attn_out, w_ssm_out, w_o, conv_w):
    d = D_MODEL
    own_t = w_in[0].T.astype(BF16)
    rows = own_t.shape[0]
    first, = _run_exchange(_gather_exchange([own_t], sources=(0, 1)), "gather_w_in_first")
    qkvz = jnp.concatenate([first[0], first[1][:W_ZS0 - rows]], axis=0)
    pairs = qkvz.reshape(4, N_HP, LANES, d).transpose(1, 0, 2, 3).reshape(W_ZS0, d)
    rest = _gather_exchange([own_t], sources=(2, 3))

    def later_rows(got):
        c1, c2, c3 = first[1], got[2], got[3]
        zs_end, xbc_end = W_XBC0 - 2 * rows, W_DT0 - 3 * rows
        return jnp.concatenate([pairs[PROJ_FIRST:], c1[W_ZS0 - rows:], c2[:zs_end], c3[W_GATE0 - 3 * rows:], c2[zs_end:], c3[:xbc_end],
                                c3[xbc_end:W_GATE0 - 3 * rows], jnp.zeros((NP - D_PROJ, d), BF16)], axis=0)

    w_proj_t = (pairs[:PROJ_FIRST], rest, later_rows)
    out_w = _gather_exchange([a[0].astype(BF16) for a in (w_attn_out, w_ssm_out, w_o)])
    conv_rows = conv_w[0].size // LANES
    conv_all = _gather_small(conv_w[0].reshape(conv_rows, LANES), False, "gather_conv_w")
    conv_w_all = conv_all[0::2].reshape(N_CHIPS, CONV_K, CONV_DIM // N_CHIPS).transpose(1, 0, 2).reshape(CONV_K, CONV_DIM)

    return w_proj_t, out_w, conv_w_all


def _local_step(x, loss_target, norm_w, w_proj_t, conv_w_all, conv_b, dt_bias, a_log, d_skip, ssm_norm_w,
                out_w, final_norm_w, grad_exchange=None):
    b, s, d = x.shape
    t = b * s
    g4, hg = SSD_GROUPS, HEADS_PER_GROUP
    dtb_g = _lane_pad(dt_bias.reshape(g4, hg)).reshape(g4, 1, LANES)
    alog_g = _lane_pad(a_log.reshape(g4, hg)).reshape(g4, 1, LANES)
    dskip_x = jnp.repeat(d_skip, HEAD_DIM, axis=1)
    fnw = final_norm_w.reshape(1, d)

    x2 = x.reshape(t, d)
    h = _rms_fwd(x2, norm_w)
    big_tm = min(t, 2048)
    if isinstance(out_w, _Exchange):
        w_first, rest, later_rows = w_proj_t
        proj, got = _matmul(h, w_first, tb=True, tm=big_tm, tn=1280, tk=1024, name="proj_first", exchange=rest,
                            window=(0, NP, None))
        w_later = later_rows(got)
        proj, *out_w = _matmul(h, w_later, tb=True, tm=big_tm, tn=1280, tk=1024, name="proj", exchange=out_w,
                               window=(PROJ_FIRST, NP, proj))
        w_proj_t = jnp.concatenate([w_first, w_later], axis=0)
    else:
        proj = _matmul(h, w_proj_t, tb=True, tm=big_tm, tn=1280, tk=1024, name="proj")
    w_ao, w_so, w_oo = (w.reshape(-1, d) for w in out_w)
    proj3 = proj.reshape(b, s, NP)
    o3, yp3 = _attn_fwd(proj3)
    xact = _conv_fwd(proj3, conv_w_all, conv_b)
    dtr = proj3[:, :, DT0:DT0 + g4 * hg].reshape(b, s, g4, hg).transpose(0, 2, 1, 3)
    dtr_g = jnp.pad(dtr, ((0, 0), (0, 0), (0, 0), (0, LANES - hg)))
    y3, yn3, hst = _ssd_fwd(xact, proj3, dtr_g, dtb_g, alog_g, dskip_x, ssm_norm_w)
    yp = yp3.reshape(t, D_MODEL)
    yn = yn3.reshape(t, SSD_WIDTH)
    ya = _matmul(yp, w_ao, tm=512, tn=1024, tk=1024, name="attn_out")
    ys = _matmul(yn, w_so, tm=512, tn=1024, tk=2048, name="ssm_out")
    merged = _merge_fwd(proj, ya, ys)
    mo = _matmul(merged, w_oo, tm=512, tn=1024, tk=1024, name="out_proj")
    dout, doutb, loss_part, d_fnw = _final_fwd_bwd(x2, mo, loss_target.reshape(t, d), fnw)

    dmerged = _matmul(doutb, w_oo, tb=True, tm=512, tn=1024, tk=1024, name="d_merged")
    g_wo = _matmul(merged, doutb, ta=True, tm=512, tn=1024, tk=1024, name="g_w_o")
    dya, dys, dproj = _merge_bwd(dmerged, proj, ya, ys)
    dyp = _matmul(dya, w_ao, tb=True, tm=512, tn=1024, tk=1024, name="d_attn_pre")
    g_wao = _matmul(yp, dya, ta=True, tm=512, tn=1024, tk=1024, name="g_w_attn_out")
    dyn = _matmul(dys, w_so, tb=True, tm=1024, tn=2048, tk=1024, name="d_ssm_norm")
    g_wso = _matmul(yn, dys, ta=True, tm=1024, tn=1024, tk=1024, name="g_w_ssm_out")
    dproj3 = _attn_bwd(proj3, dyp.reshape(b, s, D_MODEL), o3, dproj.reshape(b, s, NP))
    (dxs, dbm, dcm, dproj3, ddtr_g, d_snw_g, d_alog_g, d_dtb_g, d_dsk_g) = _ssd_bwd(
        dyn.reshape(b, s, SSD_WIDTH), y3, xact, proj3, hst, dtr_g, dtb_g, alog_g, dskip_x, ssm_norm_w, dproj3)
    dproj3, g_cw_xs, g_cb_xs = _conv_bwd(dxs, proj3, conv_w_all, conv_b, 0, "conv_bwd_x", dproj3)
    dproj3, g_cw_bm, g_cb_bm = _conv_bwd(dbm, proj3, conv_w_all, conv_b, SSD_WIDTH, "conv_bwd_b", dproj3)
    dproj3, g_cw_cm, g_cb_cm = _conv_bwd(dcm, proj3, conv_w_all, conv_b, SSD_WIDTH + g4 * SSD_STATE, "conv_bwd_c", dproj3)
    ddt = ddtr_g[:, :, :, :hg].transpose(0, 2, 1, 3).reshape(b, s, g4 * hg).astype(BF16)
    ddt = jnp.pad(ddt, ((0, 0), (0, 0), (0, DT_PAD - g4 * hg)))
    dproj = lax.dynamic_update_slice(dproj3, ddt, (0, 0, DT0)).reshape(t, NP)
    exchanged = []
    if grad_exchange:
        g_wproj, *got = _matmul(dproj, h, ta=True, tm=1280, tn=1024, tk=1024, name="g_w_in",
                                exchange=grad_exchange([g_wao, g_wso, g_wo], "out"))
        exchanged += got
        dh, *got = _matmul(dproj, w_proj_t, tm=big_tm, tn=1024, tk=1280, name="d_h", exchange=grad_exchange([g_wproj], "in"))
        exchanged += got
    else:
        g_wproj = _matmul(dproj, h, ta=True, tm=1280, tn=1024, tk=1024, name="g_w_in")
        dh = _matmul(dproj, w_proj_t, tm=big_tm, tn=1024, tk=1280, name="d_h")
    grad_x, d_nw = _rms_bwd(dh, x2, norm_w, dout)
    g_cw = jnp.concatenate([g_cw_xs, g_cw_bm, g_cw_cm], axis=1)
    g_cb = jnp.concatenate([g_cb_xs, g_cb_bm, g_cb_cm], axis=1)
    return (loss_part, grad_x, d_nw, g_wproj, g_cw, g_cb, d_dtb_g, d_alog_g, d_dsk_g, d_snw_g, g_wao, g_wso, g_wo, d_fnw,
            exchanged)


def kernel(x, norm_w, w_in, conv_w, conv_b, dt_bias, a_log, d_skip, ssm_norm_w, w_attn_out, w_ssm_out, w_o, final_norm_w, loss_target, m_norm_w, m_w_in, m_conv_w, m_conv_b, m_dt_bias, m_a_log, m_d_skip, m_ssm_norm_w, m_w_attn_out, m_w_ssm_out, m_w_o, m_final_norm_w, v_norm_w, v_w_in, v_conv_w, v_conv_b, v_dt_bias, v_a_log, v_d_skip, v_ssm_norm_w, v_w_attn_out, v_w_ssm_out, v_w_o, v_final_norm_w):
    b, s, d = x.shape
    core = lax.axis_index("c")
    g4, hg = SSD_GROUPS, HEADS_PER_GROUP
    shard_cols = w_in.shape[2]
    w_proj_t, out_w, conv_w_all = _gather_all(w_in, w_attn_out, w_ssm_out, w_o, conv_w)
    chip = 2 * lax.axis_index("x") + lax.axis_index("y")
    place = jnp.stack([core, chip]).astype(jnp.int32)
    chip_sums = []

    def grad_exchange(grads, which):
        if which == "in":
            slabs = _from_proj_layout(grads[0]).reshape(N_CHIPS, shard_cols, d)
        else:
            slabs = jnp.concatenate([g.reshape(N_CHIPS, -1, d) for g in grads], axis=1)
        from_sibling, = _swap_halves([slabs], "grad_swap_halves_" + which)
        chip_sums.append(_add_halves(slabs, from_sibling, place, "grad_add_halves_" + which))
        return _all_to_all_exchange(chip_sums[-1:])

    (loss_part, grad_x, d_nw, _, g_cw, g_cb, d_dtb_g, d_alog_g, d_dsk_g, d_snw_g, _, _, _, d_fnw, from_chips) = _local_step(
        x, loss_target, norm_w, w_proj_t, conv_w_all, conv_b, dt_bias, a_log, d_skip, ssm_norm_w, out_w, final_norm_w,
        grad_exchange)
    wholes = [_sum_chips(o, r, place, "grad_sum_chips_%d" % i) for i, (o, r) in enumerate(zip(chip_sums, from_chips))]
    g_out, g_w_in = _join_halves(wholes)

    small = jnp.concatenate([
        loss_part, d_nw, g_cb, _lane_pad(d_dtb_g[:, 0, :hg].reshape(1, -1)), _lane_pad(d_alog_g[:, 0, :hg].reshape(1, -1)),
        _lane_pad(d_dsk_g[:, 0, :hg].reshape(1, -1)),
        d_snw_g.reshape(1, -1), d_fnw, g_cw.reshape(1, -1)], axis=1)
    small_rows = small.shape[1] // LANES
    reduced = _gather_small(_pad_rows(small.reshape(small_rows, LANES), -(-small_rows // 8) * 8), True, "reduce_small")
    flat = reduced.reshape(-1)

    def take(start, n):
        return flat[start:start + n].reshape(1, n)

    loss = flat[0]
    pos = LANES
    g_norm_w = take(pos, d); pos += d
    g_conv_b = take(pos, CONV_DIM); pos += CONV_DIM
    g_dt_bias = take(pos, g4 * hg); pos += LANES
    g_a_log = take(pos, g4 * hg); pos += LANES
    g_d_skip = take(pos, g4 * hg); pos += LANES
    g_ssm_norm_w = take(pos, SSD_WIDTH); pos += SSD_WIDTH
    g_final_norm_w = take(pos, d); pos += d
    conv_cols = CONV_DIM // N_CHIPS
    g_conv_w = lax.dynamic_slice_in_dim(flat[pos:pos + CONV_K * CONV_DIM].reshape(CONV_K, CONV_DIM), chip * conv_cols, conv_cols, axis=1)

    rows_ao, rows_so = D_MODEL // N_CHIPS, SSD_WIDTH // N_CHIPS
    g_w_attn_out = g_out[:rows_ao]
    g_w_ssm_out = g_out[rows_ao:rows_ao + rows_so]
    g_w_o = g_out[rows_ao + rows_so:]

    names = ["norm_w", "w_in", "conv_w", "conv_b", "dt_bias", "a_log", "d_skip", "ssm_norm_w",
             "w_attn_out", "w_ssm_out", "w_o", "final_norm_w"]
    weights = [norm_w, w_in, conv_w, conv_b, dt_bias, a_log, d_skip, ssm_norm_w, w_attn_out, w_ssm_out, w_o, final_norm_w]
    grads = [g_norm_w, g_w_in, g_conv_w, g_conv_b, g_dt_bias, g_a_log, g_d_skip, g_ssm_norm_w,
             g_w_attn_out, g_w_ssm_out, g_w_o, g_final_norm_w]
    ms = [m_norm_w, m_w_in, m_conv_w, m_conv_b, m_dt_bias, m_a_log, m_d_skip, m_ssm_norm_w,
          m_w_attn_out, m_w_ssm_out, m_w_o, m_final_norm_w]
    vs = [v_norm_w, v_w_in, v_conv_w, v_conv_b, v_dt_bias, v_a_log, v_d_skip, v_ssm_norm_w,
          v_w_attn_out, v_w_ssm_out, v_w_o, v_final_norm_w]
    out_g, out_d, out_m, out_v = [], [], [], []
    for name, w, g, m, v in zip(names, weights, grads, ms, vs):
        if name == "w_in":
            to2, back = (lambda a: a[0].T), (lambda a: a.T.reshape(w.shape))
        else:
            to2, back = (lambda a: a.reshape(g.shape)), (lambda a: a.reshape(w.shape))
        dlt, nm, nv = _adamw(to2(w), g, to2(m), to2(v), "adamw_" + name)
        out_g.append(back(g))
        out_d.append(back(dlt))
        out_m.append(back(nm))
        out_v.append(back(nv))

    return (loss, grad_x.reshape(b, s, d), *out_g, *out_d, *out_m, *out_v)
```

```python
import jax
import jax.numpy as jnp
from jax import lax
from jax.experimental import pallas as pl
from jax.experimental.pallas import tpu as pltpu

F32 = jnp.float32
BF16 = jnp.bfloat16
MESH = pl.DeviceIdType.MESH

D_MODEL = 1024
SB_HEADS = 16
HEAD_DIM = 64
SSD_WIDTH = 2048
SSD_GROUPS = 4
GROUP_WIDTH = SSD_WIDTH // SSD_GROUPS
HEADS_PER_GROUP = 8
SSD_STATE = 128
CHUNK = 128
CONV_K = 4
CONV_DIM = 3072
D_PROJ = 11296
EPS = 1e-6
ADAM_LR, ADAM_B1, ADAM_B2, ADAM_EPS, ADAM_WD, ADAM_STEP = 0.001, 0.9, 0.999, 1e-08, 0.01, 10

LANES = 128
HP_WIDTH = 4 * LANES
ZS0, GATE0, XBC0, DT0 = 4096, 6144, 8192, 11264
DT_PAD = 256
NP = DT0 + DT_PAD
N_CHIPS = 4
VMEM_LIMIT = 56 * 1024 * 1024


N_HP = SB_HEADS // 2
W_ZS0, W_XBC0, W_DT0, W_GATE0 = 4096, 6144, 9216, 9248


def _to_proj_layout(wt):
    d = wt.shape[1]
    pairs = wt[:W_ZS0].reshape(4, N_HP, LANES, d).transpose(1, 0, 2, 3).reshape(W_ZS0, d)
    return jnp.concatenate([pairs, wt[W_ZS0:W_XBC0], wt[W_GATE0:], wt[W_XBC0:W_DT0], wt[W_DT0:W_GATE0],
                            jnp.zeros((NP - D_PROJ, d), wt.dtype)], axis=0)


def _from_proj_layout(gt):
    d = gt.shape[1]
    qkvz = gt[:ZS0].reshape(N_HP, 4, LANES, d).transpose(1, 0, 2, 3).reshape(ZS0, d)
    return jnp.concatenate([qkvz, gt[ZS0:GATE0], gt[XBC0:DT0], gt[DT0:DT0 + W_GATE0 - W_DT0], gt[GATE0:XBC0]], axis=0)


def _cparams(*sem):
    return pltpu.CompilerParams(dimension_semantics=sem or None, vmem_limit_bytes=VMEM_LIMIT)


def _sigmoid(z):
    return 1.0 / (1.0 + jnp.exp(-z))


def _dot(a, b, dims, precision=None):
    return lax.dot_general(a, b, (dims, ((), ())), preferred_element_type=F32, precision=precision)


NN = ((1,), (0,))
NT = ((1,), (1,))
TN = ((0,), (0,))


def _matmul(a, b, *, ta=False, tb=False, out_dtype=F32, tm, tn, tk, name, exchange=None):
    m, k = (a.shape[1], a.shape[0]) if ta else a.shape
    n = b.shape[0] if tb else b.shape[1]
    assert m % tm == 0 and n % tn == 0 and k % tk == 0, (name, m, n, k)
    grid = (m // tm, n // tn, k // tk)
    nk = grid[2]
    use_scratch = out_dtype != F32
    dims = ((0,) if ta else (1,), (1,) if tb else (0,))
    n_in = len(exchange.inputs) if exchange else 0
    n_out = len(exchange.out_shapes) if exchange else 0

    def kern(a_ref, b_ref, *rest):
        x_in, o_ref, x_out, scratch = rest[:n_in], rest[n_in], rest[n_in + 1:n_in + 1 + n_out], rest[n_in + 1 + n_out:]
        acc = scratch[0] if use_scratch else o_ref
        step = [pl.program_id(ax) for ax in range(3)]
        if exchange:
            sems = scratch[1:] if use_scratch else scratch

            @pl.when(jnp.logical_and(jnp.logical_and(step[0] == 0, step[1] == 0), step[2] == 0))
            def _():
                exchange.start(x_in, x_out, sems)

        @pl.when(step[2] == 0)
        def _():
            acc[...] = jnp.zeros_like(acc)

        acc[...] += _dot(a_ref[...], b_ref[...], dims)
        if use_scratch:
            @pl.when(step[2] == nk - 1)
            def _():
                o_ref[...] = acc[...].astype(out_dtype)
        if exchange:
            @pl.when(jnp.logical_and(jnp.logical_and(step[0] == grid[0] - 1, step[1] == grid[1] - 1), step[2] == nk - 1))
            def _():
                exchange.finish(x_in, x_out, sems)

    a_spec = pl.BlockSpec((tk, tm), lambda i, j, q: (q, i)) if ta else pl.BlockSpec((tm, tk), lambda i, j, q: (i, q))
    b_spec = pl.BlockSpec((tn, tk), lambda i, j, q: (j, q)) if tb else pl.BlockSpec((tk, tn), lambda i, j, q: (q, j))
    out = pl.pallas_call(
        kern, name=name,
        out_shape=[jax.ShapeDtypeStruct((m, n), out_dtype)] + (list(exchange.out_shapes) if exchange else []),
        grid=grid,
        in_specs=[a_spec, b_spec] + [ANY] * n_in,
        out_specs=[pl.BlockSpec((tm, tn), lambda i, j, q: (i, j))] + [ANY] * n_out,
        scratch_shapes=([pltpu.VMEM((tm, tn), F32)] if use_scratch else []) + (list(exchange.sems) if exchange else []),
        compiler_params=_cparams("arbitrary", "arbitrary", "arbitrary") if exchange else _cparams("parallel", "parallel", "arbitrary"),
    )(a, b, *(exchange.inputs if exchange else []))
    return out if exchange else out[0]


ROWS = 512


def _rms_fwd(x2, w):
    t, d = x2.shape

    def kern(x_ref, w_ref, h_ref):
        x = x_ref[...]
        r = lax.rsqrt(jnp.mean(x * x, axis=-1, keepdims=True) + EPS)
        h_ref[...] = (x * r * w_ref[...]).astype(BF16)

    return pl.pallas_call(
        kern, name="rms_fwd",
        out_shape=jax.ShapeDtypeStruct((t, d), BF16),
        grid=(t // ROWS,),
        in_specs=[pl.BlockSpec((ROWS, d), lambda i: (i, 0)), pl.BlockSpec((1, d), lambda i: (0, 0))],
        out_specs=pl.BlockSpec((ROWS, d), lambda i: (i, 0)),
        compiler_params=_cparams("parallel"),
    )(x2, w)


def _rms_bwd(dh, x2, w, dout):
    t, d = x2.shape

    def kern(dh_ref, x_ref, w_ref, dout_ref, gx_ref, dw_ref):
        @pl.when(pl.program_id(0) == 0)
        def _():
            dw_ref[...] = jnp.zeros_like(dw_ref)

        x = x_ref[...]
        r = lax.rsqrt(jnp.mean(x * x, axis=-1, keepdims=True) + EPS)
        xh = x * r
        g = dh_ref[...]
        dw_ref[...] += jnp.sum(g * xh, axis=0, keepdims=True)
        gw = g * w_ref[...]
        gx_ref[...] = dout_ref[...] + r * (gw - xh * jnp.mean(gw * xh, axis=-1, keepdims=True))

    row = pl.BlockSpec((ROWS, d), lambda i: (i, 0))
    vec = pl.BlockSpec((1, d), lambda i: (0, 0))
    return pl.pallas_call(
        kern, name="rms_bwd",
        out_shape=(jax.ShapeDtypeStruct((t, d), F32), jax.ShapeDtypeStruct((1, d), F32)),
        grid=(t // ROWS,),
        in_specs=[row, row, vec, row],
        out_specs=(row, vec),
        compiler_params=_cparams("arbitrary"),
    )(dh, x2, w, dout)


def _final_fwd_bwd(x2, mo, target, w):
    t, d = x2.shape

    def kern(x_ref, mo_ref, t_ref, w_ref, dout_ref, doutb_ref, loss_ref, dw_ref):
        @pl.when(pl.program_id(0) == 0)
        def _():
            loss_ref[...] = jnp.zeros_like(loss_ref)
            dw_ref[...] = jnp.zeros_like(dw_ref)

        u = x_ref[...] + mo_ref[...]
        r = lax.rsqrt(jnp.mean(u * u, axis=-1, keepdims=True) + EPS)
        uh = u * r
        wv = w_ref[...]
        err = uh * wv - t_ref[...]
        loss_ref[...] += (0.5 / d) * jnp.sum(err * err)
        dy = err * (1.0 / d)
        dw_ref[...] += jnp.sum(dy * uh, axis=0, keepdims=True)
        gw = dy * wv
        du = r * (gw - uh * jnp.mean(gw * uh, axis=-1, keepdims=True))
        dout_ref[...] = du
        doutb_ref[...] = du.astype(BF16)

    row = pl.BlockSpec((ROWS, d), lambda i: (i, 0))
    vec = pl.BlockSpec((1, d), lambda i: (0, 0))
    return pl.pallas_call(
        kern, name="final_fwd_bwd",
        out_shape=(jax.ShapeDtypeStruct((t, d), F32), jax.ShapeDtypeStruct((t, d), BF16),
                   jax.ShapeDtypeStruct((1, LANES), F32), jax.ShapeDtypeStruct((1, d), F32)),
        grid=(t // ROWS,),
        in_specs=[row, row, row, vec],
        out_specs=(row, row, pl.BlockSpec((1, LANES), lambda i: (0, 0)), vec),
        compiler_params=_cparams("arbitrary"),
    )(x2, mo, target, w)


def _merge_fwd(proj2, ya, ys):
    t = ya.shape[0]
    gblk = GATE0 // D_MODEL

    def kern(ga_ref, gs_ref, ya_ref, ys_ref, o_ref):
        o_ref[...] = (_sigmoid(ga_ref[...]) * ya_ref[...] + _sigmoid(gs_ref[...]) * ys_ref[...]).astype(BF16)

    row = pl.BlockSpec((ROWS, D_MODEL), lambda i: (i, 0))
    return pl.pallas_call(
        kern, name="merge_fwd",
        out_shape=jax.ShapeDtypeStruct((t, D_MODEL), BF16),
        grid=(t // ROWS,),
        in_specs=[pl.BlockSpec((ROWS, D_MODEL), lambda i: (i, gblk)),
                  pl.BlockSpec((ROWS, D_MODEL), lambda i: (i, gblk + 1)), row, row],
        out_specs=row,
        compiler_params=_cparams("parallel"),
    )(proj2, proj2, ya, ys)


def _merge_bwd(dm, proj2, ya, ys):
    t = ya.shape[0]
    gblk = GATE0 // D_MODEL

    def kern(dm_ref, ga_ref, gs_ref, ya_ref, ys_ref, dya_ref, dys_ref, dg_ref):
        g = dm_ref[...]
        sa = _sigmoid(ga_ref[...])
        ss = _sigmoid(gs_ref[...])
        dya_ref[...] = (g * sa).astype(BF16)
        dys_ref[...] = (g * ss).astype(BF16)
        dg_ref[:, :D_MODEL] = (g * ya_ref[...] * sa * (1.0 - sa)).astype(BF16)
        dg_ref[:, D_MODEL:] = (g * ys_ref[...] * ss * (1.0 - ss)).astype(BF16)

    row = pl.BlockSpec((ROWS, D_MODEL), lambda i: (i, 0))
    return pl.pallas_call(
        kern, name="merge_bwd",
        out_shape=(jax.ShapeDtypeStruct((t, D_MODEL), BF16), jax.ShapeDtypeStruct((t, D_MODEL), BF16),
                   jax.ShapeDtypeStruct((t, NP), BF16)),
        grid=(t // ROWS,),
        in_specs=[row, pl.BlockSpec((ROWS, D_MODEL), lambda i: (i, gblk)),
                  pl.BlockSpec((ROWS, D_MODEL), lambda i: (i, gblk + 1)), row, row],
        out_specs=(row, row, pl.BlockSpec((ROWS, 2 * D_MODEL), lambda i: (i, GATE0 // (2 * D_MODEL)))),
        compiler_params=_cparams("parallel"),
    )(dm, proj2, proj2, ya, ys)


TQ = 256
TK = 256
assert TQ == TK
HEAD_LANES = (slice(0, HEAD_DIM), slice(HEAD_DIM, 2 * HEAD_DIM))


def _tri(pred):
    r = lax.broadcasted_iota(jnp.int32, (TK, TK), 0)
    c = lax.broadcasted_iota(jnp.int32, (TK, TK), 1)
    return pred(r, c).astype(BF16)


def _split_bf16(v):
    hi = v.astype(BF16)
    lo = (v - hi.astype(F32)).astype(BF16)
    return hi, lo


def _tri_dot(v, tri):
    hi, lo = _split_bf16(v)
    return _dot(hi, tri, NN) + _dot(lo, tri, NN)


def _sb_logs(z, mask):
    l1p = jnp.log(1.0 + jnp.exp(-jnp.abs(z)))
    lb = jnp.minimum(z, 0.0) - l1p
    lom = -jnp.maximum(z, 0.0) - l1p
    if mask is not None:
        lom = jnp.where(mask, lom, 0.0)
    return lb, lom


def _sb_weights(lb, later, carry_r, mask):
    a = jnp.exp(lb + (later + carry_r))
    if mask is not None:
        a = jnp.where(mask, a, 0.0)
    return a


DEAD = -104.0


def _while_alive(n, carry, step):
    def alive(cr):
        return jnp.max(jnp.maximum(cr[0][0], cr[1][0])) > DEAD

    def cond(state):
        jj, go, _ = state
        return jnp.logical_and(jj < n, go)

    def body(state):
        jj, _, cr = state
        cr = step(jj, cr)
        return jj + 1, alive(cr), cr

    return lax.while_loop(cond, body, (jnp.int32(0), alive(carry), carry))[2]


Q_LANES, K_LANES, V_LANES, ZA_LANES = (slice(i * LANES, (i + 1) * LANES) for i in range(4))


def _split_heads(dst, src, scale=None):
    for h, lanes in enumerate(HEAD_LANES):
        v = src[:, lanes]
        dst[h] = (v if scale is None else v * scale).astype(BF16)


def _attn_fwd(proj3):
    b, s, _ = proj3.shape
    nq = s // TQ
    scale = HEAD_DIM ** -0.5

    def kern(x_ref, o_ref, yp_ref, qs, ks, vs):
        _split_heads(qs, x_ref[0, :, Q_LANES], scale)
        _split_heads(ks, x_ref[0, :, K_LANES])
        _split_heads(vs, x_ref[0, :, V_LANES])
        za_ref = x_ref.at[:, :, ZA_LANES]
        row = lax.broadcasted_iota(jnp.int32, (TQ, TK), 0)
        col = lax.broadcasted_iota(jnp.int32, (TQ, TK), 1)
        tri_gt = _tri(lambda j, sk: j > sk)

        def q_block(i, _):
            top = isinstance(i, int)
            r0 = i * TQ if top else pl.multiple_of(i * TQ, TQ)
            qh = [qs[h, pl.ds(r0, TQ), :] for h in range(2)]

            def k_blocks(blocks, carry):
                nb = range(len(blocks))
                kh = [[ks[h, pl.ds(c0, TK), :] for h in range(2)] for c0, _ in blocks]
                vh = [[vs[h, pl.ds(c0, TK), :] for h in range(2)] for c0, _ in blocks]
                z = [[_dot(qh[h], kh[bl][h], NT) for h in range(2)] for bl in nb]
                logs = [[_sb_logs(z[bl][h], blocks[bl][1]) for h in range(2)] for bl in nb]
                later = [[_tri_dot(logs[bl][h][1], tri_gt) for h in range(2)] for bl in nb]
                out = []
                for h in range(2):
                    carry_r, acc = carry[h]
                    for bl in nb:
                        lb, lom = logs[bl][h]
                        a = _sb_weights(lb, later[bl][h], carry_r, blocks[bl][1])
                        acc = acc + _dot(a.astype(BF16), vh[bl][h], NN)
                        carry_r = carry_r + (later[bl][h][:, 0:1] + lom[:, 0:1])
                    out.append((carry_r, acc))
                return tuple(out)

            start = (jnp.zeros((TQ, 1), F32), jnp.zeros((TQ, HEAD_DIM), F32))
            diag = (r0, col < row)
            if top:
                carry = k_blocks([diag], (start, start))
            else:
                carry = k_blocks([diag, (pl.multiple_of(r0 - TK, TK), None)], (start, start))
                carry = _while_alive(i - 1, carry, lambda jj, cr: k_blocks([(pl.multiple_of((i - 2 - jj) * TK, TK), None)], cr))
            for (_, acc), lanes in zip(carry, HEAD_LANES):
                o_ref[0, pl.ds(r0, TQ), lanes] = acc
                za = za_ref[0, pl.ds(r0, TQ), lanes]
                yp_ref[0, pl.ds(r0, TQ), lanes] = (acc * (za * _sigmoid(za))).astype(BF16)
            return 0

        q_block(0, 0)
        lax.fori_loop(1, nq, q_block, 0)

    out_spec = pl.BlockSpec((1, s, LANES), lambda bi, hp: (bi, 0, hp))
    return pl.pallas_call(
        kern, name="attn_fwd",
        out_shape=(jax.ShapeDtypeStruct((b, s, D_MODEL), F32), jax.ShapeDtypeStruct((b, s, D_MODEL), BF16)),
        grid=(b, SB_HEADS // 2),
        in_specs=[pl.BlockSpec((1, s, HP_WIDTH), lambda bi, hp: (bi, 0, hp))],
        out_specs=(out_spec, out_spec),
        scratch_shapes=[pltpu.VMEM((2, s, HEAD_DIM), BF16)] * 3,
        compiler_params=_cparams("parallel", "parallel"),
    )(proj3)


def _attn_bwd(proj3, dyp3, o3, dproj3):
    b, s, _ = proj3.shape
    nq = s // TQ
    scale = HEAD_DIM ** -0.5

    def kern(x_ref, dyp_ref, o_ref, _, d_ref, qs, ks, vs, dos, dk_acc, dv_acc):
        _split_heads(qs, x_ref[0, :, Q_LANES], scale)
        _split_heads(ks, x_ref[0, :, K_LANES])
        _split_heads(vs, x_ref[0, :, V_LANES])
        dq_ref, dk_ref, dv_ref = (d_ref.at[:, :, lanes] for lanes in (Q_LANES, K_LANES, V_LANES))
        za = x_ref[0, :, ZA_LANES]
        sg = _sigmoid(za)
        dyp = dyp_ref[0]
        _split_heads(dos, dyp * (za * sg))
        d_ref[0, :, ZA_LANES] = (dyp * o_ref[0] * (sg * (1.0 + za * (1.0 - sg)))).astype(BF16)
        dk_acc[...] = jnp.zeros_like(dk_acc)
        dv_acc[...] = jnp.zeros_like(dv_acc)
        row = lax.broadcasted_iota(jnp.int32, (TQ, TK), 0)
        col = lax.broadcasted_iota(jnp.int32, (TQ, TK), 1)
        tri_gt = _tri(lambda j, sk: j > sk)
        tri_ge = _tri(lambda j, sk: j >= sk)

        def q_block(i, _):
            top = isinstance(i, int)
            r0 = i * TQ if top else pl.multiple_of(i * TQ, TQ)
            qh = [qs[h, pl.ds(r0, TQ), :] for h in range(2)]
            doh = [dos[h, pl.ds(r0, TQ), :] for h in range(2)]
            totals = [jnp.sum(doh[h].astype(F32) * o_ref[0, pl.ds(r0, TQ), lanes], axis=1, keepdims=True)
                      for h, lanes in enumerate(HEAD_LANES)]

            def k_blocks(blocks, carry):
                nb = range(len(blocks))
                kh = [[ks[h, pl.ds(c0, TK), :] for h in range(2)] for c0, _ in blocks]
                vh = [[vs[h, pl.ds(c0, TK), :] for h in range(2)] for c0, _ in blocks]
                z = [[_dot(qh[h], kh[bl][h], NT) for h in range(2)] for bl in nb]
                da = [[_dot(doh[h], vh[bl][h], NT) for h in range(2)] for bl in nb]
                logs = [[_sb_logs(z[bl][h], blocks[bl][1]) for h in range(2)] for bl in nb]
                later = [[_tri_dot(logs[bl][h][1], tri_gt) for h in range(2)] for bl in nb]
                ab, g, suffix = ([[None, None] for _ in nb] for _ in range(3))
                for h in range(2):
                    cr = carry[h][0]
                    for bl in nb:
                        a = _sb_weights(logs[bl][h][0], later[bl][h], cr, blocks[bl][1])
                        ab[bl][h] = a.astype(BF16)
                        g[bl][h] = da[bl][h] * ab[bl][h].astype(F32)
                        suffix[bl][h] = _tri_dot(g[bl][h], tri_ge)
                        cr = cr + (later[bl][h][:, 0:1] + logs[bl][h][1][:, 0:1])
                out = []
                for h in range(2):
                    _, carry_g, dq = carry[h]
                    cr = carry[h][0]
                    for bl in nb:
                        c0, mask = blocks[bl]
                        lb, lom = logs[bl][h]
                        dz = g[bl][h] - (g[bl][h] + (totals[h] - carry_g) - suffix[bl][h]) * jnp.exp(lb)
                        if mask is not None:
                            dz = jnp.where(mask, dz, 0.0)
                        dzb = dz.astype(BF16)
                        dk_acc[h, pl.ds(c0, TK), :] += _dot(dzb, qh[h], TN)
                        dv_acc[h, pl.ds(c0, TK), :] += _dot(ab[bl][h], doh[h], TN)
                        dq = dq + _dot(dzb, kh[bl][h], NN)
                        carry_g = carry_g + suffix[bl][h][:, 0:1]
                        cr = cr + (later[bl][h][:, 0:1] + lom[:, 0:1])
                    out.append((cr, carry_g, dq))
                return tuple(out)

            def k_block(c0, carry, mask):
                kh = [ks[h, pl.ds(c0, TK), :] for h in range(2)]
                vh = [vs[h, pl.ds(c0, TK), :] for h in range(2)]
                z = [_dot(qh[h], kh[h], NT) for h in range(2)]
                da = [_dot(doh[h], vh[h], NT) for h in range(2)]
                logs, later = [], []
                for h in range(2):
                    logs.append(_sb_logs(z[h], mask))
                    later.append(_tri_dot(logs[h][1], tri_gt))
                ab, g, suffix = [], [], []
                for h in range(2):
                    a = _sb_weights(logs[h][0], later[h], carry[h][0], mask)
                    ab.append(a.astype(BF16))
                    g.append(da[h] * ab[h].astype(F32))
                    suffix.append(_tri_dot(g[h], tri_ge))
                out = []
                for h in range(2):
                    carry_r, carry_g, dq = carry[h]
                    lb, lom = logs[h]
                    dz = g[h] - (g[h] + (totals[h] - carry_g) - suffix[h]) * jnp.exp(lb)
                    if mask is not None:
                        dz = jnp.where(mask, dz, 0.0)
                    dzb = dz.astype(BF16)
                    dk_acc[h, pl.ds(c0, TK), :] += _dot(dzb, qh[h], TN)
                    dv_acc[h, pl.ds(c0, TK), :] += _dot(ab[h], doh[h], TN)
                    out.append((carry_r + (later[h][:, 0:1] + lom[:, 0:1]), carry_g + suffix[h][:, 0:1],
                                dq + _dot(dzb, kh[h], NN)))
                return tuple(out)

            zero = jnp.zeros((TQ, 1), F32)
            start = (zero, zero, jnp.zeros((TQ, HEAD_DIM), F32))
            diag = (r0, col < row)
            if top:
                carry = k_block(r0, (start, start), col < row)
            else:
                carry = k_blocks([diag, (pl.multiple_of(r0 - TK, TK), None)], (start, start))
                carry = _while_alive(i - 1, carry, lambda jj, cr: k_block(pl.multiple_of((i - 2 - jj) * TK, TK), cr, None))
            for (_, _, dq), lanes in zip(carry, HEAD_LANES):
                dq_ref[0, pl.ds(r0, TQ), lanes] = (dq * scale).astype(BF16)
            return 0

        q_block(0, 0)
        lax.fori_loop(1, nq, q_block, 0)

        for h, lanes in enumerate(HEAD_LANES):
            dk_ref[0, :, lanes] = dk_acc[h].astype(BF16)
            dv_ref[0, :, lanes] = dv_acc[h].astype(BF16)

    plain = pl.BlockSpec((1, s, LANES), lambda bi, hp: (bi, 0, hp))
    pair = pl.BlockSpec((1, s, HP_WIDTH), lambda bi, hp: (bi, 0, hp))
    return pl.pallas_call(
        kern, name="attn_bwd",
        out_shape=jax.ShapeDtypeStruct(dproj3.shape, dproj3.dtype),
        grid=(b, SB_HEADS // 2),
        in_specs=[pair, plain, plain, ANY],
        out_specs=pair,
        input_output_aliases={3: 0},
        scratch_shapes=[pltpu.VMEM((2, s, HEAD_DIM), BF16)] * 4 + [pltpu.VMEM((2, s, HEAD_DIM), F32)] * 2,
        compiler_params=_cparams("parallel", "parallel"),
    )(proj3, dyp3, o3, dproj3)


CONV_COLS = 256
HALO = 8


def _conv_pre(xp, w_ref, b_ref, r0):
    pre = b_ref[...] + w_ref[CONV_K - 1:CONV_K, :] * xp[pl.ds(HALO + r0, CHUNK), :]
    for kk in range(1, CONV_K):
        pre = pre + w_ref[CONV_K - 1 - kk:CONV_K - kk, :] * xp[pl.ds(HALO + r0 - kk, CHUNK), :]
    return pre


def _conv_fwd(proj3, conv_w, conv_b):
    b, s, _ = proj3.shape
    nc = s // CHUNK

    def kern(x_ref, w_ref, b_ref, o_ref, xp):
        xp[0:HALO, :] = jnp.zeros((HALO, CONV_COLS), F32)
        xp[HALO:, :] = x_ref[0]
        for ci in range(nc):
            pre = _conv_pre(xp, w_ref, b_ref, ci * CHUNK)
            o_ref[0, ci * CHUNK:(ci + 1) * CHUNK, :] = pre * _sigmoid(pre)

    return pl.pallas_call(
        kern, name="conv_fwd",
        out_shape=jax.ShapeDtypeStruct((b, s, CONV_DIM), F32),
        grid=(CONV_DIM // CONV_COLS, b),
        in_specs=[pl.BlockSpec((1, s, CONV_COLS), lambda j, bi: (bi, 0, XBC0 // CONV_COLS + j)),
                  pl.BlockSpec((CONV_K, CONV_COLS), lambda j, bi: (0, j)),
                  pl.BlockSpec((1, CONV_COLS), lambda j, bi: (0, j))],
        out_specs=pl.BlockSpec((1, s, CONV_COLS), lambda j, bi: (bi, 0, j)),
        scratch_shapes=[pltpu.VMEM((s + HALO, CONV_COLS), F32)],
        compiler_params=_cparams("parallel", "parallel"),
    )(proj3, conv_w, conv_b)


def _conv_bwd(dact, proj3, conv_w, conv_b, col0, name, dproj3):
    b, s, width = dact.shape
    nc = s // CHUNK
    j0 = col0 // CONV_COLS

    def kern(da_ref, x_ref, w_ref, b_ref, _, dx_ref, dw_ref, db_ref, xp, dp):
        @pl.when(pl.program_id(1) == 0)
        def _():
            dw_ref[...] = jnp.zeros_like(dw_ref)
            db_ref[...] = jnp.zeros_like(db_ref)

        xp[0:HALO, :] = jnp.zeros((HALO, CONV_COLS), F32)
        xp[HALO:, :] = x_ref[0]
        dp[s:, :] = jnp.zeros((HALO, CONV_COLS), F32)
        for ci in range(nc):
            r0 = ci * CHUNK
            pre = _conv_pre(xp, w_ref, b_ref, r0)
            sg = _sigmoid(pre)
            dpre = da_ref[0, r0:r0 + CHUNK, :] * (sg * (1.0 + pre * (1.0 - sg)))
            dp[r0:r0 + CHUNK, :] = dpre
            db_ref[...] += jnp.sum(dpre, axis=0, keepdims=True)
            for kk in range(CONV_K):
                tap = CONV_K - 1 - kk
                dw_ref[tap:tap + 1, :] += jnp.sum(dpre * xp[pl.ds(HALO + r0 - kk, CHUNK), :], axis=0, keepdims=True)
        for ci in range(nc):
            r0 = ci * CHUNK
            dx = w_ref[CONV_K - 1:CONV_K, :] * dp[pl.ds(r0, CHUNK), :]
            for kk in range(1, CONV_K):
                dx = dx + w_ref[CONV_K - 1 - kk:CONV_K - kk, :] * dp[pl.ds(r0 + kk, CHUNK), :]
            dx_ref[0, r0:r0 + CHUNK, :] = dx.astype(BF16)

    return pl.pallas_call(
        kern, name=name,
        out_shape=(jax.ShapeDtypeStruct(dproj3.shape, dproj3.dtype), jax.ShapeDtypeStruct((CONV_K, width), F32),
                   jax.ShapeDtypeStruct((1, width), F32)),
        grid=(width // CONV_COLS, b),
        in_specs=[pl.BlockSpec((1, s, CONV_COLS), lambda j, bi: (bi, 0, j)),
                  pl.BlockSpec((1, s, CONV_COLS), lambda j, bi: (bi, 0, XBC0 // CONV_COLS + j0 + j)),
                  pl.BlockSpec((CONV_K, CONV_COLS), lambda j, bi: (0, j0 + j)),
                  pl.BlockSpec((1, CONV_COLS), lambda j, bi: (0, j0 + j)), ANY],
        out_specs=(pl.BlockSpec((1, s, CONV_COLS), lambda j, bi: (bi, 0, XBC0 // CONV_COLS + j0 + j)),
                   pl.BlockSpec((CONV_K, CONV_COLS), lambda j, bi: (0, j)),
                   pl.BlockSpec((1, CONV_COLS), lambda j, bi: (0, j))),
        input_output_aliases={4: 0},
        scratch_shapes=[pltpu.VMEM((s + HALO, CONV_COLS), F32)] * 2,
        compiler_params=_cparams("parallel", "arbitrary"),
    )(dact, proj3, conv_w, conv_b, dproj3)


def _sel_dot(v, sel, left=False):
    hi = v.astype(BF16)
    rest = v - hi.astype(F32)
    mid = rest.astype(BF16)
    lo = (rest - mid.astype(F32)).astype(BF16)
    if left:
        return _dot(sel, hi, NN) + _dot(sel, mid, NN) + _dot(sel, lo, NN)
    return _dot(hi, sel, NN) + _dot(mid, sel, NN) + _dot(lo, sel, NN)


def _ssd_common(dtr_ref, dtb_ref, alog_ref):
    lane = lax.broadcasted_iota(jnp.int32, (CHUNK, LANES), 1)
    row = lax.broadcasted_iota(jnp.int32, (CHUNK, LANES), 0)
    head_lane = lane < HEADS_PER_GROUP
    pre = dtr_ref[0, 0] + dtb_ref[0]
    dt = jnp.where(head_lane, jnp.maximum(pre, 0.0) + jnp.log(1.0 + jnp.exp(-jnp.abs(pre))), 0.0)
    a = jnp.where(head_lane[0:1], -jnp.exp(alog_ref[0]), 0.0)
    tril = (row >= lane).astype(BF16)
    acs = _sel_dot(dt * a, tril, left=True)
    acs_t = acs.T
    er = lax.broadcasted_iota(jnp.int32, (LANES, GROUP_WIDTH), 0)
    ec = lax.broadcasted_iota(jnp.int32, (LANES, GROUP_WIDTH), 1)
    expand = ((ec // HEAD_DIM) == er).astype(BF16)
    tr = lax.broadcasted_iota(jnp.int32, (GROUP_WIDTH, LANES), 0)
    tc = lax.broadcasted_iota(jnp.int32, (GROUP_WIDTH, LANES), 1)
    reduce = ((tr // HEAD_DIM) == tc).astype(BF16)
    dt_x = _sel_dot(dt, expand)
    acs_x = _sel_dot(acs, expand)
    end_x = acs_x[CHUNK - 1:CHUNK, :]
    causal = row >= lane
    return dict(dt=dt, a=a, pre=pre, head_lane=head_lane, acs=acs, acs_t=acs_t, expand=expand, reduce=reduce,
                dt_x=dt_x, acs_x=acs_x, end_x=end_x, causal=causal, row=row, lane=lane)


def _ssd_decay(cm, h):
    seg = cm["acs"][:, h:h + 1] - cm["acs_t"][h:h + 1, :]
    return jnp.where(cm["causal"], jnp.exp(jnp.minimum(seg, 0.0)), 0.0)


def _ssd_fwd(xact, proj3, dtr_g, dtb_g, alog_g, dskip_x, snw):
    b, s, _ = xact.shape
    nc = s // CHUNK
    g4 = SSD_GROUPS

    def kern(xs_ref, bm_ref, cm_ref, zs_ref, dtr_ref, dtb_ref, alog_ref, dsk_ref, snw_ref,
             y_ref, yn_ref, hst_ref, h_sc):
        @pl.when(pl.program_id(2) == 0)
        def _():
            h_sc[...] = jnp.zeros_like(h_sc)

        cm = _ssd_common(dtr_ref, dtb_ref, alog_ref)
        x = xs_ref[0]
        bmb = bm_ref[0].astype(BF16)
        cmb = cm_ref[0].astype(BF16)
        h_in = h_sc[...]
        hst_ref[0, 0, 0] = h_in
        xdt = x * cm["dt_x"]
        xdtb = xdt.astype(BF16)
        cb = _dot(cmb, bmb, NT)
        y_off = _dot(cmb, h_in.astype(BF16), NN) * jnp.exp(cm["acs_x"])
        for h in range(HEADS_PER_GROUP):
            lanes = slice(h * HEAD_DIM, (h + 1) * HEAD_DIM)
            m = (cb * _ssd_decay(cm, h)).astype(BF16)
            y_ref[0, :, lanes] = _dot(m, xdtb[:, lanes], NN)
        y = y_ref[0] + y_off + x * dsk_ref[...]
        y_ref[0] = y
        w = (xdt * jnp.exp(cm["end_x"] - cm["acs_x"])).astype(BF16)
        h_sc[...] = h_in * jnp.exp(cm["end_x"]) + _dot(bmb, w, TN)
        zs = zs_ref[0]
        y2 = y * (zs * _sigmoid(zs))
        yn_ref[0] = (y2 * lax.rsqrt(jnp.mean(y2 * y2, axis=-1, keepdims=True) + EPS) * snw_ref[...]).astype(BF16)

    gw = GROUP_WIDTH
    small = pl.BlockSpec((1, 1, LANES), lambda gi, bi, ci: (gi, 0, 0))
    xblk = pl.BlockSpec((1, CHUNK, gw), lambda gi, bi, ci: (bi, ci, gi))
    return pl.pallas_call(
        kern, name="ssd_fwd",
        out_shape=(jax.ShapeDtypeStruct((b, s, SSD_WIDTH), F32), jax.ShapeDtypeStruct((b, s, SSD_WIDTH), BF16),
                   jax.ShapeDtypeStruct((b, nc, g4, SSD_STATE, gw), F32)),
        grid=(g4, b, nc),
        in_specs=[xblk,
                  pl.BlockSpec((1, CHUNK, LANES), lambda gi, bi, ci: (bi, ci, SSD_WIDTH // LANES + gi)),
                  pl.BlockSpec((1, CHUNK, LANES), lambda gi, bi, ci: (bi, ci, SSD_WIDTH // LANES + g4 + gi)),
                  pl.BlockSpec((1, CHUNK, gw), lambda gi, bi, ci: (bi, ci, ZS0 // gw + gi)),
                  pl.BlockSpec((1, 1, CHUNK, LANES), lambda gi, bi, ci: (bi, gi, ci, 0)),
                  small, small,
                  pl.BlockSpec((1, gw), lambda gi, bi, ci: (0, gi)),
                  pl.BlockSpec((1, gw), lambda gi, bi, ci: (0, gi))],
        out_specs=(xblk, xblk, pl.BlockSpec((1, 1, 1, SSD_STATE, gw), lambda gi, bi, ci: (bi, ci, gi, 0, 0))),
        scratch_shapes=[pltpu.VMEM((SSD_STATE, gw), F32)],
        compiler_params=_cparams("parallel", "parallel", "arbitrary"),
    )(xact, xact, xact, proj3, dtr_g, dtb_g, alog_g, dskip_x, snw)


def _ssd_bwd(dyn3, y3, xact, proj3, hst, dtr_g, dtb_g, alog_g, dskip_x, snw, dproj3):
    b, s, _ = xact.shape
    nc = s // CHUNK
    g4 = SSD_GROUPS
    gw = GROUP_WIDTH

    def kern(dyn_ref, y_ref, xs_ref, bm_ref, cm_ref, zs_ref, hst_ref, dtr_ref, dtb_ref, alog_ref, dsk_ref, snw_ref, _,
             dxs_ref, dbm_ref, dcm_ref, dzs_ref, ddtr_ref, dsnw_ref, dalog_ref, ddtb_ref, ddsk_ref, dh_sc):
        first = jnp.logical_and(pl.program_id(1) == 0, pl.program_id(2) == 0)

        @pl.when(first)
        def _():
            dsnw_ref[...] = jnp.zeros_like(dsnw_ref)
            dalog_ref[...] = jnp.zeros_like(dalog_ref)
            ddtb_ref[...] = jnp.zeros_like(ddtb_ref)
            ddsk_ref[...] = jnp.zeros_like(ddsk_ref)

        @pl.when(pl.program_id(2) == 0)
        def _():
            dh_sc[...] = jnp.zeros_like(dh_sc)

        cm = _ssd_common(dtr_ref, dtb_ref, alog_ref)
        row, lane = cm["row"], cm["lane"]
        y = y_ref[0]
        zs = zs_ref[0]
        sg = _sigmoid(zs)
        silu = zs * sg
        y2 = y * silu
        rstd = lax.rsqrt(jnp.mean(y2 * y2, axis=-1, keepdims=True) + EPS)
        y2h = y2 * rstd
        dyn = dyn_ref[0]
        dsnw_ref[0] += jnp.sum(dyn * y2h, axis=0, keepdims=True)
        gwv = dyn * snw_ref[...]
        dy2 = rstd * (gwv - y2h * jnp.mean(gwv * y2h, axis=-1, keepdims=True))
        dzs_ref[0] = (dy2 * y * (sg * (1.0 + zs * (1.0 - sg)))).astype(BF16)
        dy = dy2 * silu
        dyb = dy.astype(BF16)

        x = xs_ref[0]
        bmb = bm_ref[0].astype(BF16)
        cmb = cm_ref[0].astype(BF16)
        h_in = hst_ref[0, 0, 0]
        h_inb = h_in.astype(BF16)
        d_hn = dh_sc[...]
        d_hnb = d_hn.astype(BF16)
        xdt = x * cm["dt_x"]
        xdtb = xdt.astype(BF16)
        eacs = jnp.exp(cm["acs_x"])
        dte = jnp.exp(cm["end_x"] - cm["acs_x"])
        wb = (xdt * dte).astype(BF16)

        dsk_lanes = jnp.broadcast_to(jnp.sum(dy * x, axis=0, keepdims=True), (8, gw))
        ddsk_ref[0] += _sel_dot(dsk_lanes, cm["reduce"])[0:1, :]
        dyo = dy * eacs
        dyob = dyo.astype(BF16)
        dacs_x = dyo * _dot(cmb, h_inb, NN)
        dcm = _dot(dyob, h_inb, NT)
        dh_in = _dot(cmb, dyob, TN)
        dw = _dot(bmb, d_hnb, NN)
        dbm = _dot(wb, d_hnb, NT)
        dxdt = dw * dte
        e_l = dw * xdt * dte
        dacs_x = dacs_x - e_l
        dend_x = jnp.sum(e_l, axis=0, keepdims=True)
        chunk_decay = jnp.exp(cm["end_x"])
        dh_sc[...] = d_hn * chunk_decay + dh_in
        dend_x = dend_x + jnp.sum(d_hn * h_in, axis=0, keepdims=True) * chunk_decay
        last_row = lax.broadcasted_iota(jnp.int32, (CHUNK, gw), 0) == CHUNK - 1
        dacs_x = dacs_x + jnp.where(last_row, dend_x, 0.0)

        cb = _dot(cmb, bmb, NT)
        dcb = jnp.zeros((CHUNK, CHUNK), F32)
        dacs = jnp.zeros((CHUNK, LANES), F32)
        dacs_t = jnp.zeros((LANES, CHUNK), F32)
        for h in range(HEADS_PER_GROUP):
            lanes = slice(h * HEAD_DIM, (h + 1) * HEAD_DIM)
            decay = _ssd_decay(cm, h)
            m = cb * decay
            dm = _dot(dyb[:, lanes], xdtb[:, lanes], NT)
            dxs_ref[0, :, lanes] = _dot(m.astype(BF16), dyb[:, lanes], TN)
            dcb_h = dm * decay
            dcb = dcb + dcb_h
            n = dcb_h * cb
            dacs = dacs + jnp.where(lane == h, jnp.sum(n, axis=1, keepdims=True), 0.0)
            dacs_t = dacs_t + jnp.where(row == h, jnp.sum(n, axis=0, keepdims=True), 0.0)
        dcbb = dcb.astype(BF16)
        dcm_ref[0] = dcm + _dot(dcbb, bmb, NN)
        dbm_ref[0] = dbm + _dot(dcbb, cmb, TN)
        dxdt = dxdt + dxs_ref[0]
        dxs_ref[0] = dy * dsk_ref[...] + dxdt * cm["dt_x"]

        dacs = dacs - dacs_t.T + _sel_dot(dacs_x, cm["reduce"])
        ddt = _sel_dot(dxdt * x, cm["reduce"])
        triu = (row <= lane).astype(BF16)
        rc = _sel_dot(dacs, triu, left=True)
        ddt = ddt + cm["a"] * rc
        dalog_ref[0] += jnp.sum(cm["dt"] * rc, axis=0, keepdims=True) * cm["a"]
        ddtr = jnp.where(cm["head_lane"], ddt * _sigmoid(cm["pre"]), 0.0)
        ddtr_ref[0, 0] = ddtr
        ddtb_ref[0] += jnp.sum(ddtr, axis=0, keepdims=True)

    def rev(ci):
        return nc - 1 - ci

    small = pl.BlockSpec((1, 1, LANES), lambda gi, bi, ci: (gi, 0, 0))
    xblk = pl.BlockSpec((1, CHUNK, gw), lambda gi, bi, ci: (bi, rev(ci), gi))
    nblk = pl.BlockSpec((1, CHUNK, LANES), lambda gi, bi, ci: (bi, rev(ci), gi))
    gvec = pl.BlockSpec((1, gw), lambda gi, bi, ci: (0, gi))
    gacc = pl.BlockSpec((1, 1, gw), lambda gi, bi, ci: (gi, 0, 0))
    return pl.pallas_call(
        kern, name="ssd_bwd",
        out_shape=(jax.ShapeDtypeStruct((b, s, SSD_WIDTH), F32),
                   jax.ShapeDtypeStruct((b, s, g4 * SSD_STATE), F32),
                   jax.ShapeDtypeStruct((b, s, g4 * SSD_STATE), F32),
                   jax.ShapeDtypeStruct(dproj3.shape, dproj3.dtype),
                   jax.ShapeDtypeStruct((b, g4, s, LANES), F32),
                   jax.ShapeDtypeStruct((g4, 1, gw), F32),
                   jax.ShapeDtypeStruct((g4, 1, LANES), F32),
                   jax.ShapeDtypeStruct((g4, 1, LANES), F32),
                   jax.ShapeDtypeStruct((g4, 1, LANES), F32)),
        grid=(g4, b, nc),
        in_specs=[xblk, xblk, xblk,
                  pl.BlockSpec((1, CHUNK, LANES), lambda gi, bi, ci: (bi, rev(ci), SSD_WIDTH // LANES + gi)),
                  pl.BlockSpec((1, CHUNK, LANES), lambda gi, bi, ci: (bi, rev(ci), SSD_WIDTH // LANES + g4 + gi)),
                  pl.BlockSpec((1, CHUNK, gw), lambda gi, bi, ci: (bi, rev(ci), ZS0 // gw + gi)),
                  pl.BlockSpec((1, 1, 1, SSD_STATE, gw), lambda gi, bi, ci: (bi, rev(ci), gi, 0, 0)),
                  pl.BlockSpec((1, 1, CHUNK, LANES), lambda gi, bi, ci: (bi, gi, rev(ci), 0)),
                  small, small, gvec, gvec, ANY],
        out_specs=(xblk, nblk, nblk,
                   pl.BlockSpec((1, CHUNK, gw), lambda gi, bi, ci: (bi, rev(ci), ZS0 // gw + gi)),
                   pl.BlockSpec((1, 1, CHUNK, LANES), lambda gi, bi, ci: (bi, gi, rev(ci), 0)),
                   gacc, small, small, small),
        input_output_aliases={12: 3},
        scratch_shapes=[pltpu.VMEM((SSD_STATE, gw), F32)],
        compiler_params=_cparams("parallel", "arbitrary", "arbitrary"),
    )(dyn3, y3, xact, xact, xact, proj3, hst, dtr_g, dtb_g, alog_g, dskip_x, snw, dproj3)


def _adamw(w, g, m, v, name):
    r, c = w.shape
    tr = 128 if r % 128 == 0 else r
    tc = LANES if (tr == r and r > 128 and c % LANES == 0) else c

    def kern(w_ref, g_ref, m_ref, v_ref, d_ref, nm_ref, nv_ref):
        gv = g_ref[...]
        nm = ADAM_B1 * m_ref[...] + (1.0 - ADAM_B1) * gv
        nv = ADAM_B2 * v_ref[...] + (1.0 - ADAM_B2) * (gv * gv)
        m_hat = nm / (1.0 - ADAM_B1 ** ADAM_STEP)
        v_hat = nv / (1.0 - ADAM_B2 ** ADAM_STEP)
        d_ref[...] = -ADAM_LR * (m_hat / (jnp.sqrt(v_hat) + ADAM_EPS) + ADAM_WD * w_ref[...])
        nm_ref[...] = nm
        nv_ref[...] = nv

    blk = pl.BlockSpec((tr, tc), lambda i, j: (i, j))
    out = jax.ShapeDtypeStruct((r, c), F32)
    return pl.pallas_call(
        kern, name=name, out_shape=(out, out, out), grid=(r // tr, c // tc),
        in_specs=[blk] * 4, out_specs=(blk, blk, blk),
        compiler_params=_cparams("parallel", "parallel"),
    )(w, g, m, v)


ANY = pl.BlockSpec(memory_space=pl.ANY)


def _position():
    return lax.axis_index("x"), lax.axis_index("y"), lax.axis_index("c")


def _other_chips(x, y):
    return [(1 - x, y), (x, 1 - y), (1 - x, 1 - y)]


def _dma_sems(n):
    return [pltpu.SemaphoreType.DMA((n,)), pltpu.SemaphoreType.DMA((n,))]


class _Exchange:
    def __init__(self, inputs, out_shapes, sems, start, finish):
        self.inputs, self.out_shapes, self.sems, self.start, self.finish = inputs, out_shapes, sems, start, finish


def _run_exchange(ex, name):
    n_in, n_out = len(ex.inputs), len(ex.out_shapes)

    def body(*refs):
        x_in, x_out, sems = refs[:n_in], refs[n_in:n_in + n_out], refs[n_in + n_out:]
        ex.start(x_in, x_out, sems)
        ex.finish(x_in, x_out, sems)

    return pl.pallas_call(
        body, name=name, out_shape=list(ex.out_shapes),
        in_specs=[ANY] * n_in, out_specs=[ANY] * n_out, scratch_shapes=list(ex.sems),
    )(*ex.inputs)


def _gather_exchange(shards):
    n = len(shards)

    def copies(p_refs, out_refs, sems):
        send_sems, recv_sems = sems
        x, y, c = _position()
        me = 2 * x + y
        chips = _other_chips(x, y)

        def slab(a, chip, hf):
            half = shards[a].shape[1] // 2
            return out_refs[a].at[chip, :, pl.ds(hf * half, half)]

        def my_half(a):
            half = shards[a].shape[1] // 2
            return p_refs[a].at[:, pl.ds(c * half, half)]

        def over_ici(a, j, chip_from):
            px, py = chips[j]
            return pltpu.make_async_remote_copy(
                src_ref=my_half(a), dst_ref=slab(a, chip_from, c),
                send_sem=send_sems.at[3 * a + j], recv_sem=recv_sems.at[3 * a + j],
                device_id=(px, py, c), device_id_type=MESH)

        def to_sibling(a, j, hf):
            px, py = chips[j]
            return pltpu.make_async_remote_copy(
                src_ref=slab(a, 2 * px + py, hf), dst_ref=slab(a, 2 * px + py, hf),
                send_sem=send_sems.at[3 * (n + a) + j], recv_sem=recv_sems.at[3 * (n + a) + j],
                device_id=(x, y, 1 - c), device_id_type=MESH)

        own = [pltpu.make_async_remote_copy(
            src_ref=p_refs[a], dst_ref=out_refs[a].at[me], send_sem=send_sems.at[6 * n + a], recv_sem=recv_sems.at[6 * n + a],
            device_id=(x, y, 1 - c), device_id_type=MESH) for a in range(n)]
        first = [over_ici(a, j, me) for a in range(n) for j in range(3)]
        return chips, c, over_ici, to_sibling, first, own

    def start(p_refs, out_refs, sems):
        _, _, _, _, first, own = copies(p_refs, out_refs, sems)
        for cp in first + own:
            cp.start()

    def finish(p_refs, out_refs, sems):
        chips, c, over_ici, to_sibling, first, own = copies(p_refs, out_refs, sems)
        passed = []
        for a in range(n):
            for j, (px, py) in enumerate(chips):
                over_ici(a, j, 2 * px + py).wait_recv()
                passed.append(to_sibling(a, j, c))
                passed[-1].start()
        for a in range(n):
            for j in range(3):
                to_sibling(a, j, 1 - c).wait_recv()
        for cp in first + passed:
            cp.wait_send()
        for cp in own:
            cp.wait()

    return _Exchange(list(shards), [jax.ShapeDtypeStruct((N_CHIPS, *v.shape), v.dtype) for v in shards],
                     _dma_sems(7 * n), start, finish)


def _swap_halves(parts, name):
    n = len(parts)

    def body(*refs):
        v_refs, out_refs = refs[:n], refs[n:2 * n]
        send_sems, recv_sems = refs[2 * n:]
        x, y, c = _position()
        copies = []
        for a in range(n):
            half = parts[a].shape[2] // 2
            copies.append(pltpu.make_async_remote_copy(
                src_ref=v_refs[a].at[:, :, pl.ds((1 - c) * half, half)], dst_ref=out_refs[a],
                send_sem=send_sems.at[a], recv_sem=recv_sems.at[a], device_id=(x, y, 1 - c), device_id_type=MESH))
        for cp in copies:
            cp.start()
        for cp in copies:
            cp.wait()

    return pl.pallas_call(
        body, name=name,
        out_shape=[jax.ShapeDtypeStruct((v.shape[0], v.shape[1], v.shape[2] // 2), v.dtype) for v in parts],
        in_specs=[ANY] * n, out_specs=[ANY] * n,
        scratch_shapes=_dma_sems(n),
    )(*parts)


def _all_to_all_exchange(parts):
    n = len(parts)

    def sends(p_refs, out_refs, sems):
        send_sems, recv_sems = sems
        x, y, c = _position()
        return [pltpu.make_async_remote_copy(
            src_ref=p_refs[a].at[2 * px + py], dst_ref=out_refs[a].at[j],
            send_sem=send_sems.at[3 * a + j], recv_sem=recv_sems.at[3 * a + j],
            device_id=(px, py, c), device_id_type=MESH) for a in range(n) for j, (px, py) in enumerate(_other_chips(x, y))]

    def start(p_refs, out_refs, sems):
        for cp in sends(p_refs, out_refs, sems):
            cp.start()

    def finish(p_refs, out_refs, sems):
        for cp in sends(p_refs, out_refs, sems):
            cp.wait()

    return _Exchange(list(parts), [jax.ShapeDtypeStruct((N_CHIPS - 1, *v.shape[1:]), v.dtype) for v in parts],
                     _dma_sems(3 * n), start, finish)


def _join_halves(wholes):
    n = len(wholes)

    def body(*refs):
        out_refs = refs[n:2 * n]
        send_sems, recv_sems = refs[2 * n:]
        x, y, c = _position()
        copies = []
        for a in range(n):
            half = wholes[a].shape[1] // 2
            mine = out_refs[a].at[:, pl.ds(c * half, half)]
            copies.append(pltpu.make_async_remote_copy(
                src_ref=mine, dst_ref=mine, send_sem=send_sems.at[a], recv_sem=recv_sems.at[a],
                device_id=(x, y, 1 - c), device_id_type=MESH))
        for cp in copies:
            cp.start()
        for cp in copies:
            cp.wait()

    return pl.pallas_call(
        body, name="grad_join_halves",
        out_shape=[jax.ShapeDtypeStruct(v.shape, v.dtype) for v in wholes],
        in_specs=[ANY] * n, out_specs=[ANY] * n,
        input_output_aliases={a: a for a in range(n)},
        scratch_shapes=_dma_sems(n),
    )(*wholes)


STRIP = 256


def _add_halves(g, sw, place, name):
    n, rows, cols = g.shape
    nb = cols // 2 // STRIP

    def kern(p_ref, g_ref, s_ref, o_ref):
        o_ref[...] = (g_ref[...] + s_ref[...]).astype(BF16)

    blk = pl.BlockSpec((1, rows, STRIP), lambda j, i, p_ref: (j, 0, i))
    return pl.pallas_call(
        kern, name=name,
        out_shape=jax.ShapeDtypeStruct((n, rows, cols // 2), BF16),
        grid_spec=pltpu.PrefetchScalarGridSpec(
            num_scalar_prefetch=1, grid=(n, nb),
            in_specs=[pl.BlockSpec((1, rows, STRIP), lambda j, i, p_ref: (j, 0, p_ref[0] * nb + i)), blk],
            out_specs=blk),
        compiler_params=_cparams("parallel", "parallel"),
    )(place, g, sw)


def _sum_chips(own, rx, place, name):
    _, rows, half = rx.shape
    nb = half // STRIP

    def kern(p_ref, own_ref, r_ref, o_ref):
        total = own_ref[0].astype(F32)
        for j in range(N_CHIPS - 1):
            total = total + r_ref[j].astype(F32)
        o_ref[...] = total

    return pl.pallas_call(
        kern, name=name,
        out_shape=jax.ShapeDtypeStruct((rows, 2 * half), F32),
        grid_spec=pltpu.PrefetchScalarGridSpec(
            num_scalar_prefetch=1, grid=(nb,),
            in_specs=[pl.BlockSpec((1, rows, STRIP), lambda i, p_ref: (p_ref[1], 0, i)),
                      pl.BlockSpec((N_CHIPS - 1, rows, STRIP), lambda i, p_ref: (0, 0, i))],
            out_specs=pl.BlockSpec((rows, STRIP), lambda i, p_ref: (0, p_ref[0] * nb + i))),
        compiler_params=_cparams("parallel"),
    )(place, own, rx)


def _gather_small(v, reduce, name):
    rows = v.shape[0]

    def body(v_ref, out_ref, buf, send_sems, recv_sems):
        x, y, c = _position()
        me = 4 * x + 2 * y + c
        buf[me] = v_ref[...]
        peers = [(x ^ (k >> 2), y ^ ((k >> 1) & 1), c ^ (k & 1)) for k in range(1, 8)]
        copies = [pltpu.make_async_remote_copy(
            src_ref=v_ref, dst_ref=buf.at[me],
            send_sem=send_sems.at[k], recv_sem=recv_sems.at[k],
            device_id=peer, device_id_type=MESH) for k, peer in enumerate(peers)]
        for cp in copies:
            cp.start()
        for k, (px, py, pc) in enumerate(peers):
            pltpu.make_async_remote_copy(
                src_ref=v_ref, dst_ref=buf.at[4 * px + 2 * py + pc],
                send_sem=send_sems.at[k], recv_sem=recv_sems.at[k],
                device_id=(px, py, pc), device_id_type=MESH).wait_recv()
        for cp in copies:
            cp.wait_send()
        if reduce:
            total = buf[0]
            for d in range(1, 8):
                total = total + buf[d]
            out_ref[...] = total
        else:
            out_ref[...] = buf[...]

    vm = pl.BlockSpec(memory_space=pltpu.VMEM)
    return pl.pallas_call(
        body, name=name,
        out_shape=jax.ShapeDtypeStruct((rows, LANES) if reduce else (8, rows, LANES), F32),
        in_specs=[vm], out_specs=vm,
        scratch_shapes=[pltpu.VMEM((8, rows, LANES), F32), pltpu.SemaphoreType.DMA((7,)), pltpu.SemaphoreType.DMA((7,))],
    )(v)


def _pad_rows(a, rows):
    return jnp.pad(a, ((0, rows - a.shape[0]), (0, 0)))


def _lane_pad(v):
    n = v.shape[1]
    return jnp.pad(v, ((0, 0), (0, -n % LANES)))


def _gather_all(w_in, w_attn_out, w_ssm_out, w_o, conv_w):
    d = D_MODEL
    w_in_t, = _run_exchange(_gather_exchange([w_in[0].T.astype(BF16)]), "gather_w_in")
    w_proj_t = _to_proj_layout(w_in_t.reshape(D_PROJ, d))
    out_w = _gather_exchange([a[0].astype(BF16) for a in (w_attn_out, w_ssm_out, w_o)])
    conv_rows = conv_w[0].size // LANES
    conv_all = _gather_small(conv_w[0].reshape(conv_rows, LANES), False, "gather_conv_w")
    conv_w_all = conv_all[0::2].reshape(N_CHIPS, CONV_K, CONV_DIM // N_CHIPS).transpose(1, 0, 2).reshape(CONV_K, CONV_DIM)

    return w_proj_t, out_w, conv_w_all


def _local_step(x, loss_target, norm_w, w_proj_t, conv_w_all, conv_b, dt_bias, a_log, d_skip, ssm_norm_w,
                out_w, final_norm_w, grad_exchange=None):
    b, s, d = x.shape
    t = b * s
    g4, hg = SSD_GROUPS, HEADS_PER_GROUP
    dtb_g = _lane_pad(dt_bias.reshape(g4, hg)).reshape(g4, 1, LANES)
    alog_g = _lane_pad(a_log.reshape(g4, hg)).reshape(g4, 1, LANES)
    dskip_x = jnp.repeat(d_skip, HEAD_DIM, axis=1)
    fnw = final_norm_w.reshape(1, d)

    x2 = x.reshape(t, d)
    h = _rms_fwd(x2, norm_w)
    big_tm = min(t, 2048)
    if isinstance(out_w, _Exchange):
        proj, *out_w = _matmul(h, w_proj_t, tb=True, tm=big_tm, tn=1280, tk=1024, name="proj", exchange=out_w)
    else:
        proj = _matmul(h, w_proj_t, tb=True, tm=big_tm, tn=1280, tk=1024, name="proj")
    w_ao, w_so, w_oo = (w.reshape(-1, d) for w in out_w)
    proj3 = proj.reshape(b, s, NP)
    o3, yp3 = _attn_fwd(proj3)
    xact = _conv_fwd(proj3, conv_w_all, conv_b)
    dtr = proj3[:, :, DT0:DT0 + g4 * hg].reshape(b, s, g4, hg).transpose(0, 2, 1, 3)
    dtr_g = jnp.pad(dtr, ((0, 0), (0, 0), (0, 0), (0, LANES - hg)))
    y3, yn3, hst = _ssd_fwd(xact, proj3, dtr_g, dtb_g, alog_g, dskip_x, ssm_norm_w)
    yp = yp3.reshape(t, D_MODEL)
    yn = yn3.reshape(t, SSD_WIDTH)
    ya = _matmul(yp, w_ao, tm=1024, tn=1024, tk=1024, name="attn_out")
    ys = _matmul(yn, w_so, tm=1024, tn=1024, tk=2048, name="ssm_out")
    merged = _merge_fwd(proj, ya, ys)
    mo = _matmul(merged, w_oo, tm=1024, tn=1024, tk=1024, name="out_proj")
    dout, doutb, loss_part, d_fnw = _final_fwd_bwd(x2, mo, loss_target.reshape(t, d), fnw)

    dmerged = _matmul(doutb, w_oo, tb=True, tm=1024, tn=1024, tk=1024, name="d_merged")
    g_wo = _matmul(merged, doutb, ta=True, tm=1024, tn=1024, tk=1024, name="g_w_o")
    dya, dys, dproj = _merge_bwd(dmerged, proj, ya, ys)
    dyp = _matmul(dya, w_ao, tb=True, tm=1024, tn=1024, tk=1024, name="d_attn_pre")
    g_wao = _matmul(yp, dya, ta=True, tm=1024, tn=1024, tk=1024, name="g_w_attn_out")
    dyn = _matmul(dys, w_so, tb=True, tm=1024, tn=2048, tk=1024, name="d_ssm_norm")
    g_wso = _matmul(yn, dys, ta=True, tm=1024, tn=1024, tk=1024, name="g_w_ssm_out")
    dproj3 = _attn_bwd(proj3, dyp.reshape(b, s, D_MODEL), o3, dproj.reshape(b, s, NP))
    (dxs, dbm, dcm, dproj3, ddtr_g, d_snw_g, d_alog_g, d_dtb_g, d_dsk_g) = _ssd_bwd(
        dyn.reshape(b, s, SSD_WIDTH), y3, xact, proj3, hst, dtr_g, dtb_g, alog_g, dskip_x, ssm_norm_w, dproj3)
    dproj3, g_cw_xs, g_cb_xs = _conv_bwd(dxs, proj3, conv_w_all, conv_b, 0, "conv_bwd_x", dproj3)
    dproj3, g_cw_bm, g_cb_bm = _conv_bwd(dbm, proj3, conv_w_all, conv_b, SSD_WIDTH, "conv_bwd_b", dproj3)
    dproj3, g_cw_cm, g_cb_cm = _conv_bwd(dcm, proj3, conv_w_all, conv_b, SSD_WIDTH + g4 * SSD_STATE, "conv_bwd_c", dproj3)
    ddt = ddtr_g[:, :, :, :hg].transpose(0, 2, 1, 3).reshape(b, s, g4 * hg).astype(BF16)
    ddt = jnp.pad(ddt, ((0, 0), (0, 0), (0, DT_PAD - g4 * hg)))
    dproj = lax.dynamic_update_slice(dproj3, ddt, (0, 0, DT0)).reshape(t, NP)
    exchanged = []
    if grad_exchange:
        g_wproj, *got = _matmul(dproj, h, ta=True, tm=1280, tn=1024, tk=1024, name="g_w_in",
                                exchange=grad_exchange([g_wao, g_wso, g_wo], "out"))
        exchanged += got
        dh, *got = _matmul(dproj, w_proj_t, tm=big_tm, tn=1024, tk=1280, name="d_h", exchange=grad_exchange([g_wproj], "in"))
        exchanged += got
    else:
        g_wproj = _matmul(dproj, h, ta=True, tm=1280, tn=1024, tk=1024, name="g_w_in")
        dh = _matmul(dproj, w_proj_t, tm=big_tm, tn=1024, tk=1280, name="d_h")
    grad_x, d_nw = _rms_bwd(dh, x2, norm_w, dout)
    g_cw = jnp.concatenate([g_cw_xs, g_cw_bm, g_cw_cm], axis=1)
    g_cb = jnp.concatenate([g_cb_xs, g_cb_bm, g_cb_cm], axis=1)
    return (loss_part, grad_x, d_nw, g_wproj, g_cw, g_cb, d_dtb_g, d_alog_g, d_dsk_g, d_snw_g, g_wao, g_wso, g_wo, d_fnw,
            exchanged)


def kernel(x, norm_w, w_in, conv_w, conv_b, dt_bias, a_log, d_skip, ssm_norm_w, w_attn_out, w_ssm_out, w_o, final_norm_w, loss_target, m_norm_w, m_w_in, m_conv_w, m_conv_b, m_dt_bias, m_a_log, m_d_skip, m_ssm_norm_w, m_w_attn_out, m_w_ssm_out, m_w_o, m_final_norm_w, v_norm_w, v_w_in, v_conv_w, v_conv_b, v_dt_bias, v_a_log, v_d_skip, v_ssm_norm_w, v_w_attn_out, v_w_ssm_out, v_w_o, v_final_norm_w):
    b, s, d = x.shape
    core = lax.axis_index("c")
    g4, hg = SSD_GROUPS, HEADS_PER_GROUP
    shard_cols = w_in.shape[2]
    w_proj_t, out_w, conv_w_all = _gather_all(w_in, w_attn_out, w_ssm_out, w_o, conv_w)
    chip = 2 * lax.axis_index("x") + lax.axis_index("y")
    place = jnp.stack([core, chip]).astype(jnp.int32)
    chip_sums = []

    def grad_exchange(grads, which):
        if which == "in":
            slabs = _from_proj_layout(grads[0]).reshape(N_CHIPS, shard_cols, d)
        else:
            slabs = jnp.concatenate([g.reshape(N_CHIPS, -1, d) for g in grads], axis=1)
        from_sibling, = _swap_halves([slabs], "grad_swap_halves_" + which)
        chip_sums.append(_add_halves(slabs, from_sibling, place, "grad_add_halves_" + which))
        return _all_to_all_exchange(chip_sums[-1:])

    (loss_part, grad_x, d_nw, _, g_cw, g_cb, d_dtb_g, d_alog_g, d_dsk_g, d_snw_g, _, _, _, d_fnw, from_chips) = _local_step(
        x, loss_target, norm_w, w_proj_t, conv_w_all, conv_b, dt_bias, a_log, d_skip, ssm_norm_w, out_w, final_norm_w,
        grad_exchange)
    wholes = [_sum_chips(o, r, place, "grad_sum_chips_%d" % i) for i, (o, r) in enumerate(zip(chip_sums, from_chips))]
    g_out, g_w_in = _join_halves(wholes)

    small = jnp.concatenate([
        loss_part, d_nw, g_cb, _lane_pad(d_dtb_g[:, 0, :hg].reshape(1, -1)), _lane_pad(d_alog_g[:, 0, :hg].reshape(1, -1)),
        _lane_pad(d_dsk_g[:, 0, :hg].reshape(1, -1)),
        d_snw_g.reshape(1, -1), d_fnw, g_cw.reshape(1, -1)], axis=1)
    small_rows = small.shape[1] // LANES
    reduced = _gather_small(_pad_rows(small.reshape(small_rows, LANES), -(-small_rows // 8) * 8), True, "reduce_small")
    flat = reduced.reshape(-1)

    def take(start, n):
        return flat[start:start + n].reshape(1, n)

    loss = flat[0]
    pos = LANES
    g_norm_w = take(pos, d); pos += d
    g_conv_b = take(pos, CONV_DIM); pos += CONV_DIM
    g_dt_bias = take(pos, g4 * hg); pos += LANES
    g_a_log = take(pos, g4 * hg); pos += LANES
    g_d_skip = take(pos, g4 * hg); pos += LANES
    g_ssm_norm_w = take(pos, SSD_WIDTH); pos += SSD_WIDTH
    g_final_norm_w = take(pos, d); pos += d
    conv_cols = CONV_DIM // N_CHIPS
    g_conv_w = lax.dynamic_slice_in_dim(flat[pos:pos + CONV_K * CONV_DIM].reshape(CONV_K, CONV_DIM), chip * conv_cols, conv_cols, axis=1)

    rows_ao, rows_so = D_MODEL // N_CHIPS, SSD_WIDTH // N_CHIPS
    g_w_attn_out = g_out[:rows_ao]
    g_w_ssm_out = g_out[rows_ao:rows_ao + rows_so]
    g_w_o = g_out[rows_ao + rows_so:]

    names = ["norm_w", "w_in", "conv_w", "conv_b", "dt_bias", "a_log", "d_skip", "ssm_norm_w",
             "w_attn_out", "w_ssm_out", "w_o", "final_norm_w"]
    weights = [norm_w, w_in, conv_w, conv_b, dt_bias, a_log, d_skip, ssm_norm_w, w_attn_out, w_ssm_out, w_o, final_norm_w]
    grads = [g_norm_w, g_w_in, g_conv_w, g_conv_b, g_dt_bias, g_a_log, g_d_skip, g_ssm_norm_w,
             g_w_attn_out, g_w_ssm_out, g_w_o, g_final_norm_w]
    ms = [m_norm_w, m_w_in, m_conv_w, m_conv_b, m_dt_bias, m_a_log, m_d_skip, m_ssm_norm_w,
          m_w_attn_out, m_w_ssm_out, m_w_o, m_final_norm_w]
    vs = [v_norm_w, v_w_in, v_conv_w, v_conv_b, v_dt_bias, v_a_log, v_d_skip, v_ssm_norm_w,
          v_w_attn_out, v_w_ssm_out, v_w_o, v_final_norm_w]
    out_g, out_d, out_m, out_v = [], [], [], []
    for name, w, g, m, v in zip(names, weights, grads, ms, vs):
        if name == "w_in":
            to2, back = (lambda a: a[0].T), (lambda a: a.T.reshape(w.shape))
        else:
            to2, back = (lambda a: a.reshape(g.shape)), (lambda a: a.reshape(w.shape))
        dlt, nm, nv = _adamw(to2(w), g, to2(m), to2(v), "adamw_" + name)
        out_g.append(back(g))
        out_d.append(back(dlt))
        out_m.append(back(nm))
        out_v.append(back(nv))

    return (loss, grad_x.reshape(b, s, d), *out_g, *out_d, *out_m, *out_v)
```

```python
import jax
import jax.numpy as jnp
from jax import lax
from jax.experimental import pallas as pl
from jax.experimental.pallas import tpu as pltpu

F32 = jnp.float32
BF16 = jnp.bfloat16
MESH = pl.DeviceIdType.MESH

D_MODEL = 1024
SB_HEADS = 16
HEAD_DIM = 64
SSD_WIDTH = 2048
SSD_GROUPS = 4
GROUP_WIDTH = SSD_WIDTH // SSD_GROUPS
HEADS_PER_GROUP = 8
SSD_STATE = 128
CHUNK = 128
CONV_K = 4
CONV_DIM = 3072
D_PROJ = 11296
EPS = 1e-6
ADAM_LR, ADAM_B1, ADAM_B2, ADAM_EPS, ADAM_WD, ADAM_STEP = 0.001, 0.9, 0.999, 1e-08, 0.01, 10

LANES = 128
HP_WIDTH = 4 * LANES
ZS0, GATE0, XBC0, DT0 = 4096, 6144, 8192, 11264
DT_PAD = 256
NP = DT0 + DT_PAD
N_CHIPS = 4
VMEM_LIMIT = 56 * 1024 * 1024


N_HP = SB_HEADS // 2
W_ZS0, W_XBC0, W_DT0, W_GATE0 = 4096, 6144, 9216, 9248


def _to_proj_layout(wt):
    d = wt.shape[1]
    pairs = wt[:W_ZS0].reshape(4, N_HP, LANES, d).transpose(1, 0, 2, 3).reshape(W_ZS0, d)
    return jnp.concatenate([pairs, wt[W_ZS0:W_XBC0], wt[W_GATE0:], wt[W_XBC0:W_DT0], wt[W_DT0:W_GATE0],
                            jnp.zeros((NP - D_PROJ, d), wt.dtype)], axis=0)


def _from_proj_layout(gt):
    d = gt.shape[1]
    qkvz = gt[:ZS0].reshape(N_HP, 4, LANES, d).transpose(1, 0, 2, 3).reshape(ZS0, d)
    return jnp.concatenate([qkvz, gt[ZS0:GATE0], gt[XBC0:DT0], gt[DT0:DT0 + W_GATE0 - W_DT0], gt[GATE0:XBC0]], axis=0)


def _cparams(*sem):
    return pltpu.CompilerParams(dimension_semantics=sem or None, vmem_limit_bytes=VMEM_LIMIT)


def _sigmoid(z):
    return 1.0 / (1.0 + jnp.exp(-z))


def _dot(a, b, dims, precision=None):
    return lax.dot_general(a, b, (dims, ((), ())), preferred_element_type=F32, precision=precision)


NN = ((1,), (0,))
NT = ((1,), (1,))
TN = ((0,), (0,))


def _matmul(a, b, *, ta=False, tb=False, out_dtype=F32, tm, tn, tk, name, exchange=None):
    m, k = (a.shape[1], a.shape[0]) if ta else a.shape
    n = b.shape[0] if tb else b.shape[1]
    assert m % tm == 0 and n % tn == 0 and k % tk == 0, (name, m, n, k)
    grid = (m // tm, n // tn, k // tk)
    nk = grid[2]
    use_scratch = out_dtype != F32
    dims = ((0,) if ta else (1,), (1,) if tb else (0,))
    n_in = len(exchange.inputs) if exchange else 0
    n_out = len(exchange.out_shapes) if exchange else 0

    def kern(a_ref, b_ref, *rest):
        x_in, o_ref, x_out, scratch = rest[:n_in], rest[n_in], rest[n_in + 1:n_in + 1 + n_out], rest[n_in + 1 + n_out:]
        acc = scratch[0] if use_scratch else o_ref
        step = [pl.program_id(ax) for ax in range(3)]
        if exchange:
            sems = scratch[1:] if use_scratch else scratch

            @pl.when(jnp.logical_and(jnp.logical_and(step[0] == 0, step[1] == 0), step[2] == 0))
            def _():
                exchange.start(x_in, x_out, sems)

        @pl.when(step[2] == 0)
        def _():
            acc[...] = jnp.zeros_like(acc)

        acc[...] += _dot(a_ref[...], b_ref[...], dims)
        if use_scratch:
            @pl.when(step[2] == nk - 1)
            def _():
                o_ref[...] = acc[...].astype(out_dtype)
        if exchange:
            @pl.when(jnp.logical_and(jnp.logical_and(step[0] == grid[0] - 1, step[1] == grid[1] - 1), step[2] == nk - 1))
            def _():
                exchange.finish(x_in, x_out, sems)

    a_spec = pl.BlockSpec((tk, tm), lambda i, j, q: (q, i)) if ta else pl.BlockSpec((tm, tk), lambda i, j, q: (i, q))
    b_spec = pl.BlockSpec((tn, tk), lambda i, j, q: (j, q)) if tb else pl.BlockSpec((tk, tn), lambda i, j, q: (q, j))
    out = pl.pallas_call(
        kern, name=name,
        out_shape=[jax.ShapeDtypeStruct((m, n), out_dtype)] + (list(exchange.out_shapes) if exchange else []),
        grid=grid,
        in_specs=[a_spec, b_spec] + [ANY] * n_in,
        out_specs=[pl.BlockSpec((tm, tn), lambda i, j, q: (i, j))] + [ANY] * n_out,
        scratch_shapes=([pltpu.VMEM((tm, tn), F32)] if use_scratch else []) + (list(exchange.sems) if exchange else []),
        compiler_params=_cparams("arbitrary", "arbitrary", "arbitrary") if exchange else _cparams("parallel", "parallel", "arbitrary"),
    )(a, b, *(exchange.inputs if exchange else []))
    return out if exchange else out[0]


ROWS = 512


def _rms_fwd(x2, w, exchange=None):
    t, d = x2.shape
    steps = t // ROWS
    n_in = len(exchange.inputs) if exchange else 0
    n_out = len(exchange.out_shapes) if exchange else 0

    def kern(x_ref, w_ref, *rest):
        x_in, h_ref, x_out, sems = rest[:n_in], rest[n_in], rest[n_in + 1:n_in + 1 + n_out], rest[n_in + 1 + n_out:]
        if exchange:
            @pl.when(pl.program_id(0) == 0)
            def _():
                exchange.start(x_in, x_out, sems)

        x = x_ref[...]
        r = lax.rsqrt(jnp.mean(x * x, axis=-1, keepdims=True) + EPS)
        h_ref[...] = (x * r * w_ref[...]).astype(BF16)
        if exchange:
            @pl.when(pl.program_id(0) == steps - 1)
            def _():
                exchange.finish(x_in, x_out, sems)

    out = pl.pallas_call(
        kern, name="rms_fwd",
        out_shape=[jax.ShapeDtypeStruct((t, d), BF16)] + (list(exchange.out_shapes) if exchange else []),
        grid=(steps,),
        in_specs=[pl.BlockSpec((ROWS, d), lambda i: (i, 0)), pl.BlockSpec((1, d), lambda i: (0, 0))] + [ANY] * n_in,
        out_specs=[pl.BlockSpec((ROWS, d), lambda i: (i, 0))] + [ANY] * n_out,
        scratch_shapes=list(exchange.sems) if exchange else [],
        compiler_params=_cparams("arbitrary" if exchange else "parallel"),
    )(x2, w, *(exchange.inputs if exchange else []))
    return out if exchange else out[0]


def _rms_bwd(dh, x2, w, dout):
    t, d = x2.shape

    def kern(dh_ref, x_ref, w_ref, dout_ref, gx_ref, dw_ref):
        @pl.when(pl.program_id(0) == 0)
        def _():
            dw_ref[...] = jnp.zeros_like(dw_ref)

        x = x_ref[...]
        r = lax.rsqrt(jnp.mean(x * x, axis=-1, keepdims=True) + EPS)
        xh = x * r
        g = dh_ref[...]
        dw_ref[...] += jnp.sum(g * xh, axis=0, keepdims=True)
        gw = g * w_ref[...]
        gx_ref[...] = dout_ref[...] + r * (gw - xh * jnp.mean(gw * xh, axis=-1, keepdims=True))

    row = pl.BlockSpec((ROWS, d), lambda i: (i, 0))
    vec = pl.BlockSpec((1, d), lambda i: (0, 0))
    return pl.pallas_call(
        kern, name="rms_bwd",
        out_shape=(jax.ShapeDtypeStruct((t, d), F32), jax.ShapeDtypeStruct((1, d), F32)),
        grid=(t // ROWS,),
        in_specs=[row, row, vec, row],
        out_specs=(row, vec),
        compiler_params=_cparams("arbitrary"),
    )(dh, x2, w, dout)


def _final_fwd_bwd(x2, mo, target, w):
    t, d = x2.shape

    def kern(x_ref, mo_ref, t_ref, w_ref, dout_ref, doutb_ref, loss_ref, dw_ref):
        @pl.when(pl.program_id(0) == 0)
        def _():
            loss_ref[...] = jnp.zeros_like(loss_ref)
            dw_ref[...] = jnp.zeros_like(dw_ref)

        u = x_ref[...] + mo_ref[...]
        r = lax.rsqrt(jnp.mean(u * u, axis=-1, keepdims=True) + EPS)
        uh = u * r
        wv = w_ref[...]
        err = uh * wv - t_ref[...]
        loss_ref[...] += (0.5 / d) * jnp.sum(err * err)
        dy = err * (1.0 / d)
        dw_ref[...] += jnp.sum(dy * uh, axis=0, keepdims=True)
        gw = dy * wv
        du = r * (gw - uh * jnp.mean(gw * uh, axis=-1, keepdims=True))
        dout_ref[...] = du
        doutb_ref[...] = du.astype(BF16)

    row = pl.BlockSpec((ROWS, d), lambda i: (i, 0))
    vec = pl.BlockSpec((1, d), lambda i: (0, 0))
    return pl.pallas_call(
        kern, name="final_fwd_bwd",
        out_shape=(jax.ShapeDtypeStruct((t, d), F32), jax.ShapeDtypeStruct((t, d), BF16),
                   jax.ShapeDtypeStruct((1, LANES), F32), jax.ShapeDtypeStruct((1, d), F32)),
        grid=(t // ROWS,),
        in_specs=[row, row, row, vec],
        out_specs=(row, row, pl.BlockSpec((1, LANES), lambda i: (0, 0)), vec),
        compiler_params=_cparams("arbitrary"),
    )(x2, mo, target, w)


def _merge_fwd(proj2, ya, ys):
    t = ya.shape[0]
    gblk = GATE0 // D_MODEL

    def kern(ga_ref, gs_ref, ya_ref, ys_ref, o_ref):
        o_ref[...] = (_sigmoid(ga_ref[...]) * ya_ref[...] + _sigmoid(gs_ref[...]) * ys_ref[...]).astype(BF16)

    row = pl.BlockSpec((ROWS, D_MODEL), lambda i: (i, 0))
    return pl.pallas_call(
        kern, name="merge_fwd",
        out_shape=jax.ShapeDtypeStruct((t, D_MODEL), BF16),
        grid=(t // ROWS,),
        in_specs=[pl.BlockSpec((ROWS, D_MODEL), lambda i: (i, gblk)),
                  pl.BlockSpec((ROWS, D_MODEL), lambda i: (i, gblk + 1)), row, row],
        out_specs=row,
        compiler_params=_cparams("parallel"),
    )(proj2, proj2, ya, ys)


def _merge_bwd(dm, proj2, ya, ys):
    t = ya.shape[0]
    gblk = GATE0 // D_MODEL

    def kern(dm_ref, ga_ref, gs_ref, ya_ref, ys_ref, dya_ref, dys_ref, dg_ref):
        g = dm_ref[...]
        sa = _sigmoid(ga_ref[...])
        ss = _sigmoid(gs_ref[...])
        dya_ref[...] = (g * sa).astype(BF16)
        dys_ref[...] = (g * ss).astype(BF16)
        dg_ref[:, :D_MODEL] = (g * ya_ref[...] * sa * (1.0 - sa)).astype(BF16)
        dg_ref[:, D_MODEL:] = (g * ys_ref[...] * ss * (1.0 - ss)).astype(BF16)

    row = pl.BlockSpec((ROWS, D_MODEL), lambda i: (i, 0))
    return pl.pallas_call(
        kern, name="merge_bwd",
        out_shape=(jax.ShapeDtypeStruct((t, D_MODEL), BF16), jax.ShapeDtypeStruct((t, D_MODEL), BF16),
                   jax.ShapeDtypeStruct((t, NP), BF16)),
        grid=(t // ROWS,),
        in_specs=[row, pl.BlockSpec((ROWS, D_MODEL), lambda i: (i, gblk)),
                  pl.BlockSpec((ROWS, D_MODEL), lambda i: (i, gblk + 1)), row, row],
        out_specs=(row, row, pl.BlockSpec((ROWS, 2 * D_MODEL), lambda i: (i, GATE0 // (2 * D_MODEL)))),
        compiler_params=_cparams("parallel"),
    )(dm, proj2, proj2, ya, ys)


TQ = 256
TK = 256
assert TQ == TK
HEAD_LANES = (slice(0, HEAD_DIM), slice(HEAD_DIM, 2 * HEAD_DIM))


def _tri(pred):
    r = lax.broadcasted_iota(jnp.int32, (TK, TK), 0)
    c = lax.broadcasted_iota(jnp.int32, (TK, TK), 1)
    return pred(r, c).astype(BF16)


def _split_bf16(v):
    hi = v.astype(BF16)
    lo = (v - hi.astype(F32)).astype(BF16)
    return hi, lo


def _tri_dot(v, tri):
    hi, lo = _split_bf16(v)
    return _dot(hi, tri, NN) + _dot(lo, tri, NN)


def _sb_logs(z, mask):
    l1p = jnp.log(1.0 + jnp.exp(-jnp.abs(z)))
    lb = jnp.minimum(z, 0.0) - l1p
    lom = -jnp.maximum(z, 0.0) - l1p
    if mask is not None:
        lom = jnp.where(mask, lom, 0.0)
    return lb, lom


def _sb_weights(lb, later, carry_r, mask):
    a = jnp.exp(lb + (later + carry_r))
    if mask is not None:
        a = jnp.where(mask, a, 0.0)
    return a


DEAD = -104.0


def _while_alive(n, carry, step):
    def alive(cr):
        return jnp.max(jnp.maximum(cr[0][0], cr[1][0])) > DEAD

    def cond(state):
        jj, go, _ = state
        return jnp.logical_and(jj < n, go)

    def body(state):
        jj, _, cr = state
        cr = step(jj, cr)
        return jj + 1, alive(cr), cr

    return lax.while_loop(cond, body, (jnp.int32(0), alive(carry), carry))[2]


Q_LANES, K_LANES, V_LANES, ZA_LANES = (slice(i * LANES, (i + 1) * LANES) for i in range(4))


def _split_heads(dst, src, scale=None):
    for h, lanes in enumerate(HEAD_LANES):
        v = src[:, lanes]
        dst[h] = (v if scale is None else v * scale).astype(BF16)


def _attn_fwd(proj3):
    b, s, _ = proj3.shape
    nq = s // TQ
    scale = HEAD_DIM ** -0.5

    def kern(x_ref, o_ref, yp_ref, qs, ks, vs):
        _split_heads(qs, x_ref[0, :, Q_LANES], scale)
        _split_heads(ks, x_ref[0, :, K_LANES])
        _split_heads(vs, x_ref[0, :, V_LANES])
        za_ref = x_ref.at[:, :, ZA_LANES]
        row = lax.broadcasted_iota(jnp.int32, (TQ, TK), 0)
        col = lax.broadcasted_iota(jnp.int32, (TQ, TK), 1)
        tri_gt = _tri(lambda j, sk: j > sk)

        def q_block(i, _):
            top = isinstance(i, int)
            r0 = i * TQ if top else pl.multiple_of(i * TQ, TQ)
            qh = [qs[h, pl.ds(r0, TQ), :] for h in range(2)]

            def k_blocks(blocks, carry):
                nb = range(len(blocks))
                kh = [[ks[h, pl.ds(c0, TK), :] for h in range(2)] for c0, _ in blocks]
                vh = [[vs[h, pl.ds(c0, TK), :] for h in range(2)] for c0, _ in blocks]
                z = [[_dot(qh[h], kh[bl][h], NT) for h in range(2)] for bl in nb]
                logs = [[_sb_logs(z[bl][h], blocks[bl][1]) for h in range(2)] for bl in nb]
                later = [[_tri_dot(logs[bl][h][1], tri_gt) for h in range(2)] for bl in nb]
                out = []
                for h in range(2):
                    carry_r, acc = carry[h]
                    for bl in nb:
                        lb, lom = logs[bl][h]
                        a = _sb_weights(lb, later[bl][h], carry_r, blocks[bl][1])
                        acc = acc + _dot(a.astype(BF16), vh[bl][h], NN)
                        carry_r = carry_r + (later[bl][h][:, 0:1] + lom[:, 0:1])
                    out.append((carry_r, acc))
                return tuple(out)

            start = (jnp.zeros((TQ, 1), F32), jnp.zeros((TQ, HEAD_DIM), F32))
            diag = (r0, col < row)
            if top:
                carry = k_blocks([diag], (start, start))
            else:
                carry = k_blocks([diag, (pl.multiple_of(r0 - TK, TK), None)], (start, start))
                carry = _while_alive(i - 1, carry, lambda jj, cr: k_blocks([(pl.multiple_of((i - 2 - jj) * TK, TK), None)], cr))
            for (_, acc), lanes in zip(carry, HEAD_LANES):
                o_ref[0, pl.ds(r0, TQ), lanes] = acc
                za = za_ref[0, pl.ds(r0, TQ), lanes]
                yp_ref[0, pl.ds(r0, TQ), lanes] = (acc * (za * _sigmoid(za))).astype(BF16)
            return 0

        q_block(0, 0)
        lax.fori_loop(1, nq, q_block, 0)

    out_spec = pl.BlockSpec((1, s, LANES), lambda bi, hp: (bi, 0, hp))
    return pl.pallas_call(
        kern, name="attn_fwd",
        out_shape=(jax.ShapeDtypeStruct((b, s, D_MODEL), F32), jax.ShapeDtypeStruct((b, s, D_MODEL), BF16)),
        grid=(b, SB_HEADS // 2),
        in_specs=[pl.BlockSpec((1, s, HP_WIDTH), lambda bi, hp: (bi, 0, hp))],
        out_specs=(out_spec, out_spec),
        scratch_shapes=[pltpu.VMEM((2, s, HEAD_DIM), BF16)] * 3,
        compiler_params=_cparams("parallel", "parallel"),
    )(proj3)


def _attn_bwd(proj3, dyp3, o3, dproj3):
    b, s, _ = proj3.shape
    nq = s // TQ
    scale = HEAD_DIM ** -0.5

    def kern(x_ref, dyp_ref, o_ref, _, d_ref, qs, ks, vs, dos, dk_acc, dv_acc):
        _split_heads(qs, x_ref[0, :, Q_LANES], scale)
        _split_heads(ks, x_ref[0, :, K_LANES])
        _split_heads(vs, x_ref[0, :, V_LANES])
        dq_ref, dk_ref, dv_ref = (d_ref.at[:, :, lanes] for lanes in (Q_LANES, K_LANES, V_LANES))
        za = x_ref[0, :, ZA_LANES]
        sg = _sigmoid(za)
        dyp = dyp_ref[0]
        _split_heads(dos, dyp * (za * sg))
        d_ref[0, :, ZA_LANES] = (dyp * o_ref[0] * (sg * (1.0 + za * (1.0 - sg)))).astype(BF16)
        dk_acc[...] = jnp.zeros_like(dk_acc)
        dv_acc[...] = jnp.zeros_like(dv_acc)
        row = lax.broadcasted_iota(jnp.int32, (TQ, TK), 0)
        col = lax.broadcasted_iota(jnp.int32, (TQ, TK), 1)
        tri_gt = _tri(lambda j, sk: j > sk)
        tri_ge = _tri(lambda j, sk: j >= sk)

        def q_block(i, _):
            top = isinstance(i, int)
            r0 = i * TQ if top else pl.multiple_of(i * TQ, TQ)
            qh = [qs[h, pl.ds(r0, TQ), :] for h in range(2)]
            doh = [dos[h, pl.ds(r0, TQ), :] for h in range(2)]
            totals = [jnp.sum(doh[h].astype(F32) * o_ref[0, pl.ds(r0, TQ), lanes], axis=1, keepdims=True)
                      for h, lanes in enumerate(HEAD_LANES)]

            def k_blocks(blocks, carry):
                nb = range(len(blocks))
                kh = [[ks[h, pl.ds(c0, TK), :] for h in range(2)] for c0, _ in blocks]
                vh = [[vs[h, pl.ds(c0, TK), :] for h in range(2)] for c0, _ in blocks]
                z = [[_dot(qh[h], kh[bl][h], NT) for h in range(2)] for bl in nb]
                da = [[_dot(doh[h], vh[bl][h], NT) for h in range(2)] for bl in nb]
                logs = [[_sb_logs(z[bl][h], blocks[bl][1]) for h in range(2)] for bl in nb]
                later = [[_tri_dot(logs[bl][h][1], tri_gt) for h in range(2)] for bl in nb]
                ab, g, suffix = ([[None, None] for _ in nb] for _ in range(3))
                for h in range(2):
                    cr = carry[h][0]
                    for bl in nb:
                        a = _sb_weights(logs[bl][h][0], later[bl][h], cr, blocks[bl][1])
                        ab[bl][h] = a.astype(BF16)
                        g[bl][h] = da[bl][h] * ab[bl][h].astype(F32)
                        suffix[bl][h] = _tri_dot(g[bl][h], tri_ge)
                        cr = cr + (later[bl][h][:, 0:1] + logs[bl][h][1][:, 0:1])
                out = []
                for h in range(2):
                    _, carry_g, dq = carry[h]
                    cr = carry[h][0]
                    for bl in nb:
                        c0, mask = blocks[bl]
                        lb, lom = logs[bl][h]
                        dz = g[bl][h] - (g[bl][h] + (totals[h] - carry_g) - suffix[bl][h]) * jnp.exp(lb)
                        if mask is not None:
                            dz = jnp.where(mask, dz, 0.0)
                        dzb = dz.astype(BF16)
                        dk_acc[h, pl.ds(c0, TK), :] += _dot(dzb, qh[h], TN)
                        dv_acc[h, pl.ds(c0, TK), :] += _dot(ab[bl][h], doh[h], TN)
                        dq = dq + _dot(dzb, kh[bl][h], NN)
                        carry_g = carry_g + suffix[bl][h][:, 0:1]
                        cr = cr + (later[bl][h][:, 0:1] + lom[:, 0:1])
                    out.append((cr, carry_g, dq))
                return tuple(out)

            def k_block(c0, carry, mask):
                kh = [ks[h, pl.ds(c0, TK), :] for h in range(2)]
                vh = [vs[h, pl.ds(c0, TK), :] for h in range(2)]
                z = [_dot(qh[h], kh[h], NT) for h in range(2)]
                da = [_dot(doh[h], vh[h], NT) for h in range(2)]
                logs, later = [], []
                for h in range(2):
                    logs.append(_sb_logs(z[h], mask))
                    later.append(_tri_dot(logs[h][1], tri_gt))
                ab, g, suffix = [], [], []
                for h in range(2):
                    a = _sb_weights(logs[h][0], later[h], carry[h][0], mask)
                    ab.append(a.astype(BF16))
                    g.append(da[h] * ab[h].astype(F32))
                    suffix.append(_tri_dot(g[h], tri_ge))
                out = []
                for h in range(2):
                    carry_r, carry_g, dq = carry[h]
                    lb, lom = logs[h]
                    dz = g[h] - (g[h] + (totals[h] - carry_g) - suffix[h]) * jnp.exp(lb)
                    if mask is not None:
                        dz = jnp.where(mask, dz, 0.0)
                    dzb = dz.astype(BF16)
                    dk_acc[h, pl.ds(c0, TK), :] += _dot(dzb, qh[h], TN)
                    dv_acc[h, pl.ds(c0, TK), :] += _dot(ab[h], doh[h], TN)
                    out.append((carry_r + (later[h][:, 0:1] + lom[:, 0:1]), carry_g + suffix[h][:, 0:1],
                                dq + _dot(dzb, kh[h], NN)))
                return tuple(out)

            zero = jnp.zeros((TQ, 1), F32)
            start = (zero, zero, jnp.zeros((TQ, HEAD_DIM), F32))
            diag = (r0, col < row)
            if top:
                carry = k_block(r0, (start, start), col < row)
            else:
                carry = k_blocks([diag, (pl.multiple_of(r0 - TK, TK), None)], (start, start))
                carry = _while_alive(i - 1, carry, lambda jj, cr: k_block(pl.multiple_of((i - 2 - jj) * TK, TK), cr, None))
            for (_, _, dq), lanes in zip(carry, HEAD_LANES):
                dq_ref[0, pl.ds(r0, TQ), lanes] = (dq * scale).astype(BF16)
            return 0

        q_block(0, 0)
        lax.fori_loop(1, nq, q_block, 0)

        for h, lanes in enumerate(HEAD_LANES):
            dk_ref[0, :, lanes] = dk_acc[h].astype(BF16)
            dv_ref[0, :, lanes] = dv_acc[h].astype(BF16)

    plain = pl.BlockSpec((1, s, LANES), lambda bi, hp: (bi, 0, hp))
    pair = pl.BlockSpec((1, s, HP_WIDTH), lambda bi, hp: (bi, 0, hp))
    return pl.pallas_call(
        kern, name="attn_bwd",
        out_shape=jax.ShapeDtypeStruct(dproj3.shape, dproj3.dtype),
        grid=(b, SB_HEADS // 2),
        in_specs=[pair, plain, plain, ANY],
        out_specs=pair,
        input_output_aliases={3: 0},
        scratch_shapes=[pltpu.VMEM((2, s, HEAD_DIM), BF16)] * 4 + [pltpu.VMEM((2, s, HEAD_DIM), F32)] * 2,
        compiler_params=_cparams("parallel", "parallel"),
    )(proj3, dyp3, o3, dproj3)


CONV_COLS = 256
HALO = 8


def _conv_pre(xp, w_ref, b_ref, r0):
    pre = b_ref[...] + w_ref[CONV_K - 1:CONV_K, :] * xp[pl.ds(HALO + r0, CHUNK), :]
    for kk in range(1, CONV_K):
        pre = pre + w_ref[CONV_K - 1 - kk:CONV_K - kk, :] * xp[pl.ds(HALO + r0 - kk, CHUNK), :]
    return pre


def _conv_fwd(proj3, conv_w, conv_b):
    b, s, _ = proj3.shape
    nc = s // CHUNK

    def kern(x_ref, w_ref, b_ref, o_ref, xp):
        xp[0:HALO, :] = jnp.zeros((HALO, CONV_COLS), F32)
        xp[HALO:, :] = x_ref[0]
        for ci in range(nc):
            pre = _conv_pre(xp, w_ref, b_ref, ci * CHUNK)
            o_ref[0, ci * CHUNK:(ci + 1) * CHUNK, :] = pre * _sigmoid(pre)

    return pl.pallas_call(
        kern, name="conv_fwd",
        out_shape=jax.ShapeDtypeStruct((b, s, CONV_DIM), F32),
        grid=(CONV_DIM // CONV_COLS, b),
        in_specs=[pl.BlockSpec((1, s, CONV_COLS), lambda j, bi: (bi, 0, XBC0 // CONV_COLS + j)),
                  pl.BlockSpec((CONV_K, CONV_COLS), lambda j, bi: (0, j)),
                  pl.BlockSpec((1, CONV_COLS), lambda j, bi: (0, j))],
        out_specs=pl.BlockSpec((1, s, CONV_COLS), lambda j, bi: (bi, 0, j)),
        scratch_shapes=[pltpu.VMEM((s + HALO, CONV_COLS), F32)],
        compiler_params=_cparams("parallel", "parallel"),
    )(proj3, conv_w, conv_b)


def _conv_bwd(dact, proj3, conv_w, conv_b, col0, name, dproj3):
    b, s, width = dact.shape
    nc = s // CHUNK
    j0 = col0 // CONV_COLS

    def kern(da_ref, x_ref, w_ref, b_ref, _, dx_ref, dw_ref, db_ref, xp, dp):
        @pl.when(pl.program_id(1) == 0)
        def _():
            dw_ref[...] = jnp.zeros_like(dw_ref)
            db_ref[...] = jnp.zeros_like(db_ref)

        xp[0:HALO, :] = jnp.zeros((HALO, CONV_COLS), F32)
        xp[HALO:, :] = x_ref[0]
        dp[s:, :] = jnp.zeros((HALO, CONV_COLS), F32)
        for ci in range(nc):
            r0 = ci * CHUNK
            pre = _conv_pre(xp, w_ref, b_ref, r0)
            sg = _sigmoid(pre)
            dpre = da_ref[0, r0:r0 + CHUNK, :] * (sg * (1.0 + pre * (1.0 - sg)))
            dp[r0:r0 + CHUNK, :] = dpre
            db_ref[...] += jnp.sum(dpre, axis=0, keepdims=True)
            for kk in range(CONV_K):
                tap = CONV_K - 1 - kk
                dw_ref[tap:tap + 1, :] += jnp.sum(dpre * xp[pl.ds(HALO + r0 - kk, CHUNK), :], axis=0, keepdims=True)
        for ci in range(nc):
            r0 = ci * CHUNK
            dx = w_ref[CONV_K - 1:CONV_K, :] * dp[pl.ds(r0, CHUNK), :]
            for kk in range(1, CONV_K):
                dx = dx + w_ref[CONV_K - 1 - kk:CONV_K - kk, :] * dp[pl.ds(r0 + kk, CHUNK), :]
            dx_ref[0, r0:r0 + CHUNK, :] = dx.astype(BF16)

    return pl.pallas_call(
        kern, name=name,
        out_shape=(jax.ShapeDtypeStruct(dproj3.shape, dproj3.dtype), jax.ShapeDtypeStruct((CONV_K, width), F32),
                   jax.ShapeDtypeStruct((1, width), F32)),
        grid=(width // CONV_COLS, b),
        in_specs=[pl.BlockSpec((1, s, CONV_COLS), lambda j, bi: (bi, 0, j)),
                  pl.BlockSpec((1, s, CONV_COLS), lambda j, bi: (bi, 0, XBC0 // CONV_COLS + j0 + j)),
                  pl.BlockSpec((CONV_K, CONV_COLS), lambda j, bi: (0, j0 + j)),
                  pl.BlockSpec((1, CONV_COLS), lambda j, bi: (0, j0 + j)), ANY],
        out_specs=(pl.BlockSpec((1, s, CONV_COLS), lambda j, bi: (bi, 0, XBC0 // CONV_COLS + j0 + j)),
                   pl.BlockSpec((CONV_K, CONV_COLS), lambda j, bi: (0, j)),
                   pl.BlockSpec((1, CONV_COLS), lambda j, bi: (0, j))),
        input_output_aliases={4: 0},
        scratch_shapes=[pltpu.VMEM((s + HALO, CONV_COLS), F32)] * 2,
        compiler_params=_cparams("parallel", "arbitrary"),
    )(dact, proj3, conv_w, conv_b, dproj3)


def _sel_dot(v, sel, left=False):
    hi = v.astype(BF16)
    rest = v - hi.astype(F32)
    mid = rest.astype(BF16)
    lo = (rest - mid.astype(F32)).astype(BF16)
    if left:
        return _dot(sel, hi, NN) + _dot(sel, mid, NN) + _dot(sel, lo, NN)
    return _dot(hi, sel, NN) + _dot(mid, sel, NN) + _dot(lo, sel, NN)


def _ssd_common(dtr_ref, dtb_ref, alog_ref):
    lane = lax.broadcasted_iota(jnp.int32, (CHUNK, LANES), 1)
    row = lax.broadcasted_iota(jnp.int32, (CHUNK, LANES), 0)
    head_lane = lane < HEADS_PER_GROUP
    pre = dtr_ref[0, 0] + dtb_ref[0]
    dt = jnp.where(head_lane, jnp.maximum(pre, 0.0) + jnp.log(1.0 + jnp.exp(-jnp.abs(pre))), 0.0)
    a = jnp.where(head_lane[0:1], -jnp.exp(alog_ref[0]), 0.0)
    tril = (row >= lane).astype(BF16)
    acs = _sel_dot(dt * a, tril, left=True)
    acs_t = acs.T
    er = lax.broadcasted_iota(jnp.int32, (LANES, GROUP_WIDTH), 0)
    ec = lax.broadcasted_iota(jnp.int32, (LANES, GROUP_WIDTH), 1)
    expand = ((ec // HEAD_DIM) == er).astype(BF16)
    tr = lax.broadcasted_iota(jnp.int32, (GROUP_WIDTH, LANES), 0)
    tc = lax.broadcasted_iota(jnp.int32, (GROUP_WIDTH, LANES), 1)
    reduce = ((tr // HEAD_DIM) == tc).astype(BF16)
    dt_x = _sel_dot(dt, expand)
    acs_x = _sel_dot(acs, expand)
    end_x = acs_x[CHUNK - 1:CHUNK, :]
    causal = row >= lane
    return dict(dt=dt, a=a, pre=pre, head_lane=head_lane, acs=acs, acs_t=acs_t, expand=expand, reduce=reduce,
                dt_x=dt_x, acs_x=acs_x, end_x=end_x, causal=causal, row=row, lane=lane)


def _ssd_decay(cm, h):
    seg = cm["acs"][:, h:h + 1] - cm["acs_t"][h:h + 1, :]
    return jnp.where(cm["causal"], jnp.exp(jnp.minimum(seg, 0.0)), 0.0)


def _ssd_fwd(xact, proj3, dtr_g, dtb_g, alog_g, dskip_x, snw):
    b, s, _ = xact.shape
    nc = s // CHUNK
    g4 = SSD_GROUPS

    def kern(xs_ref, bm_ref, cm_ref, zs_ref, dtr_ref, dtb_ref, alog_ref, dsk_ref, snw_ref,
             y_ref, yn_ref, hst_ref, h_sc):
        @pl.when(pl.program_id(2) == 0)
        def _():
            h_sc[...] = jnp.zeros_like(h_sc)

        cm = _ssd_common(dtr_ref, dtb_ref, alog_ref)
        x = xs_ref[0]
        bmb = bm_ref[0].astype(BF16)
        cmb = cm_ref[0].astype(BF16)
        h_in = h_sc[...]
        hst_ref[0, 0, 0] = h_in
        xdt = x * cm["dt_x"]
        xdtb = xdt.astype(BF16)
        cb = _dot(cmb, bmb, NT)
        y_off = _dot(cmb, h_in.astype(BF16), NN) * jnp.exp(cm["acs_x"])
        for h in range(HEADS_PER_GROUP):
            lanes = slice(h * HEAD_DIM, (h + 1) * HEAD_DIM)
            m = (cb * _ssd_decay(cm, h)).astype(BF16)
            y_ref[0, :, lanes] = _dot(m, xdtb[:, lanes], NN)
        y = y_ref[0] + y_off + x * dsk_ref[...]
        y_ref[0] = y
        w = (xdt * jnp.exp(cm["end_x"] - cm["acs_x"])).astype(BF16)
        h_sc[...] = h_in * jnp.exp(cm["end_x"]) + _dot(bmb, w, TN)
        zs = zs_ref[0]
        y2 = y * (zs * _sigmoid(zs))
        yn_ref[0] = (y2 * lax.rsqrt(jnp.mean(y2 * y2, axis=-1, keepdims=True) + EPS) * snw_ref[...]).astype(BF16)

    gw = GROUP_WIDTH
    small = pl.BlockSpec((1, 1, LANES), lambda gi, bi, ci: (gi, 0, 0))
    xblk = pl.BlockSpec((1, CHUNK, gw), lambda gi, bi, ci: (bi, ci, gi))
    return pl.pallas_call(
        kern, name="ssd_fwd",
        out_shape=(jax.ShapeDtypeStruct((b, s, SSD_WIDTH), F32), jax.ShapeDtypeStruct((b, s, SSD_WIDTH), BF16),
                   jax.ShapeDtypeStruct((b, nc, g4, SSD_STATE, gw), F32)),
        grid=(g4, b, nc),
        in_specs=[xblk,
                  pl.BlockSpec((1, CHUNK, LANES), lambda gi, bi, ci: (bi, ci, SSD_WIDTH // LANES + gi)),
                  pl.BlockSpec((1, CHUNK, LANES), lambda gi, bi, ci: (bi, ci, SSD_WIDTH // LANES + g4 + gi)),
                  pl.BlockSpec((1, CHUNK, gw), lambda gi, bi, ci: (bi, ci, ZS0 // gw + gi)),
                  pl.BlockSpec((1, 1, CHUNK, LANES), lambda gi, bi, ci: (bi, gi, ci, 0)),
                  small, small,
                  pl.BlockSpec((1, gw), lambda gi, bi, ci: (0, gi)),
                  pl.BlockSpec((1, gw), lambda gi, bi, ci: (0, gi))],
        out_specs=(xblk, xblk, pl.BlockSpec((1, 1, 1, SSD_STATE, gw), lambda gi, bi, ci: (bi, ci, gi, 0, 0))),
        scratch_shapes=[pltpu.VMEM((SSD_STATE, gw), F32)],
        compiler_params=_cparams("parallel", "parallel", "arbitrary"),
    )(xact, xact, xact, proj3, dtr_g, dtb_g, alog_g, dskip_x, snw)


def _ssd_bwd(dyn3, y3, xact, proj3, hst, dtr_g, dtb_g, alog_g, dskip_x, snw, dproj3):
    b, s, _ = xact.shape
    nc = s // CHUNK
    g4 = SSD_GROUPS
    gw = GROUP_WIDTH

    def kern(dyn_ref, y_ref, xs_ref, bm_ref, cm_ref, zs_ref, hst_ref, dtr_ref, dtb_ref, alog_ref, dsk_ref, snw_ref, _,
             dxs_ref, dbm_ref, dcm_ref, dzs_ref, ddtr_ref, dsnw_ref, dalog_ref, ddtb_ref, ddsk_ref, dh_sc):
        first = jnp.logical_and(pl.program_id(1) == 0, pl.program_id(2) == 0)

        @pl.when(first)
        def _():
            dsnw_ref[...] = jnp.zeros_like(dsnw_ref)
            dalog_ref[...] = jnp.zeros_like(dalog_ref)
            ddtb_ref[...] = jnp.zeros_like(ddtb_ref)
            ddsk_ref[...] = jnp.zeros_like(ddsk_ref)

        @pl.when(pl.program_id(2) == 0)
        def _():
            dh_sc[...] = jnp.zeros_like(dh_sc)

        cm = _ssd_common(dtr_ref, dtb_ref, alog_ref)
        row, lane = cm["row"], cm["lane"]
        y = y_ref[0]
        zs = zs_ref[0]
        sg = _sigmoid(zs)
        silu = zs * sg
        y2 = y * silu
        rstd = lax.rsqrt(jnp.mean(y2 * y2, axis=-1, keepdims=True) + EPS)
        y2h = y2 * rstd
        dyn = dyn_ref[0]
        dsnw_ref[0] += jnp.sum(dyn * y2h, axis=0, keepdims=True)
        gwv = dyn * snw_ref[...]
        dy2 = rstd * (gwv - y2h * jnp.mean(gwv * y2h, axis=-1, keepdims=True))
        dzs_ref[0] = (dy2 * y * (sg * (1.0 + zs * (1.0 - sg)))).astype(BF16)
        dy = dy2 * silu
        dyb = dy.astype(BF16)

        x = xs_ref[0]
        bmb = bm_ref[0].astype(BF16)
        cmb = cm_ref[0].astype(BF16)
        h_in = hst_ref[0, 0, 0]
        h_inb = h_in.astype(BF16)
        d_hn = dh_sc[...]
        d_hnb = d_hn.astype(BF16)
        xdt = x * cm["dt_x"]
        xdtb = xdt.astype(BF16)
        eacs = jnp.exp(cm["acs_x"])
        dte = jnp.exp(cm["end_x"] - cm["acs_x"])
        wb = (xdt * dte).astype(BF16)

        dsk_lanes = jnp.broadcast_to(jnp.sum(dy * x, axis=0, keepdims=True), (8, gw))
        ddsk_ref[0] += _sel_dot(dsk_lanes, cm["reduce"])[0:1, :]
        dyo = dy * eacs
        dyob = dyo.astype(BF16)
        dacs_x = dyo * _dot(cmb, h_inb, NN)
        dcm = _dot(dyob, h_inb, NT)
        dh_in = _dot(cmb, dyob, TN)
        dw = _dot(bmb, d_hnb, NN)
        dbm = _dot(wb, d_hnb, NT)
        dxdt = dw * dte
        e_l = dw * xdt * dte
        dacs_x = dacs_x - e_l
        dend_x = jnp.sum(e_l, axis=0, keepdims=True)
        chunk_decay = jnp.exp(cm["end_x"])
        dh_sc[...] = d_hn * chunk_decay + dh_in
        dend_x = dend_x + jnp.sum(d_hn * h_in, axis=0, keepdims=True) * chunk_decay
        last_row = lax.broadcasted_iota(jnp.int32, (CHUNK, gw), 0) == CHUNK - 1
        dacs_x = dacs_x + jnp.where(last_row, dend_x, 0.0)

        cb = _dot(cmb, bmb, NT)
        dcb = jnp.zeros((CHUNK, CHUNK), F32)
        dacs = jnp.zeros((CHUNK, LANES), F32)
        dacs_t = jnp.zeros((LANES, CHUNK), F32)
        for h in range(HEADS_PER_GROUP):
            lanes = slice(h * HEAD_DIM, (h + 1) * HEAD_DIM)
            decay = _ssd_decay(cm, h)
            m = cb * decay
            dm = _dot(dyb[:, lanes], xdtb[:, lanes], NT)
            dxs_ref[0, :, lanes] = _dot(m.astype(BF16), dyb[:, lanes], TN)
            dcb_h = dm * decay
            dcb = dcb + dcb_h
            n = dcb_h * cb
            dacs = dacs + jnp.where(lane == h, jnp.sum(n, axis=1, keepdims=True), 0.0)
            dacs_t = dacs_t + jnp.where(row == h, jnp.sum(n, axis=0, keepdims=True), 0.0)
        dcbb = dcb.astype(BF16)
        dcm_ref[0] = dcm + _dot(dcbb, bmb, NN)
        dbm_ref[0] = dbm + _dot(dcbb, cmb, TN)
        dxdt = dxdt + dxs_ref[0]
        dxs_ref[0] = dy * dsk_ref[...] + dxdt * cm["dt_x"]

        dacs = dacs - dacs_t.T + _sel_dot(dacs_x, cm["reduce"])
        ddt = _sel_dot(dxdt * x, cm["reduce"])
        triu = (row <= lane).astype(BF16)
        rc = _sel_dot(dacs, triu, left=True)
        ddt = ddt + cm["a"] * rc
        dalog_ref[0] += jnp.sum(cm["dt"] * rc, axis=0, keepdims=True) * cm["a"]
        ddtr = jnp.where(cm["head_lane"], ddt * _sigmoid(cm["pre"]), 0.0)
        ddtr_ref[0, 0] = ddtr
        ddtb_ref[0] += jnp.sum(ddtr, axis=0, keepdims=True)

    def rev(ci):
        return nc - 1 - ci

    small = pl.BlockSpec((1, 1, LANES), lambda gi, bi, ci: (gi, 0, 0))
    xblk = pl.BlockSpec((1, CHUNK, gw), lambda gi, bi, ci: (bi, rev(ci), gi))
    nblk = pl.BlockSpec((1, CHUNK, LANES), lambda gi, bi, ci: (bi, rev(ci), gi))
    gvec = pl.BlockSpec((1, gw), lambda gi, bi, ci: (0, gi))
    gacc = pl.BlockSpec((1, 1, gw), lambda gi, bi, ci: (gi, 0, 0))
    return pl.pallas_call(
        kern, name="ssd_bwd",
        out_shape=(jax.ShapeDtypeStruct((b, s, SSD_WIDTH), F32),
                   jax.ShapeDtypeStruct((b, s, g4 * SSD_STATE), F32),
                   jax.ShapeDtypeStruct((b, s, g4 * SSD_STATE), F32),
                   jax.ShapeDtypeStruct(dproj3.shape, dproj3.dtype),
                   jax.ShapeDtypeStruct((b, g4, s, LANES), F32),
                   jax.ShapeDtypeStruct((g4, 1, gw), F32),
                   jax.ShapeDtypeStruct((g4, 1, LANES), F32),
                   jax.ShapeDtypeStruct((g4, 1, LANES), F32),
                   jax.ShapeDtypeStruct((g4, 1, LANES), F32)),
        grid=(g4, b, nc),
        in_specs=[xblk, xblk, xblk,
                  pl.BlockSpec((1, CHUNK, LANES), lambda gi, bi, ci: (bi, rev(ci), SSD_WIDTH // LANES + gi)),
                  pl.BlockSpec((1, CHUNK, LANES), lambda gi, bi, ci: (bi, rev(ci), SSD_WIDTH // LANES + g4 + gi)),
                  pl.BlockSpec((1, CHUNK, gw), lambda gi, bi, ci: (bi, rev(ci), ZS0 // gw + gi)),
                  pl.BlockSpec((1, 1, 1, SSD_STATE, gw), lambda gi, bi, ci: (bi, rev(ci), gi, 0, 0)),
                  pl.BlockSpec((1, 1, CHUNK, LANES), lambda gi, bi, ci: (bi, gi, rev(ci), 0)),
                  small, small, gvec, gvec, ANY],
        out_specs=(xblk, nblk, nblk,
                   pl.BlockSpec((1, CHUNK, gw), lambda gi, bi, ci: (bi, rev(ci), ZS0 // gw + gi)),
                   pl.BlockSpec((1, 1, CHUNK, LANES), lambda gi, bi, ci: (bi, gi, rev(ci), 0)),
                   gacc, small, small, small),
        input_output_aliases={12: 3},
        scratch_shapes=[pltpu.VMEM((SSD_STATE, gw), F32)],
        compiler_params=_cparams("parallel", "arbitrary", "arbitrary"),
    )(dyn3, y3, xact, xact, xact, proj3, hst, dtr_g, dtb_g, alog_g, dskip_x, snw, dproj3)


def _adamw(w, g, m, v, name):
    r, c = w.shape
    tr = 128 if r % 128 == 0 else r
    tc = LANES if (tr == r and r > 128 and c % LANES == 0) else c

    def kern(w_ref, g_ref, m_ref, v_ref, d_ref, nm_ref, nv_ref):
        gv = g_ref[...]
        nm = ADAM_B1 * m_ref[...] + (1.0 - ADAM_B1) * gv
        nv = ADAM_B2 * v_ref[...] + (1.0 - ADAM_B2) * (gv * gv)
        m_hat = nm / (1.0 - ADAM_B1 ** ADAM_STEP)
        v_hat = nv / (1.0 - ADAM_B2 ** ADAM_STEP)
        d_ref[...] = -ADAM_LR * (m_hat / (jnp.sqrt(v_hat) + ADAM_EPS) + ADAM_WD * w_ref[...])
        nm_ref[...] = nm
        nv_ref[...] = nv

    blk = pl.BlockSpec((tr, tc), lambda i, j: (i, j))
    out = jax.ShapeDtypeStruct((r, c), F32)
    return pl.pallas_call(
        kern, name=name, out_shape=(out, out, out), grid=(r // tr, c // tc),
        in_specs=[blk] * 4, out_specs=(blk, blk, blk),
        compiler_params=_cparams("parallel", "parallel"),
    )(w, g, m, v)


ANY = pl.BlockSpec(memory_space=pl.ANY)


def _position():
    return lax.axis_index("x"), lax.axis_index("y"), lax.axis_index("c")


def _other_chips(x, y):
    return [(1 - x, y), (x, 1 - y), (1 - x, 1 - y)]


def _dma_sems(n):
    return [pltpu.SemaphoreType.DMA((n,)), pltpu.SemaphoreType.DMA((n,))]


class _Exchange:
    def __init__(self, inputs, out_shapes, sems, start, finish):
        self.inputs, self.out_shapes, self.sems, self.start, self.finish = inputs, out_shapes, sems, start, finish


def _gather_exchange(shards):
    n = len(shards)

    def copies(p_refs, out_refs, sems):
        send_sems, recv_sems = sems
        x, y, c = _position()
        me = 2 * x + y
        chips = _other_chips(x, y)

        def slab(a, chip, hf):
            half = shards[a].shape[1] // 2
            return out_refs[a].at[chip, :, pl.ds(hf * half, half)]

        def my_half(a):
            half = shards[a].shape[1] // 2
            return p_refs[a].at[:, pl.ds(c * half, half)]

        def over_ici(a, j, chip_from):
            px, py = chips[j]
            return pltpu.make_async_remote_copy(
                src_ref=my_half(a), dst_ref=slab(a, chip_from, c),
                send_sem=send_sems.at[3 * a + j], recv_sem=recv_sems.at[3 * a + j],
                device_id=(px, py, c), device_id_type=MESH)

        def to_sibling(a, j, hf):
            px, py = chips[j]
            return pltpu.make_async_remote_copy(
                src_ref=slab(a, 2 * px + py, hf), dst_ref=slab(a, 2 * px + py, hf),
                send_sem=send_sems.at[3 * (n + a) + j], recv_sem=recv_sems.at[3 * (n + a) + j],
                device_id=(x, y, 1 - c), device_id_type=MESH)

        own = [pltpu.make_async_remote_copy(
            src_ref=p_refs[a], dst_ref=out_refs[a].at[me], send_sem=send_sems.at[6 * n + a], recv_sem=recv_sems.at[6 * n + a],
            device_id=(x, y, 1 - c), device_id_type=MESH) for a in range(n)]
        first = [over_ici(a, j, me) for a in range(n) for j in range(3)]
        return chips, c, over_ici, to_sibling, first, own

    def start(p_refs, out_refs, sems):
        _, _, _, _, first, own = copies(p_refs, out_refs, sems)
        for cp in first + own:
            cp.start()

    def finish(p_refs, out_refs, sems):
        chips, c, over_ici, to_sibling, first, own = copies(p_refs, out_refs, sems)
        passed = []
        for a in range(n):
            for j, (px, py) in enumerate(chips):
                over_ici(a, j, 2 * px + py).wait_recv()
                passed.append(to_sibling(a, j, c))
                passed[-1].start()
        for a in range(n):
            for j in range(3):
                to_sibling(a, j, 1 - c).wait_recv()
        for cp in first + passed:
            cp.wait_send()
        for cp in own:
            cp.wait()

    return _Exchange(list(shards), [jax.ShapeDtypeStruct((N_CHIPS, *v.shape), v.dtype) for v in shards],
                     _dma_sems(7 * n), start, finish)


def _swap_halves(parts, name):
    n = len(parts)

    def body(*refs):
        v_refs, out_refs = refs[:n], refs[n:2 * n]
        send_sems, recv_sems = refs[2 * n:]
        x, y, c = _position()
        copies = []
        for a in range(n):
            half = parts[a].shape[2] // 2
            copies.append(pltpu.make_async_remote_copy(
                src_ref=v_refs[a].at[:, :, pl.ds((1 - c) * half, half)], dst_ref=out_refs[a],
                send_sem=send_sems.at[a], recv_sem=recv_sems.at[a], device_id=(x, y, 1 - c), device_id_type=MESH))
        for cp in copies:
            cp.start()
        for cp in copies:
            cp.wait()

    return pl.pallas_call(
        body, name=name,
        out_shape=[jax.ShapeDtypeStruct((v.shape[0], v.shape[1], v.shape[2] // 2), v.dtype) for v in parts],
        in_specs=[ANY] * n, out_specs=[ANY] * n,
        scratch_shapes=_dma_sems(n),
    )(*parts)


def _all_to_all_exchange(parts):
    n = len(parts)

    def sends(p_refs, out_refs, sems):
        send_sems, recv_sems = sems
        x, y, c = _position()
        return [pltpu.make_async_remote_copy(
            src_ref=p_refs[a].at[2 * px + py], dst_ref=out_refs[a].at[j],
            send_sem=send_sems.at[3 * a + j], recv_sem=recv_sems.at[3 * a + j],
            device_id=(px, py, c), device_id_type=MESH) for a in range(n) for j, (px, py) in enumerate(_other_chips(x, y))]

    def start(p_refs, out_refs, sems):
        for cp in sends(p_refs, out_refs, sems):
            cp.start()

    def finish(p_refs, out_refs, sems):
        for cp in sends(p_refs, out_refs, sems):
            cp.wait()

    return _Exchange(list(parts), [jax.ShapeDtypeStruct((N_CHIPS - 1, *v.shape[1:]), v.dtype) for v in parts],
                     _dma_sems(3 * n), start, finish)


def _join_halves(wholes):
    n = len(wholes)

    def body(*refs):
        out_refs = refs[n:2 * n]
        send_sems, recv_sems = refs[2 * n:]
        x, y, c = _position()
        copies = []
        for a in range(n):
            half = wholes[a].shape[1] // 2
            mine = out_refs[a].at[:, pl.ds(c * half, half)]
            copies.append(pltpu.make_async_remote_copy(
                src_ref=mine, dst_ref=mine, send_sem=send_sems.at[a], recv_sem=recv_sems.at[a],
                device_id=(x, y, 1 - c), device_id_type=MESH))
        for cp in copies:
            cp.start()
        for cp in copies:
            cp.wait()

    return pl.pallas_call(
        body, name="grad_join_halves",
        out_shape=[jax.ShapeDtypeStruct(v.shape, v.dtype) for v in wholes],
        in_specs=[ANY] * n, out_specs=[ANY] * n,
        input_output_aliases={a: a for a in range(n)},
        scratch_shapes=_dma_sems(n),
    )(*wholes)


STRIP = 256


def _add_halves(g, sw, place, name):
    n, rows, cols = g.shape
    nb = cols // 2 // STRIP

    def kern(p_ref, g_ref, s_ref, o_ref):
        o_ref[...] = (g_ref[...] + s_ref[...]).astype(BF16)

    blk = pl.BlockSpec((1, rows, STRIP), lambda j, i, p_ref: (j, 0, i))
    return pl.pallas_call(
        kern, name=name,
        out_shape=jax.ShapeDtypeStruct((n, rows, cols // 2), BF16),
        grid_spec=pltpu.PrefetchScalarGridSpec(
            num_scalar_prefetch=1, grid=(n, nb),
            in_specs=[pl.BlockSpec((1, rows, STRIP), lambda j, i, p_ref: (j, 0, p_ref[0] * nb + i)), blk],
            out_specs=blk),
        compiler_params=_cparams("parallel", "parallel"),
    )(place, g, sw)


def _sum_chips(own, rx, place, name):
    _, rows, half = rx.shape
    nb = half // STRIP

    def kern(p_ref, own_ref, r_ref, o_ref):
        total = own_ref[0].astype(F32)
        for j in range(N_CHIPS - 1):
            total = total + r_ref[j].astype(F32)
        o_ref[...] = total

    return pl.pallas_call(
        kern, name=name,
        out_shape=jax.ShapeDtypeStruct((rows, 2 * half), F32),
        grid_spec=pltpu.PrefetchScalarGridSpec(
            num_scalar_prefetch=1, grid=(nb,),
            in_specs=[pl.BlockSpec((1, rows, STRIP), lambda i, p_ref: (p_ref[1], 0, i)),
                      pl.BlockSpec((N_CHIPS - 1, rows, STRIP), lambda i, p_ref: (0, 0, i))],
            out_specs=pl.BlockSpec((rows, STRIP), lambda i, p_ref: (0, p_ref[0] * nb + i))),
        compiler_params=_cparams("parallel"),
    )(place, own, rx)


def _gather_small(v, reduce, name):
    rows = v.shape[0]

    def body(v_ref, out_ref, buf, send_sems, recv_sems):
        x, y, c = _position()
        me = 4 * x + 2 * y + c
        buf[me] = v_ref[...]
        peers = [(x ^ (k >> 2), y ^ ((k >> 1) & 1), c ^ (k & 1)) for k in range(1, 8)]
        copies = [pltpu.make_async_remote_copy(
            src_ref=v_ref, dst_ref=buf.at[me],
            send_sem=send_sems.at[k], recv_sem=recv_sems.at[k],
            device_id=peer, device_id_type=MESH) for k, peer in enumerate(peers)]
        for cp in copies:
            cp.start()
        for k, (px, py, pc) in enumerate(peers):
            pltpu.make_async_remote_copy(
                src_ref=v_ref, dst_ref=buf.at[4 * px + 2 * py + pc],
                send_sem=send_sems.at[k], recv_sem=recv_sems.at[k],
                device_id=(px, py, pc), device_id_type=MESH).wait_recv()
        for cp in copies:
            cp.wait_send()
        if reduce:
            total = buf[0]
            for d in range(1, 8):
                total = total + buf[d]
            out_ref[...] = total
        else:
            out_ref[...] = buf[...]

    vm = pl.BlockSpec(memory_space=pltpu.VMEM)
    return pl.pallas_call(
        body, name=name,
        out_shape=jax.ShapeDtypeStruct((rows, LANES) if reduce else (8, rows, LANES), F32),
        in_specs=[vm], out_specs=vm,
        scratch_shapes=[pltpu.VMEM((8, rows, LANES), F32), pltpu.SemaphoreType.DMA((7,)), pltpu.SemaphoreType.DMA((7,))],
    )(v)


def _pad_rows(a, rows):
    return jnp.pad(a, ((0, rows - a.shape[0]), (0, 0)))


def _lane_pad(v):
    n = v.shape[1]
    return jnp.pad(v, ((0, 0), (0, -n % LANES)))


def _gather_all(w_in, w_attn_out, w_ssm_out, w_o, conv_w):
    d = D_MODEL
    w_proj_t = _gather_exchange([w_in[0].T.astype(BF16)])
    out_w = _gather_exchange([a[0].astype(BF16) for a in (w_attn_out, w_ssm_out, w_o)])
    conv_rows = conv_w[0].size // LANES
    conv_all = _gather_small(conv_w[0].reshape(conv_rows, LANES), False, "gather_conv_w")
    conv_w_all = conv_all[0::2].reshape(N_CHIPS, CONV_K, CONV_DIM // N_CHIPS).transpose(1, 0, 2).reshape(CONV_K, CONV_DIM)

    return w_proj_t, out_w, conv_w_all


def _local_step(x, loss_target, norm_w, w_proj_t, conv_w_all, conv_b, dt_bias, a_log, d_skip, ssm_norm_w,
                out_w, final_norm_w, grad_exchange=None):
    b, s, d = x.shape
    t = b * s
    g4, hg = SSD_GROUPS, HEADS_PER_GROUP
    dtb_g = _lane_pad(dt_bias.reshape(g4, hg)).reshape(g4, 1, LANES)
    alog_g = _lane_pad(a_log.reshape(g4, hg)).reshape(g4, 1, LANES)
    dskip_x = jnp.repeat(d_skip, HEAD_DIM, axis=1)
    fnw = final_norm_w.reshape(1, d)

    x2 = x.reshape(t, d)
    if isinstance(w_proj_t, _Exchange):
        h, w_in_t = _rms_fwd(x2, norm_w, exchange=w_proj_t)
        w_proj_t = _to_proj_layout(w_in_t.reshape(D_PROJ, d))
    else:
        h = _rms_fwd(x2, norm_w)
    big_tm = min(t, 2048)
    if isinstance(out_w, _Exchange):
        proj, *out_w = _matmul(h, w_proj_t, tb=True, tm=big_tm, tn=1280, tk=1024, name="proj", exchange=out_w)
    else:
        proj = _matmul(h, w_proj_t, tb=True, tm=big_tm, tn=1280, tk=1024, name="proj")
    w_ao, w_so, w_oo = (w.reshape(-1, d) for w in out_w)
    proj3 = proj.reshape(b, s, NP)
    o3, yp3 = _attn_fwd(proj3)
    xact = _conv_fwd(proj3, conv_w_all, conv_b)
    dtr = proj3[:, :, DT0:DT0 + g4 * hg].reshape(b, s, g4, hg).transpose(0, 2, 1, 3)
    dtr_g = jnp.pad(dtr, ((0, 0), (0, 0), (0, 0), (0, LANES - hg)))
    y3, yn3, hst = _ssd_fwd(xact, proj3, dtr_g, dtb_g, alog_g, dskip_x, ssm_norm_w)
    yp = yp3.reshape(t, D_MODEL)
    yn = yn3.reshape(t, SSD_WIDTH)
    ya = _matmul(yp, w_ao, tm=1024, tn=1024, tk=1024, name="attn_out")
    ys = _matmul(yn, w_so, tm=1024, tn=1024, tk=2048, name="ssm_out")
    merged = _merge_fwd(proj, ya, ys)
    mo = _matmul(merged, w_oo, tm=1024, tn=1024, tk=1024, name="out_proj")
    dout, doutb, loss_part, d_fnw = _final_fwd_bwd(x2, mo, loss_target.reshape(t, d), fnw)

    dmerged = _matmul(doutb, w_oo, tb=True, tm=1024, tn=1024, tk=1024, name="d_merged")
    g_wo = _matmul(merged, doutb, ta=True, tm=1024, tn=1024, tk=1024, name="g_w_o")
    dya, dys, dproj = _merge_bwd(dmerged, proj, ya, ys)
    dyp = _matmul(dya, w_ao, tb=True, tm=1024, tn=1024, tk=1024, name="d_attn_pre")
    g_wao = _matmul(yp, dya, ta=True, tm=1024, tn=1024, tk=1024, name="g_w_attn_out")
    dyn = _matmul(dys, w_so, tb=True, tm=1024, tn=2048, tk=1024, name="d_ssm_norm")
    g_wso = _matmul(yn, dys, ta=True, tm=1024, tn=1024, tk=1024, name="g_w_ssm_out")
    dproj3 = _attn_bwd(proj3, dyp.reshape(b, s, D_MODEL), o3, dproj.reshape(b, s, NP))
    (dxs, dbm, dcm, dproj3, ddtr_g, d_snw_g, d_alog_g, d_dtb_g, d_dsk_g) = _ssd_bwd(
        dyn.reshape(b, s, SSD_WIDTH), y3, xact, proj3, hst, dtr_g, dtb_g, alog_g, dskip_x, ssm_norm_w, dproj3)
    dproj3, g_cw_xs, g_cb_xs = _conv_bwd(dxs, proj3, conv_w_all, conv_b, 0, "conv_bwd_x", dproj3)
    dproj3, g_cw_bm, g_cb_bm = _conv_bwd(dbm, proj3, conv_w_all, conv_b, SSD_WIDTH, "conv_bwd_b", dproj3)
    dproj3, g_cw_cm, g_cb_cm = _conv_bwd(dcm, proj3, conv_w_all, conv_b, SSD_WIDTH + g4 * SSD_STATE, "conv_bwd_c", dproj3)
    ddt = ddtr_g[:, :, :, :hg].transpose(0, 2, 1, 3).reshape(b, s, g4 * hg).astype(BF16)
    ddt = jnp.pad(ddt, ((0, 0), (0, 0), (0, DT_PAD - g4 * hg)))
    dproj = lax.dynamic_update_slice(dproj3, ddt, (0, 0, DT0)).reshape(t, NP)
    exchanged = []
    if grad_exchange:
        g_wproj, *got = _matmul(dproj, h, ta=True, tm=1280, tn=1024, tk=1024, name="g_w_in",
                                exchange=grad_exchange([g_wao, g_wso, g_wo], "out"))
        exchanged += got
        dh, *got = _matmul(dproj, w_proj_t, tm=big_tm, tn=1024, tk=1280, name="d_h", exchange=grad_exchange([g_wproj], "in"))
        exchanged += got
    else:
        g_wproj = _matmul(dproj, h, ta=True, tm=1280, tn=1024, tk=1024, name="g_w_in")
        dh = _matmul(dproj, w_proj_t, tm=big_tm, tn=1024, tk=1280, name="d_h")
    grad_x, d_nw = _rms_bwd(dh, x2, norm_w, dout)
    g_cw = jnp.concatenate([g_cw_xs, g_cw_bm, g_cw_cm], axis=1)
    g_cb = jnp.concatenate([g_cb_xs, g_cb_bm, g_cb_cm], axis=1)
    return (loss_part, grad_x, d_nw, g_wproj, g_cw, g_cb, d_dtb_g, d_alog_g, d_dsk_g, d_snw_g, g_wao, g_wso, g_wo, d_fnw,
            exchanged)


def kernel(x, norm_w, w_in, conv_w, conv_b, dt_bias, a_log, d_skip, ssm_norm_w, w_attn_out, w_ssm_out, w_o, final_norm_w, loss_target, m_norm_w, m_w_in, m_conv_w, m_conv_b, m_dt_bias, m_a_log, m_d_skip, m_ssm_norm_w, m_w_attn_out, m_w_ssm_out, m_w_o, m_final_norm_w, v_norm_w, v_w_in, v_conv_w, v_conv_b, v_dt_bias, v_a_log, v_d_skip, v_ssm_norm_w, v_w_attn_out, v_w_ssm_out, v_w_o, v_final_norm_w):
    b, s, d = x.shape
    core = lax.axis_index("c")
    g4, hg = SSD_GROUPS, HEADS_PER_GROUP
    shard_cols = w_in.shape[2]
    w_proj_t, out_w, conv_w_all = _gather_all(w_in, w_attn_out, w_ssm_out, w_o, conv_w)
    chip = 2 * lax.axis_index("x") + lax.axis_index("y")
    place = jnp.stack([core, chip]).astype(jnp.int32)
    chip_sums = []

    def grad_exchange(grads, which):
        if which == "in":
            slabs = _from_proj_layout(grads[0]).reshape(N_CHIPS, shard_cols, d)
        else:
            slabs = jnp.concatenate([g.reshape(N_CHIPS, -1, d) for g in grads], axis=1)
        from_sibling, = _swap_halves([slabs], "grad_swap_halves_" + which)
        chip_sums.append(_add_halves(slabs, from_sibling, place, "grad_add_halves_" + which))
        return _all_to_all_exchange(chip_sums[-1:])

    (loss_part, grad_x, d_nw, _, g_cw, g_cb, d_dtb_g, d_alog_g, d_dsk_g, d_snw_g, _, _, _, d_fnw, from_chips) = _local_step(
        x, loss_target, norm_w, w_proj_t, conv_w_all, conv_b, dt_bias, a_log, d_skip, ssm_norm_w, out_w, final_norm_w,
        grad_exchange)
    wholes = [_sum_chips(o, r, place, "grad_sum_chips_%d" % i) for i, (o, r) in enumerate(zip(chip_sums, from_chips))]
    g_out, g_w_in = _join_halves(wholes)

    small = jnp.concatenate([
        loss_part, d_nw, g_cb, _lane_pad(d_dtb_g[:, 0, :hg].reshape(1, -1)), _lane_pad(d_alog_g[:, 0, :hg].reshape(1, -1)),
        _lane_pad(d_dsk_g[:, 0, :hg].reshape(1, -1)),
        d_snw_g.reshape(1, -1), d_fnw, g_cw.reshape(1, -1)], axis=1)
    small_rows = small.shape[1] // LANES
    reduced = _gather_small(_pad_rows(small.reshape(small_rows, LANES), -(-small_rows // 8) * 8), True, "reduce_small")
    flat = reduced.reshape(-1)

    def take(start, n):
        return flat[start:start + n].reshape(1, n)

    loss = flat[0]
    pos = LANES
    g_norm_w = take(pos, d); pos += d
    g_conv_b = take(pos, CONV_DIM); pos += CONV_DIM
    g_dt_bias = take(pos, g4 * hg); pos += LANES
    g_a_log = take(pos, g4 * hg); pos += LANES
    g_d_skip = take(pos, g4 * hg); pos += LANES
    g_ssm_norm_w = take(pos, SSD_WIDTH); pos += SSD_WIDTH
    g_final_norm_w = take(pos, d); pos += d
    conv_cols = CONV_DIM // N_CHIPS
    g_conv_w = lax.dynamic_slice_in_dim(flat[pos:pos + CONV_K * CONV_DIM].reshape(CONV_K, CONV_DIM), chip * conv_cols, conv_cols, axis=1)

    rows_ao, rows_so = D_MODEL // N_CHIPS, SSD_WIDTH // N_CHIPS
    g_w_attn_out = g_out[:rows_ao]
    g_w_ssm_out = g_out[rows_ao:rows_ao + rows_so]
    g_w_o = g_out[rows_ao + rows_so:]

    names = ["norm_w", "w_in", "conv_w", "conv_b", "dt_bias", "a_log", "d_skip", "ssm_norm_w",
             "w_attn_out", "w_ssm_out", "w_o", "final_norm_w"]
    weights = [norm_w, w_in, conv_w, conv_b, dt_bias, a_log, d_skip, ssm_norm_w, w_attn_out, w_ssm_out, w_o, final_norm_w]
    grads = [g_norm_w, g_w_in, g_conv_w, g_conv_b, g_dt_bias, g_a_log, g_d_skip, g_ssm_norm_w,
             g_w_attn_out, g_w_ssm_out, g_w_o, g_final_norm_w]
    ms = [m_norm_w, m_w_in, m_conv_w, m_conv_b, m_dt_bias, m_a_log, m_d_skip, m_ssm_norm_w,
          m_w_attn_out, m_w_ssm_out, m_w_o, m_final_norm_w]
    vs = [v_norm_w, v_w_in, v_conv_w, v_conv_b, v_dt_bias, v_a_log, v_d_skip, v_ssm_norm_w,
          v_w_attn_out, v_w_ssm_out, v_w_o, v_final_norm_w]
    out_g, out_d, out_m, out_v = [], [], [], []
    for name, w, g, m, v in zip(names, weights, grads, ms, vs):
        if name == "w_in":
            to2, back = (lambda a: a[0].T), (lambda a: a.T.reshape(w.shape))
        else:
            to2, back = (lambda a: a.reshape(g.shape)), (lambda a: a.reshape(w.shape))
        dlt, nm, nv = _adamw(to2(w), g, to2(m), to2(v), "adamw_" + name)
        out_g.append(back(g))
        out_d.append(back(dlt))
        out_m.append(back(nm))
        out_v.append(back(nv))

    return (loss, grad_x.reshape(b, s, d), *out_g, *out_d, *out_m, *out_v)
```

```python
import jax
import jax.numpy as jnp
from jax import lax
from jax.experimental import pallas as pl
from jax.experimental.pallas import tpu as pltpu

F32 = jnp.float32
BF16 = jnp.bfloat16
MESH = pl.DeviceIdType.MESH

D_MODEL = 1024
SB_HEADS = 16
HEAD_DIM = 64
SSD_WIDTH = 2048
SSD_GROUPS = 4
GROUP_WIDTH = SSD_WIDTH // SSD_GROUPS
HEADS_PER_GROUP = 8
SSD_STATE = 128
CHUNK = 128
CONV_K = 4
CONV_DIM = 3072
D_PROJ = 11296
EPS = 1e-6
ADAM_LR, ADAM_B1, ADAM_B2, ADAM_EPS, ADAM_WD, ADAM_STEP = 0.001, 0.9, 0.999, 1e-08, 0.01, 10

LANES = 128
HP_WIDTH = 4 * LANES
ZS0, GATE0, XBC0, DT0 = 4096, 6144, 8192, 11264
DT_PAD = 256
NP = DT0 + DT_PAD
N_CHIPS = 4
VMEM_LIMIT = 56 * 1024 * 1024


N_HP = SB_HEADS // 2
W_ZS0, W_XBC0, W_DT0, W_GATE0 = 4096, 6144, 9216, 9248


def _to_proj_layout(wt):
    d = wt.shape[1]
    pairs = wt[:W_ZS0].reshape(4, N_HP, LANES, d).transpose(1, 0, 2, 3).reshape(W_ZS0, d)
    return jnp.concatenate([pairs, wt[W_ZS0:W_XBC0], wt[W_GATE0:], wt[W_XBC0:W_DT0], wt[W_DT0:W_GATE0],
                            jnp.zeros((NP - D_PROJ, d), wt.dtype)], axis=0)


def _from_proj_layout(gt):
    d = gt.shape[1]
    qkvz = gt[:ZS0].reshape(N_HP, 4, LANES, d).transpose(1, 0, 2, 3).reshape(ZS0, d)
    return jnp.concatenate([qkvz, gt[ZS0:GATE0], gt[XBC0:DT0], gt[DT0:DT0 + W_GATE0 - W_DT0], gt[GATE0:XBC0]], axis=0)


def _cparams(*sem):
    return pltpu.CompilerParams(dimension_semantics=sem or None, vmem_limit_bytes=VMEM_LIMIT)


def _sigmoid(z):
    return 1.0 / (1.0 + jnp.exp(-z))


def _dot(a, b, dims, precision=None):
    return lax.dot_general(a, b, (dims, ((), ())), preferred_element_type=F32, precision=precision)


NN = ((1,), (0,))
NT = ((1,), (1,))
TN = ((0,), (0,))


def _matmul(a, b, *, ta=False, tb=False, out_dtype=F32, tm, tn, tk, name, exchange=None):
    m, k = (a.shape[1], a.shape[0]) if ta else a.shape
    n = b.shape[0] if tb else b.shape[1]
    assert m % tm == 0 and n % tn == 0 and k % tk == 0, (name, m, n, k)
    grid = (m // tm, n // tn, k // tk)
    nk = grid[2]
    use_scratch = out_dtype != F32
    dims = ((0,) if ta else (1,), (1,) if tb else (0,))
    n_in = len(exchange.inputs) if exchange else 0
    n_out = len(exchange.out_shapes) if exchange else 0

    def kern(a_ref, b_ref, *rest):
        x_in, o_ref, x_out, scratch = rest[:n_in], rest[n_in], rest[n_in + 1:n_in + 1 + n_out], rest[n_in + 1 + n_out:]
        acc = scratch[0] if use_scratch else o_ref
        step = [pl.program_id(ax) for ax in range(3)]
        if exchange:
            sems = scratch[1:] if use_scratch else scratch

            @pl.when(jnp.logical_and(jnp.logical_and(step[0] == 0, step[1] == 0), step[2] == 0))
            def _():
                exchange.start(x_in, x_out, sems)

        @pl.when(step[2] == 0)
        def _():
            acc[...] = jnp.zeros_like(acc)

        acc[...] += _dot(a_ref[...], b_ref[...], dims)
        if use_scratch:
            @pl.when(step[2] == nk - 1)
            def _():
                o_ref[...] = acc[...].astype(out_dtype)
        if exchange:
            @pl.when(jnp.logical_and(jnp.logical_and(step[0] == grid[0] - 1, step[1] == grid[1] - 1), step[2] == nk - 1))
            def _():
                exchange.finish(x_in, x_out, sems)

    a_spec = pl.BlockSpec((tk, tm), lambda i, j, q: (q, i)) if ta else pl.BlockSpec((tm, tk), lambda i, j, q: (i, q))
    b_spec = pl.BlockSpec((tn, tk), lambda i, j, q: (j, q)) if tb else pl.BlockSpec((tk, tn), lambda i, j, q: (q, j))
    out = pl.pallas_call(
        kern, name=name,
        out_shape=[jax.ShapeDtypeStruct((m, n), out_dtype)] + (list(exchange.out_shapes) if exchange else []),
        grid=grid,
        in_specs=[a_spec, b_spec] + [ANY] * n_in,
        out_specs=[pl.BlockSpec((tm, tn), lambda i, j, q: (i, j))] + [ANY] * n_out,
        scratch_shapes=([pltpu.VMEM((tm, tn), F32)] if use_scratch else []) + (list(exchange.sems) if exchange else []),
        compiler_params=_cparams("arbitrary", "arbitrary", "arbitrary") if exchange else _cparams("parallel", "parallel", "arbitrary"),
    )(a, b, *(exchange.inputs if exchange else []))
    return out if exchange else out[0]


ROWS = 512


def _rms_fwd(x2, w, exchange=None):
    t, d = x2.shape
    steps = t // ROWS
    n_in = len(exchange.inputs) if exchange else 0
    n_out = len(exchange.out_shapes) if exchange else 0

    def kern(x_ref, w_ref, *rest):
        x_in, h_ref, x_out, sems = rest[:n_in], rest[n_in], rest[n_in + 1:n_in + 1 + n_out], rest[n_in + 1 + n_out:]
        if exchange:
            @pl.when(pl.program_id(0) == 0)
            def _():
                exchange.start(x_in, x_out, sems)

        x = x_ref[...]
        r = lax.rsqrt(jnp.mean(x * x, axis=-1, keepdims=True) + EPS)
        h_ref[...] = (x * r * w_ref[...]).astype(BF16)
        if exchange:
            @pl.when(pl.program_id(0) == steps - 1)
            def _():
                exchange.finish(x_in, x_out, sems)

    out = pl.pallas_call(
        kern, name="rms_fwd",
        out_shape=[jax.ShapeDtypeStruct((t, d), BF16)] + (list(exchange.out_shapes) if exchange else []),
        grid=(steps,),
        in_specs=[pl.BlockSpec((ROWS, d), lambda i: (i, 0)), pl.BlockSpec((1, d), lambda i: (0, 0))] + [ANY] * n_in,
        out_specs=[pl.BlockSpec((ROWS, d), lambda i: (i, 0))] + [ANY] * n_out,
        scratch_shapes=list(exchange.sems) if exchange else [],
        compiler_params=_cparams("arbitrary" if exchange else "parallel"),
    )(x2, w, *(exchange.inputs if exchange else []))
    return out if exchange else out[0]


def _rms_bwd(dh, x2, w, dout):
    t, d = x2.shape

    def kern(dh_ref, x_ref, w_ref, dout_ref, gx_ref, dw_ref):
        @pl.when(pl.program_id(0) == 0)
        def _():
            dw_ref[...] = jnp.zeros_like(dw_ref)

        x = x_ref[...]
        r = lax.rsqrt(jnp.mean(x * x, axis=-1, keepdims=True) + EPS)
        xh = x * r
        g = dh_ref[...]
        dw_ref[...] += jnp.sum(g * xh, axis=0, keepdims=True)
        gw = g * w_ref[...]
        gx_ref[...] = dout_ref[...] + r * (gw - xh * jnp.mean(gw * xh, axis=-1, keepdims=True))

    row = pl.BlockSpec((ROWS, d), lambda i: (i, 0))
    vec = pl.BlockSpec((1, d), lambda i: (0, 0))
    return pl.pallas_call(
        kern, name="rms_bwd",
        out_shape=(jax.ShapeDtypeStruct((t, d), F32), jax.ShapeDtypeStruct((1, d), F32)),
        grid=(t // ROWS,),
        in_specs=[row, row, vec, row],
        out_specs=(row, vec),
        compiler_params=_cparams("arbitrary"),
    )(dh, x2, w, dout)


def _final_fwd_bwd(x2, mo, target, w):
    t, d = x2.shape

    def kern(x_ref, mo_ref, t_ref, w_ref, dout_ref, doutb_ref, loss_ref, dw_ref):
        @pl.when(pl.program_id(0) == 0)
        def _():
            loss_ref[...] = jnp.zeros_like(loss_ref)
            dw_ref[...] = jnp.zeros_like(dw_ref)

        u = x_ref[...] + mo_ref[...]
        r = lax.rsqrt(jnp.mean(u * u, axis=-1, keepdims=True) + EPS)
        uh = u * r
        wv = w_ref[...]
        err = uh * wv - t_ref[...]
        loss_ref[...] += (0.5 / d) * jnp.sum(err * err)
        dy = err * (1.0 / d)
        dw_ref[...] += jnp.sum(dy * uh, axis=0, keepdims=True)
        gw = dy * wv
        du = r * (gw - uh * jnp.mean(gw * uh, axis=-1, keepdims=True))
        dout_ref[...] = du
        doutb_ref[...] = du.astype(BF16)

    row = pl.BlockSpec((ROWS, d), lambda i: (i, 0))
    vec = pl.BlockSpec((1, d), lambda i: (0, 0))
    return pl.pallas_call(
        kern, name="final_fwd_bwd",
        out_shape=(jax.ShapeDtypeStruct((t, d), F32), jax.ShapeDtypeStruct((t, d), BF16),
                   jax.ShapeDtypeStruct((1, LANES), F32), jax.ShapeDtypeStruct((1, d), F32)),
        grid=(t // ROWS,),
        in_specs=[row, row, row, vec],
        out_specs=(row, row, pl.BlockSpec((1, LANES), lambda i: (0, 0)), vec),
        compiler_params=_cparams("arbitrary"),
    )(x2, mo, target, w)


def _merge_fwd(proj2, ya, ys):
    t = ya.shape[0]
    gblk = GATE0 // D_MODEL

    def kern(ga_ref, gs_ref, ya_ref, ys_ref, o_ref):
        o_ref[...] = (_sigmoid(ga_ref[...]) * ya_ref[...] + _sigmoid(gs_ref[...]) * ys_ref[...]).astype(BF16)

    row = pl.BlockSpec((ROWS, D_MODEL), lambda i: (i, 0))
    return pl.pallas_call(
        kern, name="merge_fwd",
        out_shape=jax.ShapeDtypeStruct((t, D_MODEL), BF16),
        grid=(t // ROWS,),
        in_specs=[pl.BlockSpec((ROWS, D_MODEL), lambda i: (i, gblk)),
                  pl.BlockSpec((ROWS, D_MODEL), lambda i: (i, gblk + 1)), row, row],
        out_specs=row,
        compiler_params=_cparams("parallel"),
    )(proj2, proj2, ya, ys)


def _merge_bwd(dm, proj2, ya, ys):
    t = ya.shape[0]
    gblk = GATE0 // D_MODEL

    def kern(dm_ref, ga_ref, gs_ref, ya_ref, ys_ref, dya_ref, dys_ref, dg_ref):
        g = dm_ref[...]
        sa = _sigmoid(ga_ref[...])
        ss = _sigmoid(gs_ref[...])
        dya_ref[...] = (g * sa).astype(BF16)
        dys_ref[...] = (g * ss).astype(BF16)
        dg_ref[:, :D_MODEL] = (g * ya_ref[...] * sa * (1.0 - sa)).astype(BF16)
        dg_ref[:, D_MODEL:] = (g * ys_ref[...] * ss * (1.0 - ss)).astype(BF16)

    row = pl.BlockSpec((ROWS, D_MODEL), lambda i: (i, 0))
    return pl.pallas_call(
        kern, name="merge_bwd",
        out_shape=(jax.ShapeDtypeStruct((t, D_MODEL), BF16), jax.ShapeDtypeStruct((t, D_MODEL), BF16),
                   jax.ShapeDtypeStruct((t, NP), BF16)),
        grid=(t // ROWS,),
        in_specs=[row, pl.BlockSpec((ROWS, D_MODEL), lambda i: (i, gblk)),
                  pl.BlockSpec((ROWS, D_MODEL), lambda i: (i, gblk + 1)), row, row],
        out_specs=(row, row, pl.BlockSpec((ROWS, 2 * D_MODEL), lambda i: (i, GATE0 // (2 * D_MODEL)))),
        compiler_params=_cparams("parallel"),
    )(dm, proj2, proj2, ya, ys)


TQ = 256
TK = 256
assert TQ == TK
HEAD_LANES = (slice(0, HEAD_DIM), slice(HEAD_DIM, 2 * HEAD_DIM))


def _tri(pred):
    r = lax.broadcasted_iota(jnp.int32, (TK, TK), 0)
    c = lax.broadcasted_iota(jnp.int32, (TK, TK), 1)
    return pred(r, c).astype(BF16)


def _split_bf16(v):
    hi = v.astype(BF16)
    lo = (v - hi.astype(F32)).astype(BF16)
    return hi, lo


def _tri_dot(v, tri):
    hi, lo = _split_bf16(v)
    return _dot(hi, tri, NN) + _dot(lo, tri, NN)


def _sb_logs(z, mask):
    l1p = jnp.log(1.0 + jnp.exp(-jnp.abs(z)))
    lb = jnp.minimum(z, 0.0) - l1p
    lom = -jnp.maximum(z, 0.0) - l1p
    if mask is not None:
        lom = jnp.where(mask, lom, 0.0)
    return lb, lom


def _sb_weights(lb, later, carry_r, mask):
    a = jnp.exp(lb + (later + carry_r))
    if mask is not None:
        a = jnp.where(mask, a, 0.0)
    return a


DEAD = -104.0


def _while_alive(n, carry, step):
    def alive(cr):
        return jnp.max(jnp.maximum(cr[0][0], cr[1][0])) > DEAD

    def cond(state):
        jj, go, _ = state
        return jnp.logical_and(jj < n, go)

    def body(state):
        jj, _, cr = state
        cr = step(jj, cr)
        return jj + 1, alive(cr), cr

    return lax.while_loop(cond, body, (jnp.int32(0), alive(carry), carry))[2]


Q_LANES, K_LANES, V_LANES, ZA_LANES = (slice(i * LANES, (i + 1) * LANES) for i in range(4))


def _split_heads(dst, src, scale=None):
    for h, lanes in enumerate(HEAD_LANES):
        v = src[:, lanes]
        dst[h] = (v if scale is None else v * scale).astype(BF16)


def _attn_fwd(proj3):
    b, s, _ = proj3.shape
    nq = s // TQ
    scale = HEAD_DIM ** -0.5

    def kern(x_ref, o_ref, yp_ref, qs, ks, vs):
        _split_heads(qs, x_ref[0, :, Q_LANES], scale)
        _split_heads(ks, x_ref[0, :, K_LANES])
        _split_heads(vs, x_ref[0, :, V_LANES])
        za_ref = x_ref.at[:, :, ZA_LANES]
        row = lax.broadcasted_iota(jnp.int32, (TQ, TK), 0)
        col = lax.broadcasted_iota(jnp.int32, (TQ, TK), 1)
        tri_gt = _tri(lambda j, sk: j > sk)

        def q_block(i, _):
            top = isinstance(i, int)
            r0 = i * TQ if top else pl.multiple_of(i * TQ, TQ)
            qh = [qs[h, pl.ds(r0, TQ), :] for h in range(2)]

            def k_blocks(blocks, carry):
                nb = range(len(blocks))
                kh = [[ks[h, pl.ds(c0, TK), :] for h in range(2)] for c0, _ in blocks]
                vh = [[vs[h, pl.ds(c0, TK), :] for h in range(2)] for c0, _ in blocks]
                z = [[_dot(qh[h], kh[bl][h], NT) for h in range(2)] for bl in nb]
                logs = [[_sb_logs(z[bl][h], blocks[bl][1]) for h in range(2)] for bl in nb]
                later = [[_tri_dot(logs[bl][h][1], tri_gt) for h in range(2)] for bl in nb]
                out = []
                for h in range(2):
                    carry_r, acc = carry[h]
                    for bl in nb:
                        lb, lom = logs[bl][h]
                        a = _sb_weights(lb, later[bl][h], carry_r, blocks[bl][1])
                        acc = acc + _dot(a.astype(BF16), vh[bl][h], NN)
                        carry_r = carry_r + (later[bl][h][:, 0:1] + lom[:, 0:1])
                    out.append((carry_r, acc))
                return tuple(out)

            start = (jnp.zeros((TQ, 1), F32), jnp.zeros((TQ, HEAD_DIM), F32))
            diag = (r0, col < row)
            if top:
                carry = k_blocks([diag], (start, start))
            else:
                carry = k_blocks([diag, (pl.multiple_of(r0 - TK, TK), None)], (start, start))
                carry = _while_alive(i - 1, carry, lambda jj, cr: k_blocks([(pl.multiple_of((i - 2 - jj) * TK, TK), None)], cr))
            for (_, acc), lanes in zip(carry, HEAD_LANES):
                o_ref[0, pl.ds(r0, TQ), lanes] = acc
                za = za_ref[0, pl.ds(r0, TQ), lanes]
                yp_ref[0, pl.ds(r0, TQ), lanes] = (acc * (za * _sigmoid(za))).astype(BF16)
            return 0

        q_block(0, 0)
        lax.fori_loop(1, nq, q_block, 0)

    out_spec = pl.BlockSpec((1, s, LANES), lambda bi, hp: (bi, 0, hp))
    return pl.pallas_call(
        kern, name="attn_fwd",
        out_shape=(jax.ShapeDtypeStruct((b, s, D_MODEL), F32), jax.ShapeDtypeStruct((b, s, D_MODEL), BF16)),
        grid=(b, SB_HEADS // 2),
        in_specs=[pl.BlockSpec((1, s, HP_WIDTH), lambda bi, hp: (bi, 0, hp))],
        out_specs=(out_spec, out_spec),
        scratch_shapes=[pltpu.VMEM((2, s, HEAD_DIM), BF16)] * 3,
        compiler_params=_cparams("parallel", "parallel"),
    )(proj3)


def _attn_bwd(proj3, dyp3, o3, dproj3):
    b, s, _ = proj3.shape
    nq = s // TQ
    scale = HEAD_DIM ** -0.5

    def kern(x_ref, dyp_ref, o_ref, _, d_ref, qs, ks, vs, dos, dk_acc, dv_acc):
        _split_heads(qs, x_ref[0, :, Q_LANES], scale)
        _split_heads(ks, x_ref[0, :, K_LANES])
        _split_heads(vs, x_ref[0, :, V_LANES])
        dq_ref, dk_ref, dv_ref = (d_ref.at[:, :, lanes] for lanes in (Q_LANES, K_LANES, V_LANES))
        za = x_ref[0, :, ZA_LANES]
        sg = _sigmoid(za)
        dyp = dyp_ref[0]
        _split_heads(dos, dyp * (za * sg))
        d_ref[0, :, ZA_LANES] = (dyp * o_ref[0] * (sg * (1.0 + za * (1.0 - sg)))).astype(BF16)
        dk_acc[...] = jnp.zeros_like(dk_acc)
        dv_acc[...] = jnp.zeros_like(dv_acc)
        row = lax.broadcasted_iota(jnp.int32, (TQ, TK), 0)
        col = lax.broadcasted_iota(jnp.int32, (TQ, TK), 1)
        tri_gt = _tri(lambda j, sk: j > sk)
        tri_ge = _tri(lambda j, sk: j >= sk)

        def q_block(i, _):
            top = isinstance(i, int)
            r0 = i * TQ if top else pl.multiple_of(i * TQ, TQ)
            qh = [qs[h, pl.ds(r0, TQ), :] for h in range(2)]
            doh = [dos[h, pl.ds(r0, TQ), :] for h in range(2)]
            totals = [jnp.sum(doh[h].astype(F32) * o_ref[0, pl.ds(r0, TQ), lanes], axis=1, keepdims=True)
                      for h, lanes in enumerate(HEAD_LANES)]

            def k_blocks(blocks, carry):
                nb = range(len(blocks))
                kh = [[ks[h, pl.ds(c0, TK), :] for h in range(2)] for c0, _ in blocks]
                vh = [[vs[h, pl.ds(c0, TK), :] for h in range(2)] for c0, _ in blocks]
                z = [[_dot(qh[h], kh[bl][h], NT) for h in range(2)] for bl in nb]
                da = [[_dot(doh[h], vh[bl][h], NT) for h in range(2)] for bl in nb]
                logs = [[_sb_logs(z[bl][h], blocks[bl][1]) for h in range(2)] for bl in nb]
                later = [[_tri_dot(logs[bl][h][1], tri_gt) for h in range(2)] for bl in nb]
                ab, g, suffix = ([[None, None] for _ in nb] for _ in range(3))
                for h in range(2):
                    cr = carry[h][0]
                    for bl in nb:
                        a = _sb_weights(logs[bl][h][0], later[bl][h], cr, blocks[bl][1])
                        ab[bl][h] = a.astype(BF16)
                        g[bl][h] = da[bl][h] * ab[bl][h].astype(F32)
                        suffix[bl][h] = _tri_dot(g[bl][h], tri_ge)
                        cr = cr + (later[bl][h][:, 0:1] + logs[bl][h][1][:, 0:1])
                out = []
                for h in range(2):
                    _, carry_g, dq = carry[h]
                    cr = carry[h][0]
                    for bl in nb:
                        c0, mask = blocks[bl]
                        lb, lom = logs[bl][h]
                        dz = g[bl][h] - (g[bl][h] + (totals[h] - carry_g) - suffix[bl][h]) * jnp.exp(lb)
                        if mask is not None:
                            dz = jnp.where(mask, dz, 0.0)
                        dzb = dz.astype(BF16)
                        dk_acc[h, pl.ds(c0, TK), :] += _dot(dzb, qh[h], TN)
                        dv_acc[h, pl.ds(c0, TK), :] += _dot(ab[bl][h], doh[h], TN)
                        dq = dq + _dot(dzb, kh[bl][h], NN)
                        carry_g = carry_g + suffix[bl][h][:, 0:1]
                        cr = cr + (later[bl][h][:, 0:1] + lom[:, 0:1])
                    out.append((cr, carry_g, dq))
                return tuple(out)

            def k_block(c0, carry, mask):
                kh = [ks[h, pl.ds(c0, TK), :] for h in range(2)]
                vh = [vs[h, pl.ds(c0, TK), :] for h in range(2)]
                z = [_dot(qh[h], kh[h], NT) for h in range(2)]
                da = [_dot(doh[h], vh[h], NT) for h in range(2)]
                logs, later = [], []
                for h in range(2):
                    logs.append(_sb_logs(z[h], mask))
                    later.append(_tri_dot(logs[h][1], tri_gt))
                ab, g, suffix = [], [], []
                for h in range(2):
                    a = _sb_weights(logs[h][0], later[h], carry[h][0], mask)
                    ab.append(a.astype(BF16))
                    g.append(da[h] * ab[h].astype(F32))
                    suffix.append(_tri_dot(g[h], tri_ge))
                out = []
                for h in range(2):
                    carry_r, carry_g, dq = carry[h]
                    lb, lom = logs[h]
                    dz = g[h] - (g[h] + (totals[h] - carry_g) - suffix[h]) * jnp.exp(lb)
                    if mask is not None:
                        dz = jnp.where(mask, dz, 0.0)
                    dzb = dz.astype(BF16)
                    dk_acc[h, pl.ds(c0, TK), :] += _dot(dzb, qh[h], TN)
                    dv_acc[h, pl.ds(c0, TK), :] += _dot(ab[h], doh[h], TN)
                    out.append((carry_r + (later[h][:, 0:1] + lom[:, 0:1]), carry_g + suffix[h][:, 0:1],
                                dq + _dot(dzb, kh[h], NN)))
                return tuple(out)

            zero = jnp.zeros((TQ, 1), F32)
            start = (zero, zero, jnp.zeros((TQ, HEAD_DIM), F32))
            diag = (r0, col < row)
            if top:
                carry = k_block(r0, (start, start), col < row)
            else:
                carry = k_blocks([diag, (pl.multiple_of(r0 - TK, TK), None)], (start, start))
                carry = _while_alive(i - 1, carry, lambda jj, cr: k_block(pl.multiple_of((i - 2 - jj) * TK, TK), cr, None))
            for (_, _, dq), lanes in zip(carry, HEAD_LANES):
                dq_ref[0, pl.ds(r0, TQ), lanes] = (dq * scale).astype(BF16)
            return 0

        q_block(0, 0)
        lax.fori_loop(1, nq, q_block, 0)

        for h, lanes in enumerate(HEAD_LANES):
            dk_ref[0, :, lanes] = dk_acc[h].astype(BF16)
            dv_ref[0, :, lanes] = dv_acc[h].astype(BF16)

    plain = pl.BlockSpec((1, s, LANES), lambda bi, hp: (bi, 0, hp))
    pair = pl.BlockSpec((1, s, HP_WIDTH), lambda bi, hp: (bi, 0, hp))
    return pl.pallas_call(
        kern, name="attn_bwd",
        out_shape=jax.ShapeDtypeStruct(dproj3.shape, dproj3.dtype),
        grid=(b, SB_HEADS // 2),
        in_specs=[pair, plain, plain, ANY],
        out_specs=pair,
        input_output_aliases={3: 0},
        scratch_shapes=[pltpu.VMEM((2, s, HEAD_DIM), BF16)] * 4 + [pltpu.VMEM((2, s, HEAD_DIM), F32)] * 2,
        compiler_params=_cparams("parallel", "parallel"),
    )(proj3, dyp3, o3, dproj3)


CONV_COLS = 256
HALO = 8


def _conv_pre(xp, w_ref, b_ref, r0):
    pre = b_ref[...] + w_ref[CONV_K - 1:CONV_K, :] * xp[pl.ds(HALO + r0, CHUNK), :]
    for kk in range(1, CONV_K):
        pre = pre + w_ref[CONV_K - 1 - kk:CONV_K - kk, :] * xp[pl.ds(HALO + r0 - kk, CHUNK), :]
    return pre


def _conv_fwd(proj3, conv_w, conv_b):
    b, s, _ = proj3.shape
    nc = s // CHUNK

    def kern(x_ref, w_ref, b_ref, o_ref, xp):
        xp[0:HALO, :] = jnp.zeros((HALO, CONV_COLS), F32)
        xp[HALO:, :] = x_ref[0]
        for ci in range(nc):
            pre = _conv_pre(xp, w_ref, b_ref, ci * CHUNK)
            o_ref[0, ci * CHUNK:(ci + 1) * CHUNK, :] = pre * _sigmoid(pre)

    return pl.pallas_call(
        kern, name="conv_fwd",
        out_shape=jax.ShapeDtypeStruct((b, s, CONV_DIM), F32),
        grid=(CONV_DIM // CONV_COLS, b),
        in_specs=[pl.BlockSpec((1, s, CONV_COLS), lambda j, bi: (bi, 0, XBC0 // CONV_COLS + j)),
                  pl.BlockSpec((CONV_K, CONV_COLS), lambda j, bi: (0, j)),
                  pl.BlockSpec((1, CONV_COLS), lambda j, bi: (0, j))],
        out_specs=pl.BlockSpec((1, s, CONV_COLS), lambda j, bi: (bi, 0, j)),
        scratch_shapes=[pltpu.VMEM((s + HALO, CONV_COLS), F32)],
        compiler_params=_cparams("parallel", "parallel"),
    )(proj3, conv_w, conv_b)


def _conv_bwd(dact, proj3, conv_w, conv_b, col0, name, dproj3):
    b, s, width = dact.shape
    nc = s // CHUNK
    j0 = col0 // CONV_COLS

    def kern(da_ref, x_ref, w_ref, b_ref, _, dx_ref, dw_ref, db_ref, xp, dp):
        @pl.when(pl.program_id(1) == 0)
        def _():
            dw_ref[...] = jnp.zeros_like(dw_ref)
            db_ref[...] = jnp.zeros_like(db_ref)

        xp[0:HALO, :] = jnp.zeros((HALO, CONV_COLS), F32)
        xp[HALO:, :] = x_ref[0]
        dp[s:, :] = jnp.zeros((HALO, CONV_COLS), F32)
        for ci in range(nc):
            r0 = ci * CHUNK
            pre = _conv_pre(xp, w_ref, b_ref, r0)
            sg = _sigmoid(pre)
            dpre = da_ref[0, r0:r0 + CHUNK, :] * (sg * (1.0 + pre * (1.0 - sg)))
            dp[r0:r0 + CHUNK, :] = dpre
            db_ref[...] += jnp.sum(dpre, axis=0, keepdims=True)
            for kk in range(CONV_K):
                tap = CONV_K - 1 - kk
                dw_ref[tap:tap + 1, :] += jnp.sum(dpre * xp[pl.ds(HALO + r0 - kk, CHUNK), :], axis=0, keepdims=True)
        for ci in range(nc):
            r0 = ci * CHUNK
            dx = w_ref[CONV_K - 1:CONV_K, :] * dp[pl.ds(r0, CHUNK), :]
            for kk in range(1, CONV_K):
                dx = dx + w_ref[CONV_K - 1 - kk:CONV_K - kk, :] * dp[pl.ds(r0 + kk, CHUNK), :]
            dx_ref[0, r0:r0 + CHUNK, :] = dx.astype(BF16)

    return pl.pallas_call(
        kern, name=name,
        out_shape=(jax.ShapeDtypeStruct(dproj3.shape, dproj3.dtype), jax.ShapeDtypeStruct((CONV_K, width), F32),
                   jax.ShapeDtypeStruct((1, width), F32)),
        grid=(width // CONV_COLS, b),
        in_specs=[pl.BlockSpec((1, s, CONV_COLS), lambda j, bi: (bi, 0, j)),
                  pl.BlockSpec((1, s, CONV_COLS), lambda j, bi: (bi, 0, XBC0 // CONV_COLS + j0 + j)),
                  pl.BlockSpec((CONV_K, CONV_COLS), lambda j, bi: (0, j0 + j)),
                  pl.BlockSpec((1, CONV_COLS), lambda j, bi: (0, j0 + j)), ANY],
        out_specs=(pl.BlockSpec((1, s, CONV_COLS), lambda j, bi: (bi, 0, XBC0 // CONV_COLS + j0 + j)),
                   pl.BlockSpec((CONV_K, CONV_COLS), lambda j, bi: (0, j)),
                   pl.BlockSpec((1, CONV_COLS), lambda j, bi: (0, j))),
        input_output_aliases={4: 0},
        scratch_shapes=[pltpu.VMEM((s + HALO, CONV_COLS), F32)] * 2,
        compiler_params=_cparams("parallel", "arbitrary"),
    )(dact, proj3, conv_w, conv_b, dproj3)


SSD_CHUNKS_PER_STEP = 8


def _sel_dot(v, sel, left=False):
    hi = v.astype(BF16)
    rest = v - hi.astype(F32)
    mid = rest.astype(BF16)
    lo = (rest - mid.astype(F32)).astype(BF16)
    if left:
        return _dot(sel, hi, NN) + _dot(sel, mid, NN) + _dot(sel, lo, NN)
    return _dot(hi, sel, NN) + _dot(mid, sel, NN) + _dot(lo, sel, NN)


def _ssd_common(dtr, dtb, alog):
    lane = lax.broadcasted_iota(jnp.int32, (CHUNK, LANES), 1)
    row = lax.broadcasted_iota(jnp.int32, (CHUNK, LANES), 0)
    head_lane = lane < HEADS_PER_GROUP
    pre = dtr + dtb
    dt = jnp.where(head_lane, jnp.maximum(pre, 0.0) + jnp.log(1.0 + jnp.exp(-jnp.abs(pre))), 0.0)
    a = jnp.where(head_lane[0:1], -jnp.exp(alog), 0.0)
    tril = (row >= lane).astype(BF16)
    acs = _sel_dot(dt * a, tril, left=True)
    acs_t = acs.T
    er = lax.broadcasted_iota(jnp.int32, (LANES, GROUP_WIDTH), 0)
    ec = lax.broadcasted_iota(jnp.int32, (LANES, GROUP_WIDTH), 1)
    expand = ((ec // HEAD_DIM) == er).astype(BF16)
    tr = lax.broadcasted_iota(jnp.int32, (GROUP_WIDTH, LANES), 0)
    tc = lax.broadcasted_iota(jnp.int32, (GROUP_WIDTH, LANES), 1)
    reduce = ((tr // HEAD_DIM) == tc).astype(BF16)
    dt_x = _sel_dot(dt, expand)
    acs_x = _sel_dot(acs, expand)
    end_x = acs_x[CHUNK - 1:CHUNK, :]
    causal = row >= lane
    return dict(dt=dt, a=a, pre=pre, head_lane=head_lane, acs=acs, acs_t=acs_t, expand=expand, reduce=reduce,
                dt_x=dt_x, acs_x=acs_x, end_x=end_x, causal=causal, row=row, lane=lane)


def _ssd_decay(cm, h):
    seg = cm["acs"][:, h:h + 1] - cm["acs_t"][h:h + 1, :]
    return jnp.where(cm["causal"], jnp.exp(jnp.minimum(seg, 0.0)), 0.0)


def _ssd_fwd(xact, proj3, dtr_g, dtb_g, alog_g, dskip_x, snw):
    b, s, _ = xact.shape
    nc = s // CHUNK
    g4 = SSD_GROUPS
    cps = min(nc, SSD_CHUNKS_PER_STEP)
    rows_per_step = cps * CHUNK

    def kern(xs_ref, bm_ref, cm_ref, zs_ref, dtr_ref, dtb_ref, alog_ref, dsk_ref, snw_ref,
             y_ref, yn_ref, hst_ref, h_sc):
        @pl.when(pl.program_id(2) == 0)
        def _():
            h_sc[...] = jnp.zeros_like(h_sc)

        def chunk(ci, _):
            rows = pl.ds(pl.multiple_of(ci * CHUNK, CHUNK), CHUNK)
            cm = _ssd_common(dtr_ref[0, 0, rows, :], dtb_ref[0], alog_ref[0])
            x = xs_ref[0, rows, :]
            bmb = bm_ref[0, rows, :].astype(BF16)
            cmb = cm_ref[0, rows, :].astype(BF16)
            h_in = h_sc[...]
            hst_ref[0, ci, 0] = h_in
            xdt = x * cm["dt_x"]
            xdtb = xdt.astype(BF16)
            cb = _dot(cmb, bmb, NT)
            y_off = _dot(cmb, h_in.astype(BF16), NN) * jnp.exp(cm["acs_x"])
            for h in range(HEADS_PER_GROUP):
                lanes = slice(h * HEAD_DIM, (h + 1) * HEAD_DIM)
                m = (cb * _ssd_decay(cm, h)).astype(BF16)
                y_ref[0, rows, lanes] = _dot(m, xdtb[:, lanes], NN)
            y = y_ref[0, rows, :] + y_off + x * dsk_ref[...]
            y_ref[0, rows, :] = y
            w = (xdt * jnp.exp(cm["end_x"] - cm["acs_x"])).astype(BF16)
            h_sc[...] = h_in * jnp.exp(cm["end_x"]) + _dot(bmb, w, TN)
            zs = zs_ref[0, rows, :]
            y2 = y * (zs * _sigmoid(zs))
            yn_ref[0, rows, :] = (y2 * lax.rsqrt(jnp.mean(y2 * y2, axis=-1, keepdims=True) + EPS) * snw_ref[...]).astype(BF16)
            return 0

        lax.fori_loop(0, cps, chunk, 0)

    gw = GROUP_WIDTH
    small = pl.BlockSpec((1, 1, LANES), lambda gi, bi, ci: (gi, 0, 0))
    xblk = pl.BlockSpec((1, rows_per_step, gw), lambda gi, bi, ci: (bi, ci, gi))
    return pl.pallas_call(
        kern, name="ssd_fwd",
        out_shape=(jax.ShapeDtypeStruct((b, s, SSD_WIDTH), F32), jax.ShapeDtypeStruct((b, s, SSD_WIDTH), BF16),
                   jax.ShapeDtypeStruct((b, nc, g4, SSD_STATE, gw), F32)),
        grid=(g4, b, nc // cps),
        in_specs=[xblk,
                  pl.BlockSpec((1, rows_per_step, LANES), lambda gi, bi, ci: (bi, ci, SSD_WIDTH // LANES + gi)),
                  pl.BlockSpec((1, rows_per_step, LANES), lambda gi, bi, ci: (bi, ci, SSD_WIDTH // LANES + g4 + gi)),
                  pl.BlockSpec((1, rows_per_step, gw), lambda gi, bi, ci: (bi, ci, ZS0 // gw + gi)),
                  pl.BlockSpec((1, 1, rows_per_step, LANES), lambda gi, bi, ci: (bi, gi, ci, 0)),
                  small, small,
                  pl.BlockSpec((1, gw), lambda gi, bi, ci: (0, gi)),
                  pl.BlockSpec((1, gw), lambda gi, bi, ci: (0, gi))],
        out_specs=(xblk, xblk, pl.BlockSpec((1, cps, 1, SSD_STATE, gw), lambda gi, bi, ci: (bi, ci, gi, 0, 0))),
        scratch_shapes=[pltpu.VMEM((SSD_STATE, gw), F32)],
        compiler_params=_cparams("parallel", "parallel", "arbitrary"),
    )(xact, xact, xact, proj3, dtr_g, dtb_g, alog_g, dskip_x, snw)


def _ssd_bwd(dyn3, y3, xact, proj3, hst, dtr_g, dtb_g, alog_g, dskip_x, snw, dproj3):
    b, s, _ = xact.shape
    nc = s // CHUNK
    g4 = SSD_GROUPS
    gw = GROUP_WIDTH

    cps = min(nc, SSD_CHUNKS_PER_STEP)
    rows_per_step = cps * CHUNK

    def one_chunk(dyn_ref, y_ref, xs_ref, bm_ref, cm_ref, zs_ref, hst_ref, dtr_ref, dtb_ref, alog_ref, dsk_ref, snw_ref,
                  dxs_ref, dbm_ref, dcm_ref, dzs_ref, ddtr_ref, dsnw_ref, dalog_ref, ddtb_ref, ddsk_ref, dh_sc):
        cm = _ssd_common(dtr_ref[0, 0], dtb_ref[0], alog_ref[0])
        row, lane = cm["row"], cm["lane"]
        y = y_ref[0]
        zs = zs_ref[0]
        sg = _sigmoid(zs)
        silu = zs * sg
        y2 = y * silu
        rstd = lax.rsqrt(jnp.mean(y2 * y2, axis=-1, keepdims=True) + EPS)
        y2h = y2 * rstd
        dyn = dyn_ref[0]
        dsnw_ref[0] += jnp.sum(dyn * y2h, axis=0, keepdims=True)
        gwv = dyn * snw_ref[...]
        dy2 = rstd * (gwv - y2h * jnp.mean(gwv * y2h, axis=-1, keepdims=True))
        dzs_ref[0] = (dy2 * y * (sg * (1.0 + zs * (1.0 - sg)))).astype(BF16)
        dy = dy2 * silu
        dyb = dy.astype(BF16)

        x = xs_ref[0]
        bmb = bm_ref[0].astype(BF16)
        cmb = cm_ref[0].astype(BF16)
        h_in = hst_ref[0, 0, 0]
        h_inb = h_in.astype(BF16)
        d_hn = dh_sc[...]
        d_hnb = d_hn.astype(BF16)
        xdt = x * cm["dt_x"]
        xdtb = xdt.astype(BF16)
        eacs = jnp.exp(cm["acs_x"])
        dte = jnp.exp(cm["end_x"] - cm["acs_x"])
        wb = (xdt * dte).astype(BF16)

        dsk_lanes = jnp.broadcast_to(jnp.sum(dy * x, axis=0, keepdims=True), (8, gw))
        ddsk_ref[0] += _sel_dot(dsk_lanes, cm["reduce"])[0:1, :]
        dyo = dy * eacs
        dyob = dyo.astype(BF16)
        dacs_x = dyo * _dot(cmb, h_inb, NN)
        dcm = _dot(dyob, h_inb, NT)
        dh_in = _dot(cmb, dyob, TN)
        dw = _dot(bmb, d_hnb, NN)
        dbm = _dot(wb, d_hnb, NT)
        dxdt = dw * dte
        e_l = dw * xdt * dte
        dacs_x = dacs_x - e_l
        dend_x = jnp.sum(e_l, axis=0, keepdims=True)
        chunk_decay = jnp.exp(cm["end_x"])
        dh_sc[...] = d_hn * chunk_decay + dh_in
        dend_x = dend_x + jnp.sum(d_hn * h_in, axis=0, keepdims=True) * chunk_decay
        last_row = lax.broadcasted_iota(jnp.int32, (CHUNK, gw), 0) == CHUNK - 1
        dacs_x = dacs_x + jnp.where(last_row, dend_x, 0.0)

        cb = _dot(cmb, bmb, NT)
        dcb = jnp.zeros((CHUNK, CHUNK), F32)
        dacs = jnp.zeros((CHUNK, LANES), F32)
        dacs_t = jnp.zeros((LANES, CHUNK), F32)
        for h in range(HEADS_PER_GROUP):
            lanes = slice(h * HEAD_DIM, (h + 1) * HEAD_DIM)
            decay = _ssd_decay(cm, h)
            m = cb * decay
            dm = _dot(dyb[:, lanes], xdtb[:, lanes], NT)
            dxs_ref[0, :, lanes] = _dot(m.astype(BF16), dyb[:, lanes], TN)
            dcb_h = dm * decay
            dcb = dcb + dcb_h
            n = dcb_h * cb
            dacs = dacs + jnp.where(lane == h, jnp.sum(n, axis=1, keepdims=True), 0.0)
            dacs_t = dacs_t + jnp.where(row == h, jnp.sum(n, axis=0, keepdims=True), 0.0)
        dcbb = dcb.astype(BF16)
        dcm_ref[0] = dcm + _dot(dcbb, bmb, NN)
        dbm_ref[0] = dbm + _dot(dcbb, cmb, TN)
        dxdt = dxdt + dxs_ref[0]
        dxs_ref[0] = dy * dsk_ref[...] + dxdt * cm["dt_x"]

        dacs = dacs - dacs_t.T + _sel_dot(dacs_x, cm["reduce"])
        ddt = _sel_dot(dxdt * x, cm["reduce"])
        triu = (row <= lane).astype(BF16)
        rc = _sel_dot(dacs, triu, left=True)
        ddt = ddt + cm["a"] * rc
        dalog_ref[0] += jnp.sum(cm["dt"] * rc, axis=0, keepdims=True) * cm["a"]
        ddtr = jnp.where(cm["head_lane"], ddt * _sigmoid(cm["pre"]), 0.0)
        ddtr_ref[0, 0] = ddtr
        ddtb_ref[0] += jnp.sum(ddtr, axis=0, keepdims=True)

    def kern(dyn_ref, y_ref, xs_ref, bm_ref, cm_ref, zs_ref, hst_ref, dtr_ref, dtb_ref, alog_ref, dsk_ref, snw_ref, _,
             dxs_ref, dbm_ref, dcm_ref, dzs_ref, ddtr_ref, dsnw_ref, dalog_ref, ddtb_ref, ddsk_ref, dh_sc):
        first = jnp.logical_and(pl.program_id(1) == 0, pl.program_id(2) == 0)

        @pl.when(first)
        def _():
            dsnw_ref[...] = jnp.zeros_like(dsnw_ref)
            dalog_ref[...] = jnp.zeros_like(dalog_ref)
            ddtb_ref[...] = jnp.zeros_like(ddtb_ref)
            ddsk_ref[...] = jnp.zeros_like(ddsk_ref)

        @pl.when(pl.program_id(2) == 0)
        def _():
            dh_sc[...] = jnp.zeros_like(dh_sc)

        def chunk(k, _):
            ci = cps - 1 - k
            rows = pl.ds(pl.multiple_of(ci * CHUNK, CHUNK), CHUNK)
            by_rows = [r.at[:, rows, :] for r in (dyn_ref, y_ref, xs_ref, bm_ref, cm_ref, zs_ref)]
            one_chunk(*by_rows, hst_ref.at[:, pl.ds(ci, 1)], dtr_ref.at[:, :, rows, :], dtb_ref, alog_ref, dsk_ref, snw_ref,
                      *[r.at[:, rows, :] for r in (dxs_ref, dbm_ref, dcm_ref, dzs_ref)], ddtr_ref.at[:, :, rows, :],
                      dsnw_ref, dalog_ref, ddtb_ref, ddsk_ref, dh_sc)
            return 0

        lax.fori_loop(0, cps, chunk, 0)

    def rev(ci):
        return nc // cps - 1 - ci

    small = pl.BlockSpec((1, 1, LANES), lambda gi, bi, ci: (gi, 0, 0))
    xblk = pl.BlockSpec((1, rows_per_step, gw), lambda gi, bi, ci: (bi, rev(ci), gi))
    nblk = pl.BlockSpec((1, rows_per_step, LANES), lambda gi, bi, ci: (bi, rev(ci), gi))
    gvec = pl.BlockSpec((1, gw), lambda gi, bi, ci: (0, gi))
    gacc = pl.BlockSpec((1, 1, gw), lambda gi, bi, ci: (gi, 0, 0))
    return pl.pallas_call(
        kern, name="ssd_bwd",
        out_shape=(jax.ShapeDtypeStruct((b, s, SSD_WIDTH), F32),
                   jax.ShapeDtypeStruct((b, s, g4 * SSD_STATE), F32),
                   jax.ShapeDtypeStruct((b, s, g4 * SSD_STATE), F32),
                   jax.ShapeDtypeStruct(dproj3.shape, dproj3.dtype),
                   jax.ShapeDtypeStruct((b, g4, s, LANES), F32),
                   jax.ShapeDtypeStruct((g4, 1, gw), F32),
                   jax.ShapeDtypeStruct((g4, 1, LANES), F32),
                   jax.ShapeDtypeStruct((g4, 1, LANES), F32),
                   jax.ShapeDtypeStruct((g4, 1, LANES), F32)),
        grid=(g4, b, nc // cps),
        in_specs=[xblk, xblk, xblk,
                  pl.BlockSpec((1, rows_per_step, LANES), lambda gi, bi, ci: (bi, rev(ci), SSD_WIDTH // LANES + gi)),
                  pl.BlockSpec((1, rows_per_step, LANES), lambda gi, bi, ci: (bi, rev(ci), SSD_WIDTH // LANES + g4 + gi)),
                  pl.BlockSpec((1, rows_per_step, gw), lambda gi, bi, ci: (bi, rev(ci), ZS0 // gw + gi)),
                  pl.BlockSpec((1, cps, 1, SSD_STATE, gw), lambda gi, bi, ci: (bi, rev(ci), gi, 0, 0)),
                  pl.BlockSpec((1, 1, rows_per_step, LANES), lambda gi, bi, ci: (bi, gi, rev(ci), 0)),
                  small, small, gvec, gvec, ANY],
        out_specs=(xblk, nblk, nblk,
                   pl.BlockSpec((1, rows_per_step, gw), lambda gi, bi, ci: (bi, rev(ci), ZS0 // gw + gi)),
                   pl.BlockSpec((1, 1, rows_per_step, LANES), lambda gi, bi, ci: (bi, gi, rev(ci), 0)),
                   gacc, small, small, small),
        input_output_aliases={12: 3},
        scratch_shapes=[pltpu.VMEM((SSD_STATE, gw), F32)],
        compiler_params=_cparams("parallel", "arbitrary", "arbitrary"),
    )(dyn3, y3, xact, xact, xact, proj3, hst, dtr_g, dtb_g, alog_g, dskip_x, snw, dproj3)


def _adamw(w, g, m, v, name):
    r, c = w.shape
    tr = 128 if r % 128 == 0 else r
    tc = LANES if (tr == r and r > 128 and c % LANES == 0) else c

    def kern(w_ref, g_ref, m_ref, v_ref, d_ref, nm_ref, nv_ref):
        gv = g_ref[...]
        nm = ADAM_B1 * m_ref[...] + (1.0 - ADAM_B1) * gv
        nv = ADAM_B2 * v_ref[...] + (1.0 - ADAM_B2) * (gv * gv)
        m_hat = nm / (1.0 - ADAM_B1 ** ADAM_STEP)
        v_hat = nv / (1.0 - ADAM_B2 ** ADAM_STEP)
        d_ref[...] = -ADAM_LR * (m_hat / (jnp.sqrt(v_hat) + ADAM_EPS) + ADAM_WD * w_ref[...])
        nm_ref[...] = nm
        nv_ref[...] = nv

    blk = pl.BlockSpec((tr, tc), lambda i, j: (i, j))
    out = jax.ShapeDtypeStruct((r, c), F32)
    return pl.pallas_call(
        kern, name=name, out_shape=(out, out, out), grid=(r // tr, c // tc),
        in_specs=[blk] * 4, out_specs=(blk, blk, blk),
        compiler_params=_cparams("parallel", "parallel"),
    )(w, g, m, v)


ANY = pl.BlockSpec(memory_space=pl.ANY)


def _position():
    return lax.axis_index("x"), lax.axis_index("y"), lax.axis_index("c")


def _other_chips(x, y):
    return [(1 - x, y), (x, 1 - y), (1 - x, 1 - y)]


def _dma_sems(n):
    return [pltpu.SemaphoreType.DMA((n,)), pltpu.SemaphoreType.DMA((n,))]


class _Exchange:
    def __init__(self, inputs, out_shapes, sems, start, finish):
        self.inputs, self.out_shapes, self.sems, self.start, self.finish = inputs, out_shapes, sems, start, finish


def _gather_exchange(shards):
    n = len(shards)

    def copies(p_refs, out_refs, sems):
        send_sems, recv_sems = sems
        x, y, c = _position()
        me = 2 * x + y
        chips = _other_chips(x, y)

        def slab(a, chip, hf):
            half = shards[a].shape[1] // 2
            return out_refs[a].at[chip, :, pl.ds(hf * half, half)]

        def my_half(a):
            half = shards[a].shape[1] // 2
            return p_refs[a].at[:, pl.ds(c * half, half)]

        def over_ici(a, j, chip_from):
            px, py = chips[j]
            return pltpu.make_async_remote_copy(
                src_ref=my_half(a), dst_ref=slab(a, chip_from, c),
                send_sem=send_sems.at[3 * a + j], recv_sem=recv_sems.at[3 * a + j],
                device_id=(px, py, c), device_id_type=MESH)

        def to_sibling(a, j, hf):
            px, py = chips[j]
            return pltpu.make_async_remote_copy(
                src_ref=slab(a, 2 * px + py, hf), dst_ref=slab(a, 2 * px + py, hf),
                send_sem=send_sems.at[3 * (n + a) + j], recv_sem=recv_sems.at[3 * (n + a) + j],
                device_id=(x, y, 1 - c), device_id_type=MESH)

        own = [pltpu.make_async_remote_copy(
            src_ref=p_refs[a], dst_ref=out_refs[a].at[me], send_sem=send_sems.at[6 * n + a], recv_sem=recv_sems.at[6 * n + a],
            device_id=(x, y, 1 - c), device_id_type=MESH) for a in range(n)]
        first = [over_ici(a, j, me) for a in range(n) for j in range(3)]
        return chips, c, over_ici, to_sibling, first, own

    def start(p_refs, out_refs, sems):
        _, _, _, _, first, own = copies(p_refs, out_refs, sems)
        for cp in first + own:
            cp.start()

    def finish(p_refs, out_refs, sems):
        chips, c, over_ici, to_sibling, first, own = copies(p_refs, out_refs, sems)
        passed = []
        for a in range(n):
            for j, (px, py) in enumerate(chips):
                over_ici(a, j, 2 * px + py).wait_recv()
                passed.append(to_sibling(a, j, c))
                passed[-1].start()
        for a in range(n):
            for j in range(3):
                to_sibling(a, j, 1 - c).wait_recv()
        for cp in first + passed:
            cp.wait_send()
        for cp in own:
            cp.wait()

    return _Exchange(list(shards), [jax.ShapeDtypeStruct((N_CHIPS, *v.shape), v.dtype) for v in shards],
                     _dma_sems(7 * n), start, finish)


def _swap_halves(parts, name):
    n = len(parts)

    def body(*refs):
        v_refs, out_refs = refs[:n], refs[n:2 * n]
        send_sems, recv_sems = refs[2 * n:]
        x, y, c = _position()
        copies = []
        for a in range(n):
            half = parts[a].shape[2] // 2
            copies.append(pltpu.make_async_remote_copy(
                src_ref=v_refs[a].at[:, :, pl.ds((1 - c) * half, half)], dst_ref=out_refs[a],
                send_sem=send_sems.at[a], recv_sem=recv_sems.at[a], device_id=(x, y, 1 - c), device_id_type=MESH))
        for cp in copies:
            cp.start()
        for cp in copies:
            cp.wait()

    return pl.pallas_call(
        body, name=name,
        out_shape=[jax.ShapeDtypeStruct((v.shape[0], v.shape[1], v.shape[2] // 2), v.dtype) for v in parts],
        in_specs=[ANY] * n, out_specs=[ANY] * n,
        scratch_shapes=_dma_sems(n),
    )(*parts)


def _all_to_all_exchange(parts):
    n = len(parts)

    def sends(p_refs, out_refs, sems):
        send_sems, recv_sems = sems
        x, y, c = _position()
        return [pltpu.make_async_remote_copy(
            src_ref=p_refs[a].at[2 * px + py], dst_ref=out_refs[a].at[j],
            send_sem=send_sems.at[3 * a + j], recv_sem=recv_sems.at[3 * a + j],
            device_id=(px, py, c), device_id_type=MESH) for a in range(n) for j, (px, py) in enumerate(_other_chips(x, y))]

    def start(p_refs, out_refs, sems):
        for cp in sends(p_refs, out_refs, sems):
            cp.start()

    def finish(p_refs, out_refs, sems):
        for cp in sends(p_refs, out_refs, sems):
            cp.wait()

    return _Exchange(list(parts), [jax.ShapeDtypeStruct((N_CHIPS - 1, *v.shape[1:]), v.dtype) for v in parts],
                     _dma_sems(3 * n), start, finish)


def _join_halves(wholes):
    n = len(wholes)

    def body(*refs):
        out_refs = refs[n:2 * n]
        send_sems, recv_sems = refs[2 * n:]
        x, y, c = _position()
        copies = []
        for a in range(n):
            half = wholes[a].shape[1] // 2
            mine = out_refs[a].at[:, pl.ds(c * half, half)]
            copies.append(pltpu.make_async_remote_copy(
                src_ref=mine, dst_ref=mine, send_sem=send_sems.at[a], recv_sem=recv_sems.at[a],
                device_id=(x, y, 1 - c), device_id_type=MESH))
        for cp in copies:
            cp.start()
        for cp in copies:
            cp.wait()

    return pl.pallas_call(
        body, name="grad_join_halves",
        out_shape=[jax.ShapeDtypeStruct(v.shape, v.dtype) for v in wholes],
        in_specs=[ANY] * n, out_specs=[ANY] * n,
        input_output_aliases={a: a for a in range(n)},
        scratch_shapes=_dma_sems(n),
    )(*wholes)


STRIP = 256


def _add_halves(g, sw, place, name):
    n, rows, cols = g.shape
    nb = cols // 2 // STRIP

    def kern(p_ref, g_ref, s_ref, o_ref):
        o_ref[...] = (g_ref[...] + s_ref[...]).astype(BF16)

    blk = pl.BlockSpec((1, rows, STRIP), lambda j, i, p_ref: (j, 0, i))
    return pl.pallas_call(
        kern, name=name,
        out_shape=jax.ShapeDtypeStruct((n, rows, cols // 2), BF16),
        grid_spec=pltpu.PrefetchScalarGridSpec(
            num_scalar_prefetch=1, grid=(n, nb),
            in_specs=[pl.BlockSpec((1, rows, STRIP), lambda j, i, p_ref: (j, 0, p_ref[0] * nb + i)), blk],
            out_specs=blk),
        compiler_params=_cparams("parallel", "parallel"),
    )(place, g, sw)


def _sum_chips(own, rx, place, name):
    _, rows, half = rx.shape
    nb = half // STRIP

    def kern(p_ref, own_ref, r_ref, o_ref):
        total = own_ref[0].astype(F32)
        for j in range(N_CHIPS - 1):
            total = total + r_ref[j].astype(F32)
        o_ref[...] = total

    return pl.pallas_call(
        kern, name=name,
        out_shape=jax.ShapeDtypeStruct((rows, 2 * half), F32),
        grid_spec=pltpu.PrefetchScalarGridSpec(
            num_scalar_prefetch=1, grid=(nb,),
            in_specs=[pl.BlockSpec((1, rows, STRIP), lambda i, p_ref: (p_ref[1], 0, i)),
                      pl.BlockSpec((N_CHIPS - 1, rows, STRIP), lambda i, p_ref: (0, 0, i))],
            out_specs=pl.BlockSpec((rows, STRIP), lambda i, p_ref: (0, p_ref[0] * nb + i))),
        compiler_params=_cparams("parallel"),
    )(place, own, rx)


def _gather_small(v, reduce, name):
    rows = v.shape[0]

    def body(v_ref, out_ref, buf, send_sems, recv_sems):
        x, y, c = _position()
        me = 4 * x + 2 * y + c
        buf[me] = v_ref[...]
        peers = [(x ^ (k >> 2), y ^ ((k >> 1) & 1), c ^ (k & 1)) for k in range(1, 8)]
        copies = [pltpu.make_async_remote_copy(
            src_ref=v_ref, dst_ref=buf.at[me],
            send_sem=send_sems.at[k], recv_sem=recv_sems.at[k],
            device_id=peer, device_id_type=MESH) for k, peer in enumerate(peers)]
        for cp in copies:
            cp.start()
        for k, (px, py, pc) in enumerate(peers):
            pltpu.make_async_remote_copy(
                src_ref=v_ref, dst_ref=buf.at[4 * px + 2 * py + pc],
                send_sem=send_sems.at[k], recv_sem=recv_sems.at[k],
                device_id=(px, py, pc), device_id_type=MESH).wait_recv()
        for cp in copies:
            cp.wait_send()
        if reduce:
            total = buf[0]
            for d in range(1, 8):
                total = total + buf[d]
            out_ref[...] = total
        else:
            out_ref[...] = buf[...]

    vm = pl.BlockSpec(memory_space=pltpu.VMEM)
    return pl.pallas_call(
        body, name=name,
        out_shape=jax.ShapeDtypeStruct((rows, LANES) if reduce else (8, rows, LANES), F32),
        in_specs=[vm], out_specs=vm,
        scratch_shapes=[pltpu.VMEM((8, rows, LANES), F32), pltpu.SemaphoreType.DMA((7,)), pltpu.SemaphoreType.DMA((7,))],
    )(v)


def _pad_rows(a, rows):
    return jnp.pad(a, ((0, rows - a.shape[0]), (0, 0)))


def _lane_pad(v):
    n = v.shape[1]
    return jnp.pad(v, ((0, 0), (0, -n % LANES)))


def _gather_all(w_in, w_attn_out, w_ssm_out, w_o, conv_w):
    d = D_MODEL
    w_proj_t = _gather_exchange([w_in[0].T.astype(BF16)])
    out_w = _gather_exchange([a[0].astype(BF16) for a in (w_attn_out, w_ssm_out, w_o)])
    conv_rows = conv_w[0].size // LANES
    conv_all = _gather_small(conv_w[0].reshape(conv_rows, LANES), False, "gather_conv_w")
    conv_w_all = conv_all[0::2].reshape(N_CHIPS, CONV_K, CONV_DIM // N_CHIPS).transpose(1, 0, 2).reshape(CONV_K, CONV_DIM)

    return w_proj_t, out_w, conv_w_all


def _local_step(x, loss_target, norm_w, w_proj_t, conv_w_all, conv_b, dt_bias, a_log, d_skip, ssm_norm_w,
                out_w, final_norm_w, grad_exchange=None):
    b, s, d = x.shape
    t = b * s
    g4, hg = SSD_GROUPS, HEADS_PER_GROUP
    dtb_g = _lane_pad(dt_bias.reshape(g4, hg)).reshape(g4, 1, LANES)
    alog_g = _lane_pad(a_log.reshape(g4, hg)).reshape(g4, 1, LANES)
    dskip_x = jnp.repeat(d_skip, HEAD_DIM, axis=1)
    fnw = final_norm_w.reshape(1, d)

    x2 = x.reshape(t, d)
    if isinstance(w_proj_t, _Exchange):
        h, w_in_t = _rms_fwd(x2, norm_w, exchange=w_proj_t)
        w_proj_t = _to_proj_layout(w_in_t.reshape(D_PROJ, d))
    else:
        h = _rms_fwd(x2, norm_w)
    big_tm = min(t, 2048)
    if isinstance(out_w, _Exchange):
        proj, *out_w = _matmul(h, w_proj_t, tb=True, tm=big_tm, tn=1280, tk=1024, name="proj", exchange=out_w)
    else:
        proj = _matmul(h, w_proj_t, tb=True, tm=big_tm, tn=1280, tk=1024, name="proj")
    w_ao, w_so, w_oo = (w.reshape(-1, d) for w in out_w)
    proj3 = proj.reshape(b, s, NP)
    o3, yp3 = _attn_fwd(proj3)
    xact = _conv_fwd(proj3, conv_w_all, conv_b)
    dtr = proj3[:, :, DT0:DT0 + g4 * hg].reshape(b, s, g4, hg).transpose(0, 2, 1, 3)
    dtr_g = jnp.pad(dtr, ((0, 0), (0, 0), (0, 0), (0, LANES - hg)))
    y3, yn3, hst = _ssd_fwd(xact, proj3, dtr_g, dtb_g, alog_g, dskip_x, ssm_norm_w)
    yp = yp3.reshape(t, D_MODEL)
    yn = yn3.reshape(t, SSD_WIDTH)
    ya = _matmul(yp, w_ao, tm=1024, tn=1024, tk=1024, name="attn_out")
    ys = _matmul(yn, w_so, tm=1024, tn=1024, tk=2048, name="ssm_out")
    merged = _merge_fwd(proj, ya, ys)
    mo = _matmul(merged, w_oo, tm=1024, tn=1024, tk=1024, name="out_proj")
    dout, doutb, loss_part, d_fnw = _final_fwd_bwd(x2, mo, loss_target.reshape(t, d), fnw)

    dmerged = _matmul(doutb, w_oo, tb=True, tm=1024, tn=1024, tk=1024, name="d_merged")
    g_wo = _matmul(merged, doutb, ta=True, tm=1024, tn=1024, tk=1024, name="g_w_o")
    dya, dys, dproj = _merge_bwd(dmerged, proj, ya, ys)
    dyp = _matmul(dya, w_ao, tb=True, tm=1024, tn=1024, tk=1024, name="d_attn_pre")
    g_wao = _matmul(yp, dya, ta=True, tm=1024, tn=1024, tk=1024, name="g_w_attn_out")
    dyn = _matmul(dys, w_so, tb=True, tm=1024, tn=2048, tk=1024, name="d_ssm_norm")
    g_wso = _matmul(yn, dys, ta=True, tm=1024, tn=1024, tk=1024, name="g_w_ssm_out")
    dproj3 = _attn_bwd(proj3, dyp.reshape(b, s, D_MODEL), o3, dproj.reshape(b, s, NP))
    (dxs, dbm, dcm, dproj3, ddtr_g, d_snw_g, d_alog_g, d_dtb_g, d_dsk_g) = _ssd_bwd(
        dyn.reshape(b, s, SSD_WIDTH), y3, xact, proj3, hst, dtr_g, dtb_g, alog_g, dskip_x, ssm_norm_w, dproj3)
    dproj3, g_cw_xs, g_cb_xs = _conv_bwd(dxs, proj3, conv_w_all, conv_b, 0, "conv_bwd_x", dproj3)
    dproj3, g_cw_bm, g_cb_bm = _conv_bwd(dbm, proj3, conv_w_all, conv_b, SSD_WIDTH, "conv_bwd_b", dproj3)
    dproj3, g_cw_cm, g_cb_cm = _conv_bwd(dcm, proj3, conv_w_all, conv_b, SSD_WIDTH + g4 * SSD_STATE, "conv_bwd_c", dproj3)
    ddt = ddtr_g[:, :, :, :hg].transpose(0, 2, 1, 3).reshape(b, s, g4 * hg).astype(BF16)
    ddt = jnp.pad(ddt, ((0, 0), (0, 0), (0, DT_PAD - g4 * hg)))
    dproj = lax.dynamic_update_slice(dproj3, ddt, (0, 0, DT0)).reshape(t, NP)
    exchanged = []
    if grad_exchange:
        g_wproj, *got = _matmul(dproj, h, ta=True, tm=1280, tn=1024, tk=1024, name="g_w_in",
                                exchange=grad_exchange([g_wao, g_wso, g_wo], "out"))
        exchanged += got
        dh, *got = _matmul(dproj, w_proj_t, tm=big_tm, tn=1024, tk=1280, name="d_h", exchange=grad_exchange([g_wproj], "in"))
        exchanged += got
    else:
        g_wproj = _matmul(dproj, h, ta=True, tm=1280, tn=1024, tk=1024, name="g_w_in")
        dh = _matmul(dproj, w_proj_t, tm=big_tm, tn=1024, tk=1280, name="d_h")
    grad_x, d_nw = _rms_bwd(dh, x2, norm_w, dout)
    g_cw = jnp.concatenate([g_cw_xs, g_cw_bm, g_cw_cm], axis=1)
    g_cb = jnp.concatenate([g_cb_xs, g_cb_bm, g_cb_cm], axis=1)
    return (loss_part, grad_x, d_nw, g_wproj, g_cw, g_cb, d_dtb_g, d_alog_g, d_dsk_g, d_snw_g, g_wao, g_wso, g_wo, d_fnw,
            exchanged)


def kernel(x, norm_w, w_in, conv_w, conv_b, dt_bias, a_log, d_skip, ssm_norm_w, w_attn_out, w_ssm_out, w_o, final_norm_w, loss_target, m_norm_w, m_w_in, m_conv_w, m_conv_b, m_dt_bias, m_a_log, m_d_skip, m_ssm_norm_w, m_w_attn_out, m_w_ssm_out, m_w_o, m_final_norm_w, v_norm_w, v_w_in, v_conv_w, v_conv_b, v_dt_bias, v_a_log, v_d_skip, v_ssm_norm_w, v_w_attn_out, v_w_ssm_out, v_w_o, v_final_norm_w):
    b, s, d = x.shape
    core = lax.axis_index("c")
    g4, hg = SSD_GROUPS, HEADS_PER_GROUP
    shard_cols = w_in.shape[2]
    w_proj_t, out_w, conv_w_all = _gather_all(w_in, w_attn_out, w_ssm_out, w_o, conv_w)
    chip = 2 * lax.axis_index("x") + lax.axis_index("y")
    place = jnp.stack([core, chip]).astype(jnp.int32)
    chip_sums = []

    def grad_exchange(grads, which):
        if which == "in":
            slabs = _from_proj_layout(grads[0]).reshape(N_CHIPS, shard_cols, d)
        else:
            slabs = jnp.concatenate([g.reshape(N_CHIPS, -1, d) for g in grads], axis=1)
        from_sibling, = _swap_halves([slabs], "grad_swap_halves_" + which)
        chip_sums.append(_add_halves(slabs, from_sibling, place, "grad_add_halves_" + which))
        return _all_to_all_exchange(chip_sums[-1:])

    (loss_part, grad_x, d_nw, _, g_cw, g_cb, d_dtb_g, d_alog_g, d_dsk_g, d_snw_g, _, _, _, d_fnw, from_chips) = _local_step(
        x, loss_target, norm_w, w_proj_t, conv_w_all, conv_b, dt_bias, a_log, d_skip, ssm_norm_w, out_w, final_norm_w,
        grad_exchange)
    wholes = [_sum_chips(o, r, place, "grad_sum_chips_%d" % i) for i, (o, r) in enumerate(zip(chip_sums, from_chips))]
    g_out, g_w_in = _join_halves(wholes)

    small = jnp.concatenate([
        loss_part, d_nw, g_cb, _lane_pad(d_dtb_g[:, 0, :hg].reshape(1, -1)), _lane_pad(d_alog_g[:, 0, :hg].reshape(1, -1)),
        _lane_pad(d_dsk_g[:, 0, :hg].reshape(1, -1)),
        d_snw_g.reshape(1, -1), d_fnw, g_cw.reshape(1, -1)], axis=1)
    small_rows = small.shape[1] // LANES
    reduced = _gather_small(_pad_rows(small.reshape(small_rows, LANES), -(-small_rows // 8) * 8), True, "reduce_small")
    flat = reduced.reshape(-1)

    def take(start, n):
        return flat[start:start + n].reshape(1, n)

    loss = flat[0]
    pos = LANES
    g_norm_w = take(pos, d); pos += d
    g_conv_b = take(pos, CONV_DIM); pos += CONV_DIM
    g_dt_bias = take(pos, g4 * hg); pos += LANES
    g_a_log = take(pos, g4 * hg); pos += LANES
    g_d_skip = take(pos, g4 * hg); pos += LANES
    g_ssm_norm_w = take(pos, SSD_WIDTH); pos += SSD_WIDTH
    g_final_norm_w = take(pos, d); pos += d
    conv_cols = CONV_DIM // N_CHIPS
    g_conv_w = lax.dynamic_slice_in_dim(flat[pos:pos + CONV_K * CONV_DIM].reshape(CONV_K, CONV_DIM), chip * conv_cols, conv_cols, axis=1)

    rows_ao, rows_so = D_MODEL // N_CHIPS, SSD_WIDTH // N_CHIPS
    g_w_attn_out = g_out[:rows_ao]
    g_w_ssm_out = g_out[rows_ao:rows_ao + rows_so]
    g_w_o = g_out[rows_ao + rows_so:]

    names = ["norm_w", "w_in", "conv_w", "conv_b", "dt_bias", "a_log", "d_skip", "ssm_norm_w",
             "w_attn_out", "w_ssm_out", "w_o", "final_norm_w"]
    weights = [norm_w, w_in, conv_w, conv_b, dt_bias, a_log, d_skip, ssm_norm_w, w_attn_out, w_ssm_out, w_o, final_norm_w]
    grads = [g_norm_w, g_w_in, g_conv_w, g_conv_b, g_dt_bias, g_a_log, g_d_skip, g_ssm_norm_w,
             g_w_attn_out, g_w_ssm_out, g_w_o, g_final_norm_w]
    ms = [m_norm_w, m_w_in, m_conv_w, m_conv_b, m_dt_bias, m_a_log, m_d_skip, m_ssm_norm_w,
          m_w_attn_out, m_w_ssm_out, m_w_o, m_final_norm_w]
    vs = [v_norm_w, v_w_in, v_conv_w, v_conv_b, v_dt_bias, v_a_log, v_d_skip, v_ssm_norm_w,
          v_w_attn_out, v_w_ssm_out, v_w_o, v_final_norm_w]
    out_g, out_d, out_m, out_v = [], [], [], []
    for name, w, g, m, v in zip(names, weights, grads, ms, vs):
        if name == "w_in":
            to2, back = (lambda a: a[0].T), (lambda a: a.T.reshape(w.shape))
        else:
            to2, back = (lambda a: a.reshape(g.shape)), (lambda a: a.reshape(w.shape))
        dlt, nm, nv = _adamw(to2(w), g, to2(m), to2(v), "adamw_" + name)
        out_g.append(back(g))
        out_d.append(back(dlt))
        out_m.append(back(nm))
        out_v.append(back(nv))

    return (loss, grad_x.reshape(b, s, d), *out_g, *out_d, *out_m, *out_v)
```

```python
import jax
import jax.numpy as jnp
from jax import lax
from jax.experimental import pallas as pl
from jax.experimental.pallas import tpu as pltpu

F32 = jnp.float32
BF16 = jnp.bfloat16
MESH = pl.DeviceIdType.MESH

D_MODEL = 1024
SB_HEADS = 16
HEAD_DIM = 64
SSD_WIDTH = 2048
SSD_GROUPS = 4
GROUP_WIDTH = SSD_WIDTH // SSD_GROUPS
HEADS_PER_GROUP = 8
SSD_STATE = 128
CHUNK = 128
CONV_K = 4
CONV_DIM = 3072
D_PROJ = 11296
EPS = 1e-6
ADAM_LR, ADAM_B1, ADAM_B2, ADAM_EPS, ADAM_WD, ADAM_STEP = 0.001, 0.9, 0.999, 1e-08, 0.01, 10

LANES = 128
HP_WIDTH = 4 * LANES
ZS0, GATE0, XBC0, DT0 = 4096, 6144, 8192, 11264
DT_PAD = 256
NP = DT0 + DT_PAD
N_CHIPS = 4
VMEM_LIMIT = 56 * 1024 * 1024


N_HP = SB_HEADS // 2
W_ZS0, W_XBC0, W_DT0, W_GATE0 = 4096, 6144, 9216, 9248


def _to_proj_layout(wt):
    d = wt.shape[1]
    pairs = wt[:W_ZS0].reshape(4, N_HP, LANES, d).transpose(1, 0, 2, 3).reshape(W_ZS0, d)
    return jnp.concatenate([pairs, wt[W_ZS0:W_XBC0], wt[W_GATE0:], wt[W_XBC0:W_DT0], wt[W_DT0:W_GATE0],
                            jnp.zeros((NP - D_PROJ, d), wt.dtype)], axis=0)


def _from_proj_layout(gt):
    d = gt.shape[1]
    qkvz = gt[:ZS0].reshape(N_HP, 4, LANES, d).transpose(1, 0, 2, 3).reshape(ZS0, d)
    return jnp.concatenate([qkvz, gt[ZS0:GATE0], gt[XBC0:DT0], gt[DT0:DT0 + W_GATE0 - W_DT0], gt[GATE0:XBC0]], axis=0)


def _cparams(*sem):
    return pltpu.CompilerParams(dimension_semantics=sem or None, vmem_limit_bytes=VMEM_LIMIT)


def _sigmoid(z):
    return 1.0 / (1.0 + jnp.exp(-z))


def _dot(a, b, dims, precision=None):
    return lax.dot_general(a, b, (dims, ((), ())), preferred_element_type=F32, precision=precision)


NN = ((1,), (0,))
NT = ((1,), (1,))
TN = ((0,), (0,))


def _matmul(a, b, *, ta=False, tb=False, out_dtype=F32, tm, tn, tk, name, exchange=None):
    m, k = (a.shape[1], a.shape[0]) if ta else a.shape
    n = b.shape[0] if tb else b.shape[1]
    assert m % tm == 0 and n % tn == 0 and k % tk == 0, (name, m, n, k)
    grid = (m // tm, n // tn, k // tk)
    nk = grid[2]
    use_scratch = out_dtype != F32
    dims = ((0,) if ta else (1,), (1,) if tb else (0,))
    n_in = len(exchange.inputs) if exchange else 0
    n_out = len(exchange.out_shapes) if exchange else 0

    def kern(a_ref, b_ref, *rest):
        x_in, o_ref, x_out, scratch = rest[:n_in], rest[n_in], rest[n_in + 1:n_in + 1 + n_out], rest[n_in + 1 + n_out:]
        acc = scratch[0] if use_scratch else o_ref
        step = [pl.program_id(ax) for ax in range(3)]
        if exchange:
            sems = scratch[1:] if use_scratch else scratch

            @pl.when(jnp.logical_and(jnp.logical_and(step[0] == 0, step[1] == 0), step[2] == 0))
            def _():
                exchange.start(x_in, x_out, sems)

        @pl.when(step[2] == 0)
        def _():
            acc[...] = jnp.zeros_like(acc)

        acc[...] += _dot(a_ref[...], b_ref[...], dims)
        if use_scratch:
            @pl.when(step[2] == nk - 1)
            def _():
                o_ref[...] = acc[...].astype(out_dtype)
        if exchange:
            @pl.when(jnp.logical_and(jnp.logical_and(step[0] == grid[0] - 1, step[1] == grid[1] - 1), step[2] == nk - 1))
            def _():
                exchange.finish(x_in, x_out, sems)

    a_spec = pl.BlockSpec((tk, tm), lambda i, j, q: (q, i)) if ta else pl.BlockSpec((tm, tk), lambda i, j, q: (i, q))
    b_spec = pl.BlockSpec((tn, tk), lambda i, j, q: (j, q)) if tb else pl.BlockSpec((tk, tn), lambda i, j, q: (q, j))
    out = pl.pallas_call(
        kern, name=name,
        out_shape=[jax.ShapeDtypeStruct((m, n), out_dtype)] + (list(exchange.out_shapes) if exchange else []),
        grid=grid,
        in_specs=[a_spec, b_spec] + [ANY] * n_in,
        out_specs=[pl.BlockSpec((tm, tn), lambda i, j, q: (i, j))] + [ANY] * n_out,
        scratch_shapes=([pltpu.VMEM((tm, tn), F32)] if use_scratch else []) + (list(exchange.sems) if exchange else []),
        compiler_params=_cparams("arbitrary", "arbitrary", "arbitrary") if exchange else _cparams("parallel", "parallel", "arbitrary"),
    )(a, b, *(exchange.inputs if exchange else []))
    return out if exchange else out[0]


ROWS = 512


def _rms_fwd(x2, w, exchange=None):
    t, d = x2.shape
    steps = t // ROWS
    n_in = len(exchange.inputs) if exchange else 0
    n_out = len(exchange.out_shapes) if exchange else 0

    def kern(x_ref, w_ref, *rest):
        x_in, h_ref, x_out, sems = rest[:n_in], rest[n_in], rest[n_in + 1:n_in + 1 + n_out], rest[n_in + 1 + n_out:]
        if exchange:
            @pl.when(pl.program_id(0) == 0)
            def _():
                exchange.start(x_in, x_out, sems)

        x = x_ref[...]
        r = lax.rsqrt(jnp.mean(x * x, axis=-1, keepdims=True) + EPS)
        h_ref[...] = (x * r * w_ref[...]).astype(BF16)
        if exchange:
            @pl.when(pl.program_id(0) == steps - 1)
            def _():
                exchange.finish(x_in, x_out, sems)

    out = pl.pallas_call(
        kern, name="rms_fwd",
        out_shape=[jax.ShapeDtypeStruct((t, d), BF16)] + (list(exchange.out_shapes) if exchange else []),
        grid=(steps,),
        in_specs=[pl.BlockSpec((ROWS, d), lambda i: (i, 0)), pl.BlockSpec((1, d), lambda i: (0, 0))] + [ANY] * n_in,
        out_specs=[pl.BlockSpec((ROWS, d), lambda i: (i, 0))] + [ANY] * n_out,
        scratch_shapes=list(exchange.sems) if exchange else [],
        compiler_params=_cparams("arbitrary" if exchange else "parallel"),
    )(x2, w, *(exchange.inputs if exchange else []))
    return out if exchange else out[0]


def _rms_bwd(dh, x2, w, dout):
    t, d = x2.shape

    def kern(dh_ref, x_ref, w_ref, dout_ref, gx_ref, dw_ref):
        @pl.when(pl.program_id(0) == 0)
        def _():
            dw_ref[...] = jnp.zeros_like(dw_ref)

        x = x_ref[...]
        r = lax.rsqrt(jnp.mean(x * x, axis=-1, keepdims=True) + EPS)
        xh = x * r
        g = dh_ref[...]
        dw_ref[...] += jnp.sum(g * xh, axis=0, keepdims=True)
        gw = g * w_ref[...]
        gx_ref[...] = dout_ref[...] + r * (gw - xh * jnp.mean(gw * xh, axis=-1, keepdims=True))

    row = pl.BlockSpec((ROWS, d), lambda i: (i, 0))
    vec = pl.BlockSpec((1, d), lambda i: (0, 0))
    return pl.pallas_call(
        kern, name="rms_bwd",
        out_shape=(jax.ShapeDtypeStruct((t, d), F32), jax.ShapeDtypeStruct((1, d), F32)),
        grid=(t // ROWS,),
        in_specs=[row, row, vec, row],
        out_specs=(row, vec),
        compiler_params=_cparams("arbitrary"),
    )(dh, x2, w, dout)


def _final_fwd_bwd(x2, mo, target, w):
    t, d = x2.shape

    def kern(x_ref, mo_ref, t_ref, w_ref, dout_ref, doutb_ref, loss_ref, dw_ref):
        @pl.when(pl.program_id(0) == 0)
        def _():
            loss_ref[...] = jnp.zeros_like(loss_ref)
            dw_ref[...] = jnp.zeros_like(dw_ref)

        u = x_ref[...] + mo_ref[...]
        r = lax.rsqrt(jnp.mean(u * u, axis=-1, keepdims=True) + EPS)
        uh = u * r
        wv = w_ref[...]
        err = uh * wv - t_ref[...]
        loss_ref[...] += (0.5 / d) * jnp.sum(err * err)
        dy = err * (1.0 / d)
        dw_ref[...] += jnp.sum(dy * uh, axis=0, keepdims=True)
        gw = dy * wv
        du = r * (gw - uh * jnp.mean(gw * uh, axis=-1, keepdims=True))
        dout_ref[...] = du
        doutb_ref[...] = du.astype(BF16)

    row = pl.BlockSpec((ROWS, d), lambda i: (i, 0))
    vec = pl.BlockSpec((1, d), lambda i: (0, 0))
    return pl.pallas_call(
        kern, name="final_fwd_bwd",
        out_shape=(jax.ShapeDtypeStruct((t, d), F32), jax.ShapeDtypeStruct((t, d), BF16),
                   jax.ShapeDtypeStruct((1, LANES), F32), jax.ShapeDtypeStruct((1, d), F32)),
        grid=(t // ROWS,),
        in_specs=[row, row, row, vec],
        out_specs=(row, row, pl.BlockSpec((1, LANES), lambda i: (0, 0)), vec),
        compiler_params=_cparams("arbitrary"),
    )(x2, mo, target, w)


def _merge_fwd(proj2, ya, ys):
    t = ya.shape[0]
    gblk = GATE0 // D_MODEL

    def kern(ga_ref, gs_ref, ya_ref, ys_ref, o_ref):
        o_ref[...] = (_sigmoid(ga_ref[...]) * ya_ref[...] + _sigmoid(gs_ref[...]) * ys_ref[...]).astype(BF16)

    row = pl.BlockSpec((ROWS, D_MODEL), lambda i: (i, 0))
    return pl.pallas_call(
        kern, name="merge_fwd",
        out_shape=jax.ShapeDtypeStruct((t, D_MODEL), BF16),
        grid=(t // ROWS,),
        in_specs=[pl.BlockSpec((ROWS, D_MODEL), lambda i: (i, gblk)),
                  pl.BlockSpec((ROWS, D_MODEL), lambda i: (i, gblk + 1)), row, row],
        out_specs=row,
        compiler_params=_cparams("parallel"),
    )(proj2, proj2, ya, ys)


def _merge_bwd(dm, proj2, ya, ys):
    t = ya.shape[0]
    gblk = GATE0 // D_MODEL

    def kern(dm_ref, ga_ref, gs_ref, ya_ref, ys_ref, dya_ref, dys_ref, dg_ref):
        g = dm_ref[...]
        sa = _sigmoid(ga_ref[...])
        ss = _sigmoid(gs_ref[...])
        dya_ref[...] = (g * sa).astype(BF16)
        dys_ref[...] = (g * ss).astype(BF16)
        dg_ref[:, :D_MODEL] = (g * ya_ref[...] * sa * (1.0 - sa)).astype(BF16)
        dg_ref[:, D_MODEL:] = (g * ys_ref[...] * ss * (1.0 - ss)).astype(BF16)

    row = pl.BlockSpec((ROWS, D_MODEL), lambda i: (i, 0))
    return pl.pallas_call(
        kern, name="merge_bwd",
        out_shape=(jax.ShapeDtypeStruct((t, D_MODEL), BF16), jax.ShapeDtypeStruct((t, D_MODEL), BF16),
                   jax.ShapeDtypeStruct((t, NP), BF16)),
        grid=(t // ROWS,),
        in_specs=[row, pl.BlockSpec((ROWS, D_MODEL), lambda i: (i, gblk)),
                  pl.BlockSpec((ROWS, D_MODEL), lambda i: (i, gblk + 1)), row, row],
        out_specs=(row, row, pl.BlockSpec((ROWS, 2 * D_MODEL), lambda i: (i, GATE0 // (2 * D_MODEL)))),
        compiler_params=_cparams("parallel"),
    )(dm, proj2, proj2, ya, ys)


TQ = 256
TK = 256
assert TQ == TK
HEAD_LANES = (slice(0, HEAD_DIM), slice(HEAD_DIM, 2 * HEAD_DIM))


def _tri(pred):
    r = lax.broadcasted_iota(jnp.int32, (TK, TK), 0)
    c = lax.broadcasted_iota(jnp.int32, (TK, TK), 1)
    return pred(r, c).astype(BF16)


def _split_bf16(v):
    hi = v.astype(BF16)
    lo = (v - hi.astype(F32)).astype(BF16)
    return hi, lo


def _tri_dot(v, tri):
    hi, lo = _split_bf16(v)
    return _dot(hi, tri, NN) + _dot(lo, tri, NN)


def _sb_logs(z, mask):
    l1p = jnp.log(1.0 + jnp.exp(-jnp.abs(z)))
    lb = jnp.minimum(z, 0.0) - l1p
    lom = -jnp.maximum(z, 0.0) - l1p
    if mask is not None:
        lom = jnp.where(mask, lom, 0.0)
    return lb, lom


def _sb_weights(lb, later, carry_r, mask):
    a = jnp.exp(lb + (later + carry_r))
    if mask is not None:
        a = jnp.where(mask, a, 0.0)
    return a


DEAD = -104.0


def _while_alive(n, carry, step):
    def alive(cr):
        return jnp.max(jnp.maximum(cr[0][0], cr[1][0])) > DEAD

    def cond(state):
        jj, go, _ = state
        return jnp.logical_and(jj < n, go)

    def body(state):
        jj, _, cr = state
        cr = step(jj, cr)
        return jj + 1, alive(cr), cr

    return lax.while_loop(cond, body, (jnp.int32(0), alive(carry), carry))[2]


Q_LANES, K_LANES, V_LANES, ZA_LANES = (slice(i * LANES, (i + 1) * LANES) for i in range(4))


def _split_heads(dst, src, scale=None):
    for h, lanes in enumerate(HEAD_LANES):
        v = src[:, lanes]
        dst[h] = (v if scale is None else v * scale).astype(BF16)


def _attn_fwd(proj3):
    b, s, _ = proj3.shape
    nq = s // TQ
    scale = HEAD_DIM ** -0.5

    def kern(x_ref, o_ref, yp_ref, qs, ks, vs):
        _split_heads(qs, x_ref[0, :, Q_LANES], scale)
        _split_heads(ks, x_ref[0, :, K_LANES])
        _split_heads(vs, x_ref[0, :, V_LANES])
        za_ref = x_ref.at[:, :, ZA_LANES]
        row = lax.broadcasted_iota(jnp.int32, (TQ, TK), 0)
        col = lax.broadcasted_iota(jnp.int32, (TQ, TK), 1)
        tri_gt = _tri(lambda j, sk: j > sk)

        def q_block(i, _):
            top = isinstance(i, int)
            r0 = i * TQ if top else pl.multiple_of(i * TQ, TQ)
            qh = [qs[h, pl.ds(r0, TQ), :] for h in range(2)]

            def k_blocks(blocks, carry):
                nb = range(len(blocks))
                kh = [[ks[h, pl.ds(c0, TK), :] for h in range(2)] for c0, _ in blocks]
                vh = [[vs[h, pl.ds(c0, TK), :] for h in range(2)] for c0, _ in blocks]
                z = [[_dot(qh[h], kh[bl][h], NT) for h in range(2)] for bl in nb]
                logs = [[_sb_logs(z[bl][h], blocks[bl][1]) for h in range(2)] for bl in nb]
                later = [[_tri_dot(logs[bl][h][1], tri_gt) for h in range(2)] for bl in nb]
                out = []
                for h in range(2):
                    carry_r, acc = carry[h]
                    for bl in nb:
                        lb, lom = logs[bl][h]
                        a = _sb_weights(lb, later[bl][h], carry_r, blocks[bl][1])
                        acc = acc + _dot(a.astype(BF16), vh[bl][h], NN)
                        carry_r = carry_r + (later[bl][h][:, 0:1] + lom[:, 0:1])
                    out.append((carry_r, acc))
                return tuple(out)

            start = (jnp.zeros((TQ, 1), F32), jnp.zeros((TQ, HEAD_DIM), F32))
            diag = (r0, col < row)
            if top:
                carry = k_blocks([diag], (start, start))
            else:
                carry = k_blocks([diag, (pl.multiple_of(r0 - TK, TK), None)], (start, start))
                carry = _while_alive(i - 1, carry, lambda jj, cr: k_blocks([(pl.multiple_of((i - 2 - jj) * TK, TK), None)], cr))
            for (_, acc), lanes in zip(carry, HEAD_LANES):
                o_ref[0, pl.ds(r0, TQ), lanes] = acc
                za = za_ref[0, pl.ds(r0, TQ), lanes]
                yp_ref[0, pl.ds(r0, TQ), lanes] = (acc * (za * _sigmoid(za))).astype(BF16)
            return 0

        q_block(0, 0)
        lax.fori_loop(1, nq, q_block, 0)

    out_spec = pl.BlockSpec((1, s, LANES), lambda bi, hp: (bi, 0, hp))
    return pl.pallas_call(
        kern, name="attn_fwd",
        out_shape=(jax.ShapeDtypeStruct((b, s, D_MODEL), F32), jax.ShapeDtypeStruct((b, s, D_MODEL), BF16)),
        grid=(b, SB_HEADS // 2),
        in_specs=[pl.BlockSpec((1, s, HP_WIDTH), lambda bi, hp: (bi, 0, hp))],
        out_specs=(out_spec, out_spec),
        scratch_shapes=[pltpu.VMEM((2, s, HEAD_DIM), BF16)] * 3,
        compiler_params=_cparams("parallel", "parallel"),
    )(proj3)


def _attn_bwd(proj3, dyp3, o3, dproj3):
    b, s, _ = proj3.shape
    nq = s // TQ
    scale = HEAD_DIM ** -0.5

    def kern(x_ref, dyp_ref, o_ref, _, d_ref, qs, ks, vs, dos, dk_acc, dv_acc):
        _split_heads(qs, x_ref[0, :, Q_LANES], scale)
        _split_heads(ks, x_ref[0, :, K_LANES])
        _split_heads(vs, x_ref[0, :, V_LANES])
        dq_ref, dk_ref, dv_ref = (d_ref.at[:, :, lanes] for lanes in (Q_LANES, K_LANES, V_LANES))
        za = x_ref[0, :, ZA_LANES]
        sg = _sigmoid(za)
        dyp = dyp_ref[0]
        _split_heads(dos, dyp * (za * sg))
        d_ref[0, :, ZA_LANES] = (dyp * o_ref[0] * (sg * (1.0 + za * (1.0 - sg)))).astype(BF16)
        dk_acc[...] = jnp.zeros_like(dk_acc)
        dv_acc[...] = jnp.zeros_like(dv_acc)
        row = lax.broadcasted_iota(jnp.int32, (TQ, TK), 0)
        col = lax.broadcasted_iota(jnp.int32, (TQ, TK), 1)
        tri_gt = _tri(lambda j, sk: j > sk)
        tri_ge = _tri(lambda j, sk: j >= sk)

        def q_block(i, _):
            top = isinstance(i, int)
            r0 = i * TQ if top else pl.multiple_of(i * TQ, TQ)
            qh = [qs[h, pl.ds(r0, TQ), :] for h in range(2)]
            doh = [dos[h, pl.ds(r0, TQ), :] for h in range(2)]
            totals = [jnp.sum(doh[h].astype(F32) * o_ref[0, pl.ds(r0, TQ), lanes], axis=1, keepdims=True)
                      for h, lanes in enumerate(HEAD_LANES)]

            def k_blocks(blocks, carry):
                nb = range(len(blocks))
                kh = [[ks[h, pl.ds(c0, TK), :] for h in range(2)] for c0, _ in blocks]
                vh = [[vs[h, pl.ds(c0, TK), :] for h in range(2)] for c0, _ in blocks]
                z = [[_dot(qh[h], kh[bl][h], NT) for h in range(2)] for bl in nb]
                da = [[_dot(doh[h], vh[bl][h], NT) for h in range(2)] for bl in nb]
                logs = [[_sb_logs(z[bl][h], blocks[bl][1]) for h in range(2)] for bl in nb]
                later = [[_tri_dot(logs[bl][h][1], tri_gt) for h in range(2)] for bl in nb]
                ab, g, suffix = ([[None, None] for _ in nb] for _ in range(3))
                for h in range(2):
                    cr = carry[h][0]
                    for bl in nb:
                        a = _sb_weights(logs[bl][h][0], later[bl][h], cr, blocks[bl][1])
                        ab[bl][h] = a.astype(BF16)
                        g[bl][h] = da[bl][h] * ab[bl][h].astype(F32)
                        suffix[bl][h] = _tri_dot(g[bl][h], tri_ge)
                        cr = cr + (later[bl][h][:, 0:1] + logs[bl][h][1][:, 0:1])
                out = []
                for h in range(2):
                    _, carry_g, dq = carry[h]
                    cr = carry[h][0]
                    for bl in nb:
                        c0, mask = blocks[bl]
                        lb, lom = logs[bl][h]
                        dz = g[bl][h] - (g[bl][h] + (totals[h] - carry_g) - suffix[bl][h]) * jnp.exp(lb)
                        if mask is not None:
                            dz = jnp.where(mask, dz, 0.0)
                        dzb = dz.astype(BF16)
                        dk_acc[h, pl.ds(c0, TK), :] += _dot(dzb, qh[h], TN)
                        dv_acc[h, pl.ds(c0, TK), :] += _dot(ab[bl][h], doh[h], TN)
                        dq = dq + _dot(dzb, kh[bl][h], NN)
                        carry_g = carry_g + suffix[bl][h][:, 0:1]
                        cr = cr + (later[bl][h][:, 0:1] + lom[:, 0:1])
                    out.append((cr, carry_g, dq))
                return tuple(out)

            def k_block(c0, carry, mask):
                kh = [ks[h, pl.ds(c0, TK), :] for h in range(2)]
                vh = [vs[h, pl.ds(c0, TK), :] for h in range(2)]
                z = [_dot(qh[h], kh[h], NT) for h in range(2)]
                da = [_dot(doh[h], vh[h], NT) for h in range(2)]
                logs, later = [], []
                for h in range(2):
                    logs.append(_sb_logs(z[h], mask))
                    later.append(_tri_dot(logs[h][1], tri_gt))
                ab, g, suffix = [], [], []
                for h in range(2):
                    a = _sb_weights(logs[h][0], later[h], carry[h][0], mask)
                    ab.append(a.astype(BF16))
                    g.append(da[h] * ab[h].astype(F32))
                    suffix.append(_tri_dot(g[h], tri_ge))
                out = []
                for h in range(2):
                    carry_r, carry_g, dq = carry[h]
                    lb, lom = logs[h]
                    dz = g[h] - (g[h] + (totals[h] - carry_g) - suffix[h]) * jnp.exp(lb)
                    if mask is not None:
                        dz = jnp.where(mask, dz, 0.0)
                    dzb = dz.astype(BF16)
                    dk_acc[h, pl.ds(c0, TK), :] += _dot(dzb, qh[h], TN)
                    dv_acc[h, pl.ds(c0, TK), :] += _dot(ab[h], doh[h], TN)
                    out.append((carry_r + (later[h][:, 0:1] + lom[:, 0:1]), carry_g + suffix[h][:, 0:1],
                                dq + _dot(dzb, kh[h], NN)))
                return tuple(out)

            zero = jnp.zeros((TQ, 1), F32)
            start = (zero, zero, jnp.zeros((TQ, HEAD_DIM), F32))
            diag = (r0, col < row)
            if top:
                carry = k_block(r0, (start, start), col < row)
            else:
                carry = k_blocks([diag, (pl.multiple_of(r0 - TK, TK), None)], (start, start))
                carry = _while_alive(i - 1, carry, lambda jj, cr: k_block(pl.multiple_of((i - 2 - jj) * TK, TK), cr, None))
            for (_, _, dq), lanes in zip(carry, HEAD_LANES):
                dq_ref[0, pl.ds(r0, TQ), lanes] = (dq * scale).astype(BF16)
            return 0

        q_block(0, 0)
        lax.fori_loop(1, nq, q_block, 0)

        for h, lanes in enumerate(HEAD_LANES):
            dk_ref[0, :, lanes] = dk_acc[h].astype(BF16)
            dv_ref[0, :, lanes] = dv_acc[h].astype(BF16)

    plain = pl.BlockSpec((1, s, LANES), lambda bi, hp: (bi, 0, hp))
    pair = pl.BlockSpec((1, s, HP_WIDTH), lambda bi, hp: (bi, 0, hp))
    return pl.pallas_call(
        kern, name="attn_bwd",
        out_shape=jax.ShapeDtypeStruct(dproj3.shape, dproj3.dtype),
        grid=(b, SB_HEADS // 2),
        in_specs=[pair, plain, plain, ANY],
        out_specs=pair,
        input_output_aliases={3: 0},
        scratch_shapes=[pltpu.VMEM((2, s, HEAD_DIM), BF16)] * 4 + [pltpu.VMEM((2, s, HEAD_DIM), F32)] * 2,
        compiler_params=_cparams("parallel", "parallel"),
    )(proj3, dyp3, o3, dproj3)


CONV_COLS = 256
HALO = 8


def _conv_pre(xp, w_ref, b_ref, r0):
    pre = b_ref[...] + w_ref[CONV_K - 1:CONV_K, :] * xp[pl.ds(HALO + r0, CHUNK), :]
    for kk in range(1, CONV_K):
        pre = pre + w_ref[CONV_K - 1 - kk:CONV_K - kk, :] * xp[pl.ds(HALO + r0 - kk, CHUNK), :]
    return pre


def _conv_fwd(proj3, conv_w, conv_b):
    b, s, _ = proj3.shape
    nc = s // CHUNK

    def kern(x_ref, w_ref, b_ref, o_ref, xp):
        xp[0:HALO, :] = jnp.zeros((HALO, CONV_COLS), F32)
        xp[HALO:, :] = x_ref[0]
        for ci in range(nc):
            pre = _conv_pre(xp, w_ref, b_ref, ci * CHUNK)
            o_ref[0, ci * CHUNK:(ci + 1) * CHUNK, :] = pre * _sigmoid(pre)

    return pl.pallas_call(
        kern, name="conv_fwd",
        out_shape=jax.ShapeDtypeStruct((b, s, CONV_DIM), F32),
        grid=(CONV_DIM // CONV_COLS, b),
        in_specs=[pl.BlockSpec((1, s, CONV_COLS), lambda j, bi: (bi, 0, XBC0 // CONV_COLS + j)),
                  pl.BlockSpec((CONV_K, CONV_COLS), lambda j, bi: (0, j)),
                  pl.BlockSpec((1, CONV_COLS), lambda j, bi: (0, j))],
        out_specs=pl.BlockSpec((1, s, CONV_COLS), lambda j, bi: (bi, 0, j)),
        scratch_shapes=[pltpu.VMEM((s + HALO, CONV_COLS), F32)],
        compiler_params=_cparams("parallel", "parallel"),
    )(proj3, conv_w, conv_b)


def _conv_bwd(dact, proj3, conv_w, conv_b, col0, name, dproj3):
    b, s, width = dact.shape
    nc = s // CHUNK
    j0 = col0 // CONV_COLS

    def kern(da_ref, x_ref, w_ref, b_ref, _, dx_ref, dw_ref, db_ref, xp, dp):
        @pl.when(pl.program_id(1) == 0)
        def _():
            dw_ref[...] = jnp.zeros_like(dw_ref)
            db_ref[...] = jnp.zeros_like(db_ref)

        xp[0:HALO, :] = jnp.zeros((HALO, CONV_COLS), F32)
        xp[HALO:, :] = x_ref[0]
        dp[s:, :] = jnp.zeros((HALO, CONV_COLS), F32)
        for ci in range(nc):
            r0 = ci * CHUNK
            pre = _conv_pre(xp, w_ref, b_ref, r0)
            sg = _sigmoid(pre)
            dpre = da_ref[0, r0:r0 + CHUNK, :] * (sg * (1.0 + pre * (1.0 - sg)))
            dp[r0:r0 + CHUNK, :] = dpre
            db_ref[...] += jnp.sum(dpre, axis=0, keepdims=True)
            for kk in range(CONV_K):
                tap = CONV_K - 1 - kk
                dw_ref[tap:tap + 1, :] += jnp.sum(dpre * xp[pl.ds(HALO + r0 - kk, CHUNK), :], axis=0, keepdims=True)
        for ci in range(nc):
            r0 = ci * CHUNK
            dx = w_ref[CONV_K - 1:CONV_K, :] * dp[pl.ds(r0, CHUNK), :]
            for kk in range(1, CONV_K):
                dx = dx + w_ref[CONV_K - 1 - kk:CONV_K - kk, :] * dp[pl.ds(r0 + kk, CHUNK), :]
            dx_ref[0, r0:r0 + CHUNK, :] = dx.astype(BF16)

    return pl.pallas_call(
        kern, name=name,
        out_shape=(jax.ShapeDtypeStruct(dproj3.shape, dproj3.dtype), jax.ShapeDtypeStruct((CONV_K, width), F32),
                   jax.ShapeDtypeStruct((1, width), F32)),
        grid=(width // CONV_COLS, b),
        in_specs=[pl.BlockSpec((1, s, CONV_COLS), lambda j, bi: (bi, 0, j)),
                  pl.BlockSpec((1, s, CONV_COLS), lambda j, bi: (bi, 0, XBC0 // CONV_COLS + j0 + j)),
                  pl.BlockSpec((CONV_K, CONV_COLS), lambda j, bi: (0, j0 + j)),
                  pl.BlockSpec((1, CONV_COLS), lambda j, bi: (0, j0 + j)), ANY],
        out_specs=(pl.BlockSpec((1, s, CONV_COLS), lambda j, bi: (bi, 0, XBC0 // CONV_COLS + j0 + j)),
                   pl.BlockSpec((CONV_K, CONV_COLS), lambda j, bi: (0, j)),
                   pl.BlockSpec((1, CONV_COLS), lambda j, bi: (0, j))),
        input_output_aliases={4: 0},
        scratch_shapes=[pltpu.VMEM((s + HALO, CONV_COLS), F32)] * 2,
        compiler_params=_cparams("parallel", "arbitrary"),
    )(dact, proj3, conv_w, conv_b, dproj3)


SSD_CHUNKS_PER_STEP = 8


def _sel_dot(v, sel, left=False):
    hi = v.astype(BF16)
    rest = v - hi.astype(F32)
    mid = rest.astype(BF16)
    lo = (rest - mid.astype(F32)).astype(BF16)
    if left:
        return _dot(sel, hi, NN) + _dot(sel, mid, NN) + _dot(sel, lo, NN)
    return _dot(hi, sel, NN) + _dot(mid, sel, NN) + _dot(lo, sel, NN)


def _ssd_common(dtr, dtb, alog):
    lane = lax.broadcasted_iota(jnp.int32, (CHUNK, LANES), 1)
    row = lax.broadcasted_iota(jnp.int32, (CHUNK, LANES), 0)
    head_lane = lane < HEADS_PER_GROUP
    pre = dtr + dtb
    dt = jnp.where(head_lane, jnp.maximum(pre, 0.0) + jnp.log(1.0 + jnp.exp(-jnp.abs(pre))), 0.0)
    a = jnp.where(head_lane[0:1], -jnp.exp(alog), 0.0)
    tril = (row >= lane).astype(BF16)
    acs = _sel_dot(dt * a, tril, left=True)
    acs_t = acs.T
    er = lax.broadcasted_iota(jnp.int32, (LANES, GROUP_WIDTH), 0)
    ec = lax.broadcasted_iota(jnp.int32, (LANES, GROUP_WIDTH), 1)
    expand = ((ec // HEAD_DIM) == er).astype(BF16)
    tr = lax.broadcasted_iota(jnp.int32, (GROUP_WIDTH, LANES), 0)
    tc = lax.broadcasted_iota(jnp.int32, (GROUP_WIDTH, LANES), 1)
    reduce = ((tr // HEAD_DIM) == tc).astype(BF16)
    dt_x = _sel_dot(dt, expand)
    acs_x = _sel_dot(acs, expand)
    end_x = acs_x[CHUNK - 1:CHUNK, :]
    causal = row >= lane
    return dict(dt=dt, a=a, pre=pre, head_lane=head_lane, acs=acs, acs_t=acs_t, expand=expand, reduce=reduce,
                dt_x=dt_x, acs_x=acs_x, end_x=end_x, causal=causal, row=row, lane=lane)


def _ssd_decay(cm, h):
    seg = cm["acs"][:, h:h + 1] - cm["acs_t"][h:h + 1, :]
    return jnp.where(cm["causal"], jnp.exp(jnp.minimum(seg, 0.0)), 0.0)


def _ssd_fwd(xact, proj3, dtr_g, dtb_g, alog_g, dskip_x, snw):
    b, s, _ = xact.shape
    nc = s // CHUNK
    g4 = SSD_GROUPS
    cps = min(nc, SSD_CHUNKS_PER_STEP)
    rows_per_step = cps * CHUNK

    def kern(xs_ref, bm_ref, cm_ref, zs_ref, dtr_ref, dtb_ref, alog_ref, dsk_ref, snw_ref,
             y_ref, yn_ref, hst_ref, h_sc):
        @pl.when(pl.program_id(2) == 0)
        def _():
            h_sc[...] = jnp.zeros_like(h_sc)

        def chunk(ci, _):
            rows = pl.ds(pl.multiple_of(ci * CHUNK, CHUNK), CHUNK)
            cm = _ssd_common(dtr_ref[0, 0, rows, :], dtb_ref[0], alog_ref[0])
            x = xs_ref[0, rows, :]
            bmb = bm_ref[0, rows, :].astype(BF16)
            cmb = cm_ref[0, rows, :].astype(BF16)
            h_in = h_sc[...]
            hst_ref[0, ci, 0] = h_in
            xdt = x * cm["dt_x"]
            xdtb = xdt.astype(BF16)
            cb = _dot(cmb, bmb, NT)
            y_off = _dot(cmb, h_in.astype(BF16), NN) * jnp.exp(cm["acs_x"])
            for h in range(HEADS_PER_GROUP):
                lanes = slice(h * HEAD_DIM, (h + 1) * HEAD_DIM)
                m = (cb * _ssd_decay(cm, h)).astype(BF16)
                y_ref[0, rows, lanes] = _dot(m, xdtb[:, lanes], NN)
            y = y_ref[0, rows, :] + y_off + x * dsk_ref[...]
            y_ref[0, rows, :] = y
            w = (xdt * jnp.exp(cm["end_x"] - cm["acs_x"])).astype(BF16)
            h_sc[...] = h_in * jnp.exp(cm["end_x"]) + _dot(bmb, w, TN)
            zs = zs_ref[0, rows, :]
            y2 = y * (zs * _sigmoid(zs))
            yn_ref[0, rows, :] = (y2 * lax.rsqrt(jnp.mean(y2 * y2, axis=-1, keepdims=True) + EPS) * snw_ref[...]).astype(BF16)
            return 0

        lax.fori_loop(0, cps, chunk, 0, unroll=2)

    gw = GROUP_WIDTH
    small = pl.BlockSpec((1, 1, LANES), lambda gi, bi, ci: (gi, 0, 0))
    xblk = pl.BlockSpec((1, rows_per_step, gw), lambda gi, bi, ci: (bi, ci, gi))
    return pl.pallas_call(
        kern, name="ssd_fwd",
        out_shape=(jax.ShapeDtypeStruct((b, s, SSD_WIDTH), F32), jax.ShapeDtypeStruct((b, s, SSD_WIDTH), BF16),
                   jax.ShapeDtypeStruct((b, nc, g4, SSD_STATE, gw), F32)),
        grid=(g4, b, nc // cps),
        in_specs=[xblk,
                  pl.BlockSpec((1, rows_per_step, LANES), lambda gi, bi, ci: (bi, ci, SSD_WIDTH // LANES + gi)),
                  pl.BlockSpec((1, rows_per_step, LANES), lambda gi, bi, ci: (bi, ci, SSD_WIDTH // LANES + g4 + gi)),
                  pl.BlockSpec((1, rows_per_step, gw), lambda gi, bi, ci: (bi, ci, ZS0 // gw + gi)),
                  pl.BlockSpec((1, 1, rows_per_step, LANES), lambda gi, bi, ci: (bi, gi, ci, 0)),
                  small, small,
                  pl.BlockSpec((1, gw), lambda gi, bi, ci: (0, gi)),
                  pl.BlockSpec((1, gw), lambda gi, bi, ci: (0, gi))],
        out_specs=(xblk, xblk, pl.BlockSpec((1, cps, 1, SSD_STATE, gw), lambda gi, bi, ci: (bi, ci, gi, 0, 0))),
        scratch_shapes=[pltpu.VMEM((SSD_STATE, gw), F32)],
        compiler_params=_cparams("parallel", "parallel", "arbitrary"),
    )(xact, xact, xact, proj3, dtr_g, dtb_g, alog_g, dskip_x, snw)


def _ssd_bwd(dyn3, y3, xact, proj3, hst, dtr_g, dtb_g, alog_g, dskip_x, snw, dproj3):
    b, s, _ = xact.shape
    nc = s // CHUNK
    g4 = SSD_GROUPS
    gw = GROUP_WIDTH

    cps = min(nc, SSD_CHUNKS_PER_STEP)
    rows_per_step = cps * CHUNK

    def one_chunk(dyn_ref, y_ref, xs_ref, bm_ref, cm_ref, zs_ref, hst_ref, dtr_ref, dtb_ref, alog_ref, dsk_ref, snw_ref,
                  dxs_ref, dbm_ref, dcm_ref, dzs_ref, ddtr_ref, dsnw_ref, dalog_ref, ddtb_ref, ddsk_ref, dh_sc):
        cm = _ssd_common(dtr_ref[0, 0], dtb_ref[0], alog_ref[0])
        row, lane = cm["row"], cm["lane"]
        y = y_ref[0]
        zs = zs_ref[0]
        sg = _sigmoid(zs)
        silu = zs * sg
        y2 = y * silu
        rstd = lax.rsqrt(jnp.mean(y2 * y2, axis=-1, keepdims=True) + EPS)
        y2h = y2 * rstd
        dyn = dyn_ref[0]
        dsnw_ref[0] += jnp.sum(dyn * y2h, axis=0, keepdims=True)
        gwv = dyn * snw_ref[...]
        dy2 = rstd * (gwv - y2h * jnp.mean(gwv * y2h, axis=-1, keepdims=True))
        dzs_ref[0] = (dy2 * y * (sg * (1.0 + zs * (1.0 - sg)))).astype(BF16)
        dy = dy2 * silu
        dyb = dy.astype(BF16)

        x = xs_ref[0]
        bmb = bm_ref[0].astype(BF16)
        cmb = cm_ref[0].astype(BF16)
        h_in = hst_ref[0, 0, 0]
        h_inb = h_in.astype(BF16)
        d_hn = dh_sc[...]
        d_hnb = d_hn.astype(BF16)
        xdt = x * cm["dt_x"]
        xdtb = xdt.astype(BF16)
        eacs = jnp.exp(cm["acs_x"])
        dte = jnp.exp(cm["end_x"] - cm["acs_x"])
        wb = (xdt * dte).astype(BF16)

        dsk_lanes = jnp.broadcast_to(jnp.sum(dy * x, axis=0, keepdims=True), (8, gw))
        ddsk_ref[0] += _sel_dot(dsk_lanes, cm["reduce"])[0:1, :]
        dyo = dy * eacs
        dyob = dyo.astype(BF16)
        dacs_x = dyo * _dot(cmb, h_inb, NN)
        dcm = _dot(dyob, h_inb, NT)
        dh_in = _dot(cmb, dyob, TN)
        dw = _dot(bmb, d_hnb, NN)
        dbm = _dot(wb, d_hnb, NT)
        dxdt = dw * dte
        e_l = dw * xdt * dte
        dacs_x = dacs_x - e_l
        dend_x = jnp.sum(e_l, axis=0, keepdims=True)
        chunk_decay = jnp.exp(cm["end_x"])
        dh_sc[...] = d_hn * chunk_decay + dh_in
        dend_x = dend_x + jnp.sum(d_hn * h_in, axis=0, keepdims=True) * chunk_decay
        last_row = lax.broadcasted_iota(jnp.int32, (CHUNK, gw), 0) == CHUNK - 1
        dacs_x = dacs_x + jnp.where(last_row, dend_x, 0.0)

        cb = _dot(cmb, bmb, NT)
        dcb = jnp.zeros((CHUNK, CHUNK), F32)
        dacs = jnp.zeros((CHUNK, LANES), F32)
        dacs_t = jnp.zeros((LANES, CHUNK), F32)
        for h in range(HEADS_PER_GROUP):
            lanes = slice(h * HEAD_DIM, (h + 1) * HEAD_DIM)
            decay = _ssd_decay(cm, h)
            m = cb * decay
            dm = _dot(dyb[:, lanes], xdtb[:, lanes], NT)
            dxs_ref[0, :, lanes] = _dot(m.astype(BF16), dyb[:, lanes], TN)
            dcb_h = dm * decay
            dcb = dcb + dcb_h
            n = dcb_h * cb
            dacs = dacs + jnp.where(lane == h, jnp.sum(n, axis=1, keepdims=True), 0.0)
            dacs_t = dacs_t + jnp.where(row == h, jnp.sum(n, axis=0, keepdims=True), 0.0)
        dcbb = dcb.astype(BF16)
        dcm_ref[0] = dcm + _dot(dcbb, bmb, NN)
        dbm_ref[0] = dbm + _dot(dcbb, cmb, TN)
        dxdt = dxdt + dxs_ref[0]
        dxs_ref[0] = dy * dsk_ref[...] + dxdt * cm["dt_x"]

        dacs = dacs - dacs_t.T + _sel_dot(dacs_x, cm["reduce"])
        ddt = _sel_dot(dxdt * x, cm["reduce"])
        triu = (row <= lane).astype(BF16)
        rc = _sel_dot(dacs, triu, left=True)
        ddt = ddt + cm["a"] * rc
        dalog_ref[0] += jnp.sum(cm["dt"] * rc, axis=0, keepdims=True) * cm["a"]
        ddtr = jnp.where(cm["head_lane"], ddt * _sigmoid(cm["pre"]), 0.0)
        ddtr_ref[0, 0] = ddtr
        ddtb_ref[0] += jnp.sum(ddtr, axis=0, keepdims=True)

    def kern(dyn_ref, y_ref, xs_ref, bm_ref, cm_ref, zs_ref, hst_ref, dtr_ref, dtb_ref, alog_ref, dsk_ref, snw_ref, _,
             dxs_ref, dbm_ref, dcm_ref, dzs_ref, ddtr_ref, dsnw_ref, dalog_ref, ddtb_ref, ddsk_ref, dh_sc):
        first = jnp.logical_and(pl.program_id(1) == 0, pl.program_id(2) == 0)

        @pl.when(first)
        def _():
            dsnw_ref[...] = jnp.zeros_like(dsnw_ref)
            dalog_ref[...] = jnp.zeros_like(dalog_ref)
            ddtb_ref[...] = jnp.zeros_like(ddtb_ref)
            ddsk_ref[...] = jnp.zeros_like(ddsk_ref)

        @pl.when(pl.program_id(2) == 0)
        def _():
            dh_sc[...] = jnp.zeros_like(dh_sc)

        def chunk(k, _):
            ci = cps - 1 - k
            rows = pl.ds(pl.multiple_of(ci * CHUNK, CHUNK), CHUNK)
            by_rows = [r.at[:, rows, :] for r in (dyn_ref, y_ref, xs_ref, bm_ref, cm_ref, zs_ref)]
            one_chunk(*by_rows, hst_ref.at[:, pl.ds(ci, 1)], dtr_ref.at[:, :, rows, :], dtb_ref, alog_ref, dsk_ref, snw_ref,
                      *[r.at[:, rows, :] for r in (dxs_ref, dbm_ref, dcm_ref, dzs_ref)], ddtr_ref.at[:, :, rows, :],
                      dsnw_ref, dalog_ref, ddtb_ref, ddsk_ref, dh_sc)
            return 0

        lax.fori_loop(0, cps, chunk, 0, unroll=2)

    def rev(ci):
        return nc // cps - 1 - ci

    small = pl.BlockSpec((1, 1, LANES), lambda gi, bi, ci: (gi, 0, 0))
    xblk = pl.BlockSpec((1, rows_per_step, gw), lambda gi, bi, ci: (bi, rev(ci), gi))
    nblk = pl.BlockSpec((1, rows_per_step, LANES), lambda gi, bi, ci: (bi, rev(ci), gi))
    gvec = pl.BlockSpec((1, gw), lambda gi, bi, ci: (0, gi))
    gacc = pl.BlockSpec((1, 1, gw), lambda gi, bi, ci: (gi, 0, 0))
    return pl.pallas_call(
        kern, name="ssd_bwd",
        out_shape=(jax.ShapeDtypeStruct((b, s, SSD_WIDTH), F32),
                   jax.ShapeDtypeStruct((b, s, g4 * SSD_STATE), F32),
                   jax.ShapeDtypeStruct((b, s, g4 * SSD_STATE), F32),
                   jax.ShapeDtypeStruct(dproj3.shape, dproj3.dtype),
                   jax.ShapeDtypeStruct((b, g4, s, LANES), F32),
                   jax.ShapeDtypeStruct((g4, 1, gw), F32),
                   jax.ShapeDtypeStruct((g4, 1, LANES), F32),
                   jax.ShapeDtypeStruct((g4, 1, LANES), F32),
                   jax.ShapeDtypeStruct((g4, 1, LANES), F32)),
        grid=(g4, b, nc // cps),
        in_specs=[xblk, xblk, xblk,
                  pl.BlockSpec((1, rows_per_step, LANES), lambda gi, bi, ci: (bi, rev(ci), SSD_WIDTH // LANES + gi)),
                  pl.BlockSpec((1, rows_per_step, LANES), lambda gi, bi, ci: (bi, rev(ci), SSD_WIDTH // LANES + g4 + gi)),
                  pl.BlockSpec((1, rows_per_step, gw), lambda gi, bi, ci: (bi, rev(ci), ZS0 // gw + gi)),
                  pl.BlockSpec((1, cps, 1, SSD_STATE, gw), lambda gi, bi, ci: (bi, rev(ci), gi, 0, 0)),
                  pl.BlockSpec((1, 1, rows_per_step, LANES), lambda gi, bi, ci: (bi, gi, rev(ci), 0)),
                  small, small, gvec, gvec, ANY],
        out_specs=(xblk, nblk, nblk,
                   pl.BlockSpec((1, rows_per_step, gw), lambda gi, bi, ci: (bi, rev(ci), ZS0 // gw + gi)),
                   pl.BlockSpec((1, 1, rows_per_step, LANES), lambda gi, bi, ci: (bi, gi, rev(ci), 0)),
                   gacc, small, small, small),
        input_output_aliases={12: 3},
        scratch_shapes=[pltpu.VMEM((SSD_STATE, gw), F32)],
        compiler_params=_cparams("parallel", "arbitrary", "arbitrary"),
    )(dyn3, y3, xact, xact, xact, proj3, hst, dtr_g, dtb_g, alog_g, dskip_x, snw, dproj3)


def _adamw(w, g, m, v, name):
    r, c = w.shape
    tr = 128 if r % 128 == 0 else r
    tc = LANES if (tr == r and r > 128 and c % LANES == 0) else c

    def kern(w_ref, g_ref, m_ref, v_ref, d_ref, nm_ref, nv_ref):
        gv = g_ref[...]
        nm = ADAM_B1 * m_ref[...] + (1.0 - ADAM_B1) * gv
        nv = ADAM_B2 * v_ref[...] + (1.0 - ADAM_B2) * (gv * gv)
        m_hat = nm / (1.0 - ADAM_B1 ** ADAM_STEP)
        v_hat = nv / (1.0 - ADAM_B2 ** ADAM_STEP)
        d_ref[...] = -ADAM_LR * (m_hat / (jnp.sqrt(v_hat) + ADAM_EPS) + ADAM_WD * w_ref[...])
        nm_ref[...] = nm
        nv_ref[...] = nv

    blk = pl.BlockSpec((tr, tc), lambda i, j: (i, j))
    out = jax.ShapeDtypeStruct((r, c), F32)
    return pl.pallas_call(
        kern, name=name, out_shape=(out, out, out), grid=(r // tr, c // tc),
        in_specs=[blk] * 4, out_specs=(blk, blk, blk),
        compiler_params=_cparams("parallel", "parallel"),
    )(w, g, m, v)


ANY = pl.BlockSpec(memory_space=pl.ANY)


def _position():
    return lax.axis_index("x"), lax.axis_index("y"), lax.axis_index("c")


def _other_chips(x, y):
    return [(1 - x, y), (x, 1 - y), (1 - x, 1 - y)]


def _dma_sems(n):
    return [pltpu.SemaphoreType.DMA((n,)), pltpu.SemaphoreType.DMA((n,))]


class _Exchange:
    def __init__(self, inputs, out_shapes, sems, start, finish):
        self.inputs, self.out_shapes, self.sems, self.start, self.finish = inputs, out_shapes, sems, start, finish


def _gather_exchange(shards):
    n = len(shards)

    def copies(p_refs, out_refs, sems):
        send_sems, recv_sems = sems
        x, y, c = _position()
        me = 2 * x + y
        chips = _other_chips(x, y)

        def slab(a, chip, hf):
            half = shards[a].shape[1] // 2
            return out_refs[a].at[chip, :, pl.ds(hf * half, half)]

        def my_half(a):
            half = shards[a].shape[1] // 2
            return p_refs[a].at[:, pl.ds(c * half, half)]

        def over_ici(a, j, chip_from):
            px, py = chips[j]
            return pltpu.make_async_remote_copy(
                src_ref=my_half(a), dst_ref=slab(a, chip_from, c),
                send_sem=send_sems.at[3 * a + j], recv_sem=recv_sems.at[3 * a + j],
                device_id=(px, py, c), device_id_type=MESH)

        def to_sibling(a, j, hf):
            px, py = chips[j]
            return pltpu.make_async_remote_copy(
                src_ref=slab(a, 2 * px + py, hf), dst_ref=slab(a, 2 * px + py, hf),
                send_sem=send_sems.at[3 * (n + a) + j], recv_sem=recv_sems.at[3 * (n + a) + j],
                device_id=(x, y, 1 - c), device_id_type=MESH)

        own = [pltpu.make_async_remote_copy(
            src_ref=p_refs[a], dst_ref=out_refs[a].at[me], send_sem=send_sems.at[6 * n + a], recv_sem=recv_sems.at[6 * n + a],
            device_id=(x, y, 1 - c), device_id_type=MESH) for a in range(n)]
        first = [over_ici(a, j, me) for a in range(n) for j in range(3)]
        return chips, c, over_ici, to_sibling, first, own

    def start(p_refs, out_refs, sems):
        _, _, _, _, first, own = copies(p_refs, out_refs, sems)
        for cp in first + own:
            cp.start()

    def finish(p_refs, out_refs, sems):
        chips, c, over_ici, to_sibling, first, own = copies(p_refs, out_refs, sems)
        passed = []
        for a in range(n):
            for j, (px, py) in enumerate(chips):
                over_ici(a, j, 2 * px + py).wait_recv()
                passed.append(to_sibling(a, j, c))
                passed[-1].start()
        for a in range(n):
            for j in range(3):
                to_sibling(a, j, 1 - c).wait_recv()
        for cp in first + passed:
            cp.wait_send()
        for cp in own:
            cp.wait()

    return _Exchange(list(shards), [jax.ShapeDtypeStruct((N_CHIPS, *v.shape), v.dtype) for v in shards],
                     _dma_sems(7 * n), start, finish)


def _swap_halves(parts, name):
    n = len(parts)

    def body(*refs):
        v_refs, out_refs = refs[:n], refs[n:2 * n]
        send_sems, recv_sems = refs[2 * n:]
        x, y, c = _position()
        copies = []
        for a in range(n):
            half = parts[a].shape[2] // 2
            copies.append(pltpu.make_async_remote_copy(
                src_ref=v_refs[a].at[:, :, pl.ds((1 - c) * half, half)], dst_ref=out_refs[a],
                send_sem=send_sems.at[a], recv_sem=recv_sems.at[a], device_id=(x, y, 1 - c), device_id_type=MESH))
        for cp in copies:
            cp.start()
        for cp in copies:
            cp.wait()

    return pl.pallas_call(
        body, name=name,
        out_shape=[jax.ShapeDtypeStruct((v.shape[0], v.shape[1], v.shape[2] // 2), v.dtype) for v in parts],
        in_specs=[ANY] * n, out_specs=[ANY] * n,
        scratch_shapes=_dma_sems(n),
    )(*parts)


def _all_to_all_exchange(parts):
    n = len(parts)

    def sends(p_refs, out_refs, sems):
        send_sems, recv_sems = sems
        x, y, c = _position()
        return [pltpu.make_async_remote_copy(
            src_ref=p_refs[a].at[2 * px + py], dst_ref=out_refs[a].at[j],
            send_sem=send_sems.at[3 * a + j], recv_sem=recv_sems.at[3 * a + j],
            device_id=(px, py, c), device_id_type=MESH) for a in range(n) for j, (px, py) in enumerate(_other_chips(x, y))]

    def start(p_refs, out_refs, sems):
        for cp in sends(p_refs, out_refs, sems):
            cp.start()

    def finish(p_refs, out_refs, sems):
        for cp in sends(p_refs, out_refs, sems):
            cp.wait()

    return _Exchange(list(parts), [jax.ShapeDtypeStruct((N_CHIPS - 1, *v.shape[1:]), v.dtype) for v in parts],
                     _dma_sems(3 * n), start, finish)


def _join_halves(wholes):
    n = len(wholes)

    def body(*refs):
        out_refs = refs[n:2 * n]
        send_sems, recv_sems = refs[2 * n:]
        x, y, c = _position()
        copies = []
        for a in range(n):
            half = wholes[a].shape[1] // 2
            mine = out_refs[a].at[:, pl.ds(c * half, half)]
            copies.append(pltpu.make_async_remote_copy(
                src_ref=mine, dst_ref=mine, send_sem=send_sems.at[a], recv_sem=recv_sems.at[a],
                device_id=(x, y, 1 - c), device_id_type=MESH))
        for cp in copies:
            cp.start()
        for cp in copies:
            cp.wait()

    return pl.pallas_call(
        body, name="grad_join_halves",
        out_shape=[jax.ShapeDtypeStruct(v.shape, v.dtype) for v in wholes],
        in_specs=[ANY] * n, out_specs=[ANY] * n,
        input_output_aliases={a: a for a in range(n)},
        scratch_shapes=_dma_sems(n),
    )(*wholes)


STRIP = 256


def _add_halves(g, sw, place, name):
    n, rows, cols = g.shape
    nb = cols // 2 // STRIP

    def kern(p_ref, g_ref, s_ref, o_ref):
        o_ref[...] = (g_ref[...] + s_ref[...]).astype(BF16)

    blk = pl.BlockSpec((1, rows, STRIP), lambda j, i, p_ref: (j, 0, i))
    return pl.pallas_call(
        kern, name=name,
        out_shape=jax.ShapeDtypeStruct((n, rows, cols // 2), BF16),
        grid_spec=pltpu.PrefetchScalarGridSpec(
            num_scalar_prefetch=1, grid=(n, nb),
            in_specs=[pl.BlockSpec((1, rows, STRIP), lambda j, i, p_ref: (j, 0, p_ref[0] * nb + i)), blk],
            out_specs=blk),
        compiler_params=_cparams("parallel", "parallel"),
    )(place, g, sw)


def _sum_chips(own, rx, place, name):
    _, rows, half = rx.shape
    nb = half // STRIP

    def kern(p_ref, own_ref, r_ref, o_ref):
        total = own_ref[0].astype(F32)
        for j in range(N_CHIPS - 1):
            total = total + r_ref[j].astype(F32)
        o_ref[...] = total

    return pl.pallas_call(
        kern, name=name,
        out_shape=jax.ShapeDtypeStruct((rows, 2 * half), F32),
        grid_spec=pltpu.PrefetchScalarGridSpec(
            num_scalar_prefetch=1, grid=(nb,),
            in_specs=[pl.BlockSpec((1, rows, STRIP), lambda i, p_ref: (p_ref[1], 0, i)),
                      pl.BlockSpec((N_CHIPS - 1, rows, STRIP), lambda i, p_ref: (0, 0, i))],
            out_specs=pl.BlockSpec((rows, STRIP), lambda i, p_ref: (0, p_ref[0] * nb + i))),
        compiler_params=_cparams("parallel"),
    )(place, own, rx)


def _gather_small(v, reduce, name):
    rows = v.shape[0]

    def body(v_ref, out_ref, buf, send_sems, recv_sems):
        x, y, c = _position()
        me = 4 * x + 2 * y + c
        buf[me] = v_ref[...]
        peers = [(x ^ (k >> 2), y ^ ((k >> 1) & 1), c ^ (k & 1)) for k in range(1, 8)]
        copies = [pltpu.make_async_remote_copy(
            src_ref=v_ref, dst_ref=buf.at[me],
            send_sem=send_sems.at[k], recv_sem=recv_sems.at[k],
            device_id=peer, device_id_type=MESH) for k, peer in enumerate(peers)]
        for cp in copies:
            cp.start()
        for k, (px, py, pc) in enumerate(peers):
            pltpu.make_async_remote_copy(
                src_ref=v_ref, dst_ref=buf.at[4 * px + 2 * py + pc],
                send_sem=send_sems.at[k], recv_sem=recv_sems.at[k],
                device_id=(px, py, pc), device_id_type=MESH).wait_recv()
        for cp in copies:
            cp.wait_send()
        if reduce:
            total = buf[0]
            for d in range(1, 8):
                total = total + buf[d]
            out_ref[...] = total
        else:
            out_ref[...] = buf[...]

    vm = pl.BlockSpec(memory_space=pltpu.VMEM)
    return pl.pallas_call(
        body, name=name,
        out_shape=jax.ShapeDtypeStruct((rows, LANES) if reduce else (8, rows, LANES), F32),
        in_specs=[vm], out_specs=vm,
        scratch_shapes=[pltpu.VMEM((8, rows, LANES), F32), pltpu.SemaphoreType.DMA((7,)), pltpu.SemaphoreType.DMA((7,))],
    )(v)


def _pad_rows(a, rows):
    return jnp.pad(a, ((0, rows - a.shape[0]), (0, 0)))


def _lane_pad(v):
    n = v.shape[1]
    return jnp.pad(v, ((0, 0), (0, -n % LANES)))


def _gather_all(w_in, w_attn_out, w_ssm_out, w_o, conv_w):
    d = D_MODEL
    w_proj_t = _gather_exchange([w_in[0].T.astype(BF16)])
    out_w = _gather_exchange([a[0].astype(BF16) for a in (w_attn_out, w_ssm_out, w_o)])
    conv_rows = conv_w[0].size // LANES
    conv_all = _gather_small(conv_w[0].reshape(conv_rows, LANES), False, "gather_conv_w")
    conv_w_all = conv_all[0::2].reshape(N_CHIPS, CONV_K, CONV_DIM // N_CHIPS).transpose(1, 0, 2).reshape(CONV_K, CONV_DIM)

    return w_proj_t, out_w, conv_w_all


def _local_step(x, loss_target, norm_w, w_proj_t, conv_w_all, conv_b, dt_bias, a_log, d_skip, ssm_norm_w,
                out_w, final_norm_w, grad_exchange=None):
    b, s, d = x.shape
    t = b * s
    g4, hg = SSD_GROUPS, HEADS_PER_GROUP
    dtb_g = _lane_pad(dt_bias.reshape(g4, hg)).reshape(g4, 1, LANES)
    alog_g = _lane_pad(a_log.reshape(g4, hg)).reshape(g4, 1, LANES)
    dskip_x = jnp.repeat(d_skip, HEAD_DIM, axis=1)
    fnw = final_norm_w.reshape(1, d)

    x2 = x.reshape(t, d)
    if isinstance(w_proj_t, _Exchange):
        h, w_in_t = _rms_fwd(x2, norm_w, exchange=w_proj_t)
        w_proj_t = _to_proj_layout(w_in_t.reshape(D_PROJ, d))
    else:
        h = _rms_fwd(x2, norm_w)
    big_tm = min(t, 2048)
    if isinstance(out_w, _Exchange):
        proj, *out_w = _matmul(h, w_proj_t, tb=True, tm=big_tm, tn=1280, tk=1024, name="proj", exchange=out_w)
    else:
        proj = _matmul(h, w_proj_t, tb=True, tm=big_tm, tn=1280, tk=1024, name="proj")
    w_ao, w_so, w_oo = (w.reshape(-1, d) for w in out_w)
    proj3 = proj.reshape(b, s, NP)
    o3, yp3 = _attn_fwd(proj3)
    xact = _conv_fwd(proj3, conv_w_all, conv_b)
    dtr = proj3[:, :, DT0:DT0 + g4 * hg].reshape(b, s, g4, hg).transpose(0, 2, 1, 3)
    dtr_g = jnp.pad(dtr, ((0, 0), (0, 0), (0, 0), (0, LANES - hg)))
    y3, yn3, hst = _ssd_fwd(xact, proj3, dtr_g, dtb_g, alog_g, dskip_x, ssm_norm_w)
    yp = yp3.reshape(t, D_MODEL)
    yn = yn3.reshape(t, SSD_WIDTH)
    ya = _matmul(yp, w_ao, tm=1024, tn=1024, tk=1024, name="attn_out")
    ys = _matmul(yn, w_so, tm=1024, tn=1024, tk=2048, name="ssm_out")
    merged = _merge_fwd(proj, ya, ys)
    mo = _matmul(merged, w_oo, tm=1024, tn=1024, tk=1024, name="out_proj")
    dout, doutb, loss_part, d_fnw = _final_fwd_bwd(x2, mo, loss_target.reshape(t, d), fnw)

    dmerged = _matmul(doutb, w_oo, tb=True, tm=1024, tn=1024, tk=1024, name="d_merged")
    g_wo = _matmul(merged, doutb, ta=True, tm=1024, tn=1024, tk=1024, name="g_w_o")
    dya, dys, dproj = _merge_bwd(dmerged, proj, ya, ys)
    dyp = _matmul(dya, w_ao, tb=True, tm=1024, tn=1024, tk=1024, name="d_attn_pre")
    g_wao = _matmul(yp, dya, ta=True, tm=1024, tn=1024, tk=1024, name="g_w_attn_out")
    dyn = _matmul(dys, w_so, tb=True, tm=1024, tn=2048, tk=1024, name="d_ssm_norm")
    g_wso = _matmul(yn, dys, ta=True, tm=1024, tn=1024, tk=1024, name="g_w_ssm_out")
    dproj3 = _attn_bwd(proj3, dyp.reshape(b, s, D_MODEL), o3, dproj.reshape(b, s, NP))
    (dxs, dbm, dcm, dproj3, ddtr_g, d_snw_g, d_alog_g, d_dtb_g, d_dsk_g) = _ssd_bwd(
        dyn.reshape(b, s, SSD_WIDTH), y3, xact, proj3, hst, dtr_g, dtb_g, alog_g, dskip_x, ssm_norm_w, dproj3)
    dproj3, g_cw_xs, g_cb_xs = _conv_bwd(dxs, proj3, conv_w_all, conv_b, 0, "conv_bwd_x", dproj3)
    dproj3, g_cw_bm, g_cb_bm = _conv_bwd(dbm, proj3, conv_w_all, conv_b, SSD_WIDTH, "conv_bwd_b", dproj3)
    dproj3, g_cw_cm, g_cb_cm = _conv_bwd(dcm, proj3, conv_w_all, conv_b, SSD_WIDTH + g4 * SSD_STATE, "conv_bwd_c", dproj3)
    ddt = ddtr_g[:, :, :, :hg].transpose(0, 2, 1, 3).reshape(b, s, g4 * hg).astype(BF16)
    ddt = jnp.pad(ddt, ((0, 0), (0, 0), (0, DT_PAD - g4 * hg)))
    dproj = lax.dynamic_update_slice(dproj3, ddt, (0, 0, DT0)).reshape(t, NP)
    exchanged = []
    if grad_exchange:
        g_wproj, *got = _matmul(dproj, h, ta=True, tm=1280, tn=1024, tk=1024, name="g_w_in",
                                exchange=grad_exchange([g_wao, g_wso, g_wo], "out"))
        exchanged += got
        dh, *got = _matmul(dproj, w_proj_t, tm=big_tm, tn=1024, tk=1280, name="d_h", exchange=grad_exchange([g_wproj], "in"))
        exchanged += got
    else:
        g_wproj = _matmul(dproj, h, ta=True, tm=1280, tn=1024, tk=1024, name="g_w_in")
        dh = _matmul(dproj, w_proj_t, tm=big_tm, tn=1024, tk=1280, name="d_h")
    grad_x, d_nw = _rms_bwd(dh, x2, norm_w, dout)
    g_cw = jnp.concatenate([g_cw_xs, g_cw_bm, g_cw_cm], axis=1)
    g_cb = jnp.concatenate([g_cb_xs, g_cb_bm, g_cb_cm], axis=1)
    return (loss_part, grad_x, d_nw, g_wproj, g_cw, g_cb, d_dtb_g, d_alog_g, d_dsk_g, d_snw_g, g_wao, g_wso, g_wo, d_fnw,
            exchanged)


def kernel(x, norm_w, w_in, conv_w, conv_b, dt_bias, a_log, d_skip, ssm_norm_w, w_attn_out, w_ssm_out, w_o, final_norm_w, loss_target, m_norm_w, m_w_in, m_conv_w, m_conv_b, m_dt_bias, m_a_log, m_d_skip, m_ssm_norm_w, m_w_attn_out, m_w_ssm_out, m_w_o, m_final_norm_w, v_norm_w, v_w_in, v_conv_w, v_conv_b, v_dt_bias, v_a_log, v_d_skip, v_ssm_norm_w, v_w_attn_out, v_w_ssm_out, v_w_o, v_final_norm_w):
    b, s, d = x.shape
    core = lax.axis_index("c")
    g4, hg = SSD_GROUPS, HEADS_PER_GROUP
    shard_cols = w_in.shape[2]
    w_proj_t, out_w, conv_w_all = _gather_all(w_in, w_attn_out, w_ssm_out, w_o, conv_w)
    chip = 2 * lax.axis_index("x") + lax.axis_index("y")
    place = jnp.stack([core, chip]).astype(jnp.int32)
    chip_sums = []

    def grad_exchange(grads, which):
        if which == "in":
            slabs = _from_proj_layout(grads[0]).reshape(N_CHIPS, shard_cols, d)
        else:
            slabs = jnp.concatenate([g.reshape(N_CHIPS, -1, d) for g in grads], axis=1)
        from_sibling, = _swap_halves([slabs], "grad_swap_halves_" + which)
        chip_sums.append(_add_halves(slabs, from_sibling, place, "grad_add_halves_" + which))
        return _all_to_all_exchange(chip_sums[-1:])

    (loss_part, grad_x, d_nw, _, g_cw, g_cb, d_dtb_g, d_alog_g, d_dsk_g, d_snw_g, _, _, _, d_fnw, from_chips) = _local_step(
        x, loss_target, norm_w, w_proj_t, conv_w_all, conv_b, dt_bias, a_log, d_skip, ssm_norm_w, out_w, final_norm_w,
        grad_exchange)
    wholes = [_sum_chips(o, r, place, "grad_sum_chips_%d" % i) for i, (o, r) in enumerate(zip(chip_sums, from_chips))]
    g_out, g_w_in = _join_halves(wholes)

    small = jnp.concatenate([
        loss_part, d_nw, g_cb, _lane_pad(d_dtb_g[:, 0, :hg].reshape(1, -1)), _lane_pad(d_alog_g[:, 0, :hg].reshape(1, -1)),
        _lane_pad(d_dsk_g[:, 0, :hg].reshape(1, -1)),
        d_snw_g.reshape(1, -1), d_fnw, g_cw.reshape(1, -1)], axis=1)
    small_rows = small.shape[1] // LANES
    reduced = _gather_small(_pad_rows(small.reshape(small_rows, LANES), -(-small_rows // 8) * 8), True, "reduce_small")
    flat = reduced.reshape(-1)

    def take(start, n):
        return flat[start:start + n].reshape(1, n)

    loss = flat[0]
    pos = LANES
    g_norm_w = take(pos, d); pos += d
    g_conv_b = take(pos, CONV_DIM); pos += CONV_DIM
    g_dt_bias = take(pos, g4 * hg); pos += LANES
    g_a_log = take(pos, g4 * hg); pos += LANES
    g_d_skip = take(pos, g4 * hg); pos += LANES
    g_ssm_norm_w = take(pos, SSD_WIDTH); pos += SSD_WIDTH
    g_final_norm_w = take(pos, d); pos += d
    conv_cols = CONV_DIM // N_CHIPS
    g_conv_w = lax.dynamic_slice_in_dim(flat[pos:pos + CONV_K * CONV_DIM].reshape(CONV_K, CONV_DIM), chip * conv_cols, conv_cols, axis=1)

    rows_ao, rows_so = D_MODEL // N_CHIPS, SSD_WIDTH // N_CHIPS
    g_w_attn_out = g_out[:rows_ao]
    g_w_ssm_out = g_out[rows_ao:rows_ao + rows_so]
    g_w_o = g_out[rows_ao + rows_so:]

    names = ["norm_w", "w_in", "conv_w", "conv_b", "dt_bias", "a_log", "d_skip", "ssm_norm_w",
             "w_attn_out", "w_ssm_out", "w_o", "final_norm_w"]
    weights = [norm_w, w_in, conv_w, conv_b, dt_bias, a_log, d_skip, ssm_norm_w, w_attn_out, w_ssm_out, w_o, final_norm_w]
    grads = [g_norm_w, g_w_in, g_conv_w, g_conv_b, g_dt_bias, g_a_log, g_d_skip, g_ssm_norm_w,
             g_w_attn_out, g_w_ssm_out, g_w_o, g_final_norm_w]
    ms = [m_norm_w, m_w_in, m_conv_w, m_conv_b, m_dt_bias, m_a_log, m_d_skip, m_ssm_norm_w,
          m_w_attn_out, m_w_ssm_out, m_w_o, m_final_norm_w]
    vs = [v_norm_w, v_w_in, v_conv_w, v_conv_b, v_dt_bias, v_a_log, v_d_skip, v_ssm_norm_w,
          v_w_attn_out, v_w_ssm_out, v_w_o, v_final_norm_w]
    out_g, out_d, out_m, out_v = [], [], [], []
    for name, w, g, m, v in zip(names, weights, grads, ms, vs):
        if name == "w_in":
            to2, back = (lambda a: a[0].T), (lambda a: a.T.reshape(w.shape))
        else:
            to2, back = (lambda a: a.reshape(g.shape)), (lambda a: a.reshape(w.shape))
        dlt, nm, nv = _adamw(to2(w), g, to2(m), to2(v), "adamw_" + name)
        out_g.append(back(g))
        out_d.append(back(dlt))
        out_m.append(back(nm))
        out_v.append(back(nv))

    return (loss, grad_x.reshape(b, s, d), *out_g, *out_d, *out_m, *out_v)
```

```python
import jax
import jax.numpy as jnp
from jax import lax
from jax.experimental import pallas as pl
from jax.experimental.pallas import tpu as pltpu

F32 = jnp.float32
BF16 = jnp.bfloat16
MESH = pl.DeviceIdType.MESH

D_MODEL = 1024
SB_HEADS = 16
HEAD_DIM = 64
SSD_WIDTH = 2048
SSD_GROUPS = 4
GROUP_WIDTH = SSD_WIDTH // SSD_GROUPS
HEADS_PER_GROUP = 8
SSD_STATE = 128
CHUNK = 128
CONV_K = 4
CONV_DIM = 3072
D_PROJ = 11296
EPS = 1e-6
ADAM_LR, ADAM_B1, ADAM_B2, ADAM_EPS, ADAM_WD, ADAM_STEP = 0.001, 0.9, 0.999, 1e-08, 0.01, 10

LANES = 128
HP_WIDTH = 4 * LANES
ZS0, GATE0, XBC0, DT0 = 4096, 6144, 8192, 11264
DT_PAD = 256
NP = DT0 + DT_PAD
N_CHIPS = 4
VMEM_LIMIT = 56 * 1024 * 1024


N_HP = SB_HEADS // 2
W_ZS0, W_XBC0, W_DT0, W_GATE0 = 4096, 6144, 9216, 9248


def _to_proj_layout(wt):
    d = wt.shape[1]
    pairs = wt[:W_ZS0].reshape(4, N_HP, LANES, d).transpose(1, 0, 2, 3).reshape(W_ZS0, d)
    return jnp.concatenate([pairs, wt[W_ZS0:W_XBC0], wt[W_GATE0:], wt[W_XBC0:W_DT0], wt[W_DT0:W_GATE0],
                            jnp.zeros((NP - D_PROJ, d), wt.dtype)], axis=0)


def _from_proj_layout(gt):
    d = gt.shape[1]
    qkvz = gt[:ZS0].reshape(N_HP, 4, LANES, d).transpose(1, 0, 2, 3).reshape(ZS0, d)
    return jnp.concatenate([qkvz, gt[ZS0:GATE0], gt[XBC0:DT0], gt[DT0:DT0 + W_GATE0 - W_DT0], gt[GATE0:XBC0]], axis=0)


def _cparams(*sem):
    return pltpu.CompilerParams(dimension_semantics=sem or None, vmem_limit_bytes=VMEM_LIMIT)


def _sigmoid(z):
    return 1.0 / (1.0 + jnp.exp(-z))


def _dot(a, b, dims, precision=None):
    return lax.dot_general(a, b, (dims, ((), ())), preferred_element_type=F32, precision=precision)


NN = ((1,), (0,))
NT = ((1,), (1,))
TN = ((0,), (0,))


def _matmul(a, b, *, ta=False, tb=False, out_dtype=F32, tm, tn, tk, name, exchange=None):
    m, k = (a.shape[1], a.shape[0]) if ta else a.shape
    n = b.shape[0] if tb else b.shape[1]
    assert m % tm == 0 and n % tn == 0 and k % tk == 0, (name, m, n, k)
    grid = (m // tm, n // tn, k // tk)
    nk = grid[2]
    use_scratch = out_dtype != F32
    dims = ((0,) if ta else (1,), (1,) if tb else (0,))
    n_in = len(exchange.inputs) if exchange else 0
    n_out = len(exchange.out_shapes) if exchange else 0

    def kern(a_ref, b_ref, *rest):
        x_in, o_ref, x_out, scratch = rest[:n_in], rest[n_in], rest[n_in + 1:n_in + 1 + n_out], rest[n_in + 1 + n_out:]
        acc = scratch[0] if use_scratch else o_ref
        step = [pl.program_id(ax) for ax in range(3)]
        if exchange:
            sems = scratch[1:] if use_scratch else scratch

            @pl.when(jnp.logical_and(jnp.logical_and(step[0] == 0, step[1] == 0), step[2] == 0))
            def _():
                exchange.start(x_in, x_out, sems)

        @pl.when(step[2] == 0)
        def _():
            acc[...] = jnp.zeros_like(acc)

        acc[...] += _dot(a_ref[...], b_ref[...], dims)
        if use_scratch:
            @pl.when(step[2] == nk - 1)
            def _():
                o_ref[...] = acc[...].astype(out_dtype)
        if exchange:
            @pl.when(jnp.logical_and(jnp.logical_and(step[0] == grid[0] - 1, step[1] == grid[1] - 1), step[2] == nk - 1))
            def _():
                exchange.finish(x_in, x_out, sems)

    a_spec = pl.BlockSpec((tk, tm), lambda i, j, q: (q, i)) if ta else pl.BlockSpec((tm, tk), lambda i, j, q: (i, q))
    b_spec = pl.BlockSpec((tn, tk), lambda i, j, q: (j, q)) if tb else pl.BlockSpec((tk, tn), lambda i, j, q: (q, j))
    out = pl.pallas_call(
        kern, name=name,
        out_shape=[jax.ShapeDtypeStruct((m, n), out_dtype)] + (list(exchange.out_shapes) if exchange else []),
        grid=grid,
        in_specs=[a_spec, b_spec] + [ANY] * n_in,
        out_specs=[pl.BlockSpec((tm, tn), lambda i, j, q: (i, j))] + [ANY] * n_out,
        scratch_shapes=([pltpu.VMEM((tm, tn), F32)] if use_scratch else []) + (list(exchange.sems) if exchange else []),
        compiler_params=_cparams("arbitrary", "arbitrary", "arbitrary") if exchange else _cparams("parallel", "parallel", "arbitrary"),
    )(a, b, *(exchange.inputs if exchange else []))
    return out if exchange else out[0]


ROWS = 512


def _rms_fwd(x2, w, exchange=None):
    t, d = x2.shape
    steps = t // ROWS
    n_in = len(exchange.inputs) if exchange else 0
    n_out = len(exchange.out_shapes) if exchange else 0

    def kern(x_ref, w_ref, *rest):
        x_in, h_ref, x_out, sems = rest[:n_in], rest[n_in], rest[n_in + 1:n_in + 1 + n_out], rest[n_in + 1 + n_out:]
        if exchange:
            @pl.when(pl.program_id(0) == 0)
            def _():
                exchange.start(x_in, x_out, sems)

        x = x_ref[...]
        r = lax.rsqrt(jnp.mean(x * x, axis=-1, keepdims=True) + EPS)
        h_ref[...] = (x * r * w_ref[...]).astype(BF16)
        if exchange:
            @pl.when(pl.program_id(0) == steps - 1)
            def _():
                exchange.finish(x_in, x_out, sems)

    out = pl.pallas_call(
        kern, name="rms_fwd",
        out_shape=[jax.ShapeDtypeStruct((t, d), BF16)] + (list(exchange.out_shapes) if exchange else []),
        grid=(steps,),
        in_specs=[pl.BlockSpec((ROWS, d), lambda i: (i, 0)), pl.BlockSpec((1, d), lambda i: (0, 0))] + [ANY] * n_in,
        out_specs=[pl.BlockSpec((ROWS, d), lambda i: (i, 0))] + [ANY] * n_out,
        scratch_shapes=list(exchange.sems) if exchange else [],
        compiler_params=_cparams("arbitrary" if exchange else "parallel"),
    )(x2, w, *(exchange.inputs if exchange else []))
    return out if exchange else out[0]


def _rms_bwd(dh, x2, w, dout):
    t, d = x2.shape

    def kern(dh_ref, x_ref, w_ref, dout_ref, gx_ref, dw_ref):
        @pl.when(pl.program_id(0) == 0)
        def _():
            dw_ref[...] = jnp.zeros_like(dw_ref)

        x = x_ref[...]
        r = lax.rsqrt(jnp.mean(x * x, axis=-1, keepdims=True) + EPS)
        xh = x * r
        g = dh_ref[...]
        dw_ref[...] += jnp.sum(g * xh, axis=0, keepdims=True)
        gw = g * w_ref[...]
        gx_ref[...] = dout_ref[...] + r * (gw - xh * jnp.mean(gw * xh, axis=-1, keepdims=True))

    row = pl.BlockSpec((ROWS, d), lambda i: (i, 0))
    vec = pl.BlockSpec((1, d), lambda i: (0, 0))
    return pl.pallas_call(
        kern, name="rms_bwd",
        out_shape=(jax.ShapeDtypeStruct((t, d), F32), jax.ShapeDtypeStruct((1, d), F32)),
        grid=(t // ROWS,),
        in_specs=[row, row, vec, row],
        out_specs=(row, vec),
        compiler_params=_cparams("arbitrary"),
    )(dh, x2, w, dout)


def _final_fwd_bwd(x2, mo, target, w):
    t, d = x2.shape

    def kern(x_ref, mo_ref, t_ref, w_ref, dout_ref, doutb_ref, loss_ref, dw_ref):
        @pl.when(pl.program_id(0) == 0)
        def _():
            loss_ref[...] = jnp.zeros_like(loss_ref)
            dw_ref[...] = jnp.zeros_like(dw_ref)

        u = x_ref[...] + mo_ref[...]
        r = lax.rsqrt(jnp.mean(u * u, axis=-1, keepdims=True) + EPS)
        uh = u * r
        wv = w_ref[...]
        err = uh * wv - t_ref[...]
        loss_ref[...] += (0.5 / d) * jnp.sum(err * err)
        dy = err * (1.0 / d)
        dw_ref[...] += jnp.sum(dy * uh, axis=0, keepdims=True)
        gw = dy * wv
        du = r * (gw - uh * jnp.mean(gw * uh, axis=-1, keepdims=True))
        dout_ref[...] = du
        doutb_ref[...] = du.astype(BF16)

    row = pl.BlockSpec((ROWS, d), lambda i: (i, 0))
    vec = pl.BlockSpec((1, d), lambda i: (0, 0))
    return pl.pallas_call(
        kern, name="final_fwd_bwd",
        out_shape=(jax.ShapeDtypeStruct((t, d), F32), jax.ShapeDtypeStruct((t, d), BF16),
                   jax.ShapeDtypeStruct((1, LANES), F32), jax.ShapeDtypeStruct((1, d), F32)),
        grid=(t // ROWS,),
        in_specs=[row, row, row, vec],
        out_specs=(row, row, pl.BlockSpec((1, LANES), lambda i: (0, 0)), vec),
        compiler_params=_cparams("arbitrary"),
    )(x2, mo, target, w)


def _merge_fwd(proj2, ya, ys):
    t = ya.shape[0]
    gblk = GATE0 // D_MODEL

    def kern(ga_ref, gs_ref, ya_ref, ys_ref, o_ref):
        o_ref[...] = (_sigmoid(ga_ref[...]) * ya_ref[...] + _sigmoid(gs_ref[...]) * ys_ref[...]).astype(BF16)

    row = pl.BlockSpec((ROWS, D_MODEL), lambda i: (i, 0))
    return pl.pallas_call(
        kern, name="merge_fwd",
        out_shape=jax.ShapeDtypeStruct((t, D_MODEL), BF16),
        grid=(t // ROWS,),
        in_specs=[pl.BlockSpec((ROWS, D_MODEL), lambda i: (i, gblk)),
                  pl.BlockSpec((ROWS, D_MODEL), lambda i: (i, gblk + 1)), row, row],
        out_specs=row,
        compiler_params=_cparams("parallel"),
    )(proj2, proj2, ya, ys)


def _merge_bwd(dm, proj2, ya, ys):
    t = ya.shape[0]
    gblk = GATE0 // D_MODEL

    def kern(dm_ref, ga_ref, gs_ref, ya_ref, ys_ref, dya_ref, dys_ref, dg_ref):
        g = dm_ref[...]
        sa = _sigmoid(ga_ref[...])
        ss = _sigmoid(gs_ref[...])
        dya_ref[...] = (g * sa).astype(BF16)
        dys_ref[...] = (g * ss).astype(BF16)
        dg_ref[:, :D_MODEL] = (g * ya_ref[...] * sa * (1.0 - sa)).astype(BF16)
        dg_ref[:, D_MODEL:] = (g * ys_ref[...] * ss * (1.0 - ss)).astype(BF16)

    row = pl.BlockSpec((ROWS, D_MODEL), lambda i: (i, 0))
    return pl.pallas_call(
        kern, name="merge_bwd",
        out_shape=(jax.ShapeDtypeStruct((t, D_MODEL), BF16), jax.ShapeDtypeStruct((t, D_MODEL), BF16),
                   jax.ShapeDtypeStruct((t, NP), BF16)),
        grid=(t // ROWS,),
        in_specs=[row, pl.BlockSpec((ROWS, D_MODEL), lambda i: (i, gblk)),
                  pl.BlockSpec((ROWS, D_MODEL), lambda i: (i, gblk + 1)), row, row],
        out_specs=(row, row, pl.BlockSpec((ROWS, 2 * D_MODEL), lambda i: (i, GATE0 // (2 * D_MODEL)))),
        compiler_params=_cparams("parallel"),
    )(dm, proj2, proj2, ya, ys)


TQ = 256
TK = 256
assert TQ == TK
HEAD_LANES = (slice(0, HEAD_DIM), slice(HEAD_DIM, 2 * HEAD_DIM))


def _tri(pred):
    r = lax.broadcasted_iota(jnp.int32, (TK, TK), 0)
    c = lax.broadcasted_iota(jnp.int32, (TK, TK), 1)
    return pred(r, c).astype(BF16)


def _split_bf16(v):
    hi = v.astype(BF16)
    lo = (v - hi.astype(F32)).astype(BF16)
    return hi, lo


def _tri_dot(v, tri):
    hi, lo = _split_bf16(v)
    return _dot(hi, tri, NN) + _dot(lo, tri, NN)


def _sb_logs(z, mask):
    l1p = jnp.log(1.0 + jnp.exp(-jnp.abs(z)))
    lb = jnp.minimum(z, 0.0) - l1p
    lom = -jnp.maximum(z, 0.0) - l1p
    if mask is not None:
        lom = jnp.where(mask, lom, 0.0)
    return lb, lom


def _sb_weights(lb, later, carry_r, mask):
    a = jnp.exp(lb + (later + carry_r))
    if mask is not None:
        a = jnp.where(mask, a, 0.0)
    return a


DEAD = -104.0


def _while_alive(n, carry, step):
    def alive(cr):
        return jnp.max(jnp.maximum(cr[0][0], cr[1][0])) > DEAD

    def cond(state):
        jj, go, _ = state
        return jnp.logical_and(jj < n, go)

    def body(state):
        jj, _, cr = state
        cr = step(jj, cr)
        return jj + 1, alive(cr), cr

    return lax.while_loop(cond, body, (jnp.int32(0), alive(carry), carry))[2]


Q_LANES, K_LANES, V_LANES, ZA_LANES = (slice(i * LANES, (i + 1) * LANES) for i in range(4))


def _split_heads(dst, src, scale=None):
    for h, lanes in enumerate(HEAD_LANES):
        v = src[:, lanes]
        dst[h] = (v if scale is None else v * scale).astype(BF16)


def _attn_fwd(proj3):
    b, s, _ = proj3.shape
    nq = s // TQ
    scale = HEAD_DIM ** -0.5

    def kern(x_ref, o_ref, yp_ref, qs, ks, vs):
        _split_heads(qs, x_ref[0, :, Q_LANES], scale)
        _split_heads(ks, x_ref[0, :, K_LANES])
        _split_heads(vs, x_ref[0, :, V_LANES])
        za_ref = x_ref.at[:, :, ZA_LANES]
        row = lax.broadcasted_iota(jnp.int32, (TQ, TK), 0)
        col = lax.broadcasted_iota(jnp.int32, (TQ, TK), 1)
        tri_gt = _tri(lambda j, sk: j > sk)

        def q_block(i, _):
            top = isinstance(i, int)
            r0 = i * TQ if top else pl.multiple_of(i * TQ, TQ)
            qh = [qs[h, pl.ds(r0, TQ), :] for h in range(2)]

            def k_blocks(blocks, carry):
                nb = range(len(blocks))
                kh = [[ks[h, pl.ds(c0, TK), :] for h in range(2)] for c0, _ in blocks]
                vh = [[vs[h, pl.ds(c0, TK), :] for h in range(2)] for c0, _ in blocks]
                z = [[_dot(qh[h], kh[bl][h], NT) for h in range(2)] for bl in nb]
                logs = [[_sb_logs(z[bl][h], blocks[bl][1]) for h in range(2)] for bl in nb]
                later = [[_tri_dot(logs[bl][h][1], tri_gt) for h in range(2)] for bl in nb]
                out = []
                for h in range(2):
                    carry_r, acc = carry[h]
                    for bl in nb:
                        lb, lom = logs[bl][h]
                        a = _sb_weights(lb, later[bl][h], carry_r, blocks[bl][1])
                        acc = acc + _dot(a.astype(BF16), vh[bl][h], NN)
                        carry_r = carry_r + (later[bl][h][:, 0:1] + lom[:, 0:1])
                    out.append((carry_r, acc))
                return tuple(out)

            start = (jnp.zeros((TQ, 1), F32), jnp.zeros((TQ, HEAD_DIM), F32))
            diag = (r0, col < row)
            if top:
                carry = k_blocks([diag], (start, start))
            else:
                carry = k_blocks([diag, (pl.multiple_of(r0 - TK, TK), None)], (start, start))
                carry = _while_alive(i - 1, carry, lambda jj, cr: k_blocks([(pl.multiple_of((i - 2 - jj) * TK, TK), None)], cr))
            for (_, acc), lanes in zip(carry, HEAD_LANES):
                o_ref[0, pl.ds(r0, TQ), lanes] = acc
                za = za_ref[0, pl.ds(r0, TQ), lanes]
                yp_ref[0, pl.ds(r0, TQ), lanes] = (acc * (za * _sigmoid(za))).astype(BF16)
            return 0

        q_block(0, 0)
        lax.fori_loop(1, nq, q_block, 0)

    out_spec = pl.BlockSpec((1, s, LANES), lambda bi, hp: (bi, 0, hp))
    return pl.pallas_call(
        kern, name="attn_fwd",
        out_shape=(jax.ShapeDtypeStruct((b, s, D_MODEL), F32), jax.ShapeDtypeStruct((b, s, D_MODEL), BF16)),
        grid=(b, SB_HEADS // 2),
        in_specs=[pl.BlockSpec((1, s, HP_WIDTH), lambda bi, hp: (bi, 0, hp))],
        out_specs=(out_spec, out_spec),
        scratch_shapes=[pltpu.VMEM((2, s, HEAD_DIM), BF16)] * 3,
        compiler_params=_cparams("parallel", "parallel"),
    )(proj3)


def _attn_bwd(proj3, dyp3, o3, dproj3):
    b, s, _ = proj3.shape
    nq = s // TQ
    scale = HEAD_DIM ** -0.5

    def kern(x_ref, dyp_ref, o_ref, _, d_ref, qs, ks, vs, dos, dk_acc, dv_acc):
        _split_heads(qs, x_ref[0, :, Q_LANES], scale)
        _split_heads(ks, x_ref[0, :, K_LANES])
        _split_heads(vs, x_ref[0, :, V_LANES])
        dq_ref, dk_ref, dv_ref = (d_ref.at[:, :, lanes] for lanes in (Q_LANES, K_LANES, V_LANES))
        za = x_ref[0, :, ZA_LANES]
        sg = _sigmoid(za)
        dyp = dyp_ref[0]
        _split_heads(dos, dyp * (za * sg))
        d_ref[0, :, ZA_LANES] = (dyp * o_ref[0] * (sg * (1.0 + za * (1.0 - sg)))).astype(BF16)
        dk_acc[...] = jnp.zeros_like(dk_acc)
        dv_acc[...] = jnp.zeros_like(dv_acc)
        row = lax.broadcasted_iota(jnp.int32, (TQ, TK), 0)
        col = lax.broadcasted_iota(jnp.int32, (TQ, TK), 1)
        tri_gt = _tri(lambda j, sk: j > sk)
        tri_ge = _tri(lambda j, sk: j >= sk)

        def q_block(i, _):
            top = isinstance(i, int)
            r0 = i * TQ if top else pl.multiple_of(i * TQ, TQ)
            qh = [qs[h, pl.ds(r0, TQ), :] for h in range(2)]
            doh = [dos[h, pl.ds(r0, TQ), :] for h in range(2)]
            totals = [jnp.sum(doh[h].astype(F32) * o_ref[0, pl.ds(r0, TQ), lanes], axis=1, keepdims=True)
                      for h, lanes in enumerate(HEAD_LANES)]

            def k_blocks(blocks, carry):
                nb = range(len(blocks))
                kh = [[ks[h, pl.ds(c0, TK), :] for h in range(2)] for c0, _ in blocks]
                vh = [[vs[h, pl.ds(c0, TK), :] for h in range(2)] for c0, _ in blocks]
                z = [[_dot(qh[h], kh[bl][h], NT) for h in range(2)] for bl in nb]
                da = [[_dot(doh[h], vh[bl][h], NT) for h in range(2)] for bl in nb]
                logs = [[_sb_logs(z[bl][h], blocks[bl][1]) for h in range(2)] for bl in nb]
                later = [[_tri_dot(logs[bl][h][1], tri_gt) for h in range(2)] for bl in nb]
                ab, g, suffix = ([[None, None] for _ in nb] for _ in range(3))
                for h in range(2):
                    cr = carry[h][0]
                    for bl in nb:
                        a = _sb_weights(logs[bl][h][0], later[bl][h], cr, blocks[bl][1])
                        ab[bl][h] = a.astype(BF16)
                        g[bl][h] = da[bl][h] * ab[bl][h].astype(F32)
                        suffix[bl][h] = _tri_dot(g[bl][h], tri_ge)
                        cr = cr + (later[bl][h][:, 0:1] + logs[bl][h][1][:, 0:1])
                out = []
                for h in range(2):
                    _, carry_g, dq = carry[h]
                    cr = carry[h][0]
                    for bl in nb:
                        c0, mask = blocks[bl]
                        lb, lom = logs[bl][h]
                        dz = g[bl][h] - (g[bl][h] + (totals[h] - carry_g) - suffix[bl][h]) * jnp.exp(lb)
                        if mask is not None:
                            dz = jnp.where(mask, dz, 0.0)
                        dzb = dz.astype(BF16)
                        dk_acc[h, pl.ds(c0, TK), :] += _dot(dzb, qh[h], TN)
                        dv_acc[h, pl.ds(c0, TK), :] += _dot(ab[bl][h], doh[h], TN)
                        dq = dq + _dot(dzb, kh[bl][h], NN)
                        carry_g = carry_g + suffix[bl][h][:, 0:1]
                        cr = cr + (later[bl][h][:, 0:1] + lom[:, 0:1])
                    out.append((cr, carry_g, dq))
                return tuple(out)

            def k_block(c0, carry, mask):
                kh = [ks[h, pl.ds(c0, TK), :] for h in range(2)]
                vh = [vs[h, pl.ds(c0, TK), :] for h in range(2)]
                z = [_dot(qh[h], kh[h], NT) for h in range(2)]
                da = [_dot(doh[h], vh[h], NT) for h in range(2)]
                logs, later = [], []
                for h in range(2):
                    logs.append(_sb_logs(z[h], mask))
                    later.append(_tri_dot(logs[h][1], tri_gt))
                ab, g, suffix = [], [], []
                for h in range(2):
                    a = _sb_weights(logs[h][0], later[h], carry[h][0], mask)
                    ab.append(a.astype(BF16))
                    g.append(da[h] * ab[h].astype(F32))
                    suffix.append(_tri_dot(g[h], tri_ge))
                out = []
                for h in range(2):
                    carry_r, carry_g, dq = carry[h]
                    lb, lom = logs[h]
                    dz = g[h] - (g[h] + (totals[h] - carry_g) - suffix[h]) * jnp.exp(lb)
                    if mask is not None:
                        dz = jnp.where(mask, dz, 0.0)
                    dzb = dz.astype(BF16)
                    dk_acc[h, pl.ds(c0, TK), :] += _dot(dzb, qh[h], TN)
                    dv_acc[h, pl.ds(c0, TK), :] += _dot(ab[h], doh[h], TN)
                    out.append((carry_r + (later[h][:, 0:1] + lom[:, 0:1]), carry_g + suffix[h][:, 0:1],
                                dq + _dot(dzb, kh[h], NN)))
                return tuple(out)

            zero = jnp.zeros((TQ, 1), F32)
            start = (zero, zero, jnp.zeros((TQ, HEAD_DIM), F32))
            diag = (r0, col < row)
            if top:
                carry = k_block(r0, (start, start), col < row)
            else:
                carry = k_blocks([diag, (pl.multiple_of(r0 - TK, TK), None)], (start, start))
                carry = _while_alive(i - 1, carry, lambda jj, cr: k_block(pl.multiple_of((i - 2 - jj) * TK, TK), cr, None))
            for (_, _, dq), lanes in zip(carry, HEAD_LANES):
                dq_ref[0, pl.ds(r0, TQ), lanes] = (dq * scale).astype(BF16)
            return 0

        q_block(0, 0)
        lax.fori_loop(1, nq, q_block, 0)

        for h, lanes in enumerate(HEAD_LANES):
            dk_ref[0, :, lanes] = dk_acc[h].astype(BF16)
            dv_ref[0, :, lanes] = dv_acc[h].astype(BF16)

    plain = pl.BlockSpec((1, s, LANES), lambda bi, hp: (bi, 0, hp))
    pair = pl.BlockSpec((1, s, HP_WIDTH), lambda bi, hp: (bi, 0, hp))
    return pl.pallas_call(
        kern, name="attn_bwd",
        out_shape=jax.ShapeDtypeStruct(dproj3.shape, dproj3.dtype),
        grid=(b, SB_HEADS // 2),
        in_specs=[pair, plain, plain, ANY],
        out_specs=pair,
        input_output_aliases={3: 0},
        scratch_shapes=[pltpu.VMEM((2, s, HEAD_DIM), BF16)] * 4 + [pltpu.VMEM((2, s, HEAD_DIM), F32)] * 2,
        compiler_params=_cparams("parallel", "parallel"),
    )(proj3, dyp3, o3, dproj3)


CONV_COLS = 256
HALO = 8


def _conv_pre(xp, w_ref, b_ref, r0):
    pre = b_ref[...] + w_ref[CONV_K - 1:CONV_K, :] * xp[pl.ds(HALO + r0, CHUNK), :]
    for kk in range(1, CONV_K):
        pre = pre + w_ref[CONV_K - 1 - kk:CONV_K - kk, :] * xp[pl.ds(HALO + r0 - kk, CHUNK), :]
    return pre


def _conv_fwd(proj3, conv_w, conv_b):
    b, s, _ = proj3.shape
    nc = s // CHUNK

    def kern(x_ref, w_ref, b_ref, o_ref, xp):
        xp[0:HALO, :] = jnp.zeros((HALO, CONV_COLS), F32)
        xp[HALO:, :] = x_ref[0]
        for ci in range(nc):
            pre = _conv_pre(xp, w_ref, b_ref, ci * CHUNK)
            o_ref[0, ci * CHUNK:(ci + 1) * CHUNK, :] = pre * _sigmoid(pre)

    return pl.pallas_call(
        kern, name="conv_fwd",
        out_shape=jax.ShapeDtypeStruct((b, s, CONV_DIM), F32),
        grid=(CONV_DIM // CONV_COLS, b),
        in_specs=[pl.BlockSpec((1, s, CONV_COLS), lambda j, bi: (bi, 0, XBC0 // CONV_COLS + j)),
                  pl.BlockSpec((CONV_K, CONV_COLS), lambda j, bi: (0, j)),
                  pl.BlockSpec((1, CONV_COLS), lambda j, bi: (0, j))],
        out_specs=pl.BlockSpec((1, s, CONV_COLS), lambda j, bi: (bi, 0, j)),
        scratch_shapes=[pltpu.VMEM((s + HALO, CONV_COLS), F32)],
        compiler_params=_cparams("parallel", "parallel"),
    )(proj3, conv_w, conv_b)


def _conv_bwd(dact, proj3, conv_w, conv_b, col0, name, dproj3):
    b, s, width = dact.shape
    nc = s // CHUNK
    j0 = col0 // CONV_COLS

    def kern(da_ref, x_ref, w_ref, b_ref, _, dx_ref, dw_ref, db_ref, xp, dp):
        @pl.when(pl.program_id(1) == 0)
        def _():
            dw_ref[...] = jnp.zeros_like(dw_ref)
            db_ref[...] = jnp.zeros_like(db_ref)

        xp[0:HALO, :] = jnp.zeros((HALO, CONV_COLS), F32)
        xp[HALO:, :] = x_ref[0]
        dp[s:, :] = jnp.zeros((HALO, CONV_COLS), F32)
        for ci in range(nc):
            r0 = ci * CHUNK
            pre = _conv_pre(xp, w_ref, b_ref, r0)
            sg = _sigmoid(pre)
            dpre = da_ref[0, r0:r0 + CHUNK, :] * (sg * (1.0 + pre * (1.0 - sg)))
            dp[r0:r0 + CHUNK, :] = dpre
            db_ref[...] += jnp.sum(dpre, axis=0, keepdims=True)
            for kk in range(CONV_K):
                tap = CONV_K - 1 - kk
                dw_ref[tap:tap + 1, :] += jnp.sum(dpre * xp[pl.ds(HALO + r0 - kk, CHUNK), :], axis=0, keepdims=True)
        for ci in range(nc):
            r0 = ci * CHUNK
            dx = w_ref[CONV_K - 1:CONV_K, :] * dp[pl.ds(r0, CHUNK), :]
            for kk in range(1, CONV_K):
                dx = dx + w_ref[CONV_K - 1 - kk:CONV_K - kk, :] * dp[pl.ds(r0 + kk, CHUNK), :]
            dx_ref[0, r0:r0 + CHUNK, :] = dx.astype(BF16)

    return pl.pallas_call(
        kern, name=name,
        out_shape=(jax.ShapeDtypeStruct(dproj3.shape, dproj3.dtype), jax.ShapeDtypeStruct((CONV_K, width), F32),
                   jax.ShapeDtypeStruct((1, width), F32)),
        grid=(width // CONV_COLS, b),
        in_specs=[pl.BlockSpec((1, s, CONV_COLS), lambda j, bi: (bi, 0, j)),
                  pl.BlockSpec((1, s, CONV_COLS), lambda j, bi: (bi, 0, XBC0 // CONV_COLS + j0 + j)),
                  pl.BlockSpec((CONV_K, CONV_COLS), lambda j, bi: (0, j0 + j)),
                  pl.BlockSpec((1, CONV_COLS), lambda j, bi: (0, j0 + j)), ANY],
        out_specs=(pl.BlockSpec((1, s, CONV_COLS), lambda j, bi: (bi, 0, XBC0 // CONV_COLS + j0 + j)),
                   pl.BlockSpec((CONV_K, CONV_COLS), lambda j, bi: (0, j)),
                   pl.BlockSpec((1, CONV_COLS), lambda j, bi: (0, j))),
        input_output_aliases={4: 0},
        scratch_shapes=[pltpu.VMEM((s + HALO, CONV_COLS), F32)] * 2,
        compiler_params=_cparams("parallel", "arbitrary"),
    )(dact, proj3, conv_w, conv_b, dproj3)


SSD_CHUNKS_PER_STEP = 8


def _sel_dot(v, sel, left=False):
    hi = v.astype(BF16)
    rest = v - hi.astype(F32)
    mid = rest.astype(BF16)
    lo = (rest - mid.astype(F32)).astype(BF16)
    if left:
        return _dot(sel, hi, NN) + _dot(sel, mid, NN) + _dot(sel, lo, NN)
    return _dot(hi, sel, NN) + _dot(mid, sel, NN) + _dot(lo, sel, NN)


def _ssd_common(dtr, dtb, alog):
    lane = lax.broadcasted_iota(jnp.int32, (CHUNK, LANES), 1)
    row = lax.broadcasted_iota(jnp.int32, (CHUNK, LANES), 0)
    head_lane = lane < HEADS_PER_GROUP
    pre = dtr + dtb
    dt = jnp.where(head_lane, jnp.maximum(pre, 0.0) + jnp.log(1.0 + jnp.exp(-jnp.abs(pre))), 0.0)
    a = jnp.where(head_lane[0:1], -jnp.exp(alog), 0.0)
    tril = (row >= lane).astype(BF16)
    acs = _sel_dot(dt * a, tril, left=True)
    acs_t = acs.T
    er = lax.broadcasted_iota(jnp.int32, (LANES, GROUP_WIDTH), 0)
    ec = lax.broadcasted_iota(jnp.int32, (LANES, GROUP_WIDTH), 1)
    expand = ((ec // HEAD_DIM) == er).astype(BF16)
    tr = lax.broadcasted_iota(jnp.int32, (GROUP_WIDTH, LANES), 0)
    tc = lax.broadcasted_iota(jnp.int32, (GROUP_WIDTH, LANES), 1)
    reduce = ((tr // HEAD_DIM) == tc).astype(BF16)
    dt_x = _sel_dot(dt, expand)
    acs_x = _sel_dot(acs, expand)
    end_x = acs_x[CHUNK - 1:CHUNK, :]
    causal = row >= lane
    return dict(dt=dt, a=a, pre=pre, head_lane=head_lane, acs=acs, acs_t=acs_t, expand=expand, reduce=reduce,
                dt_x=dt_x, acs_x=acs_x, end_x=end_x, causal=causal, row=row, lane=lane)


def _ssd_decay(cm, h):
    seg = cm["acs"][:, h:h + 1] - cm["acs_t"][h:h + 1, :]
    return jnp.where(cm["causal"], jnp.exp(jnp.minimum(seg, 0.0)), 0.0)


def _ssd_fwd(xact, proj3, dtr_g, dtb_g, alog_g, dskip_x, snw):
    b, s, _ = xact.shape
    nc = s // CHUNK
    g4 = SSD_GROUPS
    cps = min(nc, max(1, SSD_CHUNKS_PER_STEP // b))
    rows_per_step = cps * CHUNK

    def kern(xs_ref, bm_ref, cm_ref, zs_ref, dtr_ref, dtb_ref, alog_ref, dsk_ref, snw_ref,
             y_ref, yn_ref, hst_ref, h_sc):
        @pl.when(pl.program_id(2) == 0)
        def _():
            h_sc[...] = jnp.zeros_like(h_sc)

        def chunk(ci, _):
            for bi in range(b):
                one = pl.ds(bi, 1)
                chunk_of(ci, *[r.at[one] for r in (xs_ref, bm_ref, cm_ref, zs_ref, dtr_ref, y_ref, yn_ref, hst_ref)],
                         h_sc.at[bi])
            return 0

        def chunk_of(ci, xs_ref, bm_ref, cm_ref, zs_ref, dtr_ref, y_ref, yn_ref, hst_ref, h_sc):
            rows = pl.ds(pl.multiple_of(ci * CHUNK, CHUNK), CHUNK)
            cm = _ssd_common(dtr_ref[0, 0, rows, :], dtb_ref[0], alog_ref[0])
            x = xs_ref[0, rows, :]
            bmb = bm_ref[0, rows, :].astype(BF16)
            cmb = cm_ref[0, rows, :].astype(BF16)
            h_in = h_sc[...]
            hst_ref[0, ci, 0] = h_in
            xdt = x * cm["dt_x"]
            xdtb = xdt.astype(BF16)
            cb = _dot(cmb, bmb, NT)
            y_off = _dot(cmb, h_in.astype(BF16), NN) * jnp.exp(cm["acs_x"])
            for h in range(HEADS_PER_GROUP):
                lanes = slice(h * HEAD_DIM, (h + 1) * HEAD_DIM)
                m = (cb * _ssd_decay(cm, h)).astype(BF16)
                y_ref[0, rows, lanes] = _dot(m, xdtb[:, lanes], NN)
            y = y_ref[0, rows, :] + y_off + x * dsk_ref[...]
            y_ref[0, rows, :] = y
            w = (xdt * jnp.exp(cm["end_x"] - cm["acs_x"])).astype(BF16)
            h_sc[...] = h_in * jnp.exp(cm["end_x"]) + _dot(bmb, w, TN)
            zs = zs_ref[0, rows, :]
            y2 = y * (zs * _sigmoid(zs))
            yn_ref[0, rows, :] = (y2 * lax.rsqrt(jnp.mean(y2 * y2, axis=-1, keepdims=True) + EPS) * snw_ref[...]).astype(BF16)

        lax.fori_loop(0, cps, chunk, 0)

    gw = GROUP_WIDTH
    small = pl.BlockSpec((1, 1, LANES), lambda gi, bi, ci: (gi, 0, 0))
    xblk = pl.BlockSpec((b, rows_per_step, gw), lambda gi, bi, ci: (bi, ci, gi))
    return pl.pallas_call(
        kern, name="ssd_fwd",
        out_shape=(jax.ShapeDtypeStruct((b, s, SSD_WIDTH), F32), jax.ShapeDtypeStruct((b, s, SSD_WIDTH), BF16),
                   jax.ShapeDtypeStruct((b, nc, g4, SSD_STATE, gw), F32)),
        grid=(g4, 1, nc // cps),
        in_specs=[xblk,
                  pl.BlockSpec((b, rows_per_step, LANES), lambda gi, bi, ci: (bi, ci, SSD_WIDTH // LANES + gi)),
                  pl.BlockSpec((b, rows_per_step, LANES), lambda gi, bi, ci: (bi, ci, SSD_WIDTH // LANES + g4 + gi)),
                  pl.BlockSpec((b, rows_per_step, gw), lambda gi, bi, ci: (bi, ci, ZS0 // gw + gi)),
                  pl.BlockSpec((b, 1, rows_per_step, LANES), lambda gi, bi, ci: (bi, gi, ci, 0)),
                  small, small,
                  pl.BlockSpec((1, gw), lambda gi, bi, ci: (0, gi)),
                  pl.BlockSpec((1, gw), lambda gi, bi, ci: (0, gi))],
        out_specs=(xblk, xblk, pl.BlockSpec((b, cps, 1, SSD_STATE, gw), lambda gi, bi, ci: (bi, ci, gi, 0, 0))),
        scratch_shapes=[pltpu.VMEM((b, SSD_STATE, gw), F32)],
        compiler_params=_cparams("parallel", "parallel", "arbitrary"),
    )(xact, xact, xact, proj3, dtr_g, dtb_g, alog_g, dskip_x, snw)


def _ssd_bwd(dyn3, y3, xact, proj3, hst, dtr_g, dtb_g, alog_g, dskip_x, snw, dproj3):
    b, s, _ = xact.shape
    nc = s // CHUNK
    g4 = SSD_GROUPS
    gw = GROUP_WIDTH

    cps = min(nc, max(1, SSD_CHUNKS_PER_STEP // b))
    rows_per_step = cps * CHUNK

    def one_chunk(dyn_ref, y_ref, xs_ref, bm_ref, cm_ref, zs_ref, hst_ref, dtr_ref, dtb_ref, alog_ref, dsk_ref, snw_ref,
                  dxs_ref, dbm_ref, dcm_ref, dzs_ref, ddtr_ref, dsnw_ref, dalog_ref, ddtb_ref, ddsk_ref, dh_sc):
        cm = _ssd_common(dtr_ref[0, 0], dtb_ref[0], alog_ref[0])
        row, lane = cm["row"], cm["lane"]
        y = y_ref[0]
        zs = zs_ref[0]
        sg = _sigmoid(zs)
        silu = zs * sg
        y2 = y * silu
        rstd = lax.rsqrt(jnp.mean(y2 * y2, axis=-1, keepdims=True) + EPS)
        y2h = y2 * rstd
        dyn = dyn_ref[0]
        dsnw_ref[0] += jnp.sum(dyn * y2h, axis=0, keepdims=True)
        gwv = dyn * snw_ref[...]
        dy2 = rstd * (gwv - y2h * jnp.mean(gwv * y2h, axis=-1, keepdims=True))
        dzs_ref[0] = (dy2 * y * (sg * (1.0 + zs * (1.0 - sg)))).astype(BF16)
        dy = dy2 * silu
        dyb = dy.astype(BF16)

        x = xs_ref[0]
        bmb = bm_ref[0].astype(BF16)
        cmb = cm_ref[0].astype(BF16)
        h_in = hst_ref[0, 0, 0]
        h_inb = h_in.astype(BF16)
        d_hn = dh_sc[...]
        d_hnb = d_hn.astype(BF16)
        xdt = x * cm["dt_x"]
        xdtb = xdt.astype(BF16)
        eacs = jnp.exp(cm["acs_x"])
        dte = jnp.exp(cm["end_x"] - cm["acs_x"])
        wb = (xdt * dte).astype(BF16)

        dsk_lanes = jnp.broadcast_to(jnp.sum(dy * x, axis=0, keepdims=True), (8, gw))
        ddsk_ref[0] += _sel_dot(dsk_lanes, cm["reduce"])[0:1, :]
        dyo = dy * eacs
        dyob = dyo.astype(BF16)
        dacs_x = dyo * _dot(cmb, h_inb, NN)
        dcm = _dot(dyob, h_inb, NT)
        dh_in = _dot(cmb, dyob, TN)
        dw = _dot(bmb, d_hnb, NN)
        dbm = _dot(wb, d_hnb, NT)
        dxdt = dw * dte
        e_l = dw * xdt * dte
        dacs_x = dacs_x - e_l
        dend_x = jnp.sum(e_l, axis=0, keepdims=True)
        chunk_decay = jnp.exp(cm["end_x"])
        dh_sc[...] = d_hn * chunk_decay + dh_in
        dend_x = dend_x + jnp.sum(d_hn * h_in, axis=0, keepdims=True) * chunk_decay
        last_row = lax.broadcasted_iota(jnp.int32, (CHUNK, gw), 0) == CHUNK - 1
        dacs_x = dacs_x + jnp.where(last_row, dend_x, 0.0)

        cb = _dot(cmb, bmb, NT)
        dcb = jnp.zeros((CHUNK, CHUNK), F32)
        dacs = jnp.zeros((CHUNK, LANES), F32)
        dacs_t = jnp.zeros((LANES, CHUNK), F32)
        for h in range(HEADS_PER_GROUP):
            lanes = slice(h * HEAD_DIM, (h + 1) * HEAD_DIM)
            decay = _ssd_decay(cm, h)
            m = cb * decay
            dm = _dot(dyb[:, lanes], xdtb[:, lanes], NT)
            dxs_ref[0, :, lanes] = _dot(m.astype(BF16), dyb[:, lanes], TN)
            dcb_h = dm * decay
            dcb = dcb + dcb_h
            n = dcb_h * cb
            dacs = dacs + jnp.where(lane == h, jnp.sum(n, axis=1, keepdims=True), 0.0)
            dacs_t = dacs_t + jnp.where(row == h, jnp.sum(n, axis=0, keepdims=True), 0.0)
        dcbb = dcb.astype(BF16)
        dcm_ref[0] = dcm + _dot(dcbb, bmb, NN)
        dbm_ref[0] = dbm + _dot(dcbb, cmb, TN)
        dxdt = dxdt + dxs_ref[0]
        dxs_ref[0] = dy * dsk_ref[...] + dxdt * cm["dt_x"]

        dacs = dacs - dacs_t.T + _sel_dot(dacs_x, cm["reduce"])
        ddt = _sel_dot(dxdt * x, cm["reduce"])
        triu = (row <= lane).astype(BF16)
        rc = _sel_dot(dacs, triu, left=True)
        ddt = ddt + cm["a"] * rc
        dalog_ref[0] += jnp.sum(cm["dt"] * rc, axis=0, keepdims=True) * cm["a"]
        ddtr = jnp.where(cm["head_lane"], ddt * _sigmoid(cm["pre"]), 0.0)
        ddtr_ref[0, 0] = ddtr
        ddtb_ref[0] += jnp.sum(ddtr, axis=0, keepdims=True)

    def kern(dyn_ref, y_ref, xs_ref, bm_ref, cm_ref, zs_ref, hst_ref, dtr_ref, dtb_ref, alog_ref, dsk_ref, snw_ref, _,
             dxs_ref, dbm_ref, dcm_ref, dzs_ref, ddtr_ref, dsnw_ref, dalog_ref, ddtb_ref, ddsk_ref, dh_sc):
        first = jnp.logical_and(pl.program_id(1) == 0, pl.program_id(2) == 0)

        @pl.when(first)
        def _():
            dsnw_ref[...] = jnp.zeros_like(dsnw_ref)
            dalog_ref[...] = jnp.zeros_like(dalog_ref)
            ddtb_ref[...] = jnp.zeros_like(ddtb_ref)
            ddsk_ref[...] = jnp.zeros_like(ddsk_ref)

        @pl.when(pl.program_id(2) == 0)
        def _():
            dh_sc[...] = jnp.zeros_like(dh_sc)

        def chunk(k, _):
            ci = cps - 1 - k
            rows = pl.ds(pl.multiple_of(ci * CHUNK, CHUNK), CHUNK)
            for bi in range(b):
                one = pl.ds(bi, 1)
                by_rows = [r.at[one, rows, :] for r in (dyn_ref, y_ref, xs_ref, bm_ref, cm_ref, zs_ref)]
                one_chunk(*by_rows, hst_ref.at[one, pl.ds(ci, 1)], dtr_ref.at[one, :, rows, :], dtb_ref, alog_ref, dsk_ref, snw_ref,
                          *[r.at[one, rows, :] for r in (dxs_ref, dbm_ref, dcm_ref, dzs_ref)], ddtr_ref.at[one, :, rows, :],
                          dsnw_ref, dalog_ref, ddtb_ref, ddsk_ref, dh_sc.at[bi])
            return 0

        lax.fori_loop(0, cps, chunk, 0)

    def rev(ci):
        return nc // cps - 1 - ci

    small = pl.BlockSpec((1, 1, LANES), lambda gi, bi, ci: (gi, 0, 0))
    xblk = pl.BlockSpec((b, rows_per_step, gw), lambda gi, bi, ci: (bi, rev(ci), gi))
    nblk = pl.BlockSpec((b, rows_per_step, LANES), lambda gi, bi, ci: (bi, rev(ci), gi))
    gvec = pl.BlockSpec((1, gw), lambda gi, bi, ci: (0, gi))
    gacc = pl.BlockSpec((1, 1, gw), lambda gi, bi, ci: (gi, 0, 0))
    return pl.pallas_call(
        kern, name="ssd_bwd",
        out_shape=(jax.ShapeDtypeStruct((b, s, SSD_WIDTH), F32),
                   jax.ShapeDtypeStruct((b, s, g4 * SSD_STATE), F32),
                   jax.ShapeDtypeStruct((b, s, g4 * SSD_STATE), F32),
                   jax.ShapeDtypeStruct(dproj3.shape, dproj3.dtype),
                   jax.ShapeDtypeStruct((b, g4, s, LANES), F32),
                   jax.ShapeDtypeStruct((g4, 1, gw), F32),
                   jax.ShapeDtypeStruct((g4, 1, LANES), F32),
                   jax.ShapeDtypeStruct((g4, 1, LANES), F32),
                   jax.ShapeDtypeStruct((g4, 1, LANES), F32)),
        grid=(g4, 1, nc // cps),
        in_specs=[xblk, xblk, xblk,
                  pl.BlockSpec((b, rows_per_step, LANES), lambda gi, bi, ci: (bi, rev(ci), SSD_WIDTH // LANES + gi)),
                  pl.BlockSpec((b, rows_per_step, LANES), lambda gi, bi, ci: (bi, rev(ci), SSD_WIDTH // LANES + g4 + gi)),
                  pl.BlockSpec((b, rows_per_step, gw), lambda gi, bi, ci: (bi, rev(ci), ZS0 // gw + gi)),
                  pl.BlockSpec((b, cps, 1, SSD_STATE, gw), lambda gi, bi, ci: (bi, rev(ci), gi, 0, 0)),
                  pl.BlockSpec((b, 1, rows_per_step, LANES), lambda gi, bi, ci: (bi, gi, rev(ci), 0)),
                  small, small, gvec, gvec, ANY],
        out_specs=(xblk, nblk, nblk,
                   pl.BlockSpec((b, rows_per_step, gw), lambda gi, bi, ci: (bi, rev(ci), ZS0 // gw + gi)),
                   pl.BlockSpec((b, 1, rows_per_step, LANES), lambda gi, bi, ci: (bi, gi, rev(ci), 0)),
                   gacc, small, small, small),
        input_output_aliases={12: 3},
        scratch_shapes=[pltpu.VMEM((b, SSD_STATE, gw), F32)],
        compiler_params=_cparams("parallel", "arbitrary", "arbitrary"),
    )(dyn3, y3, xact, xact, xact, proj3, hst, dtr_g, dtb_g, alog_g, dskip_x, snw, dproj3)


def _adamw(w, g, m, v, name):
    r, c = w.shape
    tr = 128 if r % 128 == 0 else r
    tc = LANES if (tr == r and r > 128 and c % LANES == 0) else c

    def kern(w_ref, g_ref, m_ref, v_ref, d_ref, nm_ref, nv_ref):
        gv = g_ref[...]
        nm = ADAM_B1 * m_ref[...] + (1.0 - ADAM_B1) * gv
        nv = ADAM_B2 * v_ref[...] + (1.0 - ADAM_B2) * (gv * gv)
        m_hat = nm / (1.0 - ADAM_B1 ** ADAM_STEP)
        v_hat = nv / (1.0 - ADAM_B2 ** ADAM_STEP)
        d_ref[...] = -ADAM_LR * (m_hat / (jnp.sqrt(v_hat) + ADAM_EPS) + ADAM_WD * w_ref[...])
        nm_ref[...] = nm
        nv_ref[...] = nv

    blk = pl.BlockSpec((tr, tc), lambda i, j: (i, j))
    out = jax.ShapeDtypeStruct((r, c), F32)
    return pl.pallas_call(
        kern, name=name, out_shape=(out, out, out), grid=(r // tr, c // tc),
        in_specs=[blk] * 4, out_specs=(blk, blk, blk),
        compiler_params=_cparams("parallel", "parallel"),
    )(w, g, m, v)


ANY = pl.BlockSpec(memory_space=pl.ANY)


def _position():
    return lax.axis_index("x"), lax.axis_index("y"), lax.axis_index("c")


def _other_chips(x, y):
    return [(1 - x, y), (x, 1 - y), (1 - x, 1 - y)]


def _dma_sems(n):
    return [pltpu.SemaphoreType.DMA((n,)), pltpu.SemaphoreType.DMA((n,))]


class _Exchange:
    def __init__(self, inputs, out_shapes, sems, start, finish):
        self.inputs, self.out_shapes, self.sems, self.start, self.finish = inputs, out_shapes, sems, start, finish


def _gather_exchange(shards):
    n = len(shards)

    def copies(p_refs, out_refs, sems):
        send_sems, recv_sems = sems
        x, y, c = _position()
        me = 2 * x + y
        chips = _other_chips(x, y)

        def slab(a, chip, hf):
            half = shards[a].shape[1] // 2
            return out_refs[a].at[chip, :, pl.ds(hf * half, half)]

        def my_half(a):
            half = shards[a].shape[1] // 2
            return p_refs[a].at[:, pl.ds(c * half, half)]

        def over_ici(a, j, chip_from):
            px, py = chips[j]
            return pltpu.make_async_remote_copy(
                src_ref=my_half(a), dst_ref=slab(a, chip_from, c),
                send_sem=send_sems.at[3 * a + j], recv_sem=recv_sems.at[3 * a + j],
                device_id=(px, py, c), device_id_type=MESH)

        def to_sibling(a, j, hf):
            px, py = chips[j]
            return pltpu.make_async_remote_copy(
                src_ref=slab(a, 2 * px + py, hf), dst_ref=slab(a, 2 * px + py, hf),
                send_sem=send_sems.at[3 * (n + a) + j], recv_sem=recv_sems.at[3 * (n + a) + j],
                device_id=(x, y, 1 - c), device_id_type=MESH)

        own = [pltpu.make_async_remote_copy(
            src_ref=p_refs[a], dst_ref=out_refs[a].at[me], send_sem=send_sems.at[6 * n + a], recv_sem=recv_sems.at[6 * n + a],
            device_id=(x, y, 1 - c), device_id_type=MESH) for a in range(n)]
        first = [over_ici(a, j, me) for a in range(n) for j in range(3)]
        return chips, c, over_ici, to_sibling, first, own

    def start(p_refs, out_refs, sems):
        _, _, _, _, first, own = copies(p_refs, out_refs, sems)
        for cp in first + own:
            cp.start()

    def finish(p_refs, out_refs, sems):
        chips, c, over_ici, to_sibling, first, own = copies(p_refs, out_refs, sems)
        passed = []
        for a in range(n):
            for j, (px, py) in enumerate(chips):
                over_ici(a, j, 2 * px + py).wait_recv()
                passed.append(to_sibling(a, j, c))
                passed[-1].start()
        for a in range(n):
            for j in range(3):
                to_sibling(a, j, 1 - c).wait_recv()
        for cp in first + passed:
            cp.wait_send()
        for cp in own:
            cp.wait()

    return _Exchange(list(shards), [jax.ShapeDtypeStruct((N_CHIPS, *v.shape), v.dtype) for v in shards],
                     _dma_sems(7 * n), start, finish)


def _swap_halves(parts, name):
    n = len(parts)

    def body(*refs):
        v_refs, out_refs = refs[:n], refs[n:2 * n]
        send_sems, recv_sems = refs[2 * n:]
        x, y, c = _position()
        copies = []
        for a in range(n):
            half = parts[a].shape[2] // 2
            copies.append(pltpu.make_async_remote_copy(
                src_ref=v_refs[a].at[:, :, pl.ds((1 - c) * half, half)], dst_ref=out_refs[a],
                send_sem=send_sems.at[a], recv_sem=recv_sems.at[a], device_id=(x, y, 1 - c), device_id_type=MESH))
        for cp in copies:
            cp.start()
        for cp in copies:
            cp.wait()

    return pl.pallas_call(
        body, name=name,
        out_shape=[jax.ShapeDtypeStruct((v.shape[0], v.shape[1], v.shape[2] // 2), v.dtype) for v in parts],
        in_specs=[ANY] * n, out_specs=[ANY] * n,
        scratch_shapes=_dma_sems(n),
    )(*parts)


def _all_to_all_exchange(parts):
    n = len(parts)

    def sends(p_refs, out_refs, sems):
        send_sems, recv_sems = sems
        x, y, c = _position()
        return [pltpu.make_async_remote_copy(
            src_ref=p_refs[a].at[2 * px + py], dst_ref=out_refs[a].at[j],
            send_sem=send_sems.at[3 * a + j], recv_sem=recv_sems.at[3 * a + j],
            device_id=(px, py, c), device_id_type=MESH) for a in range(n) for j, (px, py) in enumerate(_other_chips(x, y))]

    def start(p_refs, out_refs, sems):
        for cp in sends(p_refs, out_refs, sems):
            cp.start()

    def finish(p_refs, out_refs, sems):
        for cp in sends(p_refs, out_refs, sems):
            cp.wait()

    return _Exchange(list(parts), [jax.ShapeDtypeStruct((N_CHIPS - 1, *v.shape[1:]), v.dtype) for v in parts],
                     _dma_sems(3 * n), start, finish)


def _join_halves(wholes):
    n = len(wholes)

    def body(*refs):
        out_refs = refs[n:2 * n]
        send_sems, recv_sems = refs[2 * n:]
        x, y, c = _position()
        copies = []
        for a in range(n):
            half = wholes[a].shape[1] // 2
            mine = out_refs[a].at[:, pl.ds(c * half, half)]
            copies.append(pltpu.make_async_remote_copy(
                src_ref=mine, dst_ref=mine, send_sem=send_sems.at[a], recv_sem=recv_sems.at[a],
                device_id=(x, y, 1 - c), device_id_type=MESH))
        for cp in copies:
            cp.start()
        for cp in copies:
            cp.wait()

    return pl.pallas_call(
        body, name="grad_join_halves",
        out_shape=[jax.ShapeDtypeStruct(v.shape, v.dtype) for v in wholes],
        in_specs=[ANY] * n, out_specs=[ANY] * n,
        input_output_aliases={a: a for a in range(n)},
        scratch_shapes=_dma_sems(n),
    )(*wholes)


STRIP = 256


def _add_halves(g, sw, place, name):
    n, rows, cols = g.shape
    nb = cols // 2 // STRIP

    def kern(p_ref, g_ref, s_ref, o_ref):
        o_ref[...] = (g_ref[...] + s_ref[...]).astype(BF16)

    blk = pl.BlockSpec((1, rows, STRIP), lambda j, i, p_ref: (j, 0, i))
    return pl.pallas_call(
        kern, name=name,
        out_shape=jax.ShapeDtypeStruct((n, rows, cols // 2), BF16),
        grid_spec=pltpu.PrefetchScalarGridSpec(
            num_scalar_prefetch=1, grid=(n, nb),
            in_specs=[pl.BlockSpec((1, rows, STRIP), lambda j, i, p_ref: (j, 0, p_ref[0] * nb + i)), blk],
            out_specs=blk),
        compiler_params=_cparams("parallel", "parallel"),
    )(place, g, sw)


def _sum_chips(own, rx, place, name):
    _, rows, half = rx.shape
    nb = half // STRIP

    def kern(p_ref, own_ref, r_ref, o_ref):
        total = own_ref[0].astype(F32)
        for j in range(N_CHIPS - 1):
            total = total + r_ref[j].astype(F32)
        o_ref[...] = total

    return pl.pallas_call(
        kern, name=name,
        out_shape=jax.ShapeDtypeStruct((rows, 2 * half), F32),
        grid_spec=pltpu.PrefetchScalarGridSpec(
            num_scalar_prefetch=1, grid=(nb,),
            in_specs=[pl.BlockSpec((1, rows, STRIP), lambda i, p_ref: (p_ref[1], 0, i)),
                      pl.BlockSpec((N_CHIPS - 1, rows, STRIP), lambda i, p_ref: (0, 0, i))],
            out_specs=pl.BlockSpec((rows, STRIP), lambda i, p_ref: (0, p_ref[0] * nb + i))),
        compiler_params=_cparams("parallel"),
    )(place, own, rx)


def _gather_small(v, reduce, name):
    rows = v.shape[0]

    def body(v_ref, out_ref, buf, send_sems, recv_sems):
        x, y, c = _position()
        me = 4 * x + 2 * y + c
        buf[me] = v_ref[...]
        peers = [(x ^ (k >> 2), y ^ ((k >> 1) & 1), c ^ (k & 1)) for k in range(1, 8)]
        copies = [pltpu.make_async_remote_copy(
            src_ref=v_ref, dst_ref=buf.at[me],
            send_sem=send_sems.at[k], recv_sem=recv_sems.at[k],
            device_id=peer, device_id_type=MESH) for k, peer in enumerate(peers)]
        for cp in copies:
            cp.start()
        for k, (px, py, pc) in enumerate(peers):
            pltpu.make_async_remote_copy(
                src_ref=v_ref, dst_ref=buf.at[4 * px + 2 * py + pc],
                send_sem=send_sems.at[k], recv_sem=recv_sems.at[k],
                device_id=(px, py, pc), device_id_type=MESH).wait_recv()
        for cp in copies:
            cp.wait_send()
        if reduce:
            total = buf[0]
            for d in range(1, 8):
                total = total + buf[d]
            out_ref[...] = total
        else:
            out_ref[...] = buf[...]

    vm = pl.BlockSpec(memory_space=pltpu.VMEM)
    return pl.pallas_call(
        body, name=name,
        out_shape=jax.ShapeDtypeStruct((rows, LANES) if reduce else (8, rows, LANES), F32),
        in_specs=[vm], out_specs=vm,
        scratch_shapes=[pltpu.VMEM((8, rows, LANES), F32), pltpu.SemaphoreType.DMA((7,)), pltpu.SemaphoreType.DMA((7,))],
    )(v)


def _pad_rows(a, rows):
    return jnp.pad(a, ((0, rows - a.shape[0]), (0, 0)))


def _lane_pad(v):
    n = v.shape[1]
    return jnp.pad(v, ((0, 0), (0, -n % LANES)))


def _gather_all(w_in, w_attn_out, w_ssm_out, w_o, conv_w):
    d = D_MODEL
    w_proj_t = _gather_exchange([w_in[0].T.astype(BF16)])
    out_w = _gather_exchange([a[0].astype(BF16) for a in (w_attn_out, w_ssm_out, w_o)])
    conv_rows = conv_w[0].size // LANES
    conv_all = _gather_small(conv_w[0].reshape(conv_rows, LANES), False, "gather_conv_w")
    conv_w_all = conv_all[0::2].reshape(N_CHIPS, CONV_K, CONV_DIM // N_CHIPS).transpose(1, 0, 2).reshape(CONV_K, CONV_DIM)

    return w_proj_t, out_w, conv_w_all


def _local_step(x, loss_target, norm_w, w_proj_t, conv_w_all, conv_b, dt_bias, a_log, d_skip, ssm_norm_w,
                out_w, final_norm_w, grad_exchange=None):
    b, s, d = x.shape
    t = b * s
    g4, hg = SSD_GROUPS, HEADS_PER_GROUP
    dtb_g = _lane_pad(dt_bias.reshape(g4, hg)).reshape(g4, 1, LANES)
    alog_g = _lane_pad(a_log.reshape(g4, hg)).reshape(g4, 1, LANES)
    dskip_x = jnp.repeat(d_skip, HEAD_DIM, axis=1)
    fnw = final_norm_w.reshape(1, d)

    x2 = x.reshape(t, d)
    if isinstance(w_proj_t, _Exchange):
        h, w_in_t = _rms_fwd(x2, norm_w, exchange=w_proj_t)
        w_proj_t = _to_proj_layout(w_in_t.reshape(D_PROJ, d))
    else:
        h = _rms_fwd(x2, norm_w)
    big_tm = min(t, 2048)
    if isinstance(out_w, _Exchange):
        proj, *out_w = _matmul(h, w_proj_t, tb=True, tm=big_tm, tn=1280, tk=1024, name="proj", exchange=out_w)
    else:
        proj = _matmul(h, w_proj_t, tb=True, tm=big_tm, tn=1280, tk=1024, name="proj")
    w_ao, w_so, w_oo = (w.reshape(-1, d) for w in out_w)
    proj3 = proj.reshape(b, s, NP)
    o3, yp3 = _attn_fwd(proj3)
    xact = _conv_fwd(proj3, conv_w_all, conv_b)
    dtr = proj3[:, :, DT0:DT0 + g4 * hg].reshape(b, s, g4, hg).transpose(0, 2, 1, 3)
    dtr_g = jnp.pad(dtr, ((0, 0), (0, 0), (0, 0), (0, LANES - hg)))
    y3, yn3, hst = _ssd_fwd(xact, proj3, dtr_g, dtb_g, alog_g, dskip_x, ssm_norm_w)
    yp = yp3.reshape(t, D_MODEL)
    yn = yn3.reshape(t, SSD_WIDTH)
    ya = _matmul(yp, w_ao, tm=1024, tn=1024, tk=1024, name="attn_out")
    ys = _matmul(yn, w_so, tm=1024, tn=1024, tk=2048, name="ssm_out")
    merged = _merge_fwd(proj, ya, ys)
    mo = _matmul(merged, w_oo, tm=1024, tn=1024, tk=1024, name="out_proj")
    dout, doutb, loss_part, d_fnw = _final_fwd_bwd(x2, mo, loss_target.reshape(t, d), fnw)

    dmerged = _matmul(doutb, w_oo, tb=True, tm=1024, tn=1024, tk=1024, name="d_merged")
    g_wo = _matmul(merged, doutb, ta=True, tm=1024, tn=1024, tk=1024, name="g_w_o")
    dya, dys, dproj = _merge_bwd(dmerged, proj, ya, ys)
    dyp = _matmul(dya, w_ao, tb=True, tm=1024, tn=1024, tk=1024, name="d_attn_pre")
    g_wao = _matmul(yp, dya, ta=True, tm=1024, tn=1024, tk=1024, name="g_w_attn_out")
    dyn = _matmul(dys, w_so, tb=True, tm=1024, tn=2048, tk=1024, name="d_ssm_norm")
    g_wso = _matmul(yn, dys, ta=True, tm=1024, tn=1024, tk=1024, name="g_w_ssm_out")
    dproj3 = _attn_bwd(proj3, dyp.reshape(b, s, D_MODEL), o3, dproj.reshape(b, s, NP))
    (dxs, dbm, dcm, dproj3, ddtr_g, d_snw_g, d_alog_g, d_dtb_g, d_dsk_g) = _ssd_bwd(
        dyn.reshape(b, s, SSD_WIDTH), y3, xact, proj3, hst, dtr_g, dtb_g, alog_g, dskip_x, ssm_norm_w, dproj3)
    dproj3, g_cw_xs, g_cb_xs = _conv_bwd(dxs, proj3, conv_w_all, conv_b, 0, "conv_bwd_x", dproj3)
    dproj3, g_cw_bm, g_cb_bm = _conv_bwd(dbm, proj3, conv_w_all, conv_b, SSD_WIDTH, "conv_bwd_b", dproj3)
    dproj3, g_cw_cm, g_cb_cm = _conv_bwd(dcm, proj3, conv_w_all, conv_b, SSD_WIDTH + g4 * SSD_STATE, "conv_bwd_c", dproj3)
    ddt = ddtr_g[:, :, :, :hg].transpose(0, 2, 1, 3).reshape(b, s, g4 * hg).astype(BF16)
    ddt = jnp.pad(ddt, ((0, 0), (0, 0), (0, DT_PAD - g4 * hg)))
    dproj = lax.dynamic_update_slice(dproj3, ddt, (0, 0, DT0)).reshape(t, NP)
    exchanged = []
    if grad_exchange:
        g_wproj, *got = _matmul(dproj, h, ta=True, tm=1280, tn=1024, tk=1024, name="g_w_in",
                                exchange=grad_exchange([g_wao, g_wso, g_wo], "out"))
        exchanged += got
        dh, *got = _matmul(dproj, w_proj_t, tm=big_tm, tn=1024, tk=1280, name="d_h", exchange=grad_exchange([g_wproj], "in"))
        exchanged += got
    else:
        g_wproj = _matmul(dproj, h, ta=True, tm=1280, tn=1024, tk=1024, name="g_w_in")
        dh = _matmul(dproj, w_proj_t, tm=big_tm, tn=1024, tk=1280, name="d_h")
    grad_x, d_nw = _rms_bwd(dh, x2, norm_w, dout)
    g_cw = jnp.concatenate([g_cw_xs, g_cw_bm, g_cw_cm], axis=1)
    g_cb = jnp.concatenate([g_cb_xs, g_cb_bm, g_cb_cm], axis=1)
    return (loss_part, grad_x, d_nw, g_wproj, g_cw, g_cb, d_dtb_g, d_alog_g, d_dsk_g, d_snw_g, g_wao, g_wso, g_wo, d_fnw,
            exchanged)


def kernel(x, norm_w, w_in, conv_w, conv_b, dt_bias, a_log, d_skip, ssm_norm_w, w_attn_out, w_ssm_out, w_o, final_norm_w, loss_target, m_norm_w, m_w_in, m_conv_w, m_conv_b, m_dt_bias, m_a_log, m_d_skip, m_ssm_norm_w, m_w_attn_out, m_w_ssm_out, m_w_o, m_final_norm_w, v_norm_w, v_w_in, v_conv_w, v_conv_b, v_dt_bias, v_a_log, v_d_skip, v_ssm_norm_w, v_w_attn_out, v_w_ssm_out, v_w_o, v_final_norm_w):
    b, s, d = x.shape
    core = lax.axis_index("c")
    g4, hg = SSD_GROUPS, HEADS_PER_GROUP
    shard_cols = w_in.shape[2]
    w_proj_t, out_w, conv_w_all = _gather_all(w_in, w_attn_out, w_ssm_out, w_o, conv_w)
    chip = 2 * lax.axis_index("x") + lax.axis_index("y")
    place = jnp.stack([core, chip]).astype(jnp.int32)
    chip_sums = []

    def grad_exchange(grads, which):
        if which == "in":
            slabs = _from_proj_layout(grads[0]).reshape(N_CHIPS, shard_cols, d)
        else:
            slabs = jnp.concatenate([g.reshape(N_CHIPS, -1, d) for g in grads], axis=1)
        from_sibling, = _swap_halves([slabs], "grad_swap_halves_" + which)
        chip_sums.append(_add_halves(slabs, from_sibling, place, "grad_add_halves_" + which))
        return _all_to_all_exchange(chip_sums[-1:])

    (loss_part, grad_x, d_nw, _, g_cw, g_cb, d_dtb_g, d_alog_g, d_dsk_g, d_snw_g, _, _, _, d_fnw, from_chips) = _local_step(
        x, loss_target, norm_w, w_proj_t, conv_w_all, conv_b, dt_bias, a_log, d_skip, ssm_norm_w, out_w, final_norm_w,
        grad_exchange)
    wholes = [_sum_chips(o, r, place, "grad_sum_chips_%d" % i) for i, (o, r) in enumerate(zip(chip_sums, from_chips))]
    g_out, g_w_in = _join_halves(wholes)

    small = jnp.concatenate([
        loss_part, d_nw, g_cb, _lane_pad(d_dtb_g[:, 0, :hg].reshape(1, -1)), _lane_pad(d_alog_g[:, 0, :hg].reshape(1, -1)),
        _lane_pad(d_dsk_g[:, 0, :hg].reshape(1, -1)),
        d_snw_g.reshape(1, -1), d_fnw, g_cw.reshape(1, -1)], axis=1)
    small_rows = small.shape[1] // LANES
    reduced = _gather_small(_pad_rows(small.reshape(small_rows, LANES), -(-small_rows // 8) * 8), True, "reduce_small")
    flat = reduced.reshape(-1)

    def take(start, n):
        return flat[start:start + n].reshape(1, n)

    loss = flat[0]
    pos = LANES
    g_norm_w = take(pos, d); pos += d
    g_conv_b = take(pos, CONV_DIM); pos += CONV_DIM
    g_dt_bias = take(pos, g4 * hg); pos += LANES
    g_a_log = take(pos, g4 * hg); pos += LANES
    g_d_skip = take(pos, g4 * hg); pos += LANES
    g_ssm_norm_w = take(pos, SSD_WIDTH); pos += SSD_WIDTH
    g_final_norm_w = take(pos, d); pos += d
    conv_cols = CONV_DIM // N_CHIPS
    g_conv_w = lax.dynamic_slice_in_dim(flat[pos:pos + CONV_K * CONV_DIM].reshape(CONV_K, CONV_DIM), chip * conv_cols, conv_cols, axis=1)

    rows_ao, rows_so = D_MODEL // N_CHIPS, SSD_WIDTH // N_CHIPS
    g_w_attn_out = g_out[:rows_ao]
    g_w_ssm_out = g_out[rows_ao:rows_ao + rows_so]
    g_w_o = g_out[rows_ao + rows_so:]

    names = ["norm_w", "w_in", "conv_w", "conv_b", "dt_bias", "a_log", "d_skip", "ssm_norm_w",
             "w_attn_out", "w_ssm_out", "w_o", "final_norm_w"]
    weights = [norm_w, w_in, conv_w, conv_b, dt_bias, a_log, d_skip, ssm_norm_w, w_attn_out, w_ssm_out, w_o, final_norm_w]
    grads = [g_norm_w, g_w_in, g_conv_w, g_conv_b, g_dt_bias, g_a_log, g_d_skip, g_ssm_norm_w,
             g_w_attn_out, g_w_ssm_out, g_w_o, g_final_norm_w]
    ms = [m_norm_w, m_w_in, m_conv_w, m_conv_b, m_dt_bias, m_a_log, m_d_skip, m_ssm_norm_w,
          m_w_attn_out, m_w_ssm_out, m_w_o, m_final_norm_w]
    vs = [v_norm_w, v_w_in, v_conv_w, v_conv_b, v_dt_bias, v_a_log, v_d_skip, v_ssm_norm_w,
          v_w_attn_out, v_w_ssm_out, v_w_o, v_final_norm_w]
    out_g, out_d, out_m, out_v = [], [], [], []
    for name, w, g, m, v in zip(names, weights, grads, ms, vs):
        if name == "w_in":
            to2, back = (lambda a: a[0].T), (lambda a: a.T.reshape(w.shape))
        else:
            to2, back = (lambda a: a.reshape(g.shape)), (lambda a: a.reshape(w.shape))
        dlt, nm, nv = _adamw(to2(w), g, to2(m), to2(v), "adamw_" + name)
        out_g.append(back(g))
        out_d.append(back(dlt))
        out_m.append(back(nm))
        out_v.append(back(nv))

    return (loss, grad_x.reshape(b, s, d), *out_g, *out_d, *out_m, *out_v)
```

```python
import jax
import jax.numpy as jnp
from jax import lax
from jax.experimental import pallas as pl
from jax.experimental.pallas import tpu as pltpu

F32 = jnp.float32
BF16 = jnp.bfloat16
MESH = pl.DeviceIdType.MESH

D_MODEL = 1024
SB_HEADS = 16
HEAD_DIM = 64
SSD_WIDTH = 2048
SSD_GROUPS = 4
GROUP_WIDTH = SSD_WIDTH // SSD_GROUPS
HEADS_PER_GROUP = 8
SSD_STATE = 128
CHUNK = 128
CONV_K = 4
CONV_DIM = 3072
D_PROJ = 11296
EPS = 1e-6
ADAM_LR, ADAM_B1, ADAM_B2, ADAM_EPS, ADAM_WD, ADAM_STEP = 0.001, 0.9, 0.999, 1e-08, 0.01, 10

LANES = 128
HP_WIDTH = 4 * LANES
ZS0, GATE0, XBC0, DT0 = 4096, 6144, 8192, 11264
DT_PAD = 256
NP = DT0 + DT_PAD
N_CHIPS = 4
VMEM_LIMIT = 56 * 1024 * 1024


N_HP = SB_HEADS // 2
W_ZS0, W_XBC0, W_DT0, W_GATE0 = 4096, 6144, 9216, 9248


def _to_proj_layout(wt):
    d = wt.shape[1]
    pairs = wt[:W_ZS0].reshape(4, N_HP, LANES, d).transpose(1, 0, 2, 3).reshape(W_ZS0, d)
    return jnp.concatenate([pairs, wt[W_ZS0:W_XBC0], wt[W_GATE0:], wt[W_XBC0:W_DT0], wt[W_DT0:W_GATE0],
                            jnp.zeros((NP - D_PROJ, d), wt.dtype)], axis=0)


def _from_proj_layout(gt):
    d = gt.shape[1]
    qkvz = gt[:ZS0].reshape(N_HP, 4, LANES, d).transpose(1, 0, 2, 3).reshape(ZS0, d)
    return jnp.concatenate([qkvz, gt[ZS0:GATE0], gt[XBC0:DT0], gt[DT0:DT0 + W_GATE0 - W_DT0], gt[GATE0:XBC0]], axis=0)


def _cparams(*sem):
    return pltpu.CompilerParams(dimension_semantics=sem or None, vmem_limit_bytes=VMEM_LIMIT)


def _sigmoid(z):
    return 1.0 / (1.0 + jnp.exp(-z))


def _dot(a, b, dims, precision=None):
    return lax.dot_general(a, b, (dims, ((), ())), preferred_element_type=F32, precision=precision)


NN = ((1,), (0,))
NT = ((1,), (1,))
TN = ((0,), (0,))


def _matmul(a, b, *, ta=False, tb=False, out_dtype=F32, tm, tn, tk, name, exchange=None):
    m, k = (a.shape[1], a.shape[0]) if ta else a.shape
    n = b.shape[0] if tb else b.shape[1]
    assert m % tm == 0 and n % tn == 0 and k % tk == 0, (name, m, n, k)
    grid = (m // tm, n // tn, k // tk)
    nk = grid[2]
    use_scratch = out_dtype != F32
    dims = ((0,) if ta else (1,), (1,) if tb else (0,))
    n_in = len(exchange.inputs) if exchange else 0
    n_out = len(exchange.out_shapes) if exchange else 0

    def kern(a_ref, b_ref, *rest):
        x_in, o_ref, x_out, scratch = rest[:n_in], rest[n_in], rest[n_in + 1:n_in + 1 + n_out], rest[n_in + 1 + n_out:]
        acc = scratch[0] if use_scratch else o_ref
        step = [pl.program_id(ax) for ax in range(3)]
        if exchange:
            sems = scratch[1:] if use_scratch else scratch

            @pl.when(jnp.logical_and(jnp.logical_and(step[0] == 0, step[1] == 0), step[2] == 0))
            def _():
                exchange.start(x_in, x_out, sems)

        @pl.when(step[2] == 0)
        def _():
            acc[...] = jnp.zeros_like(acc)

        acc[...] += _dot(a_ref[...], b_ref[...], dims)
        if use_scratch:
            @pl.when(step[2] == nk - 1)
            def _():
                o_ref[...] = acc[...].astype(out_dtype)
        if exchange:
            @pl.when(jnp.logical_and(jnp.logical_and(step[0] == grid[0] - 1, step[1] == grid[1] - 1), step[2] == nk - 1))
            def _():
                exchange.finish(x_in, x_out, sems)

    a_spec = pl.BlockSpec((tk, tm), lambda i, j, q: (q, i)) if ta else pl.BlockSpec((tm, tk), lambda i, j, q: (i, q))
    b_spec = pl.BlockSpec((tn, tk), lambda i, j, q: (j, q)) if tb else pl.BlockSpec((tk, tn), lambda i, j, q: (q, j))
    out = pl.pallas_call(
        kern, name=name,
        out_shape=[jax.ShapeDtypeStruct((m, n), out_dtype)] + (list(exchange.out_shapes) if exchange else []),
        grid=grid,
        in_specs=[a_spec, b_spec] + [ANY] * n_in,
        out_specs=[pl.BlockSpec((tm, tn), lambda i, j, q: (i, j))] + [ANY] * n_out,
        scratch_shapes=([pltpu.VMEM((tm, tn), F32)] if use_scratch else []) + (list(exchange.sems) if exchange else []),
        compiler_params=_cparams("arbitrary", "arbitrary", "arbitrary") if exchange else _cparams("parallel", "parallel", "arbitrary"),
    )(a, b, *(exchange.inputs if exchange else []))
    return out if exchange else out[0]


ROWS = 512


def _rms_fwd(x2, w, exchange=None):
    t, d = x2.shape
    steps = t // ROWS
    n_in = len(exchange.inputs) if exchange else 0
    n_out = len(exchange.out_shapes) if exchange else 0

    def kern(x_ref, w_ref, *rest):
        x_in, h_ref, x_out, sems = rest[:n_in], rest[n_in], rest[n_in + 1:n_in + 1 + n_out], rest[n_in + 1 + n_out:]
        if exchange:
            @pl.when(pl.program_id(0) == 0)
            def _():
                exchange.start(x_in, x_out, sems)

        x = x_ref[...]
        r = lax.rsqrt(jnp.mean(x * x, axis=-1, keepdims=True) + EPS)
        h_ref[...] = (x * r * w_ref[...]).astype(BF16)
        if exchange:
            @pl.when(pl.program_id(0) == steps - 1)
            def _():
                exchange.finish(x_in, x_out, sems)

    out = pl.pallas_call(
        kern, name="rms_fwd",
        out_shape=[jax.ShapeDtypeStruct((t, d), BF16)] + (list(exchange.out_shapes) if exchange else []),
        grid=(steps,),
        in_specs=[pl.BlockSpec((ROWS, d), lambda i: (i, 0)), pl.BlockSpec((1, d), lambda i: (0, 0))] + [ANY] * n_in,
        out_specs=[pl.BlockSpec((ROWS, d), lambda i: (i, 0))] + [ANY] * n_out,
        scratch_shapes=list(exchange.sems) if exchange else [],
        compiler_params=_cparams("arbitrary" if exchange else "parallel"),
    )(x2, w, *(exchange.inputs if exchange else []))
    return out if exchange else out[0]


def _rms_bwd(dh, x2, w, dout):
    t, d = x2.shape

    def kern(dh_ref, x_ref, w_ref, dout_ref, gx_ref, dw_ref):
        @pl.when(pl.program_id(0) == 0)
        def _():
            dw_ref[...] = jnp.zeros_like(dw_ref)

        x = x_ref[...]
        r = lax.rsqrt(jnp.mean(x * x, axis=-1, keepdims=True) + EPS)
        xh = x * r
        g = dh_ref[...]
        dw_ref[...] += jnp.sum(g * xh, axis=0, keepdims=True)
        gw = g * w_ref[...]
        gx_ref[...] = dout_ref[...] + r * (gw - xh * jnp.mean(gw * xh, axis=-1, keepdims=True))

    row = pl.BlockSpec((ROWS, d), lambda i: (i, 0))
    vec = pl.BlockSpec((1, d), lambda i: (0, 0))
    return pl.pallas_call(
        kern, name="rms_bwd",
        out_shape=(jax.ShapeDtypeStruct((t, d), F32), jax.ShapeDtypeStruct((1, d), F32)),
        grid=(t // ROWS,),
        in_specs=[row, row, vec, row],
        out_specs=(row, vec),
        compiler_params=_cparams("arbitrary"),
    )(dh, x2, w, dout)


def _final_fwd_bwd(x2, mo, target, w):
    t, d = x2.shape

    def kern(x_ref, mo_ref, t_ref, w_ref, dout_ref, doutb_ref, loss_ref, dw_ref):
        @pl.when(pl.program_id(0) == 0)
        def _():
            loss_ref[...] = jnp.zeros_like(loss_ref)
            dw_ref[...] = jnp.zeros_like(dw_ref)

        u = x_ref[...] + mo_ref[...]
        r = lax.rsqrt(jnp.mean(u * u, axis=-1, keepdims=True) + EPS)
        uh = u * r
        wv = w_ref[...]
        err = uh * wv - t_ref[...]
        loss_ref[...] += (0.5 / d) * jnp.sum(err * err)
        dy = err * (1.0 / d)
        dw_ref[...] += jnp.sum(dy * uh, axis=0, keepdims=True)
        gw = dy * wv
        du = r * (gw - uh * jnp.mean(gw * uh, axis=-1, keepdims=True))
        dout_ref[...] = du
        doutb_ref[...] = du.astype(BF16)

    row = pl.BlockSpec((ROWS, d), lambda i: (i, 0))
    vec = pl.BlockSpec((1, d), lambda i: (0, 0))
    return pl.pallas_call(
        kern, name="final_fwd_bwd",
        out_shape=(jax.ShapeDtypeStruct((t, d), F32), jax.ShapeDtypeStruct((t, d), BF16),
                   jax.ShapeDtypeStruct((1, LANES), F32), jax.ShapeDtypeStruct((1, d), F32)),
        grid=(t // ROWS,),
        in_specs=[row, row, row, vec],
        out_specs=(row, row, pl.BlockSpec((1, LANES), lambda i: (0, 0)), vec),
        compiler_params=_cparams("arbitrary"),
    )(x2, mo, target, w)


def _merge_fwd(proj2, ya, ys):
    t = ya.shape[0]
    gblk = GATE0 // D_MODEL

    def kern(ga_ref, gs_ref, ya_ref, ys_ref, o_ref):
        o_ref[...] = (_sigmoid(ga_ref[...]) * ya_ref[...] + _sigmoid(gs_ref[...]) * ys_ref[...]).astype(BF16)

    row = pl.BlockSpec((ROWS, D_MODEL), lambda i: (i, 0))
    return pl.pallas_call(
        kern, name="merge_fwd",
        out_shape=jax.ShapeDtypeStruct((t, D_MODEL), BF16),
        grid=(t // ROWS,),
        in_specs=[pl.BlockSpec((ROWS, D_MODEL), lambda i: (i, gblk)),
                  pl.BlockSpec((ROWS, D_MODEL), lambda i: (i, gblk + 1)), row, row],
        out_specs=row,
        compiler_params=_cparams("parallel"),
    )(proj2, proj2, ya, ys)


def _merge_bwd(dm, proj2, ya, ys):
    t = ya.shape[0]
    gblk = GATE0 // D_MODEL

    def kern(dm_ref, ga_ref, gs_ref, ya_ref, ys_ref, dya_ref, dys_ref, dg_ref):
        g = dm_ref[...]
        sa = _sigmoid(ga_ref[...])
        ss = _sigmoid(gs_ref[...])
        dya_ref[...] = (g * sa).astype(BF16)
        dys_ref[...] = (g * ss).astype(BF16)
        dg_ref[:, :D_MODEL] = (g * ya_ref[...] * sa * (1.0 - sa)).astype(BF16)
        dg_ref[:, D_MODEL:] = (g * ys_ref[...] * ss * (1.0 - ss)).astype(BF16)

    row = pl.BlockSpec((ROWS, D_MODEL), lambda i: (i, 0))
    return pl.pallas_call(
        kern, name="merge_bwd",
        out_shape=(jax.ShapeDtypeStruct((t, D_MODEL), BF16), jax.ShapeDtypeStruct((t, D_MODEL), BF16),
                   jax.ShapeDtypeStruct((t, NP), BF16)),
        grid=(t // ROWS,),
        in_specs=[row, pl.BlockSpec((ROWS, D_MODEL), lambda i: (i, gblk)),
                  pl.BlockSpec((ROWS, D_MODEL), lambda i: (i, gblk + 1)), row, row],
        out_specs=(row, row, pl.BlockSpec((ROWS, 2 * D_MODEL), lambda i: (i, GATE0 // (2 * D_MODEL)))),
        compiler_params=_cparams("parallel"),
    )(dm, proj2, proj2, ya, ys)


TQ = 256
TK = 256
assert TQ == TK
HEAD_LANES = (slice(0, HEAD_DIM), slice(HEAD_DIM, 2 * HEAD_DIM))


def _tri(pred):
    r = lax.broadcasted_iota(jnp.int32, (TK, TK), 0)
    c = lax.broadcasted_iota(jnp.int32, (TK, TK), 1)
    return pred(r, c).astype(BF16)


def _split_bf16(v):
    hi = v.astype(BF16)
    lo = (v - hi.astype(F32)).astype(BF16)
    return hi, lo


def _tri_dot(v, tri):
    hi, lo = _split_bf16(v)
    return _dot(hi, tri, NN) + _dot(lo, tri, NN)


def _sb_logs(z, mask):
    l1p = jnp.log(1.0 + jnp.exp(-jnp.abs(z)))
    lb = jnp.minimum(z, 0.0) - l1p
    lom = -jnp.maximum(z, 0.0) - l1p
    if mask is not None:
        lom = jnp.where(mask, lom, 0.0)
    return lb, lom


def _sb_weights(lb, later, carry_r, mask):
    a = jnp.exp(lb + (later + carry_r))
    if mask is not None:
        a = jnp.where(mask, a, 0.0)
    return a


DEAD = -104.0


def _while_alive(n, carry, step):
    def alive(cr):
        return jnp.max(jnp.maximum(cr[0][0], cr[1][0])) > DEAD

    def cond(state):
        jj, go, _ = state
        return jnp.logical_and(jj < n, go)

    def body(state):
        jj, _, cr = state
        cr = step(jj, cr)
        return jj + 1, alive(cr), cr

    return lax.while_loop(cond, body, (jnp.int32(0), alive(carry), carry))[2]


Q_LANES, K_LANES, V_LANES, ZA_LANES = (slice(i * LANES, (i + 1) * LANES) for i in range(4))


def _split_heads(dst, src, scale=None):
    for h, lanes in enumerate(HEAD_LANES):
        v = src[:, lanes]
        dst[h] = (v if scale is None else v * scale).astype(BF16)


def _attn_fwd(proj3):
    b, s, _ = proj3.shape
    nq = s // TQ
    scale = HEAD_DIM ** -0.5

    def kern(x_ref, o_ref, yp_ref, qs, ks, vs):
        _split_heads(qs, x_ref[0, :, Q_LANES], scale)
        _split_heads(ks, x_ref[0, :, K_LANES])
        _split_heads(vs, x_ref[0, :, V_LANES])
        za_ref = x_ref.at[:, :, ZA_LANES]
        row = lax.broadcasted_iota(jnp.int32, (TQ, TK), 0)
        col = lax.broadcasted_iota(jnp.int32, (TQ, TK), 1)
        tri_gt = _tri(lambda j, sk: j > sk)

        def q_block(i, _):
            top = isinstance(i, int)
            r0 = i * TQ if top else pl.multiple_of(i * TQ, TQ)
            qh = [qs[h, pl.ds(r0, TQ), :] for h in range(2)]

            def k_blocks(blocks, carry):
                nb = range(len(blocks))
                kh = [[ks[h, pl.ds(c0, TK), :] for h in range(2)] for c0, _ in blocks]
                vh = [[vs[h, pl.ds(c0, TK), :] for h in range(2)] for c0, _ in blocks]
                z = [[_dot(qh[h], kh[bl][h], NT) for h in range(2)] for bl in nb]
                logs = [[_sb_logs(z[bl][h], blocks[bl][1]) for h in range(2)] for bl in nb]
                later = [[_tri_dot(logs[bl][h][1], tri_gt) for h in range(2)] for bl in nb]
                out = []
                for h in range(2):
                    carry_r, acc = carry[h]
                    for bl in nb:
                        lb, lom = logs[bl][h]
                        a = _sb_weights(lb, later[bl][h], carry_r, blocks[bl][1])
                        acc = acc + _dot(a.astype(BF16), vh[bl][h], NN)
                        carry_r = carry_r + (later[bl][h][:, 0:1] + lom[:, 0:1])
                    out.append((carry_r, acc))
                return tuple(out)

            start = (jnp.zeros((TQ, 1), F32), jnp.zeros((TQ, HEAD_DIM), F32))
            diag = (r0, col < row)
            if top:
                carry = k_blocks([diag], (start, start))
            else:
                carry = k_blocks([diag, (pl.multiple_of(r0 - TK, TK), None)], (start, start))
                carry = _while_alive(i - 1, carry, lambda jj, cr: k_blocks([(pl.multiple_of((i - 2 - jj) * TK, TK), None)], cr))
            for (_, acc), lanes in zip(carry, HEAD_LANES):
                o_ref[0, pl.ds(r0, TQ), lanes] = acc
                za = za_ref[0, pl.ds(r0, TQ), lanes]
                yp_ref[0, pl.ds(r0, TQ), lanes] = (acc * (za * _sigmoid(za))).astype(BF16)
            return 0

        q_block(0, 0)
        lax.fori_loop(1, nq, q_block, 0)

    out_spec = pl.BlockSpec((1, s, LANES), lambda bi, hp: (bi, 0, hp))
    return pl.pallas_call(
        kern, name="attn_fwd",
        out_shape=(jax.ShapeDtypeStruct((b, s, D_MODEL), F32), jax.ShapeDtypeStruct((b, s, D_MODEL), BF16)),
        grid=(b, SB_HEADS // 2),
        in_specs=[pl.BlockSpec((1, s, HP_WIDTH), lambda bi, hp: (bi, 0, hp))],
        out_specs=(out_spec, out_spec),
        scratch_shapes=[pltpu.VMEM((2, s, HEAD_DIM), BF16)] * 3,
        compiler_params=_cparams("parallel", "parallel"),
    )(proj3)


def _attn_bwd(proj3, dyp3, o3, dproj3):
    b, s, _ = proj3.shape
    nq = s // TQ
    scale = HEAD_DIM ** -0.5

    def kern(x_ref, dyp_ref, o_ref, _, d_ref, qs, ks, vs, dos, dk_acc, dv_acc):
        _split_heads(qs, x_ref[0, :, Q_LANES], scale)
        _split_heads(ks, x_ref[0, :, K_LANES])
        _split_heads(vs, x_ref[0, :, V_LANES])
        dq_ref, dk_ref, dv_ref = (d_ref.at[:, :, lanes] for lanes in (Q_LANES, K_LANES, V_LANES))
        za = x_ref[0, :, ZA_LANES]
        sg = _sigmoid(za)
        dyp = dyp_ref[0]
        _split_heads(dos, dyp * (za * sg))
        d_ref[0, :, ZA_LANES] = (dyp * o_ref[0] * (sg * (1.0 + za * (1.0 - sg)))).astype(BF16)
        dk_acc[...] = jnp.zeros_like(dk_acc)
        dv_acc[...] = jnp.zeros_like(dv_acc)
        row = lax.broadcasted_iota(jnp.int32, (TQ, TK), 0)
        col = lax.broadcasted_iota(jnp.int32, (TQ, TK), 1)
        tri_gt = _tri(lambda j, sk: j > sk)
        tri_ge = _tri(lambda j, sk: j >= sk)

        def q_block(i, _):
            top = isinstance(i, int)
            r0 = i * TQ if top else pl.multiple_of(i * TQ, TQ)
            qh = [qs[h, pl.ds(r0, TQ), :] for h in range(2)]
            doh = [dos[h, pl.ds(r0, TQ), :] for h in range(2)]
            totals = [jnp.sum(doh[h].astype(F32) * o_ref[0, pl.ds(r0, TQ), lanes], axis=1, keepdims=True)
                      for h, lanes in enumerate(HEAD_LANES)]

            def k_blocks(blocks, carry):
                nb = range(len(blocks))
                kh = [[ks[h, pl.ds(c0, TK), :] for h in range(2)] for c0, _ in blocks]
                vh = [[vs[h, pl.ds(c0, TK), :] for h in range(2)] for c0, _ in blocks]
                z = [[_dot(qh[h], kh[bl][h], NT) for h in range(2)] for bl in nb]
                da = [[_dot(doh[h], vh[bl][h], NT) for h in range(2)] for bl in nb]
                logs = [[_sb_logs(z[bl][h], blocks[bl][1]) for h in range(2)] for bl in nb]
                later = [[_tri_dot(logs[bl][h][1], tri_gt) for h in range(2)] for bl in nb]
                ab, g, suffix = ([[None, None] for _ in nb] for _ in range(3))
                for h in range(2):
                    cr = carry[h][0]
                    for bl in nb:
                        a = _sb_weights(logs[bl][h][0], later[bl][h], cr, blocks[bl][1])
                        ab[bl][h] = a.astype(BF16)
                        g[bl][h] = da[bl][h] * ab[bl][h].astype(F32)
                        suffix[bl][h] = _tri_dot(g[bl][h], tri_ge)
                        cr = cr + (later[bl][h][:, 0:1] + logs[bl][h][1][:, 0:1])
                out = []
                for h in range(2):
                    _, carry_g, dq = carry[h]
                    cr = carry[h][0]
                    for bl in nb:
                        c0, mask = blocks[bl]
                        lb, lom = logs[bl][h]
                        dz = g[bl][h] - (g[bl][h] + (totals[h] - carry_g) - suffix[bl][h]) * jnp.exp(lb)
                        if mask is not None:
                            dz = jnp.where(mask, dz, 0.0)
                        dzb = dz.astype(BF16)
                        dk_acc[h, pl.ds(c0, TK), :] += _dot(dzb, qh[h], TN)
                        dv_acc[h, pl.ds(c0, TK), :] += _dot(ab[bl][h], doh[h], TN)
                        dq = dq + _dot(dzb, kh[bl][h], NN)
                        carry_g = carry_g + suffix[bl][h][:, 0:1]
                        cr = cr + (later[bl][h][:, 0:1] + lom[:, 0:1])
                    out.append((cr, carry_g, dq))
                return tuple(out)

            def k_block(c0, carry, mask):
                kh = [ks[h, pl.ds(c0, TK), :] for h in range(2)]
                vh = [vs[h, pl.ds(c0, TK), :] for h in range(2)]
                z = [_dot(qh[h], kh[h], NT) for h in range(2)]
                da = [_dot(doh[h], vh[h], NT) for h in range(2)]
                logs, later = [], []
                for h in range(2):
                    logs.append(_sb_logs(z[h], mask))
                    later.append(_tri_dot(logs[h][1], tri_gt))
                ab, g, suffix = [], [], []
                for h in range(2):
                    a = _sb_weights(logs[h][0], later[h], carry[h][0], mask)
                    ab.append(a.astype(BF16))
                    g.append(da[h] * ab[h].astype(F32))
                    suffix.append(_tri_dot(g[h], tri_ge))
                out = []
                for h in range(2):
                    carry_r, carry_g, dq = carry[h]
                    lb, lom = logs[h]
                    dz = g[h] - (g[h] + (totals[h] - carry_g) - suffix[h]) * jnp.exp(lb)
                    if mask is not None:
                        dz = jnp.where(mask, dz, 0.0)
                    dzb = dz.astype(BF16)
                    dk_acc[h, pl.ds(c0, TK), :] += _dot(dzb, qh[h], TN)
                    dv_acc[h, pl.ds(c0, TK), :] += _dot(ab[h], doh[h], TN)
                    out.append((carry_r + (later[h][:, 0:1] + lom[:, 0:1]), carry_g + suffix[h][:, 0:1],
                                dq + _dot(dzb, kh[h], NN)))
                return tuple(out)

            zero = jnp.zeros((TQ, 1), F32)
            start = (zero, zero, jnp.zeros((TQ, HEAD_DIM), F32))
            diag = (r0, col < row)
            if top:
                carry = k_block(r0, (start, start), col < row)
            else:
                carry = k_blocks([diag, (pl.multiple_of(r0 - TK, TK), None)], (start, start))
                carry = _while_alive(i - 1, carry, lambda jj, cr: k_block(pl.multiple_of((i - 2 - jj) * TK, TK), cr, None))
            for (_, _, dq), lanes in zip(carry, HEAD_LANES):
                dq_ref[0, pl.ds(r0, TQ), lanes] = (dq * scale).astype(BF16)
            return 0

        q_block(0, 0)
        lax.fori_loop(1, nq, q_block, 0)

        for h, lanes in enumerate(HEAD_LANES):
            dk_ref[0, :, lanes] = dk_acc[h].astype(BF16)
            dv_ref[0, :, lanes] = dv_acc[h].astype(BF16)

    plain = pl.BlockSpec((1, s, LANES), lambda bi, hp: (bi, 0, hp))
    pair = pl.BlockSpec((1, s, HP_WIDTH), lambda bi, hp: (bi, 0, hp))
    return pl.pallas_call(
        kern, name="attn_bwd",
        out_shape=jax.ShapeDtypeStruct(dproj3.shape, dproj3.dtype),
        grid=(b, SB_HEADS // 2),
        in_specs=[pair, plain, plain, ANY],
        out_specs=pair,
        input_output_aliases={3: 0},
        scratch_shapes=[pltpu.VMEM((2, s, HEAD_DIM), BF16)] * 4 + [pltpu.VMEM((2, s, HEAD_DIM), F32)] * 2,
        compiler_params=_cparams("parallel", "parallel"),
    )(proj3, dyp3, o3, dproj3)


CONV_COLS = 256
HALO = 8


def _conv_pre(xp, w_ref, b_ref, r0):
    pre = b_ref[...] + w_ref[CONV_K - 1:CONV_K, :] * xp[pl.ds(HALO + r0, CHUNK), :]
    for kk in range(1, CONV_K):
        pre = pre + w_ref[CONV_K - 1 - kk:CONV_K - kk, :] * xp[pl.ds(HALO + r0 - kk, CHUNK), :]
    return pre


def _conv_fwd(proj3, conv_w, conv_b):
    b, s, _ = proj3.shape
    nc = s // CHUNK

    def kern(x_ref, w_ref, b_ref, o_ref, xp):
        xp[0:HALO, :] = jnp.zeros((HALO, CONV_COLS), F32)
        xp[HALO:, :] = x_ref[0]
        for ci in range(nc):
            pre = _conv_pre(xp, w_ref, b_ref, ci * CHUNK)
            o_ref[0, ci * CHUNK:(ci + 1) * CHUNK, :] = pre * _sigmoid(pre)

    return pl.pallas_call(
        kern, name="conv_fwd",
        out_shape=jax.ShapeDtypeStruct((b, s, CONV_DIM), F32),
        grid=(CONV_DIM // CONV_COLS, b),
        in_specs=[pl.BlockSpec((1, s, CONV_COLS), lambda j, bi: (bi, 0, XBC0 // CONV_COLS + j)),
                  pl.BlockSpec((CONV_K, CONV_COLS), lambda j, bi: (0, j)),
                  pl.BlockSpec((1, CONV_COLS), lambda j, bi: (0, j))],
        out_specs=pl.BlockSpec((1, s, CONV_COLS), lambda j, bi: (bi, 0, j)),
        scratch_shapes=[pltpu.VMEM((s + HALO, CONV_COLS), F32)],
        compiler_params=_cparams("parallel", "parallel"),
    )(proj3, conv_w, conv_b)


def _conv_bwd(dact, proj3, conv_w, conv_b, col0, name, dproj3):
    b, s, width = dact.shape
    nc = s // CHUNK
    j0 = col0 // CONV_COLS

    def kern(da_ref, x_ref, w_ref, b_ref, _, dx_ref, dw_ref, db_ref, xp, dp):
        @pl.when(pl.program_id(1) == 0)
        def _():
            dw_ref[...] = jnp.zeros_like(dw_ref)
            db_ref[...] = jnp.zeros_like(db_ref)

        xp[0:HALO, :] = jnp.zeros((HALO, CONV_COLS), F32)
        xp[HALO:, :] = x_ref[0]
        dp[s:, :] = jnp.zeros((HALO, CONV_COLS), F32)
        for ci in range(nc):
            r0 = ci * CHUNK
            pre = _conv_pre(xp, w_ref, b_ref, r0)
            sg = _sigmoid(pre)
            dpre = da_ref[0, r0:r0 + CHUNK, :] * (sg * (1.0 + pre * (1.0 - sg)))
            dp[r0:r0 + CHUNK, :] = dpre
            db_ref[...] += jnp.sum(dpre, axis=0, keepdims=True)
            for kk in range(CONV_K):
                tap = CONV_K - 1 - kk
                dw_ref[tap:tap + 1, :] += jnp.sum(dpre * xp[pl.ds(HALO + r0 - kk, CHUNK), :], axis=0, keepdims=True)
        for ci in range(nc):
            r0 = ci * CHUNK
            dx = w_ref[CONV_K - 1:CONV_K, :] * dp[pl.ds(r0, CHUNK), :]
            for kk in range(1, CONV_K):
                dx = dx + w_ref[CONV_K - 1 - kk:CONV_K - kk, :] * dp[pl.ds(r0 + kk, CHUNK), :]
            dx_ref[0, r0:r0 + CHUNK, :] = dx.astype(BF16)

    return pl.pallas_call(
        kern, name=name,
        out_shape=(jax.ShapeDtypeStruct(dproj3.shape, dproj3.dtype), jax.ShapeDtypeStruct((CONV_K, width), F32),
                   jax.ShapeDtypeStruct((1, width), F32)),
        grid=(width // CONV_COLS, b),
        in_specs=[pl.BlockSpec((1, s, CONV_COLS), lambda j, bi: (bi, 0, j)),
                  pl.BlockSpec((1, s, CONV_COLS), lambda j, bi: (bi, 0, XBC0 // CONV_COLS + j0 + j)),
                  pl.BlockSpec((CONV_K, CONV_COLS), lambda j, bi: (0, j0 + j)),
                  pl.BlockSpec((1, CONV_COLS), lambda j, bi: (0, j0 + j)), ANY],
        out_specs=(pl.BlockSpec((1, s, CONV_COLS), lambda j, bi: (bi, 0, XBC0 // CONV_COLS + j0 + j)),
                   pl.BlockSpec((CONV_K, CONV_COLS), lambda j, bi: (0, j)),
                   pl.BlockSpec((1, CONV_COLS), lambda j, bi: (0, j))),
        input_output_aliases={4: 0},
        scratch_shapes=[pltpu.VMEM((s + HALO, CONV_COLS), F32)] * 2,
        compiler_params=_cparams("parallel", "arbitrary"),
    )(dact, proj3, conv_w, conv_b, dproj3)


SSD_CHUNKS_PER_STEP = 8


def _sel_dot(v, sel, left=False):
    hi = v.astype(BF16)
    rest = v - hi.astype(F32)
    mid = rest.astype(BF16)
    lo = (rest - mid.astype(F32)).astype(BF16)
    if left:
        return _dot(sel, hi, NN) + _dot(sel, mid, NN) + _dot(sel, lo, NN)
    return _dot(hi, sel, NN) + _dot(mid, sel, NN) + _dot(lo, sel, NN)


def _ssd_common(dtr, dtb, alog):
    lane = lax.broadcasted_iota(jnp.int32, (CHUNK, LANES), 1)
    row = lax.broadcasted_iota(jnp.int32, (CHUNK, LANES), 0)
    head_lane = lane < HEADS_PER_GROUP
    pre = dtr + dtb
    dt = jnp.where(head_lane, jnp.maximum(pre, 0.0) + jnp.log(1.0 + jnp.exp(-jnp.abs(pre))), 0.0)
    a = jnp.where(head_lane[0:1], -jnp.exp(alog), 0.0)
    tril = (row >= lane).astype(BF16)
    acs = _sel_dot(dt * a, tril, left=True)
    acs_t = acs.T
    er = lax.broadcasted_iota(jnp.int32, (LANES, GROUP_WIDTH), 0)
    ec = lax.broadcasted_iota(jnp.int32, (LANES, GROUP_WIDTH), 1)
    expand = ((ec // HEAD_DIM) == er).astype(BF16)
    tr = lax.broadcasted_iota(jnp.int32, (GROUP_WIDTH, LANES), 0)
    tc = lax.broadcasted_iota(jnp.int32, (GROUP_WIDTH, LANES), 1)
    reduce = ((tr // HEAD_DIM) == tc).astype(BF16)
    dt_x = _sel_dot(dt, expand)
    acs_x = _sel_dot(acs, expand)
    end_x = acs_x[CHUNK - 1:CHUNK, :]
    causal = row >= lane
    return dict(dt=dt, a=a, pre=pre, head_lane=head_lane, acs=acs, acs_t=acs_t, expand=expand, reduce=reduce,
                dt_x=dt_x, acs_x=acs_x, end_x=end_x, causal=causal, row=row, lane=lane)


def _ssd_decay(cm, h):
    seg = cm["acs"][:, h:h + 1] - cm["acs_t"][h:h + 1, :]
    return jnp.where(cm["causal"], jnp.exp(jnp.minimum(seg, 0.0)), 0.0)


def _ssd_fwd(xact, proj3, dtr_g, dtb_g, alog_g, dskip_x, snw):
    b, s, _ = xact.shape
    nc = s // CHUNK
    g4 = SSD_GROUPS
    cps = min(nc, max(1, SSD_CHUNKS_PER_STEP // b))
    rows_per_step = cps * CHUNK

    def kern(xs_ref, bm_ref, cm_ref, zs_ref, dtr_ref, dtb_ref, alog_ref, dsk_ref, snw_ref,
             y_ref, yn_ref, hst_ref, h_sc):
        @pl.when(pl.program_id(2) == 0)
        def _():
            h_sc[...] = jnp.zeros_like(h_sc)

        def chunk(ci, _):
            for bi in range(b):
                one = pl.ds(bi, 1)
                chunk_of(ci, *[r.at[one] for r in (xs_ref, bm_ref, cm_ref, zs_ref, dtr_ref, y_ref, yn_ref, hst_ref)],
                         h_sc.at[bi])
            return 0

        def chunk_of(ci, xs_ref, bm_ref, cm_ref, zs_ref, dtr_ref, y_ref, yn_ref, hst_ref, h_sc):
            rows = pl.ds(pl.multiple_of(ci * CHUNK, CHUNK), CHUNK)
            cm = _ssd_common(dtr_ref[0, 0, rows, :], dtb_ref[0], alog_ref[0])
            x = xs_ref[0, rows, :]
            bmb = bm_ref[0, rows, :].astype(BF16)
            cmb = cm_ref[0, rows, :].astype(BF16)
            h_in = h_sc[...]
            hst_ref[0, ci, 0] = h_in
            xdt = x * cm["dt_x"]
            xdtb = xdt.astype(BF16)
            cb = _dot(cmb, bmb, NT)
            y_off = _dot(cmb, h_in.astype(BF16), NN) * jnp.exp(cm["acs_x"])
            for h in range(HEADS_PER_GROUP):
                lanes = slice(h * HEAD_DIM, (h + 1) * HEAD_DIM)
                m = (cb * _ssd_decay(cm, h)).astype(BF16)
                y_ref[0, rows, lanes] = _dot(m, xdtb[:, lanes], NN)
            y = y_ref[0, rows, :] + y_off + x * dsk_ref[...]
            y_ref[0, rows, :] = y
            w = (xdt * jnp.exp(cm["end_x"] - cm["acs_x"])).astype(BF16)
            h_sc[...] = h_in * jnp.exp(cm["end_x"]) + _dot(bmb, w, TN)
            zs = zs_ref[0, rows, :]
            y2 = y * (zs * _sigmoid(zs))
            yn_ref[0, rows, :] = (y2 * lax.rsqrt(jnp.mean(y2 * y2, axis=-1, keepdims=True) + EPS) * snw_ref[...]).astype(BF16)

        lax.fori_loop(0, cps, chunk, 0, unroll=2)

    gw = GROUP_WIDTH
    small = pl.BlockSpec((1, 1, LANES), lambda gi, bi, ci: (gi, 0, 0))
    xblk = pl.BlockSpec((b, rows_per_step, gw), lambda gi, bi, ci: (bi, ci, gi))
    return pl.pallas_call(
        kern, name="ssd_fwd",
        out_shape=(jax.ShapeDtypeStruct((b, s, SSD_WIDTH), F32), jax.ShapeDtypeStruct((b, s, SSD_WIDTH), BF16),
                   jax.ShapeDtypeStruct((b, nc, g4, SSD_STATE, gw), F32)),
        grid=(g4, 1, nc // cps),
        in_specs=[xblk,
                  pl.BlockSpec((b, rows_per_step, LANES), lambda gi, bi, ci: (bi, ci, SSD_WIDTH // LANES + gi)),
                  pl.BlockSpec((b, rows_per_step, LANES), lambda gi, bi, ci: (bi, ci, SSD_WIDTH // LANES + g4 + gi)),
                  pl.BlockSpec((b, rows_per_step, gw), lambda gi, bi, ci: (bi, ci, ZS0 // gw + gi)),
                  pl.BlockSpec((b, 1, rows_per_step, LANES), lambda gi, bi, ci: (bi, gi, ci, 0)),
                  small, small,
                  pl.BlockSpec((1, gw), lambda gi, bi, ci: (0, gi)),
                  pl.BlockSpec((1, gw), lambda gi, bi, ci: (0, gi))],
        out_specs=(xblk, xblk, pl.BlockSpec((b, cps, 1, SSD_STATE, gw), lambda gi, bi, ci: (bi, ci, gi, 0, 0))),
        scratch_shapes=[pltpu.VMEM((b, SSD_STATE, gw), F32)],
        compiler_params=_cparams("parallel", "parallel", "arbitrary"),
    )(xact, xact, xact, proj3, dtr_g, dtb_g, alog_g, dskip_x, snw)


def _ssd_bwd(dyn3, y3, xact, proj3, hst, dtr_g, dtb_g, alog_g, dskip_x, snw, dproj3):
    b, s, _ = xact.shape
    nc = s // CHUNK
    g4 = SSD_GROUPS
    gw = GROUP_WIDTH

    cps = min(nc, max(1, SSD_CHUNKS_PER_STEP // b))
    rows_per_step = cps * CHUNK

    def one_chunk(dyn_ref, y_ref, xs_ref, bm_ref, cm_ref, zs_ref, hst_ref, dtr_ref, dtb_ref, alog_ref, dsk_ref, snw_ref,
                  dxs_ref, dbm_ref, dcm_ref, dzs_ref, ddtr_ref, dsnw_ref, dalog_ref, ddtb_ref, ddsk_ref, dh_sc):
        cm = _ssd_common(dtr_ref[0, 0], dtb_ref[0], alog_ref[0])
        row, lane = cm["row"], cm["lane"]
        y = y_ref[0]
        zs = zs_ref[0]
        sg = _sigmoid(zs)
        silu = zs * sg
        y2 = y * silu
        rstd = lax.rsqrt(jnp.mean(y2 * y2, axis=-1, keepdims=True) + EPS)
        y2h = y2 * rstd
        dyn = dyn_ref[0]
        dsnw_ref[0] += jnp.sum(dyn * y2h, axis=0, keepdims=True)
        gwv = dyn * snw_ref[...]
        dy2 = rstd * (gwv - y2h * jnp.mean(gwv * y2h, axis=-1, keepdims=True))
        dzs_ref[0] = (dy2 * y * (sg * (1.0 + zs * (1.0 - sg)))).astype(BF16)
        dy = dy2 * silu
        dyb = dy.astype(BF16)

        x = xs_ref[0]
        bmb = bm_ref[0].astype(BF16)
        cmb = cm_ref[0].astype(BF16)
        h_in = hst_ref[0, 0, 0]
        h_inb = h_in.astype(BF16)
        d_hn = dh_sc[...]
        d_hnb = d_hn.astype(BF16)
        xdt = x * cm["dt_x"]
        xdtb = xdt.astype(BF16)
        eacs = jnp.exp(cm["acs_x"])
        dte = jnp.exp(cm["end_x"] - cm["acs_x"])
        wb = (xdt * dte).astype(BF16)

        dsk_lanes = jnp.broadcast_to(jnp.sum(dy * x, axis=0, keepdims=True), (8, gw))
        ddsk_ref[0] += _sel_dot(dsk_lanes, cm["reduce"])[0:1, :]
        dyo = dy * eacs
        dyob = dyo.astype(BF16)
        dacs_x = dyo * _dot(cmb, h_inb, NN)
        dcm = _dot(dyob, h_inb, NT)
        dh_in = _dot(cmb, dyob, TN)
        dw = _dot(bmb, d_hnb, NN)
        dbm = _dot(wb, d_hnb, NT)
        dxdt = dw * dte
        e_l = dw * xdt * dte
        dacs_x = dacs_x - e_l
        dend_x = jnp.sum(e_l, axis=0, keepdims=True)
        chunk_decay = jnp.exp(cm["end_x"])
        dh_sc[...] = d_hn * chunk_decay + dh_in
        dend_x = dend_x + jnp.sum(d_hn * h_in, axis=0, keepdims=True) * chunk_decay
        last_row = lax.broadcasted_iota(jnp.int32, (CHUNK, gw), 0) == CHUNK - 1
        dacs_x = dacs_x + jnp.where(last_row, dend_x, 0.0)

        cb = _dot(cmb, bmb, NT)
        dcb = jnp.zeros((CHUNK, CHUNK), F32)
        dacs = jnp.zeros((CHUNK, LANES), F32)
        dacs_t = jnp.zeros((LANES, CHUNK), F32)
        for h in range(HEADS_PER_GROUP):
            lanes = slice(h * HEAD_DIM, (h + 1) * HEAD_DIM)
            decay = _ssd_decay(cm, h)
            m = cb * decay
            dm = _dot(dyb[:, lanes], xdtb[:, lanes], NT)
            dxs_ref[0, :, lanes] = _dot(m.astype(BF16), dyb[:, lanes], TN)
            dcb_h = dm * decay
            dcb = dcb + dcb_h
            n = dcb_h * cb
            dacs = dacs + jnp.where(lane == h, jnp.sum(n, axis=1, keepdims=True), 0.0)
            dacs_t = dacs_t + jnp.where(row == h, jnp.sum(n, axis=0, keepdims=True), 0.0)
        dcbb = dcb.astype(BF16)
        dcm_ref[0] = dcm + _dot(dcbb, bmb, NN)
        dbm_ref[0] = dbm + _dot(dcbb, cmb, TN)
        dxdt = dxdt + dxs_ref[0]
        dxs_ref[0] = dy * dsk_ref[...] + dxdt * cm["dt_x"]

        dacs = dacs - dacs_t.T + _sel_dot(dacs_x, cm["reduce"])
        ddt = _sel_dot(dxdt * x, cm["reduce"])
        triu = (row <= lane).astype(BF16)
        rc = _sel_dot(dacs, triu, left=True)
        ddt = ddt + cm["a"] * rc
        dalog_ref[0] += jnp.sum(cm["dt"] * rc, axis=0, keepdims=True) * cm["a"]
        ddtr = jnp.where(cm["head_lane"], ddt * _sigmoid(cm["pre"]), 0.0)
        ddtr_ref[0, 0] = ddtr
        ddtb_ref[0] += jnp.sum(ddtr, axis=0, keepdims=True)

    def kern(dyn_ref, y_ref, xs_ref, bm_ref, cm_ref, zs_ref, hst_ref, dtr_ref, dtb_ref, alog_ref, dsk_ref, snw_ref, _,
             dxs_ref, dbm_ref, dcm_ref, dzs_ref, ddtr_ref, dsnw_ref, dalog_ref, ddtb_ref, ddsk_ref, dh_sc):
        first = jnp.logical_and(pl.program_id(1) == 0, pl.program_id(2) == 0)

        @pl.when(first)
        def _():
            dsnw_ref[...] = jnp.zeros_like(dsnw_ref)
            dalog_ref[...] = jnp.zeros_like(dalog_ref)
            ddtb_ref[...] = jnp.zeros_like(ddtb_ref)
            ddsk_ref[...] = jnp.zeros_like(ddsk_ref)

        @pl.when(pl.program_id(2) == 0)
        def _():
            dh_sc[...] = jnp.zeros_like(dh_sc)

        def chunk(k, _):
            ci = cps - 1 - k
            rows = pl.ds(pl.multiple_of(ci * CHUNK, CHUNK), CHUNK)
            for bi in range(b):
                one = pl.ds(bi, 1)
                by_rows = [r.at[one, rows, :] for r in (dyn_ref, y_ref, xs_ref, bm_ref, cm_ref, zs_ref)]
                one_chunk(*by_rows, hst_ref.at[one, pl.ds(ci, 1)], dtr_ref.at[one, :, rows, :], dtb_ref, alog_ref, dsk_ref, snw_ref,
                          *[r.at[one, rows, :] for r in (dxs_ref, dbm_ref, dcm_ref, dzs_ref)], ddtr_ref.at[one, :, rows, :],
                          dsnw_ref, dalog_ref, ddtb_ref, ddsk_ref, dh_sc.at[bi])
            return 0

        lax.fori_loop(0, cps, chunk, 0, unroll=2)

    def rev(ci):
        return nc // cps - 1 - ci

    small = pl.BlockSpec((1, 1, LANES), lambda gi, bi, ci: (gi, 0, 0))
    xblk = pl.BlockSpec((b, rows_per_step, gw), lambda gi, bi, ci: (bi, rev(ci), gi))
    nblk = pl.BlockSpec((b, rows_per_step, LANES), lambda gi, bi, ci: (bi, rev(ci), gi))
    gvec = pl.BlockSpec((1, gw), lambda gi, bi, ci: (0, gi))
    gacc = pl.BlockSpec((1, 1, gw), lambda gi, bi, ci: (gi, 0, 0))
    return pl.pallas_call(
        kern, name="ssd_bwd",
        out_shape=(jax.ShapeDtypeStruct((b, s, SSD_WIDTH), F32),
                   jax.ShapeDtypeStruct((b, s, g4 * SSD_STATE), F32),
                   jax.ShapeDtypeStruct((b, s, g4 * SSD_STATE), F32),
                   jax.ShapeDtypeStruct(dproj3.shape, dproj3.dtype),
                   jax.ShapeDtypeStruct((b, g4, s, LANES), F32),
                   jax.ShapeDtypeStruct((g4, 1, gw), F32),
                   jax.ShapeDtypeStruct((g4, 1, LANES), F32),
                   jax.ShapeDtypeStruct((g4, 1, LANES), F32),
                   jax.ShapeDtypeStruct((g4, 1, LANES), F32)),
        grid=(g4, 1, nc // cps),
        in_specs=[xblk, xblk, xblk,
                  pl.BlockSpec((b, rows_per_step, LANES), lambda gi, bi, ci: (bi, rev(ci), SSD_WIDTH // LANES + gi)),
                  pl.BlockSpec((b, rows_per_step, LANES), lambda gi, bi, ci: (bi, rev(ci), SSD_WIDTH // LANES + g4 + gi)),
                  pl.BlockSpec((b, rows_per_step, gw), lambda gi, bi, ci: (bi, rev(ci), ZS0 // gw + gi)),
                  pl.BlockSpec((b, cps, 1, SSD_STATE, gw), lambda gi, bi, ci: (bi, rev(ci), gi, 0, 0)),
                  pl.BlockSpec((b, 1, rows_per_step, LANES), lambda gi, bi, ci: (bi, gi, rev(ci), 0)),
                  small, small, gvec, gvec, ANY],
        out_specs=(xblk, nblk, nblk,
                   pl.BlockSpec((b, rows_per_step, gw), lambda gi, bi, ci: (bi, rev(ci), ZS0 // gw + gi)),
                   pl.BlockSpec((b, 1, rows_per_step, LANES), lambda gi, bi, ci: (bi, gi, rev(ci), 0)),
                   gacc, small, small, small),
        input_output_aliases={12: 3},
        scratch_shapes=[pltpu.VMEM((b, SSD_STATE, gw), F32)],
        compiler_params=_cparams("parallel", "arbitrary", "arbitrary"),
    )(dyn3, y3, xact, xact, xact, proj3, hst, dtr_g, dtb_g, alog_g, dskip_x, snw, dproj3)


def _adamw(w, g, m, v, name):
    r, c = w.shape
    tr = 128 if r % 128 == 0 else r
    tc = LANES if (tr == r and r > 128 and c % LANES == 0) else c

    def kern(w_ref, g_ref, m_ref, v_ref, d_ref, nm_ref, nv_ref):
        gv = g_ref[...]
        nm = ADAM_B1 * m_ref[...] + (1.0 - ADAM_B1) * gv
        nv = ADAM_B2 * v_ref[...] + (1.0 - ADAM_B2) * (gv * gv)
        m_hat = nm / (1.0 - ADAM_B1 ** ADAM_STEP)
        v_hat = nv / (1.0 - ADAM_B2 ** ADAM_STEP)
        d_ref[...] = -ADAM_LR * (m_hat / (jnp.sqrt(v_hat) + ADAM_EPS) + ADAM_WD * w_ref[...])
        nm_ref[...] = nm
        nv_ref[...] = nv

    blk = pl.BlockSpec((tr, tc), lambda i, j: (i, j))
    out = jax.ShapeDtypeStruct((r, c), F32)
    return pl.pallas_call(
        kern, name=name, out_shape=(out, out, out), grid=(r // tr, c // tc),
        in_specs=[blk] * 4, out_specs=(blk, blk, blk),
        compiler_params=_cparams("parallel", "parallel"),
    )(w, g, m, v)


ANY = pl.BlockSpec(memory_space=pl.ANY)


def _position():
    return lax.axis_index("x"), lax.axis_index("y"), lax.axis_index("c")


def _other_chips(x, y):
    return [(1 - x, y), (x, 1 - y), (1 - x, 1 - y)]


def _dma_sems(n):
    return [pltpu.SemaphoreType.DMA((n,)), pltpu.SemaphoreType.DMA((n,))]


class _Exchange:
    def __init__(self, inputs, out_shapes, sems, start, finish):
        self.inputs, self.out_shapes, self.sems, self.start, self.finish = inputs, out_shapes, sems, start, finish


def _gather_exchange(shards):
    n = len(shards)

    def copies(p_refs, out_refs, sems):
        send_sems, recv_sems = sems
        x, y, c = _position()
        me = 2 * x + y
        chips = _other_chips(x, y)

        def slab(a, chip, hf):
            half = shards[a].shape[1] // 2
            return out_refs[a].at[chip, :, pl.ds(hf * half, half)]

        def my_half(a):
            half = shards[a].shape[1] // 2
            return p_refs[a].at[:, pl.ds(c * half, half)]

        def over_ici(a, j, chip_from):
            px, py = chips[j]
            return pltpu.make_async_remote_copy(
                src_ref=my_half(a), dst_ref=slab(a, chip_from, c),
                send_sem=send_sems.at[3 * a + j], recv_sem=recv_sems.at[3 * a + j],
                device_id=(px, py, c), device_id_type=MESH)

        def to_sibling(a, j, hf):
            px, py = chips[j]
            return pltpu.make_async_remote_copy(
                src_ref=slab(a, 2 * px + py, hf), dst_ref=slab(a, 2 * px + py, hf),
                send_sem=send_sems.at[3 * (n + a) + j], recv_sem=recv_sems.at[3 * (n + a) + j],
                device_id=(x, y, 1 - c), device_id_type=MESH)

        own = [pltpu.make_async_remote_copy(
            src_ref=p_refs[a], dst_ref=out_refs[a].at[me], send_sem=send_sems.at[6 * n + a], recv_sem=recv_sems.at[6 * n + a],
            device_id=(x, y, 1 - c), device_id_type=MESH) for a in range(n)]
        first = [over_ici(a, j, me) for a in range(n) for j in range(3)]
        return chips, c, over_ici, to_sibling, first, own

    def start(p_refs, out_refs, sems):
        _, _, _, _, first, own = copies(p_refs, out_refs, sems)
        for cp in first + own:
            cp.start()

    def finish(p_refs, out_refs, sems):
        chips, c, over_ici, to_sibling, first, own = copies(p_refs, out_refs, sems)
        passed = []
        for a in range(n):
            for j, (px, py) in enumerate(chips):
                over_ici(a, j, 2 * px + py).wait_recv()
                passed.append(to_sibling(a, j, c))
                passed[-1].start()
        for a in range(n):
            for j in range(3):
                to_sibling(a, j, 1 - c).wait_recv()
        for cp in first + passed:
            cp.wait_send()
        for cp in own:
            cp.wait()

    return _Exchange(list(shards), [jax.ShapeDtypeStruct((N_CHIPS, *v.shape), v.dtype) for v in shards],
                     _dma_sems(7 * n), start, finish)


def _swap_halves(parts, name):
    n = len(parts)

    def body(*refs):
        v_refs, out_refs = refs[:n], refs[n:2 * n]
        send_sems, recv_sems = refs[2 * n:]
        x, y, c = _position()
        copies = []
        for a in range(n):
            half = parts[a].shape[2] // 2
            copies.append(pltpu.make_async_remote_copy(
                src_ref=v_refs[a].at[:, :, pl.ds((1 - c) * half, half)], dst_ref=out_refs[a],
                send_sem=send_sems.at[a], recv_sem=recv_sems.at[a], device_id=(x, y, 1 - c), device_id_type=MESH))
        for cp in copies:
            cp.start()
        for cp in copies:
            cp.wait()

    return pl.pallas_call(
        body, name=name,
        out_shape=[jax.ShapeDtypeStruct((v.shape[0], v.shape[1], v.shape[2] // 2), v.dtype) for v in parts],
        in_specs=[ANY] * n, out_specs=[ANY] * n,
        scratch_shapes=_dma_sems(n),
    )(*parts)


def _all_to_all_exchange(parts):
    n = len(parts)

    def sends(p_refs, out_refs, sems):
        send_sems, recv_sems = sems
        x, y, c = _position()
        return [pltpu.make_async_remote_copy(
            src_ref=p_refs[a].at[2 * px + py], dst_ref=out_refs[a].at[j],
            send_sem=send_sems.at[3 * a + j], recv_sem=recv_sems.at[3 * a + j],
            device_id=(px, py, c), device_id_type=MESH) for a in range(n) for j, (px, py) in enumerate(_other_chips(x, y))]

    def start(p_refs, out_refs, sems):
        for cp in sends(p_refs, out_refs, sems):
            cp.start()

    def finish(p_refs, out_refs, sems):
        for cp in sends(p_refs, out_refs, sems):
            cp.wait()

    return _Exchange(list(parts), [jax.ShapeDtypeStruct((N_CHIPS - 1, *v.shape[1:]), v.dtype) for v in parts],
                     _dma_sems(3 * n), start, finish)


def _join_halves(wholes):
    n = len(wholes)

    def body(*refs):
        out_refs = refs[n:2 * n]
        send_sems, recv_sems = refs[2 * n:]
        x, y, c = _position()
        copies = []
        for a in range(n):
            half = wholes[a].shape[1] // 2
            mine = out_refs[a].at[:, pl.ds(c * half, half)]
            copies.append(pltpu.make_async_remote_copy(
                src_ref=mine, dst_ref=mine, send_sem=send_sems.at[a], recv_sem=recv_sems.at[a],
                device_id=(x, y, 1 - c), device_id_type=MESH))
        for cp in copies:
            cp.start()
        for cp in copies:
            cp.wait()

    return pl.pallas_call(
        body, name="grad_join_halves",
        out_shape=[jax.ShapeDtypeStruct(v.shape, v.dtype) for v in wholes],
        in_specs=[ANY] * n, out_specs=[ANY] * n,
        input_output_aliases={a: a for a in range(n)},
        scratch_shapes=_dma_sems(n),
    )(*wholes)


STRIP = 256


def _add_halves(g, sw, place, name):
    n, rows, cols = g.shape
    nb = cols // 2 // STRIP

    def kern(p_ref, g_ref, s_ref, o_ref):
        o_ref[...] = (g_ref[...] + s_ref[...]).astype(BF16)

    blk = pl.BlockSpec((1, rows, STRIP), lambda j, i, p_ref: (j, 0, i))
    return pl.pallas_call(
        kern, name=name,
        out_shape=jax.ShapeDtypeStruct((n, rows, cols // 2), BF16),
        grid_spec=pltpu.PrefetchScalarGridSpec(
            num_scalar_prefetch=1, grid=(n, nb),
            in_specs=[pl.BlockSpec((1, rows, STRIP), lambda j, i, p_ref: (j, 0, p_ref[0] * nb + i)), blk],
            out_specs=blk),
        compiler_params=_cparams("parallel", "parallel"),
    )(place, g, sw)


def _sum_chips(own, rx, place, name):
    _, rows, half = rx.shape
    nb = half // STRIP

    def kern(p_ref, own_ref, r_ref, o_ref):
        total = own_ref[0].astype(F32)
        for j in range(N_CHIPS - 1):
            total = total + r_ref[j].astype(F32)
        o_ref[...] = total

    return pl.pallas_call(
        kern, name=name,
        out_shape=jax.ShapeDtypeStruct((rows, 2 * half), F32),
        grid_spec=pltpu.PrefetchScalarGridSpec(
            num_scalar_prefetch=1, grid=(nb,),
            in_specs=[pl.BlockSpec((1, rows, STRIP), lambda i, p_ref: (p_ref[1], 0, i)),
                      pl.BlockSpec((N_CHIPS - 1, rows, STRIP), lambda i, p_ref: (0, 0, i))],
            out_specs=pl.BlockSpec((rows, STRIP), lambda i, p_ref: (0, p_ref[0] * nb + i))),
        compiler_params=_cparams("parallel"),
    )(place, own, rx)


def _gather_small(v, reduce, name):
    rows = v.shape[0]

    def body(v_ref, out_ref, buf, send_sems, recv_sems):
        x, y, c = _position()
        me = 4 * x + 2 * y + c
        buf[me] = v_ref[...]
        peers = [(x ^ (k >> 2), y ^ ((k >> 1) & 1), c ^ (k & 1)) for k in range(1, 8)]
        copies = [pltpu.make_async_remote_copy(
            src_ref=v_ref, dst_ref=buf.at[me],
            send_sem=send_sems.at[k], recv_sem=recv_sems.at[k],
            device_id=peer, device_id_type=MESH) for k, peer in enumerate(peers)]
        for cp in copies:
            cp.start()
        for k, (px, py, pc) in enumerate(peers):
            pltpu.make_async_remote_copy(
                src_ref=v_ref, dst_ref=buf.at[4 * px + 2 * py + pc],
                send_sem=send_sems.at[k], recv_sem=recv_sems.at[k],
                device_id=(px, py, pc), device_id_type=MESH).wait_recv()
        for cp in copies:
            cp.wait_send()
        if reduce:
            total = buf[0]
            for d in range(1, 8):
                total = total + buf[d]
            out_ref[...] = total
        else:
            out_ref[...] = buf[...]

    vm = pl.BlockSpec(memory_space=pltpu.VMEM)
    return pl.pallas_call(
        body, name=name,
        out_shape=jax.ShapeDtypeStruct((rows, LANES) if reduce else (8, rows, LANES), F32),
        in_specs=[vm], out_specs=vm,
        scratch_shapes=[pltpu.VMEM((8, rows, LANES), F32), pltpu.SemaphoreType.DMA((7,)), pltpu.SemaphoreType.DMA((7,))],
    )(v)


def _pad_rows(a, rows):
    return jnp.pad(a, ((0, rows - a.shape[0]), (0, 0)))


def _lane_pad(v):
    n = v.shape[1]
    return jnp.pad(v, ((0, 0), (0, -n % LANES)))


def _gather_all(w_in, w_attn_out, w_ssm_out, w_o, conv_w):
    d = D_MODEL
    w_proj_t = _gather_exchange([w_in[0].T.astype(BF16)])
    out_w = _gather_exchange([a[0].astype(BF16) for a in (w_attn_out, w_ssm_out, w_o)])
    conv_rows = conv_w[0].size // LANES
    conv_all = _gather_small(conv_w[0].reshape(conv_rows, LANES), False, "gather_conv_w")
    conv_w_all = conv_all[0::2].reshape(N_CHIPS, CONV_K, CONV_DIM // N_CHIPS).transpose(1, 0, 2).reshape(CONV_K, CONV_DIM)

    return w_proj_t, out_w, conv_w_all


def _local_step(x, loss_target, norm_w, w_proj_t, conv_w_all, conv_b, dt_bias, a_log, d_skip, ssm_norm_w,
                out_w, final_norm_w, grad_exchange=None):
    b, s, d = x.shape
    t = b * s
    g4, hg = SSD_GROUPS, HEADS_PER_GROUP
    dtb_g = _lane_pad(dt_bias.reshape(g4, hg)).reshape(g4, 1, LANES)
    alog_g = _lane_pad(a_log.reshape(g4, hg)).reshape(g4, 1, LANES)
    dskip_x = jnp.repeat(d_skip, HEAD_DIM, axis=1)
    fnw = final_norm_w.reshape(1, d)

    x2 = x.reshape(t, d)
    if isinstance(w_proj_t, _Exchange):
        h, w_in_t = _rms_fwd(x2, norm_w, exchange=w_proj_t)
        w_proj_t = _to_proj_layout(w_in_t.reshape(D_PROJ, d))
    else:
        h = _rms_fwd(x2, norm_w)
    big_tm = min(t, 2048)
    if isinstance(out_w, _Exchange):
        proj, *out_w = _matmul(h, w_proj_t, tb=True, tm=big_tm, tn=1280, tk=1024, name="proj", exchange=out_w)
    else:
        proj = _matmul(h, w_proj_t, tb=True, tm=big_tm, tn=1280, tk=1024, name="proj")
    w_ao, w_so, w_oo = (w.reshape(-1, d) for w in out_w)
    proj3 = proj.reshape(b, s, NP)
    o3, yp3 = _attn_fwd(proj3)
    xact = _conv_fwd(proj3, conv_w_all, conv_b)
    dtr = proj3[:, :, DT0:DT0 + g4 * hg].reshape(b, s, g4, hg).transpose(0, 2, 1, 3)
    dtr_g = jnp.pad(dtr, ((0, 0), (0, 0), (0, 0), (0, LANES - hg)))
    y3, yn3, hst = _ssd_fwd(xact, proj3, dtr_g, dtb_g, alog_g, dskip_x, ssm_norm_w)
    yp = yp3.reshape(t, D_MODEL)
    yn = yn3.reshape(t, SSD_WIDTH)
    ya = _matmul(yp, w_ao, tm=1024, tn=1024, tk=1024, name="attn_out")
    ys = _matmul(yn, w_so, tm=1024, tn=1024, tk=2048, name="ssm_out")
    merged = _merge_fwd(proj, ya, ys)
    mo = _matmul(merged, w_oo, tm=1024, tn=1024, tk=1024, name="out_proj")
    dout, doutb, loss_part, d_fnw = _final_fwd_bwd(x2, mo, loss_target.reshape(t, d), fnw)

    dmerged = _matmul(doutb, w_oo, tb=True, tm=1024, tn=1024, tk=1024, name="d_merged")
    g_wo = _matmul(merged, doutb, ta=True, tm=1024, tn=1024, tk=1024, name="g_w_o")
    dya, dys, dproj = _merge_bwd(dmerged, proj, ya, ys)
    dyp = _matmul(dya, w_ao, tb=True, tm=1024, tn=1024, tk=1024, name="d_attn_pre")
    g_wao = _matmul(yp, dya, ta=True, tm=1024, tn=1024, tk=1024, name="g_w_attn_out")
    dyn = _matmul(dys, w_so, tb=True, tm=1024, tn=2048, tk=1024, name="d_ssm_norm")
    g_wso = _matmul(yn, dys, ta=True, tm=1024, tn=1024, tk=1024, name="g_w_ssm_out")
    dproj3 = _attn_bwd(proj3, dyp.reshape(b, s, D_MODEL), o3, dproj.reshape(b, s, NP))
    (dxs, dbm, dcm, dproj3, ddtr_g, d_snw_g, d_alog_g, d_dtb_g, d_dsk_g) = _ssd_bwd(
        dyn.reshape(b, s, SSD_WIDTH), y3, xact, proj3, hst, dtr_g, dtb_g, alog_g, dskip_x, ssm_norm_w, dproj3)
    dproj3, g_cw_xs, g_cb_xs = _conv_bwd(dxs, proj3, conv_w_all, conv_b, 0, "conv_bwd_x", dproj3)
    dproj3, g_cw_bm, g_cb_bm = _conv_bwd(dbm, proj3, conv_w_all, conv_b, SSD_WIDTH, "conv_bwd_b", dproj3)
    dproj3, g_cw_cm, g_cb_cm = _conv_bwd(dcm, proj3, conv_w_all, conv_b, SSD_WIDTH + g4 * SSD_STATE, "conv_bwd_c", dproj3)
    ddt = ddtr_g[:, :, :, :hg].transpose(0, 2, 1, 3).reshape(b, s, g4 * hg).astype(BF16)
    ddt = jnp.pad(ddt, ((0, 0), (0, 0), (0, DT_PAD - g4 * hg)))
    dproj = lax.dynamic_update_slice(dproj3, ddt, (0, 0, DT0)).reshape(t, NP)
    exchanged = []
    if grad_exchange:
        g_wproj, *got = _matmul(dproj, h, ta=True, tm=1280, tn=1024, tk=1024, name="g_w_in",
                                exchange=grad_exchange([g_wao, g_wso, g_wo], "out"))
        exchanged += got
        dh, *got = _matmul(dproj, w_proj_t, tm=big_tm, tn=1024, tk=1280, name="d_h", exchange=grad_exchange([g_wproj], "in"))
        exchanged += got
    else:
        g_wproj = _matmul(dproj, h, ta=True, tm=1280, tn=1024, tk=1024, name="g_w_in")
        dh = _matmul(dproj, w_proj_t, tm=big_tm, tn=1024, tk=1280, name="d_h")
    grad_x, d_nw = _rms_bwd(dh, x2, norm_w, dout)
    g_cw = jnp.concatenate([g_cw_xs, g_cw_bm, g_cw_cm], axis=1)
    g_cb = jnp.concatenate([g_cb_xs, g_cb_bm, g_cb_cm], axis=1)
    return (loss_part, grad_x, d_nw, g_wproj, g_cw, g_cb, d_dtb_g, d_alog_g, d_dsk_g, d_snw_g, g_wao, g_wso, g_wo, d_fnw,
            exchanged)


def kernel(x, norm_w, w_in, conv_w, conv_b, dt_bias, a_log, d_skip, ssm_norm_w, w_attn_out, w_ssm_out, w_o, final_norm_w, loss_target, m_norm_w, m_w_in, m_conv_w, m_conv_b, m_dt_bias, m_a_log, m_d_skip, m_ssm_norm_w, m_w_attn_out, m_w_ssm_out, m_w_o, m_final_norm_w, v_norm_w, v_w_in, v_conv_w, v_conv_b, v_dt_bias, v_a_log, v_d_skip, v_ssm_norm_w, v_w_attn_out, v_w_ssm_out, v_w_o, v_final_norm_w):
    b, s, d = x.shape
    core = lax.axis_index("c")
    g4, hg = SSD_GROUPS, HEADS_PER_GROUP
    shard_cols = w_in.shape[2]
    w_proj_t, out_w, conv_w_all = _gather_all(w_in, w_attn_out, w_ssm_out, w_o, conv_w)
    chip = 2 * lax.axis_index("x") + lax.axis_index("y")
    place = jnp.stack([core, chip]).astype(jnp.int32)
    chip_sums = []

    def grad_exchange(grads, which):
        if which == "in":
            slabs = _from_proj_layout(grads[0]).reshape(N_CHIPS, shard_cols, d)
        else:
            slabs = jnp.concatenate([g.reshape(N_CHIPS, -1, d) for g in grads], axis=1)
        from_sibling, = _swap_halves([slabs], "grad_swap_halves_" + which)
        chip_sums.append(_add_halves(slabs, from_sibling, place, "grad_add_halves_" + which))
        return _all_to_all_exchange(chip_sums[-1:])

    (loss_part, grad_x, d_nw, _, g_cw, g_cb, d_dtb_g, d_alog_g, d_dsk_g, d_snw_g, _, _, _, d_fnw, from_chips) = _local_step(
        x, loss_target, norm_w, w_proj_t, conv_w_all, conv_b, dt_bias, a_log, d_skip, ssm_norm_w, out_w, final_norm_w,
        grad_exchange)
    wholes = [_sum_chips(o, r, place, "grad_sum_chips_%d" % i) for i, (o, r) in enumerate(zip(chip_sums, from_chips))]
    g_out, g_w_in = _join_halves(wholes)

    small = jnp.concatenate([
        loss_part, d_nw, g_cb, _lane_pad(d_dtb_g[:, 0, :hg].reshape(1, -1)), _lane_pad(d_alog_g[:, 0, :hg].reshape(1, -1)),
        _lane_pad(d_dsk_g[:, 0, :hg].reshape(1, -1)),
        d_snw_g.reshape(1, -1), d_fnw, g_cw.reshape(1, -1)], axis=1)
    small_rows = small.shape[1] // LANES
    reduced = _gather_small(_pad_rows(small.reshape(small_rows, LANES), -(-small_rows // 8) * 8), True, "reduce_small")
    flat = reduced.reshape(-1)

    def take(start, n):
        return flat[start:start + n].reshape(1, n)

    loss = flat[0]
    pos = LANES
    g_norm_w = take(pos, d); pos += d
    g_conv_b = take(pos, CONV_DIM); pos += CONV_DIM
    g_dt_bias = take(pos, g4 * hg); pos += LANES
    g_a_log = take(pos, g4 * hg); pos += LANES
    g_d_skip = take(pos, g4 * hg); pos += LANES
    g_ssm_norm_w = take(pos, SSD_WIDTH); pos += SSD_WIDTH
    g_final_norm_w = take(pos, d); pos += d
    conv_cols = CONV_DIM // N_CHIPS
    g_conv_w = lax.dynamic_slice_in_dim(flat[pos:pos + CONV_K * CONV_DIM].reshape(CONV_K, CONV_DIM), chip * conv_cols, conv_cols, axis=1)

    rows_ao, rows_so = D_MODEL // N_CHIPS, SSD_WIDTH // N_CHIPS
    g_w_attn_out = g_out[:rows_ao]
    g_w_ssm_out = g_out[rows_ao:rows_ao + rows_so]
    g_w_o = g_out[rows_ao + rows_so:]

    names = ["norm_w", "w_in", "conv_w", "conv_b", "dt_bias", "a_log", "d_skip", "ssm_norm_w",
             "w_attn_out", "w_ssm_out", "w_o", "final_norm_w"]
    weights = [norm_w, w_in, conv_w, conv_b, dt_bias, a_log, d_skip, ssm_norm_w, w_attn_out, w_ssm_out, w_o, final_norm_w]
    grads = [g_norm_w, g_w_in, g_conv_w, g_conv_b, g_dt_bias, g_a_log, g_d_skip, g_ssm_norm_w,
             g_w_attn_out, g_w_ssm_out, g_w_o, g_final_norm_w]
    ms = [m_norm_w, m_w_in, m_conv_w, m_conv_b, m_dt_bias, m_a_log, m_d_skip, m_ssm_norm_w,
          m_w_attn_out, m_w_ssm_out, m_w_o, m_final_norm_w]
    vs = [v_norm_w, v_w_in, v_conv_w, v_conv_b, v_dt_bias, v_a_log, v_d_skip, v_ssm_norm_w,
          v_w_attn_out, v_w_ssm_out, v_w_o, v_final_norm_w]
    out_g, out_d, out_m, out_v = [], [], [], []
    for name, w, g, m, v in zip(names, weights, grads, ms, vs):
        if name == "w_in":
            to2, back = (lambda a: a[0].T), (lambda a: a.T.reshape(w.shape))
        else:
            to2, back = (lambda a: a.reshape(g.shape)), (lambda a: a.reshape(w.shape))
        dlt, nm, nv = _adamw(to2(w), g, to2(m), to2(v), "adamw_" + name)
        out_g.append(back(g))
        out_d.append(back(dlt))
        out_m.append(back(nm))
        out_v.append(back(nv))

    return (loss, grad_x.reshape(b, s, d), *out_g, *out_d, *out_m, *out_v)
```

```python
import jax
import jax.numpy as jnp
from jax import lax
from jax.experimental import pallas as pl
from jax.experimental.pallas import tpu as pltpu

F32 = jnp.float32
BF16 = jnp.bfloat16
MESH = pl.DeviceIdType.MESH

D_MODEL = 1024
SB_HEADS = 16
HEAD_DIM = 64
SSD_WIDTH = 2048
SSD_GROUPS = 4
GROUP_WIDTH = SSD_WIDTH // SSD_GROUPS
HEADS_PER_GROUP = 8
SSD_STATE = 128
CHUNK = 128
CONV_K = 4
CONV_DIM = 3072
D_PROJ = 11296
EPS = 1e-6
ADAM_LR, ADAM_B1, ADAM_B2, ADAM_EPS, ADAM_WD, ADAM_STEP = 0.001, 0.9, 0.999, 1e-08, 0.01, 10

LANES = 128
HP_WIDTH = 4 * LANES
ZS0, GATE0, XBC0, DT0 = 4096, 6144, 8192, 11264
DT_PAD = 256
NP = DT0 + DT_PAD
N_CHIPS = 4
VMEM_LIMIT = 56 * 1024 * 1024


N_HP = SB_HEADS // 2
W_ZS0, W_XBC0, W_DT0, W_GATE0 = 4096, 6144, 9216, 9248


def _to_proj_layout(wt):
    d = wt.shape[1]
    pairs = wt[:W_ZS0].reshape(4, N_HP, LANES, d).transpose(1, 0, 2, 3).reshape(W_ZS0, d)
    return jnp.concatenate([pairs, wt[W_ZS0:W_XBC0], wt[W_GATE0:], wt[W_XBC0:W_DT0], wt[W_DT0:W_GATE0],
                            jnp.zeros((NP - D_PROJ, d), wt.dtype)], axis=0)


def _from_proj_layout(gt):
    d = gt.shape[1]
    qkvz = gt[:ZS0].reshape(N_HP, 4, LANES, d).transpose(1, 0, 2, 3).reshape(ZS0, d)
    return jnp.concatenate([qkvz, gt[ZS0:GATE0], gt[XBC0:DT0], gt[DT0:DT0 + W_GATE0 - W_DT0], gt[GATE0:XBC0]], axis=0)


def _cparams(*sem):
    return pltpu.CompilerParams(dimension_semantics=sem or None, vmem_limit_bytes=VMEM_LIMIT)


def _sigmoid(z):
    return 1.0 / (1.0 + jnp.exp(-z))


def _dot(a, b, dims, precision=None):
    return lax.dot_general(a, b, (dims, ((), ())), preferred_element_type=F32, precision=precision)


NN = ((1,), (0,))
NT = ((1,), (1,))
TN = ((0,), (0,))


def _matmul(a, b, *, ta=False, tb=False, out_dtype=F32, tm, tn, tk, name, exchange=None):
    m, k = (a.shape[1], a.shape[0]) if ta else a.shape
    n = b.shape[0] if tb else b.shape[1]
    assert m % tm == 0 and n % tn == 0 and k % tk == 0, (name, m, n, k)
    grid = (m // tm, n // tn, k // tk)
    nk = grid[2]
    use_scratch = out_dtype != F32
    dims = ((0,) if ta else (1,), (1,) if tb else (0,))
    n_in = len(exchange.inputs) if exchange else 0
    n_out = len(exchange.out_shapes) if exchange else 0

    def kern(a_ref, b_ref, *rest):
        x_in, o_ref, x_out, scratch = rest[:n_in], rest[n_in], rest[n_in + 1:n_in + 1 + n_out], rest[n_in + 1 + n_out:]
        acc = scratch[0] if use_scratch else o_ref
        step = [pl.program_id(ax) for ax in range(3)]
        if exchange:
            sems = scratch[1:] if use_scratch else scratch

            @pl.when(jnp.logical_and(jnp.logical_and(step[0] == 0, step[1] == 0), step[2] == 0))
            def _():
                exchange.start(x_in, x_out, sems)

        @pl.when(step[2] == 0)
        def _():
            acc[...] = jnp.zeros_like(acc)

        acc[...] += _dot(a_ref[...], b_ref[...], dims)
        if use_scratch:
            @pl.when(step[2] == nk - 1)
            def _():
                o_ref[...] = acc[...].astype(out_dtype)
        if exchange:
            @pl.when(jnp.logical_and(jnp.logical_and(step[0] == grid[0] - 1, step[1] == grid[1] - 1), step[2] == nk - 1))
            def _():
                exchange.finish(x_in, x_out, sems)

    a_spec = pl.BlockSpec((tk, tm), lambda i, j, q: (q, i)) if ta else pl.BlockSpec((tm, tk), lambda i, j, q: (i, q))
    b_spec = pl.BlockSpec((tn, tk), lambda i, j, q: (j, q)) if tb else pl.BlockSpec((tk, tn), lambda i, j, q: (q, j))
    out = pl.pallas_call(
        kern, name=name,
        out_shape=[jax.ShapeDtypeStruct((m, n), out_dtype)] + (list(exchange.out_shapes) if exchange else []),
        grid=grid,
        in_specs=[a_spec, b_spec] + [ANY] * n_in,
        out_specs=[pl.BlockSpec((tm, tn), lambda i, j, q: (i, j))] + [ANY] * n_out,
        scratch_shapes=([pltpu.VMEM((tm, tn), F32)] if use_scratch else []) + (list(exchange.sems) if exchange else []),
        compiler_params=_cparams("arbitrary", "arbitrary", "arbitrary") if exchange else _cparams("parallel", "parallel", "arbitrary"),
    )(a, b, *(exchange.inputs if exchange else []))
    return out if exchange else out[0]


ROWS = 512


def _rms_fwd(x2, w, exchange=None):
    t, d = x2.shape
    steps = t // ROWS
    n_in = len(exchange.inputs) if exchange else 0
    n_out = len(exchange.out_shapes) if exchange else 0

    def kern(x_ref, w_ref, *rest):
        x_in, h_ref, x_out, sems = rest[:n_in], rest[n_in], rest[n_in + 1:n_in + 1 + n_out], rest[n_in + 1 + n_out:]
        if exchange:
            @pl.when(pl.program_id(0) == 0)
            def _():
                exchange.start(x_in, x_out, sems)

        x = x_ref[...]
        r = lax.rsqrt(jnp.mean(x * x, axis=-1, keepdims=True) + EPS)
        h_ref[...] = (x * r * w_ref[...]).astype(BF16)
        if exchange:
            @pl.when(pl.program_id(0) == steps - 1)
            def _():
                exchange.finish(x_in, x_out, sems)

    out = pl.pallas_call(
        kern, name="rms_fwd",
        out_shape=[jax.ShapeDtypeStruct((t, d), BF16)] + (list(exchange.out_shapes) if exchange else []),
        grid=(steps,),
        in_specs=[pl.BlockSpec((ROWS, d), lambda i: (i, 0)), pl.BlockSpec((1, d), lambda i: (0, 0))] + [ANY] * n_in,
        out_specs=[pl.BlockSpec((ROWS, d), lambda i: (i, 0))] + [ANY] * n_out,
        scratch_shapes=list(exchange.sems) if exchange else [],
        compiler_params=_cparams("arbitrary" if exchange else "parallel"),
    )(x2, w, *(exchange.inputs if exchange else []))
    return out if exchange else out[0]


def _rms_bwd(dh, x2, w, dout):
    t, d = x2.shape

    def kern(dh_ref, x_ref, w_ref, dout_ref, gx_ref, dw_ref):
        @pl.when(pl.program_id(0) == 0)
        def _():
            dw_ref[...] = jnp.zeros_like(dw_ref)

        x = x_ref[...]
        r = lax.rsqrt(jnp.mean(x * x, axis=-1, keepdims=True) + EPS)
        xh = x * r
        g = dh_ref[...]
        dw_ref[...] += jnp.sum(g * xh, axis=0, keepdims=True)
        gw = g * w_ref[...]
        gx_ref[...] = dout_ref[...] + r * (gw - xh * jnp.mean(gw * xh, axis=-1, keepdims=True))

    row = pl.BlockSpec((ROWS, d), lambda i: (i, 0))
    vec = pl.BlockSpec((1, d), lambda i: (0, 0))
    return pl.pallas_call(
        kern, name="rms_bwd",
        out_shape=(jax.ShapeDtypeStruct((t, d), F32), jax.ShapeDtypeStruct((1, d), F32)),
        grid=(t // ROWS,),
        in_specs=[row, row, vec, row],
        out_specs=(row, vec),
        compiler_params=_cparams("arbitrary"),
    )(dh, x2, w, dout)


def _final_fwd_bwd(x2, mo, target, w):
    t, d = x2.shape

    def kern(x_ref, mo_ref, t_ref, w_ref, dout_ref, doutb_ref, loss_ref, dw_ref):
        @pl.when(pl.program_id(0) == 0)
        def _():
            loss_ref[...] = jnp.zeros_like(loss_ref)
            dw_ref[...] = jnp.zeros_like(dw_ref)

        u = x_ref[...] + mo_ref[...]
        r = lax.rsqrt(jnp.mean(u * u, axis=-1, keepdims=True) + EPS)
        uh = u * r
        wv = w_ref[...]
        err = uh * wv - t_ref[...]
        loss_ref[...] += (0.5 / d) * jnp.sum(err * err)
        dy = err * (1.0 / d)
        dw_ref[...] += jnp.sum(dy * uh, axis=0, keepdims=True)
        gw = dy * wv
        du = r * (gw - uh * jnp.mean(gw * uh, axis=-1, keepdims=True))
        dout_ref[...] = du
        doutb_ref[...] = du.astype(BF16)

    row = pl.BlockSpec((ROWS, d), lambda i: (i, 0))
    vec = pl.BlockSpec((1, d), lambda i: (0, 0))
    return pl.pallas_call(
        kern, name="final_fwd_bwd",
        out_shape=(jax.ShapeDtypeStruct((t, d), F32), jax.ShapeDtypeStruct((t, d), BF16),
                   jax.ShapeDtypeStruct((1, LANES), F32), jax.ShapeDtypeStruct((1, d), F32)),
        grid=(t // ROWS,),
        in_specs=[row, row, row, vec],
        out_specs=(row, row, pl.BlockSpec((1, LANES), lambda i: (0, 0)), vec),
        compiler_params=_cparams("arbitrary"),
    )(x2, mo, target, w)


def _merge_fwd(proj2, ya, ys):
    t = ya.shape[0]
    gblk = GATE0 // D_MODEL

    def kern(ga_ref, gs_ref, ya_ref, ys_ref, o_ref):
        o_ref[...] = (_sigmoid(ga_ref[...]) * ya_ref[...] + _sigmoid(gs_ref[...]) * ys_ref[...]).astype(BF16)

    row = pl.BlockSpec((ROWS, D_MODEL), lambda i: (i, 0))
    return pl.pallas_call(
        kern, name="merge_fwd",
        out_shape=jax.ShapeDtypeStruct((t, D_MODEL), BF16),
        grid=(t // ROWS,),
        in_specs=[pl.BlockSpec((ROWS, D_MODEL), lambda i: (i, gblk)),
                  pl.BlockSpec((ROWS, D_MODEL), lambda i: (i, gblk + 1)), row, row],
        out_specs=row,
        compiler_params=_cparams("parallel"),
    )(proj2, proj2, ya, ys)


def _merge_bwd(dm, proj2, ya, ys):
    t = ya.shape[0]
    gblk = GATE0 // D_MODEL

    def kern(dm_ref, ga_ref, gs_ref, ya_ref, ys_ref, dya_ref, dys_ref, dg_ref):
        g = dm_ref[...]
        sa = _sigmoid(ga_ref[...])
        ss = _sigmoid(gs_ref[...])
        dya_ref[...] = (g * sa).astype(BF16)
        dys_ref[...] = (g * ss).astype(BF16)
        dg_ref[:, :D_MODEL] = (g * ya_ref[...] * sa * (1.0 - sa)).astype(BF16)
        dg_ref[:, D_MODEL:] = (g * ys_ref[...] * ss * (1.0 - ss)).astype(BF16)

    row = pl.BlockSpec((ROWS, D_MODEL), lambda i: (i, 0))
    return pl.pallas_call(
        kern, name="merge_bwd",
        out_shape=(jax.ShapeDtypeStruct((t, D_MODEL), BF16), jax.ShapeDtypeStruct((t, D_MODEL), BF16),
                   jax.ShapeDtypeStruct((t, NP), BF16)),
        grid=(t // ROWS,),
        in_specs=[row, pl.BlockSpec((ROWS, D_MODEL), lambda i: (i, gblk)),
                  pl.BlockSpec((ROWS, D_MODEL), lambda i: (i, gblk + 1)), row, row],
        out_specs=(row, row, pl.BlockSpec((ROWS, 2 * D_MODEL), lambda i: (i, GATE0 // (2 * D_MODEL)))),
        compiler_params=_cparams("parallel"),
    )(dm, proj2, proj2, ya, ys)


TQ = 256
TK = 256
assert TQ == TK
HEAD_LANES = (slice(0, HEAD_DIM), slice(HEAD_DIM, 2 * HEAD_DIM))


def _tri(pred):
    r = lax.broadcasted_iota(jnp.int32, (TK, TK), 0)
    c = lax.broadcasted_iota(jnp.int32, (TK, TK), 1)
    return pred(r, c).astype(BF16)


def _split_bf16(v):
    hi = v.astype(BF16)
    lo = (v - hi.astype(F32)).astype(BF16)
    return hi, lo


def _tri_dot(v, tri):
    hi, lo = _split_bf16(v)
    return _dot(hi, tri, NN) + _dot(lo, tri, NN)


def _sb_logs(z, mask):
    l1p = jnp.log(1.0 + jnp.exp(-jnp.abs(z)))
    lb = jnp.minimum(z, 0.0) - l1p
    lom = -jnp.maximum(z, 0.0) - l1p
    if mask is not None:
        lom = jnp.where(mask, lom, 0.0)
    return lb, lom


def _sb_weights(lb, later, carry_r, mask):
    a = jnp.exp(lb + (later + carry_r))
    if mask is not None:
        a = jnp.where(mask, a, 0.0)
    return a


DEAD = -104.0


def _while_alive(n, carry, step):
    def alive(cr):
        return jnp.max(jnp.maximum(cr[0][0], cr[1][0])) > DEAD

    def cond(state):
        jj, go, _ = state
        return jnp.logical_and(jj < n, go)

    def body(state):
        jj, _, cr = state
        cr = step(jj, cr)
        return jj + 1, alive(cr), cr

    return lax.while_loop(cond, body, (jnp.int32(0), alive(carry), carry))[2]


Q_LANES, K_LANES, V_LANES, ZA_LANES = (slice(i * LANES, (i + 1) * LANES) for i in range(4))


def _split_heads(dst, src, scale=None):
    for h, lanes in enumerate(HEAD_LANES):
        v = src[:, lanes]
        dst[h] = (v if scale is None else v * scale).astype(BF16)


def _attn_fwd(proj3):
    b, s, _ = proj3.shape
    nq = s // TQ
    scale = HEAD_DIM ** -0.5

    def kern(x_ref, o_ref, yp_ref, qs, ks, vs):
        _split_heads(qs, x_ref[0, :, Q_LANES], scale)
        _split_heads(ks, x_ref[0, :, K_LANES])
        _split_heads(vs, x_ref[0, :, V_LANES])
        za_ref = x_ref.at[:, :, ZA_LANES]
        row = lax.broadcasted_iota(jnp.int32, (TQ, TK), 0)
        col = lax.broadcasted_iota(jnp.int32, (TQ, TK), 1)
        tri_gt = _tri(lambda j, sk: j > sk)

        def q_block(i, _):
            top = isinstance(i, int)
            r0 = i * TQ if top else pl.multiple_of(i * TQ, TQ)
            qh = [qs[h, pl.ds(r0, TQ), :] for h in range(2)]

            def k_blocks(blocks, carry):
                nb = range(len(blocks))
                kh = [[ks[h, pl.ds(c0, TK), :] for h in range(2)] for c0, _ in blocks]
                vh = [[vs[h, pl.ds(c0, TK), :] for h in range(2)] for c0, _ in blocks]
                z = [[_dot(qh[h], kh[bl][h], NT) for h in range(2)] for bl in nb]
                logs = [[_sb_logs(z[bl][h], blocks[bl][1]) for h in range(2)] for bl in nb]
                later = [[_tri_dot(logs[bl][h][1], tri_gt) for h in range(2)] for bl in nb]
                out = []
                for h in range(2):
                    carry_r, acc = carry[h]
                    for bl in nb:
                        lb, lom = logs[bl][h]
                        a = _sb_weights(lb, later[bl][h], carry_r, blocks[bl][1])
                        acc = acc + _dot(a.astype(BF16), vh[bl][h], NN)
                        carry_r = carry_r + (later[bl][h][:, 0:1] + lom[:, 0:1])
                    out.append((carry_r, acc))
                return tuple(out)

            start = (jnp.zeros((TQ, 1), F32), jnp.zeros((TQ, HEAD_DIM), F32))
            diag = (r0, col < row)
            if top:
                carry = k_blocks([diag], (start, start))
            else:
                carry = k_blocks([diag, (pl.multiple_of(r0 - TK, TK), None)], (start, start))
                carry = _while_alive(i - 1, carry, lambda jj, cr: k_blocks([(pl.multiple_of((i - 2 - jj) * TK, TK), None)], cr))
            for (_, acc), lanes in zip(carry, HEAD_LANES):
                o_ref[0, pl.ds(r0, TQ), lanes] = acc
                za = za_ref[0, pl.ds(r0, TQ), lanes]
                yp_ref[0, pl.ds(r0, TQ), lanes] = (acc * (za * _sigmoid(za))).astype(BF16)
            return 0

        q_block(0, 0)
        lax.fori_loop(1, nq, q_block, 0)

    out_spec = pl.BlockSpec((1, s, LANES), lambda bi, hp: (bi, 0, hp))
    return pl.pallas_call(
        kern, name="attn_fwd",
        out_shape=(jax.ShapeDtypeStruct((b, s, D_MODEL), F32), jax.ShapeDtypeStruct((b, s, D_MODEL), BF16)),
        grid=(b, SB_HEADS // 2),
        in_specs=[pl.BlockSpec((1, s, HP_WIDTH), lambda bi, hp: (bi, 0, hp))],
        out_specs=(out_spec, out_spec),
        scratch_shapes=[pltpu.VMEM((2, s, HEAD_DIM), BF16)] * 3,
        compiler_params=_cparams("parallel", "parallel"),
    )(proj3)


def _attn_bwd(proj3, dyp3, o3, dproj3):
    b, s, _ = proj3.shape
    nq = s // TQ
    scale = HEAD_DIM ** -0.5

    def kern(x_ref, dyp_ref, o_ref, _, d_ref, qs, ks, vs, dos, dk_acc, dv_acc):
        _split_heads(qs, x_ref[0, :, Q_LANES], scale)
        _split_heads(ks, x_ref[0, :, K_LANES])
        _split_heads(vs, x_ref[0, :, V_LANES])
        dq_ref, dk_ref, dv_ref = (d_ref.at[:, :, lanes] for lanes in (Q_LANES, K_LANES, V_LANES))
        za = x_ref[0, :, ZA_LANES]
        sg = _sigmoid(za)
        dyp = dyp_ref[0]
        _split_heads(dos, dyp * (za * sg))
        d_ref[0, :, ZA_LANES] = (dyp * o_ref[0] * (sg * (1.0 + za * (1.0 - sg)))).astype(BF16)
        dk_acc[...] = jnp.zeros_like(dk_acc)
        dv_acc[...] = jnp.zeros_like(dv_acc)
        row = lax.broadcasted_iota(jnp.int32, (TQ, TK), 0)
        col = lax.broadcasted_iota(jnp.int32, (TQ, TK), 1)
        tri_gt = _tri(lambda j, sk: j > sk)
        tri_ge = _tri(lambda j, sk: j >= sk)

        def q_block(i, _):
            top = isinstance(i, int)
            r0 = i * TQ if top else pl.multiple_of(i * TQ, TQ)
            qh = [qs[h, pl.ds(r0, TQ), :] for h in range(2)]
            doh = [dos[h, pl.ds(r0, TQ), :] for h in range(2)]
            totals = [jnp.sum(doh[h].astype(F32) * o_ref[0, pl.ds(r0, TQ), lanes], axis=1, keepdims=True)
                      for h, lanes in enumerate(HEAD_LANES)]

            def k_blocks(blocks, carry):
                nb = range(len(blocks))
                kh = [[ks[h, pl.ds(c0, TK), :] for h in range(2)] for c0, _ in blocks]
                vh = [[vs[h, pl.ds(c0, TK), :] for h in range(2)] for c0, _ in blocks]
                z = [[_dot(qh[h], kh[bl][h], NT) for h in range(2)] for bl in nb]
                da = [[_dot(doh[h], vh[bl][h], NT) for h in range(2)] for bl in nb]
                logs = [[_sb_logs(z[bl][h], blocks[bl][1]) for h in range(2)] for bl in nb]
                later = [[_tri_dot(logs[bl][h][1], tri_gt) for h in range(2)] for bl in nb]
                ab, g, suffix = ([[None, None] for _ in nb] for _ in range(3))
                for h in range(2):
                    cr = carry[h][0]
                    for bl in nb:
                        a = _sb_weights(logs[bl][h][0], later[bl][h], cr, blocks[bl][1])
                        ab[bl][h] = a.astype(BF16)
                        g[bl][h] = da[bl][h] * ab[bl][h].astype(F32)
                        suffix[bl][h] = _tri_dot(g[bl][h], tri_ge)
                        cr = cr + (later[bl][h][:, 0:1] + logs[bl][h][1][:, 0:1])
                out = []
                for h in range(2):
                    _, carry_g, dq = carry[h]
                    cr = carry[h][0]
                    for bl in nb:
                        c0, mask = blocks[bl]
                        lb, lom = logs[bl][h]
                        dz = g[bl][h] - (g[bl][h] + (totals[h] - carry_g) - suffix[bl][h]) * jnp.exp(lb)
                        if mask is not None:
                            dz = jnp.where(mask, dz, 0.0)
                        dzb = dz.astype(BF16)
                        dk_acc[h, pl.ds(c0, TK), :] += _dot(dzb, qh[h], TN)
                        dv_acc[h, pl.ds(c0, TK), :] += _dot(ab[bl][h], doh[h], TN)
                        dq = dq + _dot(dzb, kh[bl][h], NN)
                        carry_g = carry_g + suffix[bl][h][:, 0:1]
                        cr = cr + (later[bl][h][:, 0:1] + lom[:, 0:1])
                    out.append((cr, carry_g, dq))
                return tuple(out)

            def k_block(c0, carry, mask):
                kh = [ks[h, pl.ds(c0, TK), :] for h in range(2)]
                vh = [vs[h, pl.ds(c0, TK), :] for h in range(2)]
                z = [_dot(qh[h], kh[h], NT) for h in range(2)]
                da = [_dot(doh[h], vh[h], NT) for h in range(2)]
                logs, later = [], []
                for h in range(2):
                    logs.append(_sb_logs(z[h], mask))
                    later.append(_tri_dot(logs[h][1], tri_gt))
                ab, g, suffix = [], [], []
                for h in range(2):
                    a = _sb_weights(logs[h][0], later[h], carry[h][0], mask)
                    ab.append(a.astype(BF16))
                    g.append(da[h] * ab[h].astype(F32))
                    suffix.append(_tri_dot(g[h], tri_ge))
                out = []
                for h in range(2):
                    carry_r, carry_g, dq = carry[h]
                    lb, lom = logs[h]
                    dz = g[h] - (g[h] + (totals[h] - carry_g) - suffix[h]) * jnp.exp(lb)
                    if mask is not None:
                        dz = jnp.where(mask, dz, 0.0)
                    dzb = dz.astype(BF16)
                    dk_acc[h, pl.ds(c0, TK), :] += _dot(dzb, qh[h], TN)
                    dv_acc[h, pl.ds(c0, TK), :] += _dot(ab[h], doh[h], TN)
                    out.append((carry_r + (later[h][:, 0:1] + lom[:, 0:1]), carry_g + suffix[h][:, 0:1],
                                dq + _dot(dzb, kh[h], NN)))
                return tuple(out)

            zero = jnp.zeros((TQ, 1), F32)
            start = (zero, zero, jnp.zeros((TQ, HEAD_DIM), F32))
            diag = (r0, col < row)
            if top:
                carry = k_block(r0, (start, start), col < row)
            else:
                carry = k_blocks([diag, (pl.multiple_of(r0 - TK, TK), None)], (start, start))
                carry = _while_alive(i - 1, carry, lambda jj, cr: k_block(pl.multiple_of((i - 2 - jj) * TK, TK), cr, None))
            for (_, _, dq), lanes in zip(carry, HEAD_LANES):
                dq_ref[0, pl.ds(r0, TQ), lanes] = (dq * scale).astype(BF16)
            return 0

        q_block(0, 0)
        lax.fori_loop(1, nq, q_block, 0)

        for h, lanes in enumerate(HEAD_LANES):
            dk_ref[0, :, lanes] = dk_acc[h].astype(BF16)
            dv_ref[0, :, lanes] = dv_acc[h].astype(BF16)

    plain = pl.BlockSpec((1, s, LANES), lambda bi, hp: (bi, 0, hp))
    pair = pl.BlockSpec((1, s, HP_WIDTH), lambda bi, hp: (bi, 0, hp))
    return pl.pallas_call(
        kern, name="attn_bwd",
        out_shape=jax.ShapeDtypeStruct(dproj3.shape, dproj3.dtype),
        grid=(b, SB_HEADS // 2),
        in_specs=[pair, plain, plain, ANY],
        out_specs=pair,
        input_output_aliases={3: 0},
        scratch_shapes=[pltpu.VMEM((2, s, HEAD_DIM), BF16)] * 4 + [pltpu.VMEM((2, s, HEAD_DIM), F32)] * 2,
        compiler_params=_cparams("parallel", "parallel"),
    )(proj3, dyp3, o3, dproj3)


CONV_COLS = 256
HALO = 8


def _conv_pre(xp, w_ref, b_ref, r0):
    pre = b_ref[...] + w_ref[CONV_K - 1:CONV_K, :] * xp[pl.ds(HALO + r0, CHUNK), :]
    for kk in range(1, CONV_K):
        pre = pre + w_ref[CONV_K - 1 - kk:CONV_K - kk, :] * xp[pl.ds(HALO + r0 - kk, CHUNK), :]
    return pre


def _conv_fwd(proj3, conv_w, conv_b):
    b, s, _ = proj3.shape
    nc = s // CHUNK

    def kern(x_ref, w_ref, b_ref, o_ref, xp):
        xp[0:HALO, :] = jnp.zeros((HALO, CONV_COLS), F32)
        xp[HALO:, :] = x_ref[0]
        for ci in range(nc):
            pre = _conv_pre(xp, w_ref, b_ref, ci * CHUNK)
            o_ref[0, ci * CHUNK:(ci + 1) * CHUNK, :] = pre * _sigmoid(pre)

    return pl.pallas_call(
        kern, name="conv_fwd",
        out_shape=jax.ShapeDtypeStruct((b, s, CONV_DIM), F32),
        grid=(CONV_DIM // CONV_COLS, b),
        in_specs=[pl.BlockSpec((1, s, CONV_COLS), lambda j, bi: (bi, 0, XBC0 // CONV_COLS + j)),
                  pl.BlockSpec((CONV_K, CONV_COLS), lambda j, bi: (0, j)),
                  pl.BlockSpec((1, CONV_COLS), lambda j, bi: (0, j))],
        out_specs=pl.BlockSpec((1, s, CONV_COLS), lambda j, bi: (bi, 0, j)),
        scratch_shapes=[pltpu.VMEM((s + HALO, CONV_COLS), F32)],
        compiler_params=_cparams("parallel", "parallel"),
    )(proj3, conv_w, conv_b)


def _conv_bwd(dact, proj3, conv_w, conv_b, col0, name, dproj3):
    b, s, width = dact.shape
    nc = s // CHUNK
    j0 = col0 // CONV_COLS

    def kern(da_ref, x_ref, w_ref, b_ref, _, dx_ref, dw_ref, db_ref, xp, dp):
        @pl.when(pl.program_id(1) == 0)
        def _():
            dw_ref[...] = jnp.zeros_like(dw_ref)
            db_ref[...] = jnp.zeros_like(db_ref)

        xp[0:HALO, :] = jnp.zeros((HALO, CONV_COLS), F32)
        xp[HALO:, :] = x_ref[0]
        dp[s:, :] = jnp.zeros((HALO, CONV_COLS), F32)
        for ci in range(nc):
            r0 = ci * CHUNK
            pre = _conv_pre(xp, w_ref, b_ref, r0)
            sg = _sigmoid(pre)
            dpre = da_ref[0, r0:r0 + CHUNK, :] * (sg * (1.0 + pre * (1.0 - sg)))
            dp[r0:r0 + CHUNK, :] = dpre
            db_ref[...] += jnp.sum(dpre, axis=0, keepdims=True)
            for kk in range(CONV_K):
                tap = CONV_K - 1 - kk
                dw_ref[tap:tap + 1, :] += jnp.sum(dpre * xp[pl.ds(HALO + r0 - kk, CHUNK), :], axis=0, keepdims=True)
        for ci in range(nc):
            r0 = ci * CHUNK
            dx = w_ref[CONV_K - 1:CONV_K, :] * dp[pl.ds(r0, CHUNK), :]
            for kk in range(1, CONV_K):
                dx = dx + w_ref[CONV_K - 1 - kk:CONV_K - kk, :] * dp[pl.ds(r0 + kk, CHUNK), :]
            dx_ref[0, r0:r0 + CHUNK, :] = dx.astype(BF16)

    return pl.pallas_call(
        kern, name=name,
        out_shape=(jax.ShapeDtypeStruct(dproj3.shape, dproj3.dtype), jax.ShapeDtypeStruct((CONV_K, width), F32),
                   jax.ShapeDtypeStruct((1, width), F32)),
        grid=(width // CONV_COLS, b),
        in_specs=[pl.BlockSpec((1, s, CONV_COLS), lambda j, bi: (bi, 0, j)),
                  pl.BlockSpec((1, s, CONV_COLS), lambda j, bi: (bi, 0, XBC0 // CONV_COLS + j0 + j)),
                  pl.BlockSpec((CONV_K, CONV_COLS), lambda j, bi: (0, j0 + j)),
                  pl.BlockSpec((1, CONV_COLS), lambda j, bi: (0, j0 + j)), ANY],
        out_specs=(pl.BlockSpec((1, s, CONV_COLS), lambda j, bi: (bi, 0, XBC0 // CONV_COLS + j0 + j)),
                   pl.BlockSpec((CONV_K, CONV_COLS), lambda j, bi: (0, j)),
                   pl.BlockSpec((1, CONV_COLS), lambda j, bi: (0, j))),
        input_output_aliases={4: 0},
        scratch_shapes=[pltpu.VMEM((s + HALO, CONV_COLS), F32)] * 2,
        compiler_params=_cparams("parallel", "arbitrary"),
    )(dact, proj3, conv_w, conv_b, dproj3)


SSD_CHUNKS_PER_STEP = 8


def _sel_dot(v, sel, left=False):
    hi = v.astype(BF16)
    rest = v - hi.astype(F32)
    mid = rest.astype(BF16)
    lo = (rest - mid.astype(F32)).astype(BF16)
    if left:
        return _dot(sel, hi, NN) + _dot(sel, mid, NN) + _dot(sel, lo, NN)
    return _dot(hi, sel, NN) + _dot(mid, sel, NN) + _dot(lo, sel, NN)


def _ssd_common(dtr, dtb, alog):
    lane = lax.broadcasted_iota(jnp.int32, (CHUNK, LANES), 1)
    row = lax.broadcasted_iota(jnp.int32, (CHUNK, LANES), 0)
    head_lane = lane < HEADS_PER_GROUP
    pre = dtr + dtb
    dt = jnp.where(head_lane, jnp.maximum(pre, 0.0) + jnp.log(1.0 + jnp.exp(-jnp.abs(pre))), 0.0)
    a = jnp.where(head_lane[0:1], -jnp.exp(alog), 0.0)
    tril = (row >= lane).astype(BF16)
    acs = _sel_dot(dt * a, tril, left=True)
    acs_t = acs.T
    er = lax.broadcasted_iota(jnp.int32, (LANES, GROUP_WIDTH), 0)
    ec = lax.broadcasted_iota(jnp.int32, (LANES, GROUP_WIDTH), 1)
    expand = ((ec // HEAD_DIM) == er).astype(BF16)
    tr = lax.broadcasted_iota(jnp.int32, (GROUP_WIDTH, LANES), 0)
    tc = lax.broadcasted_iota(jnp.int32, (GROUP_WIDTH, LANES), 1)
    reduce = ((tr // HEAD_DIM) == tc).astype(BF16)
    dt_x = _sel_dot(dt, expand)
    acs_x = _sel_dot(acs, expand)
    end_x = acs_x[CHUNK - 1:CHUNK, :]
    causal = row >= lane
    return dict(dt=dt, a=a, pre=pre, head_lane=head_lane, acs=acs, acs_t=acs_t, expand=expand, reduce=reduce,
                dt_x=dt_x, acs_x=acs_x, end_x=end_x, causal=causal, row=row, lane=lane)


def _ssd_decay(cm, h):
    seg = cm["acs"][:, h:h + 1] - cm["acs_t"][h:h + 1, :]
    return jnp.where(cm["causal"], jnp.exp(jnp.minimum(seg, 0.0)), 0.0)


def _ssd_fwd(xact, proj3, dtr_g, dtb_g, alog_g, dskip_x, snw):
    b, s, _ = xact.shape
    nc = s // CHUNK
    g4 = SSD_GROUPS
    cps = min(nc, max(1, SSD_CHUNKS_PER_STEP // b))
    rows_per_step = cps * CHUNK

    def kern(xs_ref, bm_ref, cm_ref, zs_ref, dtr_ref, dtb_ref, alog_ref, dsk_ref, snw_ref,
             y_ref, yn_ref, hst_ref, h_sc):
        @pl.when(pl.program_id(2) == 0)
        def _():
            h_sc[...] = jnp.zeros_like(h_sc)

        def chunk(ci, _):
            for bi in range(b):
                one = pl.ds(bi, 1)
                chunk_of(ci, *[r.at[one] for r in (xs_ref, bm_ref, cm_ref, zs_ref, dtr_ref, y_ref, yn_ref, hst_ref)],
                         h_sc.at[bi])
            return 0

        def chunk_of(ci, xs_ref, bm_ref, cm_ref, zs_ref, dtr_ref, y_ref, yn_ref, hst_ref, h_sc):
            rows = pl.ds(pl.multiple_of(ci * CHUNK, CHUNK), CHUNK)
            cm = _ssd_common(dtr_ref[0, 0, rows, :], dtb_ref[0], alog_ref[0])
            x = xs_ref[0, rows, :]
            bmb = bm_ref[0, rows, :].astype(BF16)
            cmb = cm_ref[0, rows, :].astype(BF16)
            h_in = h_sc[...]
            hst_ref[0, ci, 0] = h_in
            xdt = x * cm["dt_x"]
            xdtb = xdt.astype(BF16)
            cb = _dot(cmb, bmb, NT)
            y_off = _dot(cmb, h_in.astype(BF16), NN) * jnp.exp(cm["acs_x"])
            for h in range(HEADS_PER_GROUP):
                lanes = slice(h * HEAD_DIM, (h + 1) * HEAD_DIM)
                m = (cb * _ssd_decay(cm, h)).astype(BF16)
                y_ref[0, rows, lanes] = _dot(m, xdtb[:, lanes], NN)
            y = y_ref[0, rows, :] + y_off + x * dsk_ref[...]
            y_ref[0, rows, :] = y
            w = (xdt * jnp.exp(cm["end_x"] - cm["acs_x"])).astype(BF16)
            h_sc[...] = h_in * jnp.exp(cm["end_x"]) + _dot(bmb, w, TN)
            zs = zs_ref[0, rows, :]
            y2 = y * (zs * _sigmoid(zs))
            yn_ref[0, rows, :] = (y2 * lax.rsqrt(jnp.mean(y2 * y2, axis=-1, keepdims=True) + EPS) * snw_ref[...]).astype(BF16)

        lax.fori_loop(0, cps, chunk, 0, unroll=True)

    gw = GROUP_WIDTH
    small = pl.BlockSpec((1, 1, LANES), lambda gi, bi, ci: (gi, 0, 0))
    xblk = pl.BlockSpec((b, rows_per_step, gw), lambda gi, bi, ci: (bi, ci, gi))
    return pl.pallas_call(
        kern, name="ssd_fwd",
        out_shape=(jax.ShapeDtypeStruct((b, s, SSD_WIDTH), F32), jax.ShapeDtypeStruct((b, s, SSD_WIDTH), BF16),
                   jax.ShapeDtypeStruct((b, nc, g4, SSD_STATE, gw), F32)),
        grid=(g4, 1, nc // cps),
        in_specs=[xblk,
                  pl.BlockSpec((b, rows_per_step, LANES), lambda gi, bi, ci: (bi, ci, SSD_WIDTH // LANES + gi)),
                  pl.BlockSpec((b, rows_per_step, LANES), lambda gi, bi, ci: (bi, ci, SSD_WIDTH // LANES + g4 + gi)),
                  pl.BlockSpec((b, rows_per_step, gw), lambda gi, bi, ci: (bi, ci, ZS0 // gw + gi)),
                  pl.BlockSpec((b, 1, rows_per_step, LANES), lambda gi, bi, ci: (bi, gi, ci, 0)),
                  small, small,
                  pl.BlockSpec((1, gw), lambda gi, bi, ci: (0, gi)),
                  pl.BlockSpec((1, gw), lambda gi, bi, ci: (0, gi))],
        out_specs=(xblk, xblk, pl.BlockSpec((b, cps, 1, SSD_STATE, gw), lambda gi, bi, ci: (bi, ci, gi, 0, 0))),
        scratch_shapes=[pltpu.VMEM((b, SSD_STATE, gw), F32)],
        compiler_params=_cparams("parallel", "parallel", "arbitrary"),
    )(xact, xact, xact, proj3, dtr_g, dtb_g, alog_g, dskip_x, snw)


def _ssd_bwd(dyn3, y3, xact, proj3, hst, dtr_g, dtb_g, alog_g, dskip_x, snw, dproj3):
    b, s, _ = xact.shape
    nc = s // CHUNK
    g4 = SSD_GROUPS
    gw = GROUP_WIDTH

    cps = min(nc, max(1, SSD_CHUNKS_PER_STEP // b))
    rows_per_step = cps * CHUNK

    def one_chunk(dyn_ref, y_ref, xs_ref, bm_ref, cm_ref, zs_ref, hst_ref, dtr_ref, dtb_ref, alog_ref, dsk_ref, snw_ref,
                  dxs_ref, dbm_ref, dcm_ref, dzs_ref, ddtr_ref, dsnw_ref, dalog_ref, ddtb_ref, ddsk_ref, dh_sc):
        cm = _ssd_common(dtr_ref[0, 0], dtb_ref[0], alog_ref[0])
        row, lane = cm["row"], cm["lane"]
        y = y_ref[0]
        zs = zs_ref[0]
        sg = _sigmoid(zs)
        silu = zs * sg
        y2 = y * silu
        rstd = lax.rsqrt(jnp.mean(y2 * y2, axis=-1, keepdims=True) + EPS)
        y2h = y2 * rstd
        dyn = dyn_ref[0]
        dsnw_ref[0] += jnp.sum(dyn * y2h, axis=0, keepdims=True)
        gwv = dyn * snw_ref[...]
        dy2 = rstd * (gwv - y2h * jnp.mean(gwv * y2h, axis=-1, keepdims=True))
        dzs_ref[0] = (dy2 * y * (sg * (1.0 + zs * (1.0 - sg)))).astype(BF16)
        dy = dy2 * silu
        dyb = dy.astype(BF16)

        x = xs_ref[0]
        bmb = bm_ref[0].astype(BF16)
        cmb = cm_ref[0].astype(BF16)
        h_in = hst_ref[0, 0, 0]
        h_inb = h_in.astype(BF16)
        d_hn = dh_sc[...]
        d_hnb = d_hn.astype(BF16)
        xdt = x * cm["dt_x"]
        xdtb = xdt.astype(BF16)
        eacs = jnp.exp(cm["acs_x"])
        dte = jnp.exp(cm["end_x"] - cm["acs_x"])
        wb = (xdt * dte).astype(BF16)

        dsk_lanes = jnp.broadcast_to(jnp.sum(dy * x, axis=0, keepdims=True), (8, gw))
        ddsk_ref[0] += _sel_dot(dsk_lanes, cm["reduce"])[0:1, :]
        dyo = dy * eacs
        dyob = dyo.astype(BF16)
        dacs_x = dyo * _dot(cmb, h_inb, NN)
        dcm = _dot(dyob, h_inb, NT)
        dh_in = _dot(cmb, dyob, TN)
        dw = _dot(bmb, d_hnb, NN)
        dbm = _dot(wb, d_hnb, NT)
        dxdt = dw * dte
        e_l = dw * xdt * dte
        dacs_x = dacs_x - e_l
        dend_x = jnp.sum(e_l, axis=0, keepdims=True)
        chunk_decay = jnp.exp(cm["end_x"])
        dh_sc[...] = d_hn * chunk_decay + dh_in
        dend_x = dend_x + jnp.sum(d_hn * h_in, axis=0, keepdims=True) * chunk_decay
        last_row = lax.broadcasted_iota(jnp.int32, (CHUNK, gw), 0) == CHUNK - 1
        dacs_x = dacs_x + jnp.where(last_row, dend_x, 0.0)

        cb = _dot(cmb, bmb, NT)
        dcb = jnp.zeros((CHUNK, CHUNK), F32)
        dacs = jnp.zeros((CHUNK, LANES), F32)
        dacs_t = jnp.zeros((LANES, CHUNK), F32)
        for h in range(HEADS_PER_GROUP):
            lanes = slice(h * HEAD_DIM, (h + 1) * HEAD_DIM)
            decay = _ssd_decay(cm, h)
            m = cb * decay
            dm = _dot(dyb[:, lanes], xdtb[:, lanes], NT)
            dxs_ref[0, :, lanes] = _dot(m.astype(BF16), dyb[:, lanes], TN)
            dcb_h = dm * decay
            dcb = dcb + dcb_h
            n = dcb_h * cb
            dacs = dacs + jnp.where(lane == h, jnp.sum(n, axis=1, keepdims=True), 0.0)
            dacs_t = dacs_t + jnp.where(row == h, jnp.sum(n, axis=0, keepdims=True), 0.0)
        dcbb = dcb.astype(BF16)
        dcm_ref[0] = dcm + _dot(dcbb, bmb, NN)
        dbm_ref[0] = dbm + _dot(dcbb, cmb, TN)
        dxdt = dxdt + dxs_ref[0]
        dxs_ref[0] = dy * dsk_ref[...] + dxdt * cm["dt_x"]

        dacs = dacs - dacs_t.T + _sel_dot(dacs_x, cm["reduce"])
        ddt = _sel_dot(dxdt * x, cm["reduce"])
        triu = (row <= lane).astype(BF16)
        rc = _sel_dot(dacs, triu, left=True)
        ddt = ddt + cm["a"] * rc
        dalog_ref[0] += jnp.sum(cm["dt"] * rc, axis=0, keepdims=True) * cm["a"]
        ddtr = jnp.where(cm["head_lane"], ddt * _sigmoid(cm["pre"]), 0.0)
        ddtr_ref[0, 0] = ddtr
        ddtb_ref[0] += jnp.sum(ddtr, axis=0, keepdims=True)

    def kern(dyn_ref, y_ref, xs_ref, bm_ref, cm_ref, zs_ref, hst_ref, dtr_ref, dtb_ref, alog_ref, dsk_ref, snw_ref, _,
             dxs_ref, dbm_ref, dcm_ref, dzs_ref, ddtr_ref, dsnw_ref, dalog_ref, ddtb_ref, ddsk_ref, dh_sc):
        first = jnp.logical_and(pl.program_id(1) == 0, pl.program_id(2) == 0)

        @pl.when(first)
        def _():
            dsnw_ref[...] = jnp.zeros_like(dsnw_ref)
            dalog_ref[...] = jnp.zeros_like(dalog_ref)
            ddtb_ref[...] = jnp.zeros_like(ddtb_ref)
            ddsk_ref[...] = jnp.zeros_like(ddsk_ref)

        @pl.when(pl.program_id(2) == 0)
        def _():
            dh_sc[...] = jnp.zeros_like(dh_sc)

        def chunk(k, _):
            ci = cps - 1 - k
            rows = pl.ds(pl.multiple_of(ci * CHUNK, CHUNK), CHUNK)
            for bi in range(b):
                one = pl.ds(bi, 1)
                by_rows = [r.at[one, rows, :] for r in (dyn_ref, y_ref, xs_ref, bm_ref, cm_ref, zs_ref)]
                one_chunk(*by_rows, hst_ref.at[one, pl.ds(ci, 1)], dtr_ref.at[one, :, rows, :], dtb_ref, alog_ref, dsk_ref, snw_ref,
                          *[r.at[one, rows, :] for r in (dxs_ref, dbm_ref, dcm_ref, dzs_ref)], ddtr_ref.at[one, :, rows, :],
                          dsnw_ref, dalog_ref, ddtb_ref, ddsk_ref, dh_sc.at[bi])
            return 0

        lax.fori_loop(0, cps, chunk, 0, unroll=True)

    def rev(ci):
        return nc // cps - 1 - ci

    small = pl.BlockSpec((1, 1, LANES), lambda gi, bi, ci: (gi, 0, 0))
    xblk = pl.BlockSpec((b, rows_per_step, gw), lambda gi, bi, ci: (bi, rev(ci), gi))
    nblk = pl.BlockSpec((b, rows_per_step, LANES), lambda gi, bi, ci: (bi, rev(ci), gi))
    gvec = pl.BlockSpec((1, gw), lambda gi, bi, ci: (0, gi))
    gacc = pl.BlockSpec((1, 1, gw), lambda gi, bi, ci: (gi, 0, 0))
    return pl.pallas_call(
        kern, name="ssd_bwd",
        out_shape=(jax.ShapeDtypeStruct((b, s, SSD_WIDTH), F32),
                   jax.ShapeDtypeStruct((b, s, g4 * SSD_STATE), F32),
                   jax.ShapeDtypeStruct((b, s, g4 * SSD_STATE), F32),
                   jax.ShapeDtypeStruct(dproj3.shape, dproj3.dtype),
                   jax.ShapeDtypeStruct((b, g4, s, LANES), F32),
                   jax.ShapeDtypeStruct((g4, 1, gw), F32),
                   jax.ShapeDtypeStruct((g4, 1, LANES), F32),
                   jax.ShapeDtypeStruct((g4, 1, LANES), F32),
                   jax.ShapeDtypeStruct((g4, 1, LANES), F32)),
        grid=(g4, 1, nc // cps),
        in_specs=[xblk, xblk, xblk,
                  pl.BlockSpec((b, rows_per_step, LANES), lambda gi, bi, ci: (bi, rev(ci), SSD_WIDTH // LANES + gi)),
                  pl.BlockSpec((b, rows_per_step, LANES), lambda gi, bi, ci: (bi, rev(ci), SSD_WIDTH // LANES + g4 + gi)),
                  pl.BlockSpec((b, rows_per_step, gw), lambda gi, bi, ci: (bi, rev(ci), ZS0 // gw + gi)),
                  pl.BlockSpec((b, cps, 1, SSD_STATE, gw), lambda gi, bi, ci: (bi, rev(ci), gi, 0, 0)),
                  pl.BlockSpec((b, 1, rows_per_step, LANES), lambda gi, bi, ci: (bi, gi, rev(ci), 0)),
                  small, small, gvec, gvec, ANY],
        out_specs=(xblk, nblk, nblk,
                   pl.BlockSpec((b, rows_per_step, gw), lambda gi, bi, ci: (bi, rev(ci), ZS0 // gw + gi)),
                   pl.BlockSpec((b, 1, rows_per_step, LANES), lambda gi, bi, ci: (bi, gi, rev(ci), 0)),
                   gacc, small, small, small),
        input_output_aliases={12: 3},
        scratch_shapes=[pltpu.VMEM((b, SSD_STATE, gw), F32)],
        compiler_params=_cparams("parallel", "arbitrary", "arbitrary"),
    )(dyn3, y3, xact, xact, xact, proj3, hst, dtr_g, dtb_g, alog_g, dskip_x, snw, dproj3)


def _adamw(w, g, m, v, name):
    r, c = w.shape
    tr = 128 if r % 128 == 0 else r
    tc = LANES if (tr == r and r > 128 and c % LANES == 0) else c

    def kern(w_ref, g_ref, m_ref, v_ref, d_ref, nm_ref, nv_ref):
        gv = g_ref[...]
        nm = ADAM_B1 * m_ref[...] + (1.0 - ADAM_B1) * gv
        nv = ADAM_B2 * v_ref[...] + (1.0 - ADAM_B2) * (gv * gv)
        m_hat = nm / (1.0 - ADAM_B1 ** ADAM_STEP)
        v_hat = nv / (1.0 - ADAM_B2 ** ADAM_STEP)
        d_ref[...] = -ADAM_LR * (m_hat / (jnp.sqrt(v_hat) + ADAM_EPS) + ADAM_WD * w_ref[...])
        nm_ref[...] = nm
        nv_ref[...] = nv

    blk = pl.BlockSpec((tr, tc), lambda i, j: (i, j))
    out = jax.ShapeDtypeStruct((r, c), F32)
    return pl.pallas_call(
        kern, name=name, out_shape=(out, out, out), grid=(r // tr, c // tc),
        in_specs=[blk] * 4, out_specs=(blk, blk, blk),
        compiler_params=_cparams("parallel", "parallel"),
    )(w, g, m, v)


ANY = pl.BlockSpec(memory_space=pl.ANY)


def _position():
    return lax.axis_index("x"), lax.axis_index("y"), lax.axis_index("c")


def _other_chips(x, y):
    return [(1 - x, y), (x, 1 - y), (1 - x, 1 - y)]


def _dma_sems(n):
    return [pltpu.SemaphoreType.DMA((n,)), pltpu.SemaphoreType.DMA((n,))]


class _Exchange:
    def __init__(self, inputs, out_shapes, sems, start, finish):
        self.inputs, self.out_shapes, self.sems, self.start, self.finish = inputs, out_shapes, sems, start, finish


def _gather_exchange(shards):
    n = len(shards)

    def copies(p_refs, out_refs, sems):
        send_sems, recv_sems = sems
        x, y, c = _position()
        me = 2 * x + y
        chips = _other_chips(x, y)

        def slab(a, chip, hf):
            half = shards[a].shape[1] // 2
            return out_refs[a].at[chip, :, pl.ds(hf * half, half)]

        def my_half(a):
            half = shards[a].shape[1] // 2
            return p_refs[a].at[:, pl.ds(c * half, half)]

        def over_ici(a, j, chip_from):
            px, py = chips[j]
            return pltpu.make_async_remote_copy(
                src_ref=my_half(a), dst_ref=slab(a, chip_from, c),
                send_sem=send_sems.at[3 * a + j], recv_sem=recv_sems.at[3 * a + j],
                device_id=(px, py, c), device_id_type=MESH)

        def to_sibling(a, j, hf):
            px, py = chips[j]
            return pltpu.make_async_remote_copy(
                src_ref=slab(a, 2 * px + py, hf), dst_ref=slab(a, 2 * px + py, hf),
                send_sem=send_sems.at[3 * (n + a) + j], recv_sem=recv_sems.at[3 * (n + a) + j],
                device_id=(x, y, 1 - c), device_id_type=MESH)

        own = [pltpu.make_async_remote_copy(
            src_ref=p_refs[a], dst_ref=out_refs[a].at[me], send_sem=send_sems.at[6 * n + a], recv_sem=recv_sems.at[6 * n + a],
            device_id=(x, y, 1 - c), device_id_type=MESH) for a in range(n)]
        first = [over_ici(a, j, me) for a in range(n) for j in range(3)]
        return chips, c, over_ici, to_sibling, first, own

    def start(p_refs, out_refs, sems):
        _, _, _, _, first, own = copies(p_refs, out_refs, sems)
        for cp in first + own:
            cp.start()

    def finish(p_refs, out_refs, sems):
        chips, c, over_ici, to_sibling, first, own = copies(p_refs, out_refs, sems)
        passed = []
        for a in range(n):
            for j, (px, py) in enumerate(chips):
                over_ici(a, j, 2 * px + py).wait_recv()
                passed.append(to_sibling(a, j, c))
                passed[-1].start()
        for a in range(n):
            for j in range(3):
                to_sibling(a, j, 1 - c).wait_recv()
        for cp in first + passed:
            cp.wait_send()
        for cp in own:
            cp.wait()

    return _Exchange(list(shards), [jax.ShapeDtypeStruct((N_CHIPS, *v.shape), v.dtype) for v in shards],
                     _dma_sems(7 * n), start, finish)


def _swap_halves(parts, name):
    n = len(parts)

    def body(*refs):
        v_refs, out_refs = refs[:n], refs[n:2 * n]
        send_sems, recv_sems = refs[2 * n:]
        x, y, c = _position()
        copies = []
        for a in range(n):
            half = parts[a].shape[2] // 2
            copies.append(pltpu.make_async_remote_copy(
                src_ref=v_refs[a].at[:, :, pl.ds((1 - c) * half, half)], dst_ref=out_refs[a],
                send_sem=send_sems.at[a], recv_sem=recv_sems.at[a], device_id=(x, y, 1 - c), device_id_type=MESH))
        for cp in copies:
            cp.start()
        for cp in copies:
            cp.wait()

    return pl.pallas_call(
        body, name=name,
        out_shape=[jax.ShapeDtypeStruct((v.shape[0], v.shape[1], v.shape[2] // 2), v.dtype) for v in parts],
        in_specs=[ANY] * n, out_specs=[ANY] * n,
        scratch_shapes=_dma_sems(n),
    )(*parts)


def _all_to_all_exchange(parts):
    n = len(parts)

    def sends(p_refs, out_refs, sems):
        send_sems, recv_sems = sems
        x, y, c = _position()
        return [pltpu.make_async_remote_copy(
            src_ref=p_refs[a].at[2 * px + py], dst_ref=out_refs[a].at[j],
            send_sem=send_sems.at[3 * a + j], recv_sem=recv_sems.at[3 * a + j],
            device_id=(px, py, c), device_id_type=MESH) for a in range(n) for j, (px, py) in enumerate(_other_chips(x, y))]

    def start(p_refs, out_refs, sems):
        for cp in sends(p_refs, out_refs, sems):
            cp.start()

    def finish(p_refs, out_refs, sems):
        for cp in sends(p_refs, out_refs, sems):
            cp.wait()

    return _Exchange(list(parts), [jax.ShapeDtypeStruct((N_CHIPS - 1, *v.shape[1:]), v.dtype) for v in parts],
                     _dma_sems(3 * n), start, finish)


def _join_halves(wholes):
    n = len(wholes)

    def body(*refs):
        out_refs = refs[n:2 * n]
        send_sems, recv_sems = refs[2 * n:]
        x, y, c = _position()
        copies = []
        for a in range(n):
            half = wholes[a].shape[1] // 2
            mine = out_refs[a].at[:, pl.ds(c * half, half)]
            copies.append(pltpu.make_async_remote_copy(
                src_ref=mine, dst_ref=mine, send_sem=send_sems.at[a], recv_sem=recv_sems.at[a],
                device_id=(x, y, 1 - c), device_id_type=MESH))
        for cp in copies:
            cp.start()
        for cp in copies:
            cp.wait()

    return pl.pallas_call(
        body, name="grad_join_halves",
        out_shape=[jax.ShapeDtypeStruct(v.shape, v.dtype) for v in wholes],
        in_specs=[ANY] * n, out_specs=[ANY] * n,
        input_output_aliases={a: a for a in range(n)},
        scratch_shapes=_dma_sems(n),
    )(*wholes)


STRIP = 256


def _add_halves(g, sw, place, name):
    n, rows, cols = g.shape
    nb = cols // 2 // STRIP

    def kern(p_ref, g_ref, s_ref, o_ref):
        o_ref[...] = (g_ref[...] + s_ref[...]).astype(BF16)

    blk = pl.BlockSpec((1, rows, STRIP), lambda j, i, p_ref: (j, 0, i))
    return pl.pallas_call(
        kern, name=name,
        out_shape=jax.ShapeDtypeStruct((n, rows, cols // 2), BF16),
        grid_spec=pltpu.PrefetchScalarGridSpec(
            num_scalar_prefetch=1, grid=(n, nb),
            in_specs=[pl.BlockSpec((1, rows, STRIP), lambda j, i, p_ref: (j, 0, p_ref[0] * nb + i)), blk],
            out_specs=blk),
        compiler_params=_cparams("parallel", "parallel"),
    )(place, g, sw)


def _sum_chips(own, rx, place, name):
    _, rows, half = rx.shape
    nb = half // STRIP

    def kern(p_ref, own_ref, r_ref, o_ref):
        total = own_ref[0].astype(F32)
        for j in range(N_CHIPS - 1):
            total = total + r_ref[j].astype(F32)
        o_ref[...] = total

    return pl.pallas_call(
        kern, name=name,
        out_shape=jax.ShapeDtypeStruct((rows, 2 * half), F32),
        grid_spec=pltpu.PrefetchScalarGridSpec(
            num_scalar_prefetch=1, grid=(nb,),
            in_specs=[pl.BlockSpec((1, rows, STRIP), lambda i, p_ref: (p_ref[1], 0, i)),
                      pl.BlockSpec((N_CHIPS - 1, rows, STRIP), lambda i, p_ref: (0, 0, i))],
            out_specs=pl.BlockSpec((rows, STRIP), lambda i, p_ref: (0, p_ref[0] * nb + i))),
        compiler_params=_cparams("parallel"),
    )(place, own, rx)


def _gather_small(v, reduce, name):
    rows = v.shape[0]

    def body(v_ref, out_ref, buf, send_sems, recv_sems):
        x, y, c = _position()
        me = 4 * x + 2 * y + c
        buf[me] = v_ref[...]
        peers = [(x ^ (k >> 2), y ^ ((k >> 1) & 1), c ^ (k & 1)) for k in range(1, 8)]
        copies = [pltpu.make_async_remote_copy(
            src_ref=v_ref, dst_ref=buf.at[me],
            send_sem=send_sems.at[k], recv_sem=recv_sems.at[k],
            device_id=peer, device_id_type=MESH) for k, peer in enumerate(peers)]
        for cp in copies:
            cp.start()
        for k, (px, py, pc) in enumerate(peers):
            pltpu.make_async_remote_copy(
                src_ref=v_ref, dst_ref=buf.at[4 * px + 2 * py + pc],
                send_sem=send_sems.at[k], recv_sem=recv_sems.at[k],
                device_id=(px, py, pc), device_id_type=MESH).wait_recv()
        for cp in copies:
            cp.wait_send()
        if reduce:
            total = buf[0]
            for d in range(1, 8):
                total = total + buf[d]
            out_ref[...] = total
        else:
            out_ref[...] = buf[...]

    vm = pl.BlockSpec(memory_space=pltpu.VMEM)
    return pl.pallas_call(
        body, name=name,
        out_shape=jax.ShapeDtypeStruct((rows, LANES) if reduce else (8, rows, LANES), F32),
        in_specs=[vm], out_specs=vm,
        scratch_shapes=[pltpu.VMEM((8, rows, LANES), F32), pltpu.SemaphoreType.DMA((7,)), pltpu.SemaphoreType.DMA((7,))],
    )(v)


def _pad_rows(a, rows):
    return jnp.pad(a, ((0, rows - a.shape[0]), (0, 0)))


def _lane_pad(v):
    n = v.shape[1]
    return jnp.pad(v, ((0, 0), (0, -n % LANES)))


def _gather_all(w_in, w_attn_out, w_ssm_out, w_o, conv_w):
    d = D_MODEL
    w_proj_t = _gather_exchange([w_in[0].T.astype(BF16)])
    out_w = _gather_exchange([a[0].astype(BF16) for a in (w_attn_out, w_ssm_out, w_o)])
    conv_rows = conv_w[0].size // LANES
    conv_all = _gather_small(conv_w[0].reshape(conv_rows, LANES), False, "gather_conv_w")
    conv_w_all = conv_all[0::2].reshape(N_CHIPS, CONV_K, CONV_DIM // N_CHIPS).transpose(1, 0, 2).reshape(CONV_K, CONV_DIM)

    return w_proj_t, out_w, conv_w_all


def _local_step(x, loss_target, norm_w, w_proj_t, conv_w_all, conv_b, dt_bias, a_log, d_skip, ssm_norm_w,
                out_w, final_norm_w, grad_exchange=None):
    b, s, d = x.shape
    t = b * s
    g4, hg = SSD_GROUPS, HEADS_PER_GROUP
    dtb_g = _lane_pad(dt_bias.reshape(g4, hg)).reshape(g4, 1, LANES)
    alog_g = _lane_pad(a_log.reshape(g4, hg)).reshape(g4, 1, LANES)
    dskip_x = jnp.repeat(d_skip, HEAD_DIM, axis=1)
    fnw = final_norm_w.reshape(1, d)

    x2 = x.reshape(t, d)
    if isinstance(w_proj_t, _Exchange):
        h, w_in_t = _rms_fwd(x2, norm_w, exchange=w_proj_t)
        w_proj_t = _to_proj_layout(w_in_t.reshape(D_PROJ, d))
    else:
        h = _rms_fwd(x2, norm_w)
    big_tm = min(t, 2048)
    if isinstance(out_w, _Exchange):
        proj, *out_w = _matmul(h, w_proj_t, tb=True, tm=big_tm, tn=1280, tk=1024, name="proj", exchange=out_w)
    else:
        proj = _matmul(h, w_proj_t, tb=True, tm=big_tm, tn=1280, tk=1024, name="proj")
    w_ao, w_so, w_oo = (w.reshape(-1, d) for w in out_w)
    proj3 = proj.reshape(b, s, NP)
    o3, yp3 = _attn_fwd(proj3)
    xact = _conv_fwd(proj3, conv_w_all, conv_b)
    dtr = proj3[:, :, DT0:DT0 + g4 * hg].reshape(b, s, g4, hg).transpose(0, 2, 1, 3)
    dtr_g = jnp.pad(dtr, ((0, 0), (0, 0), (0, 0), (0, LANES - hg)))
    y3, yn3, hst = _ssd_fwd(xact, proj3, dtr_g, dtb_g, alog_g, dskip_x, ssm_norm_w)
    yp = yp3.reshape(t, D_MODEL)
    yn = yn3.reshape(t, SSD_WIDTH)
    ya = _matmul(yp, w_ao, tm=1024, tn=1024, tk=1024, name="attn_out")
    ys = _matmul(yn, w_so, tm=1024, tn=1024, tk=2048, name="ssm_out")
    merged = _merge_fwd(proj, ya, ys)
    mo = _matmul(merged, w_oo, tm=1024, tn=1024, tk=1024, name="out_proj")
    dout, doutb, loss_part, d_fnw = _final_fwd_bwd(x2, mo, loss_target.reshape(t, d), fnw)

    dmerged = _matmul(doutb, w_oo, tb=True, tm=1024, tn=1024, tk=1024, name="d_merged")
    g_wo = _matmul(merged, doutb, ta=True, tm=1024, tn=1024, tk=1024, name="g_w_o")
    dya, dys, dproj = _merge_bwd(dmerged, proj, ya, ys)
    dyp = _matmul(dya, w_ao, tb=True, tm=1024, tn=1024, tk=1024, name="d_attn_pre")
    g_wao = _matmul(yp, dya, ta=True, tm=1024, tn=1024, tk=1024, name="g_w_attn_out")
    dyn = _matmul(dys, w_so, tb=True, tm=1024, tn=2048, tk=1024, name="d_ssm_norm")
    g_wso = _matmul(yn, dys, ta=True, tm=1024, tn=1024, tk=1024, name="g_w_ssm_out")
    dproj3 = _attn_bwd(proj3, dyp.reshape(b, s, D_MODEL), o3, dproj.reshape(b, s, NP))
    (dxs, dbm, dcm, dproj3, ddtr_g, d_snw_g, d_alog_g, d_dtb_g, d_dsk_g) = _ssd_bwd(
        dyn.reshape(b, s, SSD_WIDTH), y3, xact, proj3, hst, dtr_g, dtb_g, alog_g, dskip_x, ssm_norm_w, dproj3)
    dproj3, g_cw_xs, g_cb_xs = _conv_bwd(dxs, proj3, conv_w_all, conv_b, 0, "conv_bwd_x", dproj3)
    dproj3, g_cw_bm, g_cb_bm = _conv_bwd(dbm, proj3, conv_w_all, conv_b, SSD_WIDTH, "conv_bwd_b", dproj3)
    dproj3, g_cw_cm, g_cb_cm = _conv_bwd(dcm, proj3, conv_w_all, conv_b, SSD_WIDTH + g4 * SSD_STATE, "conv_bwd_c", dproj3)
    ddt = ddtr_g[:, :, :, :hg].transpose(0, 2, 1, 3).reshape(b, s, g4 * hg).astype(BF16)
    ddt = jnp.pad(ddt, ((0, 0), (0, 0), (0, DT_PAD - g4 * hg)))
    dproj = lax.dynamic_update_slice(dproj3, ddt, (0, 0, DT0)).reshape(t, NP)
    exchanged = []
    if grad_exchange:
        g_wproj, *got = _matmul(dproj, h, ta=True, tm=1280, tn=1024, tk=1024, name="g_w_in",
                                exchange=grad_exchange([g_wao, g_wso, g_wo], "out"))
        exchanged += got
        dh, *got = _matmul(dproj, w_proj_t, tm=big_tm, tn=1024, tk=1280, name="d_h", exchange=grad_exchange([g_wproj], "in"))
        exchanged += got
    else:
        g_wproj = _matmul(dproj, h, ta=True, tm=1280, tn=1024, tk=1024, name="g_w_in")
        dh = _matmul(dproj, w_proj_t, tm=big_tm, tn=1024, tk=1280, name="d_h")
    grad_x, d_nw = _rms_bwd(dh, x2, norm_w, dout)
    g_cw = jnp.concatenate([g_cw_xs, g_cw_bm, g_cw_cm], axis=1)
    g_cb = jnp.concatenate([g_cb_xs, g_cb_bm, g_cb_cm], axis=1)
    return (loss_part, grad_x, d_nw, g_wproj, g_cw, g_cb, d_dtb_g, d_alog_g, d_dsk_g, d_snw_g, g_wao, g_wso, g_wo, d_fnw,
            exchanged)


def kernel(x, norm_w, w_in, conv_w, conv_b, dt_bias, a_log, d_skip, ssm_norm_w, w_attn_out, w_ssm_out, w_o, final_norm_w, loss_target, m_norm_w, m_w_in, m_conv_w, m_conv_b, m_dt_bias, m_a_log, m_d_skip, m_ssm_norm_w, m_w_attn_out, m_w_ssm_out, m_w_o, m_final_norm_w, v_norm_w, v_w_in, v_conv_w, v_conv_b, v_dt_bias, v_a_log, v_d_skip, v_ssm_norm_w, v_w_attn_out, v_w_ssm_out, v_w_o, v_final_norm_w):
    b, s, d = x.shape
    core = lax.axis_index("c")
    g4, hg = SSD_GROUPS, HEADS_PER_GROUP
    shard_cols = w_in.shape[2]
    w_proj_t, out_w, conv_w_all = _gather_all(w_in, w_attn_out, w_ssm_out, w_o, conv_w)
    chip = 2 * lax.axis_index("x") + lax.axis_index("y")
    place = jnp.stack([core, chip]).astype(jnp.int32)
    chip_sums = []

    def grad_exchange(grads, which):
        if which == "in":
            slabs = _from_proj_layout(grads[0]).reshape(N_CHIPS, shard_cols, d)
        else:
            slabs = jnp.concatenate([g.reshape(N_CHIPS, -1, d) for g in grads], axis=1)
        from_sibling, = _swap_halves([slabs], "grad_swap_halves_" + which)
        chip_sums.append(_add_halves(slabs, from_sibling, place, "grad_add_halves_" + which))
        return _all_to_all_exchange(chip_sums[-1:])

    (loss_part, grad_x, d_nw, _, g_cw, g_cb, d_dtb_g, d_alog_g, d_dsk_g, d_snw_g, _, _, _, d_fnw, from_chips) = _local_step(
        x, loss_target, norm_w, w_proj_t, conv_w_all, conv_b, dt_bias, a_log, d_skip, ssm_norm_w, out_w, final_norm_w,
        grad_exchange)
    wholes = [_sum_chips(o, r, place, "grad_sum_chips_%d" % i) for i, (o, r) in enumerate(zip(chip_sums, from_chips))]
    g_out, g_w_in = _join_halves(wholes)

    small = jnp.concatenate([
        loss_part, d_nw, g_cb, _lane_pad(d_dtb_g[:, 0, :hg].reshape(1, -1)), _lane_pad(d_alog_g[:, 0, :hg].reshape(1, -1)),
        _lane_pad(d_dsk_g[:, 0, :hg].reshape(1, -1)),
        d_snw_g.reshape(1, -1), d_fnw, g_cw.reshape(1, -1)], axis=1)
    small_rows = small.shape[1] // LANES
    reduced = _gather_small(_pad_rows(small.reshape(small_rows, LANES), -(-small_rows // 8) * 8), True, "reduce_small")
    flat = reduced.reshape(-1)

    def take(start, n):
        return flat[start:start + n].reshape(1, n)

    loss = flat[0]
    pos = LANES
    g_norm_w = take(pos, d); pos += d
    g_conv_b = take(pos, CONV_DIM); pos += CONV_DIM
    g_dt_bias = take(pos, g4 * hg); pos += LANES
    g_a_log = take(pos, g4 * hg); pos += LANES
    g_d_skip = take(pos, g4 * hg); pos += LANES
    g_ssm_norm_w = take(pos, SSD_WIDTH); pos += SSD_WIDTH
    g_final_norm_w = take(pos, d); pos += d
    conv_cols = CONV_DIM // N_CHIPS
    g_conv_w = lax.dynamic_slice_in_dim(flat[pos:pos + CONV_K * CONV_DIM].reshape(CONV_K, CONV_DIM), chip * conv_cols, conv_cols, axis=1)

    rows_ao, rows_so = D_MODEL // N_CHIPS, SSD_WIDTH // N_CHIPS
    g_w_attn_out = g_out[:rows_ao]
    g_w_ssm_out = g_out[rows_ao:rows_ao + rows_so]
    g_w_o = g_out[rows_ao + rows_so:]

    names = ["norm_w", "w_in", "conv_w", "conv_b", "dt_bias", "a_log", "d_skip", "ssm_norm_w",
             "w_attn_out", "w_ssm_out", "w_o", "final_norm_w"]
    weights = [norm_w, w_in, conv_w, conv_b, dt_bias, a_log, d_skip, ssm_norm_w, w_attn_out, w_ssm_out, w_o, final_norm_w]
    grads = [g_norm_w, g_w_in, g_conv_w, g_conv_b, g_dt_bias, g_a_log, g_d_skip, g_ssm_norm_w,
             g_w_attn_out, g_w_ssm_out, g_w_o, g_final_norm_w]
    ms = [m_norm_w, m_w_in, m_conv_w, m_conv_b, m_dt_bias, m_a_log, m_d_skip, m_ssm_norm_w,
          m_w_attn_out, m_w_ssm_out, m_w_o, m_final_norm_w]
    vs = [v_norm_w, v_w_in, v_conv_w, v_conv_b, v_dt_bias, v_a_log, v_d_skip, v_ssm_norm_w,
          v_w_attn_out, v_w_ssm_out, v_w_o, v_final_norm_w]
    out_g, out_d, out_m, out_v = [], [], [], []
    for name, w, g, m, v in zip(names, weights, grads, ms, vs):
        if name == "w_in":
            to2, back = (lambda a: a[0].T), (lambda a: a.T.reshape(w.shape))
        else:
            to2, back = (lambda a: a.reshape(g.shape)), (lambda a: a.reshape(w.shape))
        dlt, nm, nv = _adamw(to2(w), g, to2(m), to2(v), "adamw_" + name)
        out_g.append(back(g))
        out_d.append(back(dlt))
        out_m.append(back(nm))
        out_v.append(back(nv))

    return (loss, grad_x.reshape(b, s, d), *out_g, *out_d, *out_m, *out_v)
```
